```python
import math
import jax, jax.numpy as jnp
from jax import lax
import numpy as np

D_MODEL = 1024
BATCH = 4
SEQ = 4096
DEPTH = 1

D_MIX = D_MODEL
D_CONV = D_MIX // 2
D_ATTN = D_MIX - D_CONV
CONV_WIDTH = 3
CONV_GROUPS = 8
HEAD_DIM = 64
V_DIM = 2 * HEAD_DIM
N_HEADS = D_ATTN // V_DIM
Q_WIDTH = N_HEADS * 2 * HEAD_DIM
D_IN = 3 * D_CONV + 2 * Q_WIDTH + N_HEADS * V_DIM
Q_BLOCK = 128
N_EXPERT_GROUPS = 4
EXPERTS_PER_GROUP = 8
TOP_K = 2
D_EXPERT = D_MODEL // 2
EPS = 1e-6

kernel_name = "hymba_conv_diffattn_hmoe_layer"


def rms_norm(x, g):
    xf = x.astype(jnp.float32)
    y = xf * lax.rsqrt(jnp.mean(xf * xf, axis=-1, keepdims=True) + EPS)
    return (y * g.astype(jnp.float32)).astype(x.dtype)


def lambda_init(layer_idx):
    return 0.8 - 0.6 * math.exp(-0.3 * layer_idx)


def causal_depthwise_conv(u, w):
    K = w.shape[0]
    S = u.shape[1]
    up = jnp.pad(u, ((0, 0), (K - 1, 0), (0, 0)))
    y = up[:, 0:S, :] * w[0]
    for k in range(1, K):
        y = y + up[:, k:k + S, :] * w[k]
    return y


def short_conv_group(xc, b_gate, c_gate, conv_w, conv_out_g):
    y = b_gate * causal_depthwise_conv(c_gate * xc, conv_w)
    Bsz, S, _ = y.shape
    yg = y.reshape(Bsz, S, CONV_GROUPS, D_CONV // CONV_GROUPS)
    yg = rms_norm(yg, jnp.ones((D_CONV // CONV_GROUPS,), y.dtype))
    return yg.reshape(Bsz, S, D_CONV) * conv_out_g.astype(y.dtype)


def diff_attention_group(q, k, v, q_norm_g, k_norm_g, lq1, lk1, lq2, lk2, subln_g, lam_init):
    Bsz, S, _ = q.shape
    q = rms_norm(q.reshape(Bsz, S, N_HEADS, 2, HEAD_DIM), q_norm_g) * (HEAD_DIM ** -0.5)
    k = rms_norm(k.reshape(Bsz, S, N_HEADS, 2, HEAD_DIM), k_norm_g)
    v = v.reshape(Bsz, S, N_HEADS, V_DIM)
    qT = q.transpose(0, 2, 3, 1, 4)
    kT = k.transpose(0, 2, 3, 1, 4)
    vT = v.transpose(0, 2, 1, 3)
    lam = (jnp.exp(jnp.sum(lq1.astype(jnp.float32) * lk1.astype(jnp.float32)))
           - jnp.exp(jnp.sum(lq2.astype(jnp.float32) * lk2.astype(jnp.float32)))
           + lam_init)
    key_pos = jnp.arange(S)
    n_blocks = S // Q_BLOCK

    def one_block(i):
        start = i * Q_BLOCK
        qb = lax.dynamic_slice_in_dim(qT, start, Q_BLOCK, axis=3)
        s = jnp.einsum('bhcqd,bhckd->bhcqk', qb, kT).astype(jnp.float32)
        q_pos = start + jnp.arange(Q_BLOCK)
        mask = key_pos[None, :] <= q_pos[:, None]
        s = jnp.where(mask, s, -jnp.inf)
        p = jax.nn.softmax(s, axis=-1)
        a = (p[:, :, 0] - lam * p[:, :, 1]).astype(vT.dtype)
        return jnp.einsum('bhqk,bhkv->bqhv', a, vT)

    out = lax.map(one_block, jnp.arange(n_blocks))
    out = out.transpose(1, 0, 2, 3, 4).reshape(Bsz, S, N_HEADS, V_DIM)
    out = rms_norm(out, subln_g) * (1.0 - lam_init)
    return out.reshape(Bsz, S, D_ATTN)


def hierarchical_moe(h, w_router_group, w_router_expert, w_gate, w_up, w_down):
    T = h.shape[0]
    G, E = N_EXPERT_GROUPS, EXPERTS_PER_GROUP
    p_group = jax.nn.softmax(jnp.einsum('td,dg->tg', h, w_router_group).astype(jnp.float32), axis=-1)
    g_idx = jnp.argmax(p_group, axis=-1)
    g_gate = jnp.max(p_group, axis=-1)
    logits_e = jnp.einsum('td,de->te', h, w_router_expert).astype(jnp.float32).reshape(T, G, E)
    logits_sel = jnp.take_along_axis(logits_e, g_idx[:, None, None], axis=1)[:, 0]
    top_vals, top_idx = lax.top_k(logits_sel, TOP_K)
    top_w = jax.nn.softmax(top_vals, axis=-1)
    exp_w = jnp.sum(jax.nn.one_hot(top_idx, E, dtype=jnp.float32) * top_w[..., None], axis=1)
    comb = (jax.nn.one_hot(g_idx, G, dtype=jnp.float32)[:, :, None]
            * exp_w[:, None, :] * g_gate[:, None, None]).astype(h.dtype)
    y = jnp.zeros_like(h)
    for g in range(G):
        hg = jnp.einsum('td,edf->tef', h, w_gate[g])
        hu = jnp.einsum('td,edf->tef', h, w_up[g])
        act = jax.nn.silu(hg) * hu * comb[:, g, :, None]
        y = y + jnp.einsum('tef,efd->td', act, w_down[g])
    return y


def setup_inputs(seed: int = 0) -> dict:
    key = jax.random.key(seed)
    ks = jax.random.split(key, 20)
    f32 = jnp.float32
    G, E, F = N_EXPERT_GROUPS, EXPERTS_PER_GROUP, D_EXPERT

    def nrm(k, shape, scale):
        return jax.random.normal(k, shape, f32) * scale

    return {
        "x": nrm(ks[0], (BATCH, SEQ, D_MODEL), 1.0),
        "attn_norm_g": 1.0 + nrm(ks[1], (DEPTH, D_MODEL), 0.02),
        "w_in": nrm(ks[2], (DEPTH, D_MODEL, D_IN), D_MODEL ** -0.5),
        "conv_w": nrm(ks[3], (DEPTH, CONV_WIDTH, D_CONV), CONV_WIDTH ** -0.5),
        "conv_out_g": 1.0 + nrm(ks[4], (DEPTH, D_CONV), 0.02),
        "q_norm_g": 1.0 + nrm(ks[5], (DEPTH, HEAD_DIM), 0.02),
        "k_norm_g": 1.0 + nrm(ks[6], (DEPTH, HEAD_DIM), 0.02),
        "lambda_q1": nrm(ks[7], (DEPTH, HEAD_DIM), 0.1),
        "lambda_k1": nrm(ks[8], (DEPTH, HEAD_DIM), 0.1),
        "lambda_q2": nrm(ks[9], (DEPTH, HEAD_DIM), 0.1),
        "lambda_k2": nrm(ks[10], (DEPTH, HEAD_DIM), 0.1),
        "attn_subln_g": 1.0 + nrm(ks[11], (DEPTH, V_DIM), 0.02),
        "w_out": nrm(ks[12], (DEPTH, D_MIX, D_MODEL), D_MIX ** -0.5),
        "ffn_norm_g": 1.0 + nrm(ks[13], (DEPTH, D_MODEL), 0.02),
        "w_router_group": nrm(ks[14], (DEPTH, D_MODEL, G), D_MODEL ** -0.5),
        "w_router_expert": nrm(ks[15], (DEPTH, D_MODEL, G * E), D_MODEL ** -0.5),
        "w_exp_gate": nrm(ks[16], (DEPTH, G, E, D_MODEL, F), D_MODEL ** -0.5),
        "w_exp_up": nrm(ks[17], (DEPTH, G, E, D_MODEL, F), D_MODEL ** -0.5),
        "w_exp_down": nrm(ks[18], (DEPTH, G, E, F, D_MODEL), F ** -0.5),
    }


def reference(x, attn_norm_g, w_in, conv_w, conv_out_g, q_norm_g, k_norm_g,
              lambda_q1, lambda_k1, lambda_q2, lambda_k2, attn_subln_g, w_out,
              ffn_norm_g, w_router_group, w_router_expert, w_exp_gate, w_exp_up, w_exp_down):
    Bsz, S, D = x.shape
    split_points = [D_CONV, 2 * D_CONV, 3 * D_CONV,
                    3 * D_CONV + Q_WIDTH, 3 * D_CONV + 2 * Q_WIDTH]
    h = x
    for l in range(DEPTH):
        lam_init = lambda_init(l)
        hn = rms_norm(h, attn_norm_g[l])
        proj = jnp.einsum('bsd,de->bse', hn, w_in[l])
        xc, b_gate, c_gate, q, k, v = jnp.split(proj, split_points, axis=-1)
        y_conv = short_conv_group(xc, b_gate, c_gate, conv_w[l], conv_out_g[l])
        y_attn = diff_attention_group(q, k, v, q_norm_g[l], k_norm_g[l],
                                      lambda_q1[l], lambda_k1[l], lambda_q2[l], lambda_k2[l],
                                      attn_subln_g[l], lam_init)
        mix = jnp.concatenate([y_conv, y_attn], axis=-1)
        h = h + jnp.einsum('bse,ed->bsd', mix, w_out[l])
        hn2 = rms_norm(h, ffn_norm_g[l]).reshape(Bsz * S, D)
        y_ffn = hierarchical_moe(hn2, w_router_group[l], w_router_expert[l],
                                 w_exp_gate[l], w_exp_up[l], w_exp_down[l])
        h = h + y_ffn.reshape(Bsz, S, D)
    return h
```

```python
import functools
import math

import jax
import jax.numpy as jnp
from jax import lax
from jax.experimental import pallas as pl
from jax.experimental.pallas import tpu as pltpu

F32 = jnp.float32
BF16 = jnp.bfloat16

HEAD_DIM = 64
V_DIM = 2 * HEAD_DIM
CONV_GROUPS = 8
N_GROUPS = 4
EXPERTS_PER_GROUP = 8
N_EXPERTS = N_GROUPS * EXPERTS_PER_GROUP
EPS = 1e-6
LANES = 128
VMEM_LIMIT = 48 * 1024 * 1024

TM_PROJ = 256
TQ = 512
TK = 512
TM_MOE = 256


def _dot(a, b):
    return jnp.dot(a, b, preferred_element_type=F32)


def _split_dot(a_f32, b_bf16):
    hi = a_f32.astype(BF16)
    lo = (a_f32 - hi.astype(F32)).astype(BF16)
    return _dot(hi, b_bf16) + _dot(lo, b_bf16)


def _inproj_kernel(x_ref, g_ref, w_ref, cw_ref, cg_ref, qg_ref, kg_ref, gm_ref,
                   yc_ref, q_ref, k_ref, v_ref, carry_ref, *, tm, dc):
    j = pl.program_id(1)

    @pl.when(j == 0)
    def _():
        carry_ref[...] = jnp.zeros_like(carry_ref)

    x = x_ref[...]
    ms = jnp.mean(x * x, axis=-1, keepdims=True)
    hn = (x * lax.rsqrt(ms + EPS) * g_ref[...]).astype(BF16)

    def proj(s):
        return _dot(hn, w_ref[:, s * dc:(s + 1) * dc])

    gm = gm_ref[...]

    u = proj(2) * proj(0)
    prev = carry_ref[...]
    rows = lax.broadcasted_iota(jnp.int32, u.shape, 0)
    u1 = jnp.where(rows == 0, prev[7:8, :], pltpu.roll(u, 1, 0))
    u2 = jnp.where(rows == 0, prev[6:7, :], jnp.where(rows == 1, prev[7:8, :], pltpu.roll(u, 2, 0)))
    carry_ref[...] = u[tm - 8:tm, :]
    cw = cw_ref[...]
    y = proj(1) * (cw[0:1, :] * u2 + cw[1:2, :] * u1 + cw[2:3, :] * u)
    yc_ref[...] = (y * lax.rsqrt(_split_dot(y * y, gm) + EPS) * cg_ref[...]).astype(BF16)

    q = proj(3)
    q_ref[...] = (q * lax.rsqrt(_split_dot(q * q, gm) + EPS) * qg_ref[...]).astype(BF16)
    k = proj(4)
    k_ref[...] = (k * lax.rsqrt(_split_dot(k * k, gm) + EPS) * kg_ref[...]).astype(BF16)
    v_ref[...] = proj(5).astype(BF16)


def _inproj(x2, g, w_in, conv_w, conv_g, qg, kg, gmat, *, batch, seq):
    t, d = x2.shape
    dc = gmat.shape[0]
    tm = TM_PROJ
    nj = seq // tm
    row = lambda b, j: (b * nj + j, 0)
    const = lambda b, j: (0, 0)
    out_sds = jax.ShapeDtypeStruct((t, dc), BF16)
    return pl.pallas_call(
        functools.partial(_inproj_kernel, tm=tm, dc=dc),
        grid=(batch, nj),
        in_specs=[
            pl.BlockSpec((tm, d), row),
            pl.BlockSpec((1, d), const),
            pl.BlockSpec(w_in.shape, const),
            pl.BlockSpec(conv_w.shape, const),
            pl.BlockSpec((1, dc), const),
            pl.BlockSpec((1, dc), const),
            pl.BlockSpec((1, dc), const),
            pl.BlockSpec(gmat.shape, const),
        ],
        out_specs=[pl.BlockSpec((tm, dc), row)] * 4,
        out_shape=[out_sds] * 4,
        scratch_shapes=[pltpu.VMEM((8, dc), F32)],
        compiler_params=pltpu.CompilerParams(
            dimension_semantics=("arbitrary", "arbitrary"), vmem_limit_bytes=VMEM_LIMIT),
        name="inproj_conv_qknorm",
    )(x2, g, w_in, conv_w, conv_g, qg, kg, gmat)


def _attn_kernel(lp_ref, sg_ref, q_ref, k_ref, v_ref, o_ref, m_ref, l_ref, acc_ref,
                 *, tq, tk, lam_init):
    qi = pl.program_id(2)
    q = q_ref[...]
    lane = lax.broadcasted_iota(jnp.int32, q.shape, 1)
    zero = jnp.zeros_like(q)
    qq = jnp.concatenate([jnp.where(lane < HEAD_DIM, q, zero),
                          jnp.where(lane >= HEAD_DIM, q, zero)], axis=0)

    m_ref[...] = jnp.full_like(m_ref, -jnp.inf)
    l_ref[...] = jnp.zeros_like(l_ref)
    acc_ref[...] = jnp.zeros_like(acc_ref)

    def step(j, masked):
        start = pl.multiple_of(j * tk, tk)
        kb = k_ref[pl.ds(start, tk), :]
        vb = v_ref[pl.ds(start, tk), :]
        s = lax.dot_general(qq, kb, (((1,), (1,)), ((), ())), preferred_element_type=F32)
        if masked:
            r = lax.broadcasted_iota(jnp.int32, s.shape, 0)
            qpos = qi * tq + jnp.where(r >= tq, r - tq, r)
            kpos = start + lax.broadcasted_iota(jnp.int32, s.shape, 1)
            s = jnp.where(kpos <= qpos, s, -jnp.inf)
        m_old = m_ref[...]
        m_new = jnp.maximum(m_old, jnp.max(s, axis=-1, keepdims=True))
        alpha = jnp.exp(m_old - m_new)
        p = jnp.exp(s - m_new)
        l_ref[...] = alpha * l_ref[...] + jnp.sum(p, axis=-1, keepdims=True)
        acc_ref[...] = alpha * acc_ref[...] + _dot(p.astype(BF16), vb)
        m_ref[...] = m_new

    def body(j, c):
        step(j, False)
        return c

    lax.fori_loop(0, qi, body, 0)
    step(qi, True)

    lp = lp_ref[...]
    lam = (jnp.exp(jnp.sum(lp[0:1, :] * lp[1:2, :], axis=-1, keepdims=True))
           - jnp.exp(jnp.sum(lp[2:3, :] * lp[3:4, :], axis=-1, keepdims=True)) + lam_init)
    o = acc_ref[...] / l_ref[...]
    d = o[0:tq, :] - lam * o[tq:2 * tq, :]
    ms = jnp.mean(d * d, axis=-1, keepdims=True)
    o_ref[...] = (d * lax.rsqrt(ms + EPS) * sg_ref[...] * (1.0 - lam_init)).astype(BF16)


def _attention(lam_params, subln_g, q, k, v, *, batch, seq, lam_init):
    t, dq = q.shape
    n_heads = dq // V_DIM
    tq, tk = TQ, TK
    assert tq == tk
    nq = seq // tq
    const = lambda b, h, i: (0, 0)
    return pl.pallas_call(
        functools.partial(_attn_kernel, tq=tq, tk=tk, lam_init=lam_init),
        grid=(batch, n_heads, nq),
        in_specs=[
            pl.BlockSpec(lam_params.shape, const),
            pl.BlockSpec((1, V_DIM), const),
            pl.BlockSpec((tq, V_DIM), lambda b, h, i: (b * nq + i, h)),
            pl.BlockSpec((seq, V_DIM), lambda b, h, i: (b, h)),
            pl.BlockSpec((seq, V_DIM), lambda b, h, i: (b, h)),
        ],
        out_specs=pl.BlockSpec((tq, V_DIM), lambda b, h, i: (b * nq + i, h)),
        out_shape=jax.ShapeDtypeStruct((t, dq), BF16),
        scratch_shapes=[pltpu.VMEM((2 * tq, 1), F32), pltpu.VMEM((2 * tq, 1), F32),
                        pltpu.VMEM((2 * tq, V_DIM), F32)],
        compiler_params=pltpu.CompilerParams(
            dimension_semantics=("arbitrary", "arbitrary", "arbitrary"), vmem_limit_bytes=VMEM_LIMIT),
        name="diff_attention",
    )(lam_params, subln_g, q, k, v)


def _outproj_router_kernel(x_ref, yc_ref, at_ref, wo_ref, g_ref, wrh_ref, wrl_ref,
                           h_ref, hn_ref, route_ref, cnt_ref, *, tm, dc):
    i = pl.program_id(0)

    @pl.when(i == 0)
    def _():
        cnt_ref[...] = jnp.zeros_like(cnt_ref)

    h = x_ref[...] + _dot(yc_ref[...], wo_ref[0:dc, :]) + _dot(at_ref[...], wo_ref[dc:2 * dc, :])
    h_ref[...] = h
    ms = jnp.mean(h * h, axis=-1, keepdims=True)
    hn = h * lax.rsqrt(ms + EPS) * g_ref[...]
    hi = hn.astype(BF16)
    hn_ref[...] = hi
    lo = (hn - hi.astype(F32)).astype(BF16)
    logits = _dot(hi, wrh_ref[...]) + _dot(lo, wrh_ref[...]) + _dot(hi, wrl_ref[...])

    lane = lax.broadcasted_iota(jnp.int32, logits.shape, 1)
    lane_f = lane.astype(F32)
    neg = -jnp.inf
    is_g = (lane >= N_EXPERTS) & (lane < N_EXPERTS + N_GROUPS)
    gl = jnp.where(is_g, logits, neg)
    gmax = jnp.max(gl, axis=-1, keepdims=True)
    g_gate = 1.0 / jnp.sum(jnp.exp(gl - gmax), axis=-1, keepdims=True)
    g_idx = jnp.min(jnp.where(gl == gmax, lane_f - N_EXPERTS, 1e9), axis=-1, keepdims=True)
    e_lo = g_idx * EXPERTS_PER_GROUP
    in_grp = (lane_f >= e_lo) & (lane_f < e_lo + EXPERTS_PER_GROUP)
    el = jnp.where(in_grp, logits, neg)
    v1 = jnp.max(el, axis=-1, keepdims=True)
    i1 = jnp.min(jnp.where(el == v1, lane_f, 1e9), axis=-1, keepdims=True)
    el2 = jnp.where(lane_f == i1, neg, el)
    v2 = jnp.max(el2, axis=-1, keepdims=True)
    i2 = jnp.min(jnp.where(el2 == v2, lane_f, 1e9), axis=-1, keepdims=True)
    tt = jnp.exp(v2 - v1)
    w1 = g_gate / (1.0 + tt)
    w2 = g_gate * tt / (1.0 + tt)

    sel1 = lane_f == i1
    sel2 = lane_f == i2
    oh = jnp.where(sel1 | sel2, 1.0, 0.0)
    rr = lax.broadcasted_iota(jnp.int32, (tm, tm), 0)
    cc = lax.broadcasted_iota(jnp.int32, (tm, tm), 1)
    lower = jnp.where(cc < rr, 1.0, 0.0).astype(BF16)
    ranks = _dot(lower, oh.astype(BF16)) + cnt_ref[0:1, :]
    r1 = jnp.sum(jnp.where(sel1, ranks, 0.0), axis=-1, keepdims=True)
    r2 = jnp.sum(jnp.where(sel2, ranks, 0.0), axis=-1, keepdims=True)
    cnt_ref[...] = cnt_ref[...] + jnp.sum(oh, axis=0, keepdims=True)

    route_ref[...] = jnp.where(
        lane == 0, i1, jnp.where(lane == 1, i2, jnp.where(lane == 2, r1, jnp.where(
            lane == 3, r2, jnp.where(lane == 4, w1, jnp.where(lane == 5, w2, 0.0))))))


def _outproj_router(x2, yc, at, w_out, g, wr_hi, wr_lo):
    t, d = x2.shape
    dc = yc.shape[1]
    tm = TM_PROJ
    row = lambda i: (i, 0)
    const = lambda i: (0, 0)
    return pl.pallas_call(
        functools.partial(_outproj_router_kernel, tm=tm, dc=dc),
        grid=(t // tm,),
        in_specs=[
            pl.BlockSpec((tm, d), row),
            pl.BlockSpec((tm, dc), row),
            pl.BlockSpec((tm, dc), row),
            pl.BlockSpec(w_out.shape, const),
            pl.BlockSpec((1, d), const),
            pl.BlockSpec(wr_hi.shape, const),
            pl.BlockSpec(wr_lo.shape, const),
        ],
        out_specs=[
            pl.BlockSpec((tm, d), row),
            pl.BlockSpec((tm, d), row),
            pl.BlockSpec((tm, LANES), row),
            pl.BlockSpec((8, LANES), const),
        ],
        out_shape=[
            jax.ShapeDtypeStruct((t, d), F32),
            jax.ShapeDtypeStruct((t, d), BF16),
            jax.ShapeDtypeStruct((t, LANES), F32),
            jax.ShapeDtypeStruct((8, LANES), F32),
        ],
        compiler_params=pltpu.CompilerParams(
            dimension_semantics=("arbitrary",), vmem_limit_bytes=VMEM_LIMIT),
        name="outproj_router",
    )(x2, yc, at, w_out, g, wr_hi, wr_lo)


def _moe_kernel(te_ref, nt_ref, x_ref, rw_ref, wg_ref, wu_ref, wd_ref, y_ref):
    i = pl.program_id(0)

    @pl.when(i < nt_ref[0])
    def _():
        x = x_ref[...]
        hg = _dot(x, wg_ref[0])
        hu = _dot(x, wu_ref[0])
        act = hg * (1.0 / (1.0 + jnp.exp(-hg))) * hu * rw_ref[...]
        y_ref[...] = _dot(act.astype(BF16), wd_ref[0])


def _moe(tile_expert, n_tiles, xs, rw, wg, wu, wd):
    p, d = xs.shape
    f = wg.shape[2]
    tm = TM_MOE
    row = lambda i, te, nt: (jnp.minimum(i, nt[0] - 1), 0)
    wsel = lambda i, te, nt: (te[jnp.minimum(i, nt[0] - 1)], 0, 0)
    grid_spec = pltpu.PrefetchScalarGridSpec(
        num_scalar_prefetch=2,
        grid=(p // tm,),
        in_specs=[
            pl.BlockSpec((tm, d), row),
            pl.BlockSpec((tm, 1), row),
            pl.BlockSpec((1, d, f), wsel),
            pl.BlockSpec((1, d, f), wsel),
            pl.BlockSpec((1, f, d), wsel),
        ],
        out_specs=pl.BlockSpec((tm, d), row),
    )
    return pl.pallas_call(
        _moe_kernel,
        grid_spec=grid_spec,
        out_shape=jax.ShapeDtypeStruct((p, d), F32),
        compiler_params=pltpu.CompilerParams(
            dimension_semantics=("arbitrary",), vmem_limit_bytes=VMEM_LIMIT),
        name="moe_experts",
    )(tile_expert, n_tiles, xs, rw, wg, wu, wd)


def _lambda_init(layer_idx):
    return 0.8 - 0.6 * math.exp(-0.3 * layer_idx)


def _layer(h, l, attn_norm_g, w_in, conv_w, conv_out_g, q_norm_g, k_norm_g,
           lambda_q1, lambda_k1, lambda_q2, lambda_k2, attn_subln_g, w_out,
           ffn_norm_g, w_router_group, w_router_expert, w_exp_gate, w_exp_up, w_exp_down):
    batch, seq, d = h.shape
    t = batch * seq
    dc = conv_w.shape[-1]
    lam_init = _lambda_init(l)
    x2 = h.reshape(t, d)

    reps = dc // HEAD_DIM
    assert dc // CONV_GROUPS == HEAD_DIM
    qg =(jnp.tile(q_norm_g[l], reps) * (HEAD_DIM ** -0.5)).reshape(1, dc)
    kg = jnp.tile(k_norm_g[l], reps).reshape(1, dc)
    grp = jnp.arange(dc) // HEAD_DIM
    gmat = jnp.where(grp[:, None] == grp[None, :], 1.0 / HEAD_DIM, 0.0).astype(BF16)
    yc, q, k, v = _inproj(x2, attn_norm_g[l].reshape(1, d), w_in[l].astype(BF16), conv_w[l],
                          conv_out_g[l].reshape(1, dc), qg, kg, gmat, batch=batch, seq=seq)

    lam_params = jnp.stack([lambda_q1[l], lambda_k1[l], lambda_q2[l], lambda_k2[l]])
    at = _attention(lam_params, attn_subln_g[l].reshape(1, V_DIM), q, k, v,
                    batch=batch, seq=seq, lam_init=lam_init)

    wr = jnp.concatenate([w_router_expert[l], w_router_group[l],
                          jnp.zeros((d, LANES - N_EXPERTS - N_GROUPS), F32)], axis=1)
    wr_hi = wr.astype(BF16)
    wr_lo = (wr - wr_hi.astype(F32)).astype(BF16)
    hres, hn2, route, cnt = _outproj_router(x2, yc, at, w_out[l].astype(BF16),
                                            ffn_norm_g[l].reshape(1, d), wr_hi, wr_lo)

    tmm = TM_MOE
    n_tiles_max = (2 * t) // tmm + N_EXPERTS
    p_rows = n_tiles_max * tmm
    e1 = route[:, 0].astype(jnp.int32)
    e2 = route[:, 1].astype(jnp.int32)
    counts = cnt[0, :N_EXPERTS].astype(jnp.int32)
    tiles = (counts + tmm - 1) // tmm
    tile_end = jnp.cumsum(tiles)
    offs = (tile_end - tiles) * tmm
    pos1 = offs[e1] + route[:, 2].astype(jnp.int32)
    pos2 = offs[e2] + route[:, 3].astype(jnp.int32)
    n_tiles = tile_end[-1:].astype(jnp.int32)
    tile_ids = jnp.arange(n_tiles_max, dtype=jnp.int32)
    tile_expert = jnp.minimum(
        jnp.sum((tile_end[None, :] <= tile_ids[:, None]).astype(jnp.int32), axis=1), N_EXPERTS - 1)

    tok = jnp.arange(t, dtype=jnp.int32)
    src = jnp.full((p_rows,), t, jnp.int32).at[pos1].set(tok).at[pos2].set(tok)
    xs = jnp.take(hn2, src, axis=0, mode="fill", fill_value=0)
    rw = jnp.zeros((p_rows,), F32).at[pos1].set(route[:, 4]).at[pos2].set(route[:, 5]).reshape(p_rows, 1)

    f = w_exp_gate.shape[-1]
    ys = _moe(tile_expert, n_tiles, xs, rw,
              w_exp_gate[l].reshape(N_EXPERTS, d, f).astype(BF16),
              w_exp_up[l].reshape(N_EXPERTS, d, f).astype(BF16),
              w_exp_down[l].reshape(N_EXPERTS, f, d).astype(BF16))
    out = hres + jnp.take(ys, pos1, axis=0) + jnp.take(ys, pos2, axis=0)
    return out.reshape(batch, seq, d)


def kernel(x, attn_norm_g, w_in, conv_w, conv_out_g, q_norm_g, k_norm_g, lambda_q1, lambda_k1,
           lambda_q2, lambda_k2, attn_subln_g, w_out, ffn_norm_g, w_router_group, w_router_expert,
           w_exp_gate, w_exp_up, w_exp_down):
    h = x
    for l in range(attn_norm_g.shape[0]):
        h = _layer(h, l, attn_norm_g, w_in, conv_w, conv_out_g, q_norm_g, k_norm_g,
                   lambda_q1, lambda_k1, lambda_q2, lambda_k2, attn_subln_g, w_out,
                   ffn_norm_g, w_router_group, w_router_expert, w_exp_gate, w_exp_up, w_exp_down)
    return h
```

```python
import functools
import math

import jax
import jax.numpy as jnp
from jax import lax
from jax.experimental import pallas as pl
from jax.experimental.pallas import tpu as pltpu

F32 = jnp.float32
BF16 = jnp.bfloat16

HEAD_DIM = 64
V_DIM = 2 * HEAD_DIM
CONV_GROUPS = 8
N_GROUPS = 4
EXPERTS_PER_GROUP = 8
N_EXPERTS = N_GROUPS * EXPERTS_PER_GROUP
EPS = 1e-6
LANES = 128
VMEM_LIMIT = 48 * 1024 * 1024

TM_PROJ = 256
TQ = 512
TK = 512
TM_MOE = 256


def _dot(a, b):
    return jnp.dot(a, b, preferred_element_type=F32)


def _split_dot(a_f32, b_bf16):
    hi = a_f32.astype(BF16)
    lo = (a_f32 - hi.astype(F32)).astype(BF16)
    return _dot(hi, b_bf16) + _dot(lo, b_bf16)


def _inproj_kernel(x_ref, g_ref, w_ref, cw_ref, cg_ref, qg_ref, kg_ref, gm_ref,
                   yc_ref, qT_ref, k_ref, vT_ref, carry_ref, *, tm, dc):
    j = pl.program_id(1)

    @pl.when(j == 0)
    def _():
        carry_ref[...] = jnp.zeros_like(carry_ref)

    x = x_ref[...]
    ms = jnp.mean(x * x, axis=-1, keepdims=True)
    hn = (x * lax.rsqrt(ms + EPS) * g_ref[...]).astype(BF16)

    def proj(s):
        return _dot(hn, w_ref[:, s * dc:(s + 1) * dc])

    gm = gm_ref[...]

    u = proj(2) * proj(0)
    prev = carry_ref[...]
    rows = lax.broadcasted_iota(jnp.int32, u.shape, 0)
    u1 = jnp.where(rows == 0, prev[7:8, :], pltpu.roll(u, 1, 0))
    u2 = jnp.where(rows == 0, prev[6:7, :], jnp.where(rows == 1, prev[7:8, :], pltpu.roll(u, 2, 0)))
    carry_ref[...] = u[tm - 8:tm, :]
    cw = cw_ref[...]
    y = proj(1) * (cw[0:1, :] * u2 + cw[1:2, :] * u1 + cw[2:3, :] * u)
    yc_ref[...] = (y * lax.rsqrt(_split_dot(y * y, gm) + EPS) * cg_ref[...]).astype(BF16)

    q = proj(3)
    qT_ref[...] = (q * lax.rsqrt(_split_dot(q * q, gm) + EPS) * qg_ref[...]).T.astype(BF16)
    k = proj(4)
    k_ref[...] = (k * lax.rsqrt(_split_dot(k * k, gm) + EPS) * kg_ref[...]).astype(BF16)
    vT_ref[0] = proj(5).T.astype(BF16)


def _inproj(x2, g, w_in, conv_w, conv_g, qg, kg, gmat, *, batch, seq):
    t, d = x2.shape
    dc = gmat.shape[0]
    tm = TK
    nj = seq // tm
    row = lambda b, j: (b * nj + j, 0)
    const = lambda b, j: (0, 0)
    out_sds = jax.ShapeDtypeStruct((t, dc), BF16)
    return pl.pallas_call(
        functools.partial(_inproj_kernel, tm=tm, dc=dc),
        grid=(batch, nj),
        in_specs=[
            pl.BlockSpec((tm, d), row),
            pl.BlockSpec((1, d), const),
            pl.BlockSpec(w_in.shape, const),
            pl.BlockSpec(conv_w.shape, const),
            pl.BlockSpec((1, dc), const),
            pl.BlockSpec((1, dc), const),
            pl.BlockSpec((1, dc), const),
            pl.BlockSpec(gmat.shape, const),
        ],
        out_specs=[
            pl.BlockSpec((tm, dc), row),
            pl.BlockSpec((dc, tm), lambda b, j: (0, b * nj + j)),
            pl.BlockSpec((tm, dc), row),
            pl.BlockSpec((1, dc, tm), lambda b, j: (b * nj + j, 0, 0)),
        ],
        out_shape=[out_sds, jax.ShapeDtypeStruct((dc, t), BF16), out_sds,
                   jax.ShapeDtypeStruct((t // tm, dc, tm), BF16)],
        scratch_shapes=[pltpu.VMEM((8, dc), F32)],
        compiler_params=pltpu.CompilerParams(
            dimension_semantics=("arbitrary", "arbitrary"), vmem_limit_bytes=VMEM_LIMIT),
        name="inproj_conv_qknorm",
    )(x2, g, w_in, conv_w, conv_g, qg, kg, gmat)


def _attn_kernel(lp_ref, sg_ref, qT_ref, k_ref, vT_ref, o_ref, m_ref, l_ref, acc_ref,
                 *, tq, tk, lam_init):
    qi = pl.program_id(2)
    qT = qT_ref[...]
    row = lax.broadcasted_iota(jnp.int32, qT.shape, 0)
    zero = jnp.zeros_like(qT)
    qqT = jnp.concatenate([jnp.where(row < HEAD_DIM, qT, zero),
                           jnp.where(row >= HEAD_DIM, qT, zero)], axis=1)

    m_ref[...] = jnp.full_like(m_ref, -jnp.inf)
    l_ref[...] = jnp.zeros_like(l_ref)
    acc_ref[...] = jnp.zeros_like(acc_ref)

    def step(j, masked):
        start = pl.multiple_of(j * tk, tk)
        kb = k_ref[pl.ds(start, tk), :]
        vb = vT_ref[j]
        s = _dot(kb, qqT)
        if masked:
            c = lax.broadcasted_iota(jnp.int32, s.shape, 1)
            qpos = qi * tq + jnp.where(c >= tq, c - tq, c)
            kpos = start + lax.broadcasted_iota(jnp.int32, s.shape, 0)
            s = jnp.where(kpos <= qpos, s, -jnp.inf)
        m_old = m_ref[...]
        m_new = jnp.maximum(m_old, jnp.max(s, axis=0, keepdims=True))
        alpha = jnp.exp2(m_old - m_new)
        p = jnp.exp2(s - m_new)
        l_ref[...] = alpha * l_ref[...] + jnp.sum(p, axis=0, keepdims=True)
        acc_ref[...] = alpha * acc_ref[...] + _dot(vb, p.astype(BF16))
        m_ref[...] = m_new

    def body(j, c):
        step(j, False)
        return c

    lax.fori_loop(0, qi, body, 0)
    step(qi, True)

    lp = lp_ref[...]
    lam = (jnp.exp(jnp.sum(lp[0:1, :] * lp[1:2, :], axis=-1, keepdims=True))
           - jnp.exp(jnp.sum(lp[2:3, :] * lp[3:4, :], axis=-1, keepdims=True)) + lam_init)
    o = acc_ref[...] / l_ref[...]
    d = o[:, 0:tq] - lam * o[:, tq:2 * tq]
    ms = jnp.mean(d * d, axis=0, keepdims=True)
    o_ref[...] = (d * lax.rsqrt(ms + EPS) * sg_ref[...] * (1.0 - lam_init)).T.astype(BF16)


def _attention(lam_params, subln_g, qT, k, vT, *, batch, seq, lam_init):
    dq, t = qT.shape
    n_heads = dq // V_DIM
    tq, tk = TQ, TK
    assert tq == tk and vT.shape[2] == tk
    nq = seq // tq
    nk = seq // tk
    const = lambda b, h, i: (0, 0)
    return pl.pallas_call(
        functools.partial(_attn_kernel, tq=tq, tk=tk, lam_init=lam_init),
        grid=(batch, n_heads, nq),
        in_specs=[
            pl.BlockSpec(lam_params.shape, const),
            pl.BlockSpec((V_DIM, 1), const),
            pl.BlockSpec((V_DIM, tq), lambda b, h, i: (h, b * nq + i)),
            pl.BlockSpec((seq, V_DIM), lambda b, h, i: (b, h)),
            pl.BlockSpec((nk, V_DIM, tk), lambda b, h, i: (b, h, 0)),
        ],
        out_specs=pl.BlockSpec((tq, V_DIM), lambda b, h, i: (b * nq + i, h)),
        out_shape=jax.ShapeDtypeStruct((t, dq), BF16),
        scratch_shapes=[pltpu.VMEM((1, 2 * tq), F32), pltpu.VMEM((1, 2 * tq), F32),
                        pltpu.VMEM((V_DIM, 2 * tq), F32)],
        compiler_params=pltpu.CompilerParams(
            dimension_semantics=("arbitrary", "arbitrary", "arbitrary"), vmem_limit_bytes=VMEM_LIMIT),
        name="diff_attention",
    )(lam_params, subln_g, qT, k, vT)


def _outproj_router_kernel(x_ref, yc_ref, at_ref, wo_ref, g_ref, wrh_ref, wrl_ref,
                           h_ref, hn_ref, route_ref, cnt_ref, *, tm, dc):
    i = pl.program_id(0)

    @pl.when(i == 0)
    def _():
        cnt_ref[...] = jnp.zeros_like(cnt_ref)

    h = x_ref[...] + _dot(yc_ref[...], wo_ref[0:dc, :]) + _dot(at_ref[...], wo_ref[dc:2 * dc, :])
    h_ref[...] = h
    ms = jnp.mean(h * h, axis=-1, keepdims=True)
    hn = h * lax.rsqrt(ms + EPS) * g_ref[...]
    hi = hn.astype(BF16)
    hn_ref[...] = hi
    lo = (hn - hi.astype(F32)).astype(BF16)
    logits = _dot(hi, wrh_ref[...]) + _dot(lo, wrh_ref[...]) + _dot(hi, wrl_ref[...])

    lane = lax.broadcasted_iota(jnp.int32, logits.shape, 1)
    lane_f = lane.astype(F32)
    neg = -jnp.inf
    is_g = (lane >= N_EXPERTS) & (lane < N_EXPERTS + N_GROUPS)
    gl = jnp.where(is_g, logits, neg)
    gmax = jnp.max(gl, axis=-1, keepdims=True)
    g_gate = 1.0 / jnp.sum(jnp.exp(gl - gmax), axis=-1, keepdims=True)
    g_idx = jnp.min(jnp.where(gl == gmax, lane_f - N_EXPERTS, 1e9), axis=-1, keepdims=True)
    e_lo = g_idx * EXPERTS_PER_GROUP
    in_grp = (lane_f >= e_lo) & (lane_f < e_lo + EXPERTS_PER_GROUP)
    el = jnp.where(in_grp, logits, neg)
    v1 = jnp.max(el, axis=-1, keepdims=True)
    i1 = jnp.min(jnp.where(el == v1, lane_f, 1e9), axis=-1, keepdims=True)
    el2 = jnp.where(lane_f == i1, neg, el)
    v2 = jnp.max(el2, axis=-1, keepdims=True)
    i2 = jnp.min(jnp.where(el2 == v2, lane_f, 1e9), axis=-1, keepdims=True)
    tt = jnp.exp(v2 - v1)
    w1 = g_gate / (1.0 + tt)
    w2 = g_gate * tt / (1.0 + tt)

    sel1 = lane_f == i1
    sel2 = lane_f == i2
    oh = jnp.where(sel1 | sel2, 1.0, 0.0)
    rr = lax.broadcasted_iota(jnp.int32, (tm, tm), 0)
    cc = lax.broadcasted_iota(jnp.int32, (tm, tm), 1)
    lower = jnp.where(cc < rr, 1.0, 0.0).astype(BF16)
    ranks = _dot(lower, oh.astype(BF16)) + cnt_ref[0:1, :]
    r1 = jnp.sum(jnp.where(sel1, ranks, 0.0), axis=-1, keepdims=True)
    r2 = jnp.sum(jnp.where(sel2, ranks, 0.0), axis=-1, keepdims=True)
    cnt_ref[...] = cnt_ref[...] + jnp.sum(oh, axis=0, keepdims=True)

    route_ref[...] = jnp.where(
        lane == 0, i1, jnp.where(lane == 1, i2, jnp.where(lane == 2, r1, jnp.where(
            lane == 3, r2, jnp.where(lane == 4, w1, jnp.where(lane == 5, w2, 0.0))))))


def _outproj_router(x2, yc, at, w_out, g, wr_hi, wr_lo):
    t, d = x2.shape
    dc = yc.shape[1]
    tm = TM_PROJ
    row = lambda i: (i, 0)
    const = lambda i: (0, 0)
    return pl.pallas_call(
        functools.partial(_outproj_router_kernel, tm=tm, dc=dc),
        grid=(t // tm,),
        in_specs=[
            pl.BlockSpec((tm, d), row),
            pl.BlockSpec((tm, dc), row),
            pl.BlockSpec((tm, dc), row),
            pl.BlockSpec(w_out.shape, const),
            pl.BlockSpec((1, d), const),
            pl.BlockSpec(wr_hi.shape, const),
            pl.BlockSpec(wr_lo.shape, const),
        ],
        out_specs=[
            pl.BlockSpec((tm, d), row),
            pl.BlockSpec((tm, d), row),
            pl.BlockSpec((tm, LANES), row),
            pl.BlockSpec((8, LANES), const),
        ],
        out_shape=[
            jax.ShapeDtypeStruct((t, d), F32),
            jax.ShapeDtypeStruct((t, d), BF16),
            jax.ShapeDtypeStruct((t, LANES), F32),
            jax.ShapeDtypeStruct((8, LANES), F32),
        ],
        compiler_params=pltpu.CompilerParams(
            dimension_semantics=("arbitrary",), vmem_limit_bytes=VMEM_LIMIT),
        name="outproj_router",
    )(x2, yc, at, w_out, g, wr_hi, wr_lo)


def _moe_kernel(te_ref, nt_ref, x_ref, rw_ref, wg_ref, wu_ref, wd_ref, y_ref):
    i = pl.program_id(0)

    @pl.when(i < nt_ref[0])
    def _():
        x = x_ref[...]
        hg = _dot(x, wg_ref[0])
        hu = _dot(x, wu_ref[0])
        act = hg * (1.0 / (1.0 + jnp.exp(-hg))) * hu * rw_ref[...]
        y_ref[...] = _dot(act.astype(BF16), wd_ref[0])


def _moe(tile_expert, n_tiles, xs, rw, wg, wu, wd):
    p, d = xs.shape
    f = wg.shape[2]
    tm = TM_MOE
    row = lambda i, te, nt: (jnp.minimum(i, nt[0] - 1), 0)
    wsel = lambda i, te, nt: (te[jnp.minimum(i, nt[0] - 1)], 0, 0)
    grid_spec = pltpu.PrefetchScalarGridSpec(
        num_scalar_prefetch=2,
        grid=(p // tm,),
        in_specs=[
            pl.BlockSpec((tm, d), row),
            pl.BlockSpec((tm, 1), row),
            pl.BlockSpec((1, d, f), wsel),
            pl.BlockSpec((1, d, f), wsel),
            pl.BlockSpec((1, f, d), wsel),
        ],
        out_specs=pl.BlockSpec((tm, d), row),
    )
    return pl.pallas_call(
        _moe_kernel,
        grid_spec=grid_spec,
        out_shape=jax.ShapeDtypeStruct((p, d), F32),
        compiler_params=pltpu.CompilerParams(
            dimension_semantics=("arbitrary",), vmem_limit_bytes=VMEM_LIMIT),
        name="moe_experts",
    )(tile_expert, n_tiles, xs, rw, wg, wu, wd)


def _lambda_init(layer_idx):
    return 0.8 - 0.6 * math.exp(-0.3 * layer_idx)


def _layer(h, l, attn_norm_g, w_in, conv_w, conv_out_g, q_norm_g, k_norm_g,
           lambda_q1, lambda_k1, lambda_q2, lambda_k2, attn_subln_g, w_out,
           ffn_norm_g, w_router_group, w_router_expert, w_exp_gate, w_exp_up, w_exp_down):
    batch, seq, d = h.shape
    t = batch * seq
    dc = conv_w.shape[-1]
    lam_init = _lambda_init(l)
    x2 = h.reshape(t, d)

    reps = dc // HEAD_DIM
    assert dc // CONV_GROUPS == HEAD_DIM
    qg = (jnp.tile(q_norm_g[l], reps) * (HEAD_DIM ** -0.5 * math.log2(math.e))).reshape(1, dc)
    kg = jnp.tile(k_norm_g[l], reps).reshape(1, dc)
    grp = jnp.arange(dc) // HEAD_DIM
    gmat = jnp.where(grp[:, None] == grp[None, :], 1.0 / HEAD_DIM, 0.0).astype(BF16)
    yc, qT, k, vT = _inproj(x2, attn_norm_g[l].reshape(1, d), w_in[l].astype(BF16), conv_w[l],
                          conv_out_g[l].reshape(1, dc), qg, kg, gmat, batch=batch, seq=seq)

    lam_params = jnp.stack([lambda_q1[l], lambda_k1[l], lambda_q2[l], lambda_k2[l]])
    at = _attention(lam_params, attn_subln_g[l].reshape(V_DIM, 1), qT, k, vT,
                    batch=batch, seq=seq, lam_init=lam_init)

    wr = jnp.concatenate([w_router_expert[l], w_router_group[l],
                          jnp.zeros((d, LANES - N_EXPERTS - N_GROUPS), F32)], axis=1)
    wr_hi = wr.astype(BF16)
    wr_lo = (wr - wr_hi.astype(F32)).astype(BF16)
    hres, hn2, route, cnt = _outproj_router(x2, yc, at, w_out[l].astype(BF16),
                                            ffn_norm_g[l].reshape(1, d), wr_hi, wr_lo)

    tmm = TM_MOE
    n_tiles_max = (2 * t) // tmm + N_EXPERTS
    p_rows = n_tiles_max * tmm
    e1 = route[:, 0].astype(jnp.int32)
    e2 = route[:, 1].astype(jnp.int32)
    counts = cnt[0, :N_EXPERTS].astype(jnp.int32)
    tiles = (counts + tmm - 1) // tmm
    tile_end = jnp.cumsum(tiles)
    offs = (tile_end - tiles) * tmm
    pos1 = offs[e1] + route[:, 2].astype(jnp.int32)
    pos2 = offs[e2] + route[:, 3].astype(jnp.int32)
    n_tiles = tile_end[-1:].astype(jnp.int32)
    tile_ids = jnp.arange(n_tiles_max, dtype=jnp.int32)
    tile_expert = jnp.minimum(
        jnp.sum((tile_end[None, :] <= tile_ids[:, None]).astype(jnp.int32), axis=1), N_EXPERTS - 1)

    tok = jnp.arange(t, dtype=jnp.int32)
    src = jnp.full((p_rows,), t, jnp.int32).at[pos1].set(tok).at[pos2].set(tok)
    xs = jnp.take(hn2, src, axis=0, mode="fill", fill_value=0)
    rw = jnp.zeros((p_rows,), F32).at[pos1].set(route[:, 4]).at[pos2].set(route[:, 5]).reshape(p_rows, 1)

    f = w_exp_gate.shape[-1]
    ys = _moe(tile_expert, n_tiles, xs, rw,
              w_exp_gate[l].reshape(N_EXPERTS, d, f).astype(BF16),
              w_exp_up[l].reshape(N_EXPERTS, d, f).astype(BF16),
              w_exp_down[l].reshape(N_EXPERTS, f, d).astype(BF16))
    out = hres + jnp.take(ys, pos1, axis=0) + jnp.take(ys, pos2, axis=0)
    return out.reshape(batch, seq, d)


def kernel(x, attn_norm_g, w_in, conv_w, conv_out_g, q_norm_g, k_norm_g, lambda_q1, lambda_k1,
           lambda_q2, lambda_k2, attn_subln_g, w_out, ffn_norm_g, w_router_group, w_router_expert,
           w_exp_gate, w_exp_up, w_exp_down):
    h = x
    for l in range(attn_norm_g.shape[0]):
        h = _layer(h, l, attn_norm_g, w_in, conv_w, conv_out_g, q_norm_g, k_norm_g,
                   lambda_q1, lambda_k1, lambda_q2, lambda_k2, attn_subln_g, w_out,
                   ffn_norm_g, w_router_group, w_router_expert, w_exp_gate, w_exp_up, w_exp_down)
    return h
```

```python
import functools
import math

import jax
import jax.numpy as jnp
from jax import lax
from jax.experimental import pallas as pl
from jax.experimental.pallas import tpu as pltpu

F32 = jnp.float32
BF16 = jnp.bfloat16

HEAD_DIM = 64
V_DIM = 2 * HEAD_DIM
CONV_GROUPS = 8
N_GROUPS = 4
EXPERTS_PER_GROUP = 8
N_EXPERTS = N_GROUPS * EXPERTS_PER_GROUP
EPS = 1e-6
LANES = 128
VMEM_LIMIT = 48 * 1024 * 1024

TM_PROJ = 256
TQ = 512
TK = 512
TM_MOE = 256


def _dot(a, b):
    return jnp.dot(a, b, preferred_element_type=F32)


def _split_dot(a_f32, b_bf16):
    hi = a_f32.astype(BF16)
    lo = (a_f32 - hi.astype(F32)).astype(BF16)
    return _dot(hi, b_bf16) + _dot(lo, b_bf16)


def _inproj_kernel(x_ref, g_ref, w_ref, cw_ref, cg_ref, qg_ref, kg_ref, gm_ref,
                   yc_ref, qT_ref, k_ref, vT_ref, carry_ref, *, tm, dc):
    j = pl.program_id(1)

    @pl.when(j == 0)
    def _():
        carry_ref[...] = jnp.zeros_like(carry_ref)

    x = x_ref[...]
    ms = jnp.mean(x * x, axis=-1, keepdims=True)
    hn = (x * lax.rsqrt(ms + EPS) * g_ref[...]).astype(BF16)

    def proj(s):
        return _dot(hn, w_ref[:, s * dc:(s + 1) * dc])

    gm = gm_ref[...]

    u = proj(2) * proj(0)
    prev = carry_ref[...]
    rows = lax.broadcasted_iota(jnp.int32, u.shape, 0)
    u1 = jnp.where(rows == 0, prev[7:8, :], pltpu.roll(u, 1, 0))
    u2 = jnp.where(rows == 0, prev[6:7, :], jnp.where(rows == 1, prev[7:8, :], pltpu.roll(u, 2, 0)))
    carry_ref[...] = u[tm - 8:tm, :]
    cw = cw_ref[...]
    y = proj(1) * (cw[0:1, :] * u2 + cw[1:2, :] * u1 + cw[2:3, :] * u)
    yc_ref[...] = (y * lax.rsqrt(_split_dot(y * y, gm) + EPS) * cg_ref[...]).astype(BF16)

    q = proj(3)
    qT_ref[...] = (q * lax.rsqrt(_split_dot(q * q, gm) + EPS) * qg_ref[...]).T.astype(BF16)
    k = proj(4)
    k_ref[...] = (k * lax.rsqrt(_split_dot(k * k, gm) + EPS) * kg_ref[...]).astype(BF16)
    vT_ref[0] = proj(5).T.astype(BF16)


def _inproj(x2, g, w_in, conv_w, conv_g, qg, kg, gmat, *, batch, seq):
    t, d = x2.shape
    dc = gmat.shape[0]
    tm = TK
    nj = seq // tm
    row = lambda b, j: (b * nj + j, 0)
    const = lambda b, j: (0, 0)
    out_sds = jax.ShapeDtypeStruct((t, dc), BF16)
    return pl.pallas_call(
        functools.partial(_inproj_kernel, tm=tm, dc=dc),
        grid=(batch, nj),
        in_specs=[
            pl.BlockSpec((tm, d), row),
            pl.BlockSpec((1, d), const),
            pl.BlockSpec(w_in.shape, const),
            pl.BlockSpec(conv_w.shape, const),
            pl.BlockSpec((1, dc), const),
            pl.BlockSpec((1, dc), const),
            pl.BlockSpec((1, dc), const),
            pl.BlockSpec(gmat.shape, const),
        ],
        out_specs=[
            pl.BlockSpec((tm, dc), row),
            pl.BlockSpec((dc, tm), lambda b, j: (0, b * nj + j)),
            pl.BlockSpec((tm, dc), row),
            pl.BlockSpec((1, dc, tm), lambda b, j: (b * nj + j, 0, 0)),
        ],
        out_shape=[out_sds, jax.ShapeDtypeStruct((dc, t), BF16), out_sds,
                   jax.ShapeDtypeStruct((t // tm, dc, tm), BF16)],
        scratch_shapes=[pltpu.VMEM((8, dc), F32)],
        compiler_params=pltpu.CompilerParams(
            dimension_semantics=("arbitrary", "arbitrary"), vmem_limit_bytes=VMEM_LIMIT),
        name="inproj_conv_qknorm",
    )(x2, g, w_in, conv_w, conv_g, qg, kg, gmat)


def _attn_kernel(lp_ref, sg_ref, qT_ref, k_ref, vT_ref, o_ref, m_ref, l_ref, acc_ref,
                 *, tq, tk, lam_init):
    qi = pl.program_id(2)
    qT = qT_ref[...]
    row = lax.broadcasted_iota(jnp.int32, qT.shape, 0)
    zero = jnp.zeros_like(qT)
    qqT = jnp.concatenate([jnp.where(row < HEAD_DIM, qT, zero),
                           jnp.where(row >= HEAD_DIM, qT, zero)], axis=1)

    m_ref[...] = jnp.full_like(m_ref, -jnp.inf)
    l_ref[...] = jnp.zeros_like(l_ref)
    acc_ref[...] = jnp.zeros_like(acc_ref)

    def step(j, masked):
        start = pl.multiple_of(j * tk, tk)
        kb = k_ref[pl.ds(start, tk), :]
        vb = vT_ref[j]
        s = _dot(kb, qqT)
        if masked:
            c = lax.broadcasted_iota(jnp.int32, s.shape, 1)
            qpos = qi * tq + jnp.where(c >= tq, c - tq, c)
            kpos = start + lax.broadcasted_iota(jnp.int32, s.shape, 0)
            s = jnp.where(kpos <= qpos, s, -jnp.inf)
        m_old = m_ref[...]
        m_new = jnp.maximum(m_old, jnp.max(s, axis=0, keepdims=True))
        alpha = jnp.exp2(m_old - m_new)
        p = jnp.exp2(s - m_new)
        l_ref[...] = alpha * l_ref[...] + jnp.sum(p, axis=0, keepdims=True)
        acc_ref[...] = alpha * acc_ref[...] + _dot(vb, p.astype(BF16))
        m_ref[...] = m_new

    def body(j, c):
        step(j, False)
        return c

    lax.fori_loop(0, qi, body, 0)
    step(qi, True)

    lp = lp_ref[...]
    lam = (jnp.exp(jnp.sum(lp[0:1, :] * lp[1:2, :], axis=-1, keepdims=True))
           - jnp.exp(jnp.sum(lp[2:3, :] * lp[3:4, :], axis=-1, keepdims=True)) + lam_init)
    o = acc_ref[...] / l_ref[...]
    d = o[:, 0:tq] - lam * o[:, tq:2 * tq]
    ms = jnp.mean(d * d, axis=0, keepdims=True)
    o_ref[...] = (d * lax.rsqrt(ms + EPS) * sg_ref[...] * (1.0 - lam_init)).T.astype(BF16)


def _attention(lam_params, subln_g, qT, k, vT, *, batch, seq, lam_init):
    dq, t = qT.shape
    n_heads = dq // V_DIM
    tq, tk = TQ, TK
    assert tq == tk and vT.shape[2] == tk
    nq = seq // tq
    nk = seq // tk
    const = lambda b, h, i: (0, 0)
    return pl.pallas_call(
        functools.partial(_attn_kernel, tq=tq, tk=tk, lam_init=lam_init),
        grid=(batch, n_heads, nq),
        in_specs=[
            pl.BlockSpec(lam_params.shape, const),
            pl.BlockSpec((V_DIM, 1), const),
            pl.BlockSpec((V_DIM, tq), lambda b, h, i: (h, b * nq + i)),
            pl.BlockSpec((seq, V_DIM), lambda b, h, i: (b, h)),
            pl.BlockSpec((nk, V_DIM, tk), lambda b, h, i: (b, h, 0)),
        ],
        out_specs=pl.BlockSpec((tq, V_DIM), lambda b, h, i: (b * nq + i, h)),
        out_shape=jax.ShapeDtypeStruct((t, dq), BF16),
        scratch_shapes=[pltpu.VMEM((1, 2 * tq), F32), pltpu.VMEM((1, 2 * tq), F32),
                        pltpu.VMEM((V_DIM, 2 * tq), F32)],
        compiler_params=pltpu.CompilerParams(
            dimension_semantics=("arbitrary", "arbitrary", "arbitrary"), vmem_limit_bytes=VMEM_LIMIT),
        name="diff_attention",
    )(lam_params, subln_g, qT, k, vT)


def _outproj_router_kernel(x_ref, yc_ref, at_ref, wo_ref, g_ref, wrh_ref, wrl_ref,
                           h_ref, hn_ref, route_ref, cnt_ref, *, tm, dc):
    i = pl.program_id(0)

    @pl.when(i == 0)
    def _():
        cnt_ref[...] = jnp.zeros_like(cnt_ref)

    h = x_ref[...] + _dot(yc_ref[...], wo_ref[0:dc, :]) + _dot(at_ref[...], wo_ref[dc:2 * dc, :])
    h_ref[...] = h
    ms = jnp.mean(h * h, axis=-1, keepdims=True)
    hn = h * lax.rsqrt(ms + EPS) * g_ref[...]
    hi = hn.astype(BF16)
    hn_ref[...] = hn
    lo = (hn - hi.astype(F32)).astype(BF16)
    logits = _dot(hi, wrh_ref[...]) + _dot(lo, wrh_ref[...]) + _dot(hi, wrl_ref[...])

    lane = lax.broadcasted_iota(jnp.int32, logits.shape, 1)
    lane_f = lane.astype(F32)
    neg = -jnp.inf
    is_g = (lane >= N_EXPERTS) & (lane < N_EXPERTS + N_GROUPS)
    gl = jnp.where(is_g, logits, neg)
    gmax = jnp.max(gl, axis=-1, keepdims=True)
    g_gate = 1.0 / jnp.sum(jnp.exp(gl - gmax), axis=-1, keepdims=True)
    g_idx = jnp.min(jnp.where(gl == gmax, lane_f - N_EXPERTS, 1e9), axis=-1, keepdims=True)
    e_lo = g_idx * EXPERTS_PER_GROUP
    in_grp = (lane_f >= e_lo) & (lane_f < e_lo + EXPERTS_PER_GROUP)
    el = jnp.where(in_grp, logits, neg)
    v1 = jnp.max(el, axis=-1, keepdims=True)
    i1 = jnp.min(jnp.where(el == v1, lane_f, 1e9), axis=-1, keepdims=True)
    el2 = jnp.where(lane_f == i1, neg, el)
    v2 = jnp.max(el2, axis=-1, keepdims=True)
    i2 = jnp.min(jnp.where(el2 == v2, lane_f, 1e9), axis=-1, keepdims=True)
    tt = jnp.exp(v2 - v1)
    w1 = g_gate / (1.0 + tt)
    w2 = g_gate * tt / (1.0 + tt)

    sel1 = lane_f == i1
    sel2 = lane_f == i2
    oh = jnp.where(sel1 | sel2, 1.0, 0.0)
    rr = lax.broadcasted_iota(jnp.int32, (tm, tm), 0)
    cc = lax.broadcasted_iota(jnp.int32, (tm, tm), 1)
    lower = jnp.where(cc < rr, 1.0, 0.0).astype(BF16)
    ranks = _dot(lower, oh.astype(BF16)) + cnt_ref[0:1, :]
    r1 = jnp.sum(jnp.where(sel1, ranks, 0.0), axis=-1, keepdims=True)
    r2 = jnp.sum(jnp.where(sel2, ranks, 0.0), axis=-1, keepdims=True)
    cnt_ref[...] = cnt_ref[...] + jnp.sum(oh, axis=0, keepdims=True)

    route_ref[...] = jnp.where(
        lane == 0, i1, jnp.where(lane == 1, i2, jnp.where(lane == 2, r1, jnp.where(
            lane == 3, r2, jnp.where(lane == 4, w1, jnp.where(lane == 5, w2, 0.0))))))


def _outproj_router(x2, yc, at, w_out, g, wr_hi, wr_lo):
    t, d = x2.shape
    dc = yc.shape[1]
    tm = TM_PROJ
    row = lambda i: (i, 0)
    const = lambda i: (0, 0)
    return pl.pallas_call(
        functools.partial(_outproj_router_kernel, tm=tm, dc=dc),
        grid=(t // tm,),
        in_specs=[
            pl.BlockSpec((tm, d), row),
            pl.BlockSpec((tm, dc), row),
            pl.BlockSpec((tm, dc), row),
            pl.BlockSpec(w_out.shape, const),
            pl.BlockSpec((1, d), const),
            pl.BlockSpec(wr_hi.shape, const),
            pl.BlockSpec(wr_lo.shape, const),
        ],
        out_specs=[
            pl.BlockSpec((tm, d), row),
            pl.BlockSpec((tm, d), row),
            pl.BlockSpec((tm, LANES), row),
            pl.BlockSpec((8, LANES), const),
        ],
        out_shape=[
            jax.ShapeDtypeStruct((t, d), F32),
            jax.ShapeDtypeStruct((t, d), F32),
            jax.ShapeDtypeStruct((t, LANES), F32),
            jax.ShapeDtypeStruct((8, LANES), F32),
        ],
        compiler_params=pltpu.CompilerParams(
            dimension_semantics=("arbitrary",), vmem_limit_bytes=VMEM_LIMIT),
        name="outproj_router",
    )(x2, yc, at, w_out, g, wr_hi, wr_lo)


def _moe_kernel(te_ref, nt_ref, x_ref, wg_ref, wu_ref, wd_ref, y_ref):
    i = pl.program_id(0)

    @pl.when(i < nt_ref[0])
    def _():
        x = x_ref[...].astype(BF16)
        hg = _dot(x, wg_ref[0])
        hu = _dot(x, wu_ref[0])
        act = hg * (1.0 / (1.0 + jnp.exp(-hg))) * hu
        y_ref[...] = _dot(act.astype(BF16), wd_ref[0])


def _moe(tile_expert, n_tiles, xs, wg, wu, wd):
    p, d = xs.shape
    f = wg.shape[2]
    tm = TM_MOE
    row = lambda i, te, nt: (jnp.minimum(i, nt[0] - 1), 0)
    wsel = lambda i, te, nt: (te[jnp.minimum(i, nt[0] - 1)], 0, 0)
    grid_spec = pltpu.PrefetchScalarGridSpec(
        num_scalar_prefetch=2,
        grid=(p // tm,),
        in_specs=[
            pl.BlockSpec((tm, d), row),
            pl.BlockSpec((1, d, f), wsel),
            pl.BlockSpec((1, d, f), wsel),
            pl.BlockSpec((1, f, d), wsel),
        ],
        out_specs=pl.BlockSpec((tm, d), row),
    )
    return pl.pallas_call(
        _moe_kernel,
        grid_spec=grid_spec,
        out_shape=jax.ShapeDtypeStruct((p, d), F32),
        compiler_params=pltpu.CompilerParams(
            dimension_semantics=("arbitrary",), vmem_limit_bytes=VMEM_LIMIT),
        name="moe_experts",
    )(tile_expert, n_tiles, xs, wg, wu, wd)


def _lambda_init(layer_idx):
    return 0.8 - 0.6 * math.exp(-0.3 * layer_idx)


def _layer(h, l, attn_norm_g, w_in, conv_w, conv_out_g, q_norm_g, k_norm_g,
           lambda_q1, lambda_k1, lambda_q2, lambda_k2, attn_subln_g, w_out,
           ffn_norm_g, w_router_group, w_router_expert, w_exp_gate, w_exp_up, w_exp_down):
    batch, seq, d = h.shape
    t = batch * seq
    dc = conv_w.shape[-1]
    lam_init = _lambda_init(l)
    x2 = h.reshape(t, d)

    reps = dc // HEAD_DIM
    assert dc // CONV_GROUPS == HEAD_DIM
    qg = (jnp.tile(q_norm_g[l], reps) * (HEAD_DIM ** -0.5 * math.log2(math.e))).reshape(1, dc)
    kg = jnp.tile(k_norm_g[l], reps).reshape(1, dc)
    grp = jnp.arange(dc) // HEAD_DIM
    gmat = jnp.where(grp[:, None] == grp[None, :], 1.0 / HEAD_DIM, 0.0).astype(BF16)
    yc, qT, k, vT = _inproj(x2, attn_norm_g[l].reshape(1, d), w_in[l].astype(BF16), conv_w[l],
                          conv_out_g[l].reshape(1, dc), qg, kg, gmat, batch=batch, seq=seq)

    lam_params = jnp.stack([lambda_q1[l], lambda_k1[l], lambda_q2[l], lambda_k2[l]])
    at = _attention(lam_params, attn_subln_g[l].reshape(V_DIM, 1), qT, k, vT,
                    batch=batch, seq=seq, lam_init=lam_init)

    wr = jnp.concatenate([w_router_expert[l], w_router_group[l],
                          jnp.zeros((d, LANES - N_EXPERTS - N_GROUPS), F32)], axis=1)
    wr_hi = wr.astype(BF16)
    wr_lo = (wr - wr_hi.astype(F32)).astype(BF16)
    hres, hn2, route, cnt = _outproj_router(x2, yc, at, w_out[l].astype(BF16),
                                            ffn_norm_g[l].reshape(1, d), wr_hi, wr_lo)

    tmm = TM_MOE
    n_tiles_max = (2 * t) // tmm + N_EXPERTS
    p_rows = n_tiles_max * tmm
    e1 = route[:, 0].astype(jnp.int32)
    e2 = route[:, 1].astype(jnp.int32)
    counts = cnt[0, :N_EXPERTS].astype(jnp.int32)
    tiles = (counts + tmm - 1) // tmm
    tile_end = jnp.cumsum(tiles)
    offs = (tile_end - tiles) * tmm
    pos1 = offs[e1] + route[:, 2].astype(jnp.int32)
    pos2 = offs[e2] + route[:, 3].astype(jnp.int32)
    n_tiles = tile_end[-1:].astype(jnp.int32)
    tile_ids = jnp.arange(n_tiles_max, dtype=jnp.int32)
    tile_expert = jnp.minimum(
        jnp.sum((tile_end[None, :] <= tile_ids[:, None]).astype(jnp.int32), axis=1), N_EXPERTS - 1)

    tok = jnp.arange(t, dtype=jnp.int32)
    src = jnp.zeros((p_rows,), jnp.int32).at[jnp.concatenate([pos1, pos2])].set(jnp.concatenate([tok, tok]))
    xs = jnp.take(hn2, src, axis=0)

    f = w_exp_gate.shape[-1]
    ys = _moe(tile_expert, n_tiles, xs,
              w_exp_gate[l].reshape(N_EXPERTS, d, f).astype(BF16),
              w_exp_up[l].reshape(N_EXPERTS, d, f).astype(BF16),
              w_exp_down[l].reshape(N_EXPERTS, f, d).astype(BF16))
    out = (hres + route[:, 4:5] * jnp.take(ys, pos1, axis=0)
           + route[:, 5:6] * jnp.take(ys, pos2, axis=0))
    return out.reshape(batch, seq, d)


def kernel(x, attn_norm_g, w_in, conv_w, conv_out_g, q_norm_g, k_norm_g, lambda_q1, lambda_k1,
           lambda_q2, lambda_k2, attn_subln_g, w_out, ffn_norm_g, w_router_group, w_router_expert,
           w_exp_gate, w_exp_up, w_exp_down):
    h = x
    for l in range(attn_norm_g.shape[0]):
        h = _layer(h, l, attn_norm_g, w_in, conv_w, conv_out_g, q_norm_g, k_norm_g,
                   lambda_q1, lambda_k1, lambda_q2, lambda_k2, attn_subln_g, w_out,
                   ffn_norm_g, w_router_group, w_router_expert, w_exp_gate, w_exp_up, w_exp_down)
    return h
```

```python
import functools
import math

import jax
import jax.numpy as jnp
from jax import lax
from jax.experimental import pallas as pl
from jax.experimental.pallas import tpu as pltpu

F32 = jnp.float32
BF16 = jnp.bfloat16

HEAD_DIM = 64
V_DIM = 2 * HEAD_DIM
CONV_GROUPS = 8
N_GROUPS = 4
EXPERTS_PER_GROUP = 8
N_EXPERTS = N_GROUPS * EXPERTS_PER_GROUP
EPS = 1e-6
LANES = 128
VMEM_LIMIT = 48 * 1024 * 1024

TM_PROJ = 256
TQ = 512
TK = 512
TM_MOE = 256


def _dot(a, b):
    return jnp.dot(a, b, preferred_element_type=F32)


def _split_dot(a_f32, b_bf16):
    hi = a_f32.astype(BF16)
    lo = (a_f32 - hi.astype(F32)).astype(BF16)
    return _dot(hi, b_bf16) + _dot(lo, b_bf16)


def _inproj_kernel(x_ref, g_ref, w_ref, cw_ref, cg_ref, qg_ref, kg_ref, gm_ref,
                   yc_ref, qT_ref, k_ref, vT_ref, carry_ref, *, tm, dc):
    j = pl.program_id(1)

    @pl.when(j == 0)
    def _():
        carry_ref[...] = jnp.zeros_like(carry_ref)

    x = x_ref[...]
    ms = jnp.mean(x * x, axis=-1, keepdims=True)
    hn = (x * lax.rsqrt(ms + EPS) * g_ref[...]).astype(BF16)

    def proj(s):
        return _dot(hn, w_ref[:, s * dc:(s + 1) * dc])

    gm = gm_ref[...]

    u = proj(2) * proj(0)
    prev = carry_ref[...]
    rows = lax.broadcasted_iota(jnp.int32, u.shape, 0)
    u1 = jnp.where(rows == 0, prev[7:8, :], pltpu.roll(u, 1, 0))
    u2 = jnp.where(rows == 0, prev[6:7, :], jnp.where(rows == 1, prev[7:8, :], pltpu.roll(u, 2, 0)))
    carry_ref[...] = u[tm - 8:tm, :]
    cw = cw_ref[...]
    y = proj(1) * (cw[0:1, :] * u2 + cw[1:2, :] * u1 + cw[2:3, :] * u)
    yc_ref[...] = (y * lax.rsqrt(_split_dot(y * y, gm) + EPS) * cg_ref[...]).astype(BF16)

    q = proj(3)
    qT_ref[...] = (q * lax.rsqrt(_split_dot(q * q, gm) + EPS) * qg_ref[...]).T.astype(BF16)
    k = proj(4)
    k_ref[...] = (k * lax.rsqrt(_split_dot(k * k, gm) + EPS) * kg_ref[...]).astype(BF16)
    vT_ref[0] = proj(5).T.astype(BF16)


def _inproj(x2, g, w_in, conv_w, conv_g, qg, kg, gmat, *, batch, seq):
    t, d = x2.shape
    dc = gmat.shape[0]
    tm = TK
    nj = seq // tm
    row = lambda b, j: (b * nj + j, 0)
    const = lambda b, j: (0, 0)
    out_sds = jax.ShapeDtypeStruct((t, dc), BF16)
    return pl.pallas_call(
        functools.partial(_inproj_kernel, tm=tm, dc=dc),
        grid=(batch, nj),
        in_specs=[
            pl.BlockSpec((tm, d), row),
            pl.BlockSpec((1, d), const),
            pl.BlockSpec(w_in.shape, const),
            pl.BlockSpec(conv_w.shape, const),
            pl.BlockSpec((1, dc), const),
            pl.BlockSpec((1, dc), const),
            pl.BlockSpec((1, dc), const),
            pl.BlockSpec(gmat.shape, const),
        ],
        out_specs=[
            pl.BlockSpec((tm, dc), row),
            pl.BlockSpec((dc, tm), lambda b, j: (0, b * nj + j)),
            pl.BlockSpec((tm, dc), row),
            pl.BlockSpec((1, dc, tm), lambda b, j: (b * nj + j, 0, 0)),
        ],
        out_shape=[out_sds, jax.ShapeDtypeStruct((dc, t), BF16), out_sds,
                   jax.ShapeDtypeStruct((t // tm, dc, tm), BF16)],
        scratch_shapes=[pltpu.VMEM((8, dc), F32)],
        compiler_params=pltpu.CompilerParams(
            dimension_semantics=("arbitrary", "arbitrary"), vmem_limit_bytes=VMEM_LIMIT),
        name="inproj_conv_qknorm",
    )(x2, g, w_in, conv_w, conv_g, qg, kg, gmat)


def _attn_kernel(lp_ref, sg_ref, qT_ref, k_ref, vT_ref, o_ref, m_ref, l_ref, acc_ref,
                 *, tq, tk, lam_init):
    qi = pl.program_id(2)
    qT = qT_ref[...]
    row = lax.broadcasted_iota(jnp.int32, qT.shape, 0)
    zero = jnp.zeros_like(qT)
    qqT = jnp.concatenate([jnp.where(row < HEAD_DIM, qT, zero),
                           jnp.where(row >= HEAD_DIM, qT, zero)], axis=1)

    m_ref[...] = jnp.full_like(m_ref, -jnp.inf)
    l_ref[...] = jnp.zeros_like(l_ref)
    acc_ref[...] = jnp.zeros_like(acc_ref)

    def step(j, masked):
        start = pl.multiple_of(j * tk, tk)
        kb = k_ref[pl.ds(start, tk), :]
        vb = vT_ref[j]
        s = _dot(kb, qqT)
        if masked:
            c = lax.broadcasted_iota(jnp.int32, s.shape, 1)
            qpos = qi * tq + jnp.where(c >= tq, c - tq, c)
            kpos = start + lax.broadcasted_iota(jnp.int32, s.shape, 0)
            s = jnp.where(kpos <= qpos, s, -jnp.inf)
        m_old = m_ref[...]
        m_new = jnp.maximum(m_old, jnp.max(s, axis=0, keepdims=True))
        alpha = jnp.exp2(m_old - m_new)
        p = jnp.exp2(s - m_new)
        l_ref[...] = alpha * l_ref[...] + jnp.sum(p, axis=0, keepdims=True)
        acc_ref[...] = alpha * acc_ref[...] + _dot(vb, p.astype(BF16))
        m_ref[...] = m_new

    def body(j, c):
        step(j, False)
        return c

    lax.fori_loop(0, qi, body, 0)
    step(qi, True)

    lp = lp_ref[...]
    lam = (jnp.exp(jnp.sum(lp[0:1, :] * lp[1:2, :], axis=-1, keepdims=True))
           - jnp.exp(jnp.sum(lp[2:3, :] * lp[3:4, :], axis=-1, keepdims=True)) + lam_init)
    o = acc_ref[...] / l_ref[...]
    d = o[:, 0:tq] - lam * o[:, tq:2 * tq]
    ms = jnp.mean(d * d, axis=0, keepdims=True)
    o_ref[...] = (d * lax.rsqrt(ms + EPS) * sg_ref[...] * (1.0 - lam_init)).T.astype(BF16)


def _attention(lam_params, subln_g, qT, k, vT, *, batch, seq, lam_init):
    dq, t = qT.shape
    n_heads = dq // V_DIM
    tq, tk = TQ, TK
    assert tq == tk and vT.shape[2] == tk
    nq = seq // tq
    nk = seq // tk
    const = lambda b, h, i: (0, 0)
    return pl.pallas_call(
        functools.partial(_attn_kernel, tq=tq, tk=tk, lam_init=lam_init),
        grid=(batch, n_heads, nq),
        in_specs=[
            pl.BlockSpec(lam_params.shape, const),
            pl.BlockSpec((V_DIM, 1), const),
            pl.BlockSpec((V_DIM, tq), lambda b, h, i: (h, b * nq + i)),
            pl.BlockSpec((seq, V_DIM), lambda b, h, i: (b, h)),
            pl.BlockSpec((nk, V_DIM, tk), lambda b, h, i: (b, h, 0)),
        ],
        out_specs=pl.BlockSpec((tq, V_DIM), lambda b, h, i: (b * nq + i, h)),
        out_shape=jax.ShapeDtypeStruct((t, dq), BF16),
        scratch_shapes=[pltpu.VMEM((1, 2 * tq), F32), pltpu.VMEM((1, 2 * tq), F32),
                        pltpu.VMEM((V_DIM, 2 * tq), F32)],
        compiler_params=pltpu.CompilerParams(
            dimension_semantics=("arbitrary", "arbitrary", "arbitrary"), vmem_limit_bytes=VMEM_LIMIT),
        name="diff_attention",
    )(lam_params, subln_g, qT, k, vT)


def _outproj_router_kernel(x_ref, yc_ref, at_ref, wo_ref, g_ref, wrh_ref, wrl_ref,
                           h_ref, hn_ref, route_ref, cnt_ref, *, tm, dc):
    i = pl.program_id(0)

    @pl.when(i == 0)
    def _():
        cnt_ref[...] = jnp.zeros_like(cnt_ref)

    h = x_ref[...] + _dot(yc_ref[...], wo_ref[0:dc, :]) + _dot(at_ref[...], wo_ref[dc:2 * dc, :])
    h_ref[...] = h
    ms = jnp.mean(h * h, axis=-1, keepdims=True)
    hn = h * lax.rsqrt(ms + EPS) * g_ref[...]
    hi = hn.astype(BF16)
    hn_ref[...] = hn
    lo = (hn - hi.astype(F32)).astype(BF16)
    logits = _dot(hi, wrh_ref[...]) + _dot(lo, wrh_ref[...]) + _dot(hi, wrl_ref[...])

    lane = lax.broadcasted_iota(jnp.int32, logits.shape, 1)
    lane_f = lane.astype(F32)
    neg = -jnp.inf
    is_g = (lane >= N_EXPERTS) & (lane < N_EXPERTS + N_GROUPS)
    gl = jnp.where(is_g, logits, neg)
    gmax = jnp.max(gl, axis=-1, keepdims=True)
    g_gate = 1.0 / jnp.sum(jnp.exp(gl - gmax), axis=-1, keepdims=True)
    g_idx = jnp.min(jnp.where(gl == gmax, lane_f - N_EXPERTS, 1e9), axis=-1, keepdims=True)
    e_lo = g_idx * EXPERTS_PER_GROUP
    in_grp = (lane_f >= e_lo) & (lane_f < e_lo + EXPERTS_PER_GROUP)
    el = jnp.where(in_grp, logits, neg)
    v1 = jnp.max(el, axis=-1, keepdims=True)
    i1 = jnp.min(jnp.where(el == v1, lane_f, 1e9), axis=-1, keepdims=True)
    el2 = jnp.where(lane_f == i1, neg, el)
    v2 = jnp.max(el2, axis=-1, keepdims=True)
    i2 = jnp.min(jnp.where(el2 == v2, lane_f, 1e9), axis=-1, keepdims=True)
    tt = jnp.exp(v2 - v1)
    w1 = g_gate / (1.0 + tt)
    w2 = g_gate * tt / (1.0 + tt)

    sel1 = lane_f == i1
    sel2 = lane_f == i2
    oh = jnp.where(sel1 | sel2, 1.0, 0.0)
    rr = lax.broadcasted_iota(jnp.int32, (tm, tm), 0)
    cc = lax.broadcasted_iota(jnp.int32, (tm, tm), 1)
    lower = jnp.where(cc < rr, 1.0, 0.0).astype(BF16)
    ranks = _dot(lower, oh.astype(BF16)) + cnt_ref[0:1, :]
    r1 = jnp.sum(jnp.where(sel1, ranks, 0.0), axis=-1, keepdims=True)
    r2 = jnp.sum(jnp.where(sel2, ranks, 0.0), axis=-1, keepdims=True)
    cnt_ref[...] = cnt_ref[...] + jnp.sum(oh, axis=0, keepdims=True)

    route_ref[...] = jnp.where(
        lane == 0, i1, jnp.where(lane == 1, i2, jnp.where(lane == 2, r1, jnp.where(
            lane == 3, r2, jnp.where(lane == 4, w1, jnp.where(lane == 5, w2, 0.0))))))


def _outproj_router(x2, yc, at, w_out, g, wr_hi, wr_lo):
    t, d = x2.shape
    dc = yc.shape[1]
    tm = TM_PROJ
    row = lambda i: (i, 0)
    const = lambda i: (0, 0)
    return pl.pallas_call(
        functools.partial(_outproj_router_kernel, tm=tm, dc=dc),
        grid=(t // tm,),
        in_specs=[
            pl.BlockSpec((tm, d), row),
            pl.BlockSpec((tm, dc), row),
            pl.BlockSpec((tm, dc), row),
            pl.BlockSpec(w_out.shape, const),
            pl.BlockSpec((1, d), const),
            pl.BlockSpec(wr_hi.shape, const),
            pl.BlockSpec(wr_lo.shape, const),
        ],
        out_specs=[
            pl.BlockSpec((tm, d), row),
            pl.BlockSpec((tm, d), row),
            pl.BlockSpec((tm, LANES), row),
            pl.BlockSpec((8, LANES), const),
        ],
        out_shape=[
            jax.ShapeDtypeStruct((t, d), F32),
            jax.ShapeDtypeStruct((t, d), F32),
            jax.ShapeDtypeStruct((t, LANES), F32),
            jax.ShapeDtypeStruct((8, LANES), F32),
        ],
        compiler_params=pltpu.CompilerParams(
            dimension_semantics=("arbitrary",), vmem_limit_bytes=VMEM_LIMIT),
        name="outproj_router",
    )(x2, yc, at, w_out, g, wr_hi, wr_lo)


def _moe_kernel(te_ref, nt_ref, x_ref, wg_ref, wu_ref, wd_ref, y_ref, wgb_ref, wub_ref, wdb_ref):
    i = pl.program_id(0)
    last = nt_ref[0] - 1
    expert = te_ref[jnp.minimum(i, last)]
    prev_expert = te_ref[jnp.minimum(jnp.maximum(i - 1, 0), last)]

    @pl.when((i == 0) | (expert != prev_expert))
    def _():
        wgb_ref[...] = wg_ref[0].astype(BF16)
        wub_ref[...] = wu_ref[0].astype(BF16)
        wdb_ref[...] = wd_ref[0].astype(BF16)

    @pl.when(i <= last)
    def _():
        x = x_ref[...].astype(BF16)
        hg = _dot(x, wgb_ref[...])
        hu = _dot(x, wub_ref[...])
        act = hg * (1.0 / (1.0 + jnp.exp(-hg))) * hu
        y_ref[...] = _dot(act.astype(BF16), wdb_ref[...])


def _moe(tile_expert, n_tiles, xs, wg, wu, wd):
    p, d = xs.shape
    f = wg.shape[2]
    tm = TM_MOE
    row = lambda i, te, nt: (jnp.minimum(i, nt[0] - 1), 0)
    wsel = lambda i, te, nt: (te[jnp.minimum(i, nt[0] - 1)], 0, 0)
    grid_spec = pltpu.PrefetchScalarGridSpec(
        num_scalar_prefetch=2,
        grid=(p // tm,),
        in_specs=[
            pl.BlockSpec((tm, d), row),
            pl.BlockSpec((1, d, f), wsel),
            pl.BlockSpec((1, d, f), wsel),
            pl.BlockSpec((1, f, d), wsel),
        ],
        out_specs=pl.BlockSpec((tm, d), row),
        scratch_shapes=[pltpu.VMEM((d, f), BF16), pltpu.VMEM((d, f), BF16), pltpu.VMEM((f, d), BF16)],
    )
    return pl.pallas_call(
        _moe_kernel,
        grid_spec=grid_spec,
        out_shape=jax.ShapeDtypeStruct((p, d), F32),
        compiler_params=pltpu.CompilerParams(
            dimension_semantics=("arbitrary",), vmem_limit_bytes=VMEM_LIMIT),
        name="moe_experts",
    )(tile_expert, n_tiles, xs, wg, wu, wd)


def _lambda_init(layer_idx):
    return 0.8 - 0.6 * math.exp(-0.3 * layer_idx)


def _layer(h, l, attn_norm_g, w_in, conv_w, conv_out_g, q_norm_g, k_norm_g,
           lambda_q1, lambda_k1, lambda_q2, lambda_k2, attn_subln_g, w_out,
           ffn_norm_g, w_router_group, w_router_expert, w_exp_gate, w_exp_up, w_exp_down):
    batch, seq, d = h.shape
    t = batch * seq
    dc = conv_w.shape[-1]
    lam_init = _lambda_init(l)
    x2 = h.reshape(t, d)

    reps = dc // HEAD_DIM
    assert dc // CONV_GROUPS == HEAD_DIM
    qg = (jnp.tile(q_norm_g[l], reps) * (HEAD_DIM ** -0.5 * math.log2(math.e))).reshape(1, dc)
    kg = jnp.tile(k_norm_g[l], reps).reshape(1, dc)
    grp = jnp.arange(dc) // HEAD_DIM
    gmat = jnp.where(grp[:, None] == grp[None, :], 1.0 / HEAD_DIM, 0.0).astype(BF16)
    yc, qT, k, vT = _inproj(x2, attn_norm_g[l].reshape(1, d), w_in[l].astype(BF16), conv_w[l],
                          conv_out_g[l].reshape(1, dc), qg, kg, gmat, batch=batch, seq=seq)

    lam_params = jnp.stack([lambda_q1[l], lambda_k1[l], lambda_q2[l], lambda_k2[l]])
    at = _attention(lam_params, attn_subln_g[l].reshape(V_DIM, 1), qT, k, vT,
                    batch=batch, seq=seq, lam_init=lam_init)

    wr = jnp.concatenate([w_router_expert[l], w_router_group[l],
                          jnp.zeros((d, LANES - N_EXPERTS - N_GROUPS), F32)], axis=1)
    wr_hi = wr.astype(BF16)
    wr_lo = (wr - wr_hi.astype(F32)).astype(BF16)
    hres, hn2, route, cnt = _outproj_router(x2, yc, at, w_out[l].astype(BF16),
                                            ffn_norm_g[l].reshape(1, d), wr_hi, wr_lo)

    tmm = TM_MOE
    n_tiles_max = (2 * t) // tmm + N_EXPERTS
    p_rows = n_tiles_max * tmm
    e1 = route[:, 0].astype(jnp.int32)
    e2 = route[:, 1].astype(jnp.int32)
    counts = cnt[0, :N_EXPERTS].astype(jnp.int32)
    tiles = (counts + tmm - 1) // tmm
    tile_end = jnp.cumsum(tiles)
    offs = (tile_end - tiles) * tmm
    pos1 = offs[e1] + route[:, 2].astype(jnp.int32)
    pos2 = offs[e2] + route[:, 3].astype(jnp.int32)
    n_tiles = tile_end[-1:].astype(jnp.int32)
    tile_ids = jnp.arange(n_tiles_max, dtype=jnp.int32)
    tile_expert = jnp.minimum(
        jnp.sum((tile_end[None, :] <= tile_ids[:, None]).astype(jnp.int32), axis=1), N_EXPERTS - 1)

    tok = jnp.arange(t, dtype=jnp.int32)
    src = jnp.zeros((p_rows,), jnp.int32).at[jnp.concatenate([pos1, pos2])].set(jnp.concatenate([tok, tok]))
    xs = jnp.take(hn2, src, axis=0)

    f = w_exp_gate.shape[-1]
    ys = _moe(tile_expert, n_tiles, xs,
              w_exp_gate[l].reshape(N_EXPERTS, d, f),
              w_exp_up[l].reshape(N_EXPERTS, d, f),
              w_exp_down[l].reshape(N_EXPERTS, f, d))
    out = (hres + route[:, 4:5] * jnp.take(ys, pos1, axis=0)
           + route[:, 5:6] * jnp.take(ys, pos2, axis=0))
    return out.reshape(batch, seq, d)


def kernel(x, attn_norm_g, w_in, conv_w, conv_out_g, q_norm_g, k_norm_g, lambda_q1, lambda_k1,
           lambda_q2, lambda_k2, attn_subln_g, w_out, ffn_norm_g, w_router_group, w_router_expert,
           w_exp_gate, w_exp_up, w_exp_down):
    h = x
    for l in range(attn_norm_g.shape[0]):
        h = _layer(h, l, attn_norm_g, w_in, conv_w, conv_out_g, q_norm_g, k_norm_g,
                   lambda_q1, lambda_k1, lambda_q2, lambda_k2, attn_subln_g, w_out,
                   ffn_norm_g, w_router_group, w_router_expert, w_exp_gate, w_exp_up, w_exp_down)
    return h
```

```python
import functools
import math

import jax
import jax.numpy as jnp
from jax import lax
from jax.experimental import pallas as pl
from jax.experimental.pallas import tpu as pltpu
from jax.experimental.pallas import tpu_sc as plsc

F32 = jnp.float32
BF16 = jnp.bfloat16

HEAD_DIM = 64
V_DIM = 2 * HEAD_DIM
CONV_GROUPS = 8
N_GROUPS = 4
EXPERTS_PER_GROUP = 8
N_EXPERTS = N_GROUPS * EXPERTS_PER_GROUP
EPS = 1e-6
LANES = 128
VMEM_LIMIT = 48 * 1024 * 1024

TM_PROJ = 256
TQ = 512
TK = 512
TM_MOE = 256
SC_CORES = 2
SC_SUBCORES = 16
SC_WIN = 32


def _dot(a, b):
    return jnp.dot(a, b, preferred_element_type=F32)


def _split_dot(a_f32, b_bf16):
    hi = a_f32.astype(BF16)
    lo = (a_f32 - hi.astype(F32)).astype(BF16)
    return _dot(hi, b_bf16) + _dot(lo, b_bf16)


def _inproj_kernel(x_ref, g_ref, w_ref, cw_ref, cg_ref, qg_ref, kg_ref, gm_ref,
                   yc_ref, qT_ref, k_ref, vT_ref, carry_ref, *, tm, dc):
    j = pl.program_id(1)

    @pl.when(j == 0)
    def _():
        carry_ref[...] = jnp.zeros_like(carry_ref)

    x = x_ref[...]
    ms = jnp.mean(x * x, axis=-1, keepdims=True)
    hn = (x * lax.rsqrt(ms + EPS) * g_ref[...]).astype(BF16)

    def proj(s):
        return _dot(hn, w_ref[:, s * dc:(s + 1) * dc])

    gm = gm_ref[...]

    u = proj(2) * proj(0)
    prev = carry_ref[...]
    rows = lax.broadcasted_iota(jnp.int32, u.shape, 0)
    u1 = jnp.where(rows == 0, prev[7:8, :], pltpu.roll(u, 1, 0))
    u2 = jnp.where(rows == 0, prev[6:7, :], jnp.where(rows == 1, prev[7:8, :], pltpu.roll(u, 2, 0)))
    carry_ref[...] = u[tm - 8:tm, :]
    cw = cw_ref[...]
    y = proj(1) * (cw[0:1, :] * u2 + cw[1:2, :] * u1 + cw[2:3, :] * u)
    yc_ref[...] = (y * lax.rsqrt(_split_dot(y * y, gm) + EPS) * cg_ref[...]).astype(BF16)

    q = proj(3)
    qT_ref[...] = (q * lax.rsqrt(_split_dot(q * q, gm) + EPS) * qg_ref[...]).T.astype(BF16)
    k = proj(4)
    k_ref[...] = (k * lax.rsqrt(_split_dot(k * k, gm) + EPS) * kg_ref[...]).astype(BF16)
    vT_ref[0] = proj(5).T.astype(BF16)


def _inproj(x2, g, w_in, conv_w, conv_g, qg, kg, gmat, *, batch, seq):
    t, d = x2.shape
    dc = gmat.shape[0]
    tm = TK
    nj = seq // tm
    row = lambda b, j: (b * nj + j, 0)
    const = lambda b, j: (0, 0)
    out_sds = jax.ShapeDtypeStruct((t, dc), BF16)
    return pl.pallas_call(
        functools.partial(_inproj_kernel, tm=tm, dc=dc),
        grid=(batch, nj),
        in_specs=[
            pl.BlockSpec((tm, d), row),
            pl.BlockSpec((1, d), const),
            pl.BlockSpec(w_in.shape, const),
            pl.BlockSpec(conv_w.shape, const),
            pl.BlockSpec((1, dc), const),
            pl.BlockSpec((1, dc), const),
            pl.BlockSpec((1, dc), const),
            pl.BlockSpec(gmat.shape, const),
        ],
        out_specs=[
            pl.BlockSpec((tm, dc), row),
            pl.BlockSpec((dc, tm), lambda b, j: (0, b * nj + j)),
            pl.BlockSpec((tm, dc), row),
            pl.BlockSpec((1, dc, tm), lambda b, j: (b * nj + j, 0, 0)),
        ],
        out_shape=[out_sds, jax.ShapeDtypeStruct((dc, t), BF16), out_sds,
                   jax.ShapeDtypeStruct((t // tm, dc, tm), BF16)],
        scratch_shapes=[pltpu.VMEM((8, dc), F32)],
        compiler_params=pltpu.CompilerParams(
            dimension_semantics=("arbitrary", "arbitrary"), vmem_limit_bytes=VMEM_LIMIT),
        name="inproj_conv_qknorm",
    )(x2, g, w_in, conv_w, conv_g, qg, kg, gmat)


def _attn_kernel(lp_ref, sg_ref, qT_ref, k_ref, vT_ref, o_ref, m_ref, l_ref, acc_ref,
                 *, tq, tk, lam_init):
    qi = pl.program_id(2)
    qT = qT_ref[...]
    row = lax.broadcasted_iota(jnp.int32, qT.shape, 0)
    zero = jnp.zeros_like(qT)
    qqT = jnp.concatenate([jnp.where(row < HEAD_DIM, qT, zero),
                           jnp.where(row >= HEAD_DIM, qT, zero)], axis=1)

    m_ref[...] = jnp.full_like(m_ref, -jnp.inf)
    l_ref[...] = jnp.zeros_like(l_ref)
    acc_ref[...] = jnp.zeros_like(acc_ref)

    def step(j, masked):
        start = pl.multiple_of(j * tk, tk)
        kb = k_ref[pl.ds(start, tk), :]
        vb = vT_ref[j]
        s = _dot(kb, qqT)
        if masked:
            c = lax.broadcasted_iota(jnp.int32, s.shape, 1)
            qpos = qi * tq + jnp.where(c >= tq, c - tq, c)
            kpos = start + lax.broadcasted_iota(jnp.int32, s.shape, 0)
            s = jnp.where(kpos <= qpos, s, -jnp.inf)
        m_old = m_ref[...]
        m_new = jnp.maximum(m_old, jnp.max(s, axis=0, keepdims=True))
        alpha = jnp.exp2(m_old - m_new)
        p = jnp.exp2(s - m_new)
        l_ref[...] = alpha * l_ref[...] + jnp.sum(p, axis=0, keepdims=True)
        acc_ref[...] = alpha * acc_ref[...] + _dot(vb, p.astype(BF16))
        m_ref[...] = m_new

    def body(j, c):
        step(j, False)
        return c

    lax.fori_loop(0, qi, body, 0)
    step(qi, True)

    lp = lp_ref[...]
    lam = (jnp.exp(jnp.sum(lp[0:1, :] * lp[1:2, :], axis=-1, keepdims=True))
           - jnp.exp(jnp.sum(lp[2:3, :] * lp[3:4, :], axis=-1, keepdims=True)) + lam_init)
    o = acc_ref[...] / l_ref[...]
    d = o[:, 0:tq] - lam * o[:, tq:2 * tq]
    ms = jnp.mean(d * d, axis=0, keepdims=True)
    o_ref[...] = (d * lax.rsqrt(ms + EPS) * sg_ref[...] * (1.0 - lam_init)).T.astype(BF16)


def _attention(lam_params, subln_g, qT, k, vT, *, batch, seq, lam_init):
    dq, t = qT.shape
    n_heads = dq // V_DIM
    tq, tk = TQ, TK
    assert tq == tk and vT.shape[2] == tk
    nq = seq // tq
    nk = seq // tk
    const = lambda b, h, i: (0, 0)
    return pl.pallas_call(
        functools.partial(_attn_kernel, tq=tq, tk=tk, lam_init=lam_init),
        grid=(batch, n_heads, nq),
        in_specs=[
            pl.BlockSpec(lam_params.shape, const),
            pl.BlockSpec((V_DIM, 1), const),
            pl.BlockSpec((V_DIM, tq), lambda b, h, i: (h, b * nq + i)),
            pl.BlockSpec((seq, V_DIM), lambda b, h, i: (b, h)),
            pl.BlockSpec((nk, V_DIM, tk), lambda b, h, i: (b, h, 0)),
        ],
        out_specs=pl.BlockSpec((tq, V_DIM), lambda b, h, i: (b * nq + i, h)),
        out_shape=jax.ShapeDtypeStruct((t, dq), BF16),
        scratch_shapes=[pltpu.VMEM((1, 2 * tq), F32), pltpu.VMEM((1, 2 * tq), F32),
                        pltpu.VMEM((V_DIM, 2 * tq), F32)],
        compiler_params=pltpu.CompilerParams(
            dimension_semantics=("arbitrary", "arbitrary", "arbitrary"), vmem_limit_bytes=VMEM_LIMIT),
        name="diff_attention",
    )(lam_params, subln_g, qT, k, vT)


def _outproj_router_kernel(x_ref, yc_ref, at_ref, wo_ref, g_ref, wrh_ref, wrl_ref,
                           h_ref, hn_ref, route_ref, cnt_ref, *, tm, dc):
    i = pl.program_id(0)

    @pl.when(i == 0)
    def _():
        cnt_ref[...] = jnp.zeros_like(cnt_ref)

    h = x_ref[...] + _dot(yc_ref[...], wo_ref[0:dc, :]) + _dot(at_ref[...], wo_ref[dc:2 * dc, :])
    h_ref[...] = h
    ms = jnp.mean(h * h, axis=-1, keepdims=True)
    hn = h * lax.rsqrt(ms + EPS) * g_ref[...]
    hi = hn.astype(BF16)
    hn_ref[...] = hn
    lo = (hn - hi.astype(F32)).astype(BF16)
    logits = _dot(hi, wrh_ref[...]) + _dot(lo, wrh_ref[...]) + _dot(hi, wrl_ref[...])

    lane = lax.broadcasted_iota(jnp.int32, logits.shape, 1)
    lane_f = lane.astype(F32)
    neg = -jnp.inf
    is_g = (lane >= N_EXPERTS) & (lane < N_EXPERTS + N_GROUPS)
    gl = jnp.where(is_g, logits, neg)
    gmax = jnp.max(gl, axis=-1, keepdims=True)
    g_gate = 1.0 / jnp.sum(jnp.exp(gl - gmax), axis=-1, keepdims=True)
    g_idx = jnp.min(jnp.where(gl == gmax, lane_f - N_EXPERTS, 1e9), axis=-1, keepdims=True)
    e_lo = g_idx * EXPERTS_PER_GROUP
    in_grp = (lane_f >= e_lo) & (lane_f < e_lo + EXPERTS_PER_GROUP)
    el = jnp.where(in_grp, logits, neg)
    v1 = jnp.max(el, axis=-1, keepdims=True)
    i1 = jnp.min(jnp.where(el == v1, lane_f, 1e9), axis=-1, keepdims=True)
    el2 = jnp.where(lane_f == i1, neg, el)
    v2 = jnp.max(el2, axis=-1, keepdims=True)
    i2 = jnp.min(jnp.where(el2 == v2, lane_f, 1e9), axis=-1, keepdims=True)
    tt = jnp.exp(v2 - v1)
    w1 = g_gate / (1.0 + tt)
    w2 = g_gate * tt / (1.0 + tt)

    sel1 = lane_f == i1
    sel2 = lane_f == i2
    oh = jnp.where(sel1 | sel2, 1.0, 0.0)
    rr = lax.broadcasted_iota(jnp.int32, (tm, tm), 0)
    cc = lax.broadcasted_iota(jnp.int32, (tm, tm), 1)
    lower = jnp.where(cc < rr, 1.0, 0.0).astype(BF16)
    ranks = _dot(lower, oh.astype(BF16)) + cnt_ref[0:1, :]
    r1 = jnp.sum(jnp.where(sel1, ranks, 0.0), axis=-1, keepdims=True)
    r2 = jnp.sum(jnp.where(sel2, ranks, 0.0), axis=-1, keepdims=True)
    cnt_ref[...] = cnt_ref[...] + jnp.sum(oh, axis=0, keepdims=True)

    route_ref[...] = jnp.where(
        lane == 0, i1, jnp.where(lane == 1, i2, jnp.where(lane == 2, r1, jnp.where(
            lane == 3, r2, jnp.where(lane == 4, w1, jnp.where(lane == 5, w2, 0.0))))))


def _outproj_router(x2, yc, at, w_out, g, wr_hi, wr_lo):
    t, d = x2.shape
    dc = yc.shape[1]
    tm = TM_PROJ
    row = lambda i: (i, 0)
    const = lambda i: (0, 0)
    return pl.pallas_call(
        functools.partial(_outproj_router_kernel, tm=tm, dc=dc),
        grid=(t // tm,),
        in_specs=[
            pl.BlockSpec((tm, d), row),
            pl.BlockSpec((tm, dc), row),
            pl.BlockSpec((tm, dc), row),
            pl.BlockSpec(w_out.shape, const),
            pl.BlockSpec((1, d), const),
            pl.BlockSpec(wr_hi.shape, const),
            pl.BlockSpec(wr_lo.shape, const),
        ],
        out_specs=[
            pl.BlockSpec((tm, d), row),
            pl.BlockSpec((tm, d), row),
            pl.BlockSpec((tm, LANES), row),
            pl.BlockSpec((8, LANES), const),
        ],
        out_shape=[
            jax.ShapeDtypeStruct((t, d), F32),
            jax.ShapeDtypeStruct((t, d), F32),
            jax.ShapeDtypeStruct((t, LANES), F32),
            jax.ShapeDtypeStruct((8, LANES), F32),
        ],
        compiler_params=pltpu.CompilerParams(
            dimension_semantics=("arbitrary",), vmem_limit_bytes=VMEM_LIMIT),
        name="outproj_router",
    )(x2, yc, at, w_out, g, wr_hi, wr_lo)


def _moe_kernel(te_ref, nt_ref, x_ref, wg_ref, wu_ref, wd_ref, y_ref, wgb_ref, wub_ref, wdb_ref):
    i = pl.program_id(0)
    last = nt_ref[0] - 1
    expert = te_ref[jnp.minimum(i, last)]
    prev_expert = te_ref[jnp.minimum(jnp.maximum(i - 1, 0), last)]

    @pl.when((i == 0) | (expert != prev_expert))
    def _():
        wgb_ref[...] = wg_ref[0].astype(BF16)
        wub_ref[...] = wu_ref[0].astype(BF16)
        wdb_ref[...] = wd_ref[0].astype(BF16)

    @pl.when(i <= last)
    def _():
        x = x_ref[...].astype(BF16)
        hg = _dot(x, wgb_ref[...])
        hu = _dot(x, wub_ref[...])
        act = hg * (1.0 / (1.0 + jnp.exp(-hg))) * hu
        y_ref[...] = _dot(act.astype(BF16), wdb_ref[...])


def _moe(tile_expert, n_tiles, xs, wg, wu, wd):
    p, d = xs.shape
    f = wg.shape[2]
    tm = TM_MOE
    row = lambda i, te, nt: (jnp.minimum(i, nt[0] - 1), 0)
    wsel = lambda i, te, nt: (te[jnp.minimum(i, nt[0] - 1)], 0, 0)
    grid_spec = pltpu.PrefetchScalarGridSpec(
        num_scalar_prefetch=2,
        grid=(p // tm,),
        in_specs=[
            pl.BlockSpec((tm, d), row),
            pl.BlockSpec((1, d, f), wsel),
            pl.BlockSpec((1, d, f), wsel),
            pl.BlockSpec((1, f, d), wsel),
        ],
        out_specs=pl.BlockSpec((tm, d), row),
        scratch_shapes=[pltpu.VMEM((d, f), BF16), pltpu.VMEM((d, f), BF16), pltpu.VMEM((f, d), BF16)],
    )
    return pl.pallas_call(
        _moe_kernel,
        grid_spec=grid_spec,
        out_shape=jax.ShapeDtypeStruct((p, d), F32),
        compiler_params=pltpu.CompilerParams(
            dimension_semantics=("arbitrary",), vmem_limit_bytes=VMEM_LIMIT),
        name="moe_experts",
    )(tile_expert, n_tiles, xs, wg, wu, wd)


def _sc_mesh():
    return plsc.VectorSubcoreMesh(core_axis_name="c", subcore_axis_name="s",
                                  num_cores=SC_CORES, num_subcores=SC_SUBCORES)


def _sc_dispatch(rows, pos_a, pos_b, n_out):
    t, d = rows.shape
    win = pos_a.shape[1]

    @functools.partial(pl.kernel, out_type=jax.ShapeDtypeStruct((n_out, d), rows.dtype),
                       mesh=_sc_mesh(), scratch_types=[], name="sc_dispatch")
    def run(rows_hbm, pa_hbm, pb_hbm, out_hbm):
        def body(rows_vmem, pa_vmem, pb_vmem):
            pltpu.sync_copy(rows_vmem, out_hbm.at[pa_vmem.at[0]])
            pltpu.sync_copy(rows_vmem, out_hbm.at[pb_vmem.at[0]])

        pltpu.emit_pipeline(
            body, grid=(t // win,),
            in_specs=[pl.BlockSpec((win, d), lambda i: (i, 0)),
                      pl.BlockSpec((1, win), lambda i: (i, 0)),
                      pl.BlockSpec((1, win), lambda i: (i, 0))],
            out_specs=[],
            core_axis_name=("c", "s"),
            dimension_semantics=(pltpu.PARALLEL,),
        )(rows_hbm, pa_hbm, pb_hbm)

    return run(rows, pos_a, pos_b)


def _sc_gather(table, idx):
    d = table.shape[1]
    n_win, win = idx.shape

    @functools.partial(pl.kernel, out_type=jax.ShapeDtypeStruct((n_win * win, d), table.dtype),
                       mesh=_sc_mesh(), scratch_types=[], name="sc_gather")
    def run(table_hbm, idx_hbm, out_hbm):
        def body(idx_vmem, out_vmem):
            pltpu.sync_copy(table_hbm.at[idx_vmem.at[0]], out_vmem)

        pltpu.emit_pipeline(
            body, grid=(n_win,),
            in_specs=[pl.BlockSpec((1, win), lambda i: (i, 0))],
            out_specs=[pl.BlockSpec((win, d), lambda i: (i, 0))],
            core_axis_name=("c", "s"),
            dimension_semantics=(pltpu.PARALLEL,),
        )(idx_hbm, out_hbm)

    return run(table, idx)


def _combine_kernel(h_ref, ya_ref, yb_ref, r_ref, o_ref):
    r = r_ref[...]
    o_ref[...] = h_ref[...] + r[:, 4:5] * ya_ref[...] + r[:, 5:6] * yb_ref[...]


def _combine(hres, yg, route):
    t, d = hres.shape
    tm = TM_PROJ
    nb = t // tm
    return pl.pallas_call(
        _combine_kernel,
        grid=(nb,),
        in_specs=[pl.BlockSpec((tm, d), lambda i: (i, 0)),
                  pl.BlockSpec((tm, d), lambda i: (i, 0)),
                  pl.BlockSpec((tm, d), lambda i: (i + nb, 0)),
                  pl.BlockSpec((tm, LANES), lambda i: (i, 0))],
        out_specs=pl.BlockSpec((tm, d), lambda i: (i, 0)),
        out_shape=jax.ShapeDtypeStruct((t, d), F32),
        compiler_params=pltpu.CompilerParams(
            dimension_semantics=("arbitrary",), vmem_limit_bytes=VMEM_LIMIT),
        name="combine",
    )(hres, yg, yg, route)


def _lambda_init(layer_idx):
    return 0.8 - 0.6 * math.exp(-0.3 * layer_idx)


def _layer(h, l, attn_norm_g, w_in, conv_w, conv_out_g, q_norm_g, k_norm_g,
           lambda_q1, lambda_k1, lambda_q2, lambda_k2, attn_subln_g, w_out,
           ffn_norm_g, w_router_group, w_router_expert, w_exp_gate, w_exp_up, w_exp_down):
    batch, seq, d = h.shape
    t = batch * seq
    dc = conv_w.shape[-1]
    lam_init = _lambda_init(l)
    x2 = h.reshape(t, d)

    reps = dc // HEAD_DIM
    assert dc // CONV_GROUPS == HEAD_DIM
    qg = (jnp.tile(q_norm_g[l], reps) * (HEAD_DIM ** -0.5 * math.log2(math.e))).reshape(1, dc)
    kg = jnp.tile(k_norm_g[l], reps).reshape(1, dc)
    grp = jnp.arange(dc) // HEAD_DIM
    gmat = jnp.where(grp[:, None] == grp[None, :], 1.0 / HEAD_DIM, 0.0).astype(BF16)
    yc, qT, k, vT = _inproj(x2, attn_norm_g[l].reshape(1, d), w_in[l].astype(BF16), conv_w[l],
                          conv_out_g[l].reshape(1, dc), qg, kg, gmat, batch=batch, seq=seq)

    lam_params = jnp.stack([lambda_q1[l], lambda_k1[l], lambda_q2[l], lambda_k2[l]])
    at = _attention(lam_params, attn_subln_g[l].reshape(V_DIM, 1), qT, k, vT,
                    batch=batch, seq=seq, lam_init=lam_init)

    wr = jnp.concatenate([w_router_expert[l], w_router_group[l],
                          jnp.zeros((d, LANES - N_EXPERTS - N_GROUPS), F32)], axis=1)
    wr_hi = wr.astype(BF16)
    wr_lo = (wr - wr_hi.astype(F32)).astype(BF16)
    hres, hn2, route, cnt = _outproj_router(x2, yc, at, w_out[l].astype(BF16),
                                            ffn_norm_g[l].reshape(1, d), wr_hi, wr_lo)

    tmm = TM_MOE
    n_tiles_max = (2 * t) // tmm + N_EXPERTS
    p_rows = n_tiles_max * tmm
    e1 = route[:, 0].astype(jnp.int32)
    e2 = route[:, 1].astype(jnp.int32)
    counts = cnt[0, :N_EXPERTS].astype(jnp.int32)
    tiles = (counts + tmm - 1) // tmm
    tile_end = jnp.cumsum(tiles)
    offs = (tile_end - tiles) * tmm
    pos1 = offs[e1] + route[:, 2].astype(jnp.int32)
    pos2 = offs[e2] + route[:, 3].astype(jnp.int32)
    n_tiles = tile_end[-1:].astype(jnp.int32)
    tile_ids = jnp.arange(n_tiles_max, dtype=jnp.int32)
    tile_expert = jnp.minimum(
        jnp.sum((tile_end[None, :] <= tile_ids[:, None]).astype(jnp.int32), axis=1), N_EXPERTS - 1)

    pos1w = pos1.reshape(t // SC_WIN, SC_WIN)
    pos2w = pos2.reshape(t // SC_WIN, SC_WIN)
    xs = _sc_dispatch(hn2, pos1w, pos2w, p_rows)

    f = w_exp_gate.shape[-1]
    ys = _moe(tile_expert, n_tiles, xs,
              w_exp_gate[l].reshape(N_EXPERTS, d, f),
              w_exp_up[l].reshape(N_EXPERTS, d, f),
              w_exp_down[l].reshape(N_EXPERTS, f, d))
    yg = _sc_gather(ys, jnp.concatenate([pos1w, pos2w], axis=0))
    out = _combine(hres, yg, route)
    return out.reshape(batch, seq, d)


def kernel(x, attn_norm_g, w_in, conv_w, conv_out_g, q_norm_g, k_norm_g, lambda_q1, lambda_k1,
           lambda_q2, lambda_k2, attn_subln_g, w_out, ffn_norm_g, w_router_group, w_router_expert,
           w_exp_gate, w_exp_up, w_exp_down):
    h = x
    for l in range(attn_norm_g.shape[0]):
        h = _layer(h, l, attn_norm_g, w_in, conv_w, conv_out_g, q_norm_g, k_norm_g,
                   lambda_q1, lambda_k1, lambda_q2, lambda_k2, attn_subln_g, w_out,
                   ffn_norm_g, w_router_group, w_router_expert, w_exp_gate, w_exp_up, w_exp_down)
    return h
```

```python
import functools
import math

import jax
import jax.numpy as jnp
from jax import lax
from jax.experimental import pallas as pl
from jax.experimental.pallas import tpu as pltpu
from jax.experimental.pallas import tpu_sc as plsc

F32 = jnp.float32
BF16 = jnp.bfloat16

HEAD_DIM = 64
V_DIM = 2 * HEAD_DIM
CONV_GROUPS = 8
N_GROUPS = 4
EXPERTS_PER_GROUP = 8
N_EXPERTS = N_GROUPS * EXPERTS_PER_GROUP
EPS = 1e-6
LANES = 128
VMEM_LIMIT = 48 * 1024 * 1024

TM_PROJ = 256
TQ = 512
TK = 512
TM_MOE = 256
SC_CORES = 2
SC_SUBCORES = 16
SC_WIN = 32


def _dot(a, b):
    return jnp.dot(a, b, preferred_element_type=F32)


def _split_dot(a_f32, b_bf16):
    hi = a_f32.astype(BF16)
    lo = (a_f32 - hi.astype(F32)).astype(BF16)
    return _dot(hi, b_bf16) + _dot(lo, b_bf16)


def _inproj_kernel(x_ref, g_ref, w_ref, cw_ref, cg_ref, qg_ref, kg_ref, gm_ref,
                   yc_ref, qT_ref, k_ref, vT_ref, carry_ref, *, tm, dc):
    j = pl.program_id(1)

    @pl.when(j == 0)
    def _():
        carry_ref[...] = jnp.zeros_like(carry_ref)

    x = x_ref[...]
    ms = jnp.mean(x * x, axis=-1, keepdims=True)
    hn = (x * lax.rsqrt(ms + EPS) * g_ref[...]).astype(BF16)

    def proj(s):
        return _dot(hn, w_ref[:, s * dc:(s + 1) * dc])

    gm = gm_ref[...]

    u = proj(2) * proj(0)
    prev = carry_ref[...]
    rows = lax.broadcasted_iota(jnp.int32, u.shape, 0)
    u1 = jnp.where(rows == 0, prev[7:8, :], pltpu.roll(u, 1, 0))
    u2 = jnp.where(rows == 0, prev[6:7, :], jnp.where(rows == 1, prev[7:8, :], pltpu.roll(u, 2, 0)))
    carry_ref[...] = u[tm - 8:tm, :]
    cw = cw_ref[...]
    y = proj(1) * (cw[0:1, :] * u2 + cw[1:2, :] * u1 + cw[2:3, :] * u)
    yc_ref[...] = (y * lax.rsqrt(_split_dot(y * y, gm) + EPS) * cg_ref[...]).astype(BF16)

    q = proj(3)
    qT_ref[...] = (q * lax.rsqrt(_split_dot(q * q, gm) + EPS) * qg_ref[...]).T.astype(BF16)
    k = proj(4)
    k_ref[...] = (k * lax.rsqrt(_split_dot(k * k, gm) + EPS) * kg_ref[...]).astype(BF16)
    vT_ref[0] = proj(5).T.astype(BF16)


def _inproj(x2, g, w_in, conv_w, conv_g, qg, kg, gmat, *, batch, seq):
    t, d = x2.shape
    dc = gmat.shape[0]
    tm = TK
    nj = seq // tm
    row = lambda b, j: (b * nj + j, 0)
    const = lambda b, j: (0, 0)
    out_sds = jax.ShapeDtypeStruct((t, dc), BF16)
    return pl.pallas_call(
        functools.partial(_inproj_kernel, tm=tm, dc=dc),
        grid=(batch, nj),
        in_specs=[
            pl.BlockSpec((tm, d), row),
            pl.BlockSpec((1, d), const),
            pl.BlockSpec(w_in.shape, const),
            pl.BlockSpec(conv_w.shape, const),
            pl.BlockSpec((1, dc), const),
            pl.BlockSpec((1, dc), const),
            pl.BlockSpec((1, dc), const),
            pl.BlockSpec(gmat.shape, const),
        ],
        out_specs=[
            pl.BlockSpec((tm, dc), row),
            pl.BlockSpec((dc, tm), lambda b, j: (0, b * nj + j)),
            pl.BlockSpec((tm, dc), row),
            pl.BlockSpec((1, dc, tm), lambda b, j: (b * nj + j, 0, 0)),
        ],
        out_shape=[out_sds, jax.ShapeDtypeStruct((dc, t), BF16), out_sds,
                   jax.ShapeDtypeStruct((t // tm, dc, tm), BF16)],
        scratch_shapes=[pltpu.VMEM((8, dc), F32)],
        compiler_params=pltpu.CompilerParams(
            dimension_semantics=("arbitrary", "arbitrary"), vmem_limit_bytes=VMEM_LIMIT),
        name="inproj_conv_qknorm",
    )(x2, g, w_in, conv_w, conv_g, qg, kg, gmat)


def _attn_kernel(lp_ref, sg_ref, qT_ref, k_ref, vT_ref, o_ref, m_ref, l_ref, acc_ref,
                 *, tq, tk, lam_init):
    qi = pl.program_id(2)
    qT = qT_ref[...]
    row = lax.broadcasted_iota(jnp.int32, qT.shape, 0)
    zero = jnp.zeros_like(qT)
    qqT = jnp.concatenate([jnp.where(row < HEAD_DIM, qT, zero),
                           jnp.where(row >= HEAD_DIM, qT, zero)], axis=1)

    m_ref[...] = jnp.full_like(m_ref, -jnp.inf)
    l_ref[...] = jnp.zeros_like(l_ref)
    acc_ref[...] = jnp.zeros_like(acc_ref)

    def step(j, masked):
        start = pl.multiple_of(j * tk, tk)
        kb = k_ref[pl.ds(start, tk), :]
        vb = vT_ref[j]
        s = _dot(kb, qqT)
        if masked:
            c = lax.broadcasted_iota(jnp.int32, s.shape, 1)
            qpos = qi * tq + jnp.where(c >= tq, c - tq, c)
            kpos = start + lax.broadcasted_iota(jnp.int32, s.shape, 0)
            s = jnp.where(kpos <= qpos, s, -jnp.inf)
        m_old = m_ref[...]
        m_new = jnp.maximum(m_old, jnp.max(s, axis=0, keepdims=True))
        alpha = jnp.exp2(m_old - m_new)
        p = jnp.exp2(s - m_new)
        l_ref[...] = alpha * l_ref[...] + jnp.sum(p, axis=0, keepdims=True)
        acc_ref[...] = alpha * acc_ref[...] + _dot(vb, p.astype(BF16))
        m_ref[...] = m_new

    def body(j, c):
        step(j, False)
        return c

    lax.fori_loop(0, qi, body, 0)
    step(qi, True)

    lp = lp_ref[...]
    lam = (jnp.exp(jnp.sum(lp[0:1, :] * lp[1:2, :], axis=-1, keepdims=True))
           - jnp.exp(jnp.sum(lp[2:3, :] * lp[3:4, :], axis=-1, keepdims=True)) + lam_init)
    o = acc_ref[...] / l_ref[...]
    d = o[:, 0:tq] - lam * o[:, tq:2 * tq]
    ms = jnp.mean(d * d, axis=0, keepdims=True)
    o_ref[...] = (d * lax.rsqrt(ms + EPS) * sg_ref[...] * (1.0 - lam_init)).T.astype(BF16)


def _attention(lam_params, subln_g, qT, k, vT, *, batch, seq, lam_init):
    dq, t = qT.shape
    n_heads = dq // V_DIM
    tq, tk = TQ, TK
    assert tq == tk and vT.shape[2] == tk
    nq = seq // tq
    nk = seq // tk
    const = lambda b, h, i: (0, 0)
    return pl.pallas_call(
        functools.partial(_attn_kernel, tq=tq, tk=tk, lam_init=lam_init),
        grid=(batch, n_heads, nq),
        in_specs=[
            pl.BlockSpec(lam_params.shape, const),
            pl.BlockSpec((V_DIM, 1), const),
            pl.BlockSpec((V_DIM, tq), lambda b, h, i: (h, b * nq + i)),
            pl.BlockSpec((seq, V_DIM), lambda b, h, i: (b, h)),
            pl.BlockSpec((nk, V_DIM, tk), lambda b, h, i: (b, h, 0)),
        ],
        out_specs=pl.BlockSpec((tq, V_DIM), lambda b, h, i: (b * nq + i, h)),
        out_shape=jax.ShapeDtypeStruct((t, dq), BF16),
        scratch_shapes=[pltpu.VMEM((1, 2 * tq), F32), pltpu.VMEM((1, 2 * tq), F32),
                        pltpu.VMEM((V_DIM, 2 * tq), F32)],
        compiler_params=pltpu.CompilerParams(
            dimension_semantics=("arbitrary", "arbitrary", "arbitrary"), vmem_limit_bytes=VMEM_LIMIT),
        name="diff_attention",
    )(lam_params, subln_g, qT, k, vT)


def _outproj_router_kernel(x_ref, yc_ref, at_ref, wo_ref, g_ref, wrh_ref, wrl_ref,
                           h_ref, hn_ref, route_ref, routeT_ref, cnt_ref, *, tm, dc):
    i = pl.program_id(0)

    @pl.when(i == 0)
    def _():
        cnt_ref[...] = jnp.zeros_like(cnt_ref)

    h = x_ref[...] + _dot(yc_ref[...], wo_ref[0:dc, :]) + _dot(at_ref[...], wo_ref[dc:2 * dc, :])
    h_ref[...] = h
    ms = jnp.mean(h * h, axis=-1, keepdims=True)
    hn = h * lax.rsqrt(ms + EPS) * g_ref[...]
    hi = hn.astype(BF16)
    hn_ref[...] = hn
    lo = (hn - hi.astype(F32)).astype(BF16)
    logits = _dot(hi, wrh_ref[...]) + _dot(lo, wrh_ref[...]) + _dot(hi, wrl_ref[...])

    lane = lax.broadcasted_iota(jnp.int32, logits.shape, 1)
    lane_f = lane.astype(F32)
    neg = -jnp.inf
    is_g = (lane >= N_EXPERTS) & (lane < N_EXPERTS + N_GROUPS)
    gl = jnp.where(is_g, logits, neg)
    gmax = jnp.max(gl, axis=-1, keepdims=True)
    g_gate = 1.0 / jnp.sum(jnp.exp(gl - gmax), axis=-1, keepdims=True)
    g_idx = jnp.min(jnp.where(gl == gmax, lane_f - N_EXPERTS, 1e9), axis=-1, keepdims=True)
    e_lo = g_idx * EXPERTS_PER_GROUP
    in_grp = (lane_f >= e_lo) & (lane_f < e_lo + EXPERTS_PER_GROUP)
    el = jnp.where(in_grp, logits, neg)
    v1 = jnp.max(el, axis=-1, keepdims=True)
    i1 = jnp.min(jnp.where(el == v1, lane_f, 1e9), axis=-1, keepdims=True)
    el2 = jnp.where(lane_f == i1, neg, el)
    v2 = jnp.max(el2, axis=-1, keepdims=True)
    i2 = jnp.min(jnp.where(el2 == v2, lane_f, 1e9), axis=-1, keepdims=True)
    tt = jnp.exp(v2 - v1)
    w1 = g_gate / (1.0 + tt)
    w2 = g_gate * tt / (1.0 + tt)

    sel1 = lane_f == i1
    sel2 = lane_f == i2
    oh = jnp.where(sel1 | sel2, 1.0, 0.0)
    rr = lax.broadcasted_iota(jnp.int32, (tm, tm), 0)
    cc = lax.broadcasted_iota(jnp.int32, (tm, tm), 1)
    lower = jnp.where(cc < rr, 1.0, 0.0).astype(BF16)
    ranks = _dot(lower, oh.astype(BF16)) + cnt_ref[0:1, :]
    r1 = jnp.sum(jnp.where(sel1, ranks, 0.0), axis=-1, keepdims=True)
    r2 = jnp.sum(jnp.where(sel2, ranks, 0.0), axis=-1, keepdims=True)
    cnt_ref[...] = cnt_ref[...] + jnp.sum(oh, axis=0, keepdims=True)

    route = jnp.where(
        lane == 0, i1, jnp.where(lane == 1, i2, jnp.where(lane == 2, r1, jnp.where(
            lane == 3, r2, jnp.where(lane == 4, w1, jnp.where(lane == 5, w2, 0.0))))))
    route_ref[...] = route
    routeT_ref[...] = route.T[0:8, :]


def _outproj_router(x2, yc, at, w_out, g, wr_hi, wr_lo):
    t, d = x2.shape
    dc = yc.shape[1]
    tm = TM_PROJ
    row = lambda i: (i, 0)
    const = lambda i: (0, 0)
    return pl.pallas_call(
        functools.partial(_outproj_router_kernel, tm=tm, dc=dc),
        grid=(t // tm,),
        in_specs=[
            pl.BlockSpec((tm, d), row),
            pl.BlockSpec((tm, dc), row),
            pl.BlockSpec((tm, dc), row),
            pl.BlockSpec(w_out.shape, const),
            pl.BlockSpec((1, d), const),
            pl.BlockSpec(wr_hi.shape, const),
            pl.BlockSpec(wr_lo.shape, const),
        ],
        out_specs=[
            pl.BlockSpec((tm, d), row),
            pl.BlockSpec((tm, d), row),
            pl.BlockSpec((tm, LANES), row),
            pl.BlockSpec((8, tm), lambda i: (0, i)),
            pl.BlockSpec((8, LANES), const),
        ],
        out_shape=[
            jax.ShapeDtypeStruct((t, d), F32),
            jax.ShapeDtypeStruct((t, d), F32),
            jax.ShapeDtypeStruct((t, LANES), F32),
            jax.ShapeDtypeStruct((8, t), F32),
            jax.ShapeDtypeStruct((8, LANES), F32),
        ],
        compiler_params=pltpu.CompilerParams(
            dimension_semantics=("arbitrary",), vmem_limit_bytes=VMEM_LIMIT),
        name="outproj_router",
    )(x2, yc, at, w_out, g, wr_hi, wr_lo)


def _positions_kernel(offs_ref, rt_ref, pos_ref):
    rt = rt_ref[...]
    ea, eb = rt[0:1, :], rt[1:2, :]
    sa = jnp.zeros_like(ea)
    sb = jnp.zeros_like(eb)
    for e in range(N_EXPERTS):
        start = offs_ref[e].astype(F32)
        sa = jnp.where(ea == e, start, sa)
        sb = jnp.where(eb == e, start, sb)
    pos_ref[0:1, :] = (sa + rt[2:3, :]).astype(jnp.int32)
    pos_ref[1:2, :] = (sb + rt[3:4, :]).astype(jnp.int32)


def _positions(offs, routeT):
    t = routeT.shape[1]
    return pl.pallas_call(
        _positions_kernel,
        grid_spec=pltpu.PrefetchScalarGridSpec(
            num_scalar_prefetch=1, grid=(1,),
            in_specs=[pl.BlockSpec(routeT.shape, lambda i, offs: (0, 0))],
            out_specs=pl.BlockSpec((2, t), lambda i, offs: (0, 0)),
        ),
        out_shape=jax.ShapeDtypeStruct((2, t), jnp.int32),
        name="positions",
    )(offs, routeT)


def _moe_kernel(te_ref, nt_ref, x_ref, wg_ref, wu_ref, wd_ref, y_ref, wgb_ref, wub_ref, wdb_ref):
    i = pl.program_id(0)
    last = nt_ref[0] - 1
    expert = te_ref[jnp.minimum(i, last)]
    prev_expert = te_ref[jnp.minimum(jnp.maximum(i - 1, 0), last)]

    @pl.when((i == 0) | (expert != prev_expert))
    def _():
        wgb_ref[...] = wg_ref[0].astype(BF16)
        wub_ref[...] = wu_ref[0].astype(BF16)
        wdb_ref[...] = wd_ref[0].astype(BF16)

    @pl.when(i <= last)
    def _():
        x = x_ref[...].astype(BF16)
        hg = _dot(x, wgb_ref[...])
        hu = _dot(x, wub_ref[...])
        act = hg * (1.0 / (1.0 + jnp.exp(-hg))) * hu
        y_ref[...] = _dot(act.astype(BF16), wdb_ref[...])


def _moe(tile_expert, n_tiles, xs, wg, wu, wd):
    p, d = xs.shape
    f = wg.shape[2]
    tm = TM_MOE
    row = lambda i, te, nt: (jnp.minimum(i, nt[0] - 1), 0)
    wsel = lambda i, te, nt: (te[jnp.minimum(i, nt[0] - 1)], 0, 0)
    grid_spec = pltpu.PrefetchScalarGridSpec(
        num_scalar_prefetch=2,
        grid=(p // tm,),
        in_specs=[
            pl.BlockSpec((tm, d), row),
            pl.BlockSpec((1, d, f), wsel),
            pl.BlockSpec((1, d, f), wsel),
            pl.BlockSpec((1, f, d), wsel),
        ],
        out_specs=pl.BlockSpec((tm, d), row),
        scratch_shapes=[pltpu.VMEM((d, f), BF16), pltpu.VMEM((d, f), BF16), pltpu.VMEM((f, d), BF16)],
    )
    return pl.pallas_call(
        _moe_kernel,
        grid_spec=grid_spec,
        out_shape=jax.ShapeDtypeStruct((p, d), F32),
        compiler_params=pltpu.CompilerParams(
            dimension_semantics=("arbitrary",), vmem_limit_bytes=VMEM_LIMIT),
        name="moe_experts",
    )(tile_expert, n_tiles, xs, wg, wu, wd)


def _sc_mesh():
    return plsc.VectorSubcoreMesh(core_axis_name="c", subcore_axis_name="s",
                                  num_cores=SC_CORES, num_subcores=SC_SUBCORES)


def _sc_dispatch(rows, pos_a, pos_b, n_out):
    t, d = rows.shape
    win = pos_a.shape[1]

    @functools.partial(pl.kernel, out_type=jax.ShapeDtypeStruct((n_out, d), rows.dtype),
                       mesh=_sc_mesh(), scratch_types=[], name="sc_dispatch")
    def run(rows_hbm, pa_hbm, pb_hbm, out_hbm):
        def body(rows_vmem, pa_vmem, pb_vmem):
            pltpu.sync_copy(rows_vmem, out_hbm.at[pa_vmem.at[0]])
            pltpu.sync_copy(rows_vmem, out_hbm.at[pb_vmem.at[0]])

        pltpu.emit_pipeline(
            body, grid=(t // win,),
            in_specs=[pl.BlockSpec((win, d), lambda i: (i, 0)),
                      pl.BlockSpec((1, win), lambda i: (i, 0)),
                      pl.BlockSpec((1, win), lambda i: (i, 0))],
            out_specs=[],
            core_axis_name=("c", "s"),
            dimension_semantics=(pltpu.PARALLEL,),
        )(rows_hbm, pa_hbm, pb_hbm)

    return run(rows, pos_a, pos_b)


def _sc_gather(table, idx):
    d = table.shape[1]
    n_win, win = idx.shape

    @functools.partial(pl.kernel, out_type=jax.ShapeDtypeStruct((n_win * win, d), table.dtype),
                       mesh=_sc_mesh(), scratch_types=[], name="sc_gather")
    def run(table_hbm, idx_hbm, out_hbm):
        def body(idx_vmem, out_vmem):
            pltpu.sync_copy(table_hbm.at[idx_vmem.at[0]], out_vmem)

        pltpu.emit_pipeline(
            body, grid=(n_win,),
            in_specs=[pl.BlockSpec((1, win), lambda i: (i, 0))],
            out_specs=[pl.BlockSpec((win, d), lambda i: (i, 0))],
            core_axis_name=("c", "s"),
            dimension_semantics=(pltpu.PARALLEL,),
        )(idx_hbm, out_hbm)

    return run(table, idx)


def _combine_kernel(h_ref, ya_ref, yb_ref, r_ref, o_ref):
    r = r_ref[...]
    o_ref[...] = h_ref[...] + r[:, 4:5] * ya_ref[...] + r[:, 5:6] * yb_ref[...]


def _combine(hres, yg, route):
    t, d = hres.shape
    tm = TM_PROJ
    nb = t // tm
    return pl.pallas_call(
        _combine_kernel,
        grid=(nb,),
        in_specs=[pl.BlockSpec((tm, d), lambda i: (i, 0)),
                  pl.BlockSpec((tm, d), lambda i: (i, 0)),
                  pl.BlockSpec((tm, d), lambda i: (i + nb, 0)),
                  pl.BlockSpec((tm, LANES), lambda i: (i, 0))],
        out_specs=pl.BlockSpec((tm, d), lambda i: (i, 0)),
        out_shape=jax.ShapeDtypeStruct((t, d), F32),
        compiler_params=pltpu.CompilerParams(
            dimension_semantics=("arbitrary",), vmem_limit_bytes=VMEM_LIMIT),
        name="combine",
    )(hres, yg, yg, route)


def _lambda_init(layer_idx):
    return 0.8 - 0.6 * math.exp(-0.3 * layer_idx)


def _layer(h, l, attn_norm_g, w_in, conv_w, conv_out_g, q_norm_g, k_norm_g,
           lambda_q1, lambda_k1, lambda_q2, lambda_k2, attn_subln_g, w_out,
           ffn_norm_g, w_router_group, w_router_expert, w_exp_gate, w_exp_up, w_exp_down):
    batch, seq, d = h.shape
    t = batch * seq
    dc = conv_w.shape[-1]
    lam_init = _lambda_init(l)
    x2 = h.reshape(t, d)

    reps = dc // HEAD_DIM
    assert dc // CONV_GROUPS == HEAD_DIM
    qg = (jnp.tile(q_norm_g[l], reps) * (HEAD_DIM ** -0.5 * math.log2(math.e))).reshape(1, dc)
    kg = jnp.tile(k_norm_g[l], reps).reshape(1, dc)
    grp = jnp.arange(dc) // HEAD_DIM
    gmat = jnp.where(grp[:, None] == grp[None, :], 1.0 / HEAD_DIM, 0.0).astype(BF16)
    yc, qT, k, vT = _inproj(x2, attn_norm_g[l].reshape(1, d), w_in[l].astype(BF16), conv_w[l],
                          conv_out_g[l].reshape(1, dc), qg, kg, gmat, batch=batch, seq=seq)

    lam_params = jnp.stack([lambda_q1[l], lambda_k1[l], lambda_q2[l], lambda_k2[l]])
    at = _attention(lam_params, attn_subln_g[l].reshape(V_DIM, 1), qT, k, vT,
                    batch=batch, seq=seq, lam_init=lam_init)

    wr = jnp.concatenate([w_router_expert[l], w_router_group[l],
                          jnp.zeros((d, LANES - N_EXPERTS - N_GROUPS), F32)], axis=1)
    wr_hi = wr.astype(BF16)
    wr_lo = (wr - wr_hi.astype(F32)).astype(BF16)
    hres, hn2, route, routeT, cnt = _outproj_router(x2, yc, at, w_out[l].astype(BF16),
                                                    ffn_norm_g[l].reshape(1, d), wr_hi, wr_lo)

    tmm = TM_MOE
    n_tiles_max = (2 * t) // tmm + N_EXPERTS
    p_rows = n_tiles_max * tmm
    counts = cnt[0, :N_EXPERTS].astype(jnp.int32)
    tiles = (counts + tmm - 1) // tmm
    tile_end = jnp.cumsum(tiles)
    offs = (tile_end - tiles) * tmm
    pos = _positions(offs, routeT)
    n_tiles = tile_end[-1:].astype(jnp.int32)
    tile_ids = jnp.arange(n_tiles_max, dtype=jnp.int32)
    tile_expert = jnp.minimum(
        jnp.sum((tile_end[None, :] <= tile_ids[:, None]).astype(jnp.int32), axis=1), N_EXPERTS - 1)

    posw = pos.reshape(2 * t // SC_WIN, SC_WIN)
    pos1w = posw[:t // SC_WIN]
    pos2w = posw[t // SC_WIN:]
    xs = _sc_dispatch(hn2, pos1w, pos2w, p_rows)

    f = w_exp_gate.shape[-1]
    ys = _moe(tile_expert, n_tiles, xs,
              w_exp_gate[l].reshape(N_EXPERTS, d, f),
              w_exp_up[l].reshape(N_EXPERTS, d, f),
              w_exp_down[l].reshape(N_EXPERTS, f, d))
    yg = _sc_gather(ys, posw)
    out = _combine(hres, yg, route)
    return out.reshape(batch, seq, d)


def kernel(x, attn_norm_g, w_in, conv_w, conv_out_g, q_norm_g, k_norm_g, lambda_q1, lambda_k1,
           lambda_q2, lambda_k2, attn_subln_g, w_out, ffn_norm_g, w_router_group, w_router_expert,
           w_exp_gate, w_exp_up, w_exp_down):
    h = x
    for l in range(attn_norm_g.shape[0]):
        h = _layer(h, l, attn_norm_g, w_in, conv_w, conv_out_g, q_norm_g, k_norm_g,
                   lambda_q1, lambda_k1, lambda_q2, lambda_k2, attn_subln_g, w_out,
                   ffn_norm_g, w_router_group, w_router_expert, w_exp_gate, w_exp_up, w_exp_down)
    return h
```

```python
import functools
import math

import jax
import jax.numpy as jnp
from jax import lax
from jax.experimental import pallas as pl
from jax.experimental.pallas import tpu as pltpu
from jax.experimental.pallas import tpu_sc as plsc

F32 = jnp.float32
BF16 = jnp.bfloat16

HEAD_DIM = 64
V_DIM = 2 * HEAD_DIM
CONV_GROUPS = 8
N_GROUPS = 4
EXPERTS_PER_GROUP = 8
N_EXPERTS = N_GROUPS * EXPERTS_PER_GROUP
EPS = 1e-6
LANES = 128
VMEM_LIMIT = 48 * 1024 * 1024

TM_PROJ = 256
TQ = 512
TK = 512
TM_MOE = 256
SC_CORES = 2
SC_SUBCORES = 16
SC_WIN = 32


def _dot(a, b):
    return jnp.dot(a, b, preferred_element_type=F32)


def _split_dot(a_f32, b_bf16):
    hi = a_f32.astype(BF16)
    lo = (a_f32 - hi.astype(F32)).astype(BF16)
    return _dot(hi, b_bf16) + _dot(lo, b_bf16)


def _inproj_kernel(x_ref, g_ref, w_ref, cw_ref, cg_ref, qg_ref, kg_ref, gm_ref,
                   yc_ref, qT_ref, k_ref, vT_ref, carry_ref, *, tm, dc):
    j = pl.program_id(1)

    @pl.when(j == 0)
    def _():
        carry_ref[...] = jnp.zeros_like(carry_ref)

    x = x_ref[...]
    ms = jnp.mean(x * x, axis=-1, keepdims=True)
    hn = (x * lax.rsqrt(ms + EPS) * g_ref[...]).astype(BF16)

    def proj(s):
        return _dot(hn, w_ref[:, s * dc:(s + 1) * dc])

    gm = gm_ref[...]

    u = proj(2) * proj(0)
    prev = carry_ref[...]
    rows = lax.broadcasted_iota(jnp.int32, u.shape, 0)
    u1 = jnp.where(rows == 0, prev[7:8, :], pltpu.roll(u, 1, 0))
    u2 = jnp.where(rows == 0, prev[6:7, :], jnp.where(rows == 1, prev[7:8, :], pltpu.roll(u, 2, 0)))
    carry_ref[...] = u[tm - 8:tm, :]
    cw = cw_ref[...]
    y = proj(1) * (cw[0:1, :] * u2 + cw[1:2, :] * u1 + cw[2:3, :] * u)
    yc_ref[...] = (y * lax.rsqrt(_split_dot(y * y, gm) + EPS) * cg_ref[...]).astype(BF16)

    q = proj(3)
    qT_ref[...] = (q * lax.rsqrt(_split_dot(q * q, gm) + EPS) * qg_ref[...]).T.astype(BF16)
    k = proj(4)
    k_ref[...] = (k * lax.rsqrt(_split_dot(k * k, gm) + EPS) * kg_ref[...]).astype(BF16)
    vT_ref[0] = proj(5).T.astype(BF16)


def _inproj(x2, g, w_in, conv_w, conv_g, qg, kg, gmat, *, batch, seq):
    t, d = x2.shape
    dc = gmat.shape[0]
    tm = TK
    nj = seq // tm
    row = lambda b, j: (b * nj + j, 0)
    const = lambda b, j: (0, 0)
    out_sds = jax.ShapeDtypeStruct((t, dc), BF16)
    return pl.pallas_call(
        functools.partial(_inproj_kernel, tm=tm, dc=dc),
        grid=(batch, nj),
        in_specs=[
            pl.BlockSpec((tm, d), row),
            pl.BlockSpec((1, d), const),
            pl.BlockSpec(w_in.shape, const),
            pl.BlockSpec(conv_w.shape, const),
            pl.BlockSpec((1, dc), const),
            pl.BlockSpec((1, dc), const),
            pl.BlockSpec((1, dc), const),
            pl.BlockSpec(gmat.shape, const),
        ],
        out_specs=[
            pl.BlockSpec((tm, dc), row),
            pl.BlockSpec((dc, tm), lambda b, j: (0, b * nj + j)),
            pl.BlockSpec((tm, dc), row),
            pl.BlockSpec((1, dc, tm), lambda b, j: (b * nj + j, 0, 0)),
        ],
        out_shape=[out_sds, jax.ShapeDtypeStruct((dc, t), BF16), out_sds,
                   jax.ShapeDtypeStruct((t // tm, dc, tm), BF16)],
        scratch_shapes=[pltpu.VMEM((8, dc), F32)],
        compiler_params=pltpu.CompilerParams(
            dimension_semantics=("arbitrary", "arbitrary"), vmem_limit_bytes=VMEM_LIMIT),
        name="inproj_conv_qknorm",
    )(x2, g, w_in, conv_w, conv_g, qg, kg, gmat)


def _attn_kernel(lp_ref, sg_ref, qT_ref, k_ref, vT_ref, o_ref, m_ref, l_ref, acc_ref,
                 sa_ref, pb_ref, ab_ref, *, tq, tk, lam_init):
    qi = pl.program_id(2)
    qT = qT_ref[...]
    row = lax.broadcasted_iota(jnp.int32, qT.shape, 0)
    zero = jnp.zeros_like(qT)
    qqT = jnp.concatenate([jnp.where(row < HEAD_DIM, qT, zero),
                           jnp.where(row >= HEAD_DIM, qT, zero)], axis=1)

    m_ref[...] = jnp.full_like(m_ref, -jnp.inf)
    l_ref[...] = jnp.zeros_like(l_ref)
    acc_ref[...] = jnp.zeros_like(acc_ref)

    map_a = slice(0, tq)
    map_b = slice(tq, 2 * tq)

    def key_block(j):
        return k_ref[pl.ds(pl.multiple_of(j * tk, tk), tk), :]

    def softmax(cols, s, masked):
        if masked:
            qrel = lax.broadcasted_iota(jnp.int32, s.shape, 1)
            krel = lax.broadcasted_iota(jnp.int32, s.shape, 0)
            s = jnp.where(krel <= qrel, s, -jnp.inf)
        m_old = m_ref[:, cols]
        m_new = jnp.maximum(m_old, jnp.max(s, axis=0, keepdims=True))
        alpha = jnp.exp2(m_old - m_new)
        p = jnp.exp2(s - m_new)
        l_ref[:, cols] = alpha * l_ref[:, cols] + jnp.sum(p, axis=0, keepdims=True)
        m_ref[:, cols] = m_new
        return alpha, p.astype(BF16)

    def accumulate(cols, alpha, vb, p):
        acc_ref[:, cols] = alpha * acc_ref[:, cols] + _dot(vb, p)

    def block(j, last):
        accumulate(map_b, ab_ref[...], vT_ref[jnp.maximum(j - 1, 0)], pb_ref[...])
        s_b = _dot(key_block(j), qqT[:, map_b])
        alpha_a, p_a = softmax(map_a, sa_ref[...], last)
        accumulate(map_a, alpha_a, vT_ref[j], p_a)
        if not last:
            sa_ref[...] = _dot(key_block(j + 1), qqT[:, map_a])
        alpha_b, p_b = softmax(map_b, s_b, last)
        if last:
            accumulate(map_b, alpha_b, vT_ref[j], p_b)
        else:
            ab_ref[...] = alpha_b
            pb_ref[...] = p_b

    sa_ref[...] = _dot(key_block(0), qqT[:, map_a])
    pb_ref[...] = jnp.zeros_like(pb_ref)
    ab_ref[...] = jnp.ones_like(ab_ref)

    def body(j, c):
        block(j, False)
        return c

    lax.fori_loop(0, qi, body, 0)
    block(qi, True)

    lp = lp_ref[...]
    lam = (jnp.exp(jnp.sum(lp[0:1, :] * lp[1:2, :], axis=-1, keepdims=True))
           - jnp.exp(jnp.sum(lp[2:3, :] * lp[3:4, :], axis=-1, keepdims=True)) + lam_init)
    o = acc_ref[...] / l_ref[...]
    d = o[:, 0:tq] - lam * o[:, tq:2 * tq]
    ms = jnp.mean(d * d, axis=0, keepdims=True)
    o_ref[...] = (d * lax.rsqrt(ms + EPS) * sg_ref[...] * (1.0 - lam_init)).T.astype(BF16)


def _attention(lam_params, subln_g, qT, k, vT, *, batch, seq, lam_init):
    dq, t = qT.shape
    n_heads = dq // V_DIM
    tq, tk = TQ, TK
    assert tq == tk and vT.shape[2] == tk
    nq = seq // tq
    nk = seq // tk
    const = lambda b, h, i: (0, 0)
    return pl.pallas_call(
        functools.partial(_attn_kernel, tq=tq, tk=tk, lam_init=lam_init),
        grid=(batch, n_heads, nq),
        in_specs=[
            pl.BlockSpec(lam_params.shape, const),
            pl.BlockSpec((V_DIM, 1), const),
            pl.BlockSpec((V_DIM, tq), lambda b, h, i: (h, b * nq + i)),
            pl.BlockSpec((seq, V_DIM), lambda b, h, i: (b, h)),
            pl.BlockSpec((nk, V_DIM, tk), lambda b, h, i: (b, h, 0)),
        ],
        out_specs=pl.BlockSpec((tq, V_DIM), lambda b, h, i: (b * nq + i, h)),
        out_shape=jax.ShapeDtypeStruct((t, dq), BF16),
        scratch_shapes=[pltpu.VMEM((1, 2 * tq), F32), pltpu.VMEM((1, 2 * tq), F32),
                        pltpu.VMEM((V_DIM, 2 * tq), F32),
                        pltpu.VMEM((tk, tq), F32), pltpu.VMEM((tk, tq), BF16), pltpu.VMEM((1, tq), F32)],
        compiler_params=pltpu.CompilerParams(
            dimension_semantics=("arbitrary", "arbitrary", "arbitrary"), vmem_limit_bytes=VMEM_LIMIT),
        name="diff_attention",
    )(lam_params, subln_g, qT, k, vT)


def _outproj_router_kernel(x_ref, yc_ref, at_ref, wo_ref, g_ref, wrh_ref, wrl_ref,
                           h_ref, hn_ref, route_ref, routeT_ref, cnt_ref, *, tm, dc):
    i = pl.program_id(0)

    @pl.when(i == 0)
    def _():
        cnt_ref[...] = jnp.zeros_like(cnt_ref)

    h = x_ref[...] + _dot(yc_ref[...], wo_ref[0:dc, :]) + _dot(at_ref[...], wo_ref[dc:2 * dc, :])
    h_ref[...] = h
    ms = jnp.mean(h * h, axis=-1, keepdims=True)
    hn = h * lax.rsqrt(ms + EPS) * g_ref[...]
    hi = hn.astype(BF16)
    hn_ref[...] = hn
    lo = (hn - hi.astype(F32)).astype(BF16)
    logits = _dot(hi, wrh_ref[...]) + _dot(lo, wrh_ref[...]) + _dot(hi, wrl_ref[...])

    lane = lax.broadcasted_iota(jnp.int32, logits.shape, 1)
    lane_f = lane.astype(F32)
    neg = -jnp.inf
    is_g = (lane >= N_EXPERTS) & (lane < N_EXPERTS + N_GROUPS)
    gl = jnp.where(is_g, logits, neg)
    gmax = jnp.max(gl, axis=-1, keepdims=True)
    g_gate = 1.0 / jnp.sum(jnp.exp(gl - gmax), axis=-1, keepdims=True)
    g_idx = jnp.min(jnp.where(gl == gmax, lane_f - N_EXPERTS, 1e9), axis=-1, keepdims=True)
    e_lo = g_idx * EXPERTS_PER_GROUP
    in_grp = (lane_f >= e_lo) & (lane_f < e_lo + EXPERTS_PER_GROUP)
    el = jnp.where(in_grp, logits, neg)
    v1 = jnp.max(el, axis=-1, keepdims=True)
    i1 = jnp.min(jnp.where(el == v1, lane_f, 1e9), axis=-1, keepdims=True)
    el2 = jnp.where(lane_f == i1, neg, el)
    v2 = jnp.max(el2, axis=-1, keepdims=True)
    i2 = jnp.min(jnp.where(el2 == v2, lane_f, 1e9), axis=-1, keepdims=True)
    tt = jnp.exp(v2 - v1)
    w1 = g_gate / (1.0 + tt)
    w2 = g_gate * tt / (1.0 + tt)

    sel1 = lane_f == i1
    sel2 = lane_f == i2
    oh = jnp.where(sel1 | sel2, 1.0, 0.0)
    rr = lax.broadcasted_iota(jnp.int32, (tm, tm), 0)
    cc = lax.broadcasted_iota(jnp.int32, (tm, tm), 1)
    lower = jnp.where(cc < rr, 1.0, 0.0).astype(BF16)
    ranks = _dot(lower, oh.astype(BF16)) + cnt_ref[0:1, :]
    r1 = jnp.sum(jnp.where(sel1, ranks, 0.0), axis=-1, keepdims=True)
    r2 = jnp.sum(jnp.where(sel2, ranks, 0.0), axis=-1, keepdims=True)
    cnt_ref[...] = cnt_ref[...] + jnp.sum(oh, axis=0, keepdims=True)

    route = jnp.where(
        lane == 0, i1, jnp.where(lane == 1, i2, jnp.where(lane == 2, r1, jnp.where(
            lane == 3, r2, jnp.where(lane == 4, w1, jnp.where(lane == 5, w2, 0.0))))))
    route_ref[...] = route
    routeT_ref[...] = route.T[0:8, :]


def _outproj_router(x2, yc, at, w_out, g, wr_hi, wr_lo):
    t, d = x2.shape
    dc = yc.shape[1]
    tm = TM_PROJ
    row = lambda i: (i, 0)
    const = lambda i: (0, 0)
    return pl.pallas_call(
        functools.partial(_outproj_router_kernel, tm=tm, dc=dc),
        grid=(t // tm,),
        in_specs=[
            pl.BlockSpec((tm, d), row),
            pl.BlockSpec((tm, dc), row),
            pl.BlockSpec((tm, dc), row),
            pl.BlockSpec(w_out.shape, const),
            pl.BlockSpec((1, d), const),
            pl.BlockSpec(wr_hi.shape, const),
            pl.BlockSpec(wr_lo.shape, const),
        ],
        out_specs=[
            pl.BlockSpec((tm, d), row),
            pl.BlockSpec((tm, d), row),
            pl.BlockSpec((tm, LANES), row),
            pl.BlockSpec((8, tm), lambda i: (0, i)),
            pl.BlockSpec((8, LANES), const),
        ],
        out_shape=[
            jax.ShapeDtypeStruct((t, d), F32),
            jax.ShapeDtypeStruct((t, d), F32),
            jax.ShapeDtypeStruct((t, LANES), F32),
            jax.ShapeDtypeStruct((8, t), F32),
            jax.ShapeDtypeStruct((8, LANES), F32),
        ],
        compiler_params=pltpu.CompilerParams(
            dimension_semantics=("arbitrary",), vmem_limit_bytes=VMEM_LIMIT),
        name="outproj_router",
    )(x2, yc, at, w_out, g, wr_hi, wr_lo)


def _positions_kernel(offs_ref, rt_ref, pos_ref):
    rt = rt_ref[...]
    ea, eb = rt[0:1, :], rt[1:2, :]
    sa = jnp.zeros_like(ea)
    sb = jnp.zeros_like(eb)
    for e in range(N_EXPERTS):
        start = offs_ref[e].astype(F32)
        sa = jnp.where(ea == e, start, sa)
        sb = jnp.where(eb == e, start, sb)
    pos_ref[0:1, :] = (sa + rt[2:3, :]).astype(jnp.int32)
    pos_ref[1:2, :] = (sb + rt[3:4, :]).astype(jnp.int32)


def _positions(offs, routeT):
    t = routeT.shape[1]
    return pl.pallas_call(
        _positions_kernel,
        grid_spec=pltpu.PrefetchScalarGridSpec(
            num_scalar_prefetch=1, grid=(1,),
            in_specs=[pl.BlockSpec(routeT.shape, lambda i, offs: (0, 0))],
            out_specs=pl.BlockSpec((2, t), lambda i, offs: (0, 0)),
        ),
        out_shape=jax.ShapeDtypeStruct((2, t), jnp.int32),
        name="positions",
    )(offs, routeT)


def _moe_kernel(te_ref, nt_ref, x_ref, wg_ref, wu_ref, wd_ref, y_ref, wgb_ref, wub_ref, wdb_ref):
    i = pl.program_id(0)
    last = nt_ref[0] - 1
    expert = te_ref[jnp.minimum(i, last)]
    prev_expert = te_ref[jnp.minimum(jnp.maximum(i - 1, 0), last)]

    @pl.when((i == 0) | (expert != prev_expert))
    def _():
        wgb_ref[...] = wg_ref[0].astype(BF16)
        wub_ref[...] = wu_ref[0].astype(BF16)
        wdb_ref[...] = wd_ref[0].astype(BF16)

    @pl.when(i <= last)
    def _():
        x = x_ref[...].astype(BF16)
        hg = _dot(x, wgb_ref[...])
        hu = _dot(x, wub_ref[...])
        act = hg * (1.0 / (1.0 + jnp.exp(-hg))) * hu
        y_ref[...] = _dot(act.astype(BF16), wdb_ref[...])


def _moe(tile_expert, n_tiles, xs, wg, wu, wd):
    p, d = xs.shape
    f = wg.shape[2]
    tm = TM_MOE
    row = lambda i, te, nt: (jnp.minimum(i, nt[0] - 1), 0)
    wsel = lambda i, te, nt: (te[jnp.minimum(i, nt[0] - 1)], 0, 0)
    grid_spec = pltpu.PrefetchScalarGridSpec(
        num_scalar_prefetch=2,
        grid=(p // tm,),
        in_specs=[
            pl.BlockSpec((tm, d), row),
            pl.BlockSpec((1, d, f), wsel),
            pl.BlockSpec((1, d, f), wsel),
            pl.BlockSpec((1, f, d), wsel),
        ],
        out_specs=pl.BlockSpec((tm, d), row),
        scratch_shapes=[pltpu.VMEM((d, f), BF16), pltpu.VMEM((d, f), BF16), pltpu.VMEM((f, d), BF16)],
    )
    return pl.pallas_call(
        _moe_kernel,
        grid_spec=grid_spec,
        out_shape=jax.ShapeDtypeStruct((p, d), F32),
        compiler_params=pltpu.CompilerParams(
            dimension_semantics=("arbitrary",), vmem_limit_bytes=VMEM_LIMIT),
        name="moe_experts",
    )(tile_expert, n_tiles, xs, wg, wu, wd)


def _sc_mesh():
    return plsc.VectorSubcoreMesh(core_axis_name="c", subcore_axis_name="s",
                                  num_cores=SC_CORES, num_subcores=SC_SUBCORES)


def _sc_dispatch(rows, pos_a, pos_b, n_out):
    t, d = rows.shape
    win = pos_a.shape[1]

    @functools.partial(pl.kernel, out_type=jax.ShapeDtypeStruct((n_out, d), rows.dtype),
                       mesh=_sc_mesh(), scratch_types=[], name="sc_dispatch")
    def run(rows_hbm, pa_hbm, pb_hbm, out_hbm):
        def body(rows_vmem, pa_vmem, pb_vmem):
            pltpu.sync_copy(rows_vmem, out_hbm.at[pa_vmem.at[0]])
            pltpu.sync_copy(rows_vmem, out_hbm.at[pb_vmem.at[0]])

        pltpu.emit_pipeline(
            body, grid=(t // win,),
            in_specs=[pl.BlockSpec((win, d), lambda i: (i, 0)),
                      pl.BlockSpec((1, win), lambda i: (i, 0)),
                      pl.BlockSpec((1, win), lambda i: (i, 0))],
            out_specs=[],
            core_axis_name=("c", "s"),
            dimension_semantics=(pltpu.PARALLEL,),
        )(rows_hbm, pa_hbm, pb_hbm)

    return run(rows, pos_a, pos_b)


def _sc_gather(table, idx):
    d = table.shape[1]
    n_win, win = idx.shape

    @functools.partial(pl.kernel, out_type=jax.ShapeDtypeStruct((n_win * win, d), table.dtype),
                       mesh=_sc_mesh(), scratch_types=[], name="sc_gather")
    def run(table_hbm, idx_hbm, out_hbm):
        def body(idx_vmem, out_vmem):
            pltpu.sync_copy(table_hbm.at[idx_vmem.at[0]], out_vmem)

        pltpu.emit_pipeline(
            body, grid=(n_win,),
            in_specs=[pl.BlockSpec((1, win), lambda i: (i, 0))],
            out_specs=[pl.BlockSpec((win, d), lambda i: (i, 0))],
            core_axis_name=("c", "s"),
            dimension_semantics=(pltpu.PARALLEL,),
        )(idx_hbm, out_hbm)

    return run(table, idx)


def _combine_kernel(h_ref, ya_ref, yb_ref, r_ref, o_ref):
    r = r_ref[...]
    o_ref[...] = h_ref[...] + r[:, 4:5] * ya_ref[...] + r[:, 5:6] * yb_ref[...]


def _combine(hres, yg, route):
    t, d = hres.shape
    tm = TM_PROJ
    nb = t // tm
    return pl.pallas_call(
        _combine_kernel,
        grid=(nb,),
        in_specs=[pl.BlockSpec((tm, d), lambda i: (i, 0)),
                  pl.BlockSpec((tm, d), lambda i: (i, 0)),
                  pl.BlockSpec((tm, d), lambda i: (i + nb, 0)),
                  pl.BlockSpec((tm, LANES), lambda i: (i, 0))],
        out_specs=pl.BlockSpec((tm, d), lambda i: (i, 0)),
        out_shape=jax.ShapeDtypeStruct((t, d), F32),
        compiler_params=pltpu.CompilerParams(
            dimension_semantics=("arbitrary",), vmem_limit_bytes=VMEM_LIMIT),
        name="combine",
    )(hres, yg, yg, route)


def _lambda_init(layer_idx):
    return 0.8 - 0.6 * math.exp(-0.3 * layer_idx)


def _layer(h, l, attn_norm_g, w_in, conv_w, conv_out_g, q_norm_g, k_norm_g,
           lambda_q1, lambda_k1, lambda_q2, lambda_k2, attn_subln_g, w_out,
           ffn_norm_g, w_router_group, w_router_expert, w_exp_gate, w_exp_up, w_exp_down):
    batch, seq, d = h.shape
    t = batch * seq
    dc = conv_w.shape[-1]
    lam_init = _lambda_init(l)
    x2 = h.reshape(t, d)

    reps = dc // HEAD_DIM
    assert dc // CONV_GROUPS == HEAD_DIM
    qg = (jnp.tile(q_norm_g[l], reps) * (HEAD_DIM ** -0.5 * math.log2(math.e))).reshape(1, dc)
    kg = jnp.tile(k_norm_g[l], reps).reshape(1, dc)
    grp = jnp.arange(dc) // HEAD_DIM
    gmat = jnp.where(grp[:, None] == grp[None, :], 1.0 / HEAD_DIM, 0.0).astype(BF16)
    yc, qT, k, vT = _inproj(x2, attn_norm_g[l].reshape(1, d), w_in[l].astype(BF16), conv_w[l],
                          conv_out_g[l].reshape(1, dc), qg, kg, gmat, batch=batch, seq=seq)

    lam_params = jnp.stack([lambda_q1[l], lambda_k1[l], lambda_q2[l], lambda_k2[l]])
    at = _attention(lam_params, attn_subln_g[l].reshape(V_DIM, 1), qT, k, vT,
                    batch=batch, seq=seq, lam_init=lam_init)

    wr = jnp.concatenate([w_router_expert[l], w_router_group[l],
                          jnp.zeros((d, LANES - N_EXPERTS - N_GROUPS), F32)], axis=1)
    wr_hi = wr.astype(BF16)
    wr_lo = (wr - wr_hi.astype(F32)).astype(BF16)
    hres, hn2, route, routeT, cnt = _outproj_router(x2, yc, at, w_out[l].astype(BF16),
                                                    ffn_norm_g[l].reshape(1, d), wr_hi, wr_lo)

    tmm = TM_MOE
    n_tiles_max = (2 * t) // tmm + N_EXPERTS
    p_rows = n_tiles_max * tmm
    counts = cnt[0, :N_EXPERTS].astype(jnp.int32)
    tiles = (counts + tmm - 1) // tmm
    tile_end = jnp.cumsum(tiles)
    offs = (tile_end - tiles) * tmm
    pos = _positions(offs, routeT)
    n_tiles = tile_end[-1:].astype(jnp.int32)
    tile_ids = jnp.arange(n_tiles_max, dtype=jnp.int32)
    tile_expert = jnp.minimum(
        jnp.sum((tile_end[None, :] <= tile_ids[:, None]).astype(jnp.int32), axis=1), N_EXPERTS - 1)

    posw = pos.reshape(2 * t // SC_WIN, SC_WIN)
    pos1w = posw[:t // SC_WIN]
    pos2w = posw[t // SC_WIN:]
    xs = _sc_dispatch(hn2, pos1w, pos2w, p_rows)

    f = w_exp_gate.shape[-1]
    ys = _moe(tile_expert, n_tiles, xs,
              w_exp_gate[l].reshape(N_EXPERTS, d, f),
              w_exp_up[l].reshape(N_EXPERTS, d, f),
              w_exp_down[l].reshape(N_EXPERTS, f, d))
    yg = _sc_gather(ys, posw)
    out = _combine(hres, yg, route)
    return out.reshape(batch, seq, d)


def kernel(x, attn_norm_g, w_in, conv_w, conv_out_g, q_norm_g, k_norm_g, lambda_q1, lambda_k1,
           lambda_q2, lambda_k2, attn_subln_g, w_out, ffn_norm_g, w_router_group, w_router_expert,
           w_exp_gate, w_exp_up, w_exp_down):
    h = x
    for l in range(attn_norm_g.shape[0]):
        h = _layer(h, l, attn_norm_g, w_in, conv_w, conv_out_g, q_norm_g, k_norm_g,
                   lambda_q1, lambda_k1, lambda_q2, lambda_k2, attn_subln_g, w_out,
                   ffn_norm_g, w_router_group, w_router_expert, w_exp_gate, w_exp_up, w_exp_down)
    return h
```

```python
import functools
import math

import jax
import jax.numpy as jnp
from jax import lax
from jax.experimental import pallas as pl
from jax.experimental.pallas import tpu as pltpu
from jax.experimental.pallas import tpu_sc as plsc

F32 = jnp.float32
BF16 = jnp.bfloat16

HEAD_DIM = 64
V_DIM = 2 * HEAD_DIM
CONV_GROUPS = 8
N_GROUPS = 4
EXPERTS_PER_GROUP = 8
N_EXPERTS = N_GROUPS * EXPERTS_PER_GROUP
EPS = 1e-6
LANES = 128
VMEM_LIMIT = 48 * 1024 * 1024

TM_PROJ = 256
TQ = 512
TK = 512
TM_MOE = 256
SC_CORES = 2
SC_SUBCORES = 16
SC_WIN = 64


def _dot(a, b):
    return jnp.dot(a, b, preferred_element_type=F32)


def _pack_rows(x):
    w = x.shape[1] // 2
    bits = lax.bitcast_convert_type(x.astype(BF16).astype(F32), jnp.uint32)
    return lax.bitcast_convert_type((bits[:, :w] >> 16) | bits[:, w:], jnp.int32)


def _unpack_rows(packed):
    bits = lax.bitcast_convert_type(packed, jnp.uint32)
    left = lax.bitcast_convert_type(bits << 16, F32)
    right = lax.bitcast_convert_type(bits & jnp.uint32(0xFFFF0000), F32)
    return left, right


def _split_dot(a_f32, b_bf16):
    hi = a_f32.astype(BF16)
    lo = (a_f32 - hi.astype(F32)).astype(BF16)
    return _dot(hi, b_bf16) + _dot(lo, b_bf16)


def _inproj_kernel(x_ref, g_ref, w_ref, cw_ref, cg_ref, qg_ref, kg_ref, gm_ref,
                   yc_ref, qT_ref, k_ref, vT_ref, carry_ref, *, tm, dc):
    j = pl.program_id(1)

    @pl.when(j == 0)
    def _():
        carry_ref[...] = jnp.zeros_like(carry_ref)

    x = x_ref[...]
    ms = jnp.mean(x * x, axis=-1, keepdims=True)
    hn = (x * lax.rsqrt(ms + EPS) * g_ref[...]).astype(BF16)

    def proj(s):
        return _dot(hn, w_ref[:, s * dc:(s + 1) * dc])

    gm = gm_ref[...]

    u = proj(2) * proj(0)
    prev = carry_ref[...]
    rows = lax.broadcasted_iota(jnp.int32, u.shape, 0)
    u1 = jnp.where(rows == 0, prev[7:8, :], pltpu.roll(u, 1, 0))
    u2 = jnp.where(rows == 0, prev[6:7, :], jnp.where(rows == 1, prev[7:8, :], pltpu.roll(u, 2, 0)))
    carry_ref[...] = u[tm - 8:tm, :]
    cw = cw_ref[...]
    y = proj(1) * (cw[0:1, :] * u2 + cw[1:2, :] * u1 + cw[2:3, :] * u)
    yc_ref[...] = (y * lax.rsqrt(_split_dot(y * y, gm) + EPS) * cg_ref[...]).astype(BF16)

    q = proj(3)
    qT_ref[...] = (q * lax.rsqrt(_split_dot(q * q, gm) + EPS) * qg_ref[...]).T.astype(BF16)
    k = proj(4)
    k_ref[...] = (k * lax.rsqrt(_split_dot(k * k, gm) + EPS) * kg_ref[...]).astype(BF16)
    vT_ref[0] = proj(5).T.astype(BF16)


def _inproj(x2, g, w_in, conv_w, conv_g, qg, kg, gmat, *, batch, seq):
    t, d = x2.shape
    dc = gmat.shape[0]
    tm = TK
    nj = seq // tm
    row = lambda b, j: (b * nj + j, 0)
    const = lambda b, j: (0, 0)
    out_sds = jax.ShapeDtypeStruct((t, dc), BF16)
    return pl.pallas_call(
        functools.partial(_inproj_kernel, tm=tm, dc=dc),
        grid=(batch, nj),
        in_specs=[
            pl.BlockSpec((tm, d), row),
            pl.BlockSpec((1, d), const),
            pl.BlockSpec(w_in.shape, const),
            pl.BlockSpec(conv_w.shape, const),
            pl.BlockSpec((1, dc), const),
            pl.BlockSpec((1, dc), const),
            pl.BlockSpec((1, dc), const),
            pl.BlockSpec(gmat.shape, const),
        ],
        out_specs=[
            pl.BlockSpec((tm, dc), row),
            pl.BlockSpec((dc, tm), lambda b, j: (0, b * nj + j)),
            pl.BlockSpec((tm, dc), row),
            pl.BlockSpec((1, dc, tm), lambda b, j: (b * nj + j, 0, 0)),
        ],
        out_shape=[out_sds, jax.ShapeDtypeStruct((dc, t), BF16), out_sds,
                   jax.ShapeDtypeStruct((t // tm, dc, tm), BF16)],
        scratch_shapes=[pltpu.VMEM((8, dc), F32)],
        compiler_params=pltpu.CompilerParams(
            dimension_semantics=("arbitrary", "arbitrary"), vmem_limit_bytes=VMEM_LIMIT),
        name="inproj_conv_qknorm",
    )(x2, g, w_in, conv_w, conv_g, qg, kg, gmat)


def _attn_kernel(lp_ref, sg_ref, qT_ref, k_ref, vT_ref, o_ref, m_ref, l_ref, acc_ref,
                 sa_ref, pb_ref, ab_ref, *, tq, tk, lam_init):
    qi = pl.program_id(2)
    qT = qT_ref[...]
    row = lax.broadcasted_iota(jnp.int32, qT.shape, 0)
    zero = jnp.zeros_like(qT)
    qqT = jnp.concatenate([jnp.where(row < HEAD_DIM, qT, zero),
                           jnp.where(row >= HEAD_DIM, qT, zero)], axis=1)

    m_ref[...] = jnp.full_like(m_ref, -jnp.inf)
    l_ref[...] = jnp.zeros_like(l_ref)
    acc_ref[...] = jnp.zeros_like(acc_ref)

    map_a = slice(0, tq)
    map_b = slice(tq, 2 * tq)

    def key_block(j):
        return k_ref[pl.ds(pl.multiple_of(j * tk, tk), tk), :]

    def softmax(cols, s, masked):
        if masked:
            qrel = lax.broadcasted_iota(jnp.int32, s.shape, 1)
            krel = lax.broadcasted_iota(jnp.int32, s.shape, 0)
            s = jnp.where(krel <= qrel, s, -jnp.inf)
        m_old = m_ref[:, cols]
        m_new = jnp.maximum(m_old, jnp.max(s, axis=0, keepdims=True))
        alpha = jnp.exp2(m_old - m_new)
        p = jnp.exp2(s - m_new)
        l_ref[:, cols] = alpha * l_ref[:, cols] + jnp.sum(p, axis=0, keepdims=True)
        m_ref[:, cols] = m_new
        return alpha, p.astype(BF16)

    def accumulate(cols, alpha, vb, p):
        acc_ref[:, cols] = alpha * acc_ref[:, cols] + _dot(vb, p)

    def block(j, last):
        accumulate(map_b, ab_ref[...], vT_ref[jnp.maximum(j - 1, 0)], pb_ref[...])
        s_b = _dot(key_block(j), qqT[:, map_b])
        alpha_a, p_a = softmax(map_a, sa_ref[...], last)
        accumulate(map_a, alpha_a, vT_ref[j], p_a)
        if not last:
            sa_ref[...] = _dot(key_block(j + 1), qqT[:, map_a])
        alpha_b, p_b = softmax(map_b, s_b, last)
        if last:
            accumulate(map_b, alpha_b, vT_ref[j], p_b)
        else:
            ab_ref[...] = alpha_b
            pb_ref[...] = p_b

    sa_ref[...] = _dot(key_block(0), qqT[:, map_a])
    pb_ref[...] = jnp.zeros_like(pb_ref)
    ab_ref[...] = jnp.ones_like(ab_ref)

    def body(j, c):
        block(j, False)
        return c

    lax.fori_loop(0, qi, body, 0)
    block(qi, True)

    lp = lp_ref[...]
    lam = (jnp.exp(jnp.sum(lp[0:1, :] * lp[1:2, :], axis=-1, keepdims=True))
           - jnp.exp(jnp.sum(lp[2:3, :] * lp[3:4, :], axis=-1, keepdims=True)) + lam_init)
    o = acc_ref[...] / l_ref[...]
    d = o[:, 0:tq] - lam * o[:, tq:2 * tq]
    ms = jnp.mean(d * d, axis=0, keepdims=True)
    o_ref[...] = (d * lax.rsqrt(ms + EPS) * sg_ref[...] * (1.0 - lam_init)).T.astype(BF16)


def _attention(lam_params, subln_g, qT, k, vT, *, batch, seq, lam_init):
    dq, t = qT.shape
    n_heads = dq // V_DIM
    tq, tk = TQ, TK
    assert tq == tk and vT.shape[2] == tk
    nq = seq // tq
    nk = seq // tk
    const = lambda b, h, i: (0, 0)
    return pl.pallas_call(
        functools.partial(_attn_kernel, tq=tq, tk=tk, lam_init=lam_init),
        grid=(batch, n_heads, nq),
        in_specs=[
            pl.BlockSpec(lam_params.shape, const),
            pl.BlockSpec((V_DIM, 1), const),
            pl.BlockSpec((V_DIM, tq), lambda b, h, i: (h, b * nq + i)),
            pl.BlockSpec((seq, V_DIM), lambda b, h, i: (b, h)),
            pl.BlockSpec((nk, V_DIM, tk), lambda b, h, i: (b, h, 0)),
        ],
        out_specs=pl.BlockSpec((tq, V_DIM), lambda b, h, i: (b * nq + i, h)),
        out_shape=jax.ShapeDtypeStruct((t, dq), BF16),
        scratch_shapes=[pltpu.VMEM((1, 2 * tq), F32), pltpu.VMEM((1, 2 * tq), F32),
                        pltpu.VMEM((V_DIM, 2 * tq), F32),
                        pltpu.VMEM((tk, tq), F32), pltpu.VMEM((tk, tq), BF16), pltpu.VMEM((1, tq), F32)],
        compiler_params=pltpu.CompilerParams(
            dimension_semantics=("arbitrary", "arbitrary", "arbitrary"), vmem_limit_bytes=VMEM_LIMIT),
        name="diff_attention",
    )(lam_params, subln_g, qT, k, vT)


def _outproj_router_kernel(x_ref, yc_ref, at_ref, wo_ref, g_ref, wrh_ref, wrl_ref,
                           h_ref, hn_ref, route_ref, routeT_ref, cnt_ref, *, tm, dc):
    i = pl.program_id(0)

    @pl.when(i == 0)
    def _():
        cnt_ref[...] = jnp.zeros_like(cnt_ref)

    h = x_ref[...] + _dot(yc_ref[...], wo_ref[0:dc, :]) + _dot(at_ref[...], wo_ref[dc:2 * dc, :])
    h_ref[...] = h
    ms = jnp.mean(h * h, axis=-1, keepdims=True)
    hn = h * lax.rsqrt(ms + EPS) * g_ref[...]
    hi = hn.astype(BF16)
    hn_ref[...] = _pack_rows(hn)
    lo = (hn - hi.astype(F32)).astype(BF16)
    logits = _dot(hi, wrh_ref[...]) + _dot(lo, wrh_ref[...]) + _dot(hi, wrl_ref[...])

    lane = lax.broadcasted_iota(jnp.int32, logits.shape, 1)
    lane_f = lane.astype(F32)
    neg = -jnp.inf
    is_g = (lane >= N_EXPERTS) & (lane < N_EXPERTS + N_GROUPS)
    gl = jnp.where(is_g, logits, neg)
    gmax = jnp.max(gl, axis=-1, keepdims=True)
    g_gate = 1.0 / jnp.sum(jnp.exp(gl - gmax), axis=-1, keepdims=True)
    g_idx = jnp.min(jnp.where(gl == gmax, lane_f - N_EXPERTS, 1e9), axis=-1, keepdims=True)
    e_lo = g_idx * EXPERTS_PER_GROUP
    in_grp = (lane_f >= e_lo) & (lane_f < e_lo + EXPERTS_PER_GROUP)
    el = jnp.where(in_grp, logits, neg)
    v1 = jnp.max(el, axis=-1, keepdims=True)
    i1 = jnp.min(jnp.where(el == v1, lane_f, 1e9), axis=-1, keepdims=True)
    el2 = jnp.where(lane_f == i1, neg, el)
    v2 = jnp.max(el2, axis=-1, keepdims=True)
    i2 = jnp.min(jnp.where(el2 == v2, lane_f, 1e9), axis=-1, keepdims=True)
    tt = jnp.exp(v2 - v1)
    w1 = g_gate / (1.0 + tt)
    w2 = g_gate * tt / (1.0 + tt)

    sel1 = lane_f == i1
    sel2 = lane_f == i2
    oh = jnp.where(sel1 | sel2, 1.0, 0.0)
    rr = lax.broadcasted_iota(jnp.int32, (tm, tm), 0)
    cc = lax.broadcasted_iota(jnp.int32, (tm, tm), 1)
    lower = jnp.where(cc < rr, 1.0, 0.0).astype(BF16)
    ranks = _dot(lower, oh.astype(BF16)) + cnt_ref[0:1, :]
    r1 = jnp.sum(jnp.where(sel1, ranks, 0.0), axis=-1, keepdims=True)
    r2 = jnp.sum(jnp.where(sel2, ranks, 0.0), axis=-1, keepdims=True)
    cnt_ref[...] = cnt_ref[...] + jnp.sum(oh, axis=0, keepdims=True)

    route = jnp.where(
        lane == 0, i1, jnp.where(lane == 1, i2, jnp.where(lane == 2, r1, jnp.where(
            lane == 3, r2, jnp.where(lane == 4, w1, jnp.where(lane == 5, w2, 0.0))))))
    route_ref[...] = route
    routeT_ref[...] = route.T[0:8, :]


def _outproj_router(x2, yc, at, w_out, g, wr_hi, wr_lo):
    t, d = x2.shape
    dc = yc.shape[1]
    tm = TM_PROJ
    row = lambda i: (i, 0)
    const = lambda i: (0, 0)
    return pl.pallas_call(
        functools.partial(_outproj_router_kernel, tm=tm, dc=dc),
        grid=(t // tm,),
        in_specs=[
            pl.BlockSpec((tm, d), row),
            pl.BlockSpec((tm, dc), row),
            pl.BlockSpec((tm, dc), row),
            pl.BlockSpec(w_out.shape, const),
            pl.BlockSpec((1, d), const),
            pl.BlockSpec(wr_hi.shape, const),
            pl.BlockSpec(wr_lo.shape, const),
        ],
        out_specs=[
            pl.BlockSpec((tm, d), row),
            pl.BlockSpec((tm, d // 2), row),
            pl.BlockSpec((tm, LANES), row),
            pl.BlockSpec((8, tm), lambda i: (0, i)),
            pl.BlockSpec((8, LANES), const),
        ],
        out_shape=[
            jax.ShapeDtypeStruct((t, d), F32),
            jax.ShapeDtypeStruct((t, d // 2), jnp.int32),
            jax.ShapeDtypeStruct((t, LANES), F32),
            jax.ShapeDtypeStruct((8, t), F32),
            jax.ShapeDtypeStruct((8, LANES), F32),
        ],
        compiler_params=pltpu.CompilerParams(
            dimension_semantics=("arbitrary",), vmem_limit_bytes=VMEM_LIMIT),
        name="outproj_router",
    )(x2, yc, at, w_out, g, wr_hi, wr_lo)


def _positions_kernel(offs_ref, rt_ref, pos_ref):
    rt = rt_ref[...]
    ea, eb = rt[0:1, :], rt[1:2, :]
    sa = jnp.zeros_like(ea)
    sb = jnp.zeros_like(eb)
    for e in range(N_EXPERTS):
        start = offs_ref[e].astype(F32)
        sa = jnp.where(ea == e, start, sa)
        sb = jnp.where(eb == e, start, sb)
    pos_ref[0:1, :] = (sa + rt[2:3, :]).astype(jnp.int32)
    pos_ref[1:2, :] = (sb + rt[3:4, :]).astype(jnp.int32)


def _positions(offs, routeT):
    t = routeT.shape[1]
    return pl.pallas_call(
        _positions_kernel,
        grid_spec=pltpu.PrefetchScalarGridSpec(
            num_scalar_prefetch=1, grid=(1,),
            in_specs=[pl.BlockSpec(routeT.shape, lambda i, offs: (0, 0))],
            out_specs=pl.BlockSpec((2, t), lambda i, offs: (0, 0)),
        ),
        out_shape=jax.ShapeDtypeStruct((2, t), jnp.int32),
        name="positions",
    )(offs, routeT)


def _moe_kernel(te_ref, nt_ref, x_ref, wg_ref, wu_ref, wd_ref, y_ref, wgb_ref, wub_ref, wdb_ref):
    i = pl.program_id(0)
    last = nt_ref[0] - 1
    expert = te_ref[jnp.minimum(i, last)]
    prev_expert = te_ref[jnp.minimum(jnp.maximum(i - 1, 0), last)]

    @pl.when((i == 0) | (expert != prev_expert))
    def _():
        wgb_ref[...] = wg_ref[0].astype(BF16)
        wub_ref[...] = wu_ref[0].astype(BF16)
        wdb_ref[...] = wd_ref[0].astype(BF16)

    @pl.when(i <= last)
    def _():
        x_l, x_r = _unpack_rows(x_ref[...])
        x_l = x_l.astype(BF16)
        x_r = x_r.astype(BF16)
        half = x_l.shape[1]
        hg = _dot(x_l, wgb_ref[0:half, :]) + _dot(x_r, wgb_ref[half:2 * half, :])
        hu = _dot(x_l, wub_ref[0:half, :]) + _dot(x_r, wub_ref[half:2 * half, :])
        act = hg * (1.0 / (1.0 + jnp.exp(-hg))) * hu
        y_ref[...] = _pack_rows(_dot(act.astype(BF16), wdb_ref[...]))


def _moe(tile_expert, n_tiles, xs, wg, wu, wd):
    p, dp = xs.shape
    d, f = wg.shape[1], wg.shape[2]
    assert dp * 2 == d
    tm = TM_MOE
    row = lambda i, te, nt: (jnp.minimum(i, nt[0] - 1), 0)
    wsel = lambda i, te, nt: (te[jnp.minimum(i, nt[0] - 1)], 0, 0)
    grid_spec = pltpu.PrefetchScalarGridSpec(
        num_scalar_prefetch=2,
        grid=(p // tm,),
        in_specs=[
            pl.BlockSpec((tm, dp), row),
            pl.BlockSpec((1, d, f), wsel),
            pl.BlockSpec((1, d, f), wsel),
            pl.BlockSpec((1, f, d), wsel),
        ],
        out_specs=pl.BlockSpec((tm, dp), row),
        scratch_shapes=[pltpu.VMEM((d, f), BF16), pltpu.VMEM((d, f), BF16), pltpu.VMEM((f, d), BF16)],
    )
    return pl.pallas_call(
        _moe_kernel,
        grid_spec=grid_spec,
        out_shape=jax.ShapeDtypeStruct((p, dp), jnp.int32),
        compiler_params=pltpu.CompilerParams(
            dimension_semantics=("arbitrary",), vmem_limit_bytes=VMEM_LIMIT),
        name="moe_experts",
    )(tile_expert, n_tiles, xs, wg, wu, wd)


def _sc_mesh():
    return plsc.VectorSubcoreMesh(core_axis_name="c", subcore_axis_name="s",
                                  num_cores=SC_CORES, num_subcores=SC_SUBCORES)


def _sc_dispatch(rows, pos_a, pos_b, n_out):
    t, d = rows.shape
    win = pos_a.shape[1]

    @functools.partial(pl.kernel, out_type=jax.ShapeDtypeStruct((n_out, d), rows.dtype),
                       mesh=_sc_mesh(), scratch_types=[], name="sc_dispatch")
    def run(rows_hbm, pa_hbm, pb_hbm, out_hbm):
        def body(rows_vmem, pa_vmem, pb_vmem):
            pltpu.sync_copy(rows_vmem, out_hbm.at[pa_vmem.at[0]])
            pltpu.sync_copy(rows_vmem, out_hbm.at[pb_vmem.at[0]])

        pltpu.emit_pipeline(
            body, grid=(t // win,),
            in_specs=[pl.BlockSpec((win, d), lambda i: (i, 0)),
                      pl.BlockSpec((1, win), lambda i: (i, 0)),
                      pl.BlockSpec((1, win), lambda i: (i, 0))],
            out_specs=[],
            core_axis_name=("c", "s"),
            dimension_semantics=(pltpu.PARALLEL,),
        )(rows_hbm, pa_hbm, pb_hbm)

    return run(rows, pos_a, pos_b)


def _sc_gather(table, idx):
    d = table.shape[1]
    n_win, win = idx.shape

    @functools.partial(pl.kernel, out_type=jax.ShapeDtypeStruct((n_win * win, d), table.dtype),
                       mesh=_sc_mesh(), scratch_types=[], name="sc_gather")
    def run(table_hbm, idx_hbm, out_hbm):
        def body(idx_vmem, out_vmem):
            pltpu.sync_copy(table_hbm.at[idx_vmem.at[0]], out_vmem)

        pltpu.emit_pipeline(
            body, grid=(n_win,),
            in_specs=[pl.BlockSpec((1, win), lambda i: (i, 0))],
            out_specs=[pl.BlockSpec((win, d), lambda i: (i, 0))],
            core_axis_name=("c", "s"),
            dimension_semantics=(pltpu.PARALLEL,),
        )(idx_hbm, out_hbm)

    return run(table, idx)


def _combine_kernel(h_ref, ya_ref, yb_ref, r_ref, o_ref):
    r = r_ref[...]
    wa, wb = r[:, 4:5], r[:, 5:6]
    a_l, a_r = _unpack_rows(ya_ref[...])
    b_l, b_r = _unpack_rows(yb_ref[...])
    half = a_l.shape[1]
    o_ref[:, 0:half] = h_ref[:, 0:half] + wa * a_l + wb * b_l
    o_ref[:, half:2 * half] = h_ref[:, half:2 * half] + wa * a_r + wb * b_r


def _combine(hres, yg, route):
    t, d = hres.shape
    tm = TM_PROJ
    nb = t // tm
    return pl.pallas_call(
        _combine_kernel,
        grid=(nb,),
        in_specs=[pl.BlockSpec((tm, d), lambda i: (i, 0)),
                  pl.BlockSpec((tm, d // 2), lambda i: (i, 0)),
                  pl.BlockSpec((tm, d // 2), lambda i: (i + nb, 0)),
                  pl.BlockSpec((tm, LANES), lambda i: (i, 0))],
        out_specs=pl.BlockSpec((tm, d), lambda i: (i, 0)),
        out_shape=jax.ShapeDtypeStruct((t, d), F32),
        compiler_params=pltpu.CompilerParams(
            dimension_semantics=("arbitrary",), vmem_limit_bytes=VMEM_LIMIT),
        name="combine",
    )(hres, yg, yg, route)


def _lambda_init(layer_idx):
    return 0.8 - 0.6 * math.exp(-0.3 * layer_idx)


def _layer(h, l, attn_norm_g, w_in, conv_w, conv_out_g, q_norm_g, k_norm_g,
           lambda_q1, lambda_k1, lambda_q2, lambda_k2, attn_subln_g, w_out,
           ffn_norm_g, w_router_group, w_router_expert, w_exp_gate, w_exp_up, w_exp_down):
    batch, seq, d = h.shape
    t = batch * seq
    dc = conv_w.shape[-1]
    lam_init = _lambda_init(l)
    x2 = h.reshape(t, d)

    reps = dc // HEAD_DIM
    assert dc // CONV_GROUPS == HEAD_DIM
    qg = (jnp.tile(q_norm_g[l], reps) * (HEAD_DIM ** -0.5 * math.log2(math.e))).reshape(1, dc)
    kg = jnp.tile(k_norm_g[l], reps).reshape(1, dc)
    grp = jnp.arange(dc) // HEAD_DIM
    gmat = jnp.where(grp[:, None] == grp[None, :], 1.0 / HEAD_DIM, 0.0).astype(BF16)
    yc, qT, k, vT = _inproj(x2, attn_norm_g[l].reshape(1, d), w_in[l].astype(BF16), conv_w[l],
                          conv_out_g[l].reshape(1, dc), qg, kg, gmat, batch=batch, seq=seq)

    lam_params = jnp.stack([lambda_q1[l], lambda_k1[l], lambda_q2[l], lambda_k2[l]])
    at = _attention(lam_params, attn_subln_g[l].reshape(V_DIM, 1), qT, k, vT,
                    batch=batch, seq=seq, lam_init=lam_init)

    wr = jnp.concatenate([w_router_expert[l], w_router_group[l],
                          jnp.zeros((d, LANES - N_EXPERTS - N_GROUPS), F32)], axis=1)
    wr_hi = wr.astype(BF16)
    wr_lo = (wr - wr_hi.astype(F32)).astype(BF16)
    hres, hn2, route, routeT, cnt = _outproj_router(x2, yc, at, w_out[l].astype(BF16),
                                                    ffn_norm_g[l].reshape(1, d), wr_hi, wr_lo)

    tmm = TM_MOE
    n_tiles_max = (2 * t) // tmm + N_EXPERTS
    p_rows = n_tiles_max * tmm
    counts = cnt[0, :N_EXPERTS].astype(jnp.int32)
    tiles = (counts + tmm - 1) // tmm
    tile_end = jnp.cumsum(tiles)
    offs = (tile_end - tiles) * tmm
    pos = _positions(offs, routeT)
    n_tiles = tile_end[-1:].astype(jnp.int32)
    tile_ids = jnp.arange(n_tiles_max, dtype=jnp.int32)
    tile_expert = jnp.minimum(
        jnp.sum((tile_end[None, :] <= tile_ids[:, None]).astype(jnp.int32), axis=1), N_EXPERTS - 1)

    posw = pos.reshape(2 * t // SC_WIN, SC_WIN)
    pos1w = posw[:t // SC_WIN]
    pos2w = posw[t // SC_WIN:]
    xs = _sc_dispatch(hn2, pos1w, pos2w, p_rows)

    f = w_exp_gate.shape[-1]
    ys = _moe(tile_expert, n_tiles, xs,
              w_exp_gate[l].reshape(N_EXPERTS, d, f),
              w_exp_up[l].reshape(N_EXPERTS, d, f),
              w_exp_down[l].reshape(N_EXPERTS, f, d))
    yg = _sc_gather(ys, posw)
    out = _combine(hres, yg, route)
    return out.reshape(batch, seq, d)


def kernel(x, attn_norm_g, w_in, conv_w, conv_out_g, q_norm_g, k_norm_g, lambda_q1, lambda_k1,
           lambda_q2, lambda_k2, attn_subln_g, w_out, ffn_norm_g, w_router_group, w_router_expert,
           w_exp_gate, w_exp_up, w_exp_down):
    h = x
    for l in range(attn_norm_g.shape[0]):
        h = _layer(h, l, attn_norm_g, w_in, conv_w, conv_out_g, q_norm_g, k_norm_g,
                   lambda_q1, lambda_k1, lambda_q2, lambda_k2, attn_subln_g, w_out,
                   ffn_norm_g, w_router_group, w_router_expert, w_exp_gate, w_exp_up, w_exp_down)
    return h
```

```python
import functools
import math

import jax
import jax.numpy as jnp
from jax import lax
from jax.experimental import pallas as pl
from jax.experimental.pallas import tpu as pltpu
from jax.experimental.pallas import tpu_sc as plsc

F32 = jnp.float32
BF16 = jnp.bfloat16

HEAD_DIM = 64
V_DIM = 2 * HEAD_DIM
CONV_GROUPS = 8
N_GROUPS = 4
EXPERTS_PER_GROUP = 8
N_EXPERTS = N_GROUPS * EXPERTS_PER_GROUP
EPS = 1e-6
LANES = 128
MXU_TILE = 256
VMEM_LIMIT = 48 * 1024 * 1024

TM_PROJ = 256
TQ = 512
TK = 512
TM_MOE = 256
SC_CORES = 2
SC_SUBCORES = 16
SC_WIN = 64


def _dot(a, b):
    return jnp.dot(a, b, preferred_element_type=F32)


def _pack_rows(x):
    w = x.shape[1] // 2
    bits = lax.bitcast_convert_type(x.astype(BF16).astype(F32), jnp.uint32)
    return lax.bitcast_convert_type((bits[:, :w] >> 16) | bits[:, w:], jnp.int32)


def _unpack_rows(packed):
    bits = lax.bitcast_convert_type(packed, jnp.uint32)
    left = lax.bitcast_convert_type(bits << 16, F32)
    right = lax.bitcast_convert_type(bits & jnp.uint32(0xFFFF0000), F32)
    return left, right


def _group_mean(sq, gm):
    w = gm.shape[0]
    sq = sq.astype(BF16)
    return jnp.concatenate([_dot(sq[:, c:c + w], gm) for c in range(0, sq.shape[1], w)], axis=1)


def _inproj_kernel(x_ref, g_ref, w_ref, cw_ref, cg_ref, qg_ref, kg_ref, gm_ref,
                   yc_ref, qT_ref, k_ref, vT_ref, carry_ref, *, tm, dc):
    j = pl.program_id(1)

    @pl.when(j == 0)
    def _():
        carry_ref[...] = jnp.zeros_like(carry_ref)

    x = x_ref[...]
    ms = jnp.mean(x * x, axis=-1, keepdims=True)
    hn = (x * lax.rsqrt(ms + EPS) * g_ref[...]).astype(BF16)

    def proj(s):
        return _dot(hn, w_ref[:, s * dc:(s + 1) * dc])

    gm = gm_ref[...]

    u = proj(2) * proj(0)
    prev = carry_ref[...]
    rows = lax.broadcasted_iota(jnp.int32, u.shape, 0)
    u1 = jnp.where(rows == 0, prev[7:8, :], pltpu.roll(u, 1, 0))
    u2 = jnp.where(rows == 0, prev[6:7, :], jnp.where(rows == 1, prev[7:8, :], pltpu.roll(u, 2, 0)))
    carry_ref[...] = u[tm - 8:tm, :]
    cw = cw_ref[...]
    y = proj(1) * (cw[0:1, :] * u2 + cw[1:2, :] * u1 + cw[2:3, :] * u)
    yc_ref[...] = (y * lax.rsqrt(_group_mean(y * y, gm) + EPS) * cg_ref[...]).astype(BF16)

    q = proj(3)
    qT_ref[...] = (q * lax.rsqrt(_group_mean(q * q, gm) + EPS) * qg_ref[...]).T.astype(BF16)
    k = proj(4)
    k_ref[...] = (k * lax.rsqrt(_group_mean(k * k, gm) + EPS) * kg_ref[...]).astype(BF16)
    vT_ref[0] = proj(5).T.astype(BF16)


def _inproj(x2, g, w_in, conv_w, conv_g, qg, kg, gmat, *, batch, seq):
    t, d = x2.shape
    dc = conv_g.shape[1]
    tm = TK
    nj = seq // tm
    row = lambda b, j: (b * nj + j, 0)
    const = lambda b, j: (0, 0)
    out_sds = jax.ShapeDtypeStruct((t, dc), BF16)
    return pl.pallas_call(
        functools.partial(_inproj_kernel, tm=tm, dc=dc),
        grid=(batch, nj),
        in_specs=[
            pl.BlockSpec((tm, d), row),
            pl.BlockSpec((1, d), const),
            pl.BlockSpec(w_in.shape, const),
            pl.BlockSpec(conv_w.shape, const),
            pl.BlockSpec((1, dc), const),
            pl.BlockSpec((1, dc), const),
            pl.BlockSpec((1, dc), const),
            pl.BlockSpec(gmat.shape, const),
        ],
        out_specs=[
            pl.BlockSpec((tm, dc), row),
            pl.BlockSpec((dc, tm), lambda b, j: (0, b * nj + j)),
            pl.BlockSpec((tm, dc), row),
            pl.BlockSpec((1, dc, tm), lambda b, j: (b * nj + j, 0, 0)),
        ],
        out_shape=[out_sds, jax.ShapeDtypeStruct((dc, t), BF16), out_sds,
                   jax.ShapeDtypeStruct((t // tm, dc, tm), BF16)],
        scratch_shapes=[pltpu.VMEM((8, dc), F32)],
        compiler_params=pltpu.CompilerParams(
            dimension_semantics=("arbitrary", "arbitrary"), vmem_limit_bytes=VMEM_LIMIT),
        name="inproj_conv_qknorm",
    )(x2, g, w_in, conv_w, conv_g, qg, kg, gmat)


def _attn_kernel(lp_ref, sg_ref, qT_ref, k_ref, vT_ref, o_ref, m_ref, l_ref, acc_ref,
                 sa_ref, pb_ref, ab_ref, *, tq, tk, lam_init):
    qi = pl.program_id(2)
    qT = qT_ref[...]
    row = lax.broadcasted_iota(jnp.int32, qT.shape, 0)
    zero = jnp.zeros_like(qT)
    qqT = jnp.concatenate([jnp.where(row < HEAD_DIM, qT, zero),
                           jnp.where(row >= HEAD_DIM, qT, zero)], axis=1)

    m_ref[...] = jnp.full_like(m_ref, -jnp.inf)
    l_ref[...] = jnp.zeros_like(l_ref)
    acc_ref[...] = jnp.zeros_like(acc_ref)

    map_a = slice(0, tq)
    map_b = slice(tq, 2 * tq)

    def key_block(j):
        return k_ref[pl.ds(pl.multiple_of(j * tk, tk), tk), :]

    def softmax(cols, s, masked):
        if masked:
            qrel = lax.broadcasted_iota(jnp.int32, s.shape, 1)
            krel = lax.broadcasted_iota(jnp.int32, s.shape, 0)
            s = jnp.where(krel <= qrel, s, -jnp.inf)
        m_old = m_ref[:, cols]
        m_new = jnp.maximum(m_old, jnp.max(s, axis=0, keepdims=True))
        alpha = jnp.exp2(m_old - m_new)
        p = jnp.exp2(s - m_new)
        l_ref[:, cols] = alpha * l_ref[:, cols] + jnp.sum(p, axis=0, keepdims=True)
        m_ref[:, cols] = m_new
        return alpha, p.astype(BF16)

    def accumulate(cols, alpha, vb, p):
        acc_ref[:, cols] = alpha * acc_ref[:, cols] + _dot(vb, p)

    def block(j, last):
        accumulate(map_b, ab_ref[...], vT_ref[jnp.maximum(j - 1, 0)], pb_ref[...])
        s_b = _dot(key_block(j), qqT[:, map_b])
        alpha_a, p_a = softmax(map_a, sa_ref[...], last)
        accumulate(map_a, alpha_a, vT_ref[j], p_a)
        if not last:
            sa_ref[...] = _dot(key_block(j + 1), qqT[:, map_a])
        alpha_b, p_b = softmax(map_b, s_b, last)
        if last:
            accumulate(map_b, alpha_b, vT_ref[j], p_b)
        else:
            ab_ref[...] = alpha_b
            pb_ref[...] = p_b

    sa_ref[...] = _dot(key_block(0), qqT[:, map_a])
    pb_ref[...] = jnp.zeros_like(pb_ref)
    ab_ref[...] = jnp.ones_like(ab_ref)

    def body(j, c):
        block(j, False)
        return c

    lax.fori_loop(0, qi, body, 0)
    block(qi, True)

    lp = lp_ref[...]
    lam = (jnp.exp(jnp.sum(lp[0:1, :] * lp[1:2, :], axis=-1, keepdims=True))
           - jnp.exp(jnp.sum(lp[2:3, :] * lp[3:4, :], axis=-1, keepdims=True)) + lam_init)
    o = acc_ref[...] / l_ref[...]
    d = o[:, 0:tq] - lam * o[:, tq:2 * tq]
    ms = jnp.mean(d * d, axis=0, keepdims=True)
    o_ref[...] = (d * lax.rsqrt(ms + EPS) * sg_ref[...] * (1.0 - lam_init)).T.astype(BF16)


def _attention(lam_params, subln_g, qT, k, vT, *, batch, seq, lam_init):
    dq, t = qT.shape
    n_heads = dq // V_DIM
    tq, tk = TQ, TK
    assert tq == tk and vT.shape[2] == tk
    nq = seq // tq
    nk = seq // tk
    const = lambda b, h, i: (0, 0)
    return pl.pallas_call(
        functools.partial(_attn_kernel, tq=tq, tk=tk, lam_init=lam_init),
        grid=(batch, n_heads, nq),
        in_specs=[
            pl.BlockSpec(lam_params.shape, const),
            pl.BlockSpec((V_DIM, 1), const),
            pl.BlockSpec((V_DIM, tq), lambda b, h, i: (h, b * nq + i)),
            pl.BlockSpec((seq, V_DIM), lambda b, h, i: (b, h)),
            pl.BlockSpec((nk, V_DIM, tk), lambda b, h, i: (b, h, 0)),
        ],
        out_specs=pl.BlockSpec((tq, V_DIM), lambda b, h, i: (b * nq + i, h)),
        out_shape=jax.ShapeDtypeStruct((t, dq), BF16),
        scratch_shapes=[pltpu.VMEM((1, 2 * tq), F32), pltpu.VMEM((1, 2 * tq), F32),
                        pltpu.VMEM((V_DIM, 2 * tq), F32),
                        pltpu.VMEM((tk, tq), F32), pltpu.VMEM((tk, tq), BF16), pltpu.VMEM((1, tq), F32)],
        compiler_params=pltpu.CompilerParams(
            dimension_semantics=("arbitrary", "arbitrary", "arbitrary"), vmem_limit_bytes=VMEM_LIMIT),
        name="diff_attention",
    )(lam_params, subln_g, qT, k, vT)


def _outproj_router_kernel(x_ref, yc_ref, at_ref, wo_ref, g_ref, wrh_ref, wrl_ref,
                           h_ref, hn_ref, route_ref, routeT_ref, cnt_ref, *, tm, dc):
    i = pl.program_id(0)

    @pl.when(i == 0)
    def _():
        cnt_ref[...] = jnp.zeros_like(cnt_ref)

    h = x_ref[...] + _dot(yc_ref[...], wo_ref[0:dc, :]) + _dot(at_ref[...], wo_ref[dc:2 * dc, :])
    h_ref[...] = h
    ms = jnp.mean(h * h, axis=-1, keepdims=True)
    hn = h * lax.rsqrt(ms + EPS) * g_ref[...]
    hi = hn.astype(BF16)
    hn_ref[...] = _pack_rows(hn)
    lo = (hn - hi.astype(F32)).astype(BF16)
    logits = _dot(hi, wrh_ref[...]) + _dot(lo, wrh_ref[...]) + _dot(hi, wrl_ref[...])

    lane = lax.broadcasted_iota(jnp.int32, logits.shape, 1)
    lane_f = lane.astype(F32)
    neg = -jnp.inf
    is_g = (lane >= N_EXPERTS) & (lane < N_EXPERTS + N_GROUPS)
    gl = jnp.where(is_g, logits, neg)
    gmax = jnp.max(gl, axis=-1, keepdims=True)
    g_gate = 1.0 / jnp.sum(jnp.exp(gl - gmax), axis=-1, keepdims=True)
    g_idx = jnp.min(jnp.where(gl == gmax, lane_f - N_EXPERTS, 1e9), axis=-1, keepdims=True)
    e_lo = g_idx * EXPERTS_PER_GROUP
    in_grp = (lane_f >= e_lo) & (lane_f < e_lo + EXPERTS_PER_GROUP)
    el = jnp.where(in_grp, logits, neg)
    v1 = jnp.max(el, axis=-1, keepdims=True)
    i1 = jnp.min(jnp.where(el == v1, lane_f, 1e9), axis=-1, keepdims=True)
    el2 = jnp.where(lane_f == i1, neg, el)
    v2 = jnp.max(el2, axis=-1, keepdims=True)
    i2 = jnp.min(jnp.where(el2 == v2, lane_f, 1e9), axis=-1, keepdims=True)
    tt = jnp.exp(v2 - v1)
    w1 = g_gate / (1.0 + tt)
    w2 = g_gate * tt / (1.0 + tt)

    sel1 = lane_f == i1
    sel2 = lane_f == i2
    oh = jnp.where(sel1 | sel2, 1.0, 0.0)
    rr = lax.broadcasted_iota(jnp.int32, (tm, tm), 0)
    cc = lax.broadcasted_iota(jnp.int32, (tm, tm), 1)
    lower = jnp.where(cc < rr, 1.0, 0.0).astype(BF16)
    ranks = _dot(lower, oh.astype(BF16)) + cnt_ref[0:1, :]
    r1 = jnp.sum(jnp.where(sel1, ranks, 0.0), axis=-1, keepdims=True)
    r2 = jnp.sum(jnp.where(sel2, ranks, 0.0), axis=-1, keepdims=True)
    cnt_ref[...] = cnt_ref[...] + jnp.sum(oh, axis=0, keepdims=True)

    route = jnp.where(
        lane == 0, i1, jnp.where(lane == 1, i2, jnp.where(lane == 2, r1, jnp.where(
            lane == 3, r2, jnp.where(lane == 4, w1, jnp.where(lane == 5, w2, 0.0))))))
    route_ref[...] = route
    routeT_ref[...] = route.T[0:8, :]


def _outproj_router(x2, yc, at, w_out, g, wr_hi, wr_lo):
    t, d = x2.shape
    dc = yc.shape[1]
    tm = TM_PROJ
    row = lambda i: (i, 0)
    const = lambda i: (0, 0)
    return pl.pallas_call(
        functools.partial(_outproj_router_kernel, tm=tm, dc=dc),
        grid=(t // tm,),
        in_specs=[
            pl.BlockSpec((tm, d), row),
            pl.BlockSpec((tm, dc), row),
            pl.BlockSpec((tm, dc), row),
            pl.BlockSpec(w_out.shape, const),
            pl.BlockSpec((1, d), const),
            pl.BlockSpec(wr_hi.shape, const),
            pl.BlockSpec(wr_lo.shape, const),
        ],
        out_specs=[
            pl.BlockSpec((tm, d), row),
            pl.BlockSpec((tm, d // 2), row),
            pl.BlockSpec((tm, LANES), row),
            pl.BlockSpec((8, tm), lambda i: (0, i)),
            pl.BlockSpec((8, LANES), const),
        ],
        out_shape=[
            jax.ShapeDtypeStruct((t, d), F32),
            jax.ShapeDtypeStruct((t, d // 2), jnp.int32),
            jax.ShapeDtypeStruct((t, LANES), F32),
            jax.ShapeDtypeStruct((8, t), F32),
            jax.ShapeDtypeStruct((8, LANES), F32),
        ],
        compiler_params=pltpu.CompilerParams(
            dimension_semantics=("arbitrary",), vmem_limit_bytes=VMEM_LIMIT),
        name="outproj_router",
    )(x2, yc, at, w_out, g, wr_hi, wr_lo)


def _positions_kernel(offs_ref, rt_ref, pos_ref):
    rt = rt_ref[...]
    ea, eb = rt[0:1, :], rt[1:2, :]
    sa = jnp.zeros_like(ea)
    sb = jnp.zeros_like(eb)
    for e in range(N_EXPERTS):
        start = offs_ref[e].astype(F32)
        sa = jnp.where(ea == e, start, sa)
        sb = jnp.where(eb == e, start, sb)
    pos_ref[0:1, :] = (sa + rt[2:3, :]).astype(jnp.int32)
    pos_ref[1:2, :] = (sb + rt[3:4, :]).astype(jnp.int32)


def _positions(offs, routeT):
    t = routeT.shape[1]
    return pl.pallas_call(
        _positions_kernel,
        grid_spec=pltpu.PrefetchScalarGridSpec(
            num_scalar_prefetch=1, grid=(1,),
            in_specs=[pl.BlockSpec(routeT.shape, lambda i, offs: (0, 0))],
            out_specs=pl.BlockSpec((2, t), lambda i, offs: (0, 0)),
        ),
        out_shape=jax.ShapeDtypeStruct((2, t), jnp.int32),
        name="positions",
    )(offs, routeT)


def _moe_kernel(te_ref, nt_ref, x_ref, wg_ref, wu_ref, wd_ref, y_ref, wgb_ref, wub_ref, wdb_ref):
    i = pl.program_id(0)
    last = nt_ref[0] - 1
    expert = te_ref[jnp.minimum(i, last)]
    prev_expert = te_ref[jnp.minimum(jnp.maximum(i - 1, 0), last)]

    @pl.when((i == 0) | (expert != prev_expert))
    def _():
        wgb_ref[...] = wg_ref[0].astype(BF16)
        wub_ref[...] = wu_ref[0].astype(BF16)
        wdb_ref[...] = wd_ref[0].astype(BF16)

    @pl.when(i <= last)
    def _():
        x_l, x_r = _unpack_rows(x_ref[...])
        x_l = x_l.astype(BF16)
        x_r = x_r.astype(BF16)
        half = x_l.shape[1]
        hg = _dot(x_l, wgb_ref[0:half, :]) + _dot(x_r, wgb_ref[half:2 * half, :])
        hu = _dot(x_l, wub_ref[0:half, :]) + _dot(x_r, wub_ref[half:2 * half, :])
        act = hg * (1.0 / (1.0 + jnp.exp(-hg))) * hu
        y_ref[...] = _pack_rows(_dot(act.astype(BF16), wdb_ref[...]))


def _moe(tile_expert, n_tiles, xs, wg, wu, wd):
    p, dp = xs.shape
    d, f = wg.shape[1], wg.shape[2]
    assert dp * 2 == d
    tm = TM_MOE
    row = lambda i, te, nt: (jnp.minimum(i, nt[0] - 1), 0)
    wsel = lambda i, te, nt: (te[jnp.minimum(i, nt[0] - 1)], 0, 0)
    grid_spec = pltpu.PrefetchScalarGridSpec(
        num_scalar_prefetch=2,
        grid=(p // tm,),
        in_specs=[
            pl.BlockSpec((tm, dp), row),
            pl.BlockSpec((1, d, f), wsel),
            pl.BlockSpec((1, d, f), wsel),
            pl.BlockSpec((1, f, d), wsel),
        ],
        out_specs=pl.BlockSpec((tm, dp), row),
        scratch_shapes=[pltpu.VMEM((d, f), BF16), pltpu.VMEM((d, f), BF16), pltpu.VMEM((f, d), BF16)],
    )
    return pl.pallas_call(
        _moe_kernel,
        grid_spec=grid_spec,
        out_shape=jax.ShapeDtypeStruct((p, dp), jnp.int32),
        compiler_params=pltpu.CompilerParams(
            dimension_semantics=("arbitrary",), vmem_limit_bytes=VMEM_LIMIT),
        name="moe_experts",
    )(tile_expert, n_tiles, xs, wg, wu, wd)


def _sc_mesh():
    return plsc.VectorSubcoreMesh(core_axis_name="c", subcore_axis_name="s",
                                  num_cores=SC_CORES, num_subcores=SC_SUBCORES)


def _sc_dispatch(rows, pos_a, pos_b, n_out):
    t, d = rows.shape
    win = pos_a.shape[1]

    @functools.partial(pl.kernel, out_type=jax.ShapeDtypeStruct((n_out, d), rows.dtype),
                       mesh=_sc_mesh(), scratch_types=[], name="sc_dispatch")
    def run(rows_hbm, pa_hbm, pb_hbm, out_hbm):
        def body(rows_vmem, pa_vmem, pb_vmem):
            pltpu.sync_copy(rows_vmem, out_hbm.at[pa_vmem.at[0]])
            pltpu.sync_copy(rows_vmem, out_hbm.at[pb_vmem.at[0]])

        pltpu.emit_pipeline(
            body, grid=(t // win,),
            in_specs=[pl.BlockSpec((win, d), lambda i: (i, 0)),
                      pl.BlockSpec((1, win), lambda i: (i, 0)),
                      pl.BlockSpec((1, win), lambda i: (i, 0))],
            out_specs=[],
            core_axis_name=("c", "s"),
            dimension_semantics=(pltpu.PARALLEL,),
        )(rows_hbm, pa_hbm, pb_hbm)

    return run(rows, pos_a, pos_b)


def _sc_gather(table, idx):
    d = table.shape[1]
    n_win, win = idx.shape

    @functools.partial(pl.kernel, out_type=jax.ShapeDtypeStruct((n_win * win, d), table.dtype),
                       mesh=_sc_mesh(), scratch_types=[], name="sc_gather")
    def run(table_hbm, idx_hbm, out_hbm):
        def body(idx_vmem, out_vmem):
            pltpu.sync_copy(table_hbm.at[idx_vmem.at[0]], out_vmem)

        pltpu.emit_pipeline(
            body, grid=(n_win,),
            in_specs=[pl.BlockSpec((1, win), lambda i: (i, 0))],
            out_specs=[pl.BlockSpec((win, d), lambda i: (i, 0))],
            core_axis_name=("c", "s"),
            dimension_semantics=(pltpu.PARALLEL,),
        )(idx_hbm, out_hbm)

    return run(table, idx)


def _combine_kernel(h_ref, ya_ref, yb_ref, r_ref, o_ref):
    r = r_ref[...]
    wa, wb = r[:, 4:5], r[:, 5:6]
    a_l, a_r = _unpack_rows(ya_ref[...])
    b_l, b_r = _unpack_rows(yb_ref[...])
    half = a_l.shape[1]
    o_ref[:, 0:half] = h_ref[:, 0:half] + wa * a_l + wb * b_l
    o_ref[:, half:2 * half] = h_ref[:, half:2 * half] + wa * a_r + wb * b_r


def _combine(hres, yg, route):
    t, d = hres.shape
    tm = TM_PROJ
    nb = t // tm
    return pl.pallas_call(
        _combine_kernel,
        grid=(nb,),
        in_specs=[pl.BlockSpec((tm, d), lambda i: (i, 0)),
                  pl.BlockSpec((tm, d // 2), lambda i: (i, 0)),
                  pl.BlockSpec((tm, d // 2), lambda i: (i + nb, 0)),
                  pl.BlockSpec((tm, LANES), lambda i: (i, 0))],
        out_specs=pl.BlockSpec((tm, d), lambda i: (i, 0)),
        out_shape=jax.ShapeDtypeStruct((t, d), F32),
        compiler_params=pltpu.CompilerParams(
            dimension_semantics=("arbitrary",), vmem_limit_bytes=VMEM_LIMIT),
        name="combine",
    )(hres, yg, yg, route)


def _lambda_init(layer_idx):
    return 0.8 - 0.6 * math.exp(-0.3 * layer_idx)


def _layer(h, l, attn_norm_g, w_in, conv_w, conv_out_g, q_norm_g, k_norm_g,
           lambda_q1, lambda_k1, lambda_q2, lambda_k2, attn_subln_g, w_out,
           ffn_norm_g, w_router_group, w_router_expert, w_exp_gate, w_exp_up, w_exp_down):
    batch, seq, d = h.shape
    t = batch * seq
    dc = conv_w.shape[-1]
    lam_init = _lambda_init(l)
    x2 = h.reshape(t, d)

    reps = dc // HEAD_DIM
    assert dc // CONV_GROUPS == HEAD_DIM
    qg = (jnp.tile(q_norm_g[l], reps) * (HEAD_DIM ** -0.5 * math.log2(math.e))).reshape(1, dc)
    kg = jnp.tile(k_norm_g[l], reps).reshape(1, dc)
    grp = jnp.arange(MXU_TILE) // HEAD_DIM
    gmat = jnp.where(grp[:, None] == grp[None, :], 1.0 / HEAD_DIM, 0.0).astype(BF16)
    yc, qT, k, vT = _inproj(x2, attn_norm_g[l].reshape(1, d), w_in[l].astype(BF16), conv_w[l],
                          conv_out_g[l].reshape(1, dc), qg, kg, gmat, batch=batch, seq=seq)

    lam_params = jnp.stack([lambda_q1[l], lambda_k1[l], lambda_q2[l], lambda_k2[l]])
    at = _attention(lam_params, attn_subln_g[l].reshape(V_DIM, 1), qT, k, vT,
                    batch=batch, seq=seq, lam_init=lam_init)

    wr = jnp.concatenate([w_router_expert[l], w_router_group[l],
                          jnp.zeros((d, LANES - N_EXPERTS - N_GROUPS), F32)], axis=1)
    wr_hi = wr.astype(BF16)
    wr_lo = (wr - wr_hi.astype(F32)).astype(BF16)
    hres, hn2, route, routeT, cnt = _outproj_router(x2, yc, at, w_out[l].astype(BF16),
                                                    ffn_norm_g[l].reshape(1, d), wr_hi, wr_lo)

    tmm = TM_MOE
    n_tiles_max = (2 * t) // tmm + N_EXPERTS
    p_rows = n_tiles_max * tmm
    counts = cnt[0, :N_EXPERTS].astype(jnp.int32)
    tiles = (counts + tmm - 1) // tmm
    tile_end = jnp.cumsum(tiles)
    offs = (tile_end - tiles) * tmm
    pos = _positions(offs, routeT)
    n_tiles = tile_end[-1:].astype(jnp.int32)
    tile_ids = jnp.arange(n_tiles_max, dtype=jnp.int32)
    tile_expert = jnp.minimum(
        jnp.sum((tile_end[None, :] <= tile_ids[:, None]).astype(jnp.int32), axis=1), N_EXPERTS - 1)

    posw = pos.reshape(2 * t // SC_WIN, SC_WIN)
    pos1w = posw[:t // SC_WIN]
    pos2w = posw[t // SC_WIN:]
    xs = _sc_dispatch(hn2, pos1w, pos2w, p_rows)

    f = w_exp_gate.shape[-1]
    ys = _moe(tile_expert, n_tiles, xs,
              w_exp_gate[l].reshape(N_EXPERTS, d, f),
              w_exp_up[l].reshape(N_EXPERTS, d, f),
              w_exp_down[l].reshape(N_EXPERTS, f, d))
    yg = _sc_gather(ys, posw)
    out = _combine(hres, yg, route)
    return out.reshape(batch, seq, d)


def kernel(x, attn_norm_g, w_in, conv_w, conv_out_g, q_norm_g, k_norm_g, lambda_q1, lambda_k1,
           lambda_q2, lambda_k2, attn_subln_g, w_out, ffn_norm_g, w_router_group, w_router_expert,
           w_exp_gate, w_exp_up, w_exp_down):
    h = x
    for l in range(attn_norm_g.shape[0]):
        h = _layer(h, l, attn_norm_g, w_in, conv_w, conv_out_g, q_norm_g, k_norm_g,
                   lambda_q1, lambda_k1, lambda_q2, lambda_k2, attn_subln_g, w_out,
                   ffn_norm_g, w_router_group, w_router_expert, w_exp_gate, w_exp_up, w_exp_down)
    return h
```

```python
import functools
import math

import jax
import jax.numpy as jnp
from jax import lax
from jax.experimental import pallas as pl
from jax.experimental.pallas import tpu as pltpu
from jax.experimental.pallas import tpu_sc as plsc

F32 = jnp.float32
BF16 = jnp.bfloat16

HEAD_DIM = 64
V_DIM = 2 * HEAD_DIM
CONV_GROUPS = 8
N_GROUPS = 4
EXPERTS_PER_GROUP = 8
N_EXPERTS = N_GROUPS * EXPERTS_PER_GROUP
EPS = 1e-6
LANES = 128
MXU_TILE = 256
VMEM_LIMIT = 48 * 1024 * 1024

TM_PROJ = 256
TQ = 512
TK = 512
TM_MOE = 256
SC_CORES = 2
SC_SUBCORES = 16
SC_WIN = 64


def _dot(a, b):
    return jnp.dot(a, b, preferred_element_type=F32)


def _pack_rows(x):
    w = x.shape[1] // 2
    bits = lax.bitcast_convert_type(x.astype(BF16).astype(F32), jnp.uint32)
    return lax.bitcast_convert_type((bits[:, :w] >> 16) | bits[:, w:], jnp.int32)


def _unpack_rows(packed):
    bits = lax.bitcast_convert_type(packed, jnp.uint32)
    left = lax.bitcast_convert_type(bits << 16, F32)
    right = lax.bitcast_convert_type(bits & jnp.uint32(0xFFFF0000), F32)
    return left, right


def _group_mean(sq, gm):
    w = gm.shape[0]
    sq = sq.astype(BF16)
    return jnp.concatenate([_dot(sq[:, c:c + w], gm) for c in range(0, sq.shape[1], w)], axis=1)


def _inproj_kernel(x_ref, g_ref, w_ref, cw_ref, cg_ref, qg_ref, kg_ref, gm_ref,
                   yc_ref, qT_ref, k_ref, vT_ref, carry_ref, *, tm, dc):
    j = pl.program_id(1)

    @pl.when(j == 0)
    def _():
        carry_ref[...] = jnp.zeros_like(carry_ref)

    x = x_ref[...]
    ms = jnp.mean(x * x, axis=-1, keepdims=True)
    hn = (x * lax.rsqrt(ms + EPS) * g_ref[...]).astype(BF16)

    def proj(s):
        return _dot(hn, w_ref[:, s * dc:(s + 1) * dc])

    gm = gm_ref[...]

    u = proj(2) * proj(0)
    prev = carry_ref[...]
    rows = lax.broadcasted_iota(jnp.int32, u.shape, 0)
    u1 = jnp.where(rows == 0, prev[7:8, :], pltpu.roll(u, 1, 0))
    u2 = jnp.where(rows == 0, prev[6:7, :], jnp.where(rows == 1, prev[7:8, :], pltpu.roll(u, 2, 0)))
    carry_ref[...] = u[tm - 8:tm, :]
    cw = cw_ref[...]
    y = proj(1) * (cw[0:1, :] * u2 + cw[1:2, :] * u1 + cw[2:3, :] * u)
    yc_ref[...] = (y * lax.rsqrt(_group_mean(y * y, gm) + EPS) * cg_ref[...]).astype(BF16)

    q = proj(3)
    qT_ref[...] = (q * lax.rsqrt(_group_mean(q * q, gm) + EPS) * qg_ref[...]).T.astype(BF16)
    k = proj(4)
    k_ref[...] = (k * lax.rsqrt(_group_mean(k * k, gm) + EPS) * kg_ref[...]).astype(BF16)
    vT_ref[0] = proj(5).T.astype(BF16)


def _inproj(x2, g, w_in, conv_w, conv_g, qg, kg, gmat, *, batch, seq):
    t, d = x2.shape
    dc = conv_g.shape[1]
    tm = TK
    nj = seq // tm
    row = lambda b, j: (b * nj + j, 0)
    const = lambda b, j: (0, 0)
    out_sds = jax.ShapeDtypeStruct((t, dc), BF16)
    return pl.pallas_call(
        functools.partial(_inproj_kernel, tm=tm, dc=dc),
        grid=(batch, nj),
        in_specs=[
            pl.BlockSpec((tm, d), row),
            pl.BlockSpec((1, d), const),
            pl.BlockSpec(w_in.shape, const),
            pl.BlockSpec(conv_w.shape, const),
            pl.BlockSpec((1, dc), const),
            pl.BlockSpec((1, dc), const),
            pl.BlockSpec((1, dc), const),
            pl.BlockSpec(gmat.shape, const),
        ],
        out_specs=[
            pl.BlockSpec((tm, dc), row),
            pl.BlockSpec((dc, tm), lambda b, j: (0, b * nj + j)),
            pl.BlockSpec((tm, dc), row),
            pl.BlockSpec((1, dc, tm), lambda b, j: (b * nj + j, 0, 0)),
        ],
        out_shape=[out_sds, jax.ShapeDtypeStruct((dc, t), BF16), out_sds,
                   jax.ShapeDtypeStruct((t // tm, dc, tm), BF16)],
        scratch_shapes=[pltpu.VMEM((8, dc), F32)],
        compiler_params=pltpu.CompilerParams(
            dimension_semantics=("arbitrary", "arbitrary"), vmem_limit_bytes=VMEM_LIMIT),
        name="inproj_conv_qknorm",
    )(x2, g, w_in, conv_w, conv_g, qg, kg, gmat)


def _attn_kernel(lp_ref, sg_ref, qT_ref, k_ref, vT_ref, o_ref, m_ref, l_ref, acc_ref,
                 sa_ref, pb_ref, ab_ref, *, tq, tk, lam_init):
    qi = pl.program_id(2)
    qT = qT_ref[...]
    row = lax.broadcasted_iota(jnp.int32, qT.shape, 0)
    zero = jnp.zeros_like(qT)
    qqT = jnp.concatenate([jnp.where(row < HEAD_DIM, qT, zero),
                           jnp.where(row >= HEAD_DIM, qT, zero)], axis=1)

    m_ref[...] = jnp.full_like(m_ref, -jnp.inf)
    l_ref[...] = jnp.zeros_like(l_ref)
    acc_ref[...] = jnp.zeros_like(acc_ref)

    map_a = slice(0, tq)
    map_b = slice(tq, 2 * tq)

    def key_block(j):
        return k_ref[pl.ds(pl.multiple_of(j * tk, tk), tk), :]

    def softmax(cols, s, masked):
        if masked:
            qrel = lax.broadcasted_iota(jnp.int32, s.shape, 1)
            krel = lax.broadcasted_iota(jnp.int32, s.shape, 0)
            s = jnp.where(krel <= qrel, s, -jnp.inf)
        m_old = m_ref[:, cols]
        m_new = jnp.maximum(m_old, jnp.max(s, axis=0, keepdims=True))
        alpha = jnp.exp2(m_old - m_new)
        p = jnp.exp2(s - m_new)
        l_ref[:, cols] = alpha * l_ref[:, cols] + jnp.sum(p, axis=0, keepdims=True)
        m_ref[:, cols] = m_new
        return alpha, p.astype(BF16)

    def accumulate(cols, alpha, vb, p):
        acc_ref[:, cols] = alpha * acc_ref[:, cols] + _dot(vb, p)

    def block(j, last):
        accumulate(map_b, ab_ref[...], vT_ref[jnp.maximum(j - 1, 0)], pb_ref[...])
        s_b = _dot(key_block(j), qqT[:, map_b])
        alpha_a, p_a = softmax(map_a, sa_ref[...], last)
        accumulate(map_a, alpha_a, vT_ref[j], p_a)
        if not last:
            sa_ref[...] = _dot(key_block(j + 1), qqT[:, map_a])
        alpha_b, p_b = softmax(map_b, s_b, last)
        if last:
            accumulate(map_b, alpha_b, vT_ref[j], p_b)
        else:
            ab_ref[...] = alpha_b
            pb_ref[...] = p_b

    sa_ref[...] = _dot(key_block(0), qqT[:, map_a])
    pb_ref[...] = jnp.zeros_like(pb_ref)
    ab_ref[...] = jnp.ones_like(ab_ref)

    def body(j, c):
        block(j, False)
        return c

    lax.fori_loop(0, qi, body, 0)
    block(qi, True)

    lp = lp_ref[...]
    lam = (jnp.exp(jnp.sum(lp[0:1, :] * lp[1:2, :], axis=-1, keepdims=True))
           - jnp.exp(jnp.sum(lp[2:3, :] * lp[3:4, :], axis=-1, keepdims=True)) + lam_init)
    o = acc_ref[...] / l_ref[...]
    d = o[:, 0:tq] - lam * o[:, tq:2 * tq]
    ms = jnp.mean(d * d, axis=0, keepdims=True)
    o_ref[...] = (d * lax.rsqrt(ms + EPS) * sg_ref[...] * (1.0 - lam_init)).T.astype(BF16)


def _attention(lam_params, subln_g, qT, k, vT, *, batch, seq, lam_init):
    dq, t = qT.shape
    n_heads = dq // V_DIM
    tq, tk = TQ, TK
    assert tq == tk and vT.shape[2] == tk
    nq = seq // tq
    nk = seq // tk
    const = lambda b, h, i: (0, 0)
    return pl.pallas_call(
        functools.partial(_attn_kernel, tq=tq, tk=tk, lam_init=lam_init),
        grid=(batch, n_heads, nq),
        in_specs=[
            pl.BlockSpec(lam_params.shape, const),
            pl.BlockSpec((V_DIM, 1), const),
            pl.BlockSpec((V_DIM, tq), lambda b, h, i: (h, b * nq + i)),
            pl.BlockSpec((seq, V_DIM), lambda b, h, i: (b, h)),
            pl.BlockSpec((nk, V_DIM, tk), lambda b, h, i: (b, h, 0)),
        ],
        out_specs=pl.BlockSpec((tq, V_DIM), lambda b, h, i: (b * nq + i, h)),
        out_shape=jax.ShapeDtypeStruct((t, dq), BF16),
        scratch_shapes=[pltpu.VMEM((1, 2 * tq), F32), pltpu.VMEM((1, 2 * tq), F32),
                        pltpu.VMEM((V_DIM, 2 * tq), F32),
                        pltpu.VMEM((tk, tq), F32), pltpu.VMEM((tk, tq), BF16), pltpu.VMEM((1, tq), F32)],
        compiler_params=pltpu.CompilerParams(
            dimension_semantics=("arbitrary", "arbitrary", "arbitrary"), vmem_limit_bytes=VMEM_LIMIT),
        name="diff_attention",
    )(lam_params, subln_g, qT, k, vT)


def _outproj_router_kernel(x_ref, yc_ref, at_ref, wo_ref, g_ref, wrh_ref, wrl_ref,
                           h_ref, hn_ref, route_ref, routeT_ref, cnt_ref, *, tm, dc):
    i = pl.program_id(0)

    @pl.when(i == 0)
    def _():
        cnt_ref[...] = jnp.zeros_like(cnt_ref)

    h = x_ref[...] + _dot(yc_ref[...], wo_ref[0:dc, :]) + _dot(at_ref[...], wo_ref[dc:2 * dc, :])
    h_ref[...] = h
    ms = jnp.mean(h * h, axis=-1, keepdims=True)
    hn = h * lax.rsqrt(ms + EPS) * g_ref[...]
    hi = hn.astype(BF16)
    hn_ref[...] = _pack_rows(hn)
    lo = (hn - hi.astype(F32)).astype(BF16)
    logits = _dot(hi, wrh_ref[...]) + _dot(lo, wrh_ref[...]) + _dot(hi, wrl_ref[...])

    lt = logits.T
    neg = -jnp.inf
    grow = lax.broadcasted_iota(jnp.int32, (8, tm), 0).astype(F32)
    gl = jnp.where(grow < N_GROUPS, lt[N_EXPERTS:N_EXPERTS + 8, :], neg)
    gmax = jnp.max(gl, axis=0, keepdims=True)
    g_gate = 1.0 / jnp.sum(jnp.exp(gl - gmax), axis=0, keepdims=True)
    g_idx = jnp.min(jnp.where(gl == gmax, grow, 1e9), axis=0, keepdims=True)
    erow = lax.broadcasted_iota(jnp.int32, (N_EXPERTS, tm), 0).astype(F32)
    e_lo = g_idx * EXPERTS_PER_GROUP
    el = jnp.where((erow >= e_lo) & (erow < e_lo + EXPERTS_PER_GROUP), lt[0:N_EXPERTS, :], neg)
    v1 = jnp.max(el, axis=0, keepdims=True)
    i1 = jnp.min(jnp.where(el == v1, erow, 1e9), axis=0, keepdims=True)
    el2 = jnp.where(erow == i1, neg, el)
    v2 = jnp.max(el2, axis=0, keepdims=True)
    i2 = jnp.min(jnp.where(el2 == v2, erow, 1e9), axis=0, keepdims=True)
    tt = jnp.exp(v2 - v1)
    w1 = g_gate / (1.0 + tt)
    w2 = g_gate * tt / (1.0 + tt)

    sel1 = erow == i1
    sel2 = erow == i2
    oh = jnp.where(sel1 | sel2, 1.0, 0.0)
    ss = lax.broadcasted_iota(jnp.int32, (tm, tm), 0)
    tt_i = lax.broadcasted_iota(jnp.int32, (tm, tm), 1)
    earlier = jnp.where(ss < tt_i, 1.0, 0.0).astype(BF16)
    ranks = _dot(oh.astype(BF16), earlier) + cnt_ref[...]
    r1 = jnp.sum(jnp.where(sel1, ranks, 0.0), axis=0, keepdims=True)
    r2 = jnp.sum(jnp.where(sel2, ranks, 0.0), axis=0, keepdims=True)
    cnt_ref[...] = cnt_ref[...] + jnp.sum(oh, axis=1, keepdims=True)

    routeT = jnp.concatenate([i1, i2, r1, r2, w1, w2, jnp.zeros((2, tm), F32)], axis=0)
    routeT_ref[...] = routeT
    route_ref[...] = jnp.concatenate([routeT, jnp.zeros((LANES - 8, tm), F32)], axis=0).T


def _outproj_router(x2, yc, at, w_out, g, wr_hi, wr_lo):
    t, d = x2.shape
    dc = yc.shape[1]
    tm = TM_PROJ
    row = lambda i: (i, 0)
    const = lambda i: (0, 0)
    return pl.pallas_call(
        functools.partial(_outproj_router_kernel, tm=tm, dc=dc),
        grid=(t // tm,),
        in_specs=[
            pl.BlockSpec((tm, d), row),
            pl.BlockSpec((tm, dc), row),
            pl.BlockSpec((tm, dc), row),
            pl.BlockSpec(w_out.shape, const),
            pl.BlockSpec((1, d), const),
            pl.BlockSpec(wr_hi.shape, const),
            pl.BlockSpec(wr_lo.shape, const),
        ],
        out_specs=[
            pl.BlockSpec((tm, d), row),
            pl.BlockSpec((tm, d // 2), row),
            pl.BlockSpec((tm, LANES), row),
            pl.BlockSpec((8, tm), lambda i: (0, i)),
            pl.BlockSpec((N_EXPERTS, 1), const),
        ],
        out_shape=[
            jax.ShapeDtypeStruct((t, d), F32),
            jax.ShapeDtypeStruct((t, d // 2), jnp.int32),
            jax.ShapeDtypeStruct((t, LANES), F32),
            jax.ShapeDtypeStruct((8, t), F32),
            jax.ShapeDtypeStruct((N_EXPERTS, 1), F32),
        ],
        compiler_params=pltpu.CompilerParams(
            dimension_semantics=("arbitrary",), vmem_limit_bytes=VMEM_LIMIT),
        name="outproj_router",
    )(x2, yc, at, w_out, g, wr_hi, wr_lo)


def _positions_kernel(offs_ref, rt_ref, pos_ref):
    rt = rt_ref[...]
    ea, eb = rt[0:1, :], rt[1:2, :]
    sa = jnp.zeros_like(ea)
    sb = jnp.zeros_like(eb)
    for e in range(N_EXPERTS):
        start = offs_ref[e].astype(F32)
        sa = jnp.where(ea == e, start, sa)
        sb = jnp.where(eb == e, start, sb)
    pos_ref[0:1, :] = (sa + rt[2:3, :]).astype(jnp.int32)
    pos_ref[1:2, :] = (sb + rt[3:4, :]).astype(jnp.int32)


def _positions(offs, routeT):
    t = routeT.shape[1]
    return pl.pallas_call(
        _positions_kernel,
        grid_spec=pltpu.PrefetchScalarGridSpec(
            num_scalar_prefetch=1, grid=(1,),
            in_specs=[pl.BlockSpec(routeT.shape, lambda i, offs: (0, 0))],
            out_specs=pl.BlockSpec((2, t), lambda i, offs: (0, 0)),
        ),
        out_shape=jax.ShapeDtypeStruct((2, t), jnp.int32),
        name="positions",
    )(offs, routeT)


def _moe_kernel(te_ref, nt_ref, x_ref, wg_ref, wu_ref, wd_ref, y_ref, wgb_ref, wub_ref, wdb_ref):
    i = pl.program_id(0)
    last = nt_ref[0] - 1
    expert = te_ref[jnp.minimum(i, last)]
    prev_expert = te_ref[jnp.minimum(jnp.maximum(i - 1, 0), last)]

    @pl.when((i == 0) | (expert != prev_expert))
    def _():
        wgb_ref[...] = wg_ref[0].astype(BF16)
        wub_ref[...] = wu_ref[0].astype(BF16)
        wdb_ref[...] = wd_ref[0].astype(BF16)

    @pl.when(i <= last)
    def _():
        x_l, x_r = _unpack_rows(x_ref[...])
        x_l = x_l.astype(BF16)
        x_r = x_r.astype(BF16)
        half = x_l.shape[1]
        hg = _dot(x_l, wgb_ref[0:half, :]) + _dot(x_r, wgb_ref[half:2 * half, :])
        hu = _dot(x_l, wub_ref[0:half, :]) + _dot(x_r, wub_ref[half:2 * half, :])
        act = hg * (1.0 / (1.0 + jnp.exp(-hg))) * hu
        y_ref[...] = _pack_rows(_dot(act.astype(BF16), wdb_ref[...]))


def _moe(tile_expert, n_tiles, xs, wg, wu, wd):
    p, dp = xs.shape
    d, f = wg.shape[1], wg.shape[2]
    assert dp * 2 == d
    tm = TM_MOE
    row = lambda i, te, nt: (jnp.minimum(i, nt[0] - 1), 0)
    wsel = lambda i, te, nt: (te[jnp.minimum(i, nt[0] - 1)], 0, 0)
    grid_spec = pltpu.PrefetchScalarGridSpec(
        num_scalar_prefetch=2,
        grid=(p // tm,),
        in_specs=[
            pl.BlockSpec((tm, dp), row),
            pl.BlockSpec((1, d, f), wsel),
            pl.BlockSpec((1, d, f), wsel),
            pl.BlockSpec((1, f, d), wsel),
        ],
        out_specs=pl.BlockSpec((tm, dp), row),
        scratch_shapes=[pltpu.VMEM((d, f), BF16), pltpu.VMEM((d, f), BF16), pltpu.VMEM((f, d), BF16)],
    )
    return pl.pallas_call(
        _moe_kernel,
        grid_spec=grid_spec,
        out_shape=jax.ShapeDtypeStruct((p, dp), jnp.int32),
        compiler_params=pltpu.CompilerParams(
            dimension_semantics=("arbitrary",), vmem_limit_bytes=VMEM_LIMIT),
        name="moe_experts",
    )(tile_expert, n_tiles, xs, wg, wu, wd)


def _sc_mesh():
    return plsc.VectorSubcoreMesh(core_axis_name="c", subcore_axis_name="s",
                                  num_cores=SC_CORES, num_subcores=SC_SUBCORES)


def _sc_dispatch(rows, pos_a, pos_b, n_out):
    t, d = rows.shape
    win = pos_a.shape[1]

    @functools.partial(pl.kernel, out_type=jax.ShapeDtypeStruct((n_out, d), rows.dtype),
                       mesh=_sc_mesh(), scratch_types=[], name="sc_dispatch")
    def run(rows_hbm, pa_hbm, pb_hbm, out_hbm):
        def body(rows_vmem, pa_vmem, pb_vmem):
            pltpu.sync_copy(rows_vmem, out_hbm.at[pa_vmem.at[0]])
            pltpu.sync_copy(rows_vmem, out_hbm.at[pb_vmem.at[0]])

        pltpu.emit_pipeline(
            body, grid=(t // win,),
            in_specs=[pl.BlockSpec((win, d), lambda i: (i, 0)),
                      pl.BlockSpec((1, win), lambda i: (i, 0)),
                      pl.BlockSpec((1, win), lambda i: (i, 0))],
            out_specs=[],
            core_axis_name=("c", "s"),
            dimension_semantics=(pltpu.PARALLEL,),
        )(rows_hbm, pa_hbm, pb_hbm)

    return run(rows, pos_a, pos_b)


def _sc_gather(table, idx):
    d = table.shape[1]
    n_win, win = idx.shape

    @functools.partial(pl.kernel, out_type=jax.ShapeDtypeStruct((n_win * win, d), table.dtype),
                       mesh=_sc_mesh(), scratch_types=[], name="sc_gather")
    def run(table_hbm, idx_hbm, out_hbm):
        def body(idx_vmem, out_vmem):
            pltpu.sync_copy(table_hbm.at[idx_vmem.at[0]], out_vmem)

        pltpu.emit_pipeline(
            body, grid=(n_win,),
            in_specs=[pl.BlockSpec((1, win), lambda i: (i, 0))],
            out_specs=[pl.BlockSpec((win, d), lambda i: (i, 0))],
            core_axis_name=("c", "s"),
            dimension_semantics=(pltpu.PARALLEL,),
        )(idx_hbm, out_hbm)

    return run(table, idx)


def _combine_kernel(h_ref, ya_ref, yb_ref, r_ref, o_ref):
    r = r_ref[...]
    wa, wb = r[:, 4:5], r[:, 5:6]
    a_l, a_r = _unpack_rows(ya_ref[...])
    b_l, b_r = _unpack_rows(yb_ref[...])
    half = a_l.shape[1]
    o_ref[:, 0:half] = h_ref[:, 0:half] + wa * a_l + wb * b_l
    o_ref[:, half:2 * half] = h_ref[:, half:2 * half] + wa * a_r + wb * b_r


def _combine(hres, yg, route):
    t, d = hres.shape
    tm = TM_PROJ
    nb = t // tm
    return pl.pallas_call(
        _combine_kernel,
        grid=(nb,),
        in_specs=[pl.BlockSpec((tm, d), lambda i: (i, 0)),
                  pl.BlockSpec((tm, d // 2), lambda i: (i, 0)),
                  pl.BlockSpec((tm, d // 2), lambda i: (i + nb, 0)),
                  pl.BlockSpec((tm, LANES), lambda i: (i, 0))],
        out_specs=pl.BlockSpec((tm, d), lambda i: (i, 0)),
        out_shape=jax.ShapeDtypeStruct((t, d), F32),
        compiler_params=pltpu.CompilerParams(
            dimension_semantics=("arbitrary",), vmem_limit_bytes=VMEM_LIMIT),
        name="combine",
    )(hres, yg, yg, route)


def _lambda_init(layer_idx):
    return 0.8 - 0.6 * math.exp(-0.3 * layer_idx)


def _layer(h, l, attn_norm_g, w_in, conv_w, conv_out_g, q_norm_g, k_norm_g,
           lambda_q1, lambda_k1, lambda_q2, lambda_k2, attn_subln_g, w_out,
           ffn_norm_g, w_router_group, w_router_expert, w_exp_gate, w_exp_up, w_exp_down):
    batch, seq, d = h.shape
    t = batch * seq
    dc = conv_w.shape[-1]
    lam_init = _lambda_init(l)
    x2 = h.reshape(t, d)

    reps = dc // HEAD_DIM
    assert dc // CONV_GROUPS == HEAD_DIM
    qg = (jnp.tile(q_norm_g[l], reps) * (HEAD_DIM ** -0.5 * math.log2(math.e))).reshape(1, dc)
    kg = jnp.tile(k_norm_g[l], reps).reshape(1, dc)
    grp = jnp.arange(MXU_TILE) // HEAD_DIM
    gmat = jnp.where(grp[:, None] == grp[None, :], 1.0 / HEAD_DIM, 0.0).astype(BF16)
    yc, qT, k, vT = _inproj(x2, attn_norm_g[l].reshape(1, d), w_in[l].astype(BF16), conv_w[l],
                          conv_out_g[l].reshape(1, dc), qg, kg, gmat, batch=batch, seq=seq)

    lam_params = jnp.stack([lambda_q1[l], lambda_k1[l], lambda_q2[l], lambda_k2[l]])
    at = _attention(lam_params, attn_subln_g[l].reshape(V_DIM, 1), qT, k, vT,
                    batch=batch, seq=seq, lam_init=lam_init)

    wr = jnp.concatenate([w_router_expert[l], w_router_group[l],
                          jnp.zeros((d, LANES - N_EXPERTS - N_GROUPS), F32)], axis=1)
    wr_hi = wr.astype(BF16)
    wr_lo = (wr - wr_hi.astype(F32)).astype(BF16)
    hres, hn2, route, routeT, cnt = _outproj_router(x2, yc, at, w_out[l].astype(BF16),
                                                    ffn_norm_g[l].reshape(1, d), wr_hi, wr_lo)

    tmm = TM_MOE
    n_tiles_max = (2 * t) // tmm + N_EXPERTS
    p_rows = n_tiles_max * tmm
    counts = cnt[:, 0].astype(jnp.int32)
    tiles = (counts + tmm - 1) // tmm
    tile_end = jnp.cumsum(tiles)
    offs = (tile_end - tiles) * tmm
    pos = _positions(offs, routeT)
    n_tiles = tile_end[-1:].astype(jnp.int32)
    tile_ids = jnp.arange(n_tiles_max, dtype=jnp.int32)
    tile_expert = jnp.minimum(
        jnp.sum((tile_end[None, :] <= tile_ids[:, None]).astype(jnp.int32), axis=1), N_EXPERTS - 1)

    posw = pos.reshape(2 * t // SC_WIN, SC_WIN)
    pos1w = posw[:t // SC_WIN]
    pos2w = posw[t // SC_WIN:]
    xs = _sc_dispatch(hn2, pos1w, pos2w, p_rows)

    f = w_exp_gate.shape[-1]
    ys = _moe(tile_expert, n_tiles, xs,
              w_exp_gate[l].reshape(N_EXPERTS, d, f),
              w_exp_up[l].reshape(N_EXPERTS, d, f),
              w_exp_down[l].reshape(N_EXPERTS, f, d))
    yg = _sc_gather(ys, posw)
    out = _combine(hres, yg, route)
    return out.reshape(batch, seq, d)


def kernel(x, attn_norm_g, w_in, conv_w, conv_out_g, q_norm_g, k_norm_g, lambda_q1, lambda_k1,
           lambda_q2, lambda_k2, attn_subln_g, w_out, ffn_norm_g, w_router_group, w_router_expert,
           w_exp_gate, w_exp_up, w_exp_down):
    h = x
    for l in range(attn_norm_g.shape[0]):
        h = _layer(h, l, attn_norm_g, w_in, conv_w, conv_out_g, q_norm_g, k_norm_g,
                   lambda_q1, lambda_k1, lambda_q2, lambda_k2, attn_subln_g, w_out,
                   ffn_norm_g, w_router_group, w_router_expert, w_exp_gate, w_exp_up, w_exp_down)
    return h
```

```python
import functools
import math

import jax
import jax.numpy as jnp
from jax import lax
from jax.experimental import pallas as pl
from jax.experimental.pallas import tpu as pltpu
from jax.experimental.pallas import tpu_sc as plsc

F32 = jnp.float32
BF16 = jnp.bfloat16

HEAD_DIM = 64
V_DIM = 2 * HEAD_DIM
CONV_GROUPS = 8
N_GROUPS = 4
EXPERTS_PER_GROUP = 8
N_EXPERTS = N_GROUPS * EXPERTS_PER_GROUP
EPS = 1e-6
LANES = 128
MXU_TILE = 256
ONES_ROWS = 16
VMEM_LIMIT = 48 * 1024 * 1024

TM_PROJ = 256
TQ = 512
TK = 512
TM_MOE = 256
SC_CORES = 2
SC_SUBCORES = 16
SC_WIN = 64


def _dot(a, b):
    return jnp.dot(a, b, preferred_element_type=F32)


def _pack_rows(x):
    w = x.shape[1] // 2
    bits = lax.bitcast_convert_type(x.astype(BF16).astype(F32), jnp.uint32)
    return lax.bitcast_convert_type((bits[:, :w] >> 16) | bits[:, w:], jnp.int32)


def _unpack_rows(packed):
    bits = lax.bitcast_convert_type(packed, jnp.uint32)
    left = lax.bitcast_convert_type(bits << 16, F32)
    right = lax.bitcast_convert_type(bits & jnp.uint32(0xFFFF0000), F32)
    return left, right


def _group_mean(sq, gm):
    w = gm.shape[0]
    sq = sq.astype(BF16)
    return jnp.concatenate([_dot(sq[:, c:c + w], gm) for c in range(0, sq.shape[1], w)], axis=1)


def _inproj_kernel(x_ref, g_ref, w_ref, cw_ref, cg_ref, qg_ref, kg_ref, gm_ref,
                   yc_ref, qT_ref, k_ref, vT_ref, carry_ref, *, tm, dc):
    j = pl.program_id(1)

    @pl.when(j == 0)
    def _():
        carry_ref[...] = jnp.zeros_like(carry_ref)

    x = x_ref[...]
    ms = jnp.mean(x * x, axis=-1, keepdims=True)
    hn = (x * lax.rsqrt(ms + EPS) * g_ref[...]).astype(BF16)

    def proj(s):
        return _dot(hn, w_ref[:, s * dc:(s + 1) * dc])

    gm = gm_ref[...]

    u = proj(2) * proj(0)
    prev = carry_ref[...]
    rows = lax.broadcasted_iota(jnp.int32, u.shape, 0)
    u1 = jnp.where(rows == 0, prev[7:8, :], pltpu.roll(u, 1, 0))
    u2 = jnp.where(rows == 0, prev[6:7, :], jnp.where(rows == 1, prev[7:8, :], pltpu.roll(u, 2, 0)))
    carry_ref[...] = u[tm - 8:tm, :]
    cw = cw_ref[...]
    y = proj(1) * (cw[0:1, :] * u2 + cw[1:2, :] * u1 + cw[2:3, :] * u)
    yc_ref[...] = (y * lax.rsqrt(_group_mean(y * y, gm) + EPS) * cg_ref[...]).astype(BF16)

    q = proj(3)
    qT_ref[...] = (q * lax.rsqrt(_group_mean(q * q, gm) + EPS) * qg_ref[...]).T.astype(BF16)
    k = proj(4)
    k_ref[...] = (k * lax.rsqrt(_group_mean(k * k, gm) + EPS) * kg_ref[...]).astype(BF16)
    vt = proj(5).T.astype(BF16)
    vrows = V_DIM + ONES_ROWS
    for h in range(dc // V_DIM):
        vT_ref[0, h * vrows:h * vrows + V_DIM, :] = vt[h * V_DIM:(h + 1) * V_DIM, :]
        vT_ref[0, h * vrows + V_DIM:(h + 1) * vrows, :] = jnp.ones((ONES_ROWS, tm), BF16)


def _inproj(x2, g, w_in, conv_w, conv_g, qg, kg, gmat, *, batch, seq):
    t, d = x2.shape
    dc = conv_g.shape[1]
    dv = dc // V_DIM * (V_DIM + ONES_ROWS)
    tm = TK
    nj = seq // tm
    row = lambda b, j: (b * nj + j, 0)
    const = lambda b, j: (0, 0)
    out_sds = jax.ShapeDtypeStruct((t, dc), BF16)
    return pl.pallas_call(
        functools.partial(_inproj_kernel, tm=tm, dc=dc),
        grid=(batch, nj),
        in_specs=[
            pl.BlockSpec((tm, d), row),
            pl.BlockSpec((1, d), const),
            pl.BlockSpec(w_in.shape, const),
            pl.BlockSpec(conv_w.shape, const),
            pl.BlockSpec((1, dc), const),
            pl.BlockSpec((1, dc), const),
            pl.BlockSpec((1, dc), const),
            pl.BlockSpec(gmat.shape, const),
        ],
        out_specs=[
            pl.BlockSpec((tm, dc), row),
            pl.BlockSpec((dc, tm), lambda b, j: (0, b * nj + j)),
            pl.BlockSpec((tm, dc), row),
            pl.BlockSpec((1, dv, tm), lambda b, j: (b * nj + j, 0, 0)),
        ],
        out_shape=[out_sds, jax.ShapeDtypeStruct((dc, t), BF16), out_sds,
                   jax.ShapeDtypeStruct((t // tm, dv, tm), BF16)],
        scratch_shapes=[pltpu.VMEM((8, dc), F32)],
        compiler_params=pltpu.CompilerParams(
            dimension_semantics=("arbitrary", "arbitrary"), vmem_limit_bytes=VMEM_LIMIT),
        name="inproj_conv_qknorm",
    )(x2, g, w_in, conv_w, conv_g, qg, kg, gmat)


def _attn_kernel(lp_ref, sg_ref, qT_ref, k_ref, vT_ref, o_ref, m_ref, acc_ref,
                 sa_ref, pb_ref, ab_ref, *, tq, tk, lam_init):
    qi = pl.program_id(2)
    qT = qT_ref[...]
    row = lax.broadcasted_iota(jnp.int32, qT.shape, 0)
    zero = jnp.zeros_like(qT)
    qqT = jnp.concatenate([jnp.where(row < HEAD_DIM, qT, zero),
                           jnp.where(row >= HEAD_DIM, qT, zero)], axis=1)

    m_ref[...] = jnp.full_like(m_ref, -jnp.inf)
    acc_ref[...] = jnp.zeros_like(acc_ref)

    map_a = slice(0, tq)
    map_b = slice(tq, 2 * tq)

    def key_block(j):
        return k_ref[pl.ds(pl.multiple_of(j * tk, tk), tk), :]

    def softmax(cols, s, masked):
        if masked:
            qrel = lax.broadcasted_iota(jnp.int32, s.shape, 1)
            krel = lax.broadcasted_iota(jnp.int32, s.shape, 0)
            s = jnp.where(krel <= qrel, s, -jnp.inf)
        sb = s.astype(BF16)
        m_old = m_ref[:, cols]
        m_new = jnp.maximum(m_old, jnp.max(sb, axis=0, keepdims=True).astype(F32))
        alpha = jnp.exp2(m_old - m_new)
        p = jnp.exp2(sb - m_new.astype(BF16))
        m_ref[:, cols] = m_new
        return alpha, p

    def accumulate(cols, alpha, vb, p):
        acc_ref[:, cols] = alpha * acc_ref[:, cols] + _dot(vb, p)

    def block(j, last):
        accumulate(map_b, ab_ref[...], vT_ref[jnp.maximum(j - 1, 0)], pb_ref[...])
        s_b = _dot(key_block(j), qqT[:, map_b])
        alpha_a, p_a = softmax(map_a, sa_ref[...], last)
        accumulate(map_a, alpha_a, vT_ref[j], p_a)
        if not last:
            sa_ref[...] = _dot(key_block(j + 1), qqT[:, map_a])
        alpha_b, p_b = softmax(map_b, s_b, last)
        if last:
            accumulate(map_b, alpha_b, vT_ref[j], p_b)
        else:
            ab_ref[...] = alpha_b
            pb_ref[...] = p_b

    sa_ref[...] = _dot(key_block(0), qqT[:, map_a])
    pb_ref[...] = jnp.zeros_like(pb_ref)
    ab_ref[...] = jnp.ones_like(ab_ref)

    def body(jj, c):
        block(2 * jj, False)
        block(2 * jj + 1, False)
        return c

    lax.fori_loop(0, qi // 2, body, 0)

    @pl.when(qi % 2 == 1)
    def _():
        block(qi - 1, False)

    block(qi, True)

    lp = lp_ref[...]
    lam = (jnp.exp(jnp.sum(lp[0:1, :] * lp[1:2, :], axis=-1, keepdims=True))
           - jnp.exp(jnp.sum(lp[2:3, :] * lp[3:4, :], axis=-1, keepdims=True)) + lam_init)
    o = acc_ref[0:V_DIM, :] / acc_ref[V_DIM:V_DIM + 1, :]
    d = o[:, 0:tq] - lam * o[:, tq:2 * tq]
    ms = jnp.mean(d * d, axis=0, keepdims=True)
    o_ref[...] = (d * lax.rsqrt(ms + EPS) * sg_ref[...] * (1.0 - lam_init)).T.astype(BF16)


def _attention(lam_params, subln_g, qT, k, vT, *, batch, seq, lam_init):
    dq, t = qT.shape
    n_heads = dq // V_DIM
    tq, tk = TQ, TK
    assert tq == tk and vT.shape[2] == tk
    nq = seq // tq
    nk = seq // tk
    vrows = V_DIM + ONES_ROWS
    const = lambda b, h, i: (0, 0)
    return pl.pallas_call(
        functools.partial(_attn_kernel, tq=tq, tk=tk, lam_init=lam_init),
        grid=(batch, n_heads, nq),
        in_specs=[
            pl.BlockSpec(lam_params.shape, const),
            pl.BlockSpec((V_DIM, 1), const),
            pl.BlockSpec((V_DIM, tq), lambda b, h, i: (h, b * nq + i)),
            pl.BlockSpec((seq, V_DIM), lambda b, h, i: (b, h)),
            pl.BlockSpec((nk, vrows, tk), lambda b, h, i: (b, h, 0)),
        ],
        out_specs=pl.BlockSpec((tq, V_DIM), lambda b, h, i: (b * nq + i, h)),
        out_shape=jax.ShapeDtypeStruct((t, dq), BF16),
        scratch_shapes=[pltpu.VMEM((1, 2 * tq), F32),
                        pltpu.VMEM((vrows, 2 * tq), F32),
                        pltpu.VMEM((tk, tq), F32), pltpu.VMEM((tk, tq), BF16), pltpu.VMEM((1, tq), F32)],
        compiler_params=pltpu.CompilerParams(
            dimension_semantics=("arbitrary", "arbitrary", "arbitrary"), vmem_limit_bytes=VMEM_LIMIT),
        name="diff_attention",
    )(lam_params, subln_g, qT, k, vT)


def _outproj_router_kernel(x_ref, yc_ref, at_ref, wo_ref, g_ref, wrh_ref, wrl_ref,
                           h_ref, hn_ref, route_ref, routeT_ref, cnt_ref, *, tm, dc):
    i = pl.program_id(0)

    @pl.when(i == 0)
    def _():
        cnt_ref[...] = jnp.zeros_like(cnt_ref)

    h = x_ref[...] + _dot(yc_ref[...], wo_ref[0:dc, :]) + _dot(at_ref[...], wo_ref[dc:2 * dc, :])
    h_ref[...] = h
    ms = jnp.mean(h * h, axis=-1, keepdims=True)
    hn = h * lax.rsqrt(ms + EPS) * g_ref[...]
    hi = hn.astype(BF16)
    hn_ref[...] = _pack_rows(hn)
    lo = (hn - hi.astype(F32)).astype(BF16)
    logits = _dot(hi, wrh_ref[...]) + _dot(lo, wrh_ref[...]) + _dot(hi, wrl_ref[...])

    lt = logits.T
    neg = -jnp.inf
    grow = lax.broadcasted_iota(jnp.int32, (8, tm), 0).astype(F32)
    gl = jnp.where(grow < N_GROUPS, lt[N_EXPERTS:N_EXPERTS + 8, :], neg)
    gmax = jnp.max(gl, axis=0, keepdims=True)
    g_gate = 1.0 / jnp.sum(jnp.exp(gl - gmax), axis=0, keepdims=True)
    g_idx = jnp.min(jnp.where(gl == gmax, grow, 1e9), axis=0, keepdims=True)
    erow = lax.broadcasted_iota(jnp.int32, (N_EXPERTS, tm), 0).astype(F32)
    e_lo = g_idx * EXPERTS_PER_GROUP
    el = jnp.where((erow >= e_lo) & (erow < e_lo + EXPERTS_PER_GROUP), lt[0:N_EXPERTS, :], neg)
    v1 = jnp.max(el, axis=0, keepdims=True)
    i1 = jnp.min(jnp.where(el == v1, erow, 1e9), axis=0, keepdims=True)
    el2 = jnp.where(erow == i1, neg, el)
    v2 = jnp.max(el2, axis=0, keepdims=True)
    i2 = jnp.min(jnp.where(el2 == v2, erow, 1e9), axis=0, keepdims=True)
    tt = jnp.exp(v2 - v1)
    w1 = g_gate / (1.0 + tt)
    w2 = g_gate * tt / (1.0 + tt)

    sel1 = erow == i1
    sel2 = erow == i2
    oh = jnp.where(sel1 | sel2, 1.0, 0.0)
    ss = lax.broadcasted_iota(jnp.int32, (tm, tm), 0)
    tt_i = lax.broadcasted_iota(jnp.int32, (tm, tm), 1)
    earlier = jnp.where(ss < tt_i, 1.0, 0.0).astype(BF16)
    ranks = _dot(oh.astype(BF16), earlier) + cnt_ref[...]
    r1 = jnp.sum(jnp.where(sel1, ranks, 0.0), axis=0, keepdims=True)
    r2 = jnp.sum(jnp.where(sel2, ranks, 0.0), axis=0, keepdims=True)
    cnt_ref[...] = cnt_ref[...] + jnp.sum(oh, axis=1, keepdims=True)

    routeT = jnp.concatenate([i1, i2, r1, r2, w1, w2, jnp.zeros((2, tm), F32)], axis=0)
    routeT_ref[...] = routeT
    route_ref[...] = jnp.concatenate([routeT, jnp.zeros((LANES - 8, tm), F32)], axis=0).T


def _outproj_router(x2, yc, at, w_out, g, wr_hi, wr_lo):
    t, d = x2.shape
    dc = yc.shape[1]
    tm = TM_PROJ
    row = lambda i: (i, 0)
    const = lambda i: (0, 0)
    return pl.pallas_call(
        functools.partial(_outproj_router_kernel, tm=tm, dc=dc),
        grid=(t // tm,),
        in_specs=[
            pl.BlockSpec((tm, d), row),
            pl.BlockSpec((tm, dc), row),
            pl.BlockSpec((tm, dc), row),
            pl.BlockSpec(w_out.shape, const),
            pl.BlockSpec((1, d), const),
            pl.BlockSpec(wr_hi.shape, const),
            pl.BlockSpec(wr_lo.shape, const),
        ],
        out_specs=[
            pl.BlockSpec((tm, d), row),
            pl.BlockSpec((tm, d // 2), row),
            pl.BlockSpec((tm, LANES), row),
            pl.BlockSpec((8, tm), lambda i: (0, i)),
            pl.BlockSpec((N_EXPERTS, 1), const),
        ],
        out_shape=[
            jax.ShapeDtypeStruct((t, d), F32),
            jax.ShapeDtypeStruct((t, d // 2), jnp.int32),
            jax.ShapeDtypeStruct((t, LANES), F32),
            jax.ShapeDtypeStruct((8, t), F32),
            jax.ShapeDtypeStruct((N_EXPERTS, 1), F32),
        ],
        compiler_params=pltpu.CompilerParams(
            dimension_semantics=("arbitrary",), vmem_limit_bytes=VMEM_LIMIT),
        name="outproj_router",
    )(x2, yc, at, w_out, g, wr_hi, wr_lo)


def _positions_kernel(offs_ref, rt_ref, pos_ref):
    rt = rt_ref[...]
    ea, eb = rt[0:1, :], rt[1:2, :]
    sa = jnp.zeros_like(ea)
    sb = jnp.zeros_like(eb)
    for e in range(N_EXPERTS):
        start = offs_ref[e].astype(F32)
        sa = jnp.where(ea == e, start, sa)
        sb = jnp.where(eb == e, start, sb)
    pos_ref[0:1, :] = (sa + rt[2:3, :]).astype(jnp.int32)
    pos_ref[1:2, :] = (sb + rt[3:4, :]).astype(jnp.int32)


def _positions(offs, routeT):
    t = routeT.shape[1]
    return pl.pallas_call(
        _positions_kernel,
        grid_spec=pltpu.PrefetchScalarGridSpec(
            num_scalar_prefetch=1, grid=(1,),
            in_specs=[pl.BlockSpec(routeT.shape, lambda i, offs: (0, 0))],
            out_specs=pl.BlockSpec((2, t), lambda i, offs: (0, 0)),
        ),
        out_shape=jax.ShapeDtypeStruct((2, t), jnp.int32),
        name="positions",
    )(offs, routeT)


def _moe_kernel(te_ref, nt_ref, x_ref, wg_ref, wu_ref, wd_ref, y_ref, wgb_ref, wub_ref, wdb_ref):
    i = pl.program_id(0)
    last = nt_ref[0] - 1
    expert = te_ref[jnp.minimum(i, last)]
    prev_expert = te_ref[jnp.minimum(jnp.maximum(i - 1, 0), last)]

    @pl.when((i == 0) | (expert != prev_expert))
    def _():
        wgb_ref[...] = wg_ref[0].astype(BF16)
        wub_ref[...] = wu_ref[0].astype(BF16)
        wdb_ref[...] = wd_ref[0].astype(BF16)

    @pl.when(i <= last)
    def _():
        x_l, x_r = _unpack_rows(x_ref[...])
        x_l = x_l.astype(BF16)
        x_r = x_r.astype(BF16)
        half = x_l.shape[1]
        hg = _dot(x_l, wgb_ref[0:half, :]) + _dot(x_r, wgb_ref[half:2 * half, :])
        hu = _dot(x_l, wub_ref[0:half, :]) + _dot(x_r, wub_ref[half:2 * half, :])
        act = hg * (1.0 / (1.0 + jnp.exp(-hg))) * hu
        y_ref[...] = _pack_rows(_dot(act.astype(BF16), wdb_ref[...]))


def _moe(tile_expert, n_tiles, xs, wg, wu, wd):
    p, dp = xs.shape
    d, f = wg.shape[1], wg.shape[2]
    assert dp * 2 == d
    tm = TM_MOE
    row = lambda i, te, nt: (jnp.minimum(i, nt[0] - 1), 0)
    wsel = lambda i, te, nt: (te[jnp.minimum(i, nt[0] - 1)], 0, 0)
    grid_spec = pltpu.PrefetchScalarGridSpec(
        num_scalar_prefetch=2,
        grid=(p // tm,),
        in_specs=[
            pl.BlockSpec((tm, dp), row),
            pl.BlockSpec((1, d, f), wsel),
            pl.BlockSpec((1, d, f), wsel),
            pl.BlockSpec((1, f, d), wsel),
        ],
        out_specs=pl.BlockSpec((tm, dp), row),
        scratch_shapes=[pltpu.VMEM((d, f), BF16), pltpu.VMEM((d, f), BF16), pltpu.VMEM((f, d), BF16)],
    )
    return pl.pallas_call(
        _moe_kernel,
        grid_spec=grid_spec,
        out_shape=jax.ShapeDtypeStruct((p, dp), jnp.int32),
        compiler_params=pltpu.CompilerParams(
            dimension_semantics=("arbitrary",), vmem_limit_bytes=VMEM_LIMIT),
        name="moe_experts",
    )(tile_expert, n_tiles, xs, wg, wu, wd)


def _sc_mesh():
    return plsc.VectorSubcoreMesh(core_axis_name="c", subcore_axis_name="s",
                                  num_cores=SC_CORES, num_subcores=SC_SUBCORES)


def _sc_dispatch(rows, pos_a, pos_b, n_out):
    t, d = rows.shape
    win = pos_a.shape[1]

    @functools.partial(pl.kernel, out_type=jax.ShapeDtypeStruct((n_out, d), rows.dtype),
                       mesh=_sc_mesh(), scratch_types=[], name="sc_dispatch")
    def run(rows_hbm, pa_hbm, pb_hbm, out_hbm):
        def body(rows_vmem, pa_vmem, pb_vmem):
            pltpu.sync_copy(rows_vmem, out_hbm.at[pa_vmem.at[0]])
            pltpu.sync_copy(rows_vmem, out_hbm.at[pb_vmem.at[0]])

        pltpu.emit_pipeline(
            body, grid=(t // win,),
            in_specs=[pl.BlockSpec((win, d), lambda i: (i, 0)),
                      pl.BlockSpec((1, win), lambda i: (i, 0)),
                      pl.BlockSpec((1, win), lambda i: (i, 0))],
            out_specs=[],
            core_axis_name=("c", "s"),
            dimension_semantics=(pltpu.PARALLEL,),
        )(rows_hbm, pa_hbm, pb_hbm)

    return run(rows, pos_a, pos_b)


def _sc_gather(table, idx):
    d = table.shape[1]
    n_win, win = idx.shape

    @functools.partial(pl.kernel, out_type=jax.ShapeDtypeStruct((n_win * win, d), table.dtype),
                       mesh=_sc_mesh(), scratch_types=[], name="sc_gather")
    def run(table_hbm, idx_hbm, out_hbm):
        def body(idx_vmem, out_vmem):
            pltpu.sync_copy(table_hbm.at[idx_vmem.at[0]], out_vmem)

        pltpu.emit_pipeline(
            body, grid=(n_win,),
            in_specs=[pl.BlockSpec((1, win), lambda i: (i, 0))],
            out_specs=[pl.BlockSpec((win, d), lambda i: (i, 0))],
            core_axis_name=("c", "s"),
            dimension_semantics=(pltpu.PARALLEL,),
        )(idx_hbm, out_hbm)

    return run(table, idx)


def _combine_kernel(h_ref, ya_ref, yb_ref, r_ref, o_ref):
    r = r_ref[...]
    wa, wb = r[:, 4:5], r[:, 5:6]
    a_l, a_r = _unpack_rows(ya_ref[...])
    b_l, b_r = _unpack_rows(yb_ref[...])
    half = a_l.shape[1]
    o_ref[:, 0:half] = h_ref[:, 0:half] + wa * a_l + wb * b_l
    o_ref[:, half:2 * half] = h_ref[:, half:2 * half] + wa * a_r + wb * b_r


def _combine(hres, yg, route):
    t, d = hres.shape
    tm = TM_PROJ
    nb = t // tm
    return pl.pallas_call(
        _combine_kernel,
        grid=(nb,),
        in_specs=[pl.BlockSpec((tm, d), lambda i: (i, 0)),
                  pl.BlockSpec((tm, d // 2), lambda i: (i, 0)),
                  pl.BlockSpec((tm, d // 2), lambda i: (i + nb, 0)),
                  pl.BlockSpec((tm, LANES), lambda i: (i, 0))],
        out_specs=pl.BlockSpec((tm, d), lambda i: (i, 0)),
        out_shape=jax.ShapeDtypeStruct((t, d), F32),
        compiler_params=pltpu.CompilerParams(
            dimension_semantics=("arbitrary",), vmem_limit_bytes=VMEM_LIMIT),
        name="combine",
    )(hres, yg, yg, route)


def _lambda_init(layer_idx):
    return 0.8 - 0.6 * math.exp(-0.3 * layer_idx)


def _layer(h, l, attn_norm_g, w_in, conv_w, conv_out_g, q_norm_g, k_norm_g,
           lambda_q1, lambda_k1, lambda_q2, lambda_k2, attn_subln_g, w_out,
           ffn_norm_g, w_router_group, w_router_expert, w_exp_gate, w_exp_up, w_exp_down):
    batch, seq, d = h.shape
    t = batch * seq
    dc = conv_w.shape[-1]
    lam_init = _lambda_init(l)
    x2 = h.reshape(t, d)

    reps = dc // HEAD_DIM
    assert dc // CONV_GROUPS == HEAD_DIM
    qg = (jnp.tile(q_norm_g[l], reps) * (HEAD_DIM ** -0.5 * math.log2(math.e))).reshape(1, dc)
    kg = jnp.tile(k_norm_g[l], reps).reshape(1, dc)
    grp = jnp.arange(MXU_TILE) // HEAD_DIM
    gmat = jnp.where(grp[:, None] == grp[None, :], 1.0 / HEAD_DIM, 0.0).astype(BF16)
    yc, qT, k, vT = _inproj(x2, attn_norm_g[l].reshape(1, d), w_in[l].astype(BF16), conv_w[l],
                          conv_out_g[l].reshape(1, dc), qg, kg, gmat, batch=batch, seq=seq)

    lam_params = jnp.stack([lambda_q1[l], lambda_k1[l], lambda_q2[l], lambda_k2[l]])
    at = _attention(lam_params, attn_subln_g[l].reshape(V_DIM, 1), qT, k, vT,
                    batch=batch, seq=seq, lam_init=lam_init)

    wr = jnp.concatenate([w_router_expert[l], w_router_group[l],
                          jnp.zeros((d, LANES - N_EXPERTS - N_GROUPS), F32)], axis=1)
    wr_hi = wr.astype(BF16)
    wr_lo = (wr - wr_hi.astype(F32)).astype(BF16)
    hres, hn2, route, routeT, cnt = _outproj_router(x2, yc, at, w_out[l].astype(BF16),
                                                    ffn_norm_g[l].reshape(1, d), wr_hi, wr_lo)

    tmm = TM_MOE
    n_tiles_max = (2 * t) // tmm + N_EXPERTS
    p_rows = n_tiles_max * tmm
    counts = cnt[:, 0].astype(jnp.int32)
    tiles = (counts + tmm - 1) // tmm
    tile_end = jnp.cumsum(tiles)
    offs = (tile_end - tiles) * tmm
    pos = _positions(offs, routeT)
    n_tiles = tile_end[-1:].astype(jnp.int32)
    tile_ids = jnp.arange(n_tiles_max, dtype=jnp.int32)
    tile_expert = jnp.minimum(
        jnp.sum((tile_end[None, :] <= tile_ids[:, None]).astype(jnp.int32), axis=1), N_EXPERTS - 1)

    posw = pos.reshape(2 * t // SC_WIN, SC_WIN)
    pos1w = posw[:t // SC_WIN]
    pos2w = posw[t // SC_WIN:]
    xs = _sc_dispatch(hn2, pos1w, pos2w, p_rows)

    f = w_exp_gate.shape[-1]
    ys = _moe(tile_expert, n_tiles, xs,
              w_exp_gate[l].reshape(N_EXPERTS, d, f),
              w_exp_up[l].reshape(N_EXPERTS, d, f),
              w_exp_down[l].reshape(N_EXPERTS, f, d))
    yg = _sc_gather(ys, posw)
    out = _combine(hres, yg, route)
    return out.reshape(batch, seq, d)


def kernel(x, attn_norm_g, w_in, conv_w, conv_out_g, q_norm_g, k_norm_g, lambda_q1, lambda_k1,
           lambda_q2, lambda_k2, attn_subln_g, w_out, ffn_norm_g, w_router_group, w_router_expert,
           w_exp_gate, w_exp_up, w_exp_down):
    h = x
    for l in range(attn_norm_g.shape[0]):
        h = _layer(h, l, attn_norm_g, w_in, conv_w, conv_out_g, q_norm_g, k_norm_g,
                   lambda_q1, lambda_k1, lambda_q2, lambda_k2, attn_subln_g, w_out,
                   ffn_norm_g, w_router_group, w_router_expert, w_exp_gate, w_exp_up, w_exp_down)
    return h
```

```python
import functools
import math

import jax
import jax.numpy as jnp
from jax import lax
from jax.experimental import pallas as pl
from jax.experimental.pallas import tpu as pltpu
from jax.experimental.pallas import tpu_sc as plsc

F32 = jnp.float32
BF16 = jnp.bfloat16

HEAD_DIM = 64
V_DIM = 2 * HEAD_DIM
CONV_GROUPS = 8
N_GROUPS = 4
EXPERTS_PER_GROUP = 8
N_EXPERTS = N_GROUPS * EXPERTS_PER_GROUP
EPS = 1e-6
LANES = 128
MXU_TILE = 256
ONES_ROWS = 16
VMEM_LIMIT = 48 * 1024 * 1024

TM_PROJ = 256
TQ = 512
TK = 512
TM_MOE = 256
SC_CORES = 2
SC_SUBCORES = 16
SC_WIN = 64


def _dot(a, b):
    return jnp.dot(a, b, preferred_element_type=F32)


def _pack_rows(x):
    w = x.shape[1] // 2
    bits = lax.bitcast_convert_type(x.astype(BF16).astype(F32), jnp.uint32)
    return lax.bitcast_convert_type((bits[:, :w] >> 16) | bits[:, w:], jnp.int32)


def _unpack_rows(packed):
    bits = lax.bitcast_convert_type(packed, jnp.uint32)
    left = lax.bitcast_convert_type(bits << 16, F32)
    right = lax.bitcast_convert_type(bits & jnp.uint32(0xFFFF0000), F32)
    return left, right


def _group_mean(sq, gm):
    w = gm.shape[0]
    sq = sq.astype(BF16)
    return jnp.concatenate([_dot(sq[:, c:c + w], gm) for c in range(0, sq.shape[1], w)], axis=1)


def _inproj_kernel(x_ref, g_ref, w_ref, cw_ref, cg_ref, qg_ref, kg_ref, gm_ref,
                   yc_ref, qT_ref, k_ref, vT_ref, carry_ref, *, tm, dc):
    j = pl.program_id(1)

    @pl.when(j == 0)
    def _():
        carry_ref[...] = jnp.zeros_like(carry_ref)

    x = x_ref[...]
    ms = jnp.mean(x * x, axis=-1, keepdims=True)
    hn = (x * lax.rsqrt(ms + EPS) * g_ref[...]).astype(BF16)

    def proj(s):
        return _dot(hn, w_ref[:, s * dc:(s + 1) * dc])

    gm = gm_ref[...]

    u = proj(2) * proj(0)
    prev = carry_ref[...]
    rows = lax.broadcasted_iota(jnp.int32, u.shape, 0)
    u1 = jnp.where(rows == 0, prev[7:8, :], pltpu.roll(u, 1, 0))
    u2 = jnp.where(rows == 0, prev[6:7, :], jnp.where(rows == 1, prev[7:8, :], pltpu.roll(u, 2, 0)))
    carry_ref[...] = u[tm - 8:tm, :]
    cw = cw_ref[...]
    y = proj(1) * (cw[0:1, :] * u2 + cw[1:2, :] * u1 + cw[2:3, :] * u)
    yc_ref[...] = (y * lax.rsqrt(_group_mean(y * y, gm) + EPS) * cg_ref[...]).astype(BF16)

    q = proj(3)
    qT_ref[...] = (q * lax.rsqrt(_group_mean(q * q, gm) + EPS) * qg_ref[...]).T.astype(BF16)
    k = proj(4)
    k_ref[...] = (k * lax.rsqrt(_group_mean(k * k, gm) + EPS) * kg_ref[...]).astype(BF16)
    vt = proj(5).T.astype(BF16)
    vrows = V_DIM + ONES_ROWS
    for h in range(dc // V_DIM):
        vT_ref[0, h * vrows:h * vrows + V_DIM, :] = vt[h * V_DIM:(h + 1) * V_DIM, :]
        vT_ref[0, h * vrows + V_DIM:(h + 1) * vrows, :] = jnp.ones((ONES_ROWS, tm), BF16)


def _inproj(x2, g, w_in, conv_w, conv_g, qg, kg, gmat, *, batch, seq):
    t, d = x2.shape
    dc = conv_g.shape[1]
    dv = dc // V_DIM * (V_DIM + ONES_ROWS)
    tm = TK
    nj = seq // tm
    row = lambda b, j: (b * nj + j, 0)
    const = lambda b, j: (0, 0)
    out_sds = jax.ShapeDtypeStruct((t, dc), BF16)
    return pl.pallas_call(
        functools.partial(_inproj_kernel, tm=tm, dc=dc),
        grid=(batch, nj),
        in_specs=[
            pl.BlockSpec((tm, d), row),
            pl.BlockSpec((1, d), const),
            pl.BlockSpec(w_in.shape, const),
            pl.BlockSpec(conv_w.shape, const),
            pl.BlockSpec((1, dc), const),
            pl.BlockSpec((1, dc), const),
            pl.BlockSpec((1, dc), const),
            pl.BlockSpec(gmat.shape, const),
        ],
        out_specs=[
            pl.BlockSpec((tm, dc), row),
            pl.BlockSpec((dc, tm), lambda b, j: (0, b * nj + j)),
            pl.BlockSpec((tm, dc), row),
            pl.BlockSpec((1, dv, tm), lambda b, j: (b * nj + j, 0, 0)),
        ],
        out_shape=[out_sds, jax.ShapeDtypeStruct((dc, t), BF16), out_sds,
                   jax.ShapeDtypeStruct((t // tm, dv, tm), BF16)],
        scratch_shapes=[pltpu.VMEM((8, dc), F32)],
        compiler_params=pltpu.CompilerParams(
            dimension_semantics=("arbitrary", "arbitrary"), vmem_limit_bytes=VMEM_LIMIT),
        name="inproj_conv_qknorm",
    )(x2, g, w_in, conv_w, conv_g, qg, kg, gmat)


def _attn_kernel(lp_ref, sg_ref, qT_ref, k_ref, vT_ref, o_ref, m_ref, acc_ref,
                 sa_ref, pb_ref, ab_ref, *, tq, tk, nq, lam_init):
    map_a = slice(0, tq)
    map_b = slice(tq, 2 * tq)

    lp = lp_ref[...]
    lam = (jnp.exp(jnp.sum(lp[0:1, :] * lp[1:2, :], axis=-1, keepdims=True))
           - jnp.exp(jnp.sum(lp[2:3, :] * lp[3:4, :], axis=-1, keepdims=True)) + lam_init)

    def stacked_queries(qi):
        qT = qT_ref[:, qi * tq:(qi + 1) * tq]
        row = lax.broadcasted_iota(jnp.int32, qT.shape, 0)
        zero = jnp.zeros_like(qT)
        return jnp.concatenate([jnp.where(row < HEAD_DIM, qT, zero),
                                jnp.where(row >= HEAD_DIM, qT, zero)], axis=1)

    def key_block(j):
        return k_ref[j * tk:(j + 1) * tk, :]

    def softmax(m, cols, s, masked):
        if masked:
            qrel = lax.broadcasted_iota(jnp.int32, s.shape, 1)
            krel = lax.broadcasted_iota(jnp.int32, s.shape, 0)
            s = jnp.where(krel <= qrel, s, -jnp.inf)
        sb = s.astype(BF16)
        m_old = m[:, cols]
        m_new = jnp.maximum(m_old, jnp.max(sb, axis=0, keepdims=True).astype(F32))
        alpha = jnp.exp2(m_old - m_new)
        p = jnp.exp2(sb - m_new.astype(BF16))
        m[:, cols] = m_new
        return alpha, p

    def accumulate(acc, cols, alpha, vb, p):
        acc[:, cols] = alpha * acc[:, cols] + _dot(vb, p)

    stacked = {0: stacked_queries(0)}
    m_ref[0] = jnp.full(m_ref.shape[1:], -jnp.inf, F32)
    acc_ref[0] = jnp.zeros(acc_ref.shape[1:], F32)
    sa_ref[0] = _dot(key_block(0), stacked[0][:, map_a])

    for qi in range(nq):
        par = qi % 2
        m, acc, sa, qq = m_ref.at[par], acc_ref.at[par], sa_ref.at[par], stacked[qi]
        for j in range(qi + 1):
            last = j == qi
            if j > 0:
                accumulate(acc, map_b, ab_ref[...], vT_ref[j - 1], pb_ref[...])
            s_b = _dot(key_block(j), qq[:, map_b])
            alpha_a, p_a = softmax(m, map_a, sa[...], last)
            accumulate(acc, map_a, alpha_a, vT_ref[j], p_a)
            if not last:
                sa[...] = _dot(key_block(j + 1), qq[:, map_a])
            elif qi + 1 < nq:
                stacked[qi + 1] = stacked_queries(qi + 1)
                m_ref[1 - par] = jnp.full(m_ref.shape[1:], -jnp.inf, F32)
                acc_ref[1 - par] = jnp.zeros(acc_ref.shape[1:], F32)
                sa_ref[1 - par] = _dot(key_block(0), stacked[qi + 1][:, map_a])
            alpha_b, p_b = softmax(m, map_b, s_b, last)
            if last:
                accumulate(acc, map_b, alpha_b, vT_ref[j], p_b)
            else:
                ab_ref[...] = alpha_b
                pb_ref[...] = p_b

        o = acc[0:V_DIM, :] / acc[V_DIM:V_DIM + 1, :]
        d = o[:, map_a] - lam * o[:, map_b]
        ms = jnp.mean(d * d, axis=0, keepdims=True)
        o_ref[qi * tq:(qi + 1) * tq, :] = (
            d * lax.rsqrt(ms + EPS) * sg_ref[...] * (1.0 - lam_init)).T.astype(BF16)
        del stacked[qi]


def _attention(lam_params, subln_g, qT, k, vT, *, batch, seq, lam_init):
    dq, t = qT.shape
    n_heads = dq // V_DIM
    tq, tk = TQ, TK
    assert tq == tk and vT.shape[2] == tk
    nq = seq // tq
    nk = seq // tk
    vrows = V_DIM + ONES_ROWS
    const = lambda b, h: (0, 0)
    return pl.pallas_call(
        functools.partial(_attn_kernel, tq=tq, tk=tk, nq=nq, lam_init=lam_init),
        grid=(batch, n_heads),
        in_specs=[
            pl.BlockSpec(lam_params.shape, const),
            pl.BlockSpec((V_DIM, 1), const),
            pl.BlockSpec((V_DIM, seq), lambda b, h: (h, b)),
            pl.BlockSpec((seq, V_DIM), lambda b, h: (b, h)),
            pl.BlockSpec((nk, vrows, tk), lambda b, h: (b, h, 0)),
        ],
        out_specs=pl.BlockSpec((seq, V_DIM), lambda b, h: (b, h)),
        out_shape=jax.ShapeDtypeStruct((t, dq), BF16),
        scratch_shapes=[pltpu.VMEM((2, 1, 2 * tq), F32),
                        pltpu.VMEM((2, vrows, 2 * tq), F32),
                        pltpu.VMEM((2, tk, tq), F32), pltpu.VMEM((tk, tq), BF16), pltpu.VMEM((1, tq), F32)],
        compiler_params=pltpu.CompilerParams(
            dimension_semantics=("arbitrary", "arbitrary"), vmem_limit_bytes=VMEM_LIMIT),
        name="diff_attention",
    )(lam_params, subln_g, qT, k, vT)


def _outproj_router_kernel(x_ref, yc_ref, at_ref, wo_ref, g_ref, wrh_ref, wrl_ref,
                           h_ref, hn_ref, route_ref, routeT_ref, cnt_ref, *, tm, dc):
    i = pl.program_id(0)

    @pl.when(i == 0)
    def _():
        cnt_ref[...] = jnp.zeros_like(cnt_ref)

    h = x_ref[...] + _dot(yc_ref[...], wo_ref[0:dc, :]) + _dot(at_ref[...], wo_ref[dc:2 * dc, :])
    h_ref[...] = h
    ms = jnp.mean(h * h, axis=-1, keepdims=True)
    hn = h * lax.rsqrt(ms + EPS) * g_ref[...]
    hi = hn.astype(BF16)
    hn_ref[...] = _pack_rows(hn)
    lo = (hn - hi.astype(F32)).astype(BF16)
    logits = _dot(hi, wrh_ref[...]) + _dot(lo, wrh_ref[...]) + _dot(hi, wrl_ref[...])

    lt = logits.T
    neg = -jnp.inf
    grow = lax.broadcasted_iota(jnp.int32, (8, tm), 0).astype(F32)
    gl = jnp.where(grow < N_GROUPS, lt[N_EXPERTS:N_EXPERTS + 8, :], neg)
    gmax = jnp.max(gl, axis=0, keepdims=True)
    g_gate = 1.0 / jnp.sum(jnp.exp(gl - gmax), axis=0, keepdims=True)
    g_idx = jnp.min(jnp.where(gl == gmax, grow, 1e9), axis=0, keepdims=True)
    erow = lax.broadcasted_iota(jnp.int32, (N_EXPERTS, tm), 0).astype(F32)
    e_lo = g_idx * EXPERTS_PER_GROUP
    el = jnp.where((erow >= e_lo) & (erow < e_lo + EXPERTS_PER_GROUP), lt[0:N_EXPERTS, :], neg)
    v1 = jnp.max(el, axis=0, keepdims=True)
    i1 = jnp.min(jnp.where(el == v1, erow, 1e9), axis=0, keepdims=True)
    el2 = jnp.where(erow == i1, neg, el)
    v2 = jnp.max(el2, axis=0, keepdims=True)
    i2 = jnp.min(jnp.where(el2 == v2, erow, 1e9), axis=0, keepdims=True)
    tt = jnp.exp(v2 - v1)
    w1 = g_gate / (1.0 + tt)
    w2 = g_gate * tt / (1.0 + tt)

    sel1 = erow == i1
    sel2 = erow == i2
    oh = jnp.where(sel1 | sel2, 1.0, 0.0)
    ss = lax.broadcasted_iota(jnp.int32, (tm, tm), 0)
    tt_i = lax.broadcasted_iota(jnp.int32, (tm, tm), 1)
    earlier = jnp.where(ss < tt_i, 1.0, 0.0).astype(BF16)
    ranks = _dot(oh.astype(BF16), earlier) + cnt_ref[...]
    r1 = jnp.sum(jnp.where(sel1, ranks, 0.0), axis=0, keepdims=True)
    r2 = jnp.sum(jnp.where(sel2, ranks, 0.0), axis=0, keepdims=True)
    cnt_ref[...] = cnt_ref[...] + jnp.sum(oh, axis=1, keepdims=True)

    routeT = jnp.concatenate([i1, i2, r1, r2, w1, w2, jnp.zeros((2, tm), F32)], axis=0)
    routeT_ref[...] = routeT
    route_ref[...] = jnp.concatenate([routeT, jnp.zeros((LANES - 8, tm), F32)], axis=0).T


def _outproj_router(x2, yc, at, w_out, g, wr_hi, wr_lo):
    t, d = x2.shape
    dc = yc.shape[1]
    tm = TM_PROJ
    row = lambda i: (i, 0)
    const = lambda i: (0, 0)
    return pl.pallas_call(
        functools.partial(_outproj_router_kernel, tm=tm, dc=dc),
        grid=(t // tm,),
        in_specs=[
            pl.BlockSpec((tm, d), row),
            pl.BlockSpec((tm, dc), row),
            pl.BlockSpec((tm, dc), row),
            pl.BlockSpec(w_out.shape, const),
            pl.BlockSpec((1, d), const),
            pl.BlockSpec(wr_hi.shape, const),
            pl.BlockSpec(wr_lo.shape, const),
        ],
        out_specs=[
            pl.BlockSpec((tm, d), row),
            pl.BlockSpec((tm, d // 2), row),
            pl.BlockSpec((tm, LANES), row),
            pl.BlockSpec((8, tm), lambda i: (0, i)),
            pl.BlockSpec((N_EXPERTS, 1), const),
        ],
        out_shape=[
            jax.ShapeDtypeStruct((t, d), F32),
            jax.ShapeDtypeStruct((t, d // 2), jnp.int32),
            jax.ShapeDtypeStruct((t, LANES), F32),
            jax.ShapeDtypeStruct((8, t), F32),
            jax.ShapeDtypeStruct((N_EXPERTS, 1), F32),
        ],
        compiler_params=pltpu.CompilerParams(
            dimension_semantics=("arbitrary",), vmem_limit_bytes=VMEM_LIMIT),
        name="outproj_router",
    )(x2, yc, at, w_out, g, wr_hi, wr_lo)


def _positions_kernel(offs_ref, rt_ref, pos_ref):
    rt = rt_ref[...]
    ea, eb = rt[0:1, :], rt[1:2, :]
    sa = jnp.zeros_like(ea)
    sb = jnp.zeros_like(eb)
    for e in range(N_EXPERTS):
        start = offs_ref[e].astype(F32)
        sa = jnp.where(ea == e, start, sa)
        sb = jnp.where(eb == e, start, sb)
    pos_ref[0:1, :] = (sa + rt[2:3, :]).astype(jnp.int32)
    pos_ref[1:2, :] = (sb + rt[3:4, :]).astype(jnp.int32)


def _positions(offs, routeT):
    t = routeT.shape[1]
    return pl.pallas_call(
        _positions_kernel,
        grid_spec=pltpu.PrefetchScalarGridSpec(
            num_scalar_prefetch=1, grid=(1,),
            in_specs=[pl.BlockSpec(routeT.shape, lambda i, offs: (0, 0))],
            out_specs=pl.BlockSpec((2, t), lambda i, offs: (0, 0)),
        ),
        out_shape=jax.ShapeDtypeStruct((2, t), jnp.int32),
        name="positions",
    )(offs, routeT)


def _moe_kernel(te_ref, nt_ref, x_ref, wg_ref, wu_ref, wd_ref, y_ref, wgb_ref, wub_ref, wdb_ref):
    i = pl.program_id(0)
    last = nt_ref[0] - 1
    expert = te_ref[jnp.minimum(i, last)]
    prev_expert = te_ref[jnp.minimum(jnp.maximum(i - 1, 0), last)]

    @pl.when((i == 0) | (expert != prev_expert))
    def _():
        wgb_ref[...] = wg_ref[0].astype(BF16)
        wub_ref[...] = wu_ref[0].astype(BF16)
        wdb_ref[...] = wd_ref[0].astype(BF16)

    @pl.when(i <= last)
    def _():
        x_l, x_r = _unpack_rows(x_ref[...])
        x_l = x_l.astype(BF16)
        x_r = x_r.astype(BF16)
        half = x_l.shape[1]
        hg = _dot(x_l, wgb_ref[0:half, :]) + _dot(x_r, wgb_ref[half:2 * half, :])
        hu = _dot(x_l, wub_ref[0:half, :]) + _dot(x_r, wub_ref[half:2 * half, :])
        act = hg * (1.0 / (1.0 + jnp.exp(-hg))) * hu
        y_ref[...] = _pack_rows(_dot(act.astype(BF16), wdb_ref[...]))


def _moe(tile_expert, n_tiles, xs, wg, wu, wd):
    p, dp = xs.shape
    d, f = wg.shape[1], wg.shape[2]
    assert dp * 2 == d
    tm = TM_MOE
    row = lambda i, te, nt: (jnp.minimum(i, nt[0] - 1), 0)
    wsel = lambda i, te, nt: (te[jnp.minimum(i, nt[0] - 1)], 0, 0)
    grid_spec = pltpu.PrefetchScalarGridSpec(
        num_scalar_prefetch=2,
        grid=(p // tm,),
        in_specs=[
            pl.BlockSpec((tm, dp), row),
            pl.BlockSpec((1, d, f), wsel),
            pl.BlockSpec((1, d, f), wsel),
            pl.BlockSpec((1, f, d), wsel),
        ],
        out_specs=pl.BlockSpec((tm, dp), row),
        scratch_shapes=[pltpu.VMEM((d, f), BF16), pltpu.VMEM((d, f), BF16), pltpu.VMEM((f, d), BF16)],
    )
    return pl.pallas_call(
        _moe_kernel,
        grid_spec=grid_spec,
        out_shape=jax.ShapeDtypeStruct((p, dp), jnp.int32),
        compiler_params=pltpu.CompilerParams(
            dimension_semantics=("arbitrary",), vmem_limit_bytes=VMEM_LIMIT),
        name="moe_experts",
    )(tile_expert, n_tiles, xs, wg, wu, wd)


def _sc_mesh():
    return plsc.VectorSubcoreMesh(core_axis_name="c", subcore_axis_name="s",
                                  num_cores=SC_CORES, num_subcores=SC_SUBCORES)


def _sc_dispatch(rows, pos_a, pos_b, n_out):
    t, d = rows.shape
    win = pos_a.shape[1]

    @functools.partial(pl.kernel, out_type=jax.ShapeDtypeStruct((n_out, d), rows.dtype),
                       mesh=_sc_mesh(), scratch_types=[], name="sc_dispatch")
    def run(rows_hbm, pa_hbm, pb_hbm, out_hbm):
        def body(rows_vmem, pa_vmem, pb_vmem):
            pltpu.sync_copy(rows_vmem, out_hbm.at[pa_vmem.at[0]])
            pltpu.sync_copy(rows_vmem, out_hbm.at[pb_vmem.at[0]])

        pltpu.emit_pipeline(
            body, grid=(t // win,),
            in_specs=[pl.BlockSpec((win, d), lambda i: (i, 0)),
                      pl.BlockSpec((1, win), lambda i: (i, 0)),
                      pl.BlockSpec((1, win), lambda i: (i, 0))],
            out_specs=[],
            core_axis_name=("c", "s"),
            dimension_semantics=(pltpu.PARALLEL,),
        )(rows_hbm, pa_hbm, pb_hbm)

    return run(rows, pos_a, pos_b)


def _sc_gather(table, idx):
    d = table.shape[1]
    n_win, win = idx.shape

    @functools.partial(pl.kernel, out_type=jax.ShapeDtypeStruct((n_win * win, d), table.dtype),
                       mesh=_sc_mesh(), scratch_types=[], name="sc_gather")
    def run(table_hbm, idx_hbm, out_hbm):
        def body(idx_vmem, out_vmem):
            pltpu.sync_copy(table_hbm.at[idx_vmem.at[0]], out_vmem)

        pltpu.emit_pipeline(
            body, grid=(n_win,),
            in_specs=[pl.BlockSpec((1, win), lambda i: (i, 0))],
            out_specs=[pl.BlockSpec((win, d), lambda i: (i, 0))],
            core_axis_name=("c", "s"),
            dimension_semantics=(pltpu.PARALLEL,),
        )(idx_hbm, out_hbm)

    return run(table, idx)


def _combine_kernel(h_ref, ya_ref, yb_ref, r_ref, o_ref):
    r = r_ref[...]
    wa, wb = r[:, 4:5], r[:, 5:6]
    a_l, a_r = _unpack_rows(ya_ref[...])
    b_l, b_r = _unpack_rows(yb_ref[...])
    half = a_l.shape[1]
    o_ref[:, 0:half] = h_ref[:, 0:half] + wa * a_l + wb * b_l
    o_ref[:, half:2 * half] = h_ref[:, half:2 * half] + wa * a_r + wb * b_r


def _combine(hres, yg, route):
    t, d = hres.shape
    tm = TM_PROJ
    nb = t // tm
    return pl.pallas_call(
        _combine_kernel,
        grid=(nb,),
        in_specs=[pl.BlockSpec((tm, d), lambda i: (i, 0)),
                  pl.BlockSpec((tm, d // 2), lambda i: (i, 0)),
                  pl.BlockSpec((tm, d // 2), lambda i: (i + nb, 0)),
                  pl.BlockSpec((tm, LANES), lambda i: (i, 0))],
        out_specs=pl.BlockSpec((tm, d), lambda i: (i, 0)),
        out_shape=jax.ShapeDtypeStruct((t, d), F32),
        compiler_params=pltpu.CompilerParams(
            dimension_semantics=("arbitrary",), vmem_limit_bytes=VMEM_LIMIT),
        name="combine",
    )(hres, yg, yg, route)


def _lambda_init(layer_idx):
    return 0.8 - 0.6 * math.exp(-0.3 * layer_idx)


def _layer(h, l, attn_norm_g, w_in, conv_w, conv_out_g, q_norm_g, k_norm_g,
           lambda_q1, lambda_k1, lambda_q2, lambda_k2, attn_subln_g, w_out,
           ffn_norm_g, w_router_group, w_router_expert, w_exp_gate, w_exp_up, w_exp_down):
    batch, seq, d = h.shape
    t = batch * seq
    dc = conv_w.shape[-1]
    lam_init = _lambda_init(l)
    x2 = h.reshape(t, d)

    reps = dc // HEAD_DIM
    assert dc // CONV_GROUPS == HEAD_DIM
    qg = (jnp.tile(q_norm_g[l], reps) * (HEAD_DIM ** -0.5 * math.log2(math.e))).reshape(1, dc)
    kg = jnp.tile(k_norm_g[l], reps).reshape(1, dc)
    grp = jnp.arange(MXU_TILE) // HEAD_DIM
    gmat = jnp.where(grp[:, None] == grp[None, :], 1.0 / HEAD_DIM, 0.0).astype(BF16)
    yc, qT, k, vT = _inproj(x2, attn_norm_g[l].reshape(1, d), w_in[l].astype(BF16), conv_w[l],
                          conv_out_g[l].reshape(1, dc), qg, kg, gmat, batch=batch, seq=seq)

    lam_params = jnp.stack([lambda_q1[l], lambda_k1[l], lambda_q2[l], lambda_k2[l]])
    at = _attention(lam_params, attn_subln_g[l].reshape(V_DIM, 1), qT, k, vT,
                    batch=batch, seq=seq, lam_init=lam_init)

    wr = jnp.concatenate([w_router_expert[l], w_router_group[l],
                          jnp.zeros((d, LANES - N_EXPERTS - N_GROUPS), F32)], axis=1)
    wr_hi = wr.astype(BF16)
    wr_lo = (wr - wr_hi.astype(F32)).astype(BF16)
    hres, hn2, route, routeT, cnt = _outproj_router(x2, yc, at, w_out[l].astype(BF16),
                                                    ffn_norm_g[l].reshape(1, d), wr_hi, wr_lo)

    tmm = TM_MOE
    n_tiles_max = (2 * t) // tmm + N_EXPERTS
    p_rows = n_tiles_max * tmm
    counts = cnt[:, 0].astype(jnp.int32)
    tiles = (counts + tmm - 1) // tmm
    tile_end = jnp.cumsum(tiles)
    offs = (tile_end - tiles) * tmm
    pos = _positions(offs, routeT)
    n_tiles = tile_end[-1:].astype(jnp.int32)
    tile_ids = jnp.arange(n_tiles_max, dtype=jnp.int32)
    tile_expert = jnp.minimum(
        jnp.sum((tile_end[None, :] <= tile_ids[:, None]).astype(jnp.int32), axis=1), N_EXPERTS - 1)

    posw = pos.reshape(2 * t // SC_WIN, SC_WIN)
    pos1w = posw[:t // SC_WIN]
    pos2w = posw[t // SC_WIN:]
    xs = _sc_dispatch(hn2, pos1w, pos2w, p_rows)

    f = w_exp_gate.shape[-1]
    ys = _moe(tile_expert, n_tiles, xs,
              w_exp_gate[l].reshape(N_EXPERTS, d, f),
              w_exp_up[l].reshape(N_EXPERTS, d, f),
              w_exp_down[l].reshape(N_EXPERTS, f, d))
    yg = _sc_gather(ys, posw)
    out = _combine(hres, yg, route)
    return out.reshape(batch, seq, d)


def kernel(x, attn_norm_g, w_in, conv_w, conv_out_g, q_norm_g, k_norm_g, lambda_q1, lambda_k1,
           lambda_q2, lambda_k2, attn_subln_g, w_out, ffn_norm_g, w_router_group, w_router_expert,
           w_exp_gate, w_exp_up, w_exp_down):
    h = x
    for l in range(attn_norm_g.shape[0]):
        h = _layer(h, l, attn_norm_g, w_in, conv_w, conv_out_g, q_norm_g, k_norm_g,
                   lambda_q1, lambda_k1, lambda_q2, lambda_k2, attn_subln_g, w_out,
                   ffn_norm_g, w_router_group, w_router_expert, w_exp_gate, w_exp_up, w_exp_down)
    return h
```

```python
import functools
import math

import jax
import jax.numpy as jnp
from jax import lax
from jax.experimental import pallas as pl
from jax.experimental.pallas import tpu as pltpu
from jax.experimental.pallas import tpu_sc as plsc

F32 = jnp.float32
BF16 = jnp.bfloat16

HEAD_DIM = 64
V_DIM = 2 * HEAD_DIM
CONV_GROUPS = 8
N_GROUPS = 4
EXPERTS_PER_GROUP = 8
N_EXPERTS = N_GROUPS * EXPERTS_PER_GROUP
EPS = 1e-6
LANES = 128
MXU_TILE = 256
ONES_ROWS = 16
VMEM_LIMIT = 48 * 1024 * 1024

TM_PROJ = 256
TQ = 512
TK = 512
TM_MOE = 256
SC_CORES = 2
SC_SUBCORES = 16
SC_WIN = 64


def _dot(a, b):
    return jnp.dot(a, b, preferred_element_type=F32)


def _pack_rows(x):
    w = x.shape[1] // 2
    bits = lax.bitcast_convert_type(x.astype(BF16).astype(F32), jnp.uint32)
    return lax.bitcast_convert_type((bits[:, :w] >> 16) | bits[:, w:], jnp.int32)


def _unpack_rows(packed):
    bits = lax.bitcast_convert_type(packed, jnp.uint32)
    left = lax.bitcast_convert_type(bits << 16, F32)
    right = lax.bitcast_convert_type(bits & jnp.uint32(0xFFFF0000), F32)
    return left, right


def _group_mean(sq, gm):
    w = gm.shape[0]
    sq = sq.astype(BF16)
    return jnp.concatenate([_dot(sq[:, c:c + w], gm) for c in range(0, sq.shape[1], w)], axis=1)


def _inproj_kernel(x_ref, g_ref, w_ref, cw_ref, cg_ref, qg_ref, kg_ref, gm_ref,
                   yc_ref, qT_ref, k_ref, vT_ref, carry_ref, *, tm, dc):
    j = pl.program_id(1)

    @pl.when(j == 0)
    def _():
        carry_ref[...] = jnp.zeros_like(carry_ref)

    x = x_ref[...]
    ms = jnp.mean(x * x, axis=-1, keepdims=True)
    hn = (x * lax.rsqrt(ms + EPS) * g_ref[...]).astype(BF16)

    def proj(s):
        return _dot(hn, w_ref[:, s * dc:(s + 1) * dc])

    gm = gm_ref[...]

    u = proj(2) * proj(0)
    prev = carry_ref[...]
    rows = lax.broadcasted_iota(jnp.int32, u.shape, 0)
    u1 = jnp.where(rows == 0, prev[7:8, :], pltpu.roll(u, 1, 0))
    u2 = jnp.where(rows == 0, prev[6:7, :], jnp.where(rows == 1, prev[7:8, :], pltpu.roll(u, 2, 0)))
    carry_ref[...] = u[tm - 8:tm, :]
    cw = cw_ref[...]
    y = proj(1) * (cw[0:1, :] * u2 + cw[1:2, :] * u1 + cw[2:3, :] * u)
    yc_ref[...] = (y * lax.rsqrt(_group_mean(y * y, gm) + EPS) * cg_ref[...]).astype(BF16)

    q = proj(3)
    qT_ref[...] = (q * lax.rsqrt(_group_mean(q * q, gm) + EPS) * qg_ref[...]).T.astype(BF16)
    k = proj(4)
    k_ref[...] = (k * lax.rsqrt(_group_mean(k * k, gm) + EPS) * kg_ref[...]).astype(BF16)
    vt = proj(5).T.astype(BF16)
    vrows = V_DIM + ONES_ROWS
    for h in range(dc // V_DIM):
        vT_ref[0, h * vrows:h * vrows + V_DIM, :] = vt[h * V_DIM:(h + 1) * V_DIM, :]
        vT_ref[0, h * vrows + V_DIM:(h + 1) * vrows, :] = jnp.ones((ONES_ROWS, tm), BF16)


def _inproj(x2, g, w_in, conv_w, conv_g, qg, kg, gmat, *, batch, seq):
    t, d = x2.shape
    dc = conv_g.shape[1]
    dv = dc // V_DIM * (V_DIM + ONES_ROWS)
    tm = TK
    nj = seq // tm
    row = lambda b, j: (b * nj + j, 0)
    const = lambda b, j: (0, 0)
    out_sds = jax.ShapeDtypeStruct((t, dc), BF16)
    return pl.pallas_call(
        functools.partial(_inproj_kernel, tm=tm, dc=dc),
        grid=(batch, nj),
        in_specs=[
            pl.BlockSpec((tm, d), row),
            pl.BlockSpec((1, d), const),
            pl.BlockSpec(w_in.shape, const),
            pl.BlockSpec(conv_w.shape, const),
            pl.BlockSpec((1, dc), const),
            pl.BlockSpec((1, dc), const),
            pl.BlockSpec((1, dc), const),
            pl.BlockSpec(gmat.shape, const),
        ],
        out_specs=[
            pl.BlockSpec((tm, dc), row),
            pl.BlockSpec((dc, tm), lambda b, j: (0, b * nj + j)),
            pl.BlockSpec((tm, dc), row),
            pl.BlockSpec((1, dv, tm), lambda b, j: (b * nj + j, 0, 0)),
        ],
        out_shape=[out_sds, jax.ShapeDtypeStruct((dc, t), BF16), out_sds,
                   jax.ShapeDtypeStruct((t // tm, dv, tm), BF16)],
        scratch_shapes=[pltpu.VMEM((8, dc), F32)],
        compiler_params=pltpu.CompilerParams(
            dimension_semantics=("arbitrary", "arbitrary"), vmem_limit_bytes=VMEM_LIMIT),
        name="inproj_conv_qknorm",
    )(x2, g, w_in, conv_w, conv_g, qg, kg, gmat)


def _attn_kernel(lp_ref, sg_ref, qT_ref, k_ref, vT_ref, o_ref, m_ref, acc_ref,
                 sa_ref, pb_ref, ab_ref, *, tq, tk, nq, lam_init):
    map_a = slice(0, tq)
    map_b = slice(tq, 2 * tq)

    lp = lp_ref[...]
    lam = (jnp.exp(jnp.sum(lp[0:1, :] * lp[1:2, :], axis=-1, keepdims=True))
           - jnp.exp(jnp.sum(lp[2:3, :] * lp[3:4, :], axis=-1, keepdims=True)) + lam_init)

    def stacked_queries(qi):
        qT = qT_ref[:, qi * tq:(qi + 1) * tq]
        row = lax.broadcasted_iota(jnp.int32, qT.shape, 0)
        zero = jnp.zeros_like(qT)
        return jnp.concatenate([jnp.where(row < HEAD_DIM, qT, zero),
                                jnp.where(row >= HEAD_DIM, qT, zero)], axis=1)

    def key_block(j):
        return k_ref[j * tk:(j + 1) * tk, :]

    def softmax(m, cols, s, masked):
        if masked:
            qrel = lax.broadcasted_iota(jnp.int32, s.shape, 1)
            krel = lax.broadcasted_iota(jnp.int32, s.shape, 0)
            s = jnp.where(krel <= qrel, s, -jnp.inf)
        sb = s.astype(BF16)
        m_old = m[:, cols]
        m_new = jnp.maximum(m_old, jnp.max(sb, axis=0, keepdims=True).astype(F32))
        alpha = jnp.exp2(m_old - m_new)
        p = jnp.exp2(sb - m_new.astype(BF16))
        m[:, cols] = m_new
        return alpha, p

    def accumulate(acc, cols, alpha, vb, p):
        acc[:, cols] = alpha * acc[:, cols] + _dot(vb, p)

    stacked = {0: stacked_queries(0)}
    m_ref[0] = jnp.full(m_ref.shape[1:], -jnp.inf, F32)
    acc_ref[0] = jnp.zeros(acc_ref.shape[1:], F32)
    sa_ref[0] = _dot(key_block(0), stacked[0][:, map_a])

    for qi in range(nq):
        par = qi % 2
        m, acc, sa, qq = m_ref.at[par], acc_ref.at[par], sa_ref.at[par], stacked[qi]
        for j in range(qi + 1):
            last = j == qi
            if j > 0:
                accumulate(acc, map_b, ab_ref[...], vT_ref[j - 1], pb_ref[...])
            s_b = _dot(key_block(j), qq[:, map_b])
            alpha_a, p_a = softmax(m, map_a, sa[...], last)
            accumulate(acc, map_a, alpha_a, vT_ref[j], p_a)
            if not last:
                sa[...] = _dot(key_block(j + 1), qq[:, map_a])
            elif qi + 1 < nq:
                stacked[qi + 1] = stacked_queries(qi + 1)
                m_ref[1 - par] = jnp.full(m_ref.shape[1:], -jnp.inf, F32)
                acc_ref[1 - par] = jnp.zeros(acc_ref.shape[1:], F32)
                sa_ref[1 - par] = _dot(key_block(0), stacked[qi + 1][:, map_a])
            alpha_b, p_b = softmax(m, map_b, s_b, last)
            if last:
                accumulate(acc, map_b, alpha_b, vT_ref[j], p_b)
            else:
                ab_ref[...] = alpha_b
                pb_ref[...] = p_b

        o = acc[0:V_DIM, :] / acc[V_DIM:V_DIM + 1, :]
        d = o[:, map_a] - lam * o[:, map_b]
        ms = jnp.mean(d * d, axis=0, keepdims=True)
        o_ref[qi * tq:(qi + 1) * tq, :] = (
            d * lax.rsqrt(ms + EPS) * sg_ref[...] * (1.0 - lam_init)).T.astype(BF16)
        del stacked[qi]


def _attention(lam_params, subln_g, qT, k, vT, *, batch, seq, lam_init):
    dq, t = qT.shape
    n_heads = dq // V_DIM
    tq, tk = TQ, TK
    assert tq == tk and vT.shape[2] == tk
    nq = seq // tq
    nk = seq // tk
    vrows = V_DIM + ONES_ROWS
    const = lambda b, h: (0, 0)
    return pl.pallas_call(
        functools.partial(_attn_kernel, tq=tq, tk=tk, nq=nq, lam_init=lam_init),
        grid=(batch, n_heads),
        in_specs=[
            pl.BlockSpec(lam_params.shape, const),
            pl.BlockSpec((V_DIM, 1), const),
            pl.BlockSpec((V_DIM, seq), lambda b, h: (h, b)),
            pl.BlockSpec((seq, V_DIM), lambda b, h: (b, h)),
            pl.BlockSpec((nk, vrows, tk), lambda b, h: (b, h, 0)),
        ],
        out_specs=pl.BlockSpec((seq, V_DIM), lambda b, h: (b, h)),
        out_shape=jax.ShapeDtypeStruct((t, dq), BF16),
        scratch_shapes=[pltpu.VMEM((2, 1, 2 * tq), F32),
                        pltpu.VMEM((2, vrows, 2 * tq), F32),
                        pltpu.VMEM((2, tk, tq), F32), pltpu.VMEM((tk, tq), BF16), pltpu.VMEM((1, tq), F32)],
        compiler_params=pltpu.CompilerParams(
            dimension_semantics=("arbitrary", "arbitrary"), vmem_limit_bytes=VMEM_LIMIT),
        name="diff_attention",
    )(lam_params, subln_g, qT, k, vT)


def _outproj_router_kernel(x_ref, yc_ref, at_ref, wo_ref, g_ref, wrh_ref, wrl_ref,
                           h_ref, hn_ref, route_ref, routeT_ref, cnt_ref, *, tm, dc):
    i = pl.program_id(0)

    @pl.when(i == 0)
    def _():
        cnt_ref[...] = jnp.zeros_like(cnt_ref)

    h = x_ref[...] + _dot(yc_ref[...], wo_ref[0:dc, :]) + _dot(at_ref[...], wo_ref[dc:2 * dc, :])
    h_ref[...] = h
    ms = jnp.mean(h * h, axis=-1, keepdims=True)
    hn = h * lax.rsqrt(ms + EPS) * g_ref[...]
    hi = hn.astype(BF16)
    hn_ref[...] = _pack_rows(hn)
    lo = (hn - hi.astype(F32)).astype(BF16)
    logits = _dot(hi, wrh_ref[...]) + _dot(lo, wrh_ref[...]) + _dot(hi, wrl_ref[...])

    lt = logits.T
    neg = -jnp.inf
    grow = lax.broadcasted_iota(jnp.int32, (8, tm), 0).astype(F32)
    gl = jnp.where(grow < N_GROUPS, lt[N_EXPERTS:N_EXPERTS + 8, :], neg)
    gmax = jnp.max(gl, axis=0, keepdims=True)
    g_gate = 1.0 / jnp.sum(jnp.exp(gl - gmax), axis=0, keepdims=True)
    g_idx = jnp.min(jnp.where(gl == gmax, grow, 1e9), axis=0, keepdims=True)
    erow = lax.broadcasted_iota(jnp.int32, (N_EXPERTS, tm), 0).astype(F32)
    e_lo = g_idx * EXPERTS_PER_GROUP
    el = jnp.where((erow >= e_lo) & (erow < e_lo + EXPERTS_PER_GROUP), lt[0:N_EXPERTS, :], neg)
    v1 = jnp.max(el, axis=0, keepdims=True)
    i1 = jnp.min(jnp.where(el == v1, erow, 1e9), axis=0, keepdims=True)
    el2 = jnp.where(erow == i1, neg, el)
    v2 = jnp.max(el2, axis=0, keepdims=True)
    i2 = jnp.min(jnp.where(el2 == v2, erow, 1e9), axis=0, keepdims=True)
    tt = jnp.exp(v2 - v1)
    w1 = g_gate / (1.0 + tt)
    w2 = g_gate * tt / (1.0 + tt)

    sel1 = erow == i1
    sel2 = erow == i2
    oh = jnp.where(sel1 | sel2, 1.0, 0.0)
    ss = lax.broadcasted_iota(jnp.int32, (tm, tm), 0)
    tt_i = lax.broadcasted_iota(jnp.int32, (tm, tm), 1)
    earlier = jnp.where(ss < tt_i, 1.0, 0.0).astype(BF16)
    ranks = _dot(oh.astype(BF16), earlier) + cnt_ref[...]
    r1 = jnp.sum(jnp.where(sel1, ranks, 0.0), axis=0, keepdims=True)
    r2 = jnp.sum(jnp.where(sel2, ranks, 0.0), axis=0, keepdims=True)
    cnt_ref[...] = cnt_ref[...] + jnp.sum(oh, axis=1, keepdims=True)

    routeT = jnp.concatenate([i1, i2, r1, r2, w1, w2, jnp.zeros((2, tm), F32)], axis=0)
    routeT_ref[...] = routeT
    route_ref[...] = jnp.concatenate([routeT, jnp.zeros((LANES - 8, tm), F32)], axis=0).T


def _outproj_router(x2, yc, at, w_out, g, wr_hi, wr_lo):
    t, d = x2.shape
    dc = yc.shape[1]
    tm = TM_PROJ
    row = lambda i: (i, 0)
    const = lambda i: (0, 0)
    return pl.pallas_call(
        functools.partial(_outproj_router_kernel, tm=tm, dc=dc),
        grid=(t // tm,),
        in_specs=[
            pl.BlockSpec((tm, d), row),
            pl.BlockSpec((tm, dc), row),
            pl.BlockSpec((tm, dc), row),
            pl.BlockSpec(w_out.shape, const),
            pl.BlockSpec((1, d), const),
            pl.BlockSpec(wr_hi.shape, const),
            pl.BlockSpec(wr_lo.shape, const),
        ],
        out_specs=[
            pl.BlockSpec((tm, d), row),
            pl.BlockSpec((tm, d // 2), row),
            pl.BlockSpec((tm, LANES), row),
            pl.BlockSpec((8, tm), lambda i: (0, i)),
            pl.BlockSpec((N_EXPERTS, 1), const),
        ],
        out_shape=[
            jax.ShapeDtypeStruct((t, d), F32),
            jax.ShapeDtypeStruct((t, d // 2), jnp.int32),
            jax.ShapeDtypeStruct((t, LANES), F32),
            jax.ShapeDtypeStruct((8, t), F32),
            jax.ShapeDtypeStruct((N_EXPERTS, 1), F32),
        ],
        compiler_params=pltpu.CompilerParams(
            dimension_semantics=("arbitrary",), vmem_limit_bytes=VMEM_LIMIT),
        name="outproj_router",
    )(x2, yc, at, w_out, g, wr_hi, wr_lo)


def _positions_kernel(offs_ref, rt_ref, pos_ref):
    rt = rt_ref[...]
    ea, eb = rt[0:1, :], rt[1:2, :]
    sa = jnp.zeros_like(ea)
    sb = jnp.zeros_like(eb)
    for e in range(N_EXPERTS):
        start = offs_ref[e].astype(F32)
        sa = jnp.where(ea == e, start, sa)
        sb = jnp.where(eb == e, start, sb)
    pos_ref[0:1, :] = (sa + rt[2:3, :]).astype(jnp.int32)
    pos_ref[1:2, :] = (sb + rt[3:4, :]).astype(jnp.int32)


def _positions(offs, routeT):
    t = routeT.shape[1]
    return pl.pallas_call(
        _positions_kernel,
        grid_spec=pltpu.PrefetchScalarGridSpec(
            num_scalar_prefetch=1, grid=(1,),
            in_specs=[pl.BlockSpec(routeT.shape, lambda i, offs: (0, 0))],
            out_specs=pl.BlockSpec((2, t), lambda i, offs: (0, 0)),
        ),
        out_shape=jax.ShapeDtypeStruct((2, t), jnp.int32),
        name="positions",
    )(offs, routeT)


def _moe_kernel(te_ref, nt_ref, nx_ref, sl_ref, x_ref, wg_hbm, wu_hbm, wd_hbm, y_ref,
                wg_st, wu_st, wd_st, wgb_ref, wub_ref, wdb_ref, sem):
    i = pl.program_id(0)
    last = nt_ref[0] - 1
    ic = jnp.minimum(i, last)
    expert = te_ref[ic]
    slot = sl_ref[ic]
    first_of_expert = (i == 0) | ((i <= last) & (expert != te_ref[jnp.maximum(ic - 1, 0)]))

    def weight_copies(e, s):
        return (pltpu.make_async_copy(wg_hbm.at[e], wg_st.at[s], sem.at[s, 0]),
                pltpu.make_async_copy(wu_hbm.at[e], wu_st.at[s], sem.at[s, 1]),
                pltpu.make_async_copy(wd_hbm.at[e], wd_st.at[s], sem.at[s, 2]))

    @pl.when(i == 0)
    def _():
        for c in weight_copies(expert, slot):
            c.start()

    @pl.when(first_of_expert)
    def _():
        for c in weight_copies(expert, slot):
            c.wait()
        nxt = nx_ref[ic]

        @pl.when(nxt >= 0)
        def _():
            for c in weight_copies(nxt, 1 - slot):
                c.start()

        wgb_ref[...] = wg_st[slot].astype(BF16)
        wub_ref[...] = wu_st[slot].astype(BF16)
        wdb_ref[...] = wd_st[slot].astype(BF16)

    @pl.when(i <= last)
    def _():
        x_l, x_r = _unpack_rows(x_ref[...])
        x_l = x_l.astype(BF16)
        x_r = x_r.astype(BF16)
        half = x_l.shape[1]
        hg = _dot(x_l, wgb_ref[0:half, :]) + _dot(x_r, wgb_ref[half:2 * half, :])
        hu = _dot(x_l, wub_ref[0:half, :]) + _dot(x_r, wub_ref[half:2 * half, :])
        act = hg * (1.0 / (1.0 + jnp.exp(-hg))) * hu
        y_ref[...] = _pack_rows(_dot(act.astype(BF16), wdb_ref[...]))


def _moe(tile_expert, n_tiles, next_expert, stage_slot, xs, wg, wu, wd):
    p, dp = xs.shape
    d, f = wg.shape[1], wg.shape[2]
    assert dp * 2 == d
    tm = TM_MOE
    row = lambda i, te, nt, nx, sl: (jnp.minimum(i, nt[0] - 1), 0)
    hbm = pl.BlockSpec(memory_space=pl.ANY)
    grid_spec = pltpu.PrefetchScalarGridSpec(
        num_scalar_prefetch=4,
        grid=(p // tm,),
        in_specs=[pl.BlockSpec((tm, dp), row), hbm, hbm, hbm],
        out_specs=pl.BlockSpec((tm, dp), row),
        scratch_shapes=[pltpu.VMEM((2, d, f), F32), pltpu.VMEM((2, d, f), F32), pltpu.VMEM((2, f, d), F32),
                        pltpu.VMEM((d, f), BF16), pltpu.VMEM((d, f), BF16), pltpu.VMEM((f, d), BF16),
                        pltpu.SemaphoreType.DMA((2, 3))],
    )
    return pl.pallas_call(
        _moe_kernel,
        grid_spec=grid_spec,
        out_shape=jax.ShapeDtypeStruct((p, dp), jnp.int32),
        compiler_params=pltpu.CompilerParams(
            dimension_semantics=("arbitrary",), vmem_limit_bytes=VMEM_LIMIT),
        name="moe_experts",
    )(tile_expert, n_tiles, next_expert, stage_slot, xs, wg, wu, wd)


def _sc_mesh():
    return plsc.VectorSubcoreMesh(core_axis_name="c", subcore_axis_name="s",
                                  num_cores=SC_CORES, num_subcores=SC_SUBCORES)


def _sc_dispatch(rows, pos_a, pos_b, n_out):
    t, d = rows.shape
    win = pos_a.shape[1]

    @functools.partial(pl.kernel, out_type=jax.ShapeDtypeStruct((n_out, d), rows.dtype),
                       mesh=_sc_mesh(), scratch_types=[], name="sc_dispatch")
    def run(rows_hbm, pa_hbm, pb_hbm, out_hbm):
        def body(rows_vmem, pa_vmem, pb_vmem):
            pltpu.sync_copy(rows_vmem, out_hbm.at[pa_vmem.at[0]])
            pltpu.sync_copy(rows_vmem, out_hbm.at[pb_vmem.at[0]])

        pltpu.emit_pipeline(
            body, grid=(t // win,),
            in_specs=[pl.BlockSpec((win, d), lambda i: (i, 0)),
                      pl.BlockSpec((1, win), lambda i: (i, 0)),
                      pl.BlockSpec((1, win), lambda i: (i, 0))],
            out_specs=[],
            core_axis_name=("c", "s"),
            dimension_semantics=(pltpu.PARALLEL,),
        )(rows_hbm, pa_hbm, pb_hbm)

    return run(rows, pos_a, pos_b)


def _sc_gather(table, idx):
    d = table.shape[1]
    n_win, win = idx.shape

    @functools.partial(pl.kernel, out_type=jax.ShapeDtypeStruct((n_win * win, d), table.dtype),
                       mesh=_sc_mesh(), scratch_types=[], name="sc_gather")
    def run(table_hbm, idx_hbm, out_hbm):
        def body(idx_vmem, out_vmem):
            pltpu.sync_copy(table_hbm.at[idx_vmem.at[0]], out_vmem)

        pltpu.emit_pipeline(
            body, grid=(n_win,),
            in_specs=[pl.BlockSpec((1, win), lambda i: (i, 0))],
            out_specs=[pl.BlockSpec((win, d), lambda i: (i, 0))],
            core_axis_name=("c", "s"),
            dimension_semantics=(pltpu.PARALLEL,),
        )(idx_hbm, out_hbm)

    return run(table, idx)


def _combine_kernel(h_ref, ya_ref, yb_ref, r_ref, o_ref):
    r = r_ref[...]
    wa, wb = r[:, 4:5], r[:, 5:6]
    a_l, a_r = _unpack_rows(ya_ref[...])
    b_l, b_r = _unpack_rows(yb_ref[...])
    half = a_l.shape[1]
    o_ref[:, 0:half] = h_ref[:, 0:half] + wa * a_l + wb * b_l
    o_ref[:, half:2 * half] = h_ref[:, half:2 * half] + wa * a_r + wb * b_r


def _combine(hres, yg, route):
    t, d = hres.shape
    tm = TM_PROJ
    nb = t // tm
    return pl.pallas_call(
        _combine_kernel,
        grid=(nb,),
        in_specs=[pl.BlockSpec((tm, d), lambda i: (i, 0)),
                  pl.BlockSpec((tm, d // 2), lambda i: (i, 0)),
                  pl.BlockSpec((tm, d // 2), lambda i: (i + nb, 0)),
                  pl.BlockSpec((tm, LANES), lambda i: (i, 0))],
        out_specs=pl.BlockSpec((tm, d), lambda i: (i, 0)),
        out_shape=jax.ShapeDtypeStruct((t, d), F32),
        compiler_params=pltpu.CompilerParams(
            dimension_semantics=("arbitrary",), vmem_limit_bytes=VMEM_LIMIT),
        name="combine",
    )(hres, yg, yg, route)


def _lambda_init(layer_idx):
    return 0.8 - 0.6 * math.exp(-0.3 * layer_idx)


def _layer(h, l, attn_norm_g, w_in, conv_w, conv_out_g, q_norm_g, k_norm_g,
           lambda_q1, lambda_k1, lambda_q2, lambda_k2, attn_subln_g, w_out,
           ffn_norm_g, w_router_group, w_router_expert, w_exp_gate, w_exp_up, w_exp_down):
    batch, seq, d = h.shape
    t = batch * seq
    dc = conv_w.shape[-1]
    lam_init = _lambda_init(l)
    x2 = h.reshape(t, d)

    reps = dc // HEAD_DIM
    assert dc // CONV_GROUPS == HEAD_DIM
    qg = (jnp.tile(q_norm_g[l], reps) * (HEAD_DIM ** -0.5 * math.log2(math.e))).reshape(1, dc)
    kg = jnp.tile(k_norm_g[l], reps).reshape(1, dc)
    grp = jnp.arange(MXU_TILE) // HEAD_DIM
    gmat = jnp.where(grp[:, None] == grp[None, :], 1.0 / HEAD_DIM, 0.0).astype(BF16)
    yc, qT, k, vT = _inproj(x2, attn_norm_g[l].reshape(1, d), w_in[l].astype(BF16), conv_w[l],
                          conv_out_g[l].reshape(1, dc), qg, kg, gmat, batch=batch, seq=seq)

    lam_params = jnp.stack([lambda_q1[l], lambda_k1[l], lambda_q2[l], lambda_k2[l]])
    at = _attention(lam_params, attn_subln_g[l].reshape(V_DIM, 1), qT, k, vT,
                    batch=batch, seq=seq, lam_init=lam_init)

    wr = jnp.concatenate([w_router_expert[l], w_router_group[l],
                          jnp.zeros((d, LANES - N_EXPERTS - N_GROUPS), F32)], axis=1)
    wr_hi = wr.astype(BF16)
    wr_lo = (wr - wr_hi.astype(F32)).astype(BF16)
    hres, hn2, route, routeT, cnt = _outproj_router(x2, yc, at, w_out[l].astype(BF16),
                                                    ffn_norm_g[l].reshape(1, d), wr_hi, wr_lo)

    tmm = TM_MOE
    n_tiles_max = (2 * t) // tmm + N_EXPERTS
    p_rows = n_tiles_max * tmm
    counts = cnt[:, 0].astype(jnp.int32)
    tiles = (counts + tmm - 1) // tmm
    tile_end = jnp.cumsum(tiles)
    offs = (tile_end - tiles) * tmm
    pos = _positions(offs, routeT)
    n_tiles = tile_end[-1:].astype(jnp.int32)
    tile_ids = jnp.arange(n_tiles_max, dtype=jnp.int32)
    tile_expert = jnp.minimum(
        jnp.sum((tile_end[None, :] <= tile_ids[:, None]).astype(jnp.int32), axis=1), N_EXPERTS - 1)
    e_ids = jnp.arange(N_EXPERTS, dtype=jnp.int32)
    nonempty = tiles > 0
    later = jnp.where(nonempty[None, :] & (e_ids[None, :] > e_ids[:, None]), e_ids[None, :], N_EXPERTS)
    next_of = jnp.min(later, axis=1)
    next_of = jnp.where(next_of == N_EXPERTS, -1, next_of).astype(jnp.int32)
    slot_of = ((jnp.cumsum(nonempty.astype(jnp.int32)) - 1) % 2).astype(jnp.int32)
    onehot_te = (tile_expert[:, None] == e_ids[None, :]).astype(jnp.int32)
    next_expert = jnp.sum(onehot_te * next_of[None, :], axis=1)
    stage_slot = jnp.sum(onehot_te * slot_of[None, :], axis=1)

    posw = pos.reshape(2 * t // SC_WIN, SC_WIN)
    pos1w = posw[:t // SC_WIN]
    pos2w = posw[t // SC_WIN:]
    xs = _sc_dispatch(hn2, pos1w, pos2w, p_rows)

    f = w_exp_gate.shape[-1]
    ys = _moe(tile_expert, n_tiles, next_expert, stage_slot, xs,
              w_exp_gate[l].reshape(N_EXPERTS, d, f),
              w_exp_up[l].reshape(N_EXPERTS, d, f),
              w_exp_down[l].reshape(N_EXPERTS, f, d))
    yg = _sc_gather(ys, posw)
    out = _combine(hres, yg, route)
    return out.reshape(batch, seq, d)


def kernel(x, attn_norm_g, w_in, conv_w, conv_out_g, q_norm_g, k_norm_g, lambda_q1, lambda_k1,
           lambda_q2, lambda_k2, attn_subln_g, w_out, ffn_norm_g, w_router_group, w_router_expert,
           w_exp_gate, w_exp_up, w_exp_down):
    h = x
    for l in range(attn_norm_g.shape[0]):
        h = _layer(h, l, attn_norm_g, w_in, conv_w, conv_out_g, q_norm_g, k_norm_g,
                   lambda_q1, lambda_k1, lambda_q2, lambda_k2, attn_subln_g, w_out,
                   ffn_norm_g, w_router_group, w_router_expert, w_exp_gate, w_exp_up, w_exp_down)
    return h
```

```python
import functools
import math

import jax
import jax.numpy as jnp
from jax import lax
from jax.experimental import pallas as pl
from jax.experimental.pallas import tpu as pltpu
from jax.experimental.pallas import tpu_sc as plsc

F32 = jnp.float32
BF16 = jnp.bfloat16

HEAD_DIM = 64
V_DIM = 2 * HEAD_DIM
CONV_GROUPS = 8
N_GROUPS = 4
EXPERTS_PER_GROUP = 8
N_EXPERTS = N_GROUPS * EXPERTS_PER_GROUP
EPS = 1e-6
LANES = 128
MXU_TILE = 256
ONES_ROWS = 16
VMEM_LIMIT = 48 * 1024 * 1024

TM_PROJ = 256
TM_COMBINE = 1024
OUTPROJ_SUBTILES = 4
TQ = 512
TK = 512
TM_MOE = 256
SC_CORES = 2
SC_SUBCORES = 16
SC_WIN = 64


def _dot(a, b):
    return jnp.dot(a, b, preferred_element_type=F32)


def _pack_rows(x):
    w = x.shape[1] // 2
    bits = lax.bitcast_convert_type(x.astype(BF16).astype(F32), jnp.uint32)
    return lax.bitcast_convert_type((bits[:, :w] >> 16) | bits[:, w:], jnp.int32)


def _unpack_rows(packed):
    bits = lax.bitcast_convert_type(packed, jnp.uint32)
    left = lax.bitcast_convert_type(bits << 16, F32)
    right = lax.bitcast_convert_type(bits & jnp.uint32(0xFFFF0000), F32)
    return left, right


def _group_mean(sq, gm):
    w = gm.shape[0]
    sq = sq.astype(BF16)
    return jnp.concatenate([_dot(sq[:, c:c + w], gm) for c in range(0, sq.shape[1], w)], axis=1)


def _inproj_kernel(x_ref, g_ref, w_ref, cw_ref, cg_ref, qg_ref, kg_ref, gm_ref,
                   yc_ref, qT_ref, k_ref, vT_ref, carry_ref, *, tm, dc):
    j = pl.program_id(1)

    @pl.when(j == 0)
    def _():
        carry_ref[...] = jnp.zeros_like(carry_ref)

    x = x_ref[...]
    ms = jnp.mean(x * x, axis=-1, keepdims=True)
    hn = (x * lax.rsqrt(ms + EPS) * g_ref[...]).astype(BF16)

    def proj(s):
        return _dot(hn, w_ref[:, s * dc:(s + 1) * dc])

    gm = gm_ref[...]

    u = proj(2) * proj(0)
    prev = carry_ref[...]
    rows = lax.broadcasted_iota(jnp.int32, u.shape, 0)
    u1 = jnp.where(rows == 0, prev[7:8, :], pltpu.roll(u, 1, 0))
    u2 = jnp.where(rows == 0, prev[6:7, :], jnp.where(rows == 1, prev[7:8, :], pltpu.roll(u, 2, 0)))
    carry_ref[...] = u[tm - 8:tm, :]
    cw = cw_ref[...]
    y = proj(1) * (cw[0:1, :] * u2 + cw[1:2, :] * u1 + cw[2:3, :] * u)
    yc_ref[...] = (y * lax.rsqrt(_group_mean(y * y, gm) + EPS) * cg_ref[...]).astype(BF16)

    q = proj(3)
    qT_ref[...] = (q * lax.rsqrt(_group_mean(q * q, gm) + EPS) * qg_ref[...]).T.astype(BF16)
    k = proj(4)
    k_ref[...] = (k * lax.rsqrt(_group_mean(k * k, gm) + EPS) * kg_ref[...]).astype(BF16)
    vt = proj(5).T.astype(BF16)
    vrows = V_DIM + ONES_ROWS
    for h in range(dc // V_DIM):
        vT_ref[0, h * vrows:h * vrows + V_DIM, :] = vt[h * V_DIM:(h + 1) * V_DIM, :]
        vT_ref[0, h * vrows + V_DIM:(h + 1) * vrows, :] = jnp.ones((ONES_ROWS, tm), BF16)


def _inproj(x2, g, w_in, conv_w, conv_g, qg, kg, gmat, *, batch, seq):
    t, d = x2.shape
    dc = conv_g.shape[1]
    dv = dc // V_DIM * (V_DIM + ONES_ROWS)
    tm = TK
    nj = seq // tm
    row = lambda b, j: (b * nj + j, 0)
    const = lambda b, j: (0, 0)
    out_sds = jax.ShapeDtypeStruct((t, dc), BF16)
    return pl.pallas_call(
        functools.partial(_inproj_kernel, tm=tm, dc=dc),
        grid=(batch, nj),
        in_specs=[
            pl.BlockSpec((tm, d), row),
            pl.BlockSpec((1, d), const),
            pl.BlockSpec(w_in.shape, const),
            pl.BlockSpec(conv_w.shape, const),
            pl.BlockSpec((1, dc), const),
            pl.BlockSpec((1, dc), const),
            pl.BlockSpec((1, dc), const),
            pl.BlockSpec(gmat.shape, const),
        ],
        out_specs=[
            pl.BlockSpec((tm, dc), row),
            pl.BlockSpec((dc, tm), lambda b, j: (0, b * nj + j)),
            pl.BlockSpec((tm, dc), row),
            pl.BlockSpec((1, dv, tm), lambda b, j: (b * nj + j, 0, 0)),
        ],
        out_shape=[out_sds, jax.ShapeDtypeStruct((dc, t), BF16), out_sds,
                   jax.ShapeDtypeStruct((t // tm, dv, tm), BF16)],
        scratch_shapes=[pltpu.VMEM((8, dc), F32)],
        compiler_params=pltpu.CompilerParams(
            dimension_semantics=("arbitrary", "arbitrary"), vmem_limit_bytes=VMEM_LIMIT),
        name="inproj_conv_qknorm",
    )(x2, g, w_in, conv_w, conv_g, qg, kg, gmat)


def _attn_kernel(lp_ref, sg_ref, qT_ref, k_ref, vT_ref, o_ref, m_ref, acc_ref,
                 sa_ref, pb_ref, ab_ref, *, tq, tk, nq, lam_init):
    map_a = slice(0, tq)
    map_b = slice(tq, 2 * tq)

    lp = lp_ref[...]
    lam = (jnp.exp(jnp.sum(lp[0:1, :] * lp[1:2, :], axis=-1, keepdims=True))
           - jnp.exp(jnp.sum(lp[2:3, :] * lp[3:4, :], axis=-1, keepdims=True)) + lam_init)

    def stacked_queries(qi):
        qT = qT_ref[:, qi * tq:(qi + 1) * tq]
        row = lax.broadcasted_iota(jnp.int32, qT.shape, 0)
        zero = jnp.zeros_like(qT)
        return jnp.concatenate([jnp.where(row < HEAD_DIM, qT, zero),
                                jnp.where(row >= HEAD_DIM, qT, zero)], axis=1)

    def key_block(j):
        return k_ref[j * tk:(j + 1) * tk, :]

    def softmax(m, cols, s, masked):
        if masked:
            qrel = lax.broadcasted_iota(jnp.int32, s.shape, 1)
            krel = lax.broadcasted_iota(jnp.int32, s.shape, 0)
            s = jnp.where(krel <= qrel, s, -jnp.inf)
        sb = s.astype(BF16)
        m_old = m[:, cols]
        m_new = jnp.maximum(m_old, jnp.max(sb, axis=0, keepdims=True).astype(F32))
        alpha = jnp.exp2(m_old - m_new)
        p = jnp.exp2(sb - m_new.astype(BF16))
        m[:, cols] = m_new
        return alpha, p

    def accumulate(acc, cols, alpha, vb, p):
        acc[:, cols] = alpha * acc[:, cols] + _dot(vb, p)

    stacked = {0: stacked_queries(0)}
    m_ref[0] = jnp.full(m_ref.shape[1:], -jnp.inf, F32)
    acc_ref[0] = jnp.zeros(acc_ref.shape[1:], F32)
    sa_ref[0] = _dot(key_block(0), stacked[0][:, map_a])

    for qi in range(nq):
        par = qi % 2
        m, acc, sa, qq = m_ref.at[par], acc_ref.at[par], sa_ref.at[par], stacked[qi]
        for j in range(qi + 1):
            last = j == qi
            if j > 0:
                accumulate(acc, map_b, ab_ref[...], vT_ref[j - 1], pb_ref[...])
            s_b = _dot(key_block(j), qq[:, map_b])
            alpha_a, p_a = softmax(m, map_a, sa[...], last)
            accumulate(acc, map_a, alpha_a, vT_ref[j], p_a)
            if not last:
                sa[...] = _dot(key_block(j + 1), qq[:, map_a])
            elif qi + 1 < nq:
                stacked[qi + 1] = stacked_queries(qi + 1)
                m_ref[1 - par] = jnp.full(m_ref.shape[1:], -jnp.inf, F32)
                acc_ref[1 - par] = jnp.zeros(acc_ref.shape[1:], F32)
                sa_ref[1 - par] = _dot(key_block(0), stacked[qi + 1][:, map_a])
            alpha_b, p_b = softmax(m, map_b, s_b, last)
            if last:
                accumulate(acc, map_b, alpha_b, vT_ref[j], p_b)
            else:
                ab_ref[...] = alpha_b
                pb_ref[...] = p_b

        o = acc[0:V_DIM, :] / acc[V_DIM:V_DIM + 1, :]
        d = o[:, map_a] - lam * o[:, map_b]
        ms = jnp.mean(d * d, axis=0, keepdims=True)
        o_ref[qi * tq:(qi + 1) * tq, :] = (
            d * lax.rsqrt(ms + EPS) * sg_ref[...] * (1.0 - lam_init)).T.astype(BF16)
        del stacked[qi]


def _attention(lam_params, subln_g, qT, k, vT, *, batch, seq, lam_init):
    dq, t = qT.shape
    n_heads = dq // V_DIM
    tq, tk = TQ, TK
    assert tq == tk and vT.shape[2] == tk
    nq = seq // tq
    nk = seq // tk
    vrows = V_DIM + ONES_ROWS
    const = lambda b, h: (0, 0)
    return pl.pallas_call(
        functools.partial(_attn_kernel, tq=tq, tk=tk, nq=nq, lam_init=lam_init),
        grid=(batch, n_heads),
        in_specs=[
            pl.BlockSpec(lam_params.shape, const),
            pl.BlockSpec((V_DIM, 1), const),
            pl.BlockSpec((V_DIM, seq), lambda b, h: (h, b)),
            pl.BlockSpec((seq, V_DIM), lambda b, h: (b, h)),
            pl.BlockSpec((nk, vrows, tk), lambda b, h: (b, h, 0)),
        ],
        out_specs=pl.BlockSpec((seq, V_DIM), lambda b, h: (b, h)),
        out_shape=jax.ShapeDtypeStruct((t, dq), BF16),
        scratch_shapes=[pltpu.VMEM((2, 1, 2 * tq), F32),
                        pltpu.VMEM((2, vrows, 2 * tq), F32),
                        pltpu.VMEM((2, tk, tq), F32), pltpu.VMEM((tk, tq), BF16), pltpu.VMEM((1, tq), F32)],
        compiler_params=pltpu.CompilerParams(
            dimension_semantics=("arbitrary", "arbitrary"), vmem_limit_bytes=VMEM_LIMIT),
        name="diff_attention",
    )(lam_params, subln_g, qT, k, vT)


def _outproj_router_kernel(x_ref, yc_ref, at_ref, wo_ref, g_ref, wr_ref,
                           h_ref, hn_ref, route_ref, routeT_ref, cnt_ref, *, tm, dc):
    i = pl.program_id(0)

    @pl.when(i == 0)
    def _():
        cnt_ref[...] = jnp.zeros_like(cnt_ref)

    for r0 in range(0, x_ref.shape[0], tm):
        _outproj_router_subtile(slice(r0, r0 + tm), x_ref, yc_ref, at_ref, wo_ref, g_ref, wr_ref,
                                h_ref, hn_ref, route_ref, routeT_ref, cnt_ref, tm=tm, dc=dc)


def _outproj_router_subtile(rows, x_ref, yc_ref, at_ref, wo_ref, g_ref, wr_ref,
                            h_ref, hn_ref, route_ref, routeT_ref, cnt_ref, *, tm, dc):
    h = x_ref[rows, :] + _dot(yc_ref[rows, :], wo_ref[0:dc, :]) + _dot(at_ref[rows, :], wo_ref[dc:2 * dc, :])
    h_ref[rows, :] = h
    ms = jnp.mean(h * h, axis=-1, keepdims=True)
    hn = h * lax.rsqrt(ms + EPS) * g_ref[...]
    hi = hn.astype(BF16)
    hn_ref[rows, :] = _pack_rows(hn)
    lo = (hn - hi.astype(F32)).astype(BF16)
    w_hi = wr_ref[:, 0:LANES]
    logits = _dot(hi, w_hi) + _dot(lo, w_hi) + _dot(hi, wr_ref[:, LANES:2 * LANES])

    lt = logits.T
    neg = -jnp.inf
    grow = lax.broadcasted_iota(jnp.int32, (8, tm), 0).astype(F32)
    gl = jnp.where(grow < N_GROUPS, lt[N_EXPERTS:N_EXPERTS + 8, :], neg)
    gmax = jnp.max(gl, axis=0, keepdims=True)
    g_gate = 1.0 / jnp.sum(jnp.exp(gl - gmax), axis=0, keepdims=True)
    g_idx = jnp.min(jnp.where(gl == gmax, grow, 1e9), axis=0, keepdims=True)
    erow = lax.broadcasted_iota(jnp.int32, (N_EXPERTS, tm), 0).astype(F32)
    e_lo = g_idx * EXPERTS_PER_GROUP
    el = jnp.where((erow >= e_lo) & (erow < e_lo + EXPERTS_PER_GROUP), lt[0:N_EXPERTS, :], neg)
    v1 = jnp.max(el, axis=0, keepdims=True)
    i1 = jnp.min(jnp.where(el == v1, erow, 1e9), axis=0, keepdims=True)
    el2 = jnp.where(erow == i1, neg, el)
    v2 = jnp.max(el2, axis=0, keepdims=True)
    i2 = jnp.min(jnp.where(el2 == v2, erow, 1e9), axis=0, keepdims=True)
    tt = jnp.exp(v2 - v1)
    w1 = g_gate / (1.0 + tt)
    w2 = g_gate * tt / (1.0 + tt)

    sel1 = erow == i1
    sel2 = erow == i2
    oh = jnp.where(sel1 | sel2, 1.0, 0.0)
    ss = lax.broadcasted_iota(jnp.int32, (tm, tm), 0)
    tt_i = lax.broadcasted_iota(jnp.int32, (tm, tm), 1)
    earlier = jnp.where(ss < tt_i, 1.0, 0.0).astype(BF16)
    ranks = _dot(oh.astype(BF16), earlier) + cnt_ref[...]
    r1 = jnp.sum(jnp.where(sel1, ranks, 0.0), axis=0, keepdims=True)
    r2 = jnp.sum(jnp.where(sel2, ranks, 0.0), axis=0, keepdims=True)
    cnt_ref[...] = cnt_ref[...] + jnp.sum(oh, axis=1, keepdims=True)

    routeT = jnp.concatenate([i1, i2, r1, r2, w1, w2, jnp.zeros((2, tm), F32)], axis=0)
    routeT_ref[:, rows] = routeT
    route_ref[rows, :] = jnp.concatenate([routeT, jnp.zeros((LANES - 8, tm), F32)], axis=0).T


def _outproj_router(x2, yc, at, w_out, g, wr_cat):
    t, d = x2.shape
    dc = yc.shape[1]
    tm = TM_PROJ
    tb = tm * OUTPROJ_SUBTILES
    row = lambda i: (i, 0)
    const = lambda i: (0, 0)
    return pl.pallas_call(
        functools.partial(_outproj_router_kernel, tm=tm, dc=dc),
        grid=(t // tb,),
        in_specs=[
            pl.BlockSpec((tb, d), row),
            pl.BlockSpec((tb, dc), row),
            pl.BlockSpec((tb, dc), row),
            pl.BlockSpec(w_out.shape, const),
            pl.BlockSpec((1, d), const),
            pl.BlockSpec(wr_cat.shape, const),
        ],
        out_specs=[
            pl.BlockSpec((tb, d), row),
            pl.BlockSpec((tb, d // 2), row),
            pl.BlockSpec((tb, LANES), row),
            pl.BlockSpec((8, tb), lambda i: (0, i)),
            pl.BlockSpec((N_EXPERTS, 1), const),
        ],
        out_shape=[
            jax.ShapeDtypeStruct((t, d), F32),
            jax.ShapeDtypeStruct((t, d // 2), jnp.int32),
            jax.ShapeDtypeStruct((t, LANES), F32),
            jax.ShapeDtypeStruct((8, t), F32),
            jax.ShapeDtypeStruct((N_EXPERTS, 1), F32),
        ],
        compiler_params=pltpu.CompilerParams(
            dimension_semantics=("arbitrary",), vmem_limit_bytes=VMEM_LIMIT),
        name="outproj_router",
    )(x2, yc, at, w_out, g, wr_cat)


def _positions_kernel(offs_ref, rt_ref, pos_ref):
    rt = rt_ref[...]
    ea, eb = rt[0:1, :], rt[1:2, :]
    sa = jnp.zeros_like(ea)
    sb = jnp.zeros_like(eb)
    for e in range(N_EXPERTS):
        start = offs_ref[e].astype(F32)
        sa = jnp.where(ea == e, start, sa)
        sb = jnp.where(eb == e, start, sb)
    pos_ref[0:1, :] = (sa + rt[2:3, :]).astype(jnp.int32)
    pos_ref[1:2, :] = (sb + rt[3:4, :]).astype(jnp.int32)


def _positions(offs, routeT):
    t = routeT.shape[1]
    return pl.pallas_call(
        _positions_kernel,
        grid_spec=pltpu.PrefetchScalarGridSpec(
            num_scalar_prefetch=1, grid=(1,),
            in_specs=[pl.BlockSpec(routeT.shape, lambda i, offs: (0, 0))],
            out_specs=pl.BlockSpec((2, t), lambda i, offs: (0, 0)),
        ),
        out_shape=jax.ShapeDtypeStruct((2, t), jnp.int32),
        name="positions",
    )(offs, routeT)


def _moe_kernel(te_ref, nt_ref, nx_ref, sl_ref, x_ref, wg_hbm, wu_hbm, wd_hbm, y_ref,
                wg_st, wu_st, wd_st, wgb_ref, wub_ref, wdb_ref, sem):
    i = pl.program_id(0)
    last = nt_ref[0] - 1
    ic = jnp.minimum(i, last)
    expert = te_ref[ic]
    slot = sl_ref[ic]
    first_of_expert = (i == 0) | ((i <= last) & (expert != te_ref[jnp.maximum(ic - 1, 0)]))

    def weight_copies(e, s):
        return (pltpu.make_async_copy(wg_hbm.at[e], wg_st.at[s], sem.at[s, 0]),
                pltpu.make_async_copy(wu_hbm.at[e], wu_st.at[s], sem.at[s, 1]),
                pltpu.make_async_copy(wd_hbm.at[e], wd_st.at[s], sem.at[s, 2]))

    @pl.when(i == 0)
    def _():
        for c in weight_copies(expert, slot):
            c.start()

    @pl.when(first_of_expert)
    def _():
        for c in weight_copies(expert, slot):
            c.wait()
        nxt = nx_ref[ic]

        @pl.when(nxt >= 0)
        def _():
            for c in weight_copies(nxt, 1 - slot):
                c.start()

        wgb_ref[...] = wg_st[slot].astype(BF16)
        wub_ref[...] = wu_st[slot].astype(BF16)
        wdb_ref[...] = wd_st[slot].astype(BF16)

    @pl.when(i <= last)
    def _():
        x_l, x_r = _unpack_rows(x_ref[...])
        x_l = x_l.astype(BF16)
        x_r = x_r.astype(BF16)
        half = x_l.shape[1]
        hg = _dot(x_l, wgb_ref[0:half, :]) + _dot(x_r, wgb_ref[half:2 * half, :])
        hu = _dot(x_l, wub_ref[0:half, :]) + _dot(x_r, wub_ref[half:2 * half, :])
        act = hg * (1.0 / (1.0 + jnp.exp(-hg))) * hu
        y_ref[...] = _pack_rows(_dot(act.astype(BF16), wdb_ref[...]))


def _moe(tile_expert, n_tiles, next_expert, stage_slot, xs, wg, wu, wd):
    p, dp = xs.shape
    d, f = wg.shape[1], wg.shape[2]
    assert dp * 2 == d
    tm = TM_MOE
    row = lambda i, te, nt, nx, sl: (jnp.minimum(i, nt[0] - 1), 0)
    hbm = pl.BlockSpec(memory_space=pl.ANY)
    grid_spec = pltpu.PrefetchScalarGridSpec(
        num_scalar_prefetch=4,
        grid=(p // tm,),
        in_specs=[pl.BlockSpec((tm, dp), row), hbm, hbm, hbm],
        out_specs=pl.BlockSpec((tm, dp), row),
        scratch_shapes=[pltpu.VMEM((2, d, f), F32), pltpu.VMEM((2, d, f), F32), pltpu.VMEM((2, f, d), F32),
                        pltpu.VMEM((d, f), BF16), pltpu.VMEM((d, f), BF16), pltpu.VMEM((f, d), BF16),
                        pltpu.SemaphoreType.DMA((2, 3))],
    )
    return pl.pallas_call(
        _moe_kernel,
        grid_spec=grid_spec,
        out_shape=jax.ShapeDtypeStruct((p, dp), jnp.int32),
        compiler_params=pltpu.CompilerParams(
            dimension_semantics=("arbitrary",), vmem_limit_bytes=VMEM_LIMIT),
        name="moe_experts",
    )(tile_expert, n_tiles, next_expert, stage_slot, xs, wg, wu, wd)


def _sc_mesh():
    return plsc.VectorSubcoreMesh(core_axis_name="c", subcore_axis_name="s",
                                  num_cores=SC_CORES, num_subcores=SC_SUBCORES)


def _sc_dispatch(rows, pos_a, pos_b, n_out):
    t, d = rows.shape
    win = pos_a.shape[1]

    @functools.partial(pl.kernel, out_type=jax.ShapeDtypeStruct((n_out, d), rows.dtype),
                       mesh=_sc_mesh(), scratch_types=[], name="sc_dispatch")
    def run(rows_hbm, pa_hbm, pb_hbm, out_hbm):
        def body(rows_vmem, pa_vmem, pb_vmem):
            pltpu.sync_copy(rows_vmem, out_hbm.at[pa_vmem.at[0]])
            pltpu.sync_copy(rows_vmem, out_hbm.at[pb_vmem.at[0]])

        pltpu.emit_pipeline(
            body, grid=(t // win,),
            in_specs=[pl.BlockSpec((win, d), lambda i: (i, 0)),
                      pl.BlockSpec((1, win), lambda i: (i, 0)),
                      pl.BlockSpec((1, win), lambda i: (i, 0))],
            out_specs=[],
            core_axis_name=("c", "s"),
            dimension_semantics=(pltpu.PARALLEL,),
        )(rows_hbm, pa_hbm, pb_hbm)

    return run(rows, pos_a, pos_b)


def _sc_gather(table, idx):
    d = table.shape[1]
    n_win, win = idx.shape

    @functools.partial(pl.kernel, out_type=jax.ShapeDtypeStruct((n_win * win, d), table.dtype),
                       mesh=_sc_mesh(), scratch_types=[], name="sc_gather")
    def run(table_hbm, idx_hbm, out_hbm):
        def body(idx_vmem, out_vmem):
            pltpu.sync_copy(table_hbm.at[idx_vmem.at[0]], out_vmem)

        pltpu.emit_pipeline(
            body, grid=(n_win,),
            in_specs=[pl.BlockSpec((1, win), lambda i: (i, 0))],
            out_specs=[pl.BlockSpec((win, d), lambda i: (i, 0))],
            core_axis_name=("c", "s"),
            dimension_semantics=(pltpu.PARALLEL,),
        )(idx_hbm, out_hbm)

    return run(table, idx)


def _combine_kernel(h_ref, ya_ref, yb_ref, r_ref, o_ref):
    r = r_ref[...]
    wa, wb = r[:, 4:5], r[:, 5:6]
    a_l, a_r = _unpack_rows(ya_ref[...])
    b_l, b_r = _unpack_rows(yb_ref[...])
    half = a_l.shape[1]
    o_ref[:, 0:half] = h_ref[:, 0:half] + wa * a_l + wb * b_l
    o_ref[:, half:2 * half] = h_ref[:, half:2 * half] + wa * a_r + wb * b_r


def _combine(hres, yg, route):
    t, d = hres.shape
    tm = TM_COMBINE
    nb = t // tm
    return pl.pallas_call(
        _combine_kernel,
        grid=(nb,),
        in_specs=[pl.BlockSpec((tm, d), lambda i: (i, 0)),
                  pl.BlockSpec((tm, d // 2), lambda i: (i, 0)),
                  pl.BlockSpec((tm, d // 2), lambda i: (i + nb, 0)),
                  pl.BlockSpec((tm, LANES), lambda i: (i, 0))],
        out_specs=pl.BlockSpec((tm, d), lambda i: (i, 0)),
        out_shape=jax.ShapeDtypeStruct((t, d), F32),
        compiler_params=pltpu.CompilerParams(
            dimension_semantics=("arbitrary",), vmem_limit_bytes=VMEM_LIMIT),
        name="combine",
    )(hres, yg, yg, route)


def _lambda_init(layer_idx):
    return 0.8 - 0.6 * math.exp(-0.3 * layer_idx)


def _layer(h, l, attn_norm_g, w_in, conv_w, conv_out_g, q_norm_g, k_norm_g,
           lambda_q1, lambda_k1, lambda_q2, lambda_k2, attn_subln_g, w_out,
           ffn_norm_g, w_router_group, w_router_expert, w_exp_gate, w_exp_up, w_exp_down):
    batch, seq, d = h.shape
    t = batch * seq
    dc = conv_w.shape[-1]
    lam_init = _lambda_init(l)
    x2 = h.reshape(t, d)

    reps = dc // HEAD_DIM
    assert dc // CONV_GROUPS == HEAD_DIM
    qg = (jnp.tile(q_norm_g[l], reps) * (HEAD_DIM ** -0.5 * math.log2(math.e))).reshape(1, dc)
    kg = jnp.tile(k_norm_g[l], reps).reshape(1, dc)
    grp = jnp.arange(MXU_TILE) // HEAD_DIM
    gmat = jnp.where(grp[:, None] == grp[None, :], 1.0 / HEAD_DIM, 0.0).astype(BF16)
    yc, qT, k, vT = _inproj(x2, attn_norm_g[l].reshape(1, d), w_in[l].astype(BF16), conv_w[l],
                          conv_out_g[l].reshape(1, dc), qg, kg, gmat, batch=batch, seq=seq)

    lam_params = jnp.stack([lambda_q1[l], lambda_k1[l], lambda_q2[l], lambda_k2[l]])
    at = _attention(lam_params, attn_subln_g[l].reshape(V_DIM, 1), qT, k, vT,
                    batch=batch, seq=seq, lam_init=lam_init)

    wr = jnp.concatenate([w_router_expert[l], w_router_group[l],
                          jnp.zeros((d, LANES - N_EXPERTS - N_GROUPS), F32)], axis=1)
    wr_hi = wr.astype(BF16)
    wr_lo = (wr - wr_hi.astype(F32)).astype(BF16)
    hres, hn2, route, routeT, cnt = _outproj_router(x2, yc, at, w_out[l].astype(BF16),
                                                    ffn_norm_g[l].reshape(1, d),
                                                    jnp.concatenate([wr_hi, wr_lo], axis=1))

    tmm = TM_MOE
    n_tiles_max = (2 * t) // tmm + N_EXPERTS
    p_rows = n_tiles_max * tmm
    counts = cnt[:, 0].astype(jnp.int32)
    tiles = (counts + tmm - 1) // tmm
    tile_end = jnp.cumsum(tiles)
    offs = (tile_end - tiles) * tmm
    pos = _positions(offs, routeT)
    n_tiles = tile_end[-1:].astype(jnp.int32)
    tile_ids = jnp.arange(n_tiles_max, dtype=jnp.int32)
    tile_expert = jnp.minimum(
        jnp.sum((tile_end[None, :] <= tile_ids[:, None]).astype(jnp.int32), axis=1), N_EXPERTS - 1)
    e_ids = jnp.arange(N_EXPERTS, dtype=jnp.int32)
    nonempty = tiles > 0
    later = jnp.where(nonempty[None, :] & (e_ids[None, :] > e_ids[:, None]), e_ids[None, :], N_EXPERTS)
    next_of = jnp.min(later, axis=1)
    next_of = jnp.where(next_of == N_EXPERTS, -1, next_of).astype(jnp.int32)
    slot_of = ((jnp.cumsum(nonempty.astype(jnp.int32)) - 1) % 2).astype(jnp.int32)
    onehot_te = (tile_expert[:, None] == e_ids[None, :]).astype(jnp.int32)
    next_expert = jnp.sum(onehot_te * next_of[None, :], axis=1)
    stage_slot = jnp.sum(onehot_te * slot_of[None, :], axis=1)

    posw = pos.reshape(2 * t // SC_WIN, SC_WIN)
    pos1w = posw[:t // SC_WIN]
    pos2w = posw[t // SC_WIN:]
    xs = _sc_dispatch(hn2, pos1w, pos2w, p_rows)

    f = w_exp_gate.shape[-1]
    ys = _moe(tile_expert, n_tiles, next_expert, stage_slot, xs,
              w_exp_gate[l].reshape(N_EXPERTS, d, f),
              w_exp_up[l].reshape(N_EXPERTS, d, f),
              w_exp_down[l].reshape(N_EXPERTS, f, d))
    yg = _sc_gather(ys, posw)
    out = _combine(hres, yg, route)
    return out.reshape(batch, seq, d)


def kernel(x, attn_norm_g, w_in, conv_w, conv_out_g, q_norm_g, k_norm_g, lambda_q1, lambda_k1,
           lambda_q2, lambda_k2, attn_subln_g, w_out, ffn_norm_g, w_router_group, w_router_expert,
           w_exp_gate, w_exp_up, w_exp_down):
    h = x
    for l in range(attn_norm_g.shape[0]):
        h = _layer(h, l, attn_norm_g, w_in, conv_w, conv_out_g, q_norm_g, k_norm_g,
                   lambda_q1, lambda_k1, lambda_q2, lambda_k2, attn_subln_g, w_out,
                   ffn_norm_g, w_router_group, w_router_expert, w_exp_gate, w_exp_up, w_exp_down)
    return h
```

```python
import functools
import math

import jax
import jax.numpy as jnp
from jax import lax
from jax.experimental import pallas as pl
from jax.experimental.pallas import tpu as pltpu
from jax.experimental.pallas import tpu_sc as plsc

F32 = jnp.float32
BF16 = jnp.bfloat16

HEAD_DIM = 64
V_DIM = 2 * HEAD_DIM
CONV_GROUPS = 8
N_GROUPS = 4
EXPERTS_PER_GROUP = 8
N_EXPERTS = N_GROUPS * EXPERTS_PER_GROUP
EPS = 1e-6
LANES = 128
MXU_TILE = 256
ONES_ROWS = 16
VMEM_LIMIT = 48 * 1024 * 1024

TM_PROJ = 256
TM_COMBINE = 1024
COMBINE_CHUNKS = 4
OUTPROJ_SUBTILES = 4
TQ = 512
TK = 512
TM_MOE = 256
SC_CORES = 2
SC_SUBCORES = 16
SC_WIN = 64


def _dot(a, b):
    return jnp.dot(a, b, preferred_element_type=F32)


def _pack_rows(x):
    w = x.shape[1] // 2
    bits = lax.bitcast_convert_type(x.astype(BF16).astype(F32), jnp.uint32)
    return lax.bitcast_convert_type((bits[:, :w] >> 16) | bits[:, w:], jnp.int32)


def _unpack_rows(packed):
    bits = lax.bitcast_convert_type(packed, jnp.uint32)
    left = lax.bitcast_convert_type(bits << 16, F32)
    right = lax.bitcast_convert_type(bits & jnp.uint32(0xFFFF0000), F32)
    return left, right


def _group_mean(sq, gm):
    w = gm.shape[0]
    sq = sq.astype(BF16)
    return jnp.concatenate([_dot(sq[:, c:c + w], gm) for c in range(0, sq.shape[1], w)], axis=1)


def _inproj_kernel(x_ref, g_ref, w_ref, cw_ref, cg_ref, qg_ref, kg_ref, gm_ref,
                   yc_ref, qT_ref, k_ref, vT_ref, carry_ref, *, tm, dc):
    j = pl.program_id(1)

    @pl.when(j == 0)
    def _():
        carry_ref[...] = jnp.zeros_like(carry_ref)

    x = x_ref[...]
    ms = jnp.mean(x * x, axis=-1, keepdims=True)
    hn = (x * lax.rsqrt(ms + EPS) * g_ref[...]).astype(BF16)

    def proj(s):
        return _dot(hn, w_ref[:, s * dc:(s + 1) * dc])

    gm = gm_ref[...]

    u = proj(2) * proj(0)
    prev = carry_ref[...]
    rows = lax.broadcasted_iota(jnp.int32, u.shape, 0)
    u1 = jnp.where(rows == 0, prev[7:8, :], pltpu.roll(u, 1, 0))
    u2 = jnp.where(rows == 0, prev[6:7, :], jnp.where(rows == 1, prev[7:8, :], pltpu.roll(u, 2, 0)))
    carry_ref[...] = u[tm - 8:tm, :]
    cw = cw_ref[...]
    y = proj(1) * (cw[0:1, :] * u2 + cw[1:2, :] * u1 + cw[2:3, :] * u)
    yc_ref[...] = (y * lax.rsqrt(_group_mean(y * y, gm) + EPS) * cg_ref[...]).astype(BF16)

    q = proj(3)
    qT_ref[...] = (q * lax.rsqrt(_group_mean(q * q, gm) + EPS) * qg_ref[...]).T.astype(BF16)
    k = proj(4)
    k_ref[...] = (k * lax.rsqrt(_group_mean(k * k, gm) + EPS) * kg_ref[...]).astype(BF16)
    vt = proj(5).T.astype(BF16)
    vrows = V_DIM + ONES_ROWS
    for h in range(dc // V_DIM):
        vT_ref[0, h * vrows:h * vrows + V_DIM, :] = vt[h * V_DIM:(h + 1) * V_DIM, :]
        vT_ref[0, h * vrows + V_DIM:(h + 1) * vrows, :] = jnp.ones((ONES_ROWS, tm), BF16)


def _inproj(x2, g, w_in, conv_w, conv_g, qg, kg, gmat, *, batch, seq):
    t, d = x2.shape
    dc = conv_g.shape[1]
    dv = dc // V_DIM * (V_DIM + ONES_ROWS)
    tm = TK
    nj = seq // tm
    row = lambda b, j: (b * nj + j, 0)
    const = lambda b, j: (0, 0)
    out_sds = jax.ShapeDtypeStruct((t, dc), BF16)
    return pl.pallas_call(
        functools.partial(_inproj_kernel, tm=tm, dc=dc),
        grid=(batch, nj),
        in_specs=[
            pl.BlockSpec((tm, d), row),
            pl.BlockSpec((1, d), const),
            pl.BlockSpec(w_in.shape, const),
            pl.BlockSpec(conv_w.shape, const),
            pl.BlockSpec((1, dc), const),
            pl.BlockSpec((1, dc), const),
            pl.BlockSpec((1, dc), const),
            pl.BlockSpec(gmat.shape, const),
        ],
        out_specs=[
            pl.BlockSpec((tm, dc), row),
            pl.BlockSpec((dc, tm), lambda b, j: (0, b * nj + j)),
            pl.BlockSpec((tm, dc), row),
            pl.BlockSpec((1, dv, tm), lambda b, j: (b * nj + j, 0, 0)),
        ],
        out_shape=[out_sds, jax.ShapeDtypeStruct((dc, t), BF16), out_sds,
                   jax.ShapeDtypeStruct((t // tm, dv, tm), BF16)],
        scratch_shapes=[pltpu.VMEM((8, dc), F32)],
        compiler_params=pltpu.CompilerParams(
            dimension_semantics=("arbitrary", "arbitrary"), vmem_limit_bytes=VMEM_LIMIT),
        name="inproj_conv_qknorm",
    )(x2, g, w_in, conv_w, conv_g, qg, kg, gmat)


def _attn_kernel(lp_ref, sg_ref, qT_ref, k_ref, vT_ref, o_ref, m_ref, acc_ref,
                 sa_ref, pb_ref, ab_ref, *, tq, tk, nq, lam_init):
    map_a = slice(0, tq)
    map_b = slice(tq, 2 * tq)

    lp = lp_ref[...]
    lam = (jnp.exp(jnp.sum(lp[0:1, :] * lp[1:2, :], axis=-1, keepdims=True))
           - jnp.exp(jnp.sum(lp[2:3, :] * lp[3:4, :], axis=-1, keepdims=True)) + lam_init)

    def stacked_queries(qi):
        qT = qT_ref[:, qi * tq:(qi + 1) * tq]
        row = lax.broadcasted_iota(jnp.int32, qT.shape, 0)
        zero = jnp.zeros_like(qT)
        return jnp.concatenate([jnp.where(row < HEAD_DIM, qT, zero),
                                jnp.where(row >= HEAD_DIM, qT, zero)], axis=1)

    def key_block(j):
        return k_ref[j * tk:(j + 1) * tk, :]

    def softmax(m, cols, s, masked):
        if masked:
            qrel = lax.broadcasted_iota(jnp.int32, s.shape, 1)
            krel = lax.broadcasted_iota(jnp.int32, s.shape, 0)
            s = jnp.where(krel <= qrel, s, -jnp.inf)
        sb = s.astype(BF16)
        m_old = m[:, cols]
        m_new = jnp.maximum(m_old, jnp.max(sb, axis=0, keepdims=True).astype(F32))
        alpha = jnp.exp2(m_old - m_new)
        p = jnp.exp2(sb - m_new.astype(BF16))
        m[:, cols] = m_new
        return alpha, p

    def accumulate(acc, cols, alpha, vb, p):
        acc[:, cols] = alpha * acc[:, cols] + _dot(vb, p)

    stacked = {0: stacked_queries(0)}
    m_ref[0] = jnp.full(m_ref.shape[1:], -jnp.inf, F32)
    acc_ref[0] = jnp.zeros(acc_ref.shape[1:], F32)
    sa_ref[0] = _dot(key_block(0), stacked[0][:, map_a])

    for qi in range(nq):
        par = qi % 2
        m, acc, sa, qq = m_ref.at[par], acc_ref.at[par], sa_ref.at[par], stacked[qi]
        for j in range(qi + 1):
            last = j == qi
            if j > 0:
                accumulate(acc, map_b, ab_ref[...], vT_ref[j - 1], pb_ref[...])
            s_b = _dot(key_block(j), qq[:, map_b])
            alpha_a, p_a = softmax(m, map_a, sa[...], last)
            accumulate(acc, map_a, alpha_a, vT_ref[j], p_a)
            if not last:
                sa[...] = _dot(key_block(j + 1), qq[:, map_a])
            elif qi + 1 < nq:
                stacked[qi + 1] = stacked_queries(qi + 1)
                m_ref[1 - par] = jnp.full(m_ref.shape[1:], -jnp.inf, F32)
                acc_ref[1 - par] = jnp.zeros(acc_ref.shape[1:], F32)
                sa_ref[1 - par] = _dot(key_block(0), stacked[qi + 1][:, map_a])
            alpha_b, p_b = softmax(m, map_b, s_b, last)
            if last:
                accumulate(acc, map_b, alpha_b, vT_ref[j], p_b)
            else:
                ab_ref[...] = alpha_b
                pb_ref[...] = p_b

        o = acc[0:V_DIM, :] / acc[V_DIM:V_DIM + 1, :]
        d = o[:, map_a] - lam * o[:, map_b]
        ms = jnp.mean(d * d, axis=0, keepdims=True)
        o_ref[qi * tq:(qi + 1) * tq, :] = (
            d * lax.rsqrt(ms + EPS) * sg_ref[...] * (1.0 - lam_init)).T.astype(BF16)
        del stacked[qi]


def _attention(lam_params, subln_g, qT, k, vT, *, batch, seq, lam_init):
    dq, t = qT.shape
    n_heads = dq // V_DIM
    tq, tk = TQ, TK
    assert tq == tk and vT.shape[2] == tk
    nq = seq // tq
    nk = seq // tk
    vrows = V_DIM + ONES_ROWS
    const = lambda b, h: (0, 0)
    return pl.pallas_call(
        functools.partial(_attn_kernel, tq=tq, tk=tk, nq=nq, lam_init=lam_init),
        grid=(batch, n_heads),
        in_specs=[
            pl.BlockSpec(lam_params.shape, const),
            pl.BlockSpec((V_DIM, 1), const),
            pl.BlockSpec((V_DIM, seq), lambda b, h: (h, b)),
            pl.BlockSpec((seq, V_DIM), lambda b, h: (b, h)),
            pl.BlockSpec((nk, vrows, tk), lambda b, h: (b, h, 0)),
        ],
        out_specs=pl.BlockSpec((seq, V_DIM), lambda b, h: (b, h)),
        out_shape=jax.ShapeDtypeStruct((t, dq), BF16),
        scratch_shapes=[pltpu.VMEM((2, 1, 2 * tq), F32),
                        pltpu.VMEM((2, vrows, 2 * tq), F32),
                        pltpu.VMEM((2, tk, tq), F32), pltpu.VMEM((tk, tq), BF16), pltpu.VMEM((1, tq), F32)],
        compiler_params=pltpu.CompilerParams(
            dimension_semantics=("arbitrary", "arbitrary"), vmem_limit_bytes=VMEM_LIMIT),
        name="diff_attention",
    )(lam_params, subln_g, qT, k, vT)


def _outproj_router_kernel(x_ref, yc_ref, at_ref, wo_ref, g_ref, wr_ref,
                           h_ref, hn_ref, route_ref, routeT_ref, cnt_ref, *, tm, dc):
    i = pl.program_id(0)

    @pl.when(i == 0)
    def _():
        cnt_ref[...] = jnp.zeros_like(cnt_ref)

    for r0 in range(0, x_ref.shape[0], tm):
        _outproj_router_subtile(slice(r0, r0 + tm), x_ref, yc_ref, at_ref, wo_ref, g_ref, wr_ref,
                                h_ref, hn_ref, route_ref, routeT_ref, cnt_ref, tm=tm, dc=dc)


def _outproj_router_subtile(rows, x_ref, yc_ref, at_ref, wo_ref, g_ref, wr_ref,
                            h_ref, hn_ref, route_ref, routeT_ref, cnt_ref, *, tm, dc):
    h = x_ref[rows, :] + _dot(yc_ref[rows, :], wo_ref[0:dc, :]) + _dot(at_ref[rows, :], wo_ref[dc:2 * dc, :])
    h_ref[rows, :] = h
    ms = jnp.mean(h * h, axis=-1, keepdims=True)
    hn = h * lax.rsqrt(ms + EPS) * g_ref[...]
    hi = hn.astype(BF16)
    hn_ref[rows, :] = _pack_rows(hn)
    lo = (hn - hi.astype(F32)).astype(BF16)
    w_hi = wr_ref[:, 0:LANES]
    logits = _dot(hi, w_hi) + _dot(lo, w_hi) + _dot(hi, wr_ref[:, LANES:2 * LANES])

    lt = logits.T
    neg = -jnp.inf
    grow = lax.broadcasted_iota(jnp.int32, (8, tm), 0).astype(F32)
    gl = jnp.where(grow < N_GROUPS, lt[N_EXPERTS:N_EXPERTS + 8, :], neg)
    gmax = jnp.max(gl, axis=0, keepdims=True)
    g_gate = 1.0 / jnp.sum(jnp.exp(gl - gmax), axis=0, keepdims=True)
    g_idx = jnp.min(jnp.where(gl == gmax, grow, 1e9), axis=0, keepdims=True)
    erow = lax.broadcasted_iota(jnp.int32, (N_EXPERTS, tm), 0).astype(F32)
    e_lo = g_idx * EXPERTS_PER_GROUP
    el = jnp.where((erow >= e_lo) & (erow < e_lo + EXPERTS_PER_GROUP), lt[0:N_EXPERTS, :], neg)
    v1 = jnp.max(el, axis=0, keepdims=True)
    i1 = jnp.min(jnp.where(el == v1, erow, 1e9), axis=0, keepdims=True)
    el2 = jnp.where(erow == i1, neg, el)
    v2 = jnp.max(el2, axis=0, keepdims=True)
    i2 = jnp.min(jnp.where(el2 == v2, erow, 1e9), axis=0, keepdims=True)
    tt = jnp.exp(v2 - v1)
    w1 = g_gate / (1.0 + tt)
    w2 = g_gate * tt / (1.0 + tt)

    sel1 = erow == i1
    sel2 = erow == i2
    oh = jnp.where(sel1 | sel2, 1.0, 0.0)
    ss = lax.broadcasted_iota(jnp.int32, (tm, tm), 0)
    tt_i = lax.broadcasted_iota(jnp.int32, (tm, tm), 1)
    earlier = jnp.where(ss < tt_i, 1.0, 0.0).astype(BF16)
    ranks = _dot(oh.astype(BF16), earlier) + cnt_ref[...]
    r1 = jnp.sum(jnp.where(sel1, ranks, 0.0), axis=0, keepdims=True)
    r2 = jnp.sum(jnp.where(sel2, ranks, 0.0), axis=0, keepdims=True)
    cnt_ref[...] = cnt_ref[...] + jnp.sum(oh, axis=1, keepdims=True)

    routeT = jnp.concatenate([i1, i2, r1, r2, w1, w2, jnp.zeros((2, tm), F32)], axis=0)
    routeT_ref[:, rows] = routeT
    route_ref[rows, :] = jnp.concatenate([routeT, jnp.zeros((LANES - 8, tm), F32)], axis=0).T


def _outproj_router(x2, yc, at, w_out, g, wr_cat):
    t, d = x2.shape
    dc = yc.shape[1]
    tm = TM_PROJ
    tb = tm * OUTPROJ_SUBTILES
    row = lambda i: (i, 0)
    const = lambda i: (0, 0)
    return pl.pallas_call(
        functools.partial(_outproj_router_kernel, tm=tm, dc=dc),
        grid=(t // tb,),
        in_specs=[
            pl.BlockSpec((tb, d), row),
            pl.BlockSpec((tb, dc), row),
            pl.BlockSpec((tb, dc), row),
            pl.BlockSpec(w_out.shape, const),
            pl.BlockSpec((1, d), const),
            pl.BlockSpec(wr_cat.shape, const),
        ],
        out_specs=[
            pl.BlockSpec((tb, d), row),
            pl.BlockSpec((tb, d // 2), row),
            pl.BlockSpec((tb, LANES), row),
            pl.BlockSpec((8, tb), lambda i: (0, i)),
            pl.BlockSpec((N_EXPERTS, 1), const),
        ],
        out_shape=[
            jax.ShapeDtypeStruct((t, d), F32),
            jax.ShapeDtypeStruct((t, d // 2), jnp.int32),
            jax.ShapeDtypeStruct((t, LANES), F32),
            jax.ShapeDtypeStruct((8, t), F32),
            jax.ShapeDtypeStruct((N_EXPERTS, 1), F32),
        ],
        compiler_params=pltpu.CompilerParams(
            dimension_semantics=("arbitrary",), vmem_limit_bytes=VMEM_LIMIT),
        name="outproj_router",
    )(x2, yc, at, w_out, g, wr_cat)


def _positions_kernel(offs_ref, rt_ref, pos_ref):
    rt = rt_ref[...]
    ea, eb = rt[0:1, :], rt[1:2, :]
    sa = jnp.zeros_like(ea)
    sb = jnp.zeros_like(eb)
    for e in range(N_EXPERTS):
        start = offs_ref[e].astype(F32)
        sa = jnp.where(ea == e, start, sa)
        sb = jnp.where(eb == e, start, sb)
    pos_ref[0:1, :] = (sa + rt[2:3, :]).astype(jnp.int32)
    pos_ref[1:2, :] = (sb + rt[3:4, :]).astype(jnp.int32)


def _positions(offs, routeT):
    t = routeT.shape[1]
    return pl.pallas_call(
        _positions_kernel,
        grid_spec=pltpu.PrefetchScalarGridSpec(
            num_scalar_prefetch=1, grid=(1,),
            in_specs=[pl.BlockSpec(routeT.shape, lambda i, offs: (0, 0))],
            out_specs=pl.BlockSpec((2, t), lambda i, offs: (0, 0)),
        ),
        out_shape=jax.ShapeDtypeStruct((2, t), jnp.int32),
        name="positions",
    )(offs, routeT)


def _moe_kernel(te_ref, nt_ref, nx_ref, sl_ref, x_ref, wg_hbm, wu_hbm, wd_hbm, y_ref,
                wg_st, wu_st, wd_st, wgb_ref, wub_ref, wdb_ref, sem):
    i = pl.program_id(0)
    last = nt_ref[0] - 1
    ic = jnp.minimum(i, last)
    expert = te_ref[ic]
    slot = sl_ref[ic]
    first_of_expert = (i == 0) | ((i <= last) & (expert != te_ref[jnp.maximum(ic - 1, 0)]))

    def weight_copies(e, s):
        return (pltpu.make_async_copy(wg_hbm.at[e], wg_st.at[s], sem.at[s, 0]),
                pltpu.make_async_copy(wu_hbm.at[e], wu_st.at[s], sem.at[s, 1]),
                pltpu.make_async_copy(wd_hbm.at[e], wd_st.at[s], sem.at[s, 2]))

    @pl.when(i == 0)
    def _():
        for c in weight_copies(expert, slot):
            c.start()

    @pl.when(first_of_expert)
    def _():
        for c in weight_copies(expert, slot):
            c.wait()
        nxt = nx_ref[ic]

        @pl.when(nxt >= 0)
        def _():
            for c in weight_copies(nxt, 1 - slot):
                c.start()

        wgb_ref[...] = wg_st[slot].astype(BF16)
        wub_ref[...] = wu_st[slot].astype(BF16)
        wdb_ref[...] = wd_st[slot].astype(BF16)

    @pl.when(i <= last)
    def _():
        x_l, x_r = _unpack_rows(x_ref[...])
        x_l = x_l.astype(BF16)
        x_r = x_r.astype(BF16)
        half = x_l.shape[1]
        hg = _dot(x_l, wgb_ref[0:half, :]) + _dot(x_r, wgb_ref[half:2 * half, :])
        hu = _dot(x_l, wub_ref[0:half, :]) + _dot(x_r, wub_ref[half:2 * half, :])
        act = hg * (1.0 / (1.0 + jnp.exp(-hg))) * hu
        y_ref[...] = _pack_rows(_dot(act.astype(BF16), wdb_ref[...]))


def _moe(tile_expert, n_tiles, next_expert, stage_slot, xs, wg, wu, wd):
    p, dp = xs.shape
    d, f = wg.shape[1], wg.shape[2]
    assert dp * 2 == d
    tm = TM_MOE
    row = lambda i, te, nt, nx, sl: (jnp.minimum(i, nt[0] - 1), 0)
    hbm = pl.BlockSpec(memory_space=pl.ANY)
    grid_spec = pltpu.PrefetchScalarGridSpec(
        num_scalar_prefetch=4,
        grid=(p // tm,),
        in_specs=[pl.BlockSpec((tm, dp), row), hbm, hbm, hbm],
        out_specs=pl.BlockSpec((tm, dp), row),
        scratch_shapes=[pltpu.VMEM((2, d, f), F32), pltpu.VMEM((2, d, f), F32), pltpu.VMEM((2, f, d), F32),
                        pltpu.VMEM((d, f), BF16), pltpu.VMEM((d, f), BF16), pltpu.VMEM((f, d), BF16),
                        pltpu.SemaphoreType.DMA((2, 3))],
    )
    return pl.pallas_call(
        _moe_kernel,
        grid_spec=grid_spec,
        out_shape=jax.ShapeDtypeStruct((p, dp), jnp.int32),
        compiler_params=pltpu.CompilerParams(
            dimension_semantics=("arbitrary",), vmem_limit_bytes=VMEM_LIMIT),
        name="moe_experts",
    )(tile_expert, n_tiles, next_expert, stage_slot, xs, wg, wu, wd)


def _sc_mesh():
    return plsc.VectorSubcoreMesh(core_axis_name="c", subcore_axis_name="s",
                                  num_cores=SC_CORES, num_subcores=SC_SUBCORES)


def _sc_dispatch(rows, pos_a, pos_b, n_out):
    t, d = rows.shape
    win = pos_a.shape[1]

    @functools.partial(pl.kernel, out_type=jax.ShapeDtypeStruct((n_out, d), rows.dtype),
                       mesh=_sc_mesh(), scratch_types=[], name="sc_dispatch")
    def run(rows_hbm, pa_hbm, pb_hbm, out_hbm):
        def body(rows_vmem, pa_vmem, pb_vmem):
            pltpu.sync_copy(rows_vmem, out_hbm.at[pa_vmem.at[0]])
            pltpu.sync_copy(rows_vmem, out_hbm.at[pb_vmem.at[0]])

        pltpu.emit_pipeline(
            body, grid=(t // win,),
            in_specs=[pl.BlockSpec((win, d), lambda i: (i, 0)),
                      pl.BlockSpec((1, win), lambda i: (i, 0)),
                      pl.BlockSpec((1, win), lambda i: (i, 0))],
            out_specs=[],
            core_axis_name=("c", "s"),
            dimension_semantics=(pltpu.PARALLEL,),
        )(rows_hbm, pa_hbm, pb_hbm)

    return run(rows, pos_a, pos_b)


def _sc_gather(table, idx):
    d = table.shape[1]
    n_win, win = idx.shape

    @functools.partial(pl.kernel, out_type=jax.ShapeDtypeStruct((n_win * win, d), table.dtype),
                       mesh=_sc_mesh(), scratch_types=[], name="sc_gather")
    def run(table_hbm, idx_hbm, out_hbm):
        def body(idx_vmem, out_vmem):
            pltpu.sync_copy(table_hbm.at[idx_vmem.at[0]], out_vmem)

        pltpu.emit_pipeline(
            body, grid=(n_win,),
            in_specs=[pl.BlockSpec((1, win), lambda i: (i, 0))],
            out_specs=[pl.BlockSpec((win, d), lambda i: (i, 0))],
            core_axis_name=("c", "s"),
            dimension_semantics=(pltpu.PARALLEL,),
        )(idx_hbm, out_hbm)

    return run(table, idx)


def _combine_kernel(*refs):
    h_ref, ya_ref, yb_ref, r_ref, o_ref = refs[-5:]
    r = r_ref[...]
    wa, wb = r[:, 4:5], r[:, 5:6]
    a_l, a_r = _unpack_rows(ya_ref[...])
    b_l, b_r = _unpack_rows(yb_ref[...])
    half = a_l.shape[1]
    o_ref[:, 0:half] = h_ref[:, 0:half] + wa * a_l + wb * b_l
    o_ref[:, half:2 * half] = h_ref[:, half:2 * half] + wa * a_r + wb * b_r


def _combine_chunk(out_prev, hres, yg, route, chunk, n_chunks):
    t, d = hres.shape
    tm = TM_COMBINE
    nb = t // n_chunks // tm
    base = chunk * nb
    specs = [pl.BlockSpec((tm, d), lambda i: (base + i, 0)),
             pl.BlockSpec((tm, d // 2), lambda i: (i, 0)),
             pl.BlockSpec((tm, d // 2), lambda i: (i + nb, 0)),
             pl.BlockSpec((tm, LANES), lambda i: (base + i, 0))]
    args = [hres, yg, yg, route]
    aliases = {}
    if out_prev is not None:
        specs = [pl.BlockSpec(memory_space=pl.ANY)] + specs
        args = [out_prev] + args
        aliases = {0: 0}
    return pl.pallas_call(
        _combine_kernel,
        grid=(nb,),
        in_specs=specs,
        out_specs=pl.BlockSpec((tm, d), lambda i: (base + i, 0)),
        out_shape=jax.ShapeDtypeStruct((t, d), F32),
        input_output_aliases=aliases,
        compiler_params=pltpu.CompilerParams(
            dimension_semantics=("arbitrary",), vmem_limit_bytes=VMEM_LIMIT),
        name="combine",
    )(*args)


def _lambda_init(layer_idx):
    return 0.8 - 0.6 * math.exp(-0.3 * layer_idx)


def _layer(h, l, attn_norm_g, w_in, conv_w, conv_out_g, q_norm_g, k_norm_g,
           lambda_q1, lambda_k1, lambda_q2, lambda_k2, attn_subln_g, w_out,
           ffn_norm_g, w_router_group, w_router_expert, w_exp_gate, w_exp_up, w_exp_down):
    batch, seq, d = h.shape
    t = batch * seq
    dc = conv_w.shape[-1]
    lam_init = _lambda_init(l)
    x2 = h.reshape(t, d)

    reps = dc // HEAD_DIM
    assert dc // CONV_GROUPS == HEAD_DIM
    qg = (jnp.tile(q_norm_g[l], reps) * (HEAD_DIM ** -0.5 * math.log2(math.e))).reshape(1, dc)
    kg = jnp.tile(k_norm_g[l], reps).reshape(1, dc)
    grp = jnp.arange(MXU_TILE) // HEAD_DIM
    gmat = jnp.where(grp[:, None] == grp[None, :], 1.0 / HEAD_DIM, 0.0).astype(BF16)
    yc, qT, k, vT = _inproj(x2, attn_norm_g[l].reshape(1, d), w_in[l].astype(BF16), conv_w[l],
                          conv_out_g[l].reshape(1, dc), qg, kg, gmat, batch=batch, seq=seq)

    lam_params = jnp.stack([lambda_q1[l], lambda_k1[l], lambda_q2[l], lambda_k2[l]])
    at = _attention(lam_params, attn_subln_g[l].reshape(V_DIM, 1), qT, k, vT,
                    batch=batch, seq=seq, lam_init=lam_init)

    wr = jnp.concatenate([w_router_expert[l], w_router_group[l],
                          jnp.zeros((d, LANES - N_EXPERTS - N_GROUPS), F32)], axis=1)
    wr_hi = wr.astype(BF16)
    wr_lo = (wr - wr_hi.astype(F32)).astype(BF16)
    hres, hn2, route, routeT, cnt = _outproj_router(x2, yc, at, w_out[l].astype(BF16),
                                                    ffn_norm_g[l].reshape(1, d),
                                                    jnp.concatenate([wr_hi, wr_lo], axis=1))

    tmm = TM_MOE
    n_tiles_max = (2 * t) // tmm + N_EXPERTS
    p_rows = n_tiles_max * tmm
    counts = cnt[:, 0].astype(jnp.int32)
    tiles = (counts + tmm - 1) // tmm
    tile_end = jnp.cumsum(tiles)
    offs = (tile_end - tiles) * tmm
    pos = _positions(offs, routeT)
    n_tiles = tile_end[-1:].astype(jnp.int32)
    tile_ids = jnp.arange(n_tiles_max, dtype=jnp.int32)
    tile_expert = jnp.minimum(
        jnp.sum((tile_end[None, :] <= tile_ids[:, None]).astype(jnp.int32), axis=1), N_EXPERTS - 1)
    e_ids = jnp.arange(N_EXPERTS, dtype=jnp.int32)
    nonempty = tiles > 0
    later = jnp.where(nonempty[None, :] & (e_ids[None, :] > e_ids[:, None]), e_ids[None, :], N_EXPERTS)
    next_of = jnp.min(later, axis=1)
    next_of = jnp.where(next_of == N_EXPERTS, -1, next_of).astype(jnp.int32)
    slot_of = ((jnp.cumsum(nonempty.astype(jnp.int32)) - 1) % 2).astype(jnp.int32)
    onehot_te = (tile_expert[:, None] == e_ids[None, :]).astype(jnp.int32)
    next_expert = jnp.sum(onehot_te * next_of[None, :], axis=1)
    stage_slot = jnp.sum(onehot_te * slot_of[None, :], axis=1)

    posw = pos.reshape(2 * t // SC_WIN, SC_WIN)
    pos1w = posw[:t // SC_WIN]
    pos2w = posw[t // SC_WIN:]
    xs = _sc_dispatch(hn2, pos1w, pos2w, p_rows)

    f = w_exp_gate.shape[-1]
    ys = _moe(tile_expert, n_tiles, next_expert, stage_slot, xs,
              w_exp_gate[l].reshape(N_EXPERTS, d, f),
              w_exp_up[l].reshape(N_EXPERTS, d, f),
              w_exp_down[l].reshape(N_EXPERTS, f, d))
    out = None
    wpc = t // SC_WIN // COMBINE_CHUNKS
    for c in range(COMBINE_CHUNKS):
        idx = jnp.concatenate([pos1w[c * wpc:(c + 1) * wpc], pos2w[c * wpc:(c + 1) * wpc]], axis=0)
        out = _combine_chunk(out, hres, _sc_gather(ys, idx), route, c, COMBINE_CHUNKS)
    return out.reshape(batch, seq, d)


def kernel(x, attn_norm_g, w_in, conv_w, conv_out_g, q_norm_g, k_norm_g, lambda_q1, lambda_k1,
           lambda_q2, lambda_k2, attn_subln_g, w_out, ffn_norm_g, w_router_group, w_router_expert,
           w_exp_gate, w_exp_up, w_exp_down):
    h = x
    for l in range(attn_norm_g.shape[0]):
        h = _layer(h, l, attn_norm_g, w_in, conv_w, conv_out_g, q_norm_g, k_norm_g,
                   lambda_q1, lambda_k1, lambda_q2, lambda_k2, attn_subln_g, w_out,
                   ffn_norm_g, w_router_group, w_router_expert, w_exp_gate, w_exp_up, w_exp_down)
    return h
```

```python
import functools
import math

import jax
import jax.numpy as jnp
from jax import lax
from jax.experimental import pallas as pl
from jax.experimental.pallas import tpu as pltpu
from jax.experimental.pallas import tpu_sc as plsc

F32 = jnp.float32
BF16 = jnp.bfloat16

HEAD_DIM = 64
V_DIM = 2 * HEAD_DIM
CONV_GROUPS = 8
N_GROUPS = 4
EXPERTS_PER_GROUP = 8
N_EXPERTS = N_GROUPS * EXPERTS_PER_GROUP
EPS = 1e-6
LANES = 128
MXU_TILE = 256
ONES_ROWS = 16
VMEM_LIMIT = 48 * 1024 * 1024

TM_PROJ = 256
TM_COMBINE = 1024
OUTPROJ_SUBTILES = 4
TQ = 512
TK = 512
TM_MOE = 256
SC_CORES = 2
SC_SUBCORES = 16
SC_WIN = 64


def _dot(a, b):
    return jnp.dot(a, b, preferred_element_type=F32)


def _pack_rows(x):
    w = x.shape[1] // 2
    bits = lax.bitcast_convert_type(x.astype(BF16).astype(F32), jnp.uint32)
    return lax.bitcast_convert_type((bits[:, :w] >> 16) | bits[:, w:], jnp.int32)


def _unpack_rows(packed):
    bits = lax.bitcast_convert_type(packed, jnp.uint32)
    left = lax.bitcast_convert_type(bits << 16, F32)
    right = lax.bitcast_convert_type(bits & jnp.uint32(0xFFFF0000), F32)
    return left, right


def _group_mean(sq, gm):
    w = gm.shape[0]
    sq = sq.astype(BF16)
    return jnp.concatenate([_dot(sq[:, c:c + w], gm) for c in range(0, sq.shape[1], w)], axis=1)


def _inproj_kernel(x_ref, g_ref, w_ref, cw_ref, cg_ref, qg_ref, kg_ref, gm_ref,
                   yc_ref, qT_ref, k_ref, vT_ref, carry_ref, *, tm, dc):
    j = pl.program_id(1)

    @pl.when(j == 0)
    def _():
        carry_ref[...] = jnp.zeros_like(carry_ref)

    x = x_ref[...]
    ms = jnp.mean(x * x, axis=-1, keepdims=True)
    hn = (x * lax.rsqrt(ms + EPS) * g_ref[...]).astype(BF16)

    def proj(s):
        return _dot(hn, w_ref[:, s * dc:(s + 1) * dc])

    gm = gm_ref[...]

    u = proj(2) * proj(0)
    prev = carry_ref[...]
    rows = lax.broadcasted_iota(jnp.int32, u.shape, 0)
    u1 = jnp.where(rows == 0, prev[7:8, :], pltpu.roll(u, 1, 0))
    u2 = jnp.where(rows == 0, prev[6:7, :], jnp.where(rows == 1, prev[7:8, :], pltpu.roll(u, 2, 0)))
    carry_ref[...] = u[tm - 8:tm, :]
    cw = cw_ref[...]
    y = proj(1) * (cw[0:1, :] * u2 + cw[1:2, :] * u1 + cw[2:3, :] * u)
    yc_ref[...] = (y * lax.rsqrt(_group_mean(y * y, gm) + EPS) * cg_ref[...]).astype(BF16)

    q = proj(3)
    qT_ref[...] = (q * lax.rsqrt(_group_mean(q * q, gm) + EPS) * qg_ref[...]).T.astype(BF16)
    k = proj(4)
    k_ref[...] = (k * lax.rsqrt(_group_mean(k * k, gm) + EPS) * kg_ref[...]).astype(BF16)
    vt = proj(5).T.astype(BF16)
    vrows = V_DIM + ONES_ROWS
    for h in range(dc // V_DIM):
        vT_ref[0, h * vrows:h * vrows + V_DIM, :] = vt[h * V_DIM:(h + 1) * V_DIM, :]
        vT_ref[0, h * vrows + V_DIM:(h + 1) * vrows, :] = jnp.ones((ONES_ROWS, tm), BF16)


def _inproj(x2, g, w_in, conv_w, conv_g, qg, kg, gmat, *, batch, seq):
    t, d = x2.shape
    dc = conv_g.shape[1]
    dv = dc // V_DIM * (V_DIM + ONES_ROWS)
    tm = TK
    nj = seq // tm
    row = lambda b, j: (b * nj + j, 0)
    const = lambda b, j: (0, 0)
    out_sds = jax.ShapeDtypeStruct((t, dc), BF16)
    return pl.pallas_call(
        functools.partial(_inproj_kernel, tm=tm, dc=dc),
        grid=(batch, nj),
        in_specs=[
            pl.BlockSpec((tm, d), row),
            pl.BlockSpec((1, d), const),
            pl.BlockSpec(w_in.shape, const),
            pl.BlockSpec(conv_w.shape, const),
            pl.BlockSpec((1, dc), const),
            pl.BlockSpec((1, dc), const),
            pl.BlockSpec((1, dc), const),
            pl.BlockSpec(gmat.shape, const),
        ],
        out_specs=[
            pl.BlockSpec((tm, dc), row),
            pl.BlockSpec((dc, tm), lambda b, j: (0, b * nj + j)),
            pl.BlockSpec((tm, dc), row),
            pl.BlockSpec((1, dv, tm), lambda b, j: (b * nj + j, 0, 0)),
        ],
        out_shape=[out_sds, jax.ShapeDtypeStruct((dc, t), BF16), out_sds,
                   jax.ShapeDtypeStruct((t // tm, dv, tm), BF16)],
        scratch_shapes=[pltpu.VMEM((8, dc), F32)],
        compiler_params=pltpu.CompilerParams(
            dimension_semantics=("arbitrary", "arbitrary"), vmem_limit_bytes=VMEM_LIMIT),
        name="inproj_conv_qknorm",
    )(x2, g, w_in, conv_w, conv_g, qg, kg, gmat)


def _attn_kernel(lp_ref, sg_ref, qT_ref, k_ref, vT_ref, o_ref, m_ref, acc_ref,
                 sa_ref, pb_ref, ab_ref, *, tq, tk, nq, lam_init):
    map_a = slice(0, tq)
    map_b = slice(tq, 2 * tq)

    lp = lp_ref[...]
    lam = (jnp.exp(jnp.sum(lp[0:1, :] * lp[1:2, :], axis=-1, keepdims=True))
           - jnp.exp(jnp.sum(lp[2:3, :] * lp[3:4, :], axis=-1, keepdims=True)) + lam_init)

    def stacked_queries(qi):
        qT = qT_ref[:, qi * tq:(qi + 1) * tq]
        row = lax.broadcasted_iota(jnp.int32, qT.shape, 0)
        zero = jnp.zeros_like(qT)
        return jnp.concatenate([jnp.where(row < HEAD_DIM, qT, zero),
                                jnp.where(row >= HEAD_DIM, qT, zero)], axis=1)

    def pieces(diag):
        return ((slice(0, tq // 2), tk // 2), (slice(tq // 2, tq), tk)) if diag else ((slice(0, tq), tk),)

    def shifted(cols, off):
        return slice(cols.start + off, cols.stop + off)

    def scores(j, qq, cols, n_keys):
        return _dot(k_ref[j * tk:j * tk + n_keys, :], qq[:, cols])

    def softmax(m, cols, s, q0):
        if q0 is not None:
            qrel = q0 + lax.broadcasted_iota(jnp.int32, s.shape, 1)
            krel = lax.broadcasted_iota(jnp.int32, s.shape, 0)
            s = jnp.where(krel <= qrel, s, -jnp.inf)
        sb = s.astype(BF16)
        m_old = m[:, cols]
        m_new = jnp.maximum(m_old, jnp.max(sb, axis=0, keepdims=True).astype(F32))
        alpha = jnp.exp2(m_old - m_new)
        p = jnp.exp2(sb - m_new.astype(BF16))
        m[:, cols] = m_new
        return alpha, p

    def accumulate(acc, cols, alpha, vb, p):
        acc[:, cols] = alpha * acc[:, cols] + _dot(vb, p)

    def prefetch_map_a(sa, j, qq, diag):
        for cols, n_keys in pieces(diag):
            sa[0:n_keys, cols] = scores(j, qq, cols, n_keys)

    stacked = {0: stacked_queries(0)}
    m_ref[0] = jnp.full(m_ref.shape[1:], -jnp.inf, F32)
    acc_ref[0] = jnp.zeros(acc_ref.shape[1:], F32)
    prefetch_map_a(sa_ref.at[0], 0, stacked[0], True)

    for qi in range(nq):
        par = qi % 2
        m, acc, sa, qq = m_ref.at[par], acc_ref.at[par], sa_ref.at[par], stacked[qi]
        for j in range(qi + 1):
            last = j == qi
            if j > 0:
                accumulate(acc, map_b, ab_ref[...], vT_ref[j - 1], pb_ref[...])
            s_b = [scores(j, qq, shifted(cols, tq), n_keys) for cols, n_keys in pieces(last)]
            for cols, n_keys in pieces(last):
                alpha_a, p_a = softmax(m, cols, sa[0:n_keys, cols], cols.start if last else None)
                accumulate(acc, cols, alpha_a, vT_ref[j, :, 0:n_keys], p_a)
            if not last:
                prefetch_map_a(sa, j + 1, qq, j + 1 == qi)
            elif qi + 1 < nq:
                stacked[qi + 1] = stacked_queries(qi + 1)
                m_ref[1 - par] = jnp.full(m_ref.shape[1:], -jnp.inf, F32)
                acc_ref[1 - par] = jnp.zeros(acc_ref.shape[1:], F32)
                prefetch_map_a(sa_ref.at[1 - par], 0, stacked[qi + 1], False)
            for (cols, n_keys), s in zip(pieces(last), s_b):
                alpha_b, p_b = softmax(m, shifted(cols, tq), s, cols.start if last else None)
                if last:
                    accumulate(acc, shifted(cols, tq), alpha_b, vT_ref[j, :, 0:n_keys], p_b)
                else:
                    ab_ref[...] = alpha_b
                    pb_ref[...] = p_b

        o = acc[0:V_DIM, :] / acc[V_DIM:V_DIM + 1, :]
        d = o[:, map_a] - lam * o[:, map_b]
        ms = jnp.mean(d * d, axis=0, keepdims=True)
        o_ref[qi * tq:(qi + 1) * tq, :] = (
            d * lax.rsqrt(ms + EPS) * sg_ref[...] * (1.0 - lam_init)).T.astype(BF16)
        del stacked[qi]


def _attention(lam_params, subln_g, qT, k, vT, *, batch, seq, lam_init):
    dq, t = qT.shape
    n_heads = dq // V_DIM
    tq, tk = TQ, TK
    assert tq == tk and vT.shape[2] == tk
    nq = seq // tq
    nk = seq // tk
    vrows = V_DIM + ONES_ROWS
    const = lambda b, h: (0, 0)
    return pl.pallas_call(
        functools.partial(_attn_kernel, tq=tq, tk=tk, nq=nq, lam_init=lam_init),
        grid=(batch, n_heads),
        in_specs=[
            pl.BlockSpec(lam_params.shape, const),
            pl.BlockSpec((V_DIM, 1), const),
            pl.BlockSpec((V_DIM, seq), lambda b, h: (h, b)),
            pl.BlockSpec((seq, V_DIM), lambda b, h: (b, h)),
            pl.BlockSpec((nk, vrows, tk), lambda b, h: (b, h, 0)),
        ],
        out_specs=pl.BlockSpec((seq, V_DIM), lambda b, h: (b, h)),
        out_shape=jax.ShapeDtypeStruct((t, dq), BF16),
        scratch_shapes=[pltpu.VMEM((2, 1, 2 * tq), F32),
                        pltpu.VMEM((2, vrows, 2 * tq), F32),
                        pltpu.VMEM((2, tk, tq), F32), pltpu.VMEM((tk, tq), BF16), pltpu.VMEM((1, tq), F32)],
        compiler_params=pltpu.CompilerParams(
            dimension_semantics=("arbitrary", "arbitrary"), vmem_limit_bytes=VMEM_LIMIT),
        name="diff_attention",
    )(lam_params, subln_g, qT, k, vT)


def _outproj_router_kernel(x_ref, yc_ref, at_ref, wo_ref, g_ref, wr_ref,
                           h_ref, hn_ref, route_ref, routeT_ref, cnt_ref, *, tm, dc):
    i = pl.program_id(0)

    @pl.when(i == 0)
    def _():
        cnt_ref[...] = jnp.zeros_like(cnt_ref)

    for r0 in range(0, x_ref.shape[0], tm):
        _outproj_router_subtile(slice(r0, r0 + tm), x_ref, yc_ref, at_ref, wo_ref, g_ref, wr_ref,
                                h_ref, hn_ref, route_ref, routeT_ref, cnt_ref, tm=tm, dc=dc)


def _outproj_router_subtile(rows, x_ref, yc_ref, at_ref, wo_ref, g_ref, wr_ref,
                            h_ref, hn_ref, route_ref, routeT_ref, cnt_ref, *, tm, dc):
    h = x_ref[rows, :] + _dot(yc_ref[rows, :], wo_ref[0:dc, :]) + _dot(at_ref[rows, :], wo_ref[dc:2 * dc, :])
    h_ref[rows, :] = h
    ms = jnp.mean(h * h, axis=-1, keepdims=True)
    hn = h * lax.rsqrt(ms + EPS) * g_ref[...]
    hi = hn.astype(BF16)
    hn_ref[rows, :] = _pack_rows(hn)
    lo = (hn - hi.astype(F32)).astype(BF16)
    w_hi = wr_ref[:, 0:LANES]
    logits = _dot(hi, w_hi) + _dot(lo, w_hi) + _dot(hi, wr_ref[:, LANES:2 * LANES])

    lt = logits.T
    neg = -jnp.inf
    grow = lax.broadcasted_iota(jnp.int32, (8, tm), 0).astype(F32)
    gl = jnp.where(grow < N_GROUPS, lt[N_EXPERTS:N_EXPERTS + 8, :], neg)
    gmax = jnp.max(gl, axis=0, keepdims=True)
    g_gate = 1.0 / jnp.sum(jnp.exp(gl - gmax), axis=0, keepdims=True)
    g_idx = jnp.min(jnp.where(gl == gmax, grow, 1e9), axis=0, keepdims=True)
    erow = lax.broadcasted_iota(jnp.int32, (N_EXPERTS, tm), 0).astype(F32)
    e_lo = g_idx * EXPERTS_PER_GROUP
    el = jnp.where((erow >= e_lo) & (erow < e_lo + EXPERTS_PER_GROUP), lt[0:N_EXPERTS, :], neg)
    v1 = jnp.max(el, axis=0, keepdims=True)
    i1 = jnp.min(jnp.where(el == v1, erow, 1e9), axis=0, keepdims=True)
    el2 = jnp.where(erow == i1, neg, el)
    v2 = jnp.max(el2, axis=0, keepdims=True)
    i2 = jnp.min(jnp.where(el2 == v2, erow, 1e9), axis=0, keepdims=True)
    tt = jnp.exp(v2 - v1)
    w1 = g_gate / (1.0 + tt)
    w2 = g_gate * tt / (1.0 + tt)

    sel1 = erow == i1
    sel2 = erow == i2
    oh = jnp.where(sel1 | sel2, 1.0, 0.0)
    ss = lax.broadcasted_iota(jnp.int32, (tm, tm), 0)
    tt_i = lax.broadcasted_iota(jnp.int32, (tm, tm), 1)
    earlier = jnp.where(ss < tt_i, 1.0, 0.0).astype(BF16)
    ranks = _dot(oh.astype(BF16), earlier) + cnt_ref[...]
    r1 = jnp.sum(jnp.where(sel1, ranks, 0.0), axis=0, keepdims=True)
    r2 = jnp.sum(jnp.where(sel2, ranks, 0.0), axis=0, keepdims=True)
    cnt_ref[...] = cnt_ref[...] + jnp.sum(oh, axis=1, keepdims=True)

    routeT = jnp.concatenate([i1, i2, r1, r2, w1, w2, jnp.zeros((2, tm), F32)], axis=0)
    routeT_ref[:, rows] = routeT
    route_ref[rows, :] = jnp.concatenate([routeT, jnp.zeros((LANES - 8, tm), F32)], axis=0).T


def _outproj_router(x2, yc, at, w_out, g, wr_cat):
    t, d = x2.shape
    dc = yc.shape[1]
    tm = TM_PROJ
    tb = tm * OUTPROJ_SUBTILES
    row = lambda i: (i, 0)
    const = lambda i: (0, 0)
    return pl.pallas_call(
        functools.partial(_outproj_router_kernel, tm=tm, dc=dc),
        grid=(t // tb,),
        in_specs=[
            pl.BlockSpec((tb, d), row),
            pl.BlockSpec((tb, dc), row),
            pl.BlockSpec((tb, dc), row),
            pl.BlockSpec(w_out.shape, const),
            pl.BlockSpec((1, d), const),
            pl.BlockSpec(wr_cat.shape, const),
        ],
        out_specs=[
            pl.BlockSpec((tb, d), row),
            pl.BlockSpec((tb, d // 2), row),
            pl.BlockSpec((tb, LANES), row),
            pl.BlockSpec((8, tb), lambda i: (0, i)),
            pl.BlockSpec((N_EXPERTS, 1), const),
        ],
        out_shape=[
            jax.ShapeDtypeStruct((t, d), F32),
            jax.ShapeDtypeStruct((t, d // 2), jnp.int32),
            jax.ShapeDtypeStruct((t, LANES), F32),
            jax.ShapeDtypeStruct((8, t), F32),
            jax.ShapeDtypeStruct((N_EXPERTS, 1), F32),
        ],
        compiler_params=pltpu.CompilerParams(
            dimension_semantics=("arbitrary",), vmem_limit_bytes=VMEM_LIMIT),
        name="outproj_router",
    )(x2, yc, at, w_out, g, wr_cat)


def _positions_kernel(offs_ref, rt_ref, pos_ref):
    rt = rt_ref[...]
    ea, eb = rt[0:1, :], rt[1:2, :]
    sa = jnp.zeros_like(ea)
    sb = jnp.zeros_like(eb)
    for e in range(N_EXPERTS):
        start = offs_ref[e].astype(F32)
        sa = jnp.where(ea == e, start, sa)
        sb = jnp.where(eb == e, start, sb)
    pos_ref[0:1, :] = (sa + rt[2:3, :]).astype(jnp.int32)
    pos_ref[1:2, :] = (sb + rt[3:4, :]).astype(jnp.int32)


def _positions(offs, routeT):
    t = routeT.shape[1]
    return pl.pallas_call(
        _positions_kernel,
        grid_spec=pltpu.PrefetchScalarGridSpec(
            num_scalar_prefetch=1, grid=(1,),
            in_specs=[pl.BlockSpec(routeT.shape, lambda i, offs: (0, 0))],
            out_specs=pl.BlockSpec((2, t), lambda i, offs: (0, 0)),
        ),
        out_shape=jax.ShapeDtypeStruct((2, t), jnp.int32),
        name="positions",
    )(offs, routeT)


def _moe_kernel(te_ref, nt_ref, nx_ref, sl_ref, x_ref, wg_hbm, wu_hbm, wd_hbm, y_ref,
                wg_st, wu_st, wd_st, wgb_ref, wub_ref, wdb_ref, sem):
    i = pl.program_id(0)
    last = nt_ref[0] - 1
    ic = jnp.minimum(i, last)
    expert = te_ref[ic]
    slot = sl_ref[ic]
    first_of_expert = (i == 0) | ((i <= last) & (expert != te_ref[jnp.maximum(ic - 1, 0)]))

    def weight_copies(e, s):
        return (pltpu.make_async_copy(wg_hbm.at[e], wg_st.at[s], sem.at[s, 0]),
                pltpu.make_async_copy(wu_hbm.at[e], wu_st.at[s], sem.at[s, 1]),
                pltpu.make_async_copy(wd_hbm.at[e], wd_st.at[s], sem.at[s, 2]))

    @pl.when(i == 0)
    def _():
        for c in weight_copies(expert, slot):
            c.start()

    @pl.when(first_of_expert)
    def _():
        for c in weight_copies(expert, slot):
            c.wait()
        nxt = nx_ref[ic]

        @pl.when(nxt >= 0)
        def _():
            for c in weight_copies(nxt, 1 - slot):
                c.start()

        wgb_ref[...] = wg_st[slot].astype(BF16)
        wub_ref[...] = wu_st[slot].astype(BF16)
        wdb_ref[...] = wd_st[slot].astype(BF16)

    @pl.when(i <= last)
    def _():
        x_l, x_r = _unpack_rows(x_ref[...])
        x_l = x_l.astype(BF16)
        x_r = x_r.astype(BF16)
        half = x_l.shape[1]
        hg = _dot(x_l, wgb_ref[0:half, :]) + _dot(x_r, wgb_ref[half:2 * half, :])
        hu = _dot(x_l, wub_ref[0:half, :]) + _dot(x_r, wub_ref[half:2 * half, :])
        act = hg * (1.0 / (1.0 + jnp.exp(-hg))) * hu
        y_ref[...] = _pack_rows(_dot(act.astype(BF16), wdb_ref[...]))


def _moe(tile_expert, n_tiles, next_expert, stage_slot, xs, wg, wu, wd):
    p, dp = xs.shape
    d, f = wg.shape[1], wg.shape[2]
    assert dp * 2 == d
    tm = TM_MOE
    row = lambda i, te, nt, nx, sl: (jnp.minimum(i, nt[0] - 1), 0)
    hbm = pl.BlockSpec(memory_space=pl.ANY)
    grid_spec = pltpu.PrefetchScalarGridSpec(
        num_scalar_prefetch=4,
        grid=(p // tm,),
        in_specs=[pl.BlockSpec((tm, dp), row), hbm, hbm, hbm],
        out_specs=pl.BlockSpec((tm, dp), row),
        scratch_shapes=[pltpu.VMEM((2, d, f), F32), pltpu.VMEM((2, d, f), F32), pltpu.VMEM((2, f, d), F32),
                        pltpu.VMEM((d, f), BF16), pltpu.VMEM((d, f), BF16), pltpu.VMEM((f, d), BF16),
                        pltpu.SemaphoreType.DMA((2, 3))],
    )
    return pl.pallas_call(
        _moe_kernel,
        grid_spec=grid_spec,
        out_shape=jax.ShapeDtypeStruct((p, dp), jnp.int32),
        compiler_params=pltpu.CompilerParams(
            dimension_semantics=("arbitrary",), vmem_limit_bytes=VMEM_LIMIT),
        name="moe_experts",
    )(tile_expert, n_tiles, next_expert, stage_slot, xs, wg, wu, wd)


def _sc_mesh():
    return plsc.VectorSubcoreMesh(core_axis_name="c", subcore_axis_name="s",
                                  num_cores=SC_CORES, num_subcores=SC_SUBCORES)


def _sc_dispatch(rows, pos_a, pos_b, n_out):
    t, d = rows.shape
    win = pos_a.shape[1]

    @functools.partial(pl.kernel, out_type=jax.ShapeDtypeStruct((n_out, d), rows.dtype),
                       mesh=_sc_mesh(), scratch_types=[], name="sc_dispatch")
    def run(rows_hbm, pa_hbm, pb_hbm, out_hbm):
        def body(rows_vmem, pa_vmem, pb_vmem):
            pltpu.sync_copy(rows_vmem, out_hbm.at[pa_vmem.at[0]])
            pltpu.sync_copy(rows_vmem, out_hbm.at[pb_vmem.at[0]])

        pltpu.emit_pipeline(
            body, grid=(t // win,),
            in_specs=[pl.BlockSpec((win, d), lambda i: (i, 0)),
                      pl.BlockSpec((1, win), lambda i: (i, 0)),
                      pl.BlockSpec((1, win), lambda i: (i, 0))],
            out_specs=[],
            core_axis_name=("c", "s"),
            dimension_semantics=(pltpu.PARALLEL,),
        )(rows_hbm, pa_hbm, pb_hbm)

    return run(rows, pos_a, pos_b)


def _sc_gather(table, idx):
    d = table.shape[1]
    n_win, win = idx.shape

    @functools.partial(pl.kernel, out_type=jax.ShapeDtypeStruct((n_win * win, d), table.dtype),
                       mesh=_sc_mesh(), scratch_types=[], name="sc_gather")
    def run(table_hbm, idx_hbm, out_hbm):
        def body(idx_vmem, out_vmem):
            pltpu.sync_copy(table_hbm.at[idx_vmem.at[0]], out_vmem)

        pltpu.emit_pipeline(
            body, grid=(n_win,),
            in_specs=[pl.BlockSpec((1, win), lambda i: (i, 0))],
            out_specs=[pl.BlockSpec((win, d), lambda i: (i, 0))],
            core_axis_name=("c", "s"),
            dimension_semantics=(pltpu.PARALLEL,),
        )(idx_hbm, out_hbm)

    return run(table, idx)


def _combine_kernel(h_ref, ya_ref, yb_ref, r_ref, o_ref):
    r = r_ref[...]
    wa, wb = r[:, 4:5], r[:, 5:6]
    a_l, a_r = _unpack_rows(ya_ref[...])
    b_l, b_r = _unpack_rows(yb_ref[...])
    half = a_l.shape[1]
    o_ref[:, 0:half] = h_ref[:, 0:half] + wa * a_l + wb * b_l
    o_ref[:, half:2 * half] = h_ref[:, half:2 * half] + wa * a_r + wb * b_r


def _combine(hres, yg, route):
    t, d = hres.shape
    tm = TM_COMBINE
    nb = t // tm
    return pl.pallas_call(
        _combine_kernel,
        grid=(nb,),
        in_specs=[pl.BlockSpec((tm, d), lambda i: (i, 0)),
                  pl.BlockSpec((tm, d // 2), lambda i: (i, 0)),
                  pl.BlockSpec((tm, d // 2), lambda i: (i + nb, 0)),
                  pl.BlockSpec((tm, LANES), lambda i: (i, 0))],
        out_specs=pl.BlockSpec((tm, d), lambda i: (i, 0)),
        out_shape=jax.ShapeDtypeStruct((t, d), F32),
        compiler_params=pltpu.CompilerParams(
            dimension_semantics=("arbitrary",), vmem_limit_bytes=VMEM_LIMIT),
        name="combine",
    )(hres, yg, yg, route)


def _lambda_init(layer_idx):
    return 0.8 - 0.6 * math.exp(-0.3 * layer_idx)


def _layer(h, l, attn_norm_g, w_in, conv_w, conv_out_g, q_norm_g, k_norm_g,
           lambda_q1, lambda_k1, lambda_q2, lambda_k2, attn_subln_g, w_out,
           ffn_norm_g, w_router_group, w_router_expert, w_exp_gate, w_exp_up, w_exp_down):
    batch, seq, d = h.shape
    t = batch * seq
    dc = conv_w.shape[-1]
    lam_init = _lambda_init(l)
    x2 = h.reshape(t, d)

    reps = dc // HEAD_DIM
    assert dc // CONV_GROUPS == HEAD_DIM
    qg = (jnp.tile(q_norm_g[l], reps) * (HEAD_DIM ** -0.5 * math.log2(math.e))).reshape(1, dc)
    kg = jnp.tile(k_norm_g[l], reps).reshape(1, dc)
    grp = jnp.arange(MXU_TILE) // HEAD_DIM
    gmat = jnp.where(grp[:, None] == grp[None, :], 1.0 / HEAD_DIM, 0.0).astype(BF16)
    yc, qT, k, vT = _inproj(x2, attn_norm_g[l].reshape(1, d), w_in[l].astype(BF16), conv_w[l],
                          conv_out_g[l].reshape(1, dc), qg, kg, gmat, batch=batch, seq=seq)

    lam_params = jnp.stack([lambda_q1[l], lambda_k1[l], lambda_q2[l], lambda_k2[l]])
    at = _attention(lam_params, attn_subln_g[l].reshape(V_DIM, 1), qT, k, vT,
                    batch=batch, seq=seq, lam_init=lam_init)

    wr = jnp.concatenate([w_router_expert[l], w_router_group[l],
                          jnp.zeros((d, LANES - N_EXPERTS - N_GROUPS), F32)], axis=1)
    wr_hi = wr.astype(BF16)
    wr_lo = (wr - wr_hi.astype(F32)).astype(BF16)
    hres, hn2, route, routeT, cnt = _outproj_router(x2, yc, at, w_out[l].astype(BF16),
                                                    ffn_norm_g[l].reshape(1, d),
                                                    jnp.concatenate([wr_hi, wr_lo], axis=1))

    tmm = TM_MOE
    n_tiles_max = (2 * t) // tmm + N_EXPERTS
    p_rows = n_tiles_max * tmm
    counts = cnt[:, 0].astype(jnp.int32)
    tiles = (counts + tmm - 1) // tmm
    tile_end = jnp.cumsum(tiles)
    offs = (tile_end - tiles) * tmm
    pos = _positions(offs, routeT)
    n_tiles = tile_end[-1:].astype(jnp.int32)
    tile_ids = jnp.arange(n_tiles_max, dtype=jnp.int32)
    tile_expert = jnp.minimum(
        jnp.sum((tile_end[None, :] <= tile_ids[:, None]).astype(jnp.int32), axis=1), N_EXPERTS - 1)
    e_ids = jnp.arange(N_EXPERTS, dtype=jnp.int32)
    nonempty = tiles > 0
    later = jnp.where(nonempty[None, :] & (e_ids[None, :] > e_ids[:, None]), e_ids[None, :], N_EXPERTS)
    next_of = jnp.min(later, axis=1)
    next_of = jnp.where(next_of == N_EXPERTS, -1, next_of).astype(jnp.int32)
    slot_of = ((jnp.cumsum(nonempty.astype(jnp.int32)) - 1) % 2).astype(jnp.int32)
    onehot_te = (tile_expert[:, None] == e_ids[None, :]).astype(jnp.int32)
    next_expert = jnp.sum(onehot_te * next_of[None, :], axis=1)
    stage_slot = jnp.sum(onehot_te * slot_of[None, :], axis=1)

    posw = pos.reshape(2 * t // SC_WIN, SC_WIN)
    pos1w = posw[:t // SC_WIN]
    pos2w = posw[t // SC_WIN:]
    xs = _sc_dispatch(hn2, pos1w, pos2w, p_rows)

    f = w_exp_gate.shape[-1]
    ys = _moe(tile_expert, n_tiles, next_expert, stage_slot, xs,
              w_exp_gate[l].reshape(N_EXPERTS, d, f),
              w_exp_up[l].reshape(N_EXPERTS, d, f),
              w_exp_down[l].reshape(N_EXPERTS, f, d))
    yg = _sc_gather(ys, posw)
    out = _combine(hres, yg, route)
    return out.reshape(batch, seq, d)


def kernel(x, attn_norm_g, w_in, conv_w, conv_out_g, q_norm_g, k_norm_g, lambda_q1, lambda_k1,
           lambda_q2, lambda_k2, attn_subln_g, w_out, ffn_norm_g, w_router_group, w_router_expert,
           w_exp_gate, w_exp_up, w_exp_down):
    h = x
    for l in range(attn_norm_g.shape[0]):
        h = _layer(h, l, attn_norm_g, w_in, conv_w, conv_out_g, q_norm_g, k_norm_g,
                   lambda_q1, lambda_k1, lambda_q2, lambda_k2, attn_subln_g, w_out,
                   ffn_norm_g, w_router_group, w_router_expert, w_exp_gate, w_exp_up, w_exp_down)
    return h
```

```python
import functools
import math

import jax
import jax.numpy as jnp
from jax import lax
from jax.experimental import pallas as pl
from jax.experimental.pallas import tpu as pltpu
from jax.experimental.pallas import tpu_sc as plsc

F32 = jnp.float32
BF16 = jnp.bfloat16

HEAD_DIM = 64
V_DIM = 2 * HEAD_DIM
CONV_GROUPS = 8
N_GROUPS = 4
EXPERTS_PER_GROUP = 8
N_EXPERTS = N_GROUPS * EXPERTS_PER_GROUP
EPS = 1e-6
LANES = 128
MXU_TILE = 256
ONES_ROWS = 16
VMEM_LIMIT = 48 * 1024 * 1024

TM_PROJ = 256
TM_COMBINE = 1024
OUTPROJ_SUBTILES = 4
TQ = 512
TK = 512
TM_MOE = 256
SC_CORES = 2
SC_SUBCORES = 16
SC_WIN = 64


def _dot(a, b):
    return jnp.dot(a, b, preferred_element_type=F32)


def _pack_rows(x):
    w = x.shape[1] // 2
    bits = lax.bitcast_convert_type(x.astype(BF16).astype(F32), jnp.uint32)
    return lax.bitcast_convert_type((bits[:, :w] >> 16) | bits[:, w:], jnp.int32)


def _unpack_rows(packed):
    bits = lax.bitcast_convert_type(packed, jnp.uint32)
    left = lax.bitcast_convert_type(bits << 16, F32)
    right = lax.bitcast_convert_type(bits & jnp.uint32(0xFFFF0000), F32)
    return left, right


def _group_mean(sq, gm):
    w = gm.shape[0]
    sq = sq.astype(BF16)
    return jnp.concatenate([_dot(sq[:, c:c + w], gm) for c in range(0, sq.shape[1], w)], axis=1)


def _inproj_kernel(x_ref, g_ref, w_ref, cw_ref, cg_ref, qg_ref, kg_ref, gm_ref,
                   yc_ref, qT_ref, k_ref, vT_ref, carry_ref, *, tm, dc):
    j = pl.program_id(1)

    @pl.when(j == 0)
    def _():
        carry_ref[...] = jnp.zeros_like(carry_ref)

    x = x_ref[...]
    ms = jnp.mean(x * x, axis=-1, keepdims=True)
    hn = (x * lax.rsqrt(ms + EPS) * g_ref[...]).astype(BF16)

    def proj(s):
        return _dot(hn, w_ref[:, s * dc:(s + 1) * dc])

    gm = gm_ref[...]

    u = proj(2) * proj(0)
    prev = carry_ref[...]
    rows = lax.broadcasted_iota(jnp.int32, u.shape, 0)
    u1 = jnp.where(rows == 0, prev[7:8, :], pltpu.roll(u, 1, 0))
    u2 = jnp.where(rows == 0, prev[6:7, :], jnp.where(rows == 1, prev[7:8, :], pltpu.roll(u, 2, 0)))
    carry_ref[...] = u[tm - 8:tm, :]
    cw = cw_ref[...]
    y = proj(1) * (cw[0:1, :] * u2 + cw[1:2, :] * u1 + cw[2:3, :] * u)
    yc_ref[...] = (y * lax.rsqrt(_group_mean(y * y, gm) + EPS) * cg_ref[...]).astype(BF16)

    q = proj(3)
    qT_ref[...] = (q * lax.rsqrt(_group_mean(q * q, gm) + EPS) * qg_ref[...]).T.astype(BF16)
    k = proj(4)
    k_ref[...] = (k * lax.rsqrt(_group_mean(k * k, gm) + EPS) * kg_ref[...]).astype(BF16)
    vt = proj(5).T.astype(BF16)
    vrows = V_DIM + ONES_ROWS
    for h in range(dc // V_DIM):
        vT_ref[0, h * vrows:h * vrows + V_DIM, :] = vt[h * V_DIM:(h + 1) * V_DIM, :]
        vT_ref[0, h * vrows + V_DIM:(h + 1) * vrows, :] = jnp.ones((ONES_ROWS, tm), BF16)


def _inproj(x2, g, w_in, conv_w, conv_g, qg, kg, gmat, *, batch, seq):
    t, d = x2.shape
    dc = conv_g.shape[1]
    dv = dc // V_DIM * (V_DIM + ONES_ROWS)
    tm = TK
    nj = seq // tm
    row = lambda b, j: (b * nj + j, 0)
    const = lambda b, j: (0, 0)
    out_sds = jax.ShapeDtypeStruct((t, dc), BF16)
    return pl.pallas_call(
        functools.partial(_inproj_kernel, tm=tm, dc=dc),
        grid=(batch, nj),
        in_specs=[
            pl.BlockSpec((tm, d), row),
            pl.BlockSpec((1, d), const),
            pl.BlockSpec(w_in.shape, const),
            pl.BlockSpec(conv_w.shape, const),
            pl.BlockSpec((1, dc), const),
            pl.BlockSpec((1, dc), const),
            pl.BlockSpec((1, dc), const),
            pl.BlockSpec(gmat.shape, const),
        ],
        out_specs=[
            pl.BlockSpec((tm, dc), row),
            pl.BlockSpec((dc, tm), lambda b, j: (0, b * nj + j)),
            pl.BlockSpec((tm, dc), row),
            pl.BlockSpec((1, dv, tm), lambda b, j: (b * nj + j, 0, 0)),
        ],
        out_shape=[out_sds, jax.ShapeDtypeStruct((dc, t), BF16), out_sds,
                   jax.ShapeDtypeStruct((t // tm, dv, tm), BF16)],
        scratch_shapes=[pltpu.VMEM((8, dc), F32)],
        compiler_params=pltpu.CompilerParams(
            dimension_semantics=("arbitrary", "arbitrary"), vmem_limit_bytes=VMEM_LIMIT),
        name="inproj_conv_qknorm",
    )(x2, g, w_in, conv_w, conv_g, qg, kg, gmat)


def _attn_kernel(lp_ref, sg_ref, qT_ref, k_ref, vT_ref, o_ref, m_ref, acc_ref,
                 sa_ref, pb_ref, ab_ref, *, tq, tk, nq, lam_init):
    map_a = slice(0, tq)
    map_b = slice(tq, 2 * tq)

    lp = lp_ref[...]
    lam = (jnp.exp(jnp.sum(lp[0:1, :] * lp[1:2, :], axis=-1, keepdims=True))
           - jnp.exp(jnp.sum(lp[2:3, :] * lp[3:4, :], axis=-1, keepdims=True)) + lam_init)

    def stacked_queries(qi):
        qT = qT_ref[:, qi * tq:(qi + 1) * tq]
        row = lax.broadcasted_iota(jnp.int32, qT.shape, 0)
        zero = jnp.zeros_like(qT)
        return jnp.concatenate([jnp.where(row < HEAD_DIM, qT, zero),
                                jnp.where(row >= HEAD_DIM, qT, zero)], axis=1)

    def pieces(diag):
        return ((slice(0, tq // 2), tk // 2), (slice(tq // 2, tq), tk)) if diag else ((slice(0, tq), tk),)

    def shifted(cols, off):
        return slice(cols.start + off, cols.stop + off)

    def scores(j, qq, cols, n_keys):
        return _dot(k_ref[j * tk:j * tk + n_keys, :], qq[:, cols])

    def softmax(m, cols, s, q0):
        if q0 is not None:
            qrel = q0 + lax.broadcasted_iota(jnp.int32, s.shape, 1)
            krel = lax.broadcasted_iota(jnp.int32, s.shape, 0)
            s = jnp.where(krel <= qrel, s, -jnp.inf)
        sb = s.astype(BF16)
        m_old = m[:, cols]
        m_new = jnp.maximum(m_old, jnp.max(sb, axis=0, keepdims=True).astype(F32))
        alpha = jnp.exp2(m_old - m_new)
        p = jnp.exp2(sb - m_new.astype(BF16))
        m[:, cols] = m_new
        return alpha, p

    def accumulate(acc, cols, alpha, vb, p):
        acc[:, cols] = alpha * acc[:, cols] + _dot(vb, p)

    def prefetch_map_a(sa, j, qq, diag):
        for cols, n_keys in pieces(diag):
            sa[0:n_keys, cols] = scores(j, qq, cols, n_keys)

    stacked = {0: stacked_queries(0)}
    m_ref[0] = jnp.full(m_ref.shape[1:], -jnp.inf, F32)
    acc_ref[0] = jnp.zeros(acc_ref.shape[1:], F32)
    prefetch_map_a(sa_ref.at[0], 0, stacked[0], True)

    for qi in range(nq):
        par = qi % 2
        m, acc, sa, qq = m_ref.at[par], acc_ref.at[par], sa_ref.at[par], stacked[qi]
        for j in range(qi + 1):
            last = j == qi
            if j > 0:
                accumulate(acc, map_b, ab_ref[...], vT_ref[j - 1], pb_ref[...])
            s_b = [scores(j, qq, shifted(cols, tq), n_keys) for cols, n_keys in pieces(last)]
            for cols, n_keys in pieces(last):
                alpha_a, p_a = softmax(m, cols, sa[0:n_keys, cols], cols.start if last else None)
                accumulate(acc, cols, alpha_a, vT_ref[j, :, 0:n_keys], p_a)
            if not last:
                prefetch_map_a(sa, j + 1, qq, j + 1 == qi)
            elif qi + 1 < nq:
                stacked[qi + 1] = stacked_queries(qi + 1)
                m_ref[1 - par] = jnp.full(m_ref.shape[1:], -jnp.inf, F32)
                acc_ref[1 - par] = jnp.zeros(acc_ref.shape[1:], F32)
                prefetch_map_a(sa_ref.at[1 - par], 0, stacked[qi + 1], False)
            for (cols, n_keys), s in zip(pieces(last), s_b):
                alpha_b, p_b = softmax(m, shifted(cols, tq), s, cols.start if last else None)
                if last:
                    accumulate(acc, shifted(cols, tq), alpha_b, vT_ref[j, :, 0:n_keys], p_b)
                else:
                    ab_ref[...] = alpha_b
                    pb_ref[...] = p_b

        o = acc[0:V_DIM, :] / acc[V_DIM:V_DIM + 1, :]
        d = o[:, map_a] - lam * o[:, map_b]
        ms = jnp.mean(d * d, axis=0, keepdims=True)
        o_ref[qi * tq:(qi + 1) * tq, :] = (
            d * lax.rsqrt(ms + EPS) * sg_ref[...] * (1.0 - lam_init)).T.astype(BF16)
        del stacked[qi]


def _attention(lam_params, subln_g, qT, k, vT, *, batch, seq, lam_init):
    dq, t = qT.shape
    n_heads = dq // V_DIM
    tq, tk = TQ, TK
    assert tq == tk and vT.shape[2] == tk
    nq = seq // tq
    nk = seq // tk
    vrows = V_DIM + ONES_ROWS
    const = lambda b, h: (0, 0)
    return pl.pallas_call(
        functools.partial(_attn_kernel, tq=tq, tk=tk, nq=nq, lam_init=lam_init),
        grid=(batch, n_heads),
        in_specs=[
            pl.BlockSpec(lam_params.shape, const),
            pl.BlockSpec((V_DIM, 1), const),
            pl.BlockSpec((V_DIM, seq), lambda b, h: (h, b)),
            pl.BlockSpec((seq, V_DIM), lambda b, h: (b, h)),
            pl.BlockSpec((nk, vrows, tk), lambda b, h: (b, h, 0)),
        ],
        out_specs=pl.BlockSpec((seq, V_DIM), lambda b, h: (b, h)),
        out_shape=jax.ShapeDtypeStruct((t, dq), BF16),
        scratch_shapes=[pltpu.VMEM((2, 1, 2 * tq), F32),
                        pltpu.VMEM((2, vrows, 2 * tq), F32),
                        pltpu.VMEM((2, tk, tq), F32), pltpu.VMEM((tk, tq), BF16), pltpu.VMEM((1, tq), F32)],
        compiler_params=pltpu.CompilerParams(
            dimension_semantics=("arbitrary", "arbitrary"), vmem_limit_bytes=VMEM_LIMIT),
        name="diff_attention",
    )(lam_params, subln_g, qT, k, vT)


def _outproj_router_kernel(x_ref, yc_ref, at_ref, wo_ref, g_ref, wr_ref,
                           h_ref, hn_ref, route_ref, routeT_ref, cnt_ref, *, tm, dc):
    i = pl.program_id(0)

    @pl.when(i == 0)
    def _():
        cnt_ref[...] = jnp.zeros_like(cnt_ref)

    for r0 in range(0, x_ref.shape[0], tm):
        _outproj_router_subtile(slice(r0, r0 + tm), x_ref, yc_ref, at_ref, wo_ref, g_ref, wr_ref,
                                h_ref, hn_ref, route_ref, routeT_ref, cnt_ref, tm=tm, dc=dc)


def _outproj_router_subtile(rows, x_ref, yc_ref, at_ref, wo_ref, g_ref, wr_ref,
                            h_ref, hn_ref, route_ref, routeT_ref, cnt_ref, *, tm, dc):
    h = x_ref[rows, :] + _dot(yc_ref[rows, :], wo_ref[0:dc, :]) + _dot(at_ref[rows, :], wo_ref[dc:2 * dc, :])
    h_ref[rows, :] = h
    ms = jnp.mean(h * h, axis=-1, keepdims=True)
    hn = h * lax.rsqrt(ms + EPS) * g_ref[...]
    hi = hn.astype(BF16)
    hn_ref[rows, :] = _pack_rows(hn)
    lo = (hn - hi.astype(F32)).astype(BF16)
    prod = _dot(hi, wr_ref[...])
    logits = prod[:, 0:LANES] + prod[:, LANES:2 * LANES] + _dot(lo, wr_ref[:, 0:LANES])

    lt = logits.T
    neg = -jnp.inf
    grow = lax.broadcasted_iota(jnp.int32, (8, tm), 0).astype(F32)
    gl = jnp.where(grow < N_GROUPS, lt[N_EXPERTS:N_EXPERTS + 8, :], neg)
    gmax = jnp.max(gl, axis=0, keepdims=True)
    g_gate = 1.0 / jnp.sum(jnp.exp(gl - gmax), axis=0, keepdims=True)
    g_idx = jnp.min(jnp.where(gl == gmax, grow, 1e9), axis=0, keepdims=True)
    erow = lax.broadcasted_iota(jnp.int32, (N_EXPERTS, tm), 0).astype(F32)
    e_lo = g_idx * EXPERTS_PER_GROUP
    el = jnp.where((erow >= e_lo) & (erow < e_lo + EXPERTS_PER_GROUP), lt[0:N_EXPERTS, :], neg)
    v1 = jnp.max(el, axis=0, keepdims=True)
    i1 = jnp.min(jnp.where(el == v1, erow, 1e9), axis=0, keepdims=True)
    el2 = jnp.where(erow == i1, neg, el)
    v2 = jnp.max(el2, axis=0, keepdims=True)
    i2 = jnp.min(jnp.where(el2 == v2, erow, 1e9), axis=0, keepdims=True)
    tt = jnp.exp(v2 - v1)
    w1 = g_gate / (1.0 + tt)
    w2 = g_gate * tt / (1.0 + tt)

    sel1 = erow == i1
    sel2 = erow == i2
    oh = jnp.where(sel1 | sel2, 1.0, 0.0)
    ss = lax.broadcasted_iota(jnp.int32, (tm, tm), 0)
    tt_i = lax.broadcasted_iota(jnp.int32, (tm, tm), 1)
    earlier = jnp.where(ss < tt_i, 1.0, 0.0).astype(BF16)
    ranks = _dot(oh.astype(BF16), earlier) + cnt_ref[...]
    r1 = jnp.sum(jnp.where(sel1, ranks, 0.0), axis=0, keepdims=True)
    r2 = jnp.sum(jnp.where(sel2, ranks, 0.0), axis=0, keepdims=True)
    cnt_ref[...] = cnt_ref[...] + jnp.sum(oh, axis=1, keepdims=True)

    routeT = jnp.concatenate([i1, i2, r1, r2, w1, w2, jnp.zeros((2, tm), F32)], axis=0)
    routeT_ref[:, rows] = routeT
    route_ref[rows, :] = jnp.concatenate([routeT, jnp.zeros((LANES - 8, tm), F32)], axis=0).T


def _outproj_router(x2, yc, at, w_out, g, wr_cat):
    t, d = x2.shape
    dc = yc.shape[1]
    tm = TM_PROJ
    tb = tm * OUTPROJ_SUBTILES
    row = lambda i: (i, 0)
    const = lambda i: (0, 0)
    return pl.pallas_call(
        functools.partial(_outproj_router_kernel, tm=tm, dc=dc),
        grid=(t // tb,),
        in_specs=[
            pl.BlockSpec((tb, d), row),
            pl.BlockSpec((tb, dc), row),
            pl.BlockSpec((tb, dc), row),
            pl.BlockSpec(w_out.shape, const),
            pl.BlockSpec((1, d), const),
            pl.BlockSpec(wr_cat.shape, const),
        ],
        out_specs=[
            pl.BlockSpec((tb, d), row),
            pl.BlockSpec((tb, d // 2), row),
            pl.BlockSpec((tb, LANES), row),
            pl.BlockSpec((8, tb), lambda i: (0, i)),
            pl.BlockSpec((N_EXPERTS, 1), const),
        ],
        out_shape=[
            jax.ShapeDtypeStruct((t, d), F32),
            jax.ShapeDtypeStruct((t, d // 2), jnp.int32),
            jax.ShapeDtypeStruct((t, LANES), F32),
            jax.ShapeDtypeStruct((8, t), F32),
            jax.ShapeDtypeStruct((N_EXPERTS, 1), F32),
        ],
        compiler_params=pltpu.CompilerParams(
            dimension_semantics=("arbitrary",), vmem_limit_bytes=VMEM_LIMIT),
        name="outproj_router",
    )(x2, yc, at, w_out, g, wr_cat)


def _positions_kernel(offs_ref, rt_ref, pos_ref):
    rt = rt_ref[...]
    ea, eb = rt[0:1, :], rt[1:2, :]
    sa = jnp.zeros_like(ea)
    sb = jnp.zeros_like(eb)
    for e in range(N_EXPERTS):
        start = offs_ref[e].astype(F32)
        sa = jnp.where(ea == e, start, sa)
        sb = jnp.where(eb == e, start, sb)
    pos_ref[0:1, :] = (sa + rt[2:3, :]).astype(jnp.int32)
    pos_ref[1:2, :] = (sb + rt[3:4, :]).astype(jnp.int32)


def _positions(offs, routeT):
    t = routeT.shape[1]
    return pl.pallas_call(
        _positions_kernel,
        grid_spec=pltpu.PrefetchScalarGridSpec(
            num_scalar_prefetch=1, grid=(1,),
            in_specs=[pl.BlockSpec(routeT.shape, lambda i, offs: (0, 0))],
            out_specs=pl.BlockSpec((2, t), lambda i, offs: (0, 0)),
        ),
        out_shape=jax.ShapeDtypeStruct((2, t), jnp.int32),
        name="positions",
    )(offs, routeT)


def _moe_kernel(te_ref, nt_ref, nx_ref, sl_ref, x_ref, wg_hbm, wu_hbm, wd_hbm, y_ref,
                wg_st, wu_st, wd_st, wgb_ref, wub_ref, wdb_ref, sem):
    i = pl.program_id(0)
    last = nt_ref[0] - 1
    ic = jnp.minimum(i, last)
    expert = te_ref[ic]
    slot = sl_ref[ic]
    first_of_expert = (i == 0) | ((i <= last) & (expert != te_ref[jnp.maximum(ic - 1, 0)]))

    def weight_copies(e, s):
        return (pltpu.make_async_copy(wg_hbm.at[e], wg_st.at[s], sem.at[s, 0]),
                pltpu.make_async_copy(wu_hbm.at[e], wu_st.at[s], sem.at[s, 1]),
                pltpu.make_async_copy(wd_hbm.at[e], wd_st.at[s], sem.at[s, 2]))

    @pl.when(i == 0)
    def _():
        for c in weight_copies(expert, slot):
            c.start()

    @pl.when(first_of_expert)
    def _():
        for c in weight_copies(expert, slot):
            c.wait()
        nxt = nx_ref[ic]

        @pl.when(nxt >= 0)
        def _():
            for c in weight_copies(nxt, 1 - slot):
                c.start()

        wgb_ref[...] = wg_st[slot].astype(BF16)
        wub_ref[...] = wu_st[slot].astype(BF16)
        wdb_ref[...] = wd_st[slot].astype(BF16)

    @pl.when(i <= last)
    def _():
        x_l, x_r = _unpack_rows(x_ref[...])
        x_l = x_l.astype(BF16)
        x_r = x_r.astype(BF16)
        half = x_l.shape[1]
        hg = _dot(x_l, wgb_ref[0:half, :]) + _dot(x_r, wgb_ref[half:2 * half, :])
        hu = _dot(x_l, wub_ref[0:half, :]) + _dot(x_r, wub_ref[half:2 * half, :])
        act = hg * (1.0 / (1.0 + jnp.exp(-hg))) * hu
        y_ref[...] = _pack_rows(_dot(act.astype(BF16), wdb_ref[...]))


def _moe(tile_expert, n_tiles, next_expert, stage_slot, xs, wg, wu, wd):
    p, dp = xs.shape
    d, f = wg.shape[1], wg.shape[2]
    assert dp * 2 == d
    tm = TM_MOE
    row = lambda i, te, nt, nx, sl: (jnp.minimum(i, nt[0] - 1), 0)
    hbm = pl.BlockSpec(memory_space=pl.ANY)
    grid_spec = pltpu.PrefetchScalarGridSpec(
        num_scalar_prefetch=4,
        grid=(p // tm,),
        in_specs=[pl.BlockSpec((tm, dp), row), hbm, hbm, hbm],
        out_specs=pl.BlockSpec((tm, dp), row),
        scratch_shapes=[pltpu.VMEM((2, d, f), F32), pltpu.VMEM((2, d, f), F32), pltpu.VMEM((2, f, d), F32),
                        pltpu.VMEM((d, f), BF16), pltpu.VMEM((d, f), BF16), pltpu.VMEM((f, d), BF16),
                        pltpu.SemaphoreType.DMA((2, 3))],
    )
    return pl.pallas_call(
        _moe_kernel,
        grid_spec=grid_spec,
        out_shape=jax.ShapeDtypeStruct((p, dp), jnp.int32),
        compiler_params=pltpu.CompilerParams(
            dimension_semantics=("arbitrary",), vmem_limit_bytes=VMEM_LIMIT),
        name="moe_experts",
    )(tile_expert, n_tiles, next_expert, stage_slot, xs, wg, wu, wd)


def _sc_mesh():
    return plsc.VectorSubcoreMesh(core_axis_name="c", subcore_axis_name="s",
                                  num_cores=SC_CORES, num_subcores=SC_SUBCORES)


def _sc_dispatch(rows, pos_a, pos_b, n_out):
    t, d = rows.shape
    win = pos_a.shape[1]

    @functools.partial(pl.kernel, out_type=jax.ShapeDtypeStruct((n_out, d), rows.dtype),
                       mesh=_sc_mesh(), scratch_types=[], name="sc_dispatch")
    def run(rows_hbm, pa_hbm, pb_hbm, out_hbm):
        def body(rows_vmem, pa_vmem, pb_vmem):
            pltpu.sync_copy(rows_vmem, out_hbm.at[pa_vmem.at[0]])
            pltpu.sync_copy(rows_vmem, out_hbm.at[pb_vmem.at[0]])

        pltpu.emit_pipeline(
            body, grid=(t // win,),
            in_specs=[pl.BlockSpec((win, d), lambda i: (i, 0)),
                      pl.BlockSpec((1, win), lambda i: (i, 0)),
                      pl.BlockSpec((1, win), lambda i: (i, 0))],
            out_specs=[],
            core_axis_name=("c", "s"),
            dimension_semantics=(pltpu.PARALLEL,),
        )(rows_hbm, pa_hbm, pb_hbm)

    return run(rows, pos_a, pos_b)


def _sc_gather(table, idx):
    d = table.shape[1]
    n_win, win = idx.shape

    @functools.partial(pl.kernel, out_type=jax.ShapeDtypeStruct((n_win * win, d), table.dtype),
                       mesh=_sc_mesh(), scratch_types=[], name="sc_gather")
    def run(table_hbm, idx_hbm, out_hbm):
        def body(idx_vmem, out_vmem):
            pltpu.sync_copy(table_hbm.at[idx_vmem.at[0]], out_vmem)

        pltpu.emit_pipeline(
            body, grid=(n_win,),
            in_specs=[pl.BlockSpec((1, win), lambda i: (i, 0))],
            out_specs=[pl.BlockSpec((win, d), lambda i: (i, 0))],
            core_axis_name=("c", "s"),
            dimension_semantics=(pltpu.PARALLEL,),
        )(idx_hbm, out_hbm)

    return run(table, idx)


def _combine_kernel(h_ref, ya_ref, yb_ref, r_ref, o_ref):
    r = r_ref[...]
    wa, wb = r[:, 4:5], r[:, 5:6]
    a_l, a_r = _unpack_rows(ya_ref[...])
    b_l, b_r = _unpack_rows(yb_ref[...])
    half = a_l.shape[1]
    o_ref[:, 0:half] = h_ref[:, 0:half] + wa * a_l + wb * b_l
    o_ref[:, half:2 * half] = h_ref[:, half:2 * half] + wa * a_r + wb * b_r


def _combine(hres, yg, route):
    t, d = hres.shape
    tm = TM_COMBINE
    nb = t // tm
    return pl.pallas_call(
        _combine_kernel,
        grid=(nb,),
        in_specs=[pl.BlockSpec((tm, d), lambda i: (i, 0)),
                  pl.BlockSpec((tm, d // 2), lambda i: (i, 0)),
                  pl.BlockSpec((tm, d // 2), lambda i: (i + nb, 0)),
                  pl.BlockSpec((tm, LANES), lambda i: (i, 0))],
        out_specs=pl.BlockSpec((tm, d), lambda i: (i, 0)),
        out_shape=jax.ShapeDtypeStruct((t, d), F32),
        compiler_params=pltpu.CompilerParams(
            dimension_semantics=("arbitrary",), vmem_limit_bytes=VMEM_LIMIT),
        name="combine",
    )(hres, yg, yg, route)


def _lambda_init(layer_idx):
    return 0.8 - 0.6 * math.exp(-0.3 * layer_idx)


def _layer(h, l, attn_norm_g, w_in, conv_w, conv_out_g, q_norm_g, k_norm_g,
           lambda_q1, lambda_k1, lambda_q2, lambda_k2, attn_subln_g, w_out,
           ffn_norm_g, w_router_group, w_router_expert, w_exp_gate, w_exp_up, w_exp_down):
    batch, seq, d = h.shape
    t = batch * seq
    dc = conv_w.shape[-1]
    lam_init = _lambda_init(l)
    x2 = h.reshape(t, d)

    reps = dc // HEAD_DIM
    assert dc // CONV_GROUPS == HEAD_DIM
    qg = (jnp.tile(q_norm_g[l], reps) * (HEAD_DIM ** -0.5 * math.log2(math.e))).reshape(1, dc)
    kg = jnp.tile(k_norm_g[l], reps).reshape(1, dc)
    grp = jnp.arange(MXU_TILE) // HEAD_DIM
    gmat = jnp.where(grp[:, None] == grp[None, :], 1.0 / HEAD_DIM, 0.0).astype(BF16)
    yc, qT, k, vT = _inproj(x2, attn_norm_g[l].reshape(1, d), w_in[l].astype(BF16), conv_w[l],
                          conv_out_g[l].reshape(1, dc), qg, kg, gmat, batch=batch, seq=seq)

    lam_params = jnp.stack([lambda_q1[l], lambda_k1[l], lambda_q2[l], lambda_k2[l]])
    at = _attention(lam_params, attn_subln_g[l].reshape(V_DIM, 1), qT, k, vT,
                    batch=batch, seq=seq, lam_init=lam_init)

    wr = jnp.concatenate([w_router_expert[l], w_router_group[l],
                          jnp.zeros((d, LANES - N_EXPERTS - N_GROUPS), F32)], axis=1)
    wr_hi = wr.astype(BF16)
    wr_lo = (wr - wr_hi.astype(F32)).astype(BF16)
    hres, hn2, route, routeT, cnt = _outproj_router(x2, yc, at, w_out[l].astype(BF16),
                                                    ffn_norm_g[l].reshape(1, d),
                                                    jnp.concatenate([wr_hi, wr_lo], axis=1))

    tmm = TM_MOE
    n_tiles_max = (2 * t) // tmm + N_EXPERTS
    p_rows = n_tiles_max * tmm
    counts = cnt[:, 0].astype(jnp.int32)
    tiles = (counts + tmm - 1) // tmm
    tile_end = jnp.cumsum(tiles)
    offs = (tile_end - tiles) * tmm
    pos = _positions(offs, routeT)
    n_tiles = tile_end[-1:].astype(jnp.int32)
    tile_ids = jnp.arange(n_tiles_max, dtype=jnp.int32)
    tile_expert = jnp.minimum(
        jnp.sum((tile_end[None, :] <= tile_ids[:, None]).astype(jnp.int32), axis=1), N_EXPERTS - 1)
    e_ids = jnp.arange(N_EXPERTS, dtype=jnp.int32)
    nonempty = tiles > 0
    later = jnp.where(nonempty[None, :] & (e_ids[None, :] > e_ids[:, None]), e_ids[None, :], N_EXPERTS)
    next_of = jnp.min(later, axis=1)
    next_of = jnp.where(next_of == N_EXPERTS, -1, next_of).astype(jnp.int32)
    slot_of = ((jnp.cumsum(nonempty.astype(jnp.int32)) - 1) % 2).astype(jnp.int32)
    onehot_te = (tile_expert[:, None] == e_ids[None, :]).astype(jnp.int32)
    next_expert = jnp.sum(onehot_te * next_of[None, :], axis=1)
    stage_slot = jnp.sum(onehot_te * slot_of[None, :], axis=1)

    posw = pos.reshape(2 * t // SC_WIN, SC_WIN)
    pos1w = posw[:t // SC_WIN]
    pos2w = posw[t // SC_WIN:]
    xs = _sc_dispatch(hn2, pos1w, pos2w, p_rows)

    f = w_exp_gate.shape[-1]
    ys = _moe(tile_expert, n_tiles, next_expert, stage_slot, xs,
              w_exp_gate[l].reshape(N_EXPERTS, d, f),
              w_exp_up[l].reshape(N_EXPERTS, d, f),
              w_exp_down[l].reshape(N_EXPERTS, f, d))
    yg = _sc_gather(ys, posw)
    out = _combine(hres, yg, route)
    return out.reshape(batch, seq, d)


def kernel(x, attn_norm_g, w_in, conv_w, conv_out_g, q_norm_g, k_norm_g, lambda_q1, lambda_k1,
           lambda_q2, lambda_k2, attn_subln_g, w_out, ffn_norm_g, w_router_group, w_router_expert,
           w_exp_gate, w_exp_up, w_exp_down):
    h = x
    for l in range(attn_norm_g.shape[0]):
        h = _layer(h, l, attn_norm_g, w_in, conv_w, conv_out_g, q_norm_g, k_norm_g,
                   lambda_q1, lambda_k1, lambda_q2, lambda_k2, attn_subln_g, w_out,
                   ffn_norm_g, w_router_group, w_router_expert, w_exp_gate, w_exp_up, w_exp_down)
    return h
```

```python
import functools
import math

import jax
import jax.numpy as jnp
from jax import lax
from jax.experimental import pallas as pl
from jax.experimental.pallas import tpu as pltpu
from jax.experimental.pallas import tpu_sc as plsc

F32 = jnp.float32
BF16 = jnp.bfloat16

HEAD_DIM = 64
V_DIM = 2 * HEAD_DIM
CONV_GROUPS = 8
N_GROUPS = 4
EXPERTS_PER_GROUP = 8
N_EXPERTS = N_GROUPS * EXPERTS_PER_GROUP
EPS = 1e-6
LANES = 128
MXU_TILE = 256
ONES_ROWS = 16
VMEM_LIMIT = 48 * 1024 * 1024

TM_PROJ = 256
TM_COMBINE = 1024
OUTPROJ_SUBTILES = 4
TQ = 512
TK = 512
TM_MOE = 512
SC_CORES = 2
SC_SUBCORES = 16
SC_WIN = 64


def _dot(a, b):
    return jnp.dot(a, b, preferred_element_type=F32)


def _pack_rows(x):
    w = x.shape[1] // 2
    bits = lax.bitcast_convert_type(x.astype(BF16).astype(F32), jnp.uint32)
    return lax.bitcast_convert_type((bits[:, :w] >> 16) | bits[:, w:], jnp.int32)


def _unpack_rows(packed):
    bits = lax.bitcast_convert_type(packed, jnp.uint32)
    left = lax.bitcast_convert_type(bits << 16, F32)
    right = lax.bitcast_convert_type(bits & jnp.uint32(0xFFFF0000), F32)
    return left, right


def _group_mean(sq, gm):
    w = gm.shape[0]
    sq = sq.astype(BF16)
    return jnp.concatenate([_dot(sq[:, c:c + w], gm) for c in range(0, sq.shape[1], w)], axis=1)


def _inproj_kernel(x_ref, g_ref, w_ref, cw_ref, cg_ref, qg_ref, kg_ref, gm_ref,
                   yc_ref, qT_ref, k_ref, vT_ref, carry_ref, *, tm, dc):
    j = pl.program_id(1)

    @pl.when(j == 0)
    def _():
        carry_ref[...] = jnp.zeros_like(carry_ref)

    x = x_ref[...]
    ms = jnp.mean(x * x, axis=-1, keepdims=True)
    hn = (x * lax.rsqrt(ms + EPS) * g_ref[...]).astype(BF16)

    def proj(s):
        return _dot(hn, w_ref[:, s * dc:(s + 1) * dc])

    gm = gm_ref[...]

    u = proj(2) * proj(0)
    prev = carry_ref[...]
    rows = lax.broadcasted_iota(jnp.int32, u.shape, 0)
    u1 = jnp.where(rows == 0, prev[7:8, :], pltpu.roll(u, 1, 0))
    u2 = jnp.where(rows == 0, prev[6:7, :], jnp.where(rows == 1, prev[7:8, :], pltpu.roll(u, 2, 0)))
    carry_ref[...] = u[tm - 8:tm, :]
    cw = cw_ref[...]
    y = proj(1) * (cw[0:1, :] * u2 + cw[1:2, :] * u1 + cw[2:3, :] * u)
    yc_ref[...] = (y * lax.rsqrt(_group_mean(y * y, gm) + EPS) * cg_ref[...]).astype(BF16)

    q = proj(3)
    qT_ref[...] = (q * lax.rsqrt(_group_mean(q * q, gm) + EPS) * qg_ref[...]).T.astype(BF16)
    k = proj(4)
    k_ref[...] = (k * lax.rsqrt(_group_mean(k * k, gm) + EPS) * kg_ref[...]).astype(BF16)
    vt = proj(5).T.astype(BF16)
    vrows = V_DIM + ONES_ROWS
    for h in range(dc // V_DIM):
        vT_ref[0, h * vrows:h * vrows + V_DIM, :] = vt[h * V_DIM:(h + 1) * V_DIM, :]
        vT_ref[0, h * vrows + V_DIM:(h + 1) * vrows, :] = jnp.ones((ONES_ROWS, tm), BF16)


def _inproj(x2, g, w_in, conv_w, conv_g, qg, kg, gmat, *, batch, seq):
    t, d = x2.shape
    dc = conv_g.shape[1]
    dv = dc // V_DIM * (V_DIM + ONES_ROWS)
    tm = TK
    nj = seq // tm
    row = lambda b, j: (b * nj + j, 0)
    const = lambda b, j: (0, 0)
    out_sds = jax.ShapeDtypeStruct((t, dc), BF16)
    return pl.pallas_call(
        functools.partial(_inproj_kernel, tm=tm, dc=dc),
        grid=(batch, nj),
        in_specs=[
            pl.BlockSpec((tm, d), row),
            pl.BlockSpec((1, d), const),
            pl.BlockSpec(w_in.shape, const),
            pl.BlockSpec(conv_w.shape, const),
            pl.BlockSpec((1, dc), const),
            pl.BlockSpec((1, dc), const),
            pl.BlockSpec((1, dc), const),
            pl.BlockSpec(gmat.shape, const),
        ],
        out_specs=[
            pl.BlockSpec((tm, dc), row),
            pl.BlockSpec((dc, tm), lambda b, j: (0, b * nj + j)),
            pl.BlockSpec((tm, dc), row),
            pl.BlockSpec((1, dv, tm), lambda b, j: (b * nj + j, 0, 0)),
        ],
        out_shape=[out_sds, jax.ShapeDtypeStruct((dc, t), BF16), out_sds,
                   jax.ShapeDtypeStruct((t // tm, dv, tm), BF16)],
        scratch_shapes=[pltpu.VMEM((8, dc), F32)],
        compiler_params=pltpu.CompilerParams(
            dimension_semantics=("arbitrary", "arbitrary"), vmem_limit_bytes=VMEM_LIMIT),
        name="inproj_conv_qknorm",
    )(x2, g, w_in, conv_w, conv_g, qg, kg, gmat)


def _attn_kernel(lp_ref, sg_ref, qT_ref, k_ref, vT_ref, o_ref, m_ref, acc_ref,
                 sa_ref, pb_ref, ab_ref, *, tq, tk, nq, lam_init):
    map_a = slice(0, tq)
    map_b = slice(tq, 2 * tq)

    lp = lp_ref[...]
    lam = (jnp.exp(jnp.sum(lp[0:1, :] * lp[1:2, :], axis=-1, keepdims=True))
           - jnp.exp(jnp.sum(lp[2:3, :] * lp[3:4, :], axis=-1, keepdims=True)) + lam_init)

    def stacked_queries(qi):
        qT = qT_ref[:, qi * tq:(qi + 1) * tq]
        row = lax.broadcasted_iota(jnp.int32, qT.shape, 0)
        zero = jnp.zeros_like(qT)
        return jnp.concatenate([jnp.where(row < HEAD_DIM, qT, zero),
                                jnp.where(row >= HEAD_DIM, qT, zero)], axis=1)

    def pieces(diag):
        return ((slice(0, tq // 2), tk // 2), (slice(tq // 2, tq), tk)) if diag else ((slice(0, tq), tk),)

    def shifted(cols, off):
        return slice(cols.start + off, cols.stop + off)

    def scores(j, qq, cols, n_keys):
        return _dot(k_ref[j * tk:j * tk + n_keys, :], qq[:, cols])

    def softmax(m, cols, s, q0):
        if q0 is not None:
            qrel = q0 + lax.broadcasted_iota(jnp.int32, s.shape, 1)
            krel = lax.broadcasted_iota(jnp.int32, s.shape, 0)
            s = jnp.where(krel <= qrel, s, -jnp.inf)
        sb = s.astype(BF16)
        m_old = m[:, cols]
        m_new = jnp.maximum(m_old, jnp.max(sb, axis=0, keepdims=True).astype(F32))
        alpha = jnp.exp2(m_old - m_new)
        p = jnp.exp2(sb - m_new.astype(BF16))
        m[:, cols] = m_new
        return alpha, p

    def accumulate(acc, cols, alpha, vb, p):
        acc[:, cols] = alpha * acc[:, cols] + _dot(vb, p)

    def prefetch_map_a(sa, j, qq, diag):
        for cols, n_keys in pieces(diag):
            sa[0:n_keys, cols] = scores(j, qq, cols, n_keys)

    stacked = {0: stacked_queries(0)}
    m_ref[0] = jnp.full(m_ref.shape[1:], -jnp.inf, F32)
    acc_ref[0] = jnp.zeros(acc_ref.shape[1:], F32)
    prefetch_map_a(sa_ref.at[0], 0, stacked[0], True)

    for qi in range(nq):
        par = qi % 2
        m, acc, sa, qq = m_ref.at[par], acc_ref.at[par], sa_ref.at[par], stacked[qi]
        for j in range(qi + 1):
            last = j == qi
            if j > 0:
                accumulate(acc, map_b, ab_ref[...], vT_ref[j - 1], pb_ref[...])
            s_b = [scores(j, qq, shifted(cols, tq), n_keys) for cols, n_keys in pieces(last)]
            for cols, n_keys in pieces(last):
                alpha_a, p_a = softmax(m, cols, sa[0:n_keys, cols], cols.start if last else None)
                accumulate(acc, cols, alpha_a, vT_ref[j, :, 0:n_keys], p_a)
            if not last:
                prefetch_map_a(sa, j + 1, qq, j + 1 == qi)
            elif qi + 1 < nq:
                stacked[qi + 1] = stacked_queries(qi + 1)
                m_ref[1 - par] = jnp.full(m_ref.shape[1:], -jnp.inf, F32)
                acc_ref[1 - par] = jnp.zeros(acc_ref.shape[1:], F32)
                prefetch_map_a(sa_ref.at[1 - par], 0, stacked[qi + 1], False)
            for (cols, n_keys), s in zip(pieces(last), s_b):
                alpha_b, p_b = softmax(m, shifted(cols, tq), s, cols.start if last else None)
                if last:
                    accumulate(acc, shifted(cols, tq), alpha_b, vT_ref[j, :, 0:n_keys], p_b)
                else:
                    ab_ref[...] = alpha_b
                    pb_ref[...] = p_b

        o = acc[0:V_DIM, :] / acc[V_DIM:V_DIM + 1, :]
        d = o[:, map_a] - lam * o[:, map_b]
        ms = jnp.mean(d * d, axis=0, keepdims=True)
        o_ref[qi * tq:(qi + 1) * tq, :] = (
            d * lax.rsqrt(ms + EPS) * sg_ref[...] * (1.0 - lam_init)).T.astype(BF16)
        del stacked[qi]


def _attention(lam_params, subln_g, qT, k, vT, *, batch, seq, lam_init):
    dq, t = qT.shape
    n_heads = dq // V_DIM
    tq, tk = TQ, TK
    assert tq == tk and vT.shape[2] == tk
    nq = seq // tq
    nk = seq // tk
    vrows = V_DIM + ONES_ROWS
    const = lambda b, h: (0, 0)
    return pl.pallas_call(
        functools.partial(_attn_kernel, tq=tq, tk=tk, nq=nq, lam_init=lam_init),
        grid=(batch, n_heads),
        in_specs=[
            pl.BlockSpec(lam_params.shape, const),
            pl.BlockSpec((V_DIM, 1), const),
            pl.BlockSpec((V_DIM, seq), lambda b, h: (h, b)),
            pl.BlockSpec((seq, V_DIM), lambda b, h: (b, h)),
            pl.BlockSpec((nk, vrows, tk), lambda b, h: (b, h, 0)),
        ],
        out_specs=pl.BlockSpec((seq, V_DIM), lambda b, h: (b, h)),
        out_shape=jax.ShapeDtypeStruct((t, dq), BF16),
        scratch_shapes=[pltpu.VMEM((2, 1, 2 * tq), F32),
                        pltpu.VMEM((2, vrows, 2 * tq), F32),
                        pltpu.VMEM((2, tk, tq), F32), pltpu.VMEM((tk, tq), BF16), pltpu.VMEM((1, tq), F32)],
        compiler_params=pltpu.CompilerParams(
            dimension_semantics=("arbitrary", "arbitrary"), vmem_limit_bytes=VMEM_LIMIT),
        name="diff_attention",
    )(lam_params, subln_g, qT, k, vT)


def _outproj_router_kernel(x_ref, yc_ref, at_ref, wo_ref, g_ref, wr_ref,
                           h_ref, hn_ref, route_ref, routeT_ref, cnt_ref, *, tm, dc):
    i = pl.program_id(0)

    @pl.when(i == 0)
    def _():
        cnt_ref[...] = jnp.zeros_like(cnt_ref)

    for r0 in range(0, x_ref.shape[0], tm):
        _outproj_router_subtile(slice(r0, r0 + tm), x_ref, yc_ref, at_ref, wo_ref, g_ref, wr_ref,
                                h_ref, hn_ref, route_ref, routeT_ref, cnt_ref, tm=tm, dc=dc)


def _outproj_router_subtile(rows, x_ref, yc_ref, at_ref, wo_ref, g_ref, wr_ref,
                            h_ref, hn_ref, route_ref, routeT_ref, cnt_ref, *, tm, dc):
    h = x_ref[rows, :] + _dot(yc_ref[rows, :], wo_ref[0:dc, :]) + _dot(at_ref[rows, :], wo_ref[dc:2 * dc, :])
    h_ref[rows, :] = h
    ms = jnp.mean(h * h, axis=-1, keepdims=True)
    hn = h * lax.rsqrt(ms + EPS) * g_ref[...]
    hi = hn.astype(BF16)
    hn_ref[rows, :] = _pack_rows(hn)
    lo = (hn - hi.astype(F32)).astype(BF16)
    prod = _dot(hi, wr_ref[...])
    logits = prod[:, 0:LANES] + prod[:, LANES:2 * LANES] + _dot(lo, wr_ref[:, 0:LANES])

    lt = logits.T
    neg = -jnp.inf
    grow = lax.broadcasted_iota(jnp.int32, (8, tm), 0).astype(F32)
    gl = jnp.where(grow < N_GROUPS, lt[N_EXPERTS:N_EXPERTS + 8, :], neg)
    gmax = jnp.max(gl, axis=0, keepdims=True)
    g_gate = 1.0 / jnp.sum(jnp.exp(gl - gmax), axis=0, keepdims=True)
    g_idx = jnp.min(jnp.where(gl == gmax, grow, 1e9), axis=0, keepdims=True)
    erow = lax.broadcasted_iota(jnp.int32, (N_EXPERTS, tm), 0).astype(F32)
    e_lo = g_idx * EXPERTS_PER_GROUP
    el = jnp.where((erow >= e_lo) & (erow < e_lo + EXPERTS_PER_GROUP), lt[0:N_EXPERTS, :], neg)
    v1 = jnp.max(el, axis=0, keepdims=True)
    i1 = jnp.min(jnp.where(el == v1, erow, 1e9), axis=0, keepdims=True)
    el2 = jnp.where(erow == i1, neg, el)
    v2 = jnp.max(el2, axis=0, keepdims=True)
    i2 = jnp.min(jnp.where(el2 == v2, erow, 1e9), axis=0, keepdims=True)
    tt = jnp.exp(v2 - v1)
    w1 = g_gate / (1.0 + tt)
    w2 = g_gate * tt / (1.0 + tt)

    sel1 = erow == i1
    sel2 = erow == i2
    oh = jnp.where(sel1 | sel2, 1.0, 0.0)
    ss = lax.broadcasted_iota(jnp.int32, (tm, tm), 0)
    tt_i = lax.broadcasted_iota(jnp.int32, (tm, tm), 1)
    earlier = jnp.where(ss < tt_i, 1.0, 0.0).astype(BF16)
    ranks = _dot(oh.astype(BF16), earlier) + cnt_ref[...]
    r1 = jnp.sum(jnp.where(sel1, ranks, 0.0), axis=0, keepdims=True)
    r2 = jnp.sum(jnp.where(sel2, ranks, 0.0), axis=0, keepdims=True)
    cnt_ref[...] = cnt_ref[...] + jnp.sum(oh, axis=1, keepdims=True)

    routeT = jnp.concatenate([i1, i2, r1, r2, w1, w2, jnp.zeros((2, tm), F32)], axis=0)
    routeT_ref[:, rows] = routeT
    route_ref[rows, :] = jnp.concatenate([routeT, jnp.zeros((LANES - 8, tm), F32)], axis=0).T


def _outproj_router(x2, yc, at, w_out, g, wr_cat):
    t, d = x2.shape
    dc = yc.shape[1]
    tm = TM_PROJ
    tb = tm * OUTPROJ_SUBTILES
    row = lambda i: (i, 0)
    const = lambda i: (0, 0)
    return pl.pallas_call(
        functools.partial(_outproj_router_kernel, tm=tm, dc=dc),
        grid=(t // tb,),
        in_specs=[
            pl.BlockSpec((tb, d), row),
            pl.BlockSpec((tb, dc), row),
            pl.BlockSpec((tb, dc), row),
            pl.BlockSpec(w_out.shape, const),
            pl.BlockSpec((1, d), const),
            pl.BlockSpec(wr_cat.shape, const),
        ],
        out_specs=[
            pl.BlockSpec((tb, d), row),
            pl.BlockSpec((tb, d // 2), row),
            pl.BlockSpec((tb, LANES), row),
            pl.BlockSpec((8, tb), lambda i: (0, i)),
            pl.BlockSpec((N_EXPERTS, 1), const),
        ],
        out_shape=[
            jax.ShapeDtypeStruct((t, d), F32),
            jax.ShapeDtypeStruct((t, d // 2), jnp.int32),
            jax.ShapeDtypeStruct((t, LANES), F32),
            jax.ShapeDtypeStruct((8, t), F32),
            jax.ShapeDtypeStruct((N_EXPERTS, 1), F32),
        ],
        compiler_params=pltpu.CompilerParams(
            dimension_semantics=("arbitrary",), vmem_limit_bytes=VMEM_LIMIT),
        name="outproj_router",
    )(x2, yc, at, w_out, g, wr_cat)


def _positions_kernel(offs_ref, rt_ref, pos_ref):
    rt = rt_ref[...]
    ea, eb = rt[0:1, :], rt[1:2, :]
    sa = jnp.zeros_like(ea)
    sb = jnp.zeros_like(eb)
    for e in range(N_EXPERTS):
        start = offs_ref[e].astype(F32)
        sa = jnp.where(ea == e, start, sa)
        sb = jnp.where(eb == e, start, sb)
    pos_ref[0:1, :] = (sa + rt[2:3, :]).astype(jnp.int32)
    pos_ref[1:2, :] = (sb + rt[3:4, :]).astype(jnp.int32)


def _positions(offs, routeT):
    t = routeT.shape[1]
    return pl.pallas_call(
        _positions_kernel,
        grid_spec=pltpu.PrefetchScalarGridSpec(
            num_scalar_prefetch=1, grid=(1,),
            in_specs=[pl.BlockSpec(routeT.shape, lambda i, offs: (0, 0))],
            out_specs=pl.BlockSpec((2, t), lambda i, offs: (0, 0)),
        ),
        out_shape=jax.ShapeDtypeStruct((2, t), jnp.int32),
        name="positions",
    )(offs, routeT)


def _moe_kernel(te_ref, nt_ref, nx_ref, sl_ref, x_ref, wg_hbm, wu_hbm, wd_hbm, y_ref,
                wg_st, wu_st, wd_st, wgb_ref, wub_ref, wdb_ref, sem):
    i = pl.program_id(0)
    last = nt_ref[0] - 1
    ic = jnp.minimum(i, last)
    expert = te_ref[ic]
    slot = sl_ref[ic]
    first_of_expert = (i == 0) | ((i <= last) & (expert != te_ref[jnp.maximum(ic - 1, 0)]))

    def weight_copies(e, s):
        return (pltpu.make_async_copy(wg_hbm.at[e], wg_st.at[s], sem.at[s, 0]),
                pltpu.make_async_copy(wu_hbm.at[e], wu_st.at[s], sem.at[s, 1]),
                pltpu.make_async_copy(wd_hbm.at[e], wd_st.at[s], sem.at[s, 2]))

    @pl.when(i == 0)
    def _():
        for c in weight_copies(expert, slot):
            c.start()

    @pl.when(first_of_expert)
    def _():
        for c in weight_copies(expert, slot):
            c.wait()
        nxt = nx_ref[ic]

        @pl.when(nxt >= 0)
        def _():
            for c in weight_copies(nxt, 1 - slot):
                c.start()

        wgb_ref[...] = wg_st[slot].astype(BF16)
        wub_ref[...] = wu_st[slot].astype(BF16)
        wdb_ref[...] = wd_st[slot].astype(BF16)

    @pl.when(i <= last)
    def _():
        x_l, x_r = _unpack_rows(x_ref[...])
        x_l = x_l.astype(BF16)
        x_r = x_r.astype(BF16)
        half = x_l.shape[1]
        hg = _dot(x_l, wgb_ref[0:half, :]) + _dot(x_r, wgb_ref[half:2 * half, :])
        hu = _dot(x_l, wub_ref[0:half, :]) + _dot(x_r, wub_ref[half:2 * half, :])
        act = hg * (1.0 / (1.0 + jnp.exp(-hg))) * hu
        y_ref[...] = _pack_rows(_dot(act.astype(BF16), wdb_ref[...]))


def _moe(tile_expert, n_tiles, next_expert, stage_slot, xs, wg, wu, wd):
    p, dp = xs.shape
    d, f = wg.shape[1], wg.shape[2]
    assert dp * 2 == d
    tm = TM_MOE
    row = lambda i, te, nt, nx, sl: (jnp.minimum(i, nt[0] - 1), 0)
    hbm = pl.BlockSpec(memory_space=pl.ANY)
    grid_spec = pltpu.PrefetchScalarGridSpec(
        num_scalar_prefetch=4,
        grid=(p // tm,),
        in_specs=[pl.BlockSpec((tm, dp), row), hbm, hbm, hbm],
        out_specs=pl.BlockSpec((tm, dp), row),
        scratch_shapes=[pltpu.VMEM((2, d, f), F32), pltpu.VMEM((2, d, f), F32), pltpu.VMEM((2, f, d), F32),
                        pltpu.VMEM((d, f), BF16), pltpu.VMEM((d, f), BF16), pltpu.VMEM((f, d), BF16),
                        pltpu.SemaphoreType.DMA((2, 3))],
    )
    return pl.pallas_call(
        _moe_kernel,
        grid_spec=grid_spec,
        out_shape=jax.ShapeDtypeStruct((p, dp), jnp.int32),
        compiler_params=pltpu.CompilerParams(
            dimension_semantics=("arbitrary",), vmem_limit_bytes=VMEM_LIMIT),
        name="moe_experts",
    )(tile_expert, n_tiles, next_expert, stage_slot, xs, wg, wu, wd)


def _sc_mesh():
    return plsc.VectorSubcoreMesh(core_axis_name="c", subcore_axis_name="s",
                                  num_cores=SC_CORES, num_subcores=SC_SUBCORES)


def _sc_dispatch(rows, pos_a, pos_b, n_out):
    t, d = rows.shape
    win = pos_a.shape[1]

    @functools.partial(pl.kernel, out_type=jax.ShapeDtypeStruct((n_out, d), rows.dtype),
                       mesh=_sc_mesh(), scratch_types=[], name="sc_dispatch")
    def run(rows_hbm, pa_hbm, pb_hbm, out_hbm):
        def body(rows_vmem, pa_vmem, pb_vmem):
            pltpu.sync_copy(rows_vmem, out_hbm.at[pa_vmem.at[0]])
            pltpu.sync_copy(rows_vmem, out_hbm.at[pb_vmem.at[0]])

        pltpu.emit_pipeline(
            body, grid=(t // win,),
            in_specs=[pl.BlockSpec((win, d), lambda i: (i, 0)),
                      pl.BlockSpec((1, win), lambda i: (i, 0)),
                      pl.BlockSpec((1, win), lambda i: (i, 0))],
            out_specs=[],
            core_axis_name=("c", "s"),
            dimension_semantics=(pltpu.PARALLEL,),
        )(rows_hbm, pa_hbm, pb_hbm)

    return run(rows, pos_a, pos_b)


def _sc_gather(table, idx):
    d = table.shape[1]
    n_win, win = idx.shape

    @functools.partial(pl.kernel, out_type=jax.ShapeDtypeStruct((n_win * win, d), table.dtype),
                       mesh=_sc_mesh(), scratch_types=[], name="sc_gather")
    def run(table_hbm, idx_hbm, out_hbm):
        def body(idx_vmem, out_vmem):
            pltpu.sync_copy(table_hbm.at[idx_vmem.at[0]], out_vmem)

        pltpu.emit_pipeline(
            body, grid=(n_win,),
            in_specs=[pl.BlockSpec((1, win), lambda i: (i, 0))],
            out_specs=[pl.BlockSpec((win, d), lambda i: (i, 0))],
            core_axis_name=("c", "s"),
            dimension_semantics=(pltpu.PARALLEL,),
        )(idx_hbm, out_hbm)

    return run(table, idx)


def _combine_kernel(h_ref, ya_ref, yb_ref, r_ref, o_ref):
    r = r_ref[...]
    wa, wb = r[:, 4:5], r[:, 5:6]
    a_l, a_r = _unpack_rows(ya_ref[...])
    b_l, b_r = _unpack_rows(yb_ref[...])
    half = a_l.shape[1]
    o_ref[:, 0:half] = h_ref[:, 0:half] + wa * a_l + wb * b_l
    o_ref[:, half:2 * half] = h_ref[:, half:2 * half] + wa * a_r + wb * b_r


def _combine(hres, yg, route):
    t, d = hres.shape
    tm = TM_COMBINE
    nb = t // tm
    return pl.pallas_call(
        _combine_kernel,
        grid=(nb,),
        in_specs=[pl.BlockSpec((tm, d), lambda i: (i, 0)),
                  pl.BlockSpec((tm, d // 2), lambda i: (i, 0)),
                  pl.BlockSpec((tm, d // 2), lambda i: (i + nb, 0)),
                  pl.BlockSpec((tm, LANES), lambda i: (i, 0))],
        out_specs=pl.BlockSpec((tm, d), lambda i: (i, 0)),
        out_shape=jax.ShapeDtypeStruct((t, d), F32),
        compiler_params=pltpu.CompilerParams(
            dimension_semantics=("arbitrary",), vmem_limit_bytes=VMEM_LIMIT),
        name="combine",
    )(hres, yg, yg, route)


def _lambda_init(layer_idx):
    return 0.8 - 0.6 * math.exp(-0.3 * layer_idx)


def _layer(h, l, attn_norm_g, w_in, conv_w, conv_out_g, q_norm_g, k_norm_g,
           lambda_q1, lambda_k1, lambda_q2, lambda_k2, attn_subln_g, w_out,
           ffn_norm_g, w_router_group, w_router_expert, w_exp_gate, w_exp_up, w_exp_down):
    batch, seq, d = h.shape
    t = batch * seq
    dc = conv_w.shape[-1]
    lam_init = _lambda_init(l)
    x2 = h.reshape(t, d)

    reps = dc // HEAD_DIM
    assert dc // CONV_GROUPS == HEAD_DIM
    qg = (jnp.tile(q_norm_g[l], reps) * (HEAD_DIM ** -0.5 * math.log2(math.e))).reshape(1, dc)
    kg = jnp.tile(k_norm_g[l], reps).reshape(1, dc)
    grp = jnp.arange(MXU_TILE) // HEAD_DIM
    gmat = jnp.where(grp[:, None] == grp[None, :], 1.0 / HEAD_DIM, 0.0).astype(BF16)
    yc, qT, k, vT = _inproj(x2, attn_norm_g[l].reshape(1, d), w_in[l].astype(BF16), conv_w[l],
                          conv_out_g[l].reshape(1, dc), qg, kg, gmat, batch=batch, seq=seq)

    lam_params = jnp.stack([lambda_q1[l], lambda_k1[l], lambda_q2[l], lambda_k2[l]])
    at = _attention(lam_params, attn_subln_g[l].reshape(V_DIM, 1), qT, k, vT,
                    batch=batch, seq=seq, lam_init=lam_init)

    wr = jnp.concatenate([w_router_expert[l], w_router_group[l],
                          jnp.zeros((d, LANES - N_EXPERTS - N_GROUPS), F32)], axis=1)
    wr_hi = wr.astype(BF16)
    wr_lo = (wr - wr_hi.astype(F32)).astype(BF16)
    hres, hn2, route, routeT, cnt = _outproj_router(x2, yc, at, w_out[l].astype(BF16),
                                                    ffn_norm_g[l].reshape(1, d),
                                                    jnp.concatenate([wr_hi, wr_lo], axis=1))

    tmm = TM_MOE
    n_tiles_max = (2 * t) // tmm + N_EXPERTS
    p_rows = n_tiles_max * tmm
    counts = cnt[:, 0].astype(jnp.int32)
    tiles = (counts + tmm - 1) // tmm
    tile_end = jnp.cumsum(tiles)
    offs = (tile_end - tiles) * tmm
    pos = _positions(offs, routeT)
    n_tiles = tile_end[-1:].astype(jnp.int32)
    tile_ids = jnp.arange(n_tiles_max, dtype=jnp.int32)
    tile_expert = jnp.minimum(
        jnp.sum((tile_end[None, :] <= tile_ids[:, None]).astype(jnp.int32), axis=1), N_EXPERTS - 1)
    e_ids = jnp.arange(N_EXPERTS, dtype=jnp.int32)
    nonempty = tiles > 0
    later = jnp.where(nonempty[None, :] & (e_ids[None, :] > e_ids[:, None]), e_ids[None, :], N_EXPERTS)
    next_of = jnp.min(later, axis=1)
    next_of = jnp.where(next_of == N_EXPERTS, -1, next_of).astype(jnp.int32)
    slot_of = ((jnp.cumsum(nonempty.astype(jnp.int32)) - 1) % 2).astype(jnp.int32)
    onehot_te = (tile_expert[:, None] == e_ids[None, :]).astype(jnp.int32)
    next_expert = jnp.sum(onehot_te * next_of[None, :], axis=1)
    stage_slot = jnp.sum(onehot_te * slot_of[None, :], axis=1)

    posw = pos.reshape(2 * t // SC_WIN, SC_WIN)
    pos1w = posw[:t // SC_WIN]
    pos2w = posw[t // SC_WIN:]
    xs = _sc_dispatch(hn2, pos1w, pos2w, p_rows)

    f = w_exp_gate.shape[-1]
    ys = _moe(tile_expert, n_tiles, next_expert, stage_slot, xs,
              w_exp_gate[l].reshape(N_EXPERTS, d, f),
              w_exp_up[l].reshape(N_EXPERTS, d, f),
              w_exp_down[l].reshape(N_EXPERTS, f, d))
    yg = _sc_gather(ys, posw)
    out = _combine(hres, yg, route)
    return out.reshape(batch, seq, d)


def kernel(x, attn_norm_g, w_in, conv_w, conv_out_g, q_norm_g, k_norm_g, lambda_q1, lambda_k1,
           lambda_q2, lambda_k2, attn_subln_g, w_out, ffn_norm_g, w_router_group, w_router_expert,
           w_exp_gate, w_exp_up, w_exp_down):
    h = x
    for l in range(attn_norm_g.shape[0]):
        h = _layer(h, l, attn_norm_g, w_in, conv_w, conv_out_g, q_norm_g, k_norm_g,
                   lambda_q1, lambda_k1, lambda_q2, lambda_k2, attn_subln_g, w_out,
                   ffn_norm_g, w_router_group, w_router_expert, w_exp_gate, w_exp_up, w_exp_down)
    return h
```

```python
import functools
import math

import jax
import jax.numpy as jnp
from jax import lax
from jax.experimental import pallas as pl
from jax.experimental.pallas import tpu as pltpu
from jax.experimental.pallas import tpu_sc as plsc

F32 = jnp.float32
BF16 = jnp.bfloat16

HEAD_DIM = 64
V_DIM = 2 * HEAD_DIM
CONV_GROUPS = 8
N_GROUPS = 4
EXPERTS_PER_GROUP = 8
N_EXPERTS = N_GROUPS * EXPERTS_PER_GROUP
EPS = 1e-6
LANES = 128
MXU_TILE = 256
ONES_ROWS = 16
VMEM_LIMIT = 48 * 1024 * 1024

TM_PROJ = 256
TM_COMBINE = 1024
OUTPROJ_SUBTILES = 4
TQ = 512
TK = 512
TM_MOE = 256
MOE_TILES_PER_STEP = 4
SC_CORES = 2
SC_SUBCORES = 16
SC_WIN = 64


def _dot(a, b):
    return jnp.dot(a, b, preferred_element_type=F32)


def _pack_rows(x):
    w = x.shape[1] // 2
    bits = lax.bitcast_convert_type(x.astype(BF16).astype(F32), jnp.uint32)
    return lax.bitcast_convert_type((bits[:, :w] >> 16) | bits[:, w:], jnp.int32)


def _unpack_rows(packed):
    bits = lax.bitcast_convert_type(packed, jnp.uint32)
    left = lax.bitcast_convert_type(bits << 16, F32)
    right = lax.bitcast_convert_type(bits & jnp.uint32(0xFFFF0000), F32)
    return left, right


def _group_mean(sq, gm):
    w = gm.shape[0]
    sq = sq.astype(BF16)
    return jnp.concatenate([_dot(sq[:, c:c + w], gm) for c in range(0, sq.shape[1], w)], axis=1)


def _inproj_kernel(x_ref, g_ref, w_ref, cw_ref, cg_ref, qg_ref, kg_ref, gm_ref,
                   yc_ref, qT_ref, k_ref, vT_ref, carry_ref, *, tm, dc):
    j = pl.program_id(1)

    @pl.when(j == 0)
    def _():
        carry_ref[...] = jnp.zeros_like(carry_ref)

    x = x_ref[...]
    ms = jnp.mean(x * x, axis=-1, keepdims=True)
    hn = (x * lax.rsqrt(ms + EPS) * g_ref[...]).astype(BF16)

    def proj(s):
        return _dot(hn, w_ref[:, s * dc:(s + 1) * dc])

    gm = gm_ref[...]

    u = proj(2) * proj(0)
    prev = carry_ref[...]
    rows = lax.broadcasted_iota(jnp.int32, u.shape, 0)
    u1 = jnp.where(rows == 0, prev[7:8, :], pltpu.roll(u, 1, 0))
    u2 = jnp.where(rows == 0, prev[6:7, :], jnp.where(rows == 1, prev[7:8, :], pltpu.roll(u, 2, 0)))
    carry_ref[...] = u[tm - 8:tm, :]
    cw = cw_ref[...]
    y = proj(1) * (cw[0:1, :] * u2 + cw[1:2, :] * u1 + cw[2:3, :] * u)
    yc_ref[...] = (y * lax.rsqrt(_group_mean(y * y, gm) + EPS) * cg_ref[...]).astype(BF16)

    q = proj(3)
    qT_ref[...] = (q * lax.rsqrt(_group_mean(q * q, gm) + EPS) * qg_ref[...]).T.astype(BF16)
    k = proj(4)
    k_ref[...] = (k * lax.rsqrt(_group_mean(k * k, gm) + EPS) * kg_ref[...]).astype(BF16)
    vt = proj(5).T.astype(BF16)
    vrows = V_DIM + ONES_ROWS
    for h in range(dc // V_DIM):
        vT_ref[0, h * vrows:h * vrows + V_DIM, :] = vt[h * V_DIM:(h + 1) * V_DIM, :]
        vT_ref[0, h * vrows + V_DIM:(h + 1) * vrows, :] = jnp.ones((ONES_ROWS, tm), BF16)


def _inproj(x2, g, w_in, conv_w, conv_g, qg, kg, gmat, *, batch, seq):
    t, d = x2.shape
    dc = conv_g.shape[1]
    dv = dc // V_DIM * (V_DIM + ONES_ROWS)
    tm = TK
    nj = seq // tm
    row = lambda b, j: (b * nj + j, 0)
    const = lambda b, j: (0, 0)
    out_sds = jax.ShapeDtypeStruct((t, dc), BF16)
    return pl.pallas_call(
        functools.partial(_inproj_kernel, tm=tm, dc=dc),
        grid=(batch, nj),
        in_specs=[
            pl.BlockSpec((tm, d), row),
            pl.BlockSpec((1, d), const),
            pl.BlockSpec(w_in.shape, const),
            pl.BlockSpec(conv_w.shape, const),
            pl.BlockSpec((1, dc), const),
            pl.BlockSpec((1, dc), const),
            pl.BlockSpec((1, dc), const),
            pl.BlockSpec(gmat.shape, const),
        ],
        out_specs=[
            pl.BlockSpec((tm, dc), row),
            pl.BlockSpec((dc, tm), lambda b, j: (0, b * nj + j)),
            pl.BlockSpec((tm, dc), row),
            pl.BlockSpec((1, dv, tm), lambda b, j: (b * nj + j, 0, 0)),
        ],
        out_shape=[out_sds, jax.ShapeDtypeStruct((dc, t), BF16), out_sds,
                   jax.ShapeDtypeStruct((t // tm, dv, tm), BF16)],
        scratch_shapes=[pltpu.VMEM((8, dc), F32)],
        compiler_params=pltpu.CompilerParams(
            dimension_semantics=("arbitrary", "arbitrary"), vmem_limit_bytes=VMEM_LIMIT),
        name="inproj_conv_qknorm",
    )(x2, g, w_in, conv_w, conv_g, qg, kg, gmat)


def _attn_kernel(lp_ref, sg_ref, qT_ref, k_ref, vT_ref, o_ref, m_ref, acc_ref,
                 sa_ref, pb_ref, ab_ref, *, tq, tk, nq, lam_init):
    map_a = slice(0, tq)
    map_b = slice(tq, 2 * tq)

    lp = lp_ref[...]
    lam = (jnp.exp(jnp.sum(lp[0:1, :] * lp[1:2, :], axis=-1, keepdims=True))
           - jnp.exp(jnp.sum(lp[2:3, :] * lp[3:4, :], axis=-1, keepdims=True)) + lam_init)

    def stacked_queries(qi):
        qT = qT_ref[:, qi * tq:(qi + 1) * tq]
        row = lax.broadcasted_iota(jnp.int32, qT.shape, 0)
        zero = jnp.zeros_like(qT)
        return jnp.concatenate([jnp.where(row < HEAD_DIM, qT, zero),
                                jnp.where(row >= HEAD_DIM, qT, zero)], axis=1)

    def pieces(diag):
        return ((slice(0, tq // 2), tk // 2), (slice(tq // 2, tq), tk)) if diag else ((slice(0, tq), tk),)

    def shifted(cols, off):
        return slice(cols.start + off, cols.stop + off)

    def scores(j, qq, cols, n_keys):
        return _dot(k_ref[j * tk:j * tk + n_keys, :], qq[:, cols])

    def softmax(m, cols, s, q0):
        if q0 is not None:
            qrel = q0 + lax.broadcasted_iota(jnp.int32, s.shape, 1)
            krel = lax.broadcasted_iota(jnp.int32, s.shape, 0)
            s = jnp.where(krel <= qrel, s, -jnp.inf)
        sb = s.astype(BF16)
        m_old = m[:, cols]
        m_new = jnp.maximum(m_old, jnp.max(sb, axis=0, keepdims=True).astype(F32))
        alpha = jnp.exp2(m_old - m_new)
        p = jnp.exp2(sb - m_new.astype(BF16))
        m[:, cols] = m_new
        return alpha, p

    def accumulate(acc, cols, alpha, vb, p):
        acc[:, cols] = alpha * acc[:, cols] + _dot(vb, p)

    def prefetch_map_a(sa, j, qq, diag):
        for cols, n_keys in pieces(diag):
            sa[0:n_keys, cols] = scores(j, qq, cols, n_keys)

    stacked = {0: stacked_queries(0)}
    m_ref[0] = jnp.full(m_ref.shape[1:], -jnp.inf, F32)
    acc_ref[0] = jnp.zeros(acc_ref.shape[1:], F32)
    prefetch_map_a(sa_ref.at[0], 0, stacked[0], True)

    for qi in range(nq):
        par = qi % 2
        m, acc, sa, qq = m_ref.at[par], acc_ref.at[par], sa_ref.at[par], stacked[qi]
        for j in range(qi + 1):
            last = j == qi
            if j > 0:
                accumulate(acc, map_b, ab_ref[...], vT_ref[j - 1], pb_ref[...])
            s_b = [scores(j, qq, shifted(cols, tq), n_keys) for cols, n_keys in pieces(last)]
            for cols, n_keys in pieces(last):
                alpha_a, p_a = softmax(m, cols, sa[0:n_keys, cols], cols.start if last else None)
                accumulate(acc, cols, alpha_a, vT_ref[j, :, 0:n_keys], p_a)
            if not last:
                prefetch_map_a(sa, j + 1, qq, j + 1 == qi)
            elif qi + 1 < nq:
                stacked[qi + 1] = stacked_queries(qi + 1)
                m_ref[1 - par] = jnp.full(m_ref.shape[1:], -jnp.inf, F32)
                acc_ref[1 - par] = jnp.zeros(acc_ref.shape[1:], F32)
                prefetch_map_a(sa_ref.at[1 - par], 0, stacked[qi + 1], False)
            for (cols, n_keys), s in zip(pieces(last), s_b):
                alpha_b, p_b = softmax(m, shifted(cols, tq), s, cols.start if last else None)
                if last:
                    accumulate(acc, shifted(cols, tq), alpha_b, vT_ref[j, :, 0:n_keys], p_b)
                else:
                    ab_ref[...] = alpha_b
                    pb_ref[...] = p_b

        o = acc[0:V_DIM, :] / acc[V_DIM:V_DIM + 1, :]
        d = o[:, map_a] - lam * o[:, map_b]
        ms = jnp.mean(d * d, axis=0, keepdims=True)
        o_ref[qi * tq:(qi + 1) * tq, :] = (
            d * lax.rsqrt(ms + EPS) * sg_ref[...] * (1.0 - lam_init)).T.astype(BF16)
        del stacked[qi]


def _attention(lam_params, subln_g, qT, k, vT, *, batch, seq, lam_init):
    dq, t = qT.shape
    n_heads = dq // V_DIM
    tq, tk = TQ, TK
    assert tq == tk and vT.shape[2] == tk
    nq = seq // tq
    nk = seq // tk
    vrows = V_DIM + ONES_ROWS
    const = lambda b, h: (0, 0)
    return pl.pallas_call(
        functools.partial(_attn_kernel, tq=tq, tk=tk, nq=nq, lam_init=lam_init),
        grid=(batch, n_heads),
        in_specs=[
            pl.BlockSpec(lam_params.shape, const),
            pl.BlockSpec((V_DIM, 1), const),
            pl.BlockSpec((V_DIM, seq), lambda b, h: (h, b)),
            pl.BlockSpec((seq, V_DIM), lambda b, h: (b, h)),
            pl.BlockSpec((nk, vrows, tk), lambda b, h: (b, h, 0)),
        ],
        out_specs=pl.BlockSpec((seq, V_DIM), lambda b, h: (b, h)),
        out_shape=jax.ShapeDtypeStruct((t, dq), BF16),
        scratch_shapes=[pltpu.VMEM((2, 1, 2 * tq), F32),
                        pltpu.VMEM((2, vrows, 2 * tq), F32),
                        pltpu.VMEM((2, tk, tq), F32), pltpu.VMEM((tk, tq), BF16), pltpu.VMEM((1, tq), F32)],
        compiler_params=pltpu.CompilerParams(
            dimension_semantics=("arbitrary", "arbitrary"), vmem_limit_bytes=VMEM_LIMIT),
        name="diff_attention",
    )(lam_params, subln_g, qT, k, vT)


def _outproj_router_kernel(x_ref, yc_ref, at_ref, wo_ref, g_ref, wr_ref,
                           h_ref, hn_ref, route_ref, routeT_ref, cnt_ref, *, tm, dc):
    i = pl.program_id(0)

    @pl.when(i == 0)
    def _():
        cnt_ref[...] = jnp.zeros_like(cnt_ref)

    for r0 in range(0, x_ref.shape[0], tm):
        _outproj_router_subtile(slice(r0, r0 + tm), x_ref, yc_ref, at_ref, wo_ref, g_ref, wr_ref,
                                h_ref, hn_ref, route_ref, routeT_ref, cnt_ref, tm=tm, dc=dc)


def _outproj_router_subtile(rows, x_ref, yc_ref, at_ref, wo_ref, g_ref, wr_ref,
                            h_ref, hn_ref, route_ref, routeT_ref, cnt_ref, *, tm, dc):
    h = x_ref[rows, :] + _dot(yc_ref[rows, :], wo_ref[0:dc, :]) + _dot(at_ref[rows, :], wo_ref[dc:2 * dc, :])
    h_ref[rows, :] = h
    ms = jnp.mean(h * h, axis=-1, keepdims=True)
    hn = h * lax.rsqrt(ms + EPS) * g_ref[...]
    hi = hn.astype(BF16)
    hn_ref[rows, :] = _pack_rows(hn)
    lo = (hn - hi.astype(F32)).astype(BF16)
    prod = _dot(hi, wr_ref[...])
    logits = prod[:, 0:LANES] + prod[:, LANES:2 * LANES] + _dot(lo, wr_ref[:, 0:LANES])

    lt = logits.T
    neg = -jnp.inf
    grow = lax.broadcasted_iota(jnp.int32, (8, tm), 0).astype(F32)
    gl = jnp.where(grow < N_GROUPS, lt[N_EXPERTS:N_EXPERTS + 8, :], neg)
    gmax = jnp.max(gl, axis=0, keepdims=True)
    g_gate = 1.0 / jnp.sum(jnp.exp(gl - gmax), axis=0, keepdims=True)
    g_idx = jnp.min(jnp.where(gl == gmax, grow, 1e9), axis=0, keepdims=True)
    erow = lax.broadcasted_iota(jnp.int32, (N_EXPERTS, tm), 0).astype(F32)
    e_lo = g_idx * EXPERTS_PER_GROUP
    el = jnp.where((erow >= e_lo) & (erow < e_lo + EXPERTS_PER_GROUP), lt[0:N_EXPERTS, :], neg)
    v1 = jnp.max(el, axis=0, keepdims=True)
    i1 = jnp.min(jnp.where(el == v1, erow, 1e9), axis=0, keepdims=True)
    el2 = jnp.where(erow == i1, neg, el)
    v2 = jnp.max(el2, axis=0, keepdims=True)
    i2 = jnp.min(jnp.where(el2 == v2, erow, 1e9), axis=0, keepdims=True)
    tt = jnp.exp(v2 - v1)
    w1 = g_gate / (1.0 + tt)
    w2 = g_gate * tt / (1.0 + tt)

    sel1 = erow == i1
    sel2 = erow == i2
    oh = jnp.where(sel1 | sel2, 1.0, 0.0)
    ss = lax.broadcasted_iota(jnp.int32, (tm, tm), 0)
    tt_i = lax.broadcasted_iota(jnp.int32, (tm, tm), 1)
    earlier = jnp.where(ss < tt_i, 1.0, 0.0).astype(BF16)
    ranks = _dot(oh.astype(BF16), earlier) + cnt_ref[...]
    r1 = jnp.sum(jnp.where(sel1, ranks, 0.0), axis=0, keepdims=True)
    r2 = jnp.sum(jnp.where(sel2, ranks, 0.0), axis=0, keepdims=True)
    cnt_ref[...] = cnt_ref[...] + jnp.sum(oh, axis=1, keepdims=True)

    routeT = jnp.concatenate([i1, i2, r1, r2, w1, w2, jnp.zeros((2, tm), F32)], axis=0)
    routeT_ref[:, rows] = routeT
    route_ref[rows, :] = jnp.concatenate([routeT, jnp.zeros((LANES - 8, tm), F32)], axis=0).T


def _outproj_router(x2, yc, at, w_out, g, wr_cat):
    t, d = x2.shape
    dc = yc.shape[1]
    tm = TM_PROJ
    tb = tm * OUTPROJ_SUBTILES
    row = lambda i: (i, 0)
    const = lambda i: (0, 0)
    return pl.pallas_call(
        functools.partial(_outproj_router_kernel, tm=tm, dc=dc),
        grid=(t // tb,),
        in_specs=[
            pl.BlockSpec((tb, d), row),
            pl.BlockSpec((tb, dc), row),
            pl.BlockSpec((tb, dc), row),
            pl.BlockSpec(w_out.shape, const),
            pl.BlockSpec((1, d), const),
            pl.BlockSpec(wr_cat.shape, const),
        ],
        out_specs=[
            pl.BlockSpec((tb, d), row),
            pl.BlockSpec((tb, d // 2), row),
            pl.BlockSpec((tb, LANES), row),
            pl.BlockSpec((8, tb), lambda i: (0, i)),
            pl.BlockSpec((N_EXPERTS, 1), const),
        ],
        out_shape=[
            jax.ShapeDtypeStruct((t, d), F32),
            jax.ShapeDtypeStruct((t, d // 2), jnp.int32),
            jax.ShapeDtypeStruct((t, LANES), F32),
            jax.ShapeDtypeStruct((8, t), F32),
            jax.ShapeDtypeStruct((N_EXPERTS, 1), F32),
        ],
        compiler_params=pltpu.CompilerParams(
            dimension_semantics=("arbitrary",), vmem_limit_bytes=VMEM_LIMIT),
        name="outproj_router",
    )(x2, yc, at, w_out, g, wr_cat)


def _positions_kernel(offs_ref, rt_ref, pos_ref):
    rt = rt_ref[...]
    ea, eb = rt[0:1, :], rt[1:2, :]
    sa = jnp.zeros_like(ea)
    sb = jnp.zeros_like(eb)
    for e in range(N_EXPERTS):
        start = offs_ref[e].astype(F32)
        sa = jnp.where(ea == e, start, sa)
        sb = jnp.where(eb == e, start, sb)
    pos_ref[0:1, :] = (sa + rt[2:3, :]).astype(jnp.int32)
    pos_ref[1:2, :] = (sb + rt[3:4, :]).astype(jnp.int32)


def _positions(offs, routeT):
    t = routeT.shape[1]
    return pl.pallas_call(
        _positions_kernel,
        grid_spec=pltpu.PrefetchScalarGridSpec(
            num_scalar_prefetch=1, grid=(1,),
            in_specs=[pl.BlockSpec(routeT.shape, lambda i, offs: (0, 0))],
            out_specs=pl.BlockSpec((2, t), lambda i, offs: (0, 0)),
        ),
        out_shape=jax.ShapeDtypeStruct((2, t), jnp.int32),
        name="positions",
    )(offs, routeT)


def _moe_kernel(te_ref, nt_ref, nx_ref, sl_ref, x_ref, wg_hbm, wu_hbm, wd_hbm, y_ref,
                wg_st, wu_st, wd_st, wgb_ref, wub_ref, wdb_ref, sem, *, tm, tiles_per_step):
    last = nt_ref[0] - 1

    def weight_copies(e, s):
        return (pltpu.make_async_copy(wg_hbm.at[e], wg_st.at[s], sem.at[s, 0]),
                pltpu.make_async_copy(wu_hbm.at[e], wu_st.at[s], sem.at[s, 1]),
                pltpu.make_async_copy(wd_hbm.at[e], wd_st.at[s], sem.at[s, 2]))

    for u in range(tiles_per_step):
        i = pl.program_id(0) * tiles_per_step + u
        rows = slice(u * tm, (u + 1) * tm)
        ic = jnp.minimum(i, last)
        expert = te_ref[ic]
        slot = sl_ref[ic]
        first_of_expert = (i == 0) | ((i <= last) & (expert != te_ref[jnp.maximum(ic - 1, 0)]))

        if u == 0:
            @pl.when(i == 0)
            def _():
                for c in weight_copies(expert, slot):
                    c.start()

        @pl.when(first_of_expert)
        def _():
            for c in weight_copies(expert, slot):
                c.wait()
            nxt = nx_ref[ic]

            @pl.when(nxt >= 0)
            def _():
                for c in weight_copies(nxt, 1 - slot):
                    c.start()

            wgb_ref[...] = wg_st[slot].astype(BF16)
            wub_ref[...] = wu_st[slot].astype(BF16)
            wdb_ref[...] = wd_st[slot].astype(BF16)

        @pl.when(i <= last)
        def _():
            x_l, x_r = _unpack_rows(x_ref[rows, :])
            x_l = x_l.astype(BF16)
            x_r = x_r.astype(BF16)
            half = x_l.shape[1]
            hg = _dot(x_l, wgb_ref[0:half, :]) + _dot(x_r, wgb_ref[half:2 * half, :])
            hu = _dot(x_l, wub_ref[0:half, :]) + _dot(x_r, wub_ref[half:2 * half, :])
            act = hg * (1.0 / (1.0 + jnp.exp(-hg))) * hu
            y_ref[rows, :] = _pack_rows(_dot(act.astype(BF16), wdb_ref[...]))


def _moe(tile_expert, n_tiles, next_expert, stage_slot, xs, wg, wu, wd):
    p, dp = xs.shape
    d, f = wg.shape[1], wg.shape[2]
    assert dp * 2 == d
    tm, tps = TM_MOE, MOE_TILES_PER_STEP
    tb = tm * tps
    row = lambda s, te, nt, nx, sl: (jnp.minimum(s, (nt[0] - 1) // tps), 0)
    hbm = pl.BlockSpec(memory_space=pl.ANY)
    grid_spec = pltpu.PrefetchScalarGridSpec(
        num_scalar_prefetch=4,
        grid=(p // tb,),
        in_specs=[pl.BlockSpec((tb, dp), row), hbm, hbm, hbm],
        out_specs=pl.BlockSpec((tb, dp), row),
        scratch_shapes=[pltpu.VMEM((2, d, f), F32), pltpu.VMEM((2, d, f), F32), pltpu.VMEM((2, f, d), F32),
                        pltpu.VMEM((d, f), BF16), pltpu.VMEM((d, f), BF16), pltpu.VMEM((f, d), BF16),
                        pltpu.SemaphoreType.DMA((2, 3))],
    )
    return pl.pallas_call(
        functools.partial(_moe_kernel, tm=tm, tiles_per_step=tps),
        grid_spec=grid_spec,
        out_shape=jax.ShapeDtypeStruct((p, dp), jnp.int32),
        compiler_params=pltpu.CompilerParams(
            dimension_semantics=("arbitrary",), vmem_limit_bytes=VMEM_LIMIT),
        name="moe_experts",
    )(tile_expert, n_tiles, next_expert, stage_slot, xs, wg, wu, wd)


def _sc_mesh():
    return plsc.VectorSubcoreMesh(core_axis_name="c", subcore_axis_name="s",
                                  num_cores=SC_CORES, num_subcores=SC_SUBCORES)


def _sc_dispatch(rows, pos_a, pos_b, n_out):
    t, d = rows.shape
    win = pos_a.shape[1]

    @functools.partial(pl.kernel, out_type=jax.ShapeDtypeStruct((n_out, d), rows.dtype),
                       mesh=_sc_mesh(), scratch_types=[], name="sc_dispatch")
    def run(rows_hbm, pa_hbm, pb_hbm, out_hbm):
        def body(rows_vmem, pa_vmem, pb_vmem):
            pltpu.sync_copy(rows_vmem, out_hbm.at[pa_vmem.at[0]])
            pltpu.sync_copy(rows_vmem, out_hbm.at[pb_vmem.at[0]])

        pltpu.emit_pipeline(
            body, grid=(t // win,),
            in_specs=[pl.BlockSpec((win, d), lambda i: (i, 0)),
                      pl.BlockSpec((1, win), lambda i: (i, 0)),
                      pl.BlockSpec((1, win), lambda i: (i, 0))],
            out_specs=[],
            core_axis_name=("c", "s"),
            dimension_semantics=(pltpu.PARALLEL,),
        )(rows_hbm, pa_hbm, pb_hbm)

    return run(rows, pos_a, pos_b)


def _sc_gather(table, idx):
    d = table.shape[1]
    n_win, win = idx.shape

    @functools.partial(pl.kernel, out_type=jax.ShapeDtypeStruct((n_win * win, d), table.dtype),
                       mesh=_sc_mesh(), scratch_types=[], name="sc_gather")
    def run(table_hbm, idx_hbm, out_hbm):
        def body(idx_vmem, out_vmem):
            pltpu.sync_copy(table_hbm.at[idx_vmem.at[0]], out_vmem)

        pltpu.emit_pipeline(
            body, grid=(n_win,),
            in_specs=[pl.BlockSpec((1, win), lambda i: (i, 0))],
            out_specs=[pl.BlockSpec((win, d), lambda i: (i, 0))],
            core_axis_name=("c", "s"),
            dimension_semantics=(pltpu.PARALLEL,),
        )(idx_hbm, out_hbm)

    return run(table, idx)


def _combine_kernel(h_ref, ya_ref, yb_ref, r_ref, o_ref):
    r = r_ref[...]
    wa, wb = r[:, 4:5], r[:, 5:6]
    a_l, a_r = _unpack_rows(ya_ref[...])
    b_l, b_r = _unpack_rows(yb_ref[...])
    half = a_l.shape[1]
    o_ref[:, 0:half] = h_ref[:, 0:half] + wa * a_l + wb * b_l
    o_ref[:, half:2 * half] = h_ref[:, half:2 * half] + wa * a_r + wb * b_r


def _combine(hres, yg, route):
    t, d = hres.shape
    tm = TM_COMBINE
    nb = t // tm
    return pl.pallas_call(
        _combine_kernel,
        grid=(nb,),
        in_specs=[pl.BlockSpec((tm, d), lambda i: (i, 0)),
                  pl.BlockSpec((tm, d // 2), lambda i: (i, 0)),
                  pl.BlockSpec((tm, d // 2), lambda i: (i + nb, 0)),
                  pl.BlockSpec((tm, LANES), lambda i: (i, 0))],
        out_specs=pl.BlockSpec((tm, d), lambda i: (i, 0)),
        out_shape=jax.ShapeDtypeStruct((t, d), F32),
        compiler_params=pltpu.CompilerParams(
            dimension_semantics=("arbitrary",), vmem_limit_bytes=VMEM_LIMIT),
        name="combine",
    )(hres, yg, yg, route)


def _lambda_init(layer_idx):
    return 0.8 - 0.6 * math.exp(-0.3 * layer_idx)


def _layer(h, l, attn_norm_g, w_in, conv_w, conv_out_g, q_norm_g, k_norm_g,
           lambda_q1, lambda_k1, lambda_q2, lambda_k2, attn_subln_g, w_out,
           ffn_norm_g, w_router_group, w_router_expert, w_exp_gate, w_exp_up, w_exp_down):
    batch, seq, d = h.shape
    t = batch * seq
    dc = conv_w.shape[-1]
    lam_init = _lambda_init(l)
    x2 = h.reshape(t, d)

    reps = dc // HEAD_DIM
    assert dc // CONV_GROUPS == HEAD_DIM
    qg = (jnp.tile(q_norm_g[l], reps) * (HEAD_DIM ** -0.5 * math.log2(math.e))).reshape(1, dc)
    kg = jnp.tile(k_norm_g[l], reps).reshape(1, dc)
    grp = jnp.arange(MXU_TILE) // HEAD_DIM
    gmat = jnp.where(grp[:, None] == grp[None, :], 1.0 / HEAD_DIM, 0.0).astype(BF16)
    yc, qT, k, vT = _inproj(x2, attn_norm_g[l].reshape(1, d), w_in[l].astype(BF16), conv_w[l],
                          conv_out_g[l].reshape(1, dc), qg, kg, gmat, batch=batch, seq=seq)

    lam_params = jnp.stack([lambda_q1[l], lambda_k1[l], lambda_q2[l], lambda_k2[l]])
    at = _attention(lam_params, attn_subln_g[l].reshape(V_DIM, 1), qT, k, vT,
                    batch=batch, seq=seq, lam_init=lam_init)

    wr = jnp.concatenate([w_router_expert[l], w_router_group[l],
                          jnp.zeros((d, LANES - N_EXPERTS - N_GROUPS), F32)], axis=1)
    wr_hi = wr.astype(BF16)
    wr_lo = (wr - wr_hi.astype(F32)).astype(BF16)
    hres, hn2, route, routeT, cnt = _outproj_router(x2, yc, at, w_out[l].astype(BF16),
                                                    ffn_norm_g[l].reshape(1, d),
                                                    jnp.concatenate([wr_hi, wr_lo], axis=1))

    tmm = TM_MOE
    n_tiles_max = (2 * t) // tmm + N_EXPERTS
    p_rows = n_tiles_max * tmm
    counts = cnt[:, 0].astype(jnp.int32)
    tiles = (counts + tmm - 1) // tmm
    tile_end = jnp.cumsum(tiles)
    offs = (tile_end - tiles) * tmm
    pos = _positions(offs, routeT)
    n_tiles = tile_end[-1:].astype(jnp.int32)
    tile_ids = jnp.arange(n_tiles_max, dtype=jnp.int32)
    tile_expert = jnp.minimum(
        jnp.sum((tile_end[None, :] <= tile_ids[:, None]).astype(jnp.int32), axis=1), N_EXPERTS - 1)
    e_ids = jnp.arange(N_EXPERTS, dtype=jnp.int32)
    nonempty = tiles > 0
    later = jnp.where(nonempty[None, :] & (e_ids[None, :] > e_ids[:, None]), e_ids[None, :], N_EXPERTS)
    next_of = jnp.min(later, axis=1)
    next_of = jnp.where(next_of == N_EXPERTS, -1, next_of).astype(jnp.int32)
    slot_of = ((jnp.cumsum(nonempty.astype(jnp.int32)) - 1) % 2).astype(jnp.int32)
    onehot_te = (tile_expert[:, None] == e_ids[None, :]).astype(jnp.int32)
    next_expert = jnp.sum(onehot_te * next_of[None, :], axis=1)
    stage_slot = jnp.sum(onehot_te * slot_of[None, :], axis=1)

    posw = pos.reshape(2 * t // SC_WIN, SC_WIN)
    pos1w = posw[:t // SC_WIN]
    pos2w = posw[t // SC_WIN:]
    xs = _sc_dispatch(hn2, pos1w, pos2w, p_rows)

    f = w_exp_gate.shape[-1]
    ys = _moe(tile_expert, n_tiles, next_expert, stage_slot, xs,
              w_exp_gate[l].reshape(N_EXPERTS, d, f),
              w_exp_up[l].reshape(N_EXPERTS, d, f),
              w_exp_down[l].reshape(N_EXPERTS, f, d))
    yg = _sc_gather(ys, posw)
    out = _combine(hres, yg, route)
    return out.reshape(batch, seq, d)


def kernel(x, attn_norm_g, w_in, conv_w, conv_out_g, q_norm_g, k_norm_g, lambda_q1, lambda_k1,
           lambda_q2, lambda_k2, attn_subln_g, w_out, ffn_norm_g, w_router_group, w_router_expert,
           w_exp_gate, w_exp_up, w_exp_down):
    h = x
    for l in range(attn_norm_g.shape[0]):
        h = _layer(h, l, attn_norm_g, w_in, conv_w, conv_out_g, q_norm_g, k_norm_g,
                   lambda_q1, lambda_k1, lambda_q2, lambda_k2, attn_subln_g, w_out,
                   ffn_norm_g, w_router_group, w_router_expert, w_exp_gate, w_exp_up, w_exp_down)
    return h
```

```python
import functools
import math

import jax
import jax.numpy as jnp
from jax import lax
from jax.experimental import pallas as pl
from jax.experimental.pallas import tpu as pltpu
from jax.experimental.pallas import tpu_sc as plsc

F32 = jnp.float32
BF16 = jnp.bfloat16

HEAD_DIM = 64
V_DIM = 2 * HEAD_DIM
CONV_GROUPS = 8
N_GROUPS = 4
EXPERTS_PER_GROUP = 8
N_EXPERTS = N_GROUPS * EXPERTS_PER_GROUP
EPS = 1e-6
LANES = 128
MXU_TILE = 256
ONES_ROWS = 16
VMEM_LIMIT = 48 * 1024 * 1024

TM_PROJ = 256
TM_COMBINE = 1024
OUTPROJ_SUBTILES = 4
TQ = 512
TK = 512
TM_MOE = 512
MOE_TILES_PER_STEP = 2
SC_CORES = 2
SC_SUBCORES = 16
SC_WIN = 64


def _dot(a, b):
    return jnp.dot(a, b, preferred_element_type=F32)


def _pack_rows(x):
    w = x.shape[1] // 2
    bits = lax.bitcast_convert_type(x.astype(BF16).astype(F32), jnp.uint32)
    return lax.bitcast_convert_type((bits[:, :w] >> 16) | bits[:, w:], jnp.int32)


def _unpack_rows(packed):
    bits = lax.bitcast_convert_type(packed, jnp.uint32)
    left = lax.bitcast_convert_type(bits << 16, F32)
    right = lax.bitcast_convert_type(bits & jnp.uint32(0xFFFF0000), F32)
    return left, right


def _group_mean(sq, gm):
    w = gm.shape[0]
    sq = sq.astype(BF16)
    return jnp.concatenate([_dot(sq[:, c:c + w], gm) for c in range(0, sq.shape[1], w)], axis=1)


def _inproj_kernel(x_ref, g_ref, w_ref, cw_ref, cg_ref, qg_ref, kg_ref, gm_ref,
                   yc_ref, qT_ref, k_ref, vT_ref, carry_ref, *, tm, dc):
    j = pl.program_id(1)

    @pl.when(j == 0)
    def _():
        carry_ref[...] = jnp.zeros_like(carry_ref)

    x = x_ref[...]
    ms = jnp.mean(x * x, axis=-1, keepdims=True)
    hn = (x * lax.rsqrt(ms + EPS) * g_ref[...]).astype(BF16)

    def proj(s):
        return _dot(hn, w_ref[:, s * dc:(s + 1) * dc])

    gm = gm_ref[...]

    u = proj(2) * proj(0)
    prev = carry_ref[...]
    rows = lax.broadcasted_iota(jnp.int32, u.shape, 0)
    u1 = jnp.where(rows == 0, prev[7:8, :], pltpu.roll(u, 1, 0))
    u2 = jnp.where(rows == 0, prev[6:7, :], jnp.where(rows == 1, prev[7:8, :], pltpu.roll(u, 2, 0)))
    carry_ref[...] = u[tm - 8:tm, :]
    cw = cw_ref[...]
    y = proj(1) * (cw[0:1, :] * u2 + cw[1:2, :] * u1 + cw[2:3, :] * u)
    yc_ref[...] = (y * lax.rsqrt(_group_mean(y * y, gm) + EPS) * cg_ref[...]).astype(BF16)

    q = proj(3)
    qT_ref[...] = (q * lax.rsqrt(_group_mean(q * q, gm) + EPS) * qg_ref[...]).T.astype(BF16)
    k = proj(4)
    k_ref[...] = (k * lax.rsqrt(_group_mean(k * k, gm) + EPS) * kg_ref[...]).astype(BF16)
    vt = proj(5).T.astype(BF16)
    vrows = V_DIM + ONES_ROWS
    for h in range(dc // V_DIM):
        vT_ref[0, h * vrows:h * vrows + V_DIM, :] = vt[h * V_DIM:(h + 1) * V_DIM, :]
        vT_ref[0, h * vrows + V_DIM:(h + 1) * vrows, :] = jnp.ones((ONES_ROWS, tm), BF16)


def _inproj(x2, g, w_in, conv_w, conv_g, qg, kg, gmat, *, batch, seq):
    t, d = x2.shape
    dc = conv_g.shape[1]
    dv = dc // V_DIM * (V_DIM + ONES_ROWS)
    tm = TK
    nj = seq // tm
    row = lambda b, j: (b * nj + j, 0)
    const = lambda b, j: (0, 0)
    out_sds = jax.ShapeDtypeStruct((t, dc), BF16)
    return pl.pallas_call(
        functools.partial(_inproj_kernel, tm=tm, dc=dc),
        grid=(batch, nj),
        in_specs=[
            pl.BlockSpec((tm, d), row),
            pl.BlockSpec((1, d), const),
            pl.BlockSpec(w_in.shape, const),
            pl.BlockSpec(conv_w.shape, const),
            pl.BlockSpec((1, dc), const),
            pl.BlockSpec((1, dc), const),
            pl.BlockSpec((1, dc), const),
            pl.BlockSpec(gmat.shape, const),
        ],
        out_specs=[
            pl.BlockSpec((tm, dc), row),
            pl.BlockSpec((dc, tm), lambda b, j: (0, b * nj + j)),
            pl.BlockSpec((tm, dc), row),
            pl.BlockSpec((1, dv, tm), lambda b, j: (b * nj + j, 0, 0)),
        ],
        out_shape=[out_sds, jax.ShapeDtypeStruct((dc, t), BF16), out_sds,
                   jax.ShapeDtypeStruct((t // tm, dv, tm), BF16)],
        scratch_shapes=[pltpu.VMEM((8, dc), F32)],
        compiler_params=pltpu.CompilerParams(
            dimension_semantics=("arbitrary", "arbitrary"), vmem_limit_bytes=VMEM_LIMIT),
        name="inproj_conv_qknorm",
    )(x2, g, w_in, conv_w, conv_g, qg, kg, gmat)


def _attn_kernel(lp_ref, sg_ref, qT_ref, k_ref, vT_ref, o_ref, m_ref, acc_ref,
                 sa_ref, pb_ref, ab_ref, *, tq, tk, nq, lam_init):
    map_a = slice(0, tq)
    map_b = slice(tq, 2 * tq)

    lp = lp_ref[...]
    lam = (jnp.exp(jnp.sum(lp[0:1, :] * lp[1:2, :], axis=-1, keepdims=True))
           - jnp.exp(jnp.sum(lp[2:3, :] * lp[3:4, :], axis=-1, keepdims=True)) + lam_init)

    def stacked_queries(qi):
        qT = qT_ref[:, qi * tq:(qi + 1) * tq]
        row = lax.broadcasted_iota(jnp.int32, qT.shape, 0)
        zero = jnp.zeros_like(qT)
        return jnp.concatenate([jnp.where(row < HEAD_DIM, qT, zero),
                                jnp.where(row >= HEAD_DIM, qT, zero)], axis=1)

    def pieces(diag):
        return ((slice(0, tq // 2), tk // 2), (slice(tq // 2, tq), tk)) if diag else ((slice(0, tq), tk),)

    def shifted(cols, off):
        return slice(cols.start + off, cols.stop + off)

    def scores(j, qq, cols, n_keys):
        return _dot(k_ref[j * tk:j * tk + n_keys, :], qq[:, cols])

    def softmax(m, cols, s, q0):
        if q0 is not None:
            qrel = q0 + lax.broadcasted_iota(jnp.int32, s.shape, 1)
            krel = lax.broadcasted_iota(jnp.int32, s.shape, 0)
            s = jnp.where(krel <= qrel, s, -jnp.inf)
        sb = s.astype(BF16)
        m_old = m[:, cols]
        m_new = jnp.maximum(m_old, jnp.max(sb, axis=0, keepdims=True).astype(F32))
        alpha = jnp.exp2(m_old - m_new)
        p = jnp.exp2(sb - m_new.astype(BF16))
        m[:, cols] = m_new
        return alpha, p

    def accumulate(acc, cols, alpha, vb, p):
        acc[:, cols] = alpha * acc[:, cols] + _dot(vb, p)

    def prefetch_map_a(sa, j, qq, diag):
        for cols, n_keys in pieces(diag):
            sa[0:n_keys, cols] = scores(j, qq, cols, n_keys)

    stacked = {0: stacked_queries(0)}
    m_ref[0] = jnp.full(m_ref.shape[1:], -jnp.inf, F32)
    acc_ref[0] = jnp.zeros(acc_ref.shape[1:], F32)
    prefetch_map_a(sa_ref.at[0], 0, stacked[0], True)

    for qi in range(nq):
        par = qi % 2
        m, acc, sa, qq = m_ref.at[par], acc_ref.at[par], sa_ref.at[par], stacked[qi]
        for j in range(qi + 1):
            last = j == qi
            if j > 0:
                accumulate(acc, map_b, ab_ref[...], vT_ref[j - 1], pb_ref[...])
            s_b = [scores(j, qq, shifted(cols, tq), n_keys) for cols, n_keys in pieces(last)]
            for cols, n_keys in pieces(last):
                alpha_a, p_a = softmax(m, cols, sa[0:n_keys, cols], cols.start if last else None)
                accumulate(acc, cols, alpha_a, vT_ref[j, :, 0:n_keys], p_a)
            if not last:
                prefetch_map_a(sa, j + 1, qq, j + 1 == qi)
            elif qi + 1 < nq:
                stacked[qi + 1] = stacked_queries(qi + 1)
                m_ref[1 - par] = jnp.full(m_ref.shape[1:], -jnp.inf, F32)
                acc_ref[1 - par] = jnp.zeros(acc_ref.shape[1:], F32)
                prefetch_map_a(sa_ref.at[1 - par], 0, stacked[qi + 1], False)
            for (cols, n_keys), s in zip(pieces(last), s_b):
                alpha_b, p_b = softmax(m, shifted(cols, tq), s, cols.start if last else None)
                if last:
                    accumulate(acc, shifted(cols, tq), alpha_b, vT_ref[j, :, 0:n_keys], p_b)
                else:
                    ab_ref[...] = alpha_b
                    pb_ref[...] = p_b

        o = acc[0:V_DIM, :] / acc[V_DIM:V_DIM + 1, :]
        d = o[:, map_a] - lam * o[:, map_b]
        ms = jnp.mean(d * d, axis=0, keepdims=True)
        o_ref[qi * tq:(qi + 1) * tq, :] = (
            d * lax.rsqrt(ms + EPS) * sg_ref[...] * (1.0 - lam_init)).T.astype(BF16)
        del stacked[qi]


def _attention(lam_params, subln_g, qT, k, vT, *, batch, seq, lam_init):
    dq, t = qT.shape
    n_heads = dq // V_DIM
    tq, tk = TQ, TK
    assert tq == tk and vT.shape[2] == tk
    nq = seq // tq
    nk = seq // tk
    vrows = V_DIM + ONES_ROWS
    const = lambda b, h: (0, 0)
    return pl.pallas_call(
        functools.partial(_attn_kernel, tq=tq, tk=tk, nq=nq, lam_init=lam_init),
        grid=(batch, n_heads),
        in_specs=[
            pl.BlockSpec(lam_params.shape, const),
            pl.BlockSpec((V_DIM, 1), const),
            pl.BlockSpec((V_DIM, seq), lambda b, h: (h, b)),
            pl.BlockSpec((seq, V_DIM), lambda b, h: (b, h)),
            pl.BlockSpec((nk, vrows, tk), lambda b, h: (b, h, 0)),
        ],
        out_specs=pl.BlockSpec((seq, V_DIM), lambda b, h: (b, h)),
        out_shape=jax.ShapeDtypeStruct((t, dq), BF16),
        scratch_shapes=[pltpu.VMEM((2, 1, 2 * tq), F32),
                        pltpu.VMEM((2, vrows, 2 * tq), F32),
                        pltpu.VMEM((2, tk, tq), F32), pltpu.VMEM((tk, tq), BF16), pltpu.VMEM((1, tq), F32)],
        compiler_params=pltpu.CompilerParams(
            dimension_semantics=("arbitrary", "arbitrary"), vmem_limit_bytes=VMEM_LIMIT),
        name="diff_attention",
    )(lam_params, subln_g, qT, k, vT)


def _outproj_router_kernel(x_ref, yc_ref, at_ref, wo_ref, g_ref, wr_ref,
                           h_ref, hn_ref, route_ref, routeT_ref, cnt_ref, *, tm, dc):
    i = pl.program_id(0)

    @pl.when(i == 0)
    def _():
        cnt_ref[...] = jnp.zeros_like(cnt_ref)

    for r0 in range(0, x_ref.shape[0], tm):
        _outproj_router_subtile(slice(r0, r0 + tm), x_ref, yc_ref, at_ref, wo_ref, g_ref, wr_ref,
                                h_ref, hn_ref, route_ref, routeT_ref, cnt_ref, tm=tm, dc=dc)


def _outproj_router_subtile(rows, x_ref, yc_ref, at_ref, wo_ref, g_ref, wr_ref,
                            h_ref, hn_ref, route_ref, routeT_ref, cnt_ref, *, tm, dc):
    h = x_ref[rows, :] + _dot(yc_ref[rows, :], wo_ref[0:dc, :]) + _dot(at_ref[rows, :], wo_ref[dc:2 * dc, :])
    h_ref[rows, :] = h
    ms = jnp.mean(h * h, axis=-1, keepdims=True)
    hn = h * lax.rsqrt(ms + EPS) * g_ref[...]
    hi = hn.astype(BF16)
    hn_ref[rows, :] = _pack_rows(hn)
    lo = (hn - hi.astype(F32)).astype(BF16)
    prod = _dot(hi, wr_ref[...])
    logits = prod[:, 0:LANES] + prod[:, LANES:2 * LANES] + _dot(lo, wr_ref[:, 0:LANES])

    lt = logits.T
    neg = -jnp.inf
    grow = lax.broadcasted_iota(jnp.int32, (8, tm), 0).astype(F32)
    gl = jnp.where(grow < N_GROUPS, lt[N_EXPERTS:N_EXPERTS + 8, :], neg)
    gmax = jnp.max(gl, axis=0, keepdims=True)
    g_gate = 1.0 / jnp.sum(jnp.exp(gl - gmax), axis=0, keepdims=True)
    g_idx = jnp.min(jnp.where(gl == gmax, grow, 1e9), axis=0, keepdims=True)
    erow = lax.broadcasted_iota(jnp.int32, (N_EXPERTS, tm), 0).astype(F32)
    e_lo = g_idx * EXPERTS_PER_GROUP
    el = jnp.where((erow >= e_lo) & (erow < e_lo + EXPERTS_PER_GROUP), lt[0:N_EXPERTS, :], neg)
    v1 = jnp.max(el, axis=0, keepdims=True)
    i1 = jnp.min(jnp.where(el == v1, erow, 1e9), axis=0, keepdims=True)
    el2 = jnp.where(erow == i1, neg, el)
    v2 = jnp.max(el2, axis=0, keepdims=True)
    i2 = jnp.min(jnp.where(el2 == v2, erow, 1e9), axis=0, keepdims=True)
    tt = jnp.exp(v2 - v1)
    w1 = g_gate / (1.0 + tt)
    w2 = g_gate * tt / (1.0 + tt)

    sel1 = erow == i1
    sel2 = erow == i2
    oh = jnp.where(sel1 | sel2, 1.0, 0.0)
    ss = lax.broadcasted_iota(jnp.int32, (tm, tm), 0)
    tt_i = lax.broadcasted_iota(jnp.int32, (tm, tm), 1)
    earlier = jnp.where(ss < tt_i, 1.0, 0.0).astype(BF16)
    ranks = _dot(oh.astype(BF16), earlier) + cnt_ref[...]
    r1 = jnp.sum(jnp.where(sel1, ranks, 0.0), axis=0, keepdims=True)
    r2 = jnp.sum(jnp.where(sel2, ranks, 0.0), axis=0, keepdims=True)
    cnt_ref[...] = cnt_ref[...] + jnp.sum(oh, axis=1, keepdims=True)

    routeT = jnp.concatenate([i1, i2, r1, r2, w1, w2, jnp.zeros((2, tm), F32)], axis=0)
    routeT_ref[:, rows] = routeT
    route_ref[rows, :] = jnp.concatenate([routeT, jnp.zeros((LANES - 8, tm), F32)], axis=0).T


def _outproj_router(x2, yc, at, w_out, g, wr_cat):
    t, d = x2.shape
    dc = yc.shape[1]
    tm = TM_PROJ
    tb = tm * OUTPROJ_SUBTILES
    row = lambda i: (i, 0)
    const = lambda i: (0, 0)
    return pl.pallas_call(
        functools.partial(_outproj_router_kernel, tm=tm, dc=dc),
        grid=(t // tb,),
        in_specs=[
            pl.BlockSpec((tb, d), row),
            pl.BlockSpec((tb, dc), row),
            pl.BlockSpec((tb, dc), row),
            pl.BlockSpec(w_out.shape, const),
            pl.BlockSpec((1, d), const),
            pl.BlockSpec(wr_cat.shape, const),
        ],
        out_specs=[
            pl.BlockSpec((tb, d), row),
            pl.BlockSpec((tb, d // 2), row),
            pl.BlockSpec((tb, LANES), row),
            pl.BlockSpec((8, tb), lambda i: (0, i)),
            pl.BlockSpec((N_EXPERTS, 1), const),
        ],
        out_shape=[
            jax.ShapeDtypeStruct((t, d), F32),
            jax.ShapeDtypeStruct((t, d // 2), jnp.int32),
            jax.ShapeDtypeStruct((t, LANES), F32),
            jax.ShapeDtypeStruct((8, t), F32),
            jax.ShapeDtypeStruct((N_EXPERTS, 1), F32),
        ],
        compiler_params=pltpu.CompilerParams(
            dimension_semantics=("arbitrary",), vmem_limit_bytes=VMEM_LIMIT),
        name="outproj_router",
    )(x2, yc, at, w_out, g, wr_cat)


def _positions_kernel(offs_ref, rt_ref, pos_ref):
    rt = rt_ref[...]
    ea, eb = rt[0:1, :], rt[1:2, :]
    sa = jnp.zeros_like(ea)
    sb = jnp.zeros_like(eb)
    for e in range(N_EXPERTS):
        start = offs_ref[e].astype(F32)
        sa = jnp.where(ea == e, start, sa)
        sb = jnp.where(eb == e, start, sb)
    pos_ref[0:1, :] = (sa + rt[2:3, :]).astype(jnp.int32)
    pos_ref[1:2, :] = (sb + rt[3:4, :]).astype(jnp.int32)


def _positions(offs, routeT):
    t = routeT.shape[1]
    return pl.pallas_call(
        _positions_kernel,
        grid_spec=pltpu.PrefetchScalarGridSpec(
            num_scalar_prefetch=1, grid=(1,),
            in_specs=[pl.BlockSpec(routeT.shape, lambda i, offs: (0, 0))],
            out_specs=pl.BlockSpec((2, t), lambda i, offs: (0, 0)),
        ),
        out_shape=jax.ShapeDtypeStruct((2, t), jnp.int32),
        name="positions",
    )(offs, routeT)


def _moe_kernel(te_ref, nt_ref, nx_ref, sl_ref, x_ref, wg_hbm, wu_hbm, wd_hbm, y_ref,
                wg_st, wu_st, wd_st, wgb_ref, wub_ref, wdb_ref, sem, *, tm, tiles_per_step):
    last = nt_ref[0] - 1

    def weight_copies(e, s):
        return (pltpu.make_async_copy(wg_hbm.at[e], wg_st.at[s], sem.at[s, 0]),
                pltpu.make_async_copy(wu_hbm.at[e], wu_st.at[s], sem.at[s, 1]),
                pltpu.make_async_copy(wd_hbm.at[e], wd_st.at[s], sem.at[s, 2]))

    for u in range(tiles_per_step):
        i = pl.program_id(0) * tiles_per_step + u
        rows = slice(u * tm, (u + 1) * tm)
        ic = jnp.minimum(i, last)
        expert = te_ref[ic]
        slot = sl_ref[ic]
        first_of_expert = (i == 0) | ((i <= last) & (expert != te_ref[jnp.maximum(ic - 1, 0)]))

        if u == 0:
            @pl.when(i == 0)
            def _():
                for c in weight_copies(expert, slot):
                    c.start()

        @pl.when(first_of_expert)
        def _():
            for c in weight_copies(expert, slot):
                c.wait()
            nxt = nx_ref[ic]

            @pl.when(nxt >= 0)
            def _():
                for c in weight_copies(nxt, 1 - slot):
                    c.start()

            wgb_ref[...] = wg_st[slot].astype(BF16)
            wub_ref[...] = wu_st[slot].astype(BF16)
            wdb_ref[...] = wd_st[slot].astype(BF16)

        @pl.when(i <= last)
        def _():
            x_l, x_r = _unpack_rows(x_ref[rows, :])
            x_l = x_l.astype(BF16)
            x_r = x_r.astype(BF16)
            half = x_l.shape[1]
            hg = _dot(x_l, wgb_ref[0:half, :]) + _dot(x_r, wgb_ref[half:2 * half, :])
            hu = _dot(x_l, wub_ref[0:half, :]) + _dot(x_r, wub_ref[half:2 * half, :])
            act = hg * (1.0 / (1.0 + jnp.exp(-hg))) * hu
            y_ref[rows, :] = _pack_rows(_dot(act.astype(BF16), wdb_ref[...]))


def _moe(tile_expert, n_tiles, next_expert, stage_slot, xs, wg, wu, wd):
    p, dp = xs.shape
    d, f = wg.shape[1], wg.shape[2]
    assert dp * 2 == d
    tm, tps = TM_MOE, MOE_TILES_PER_STEP
    tb = tm * tps
    row = lambda s, te, nt, nx, sl: (jnp.minimum(s, (nt[0] - 1) // tps), 0)
    hbm = pl.BlockSpec(memory_space=pl.ANY)
    grid_spec = pltpu.PrefetchScalarGridSpec(
        num_scalar_prefetch=4,
        grid=(p // tb,),
        in_specs=[pl.BlockSpec((tb, dp), row), hbm, hbm, hbm],
        out_specs=pl.BlockSpec((tb, dp), row),
        scratch_shapes=[pltpu.VMEM((2, d, f), F32), pltpu.VMEM((2, d, f), F32), pltpu.VMEM((2, f, d), F32),
                        pltpu.VMEM((d, f), BF16), pltpu.VMEM((d, f), BF16), pltpu.VMEM((f, d), BF16),
                        pltpu.SemaphoreType.DMA((2, 3))],
    )
    return pl.pallas_call(
        functools.partial(_moe_kernel, tm=tm, tiles_per_step=tps),
        grid_spec=grid_spec,
        out_shape=jax.ShapeDtypeStruct((p, dp), jnp.int32),
        compiler_params=pltpu.CompilerParams(
            dimension_semantics=("arbitrary",), vmem_limit_bytes=VMEM_LIMIT),
        name="moe_experts",
    )(tile_expert, n_tiles, next_expert, stage_slot, xs, wg, wu, wd)


def _sc_mesh():
    return plsc.VectorSubcoreMesh(core_axis_name="c", subcore_axis_name="s",
                                  num_cores=SC_CORES, num_subcores=SC_SUBCORES)


def _sc_dispatch(rows, pos_a, pos_b, n_out):
    t, d = rows.shape
    win = pos_a.shape[1]

    @functools.partial(pl.kernel, out_type=jax.ShapeDtypeStruct((n_out, d), rows.dtype),
                       mesh=_sc_mesh(), scratch_types=[], name="sc_dispatch")
    def run(rows_hbm, pa_hbm, pb_hbm, out_hbm):
        def body(rows_vmem, pa_vmem, pb_vmem):
            pltpu.sync_copy(rows_vmem, out_hbm.at[pa_vmem.at[0]])
            pltpu.sync_copy(rows_vmem, out_hbm.at[pb_vmem.at[0]])

        pltpu.emit_pipeline(
            body, grid=(t // win,),
            in_specs=[pl.BlockSpec((win, d), lambda i: (i, 0)),
                      pl.BlockSpec((1, win), lambda i: (i, 0)),
                      pl.BlockSpec((1, win), lambda i: (i, 0))],
            out_specs=[],
            core_axis_name=("c", "s"),
            dimension_semantics=(pltpu.PARALLEL,),
        )(rows_hbm, pa_hbm, pb_hbm)

    return run(rows, pos_a, pos_b)


def _sc_gather(table, idx):
    d = table.shape[1]
    n_win, win = idx.shape

    @functools.partial(pl.kernel, out_type=jax.ShapeDtypeStruct((n_win * win, d), table.dtype),
                       mesh=_sc_mesh(), scratch_types=[], name="sc_gather")
    def run(table_hbm, idx_hbm, out_hbm):
        def body(idx_vmem, out_vmem):
            pltpu.sync_copy(table_hbm.at[idx_vmem.at[0]], out_vmem)

        pltpu.emit_pipeline(
            body, grid=(n_win,),
            in_specs=[pl.BlockSpec((1, win), lambda i: (i, 0))],
            out_specs=[pl.BlockSpec((win, d), lambda i: (i, 0))],
            core_axis_name=("c", "s"),
            dimension_semantics=(pltpu.PARALLEL,),
        )(idx_hbm, out_hbm)

    return run(table, idx)


def _combine_kernel(h_ref, ya_ref, yb_ref, r_ref, o_ref):
    r = r_ref[...]
    wa, wb = r[:, 4:5], r[:, 5:6]
    a_l, a_r = _unpack_rows(ya_ref[...])
    b_l, b_r = _unpack_rows(yb_ref[...])
    half = a_l.shape[1]
    o_ref[:, 0:half] = h_ref[:, 0:half] + wa * a_l + wb * b_l
    o_ref[:, half:2 * half] = h_ref[:, half:2 * half] + wa * a_r + wb * b_r


def _combine(hres, yg, route):
    t, d = hres.shape
    tm = TM_COMBINE
    nb = t // tm
    return pl.pallas_call(
        _combine_kernel,
        grid=(nb,),
        in_specs=[pl.BlockSpec((tm, d), lambda i: (i, 0)),
                  pl.BlockSpec((tm, d // 2), lambda i: (i, 0)),
                  pl.BlockSpec((tm, d // 2), lambda i: (i + nb, 0)),
                  pl.BlockSpec((tm, LANES), lambda i: (i, 0))],
        out_specs=pl.BlockSpec((tm, d), lambda i: (i, 0)),
        out_shape=jax.ShapeDtypeStruct((t, d), F32),
        compiler_params=pltpu.CompilerParams(
            dimension_semantics=("arbitrary",), vmem_limit_bytes=VMEM_LIMIT),
        name="combine",
    )(hres, yg, yg, route)


def _lambda_init(layer_idx):
    return 0.8 - 0.6 * math.exp(-0.3 * layer_idx)


def _layer(h, l, attn_norm_g, w_in, conv_w, conv_out_g, q_norm_g, k_norm_g,
           lambda_q1, lambda_k1, lambda_q2, lambda_k2, attn_subln_g, w_out,
           ffn_norm_g, w_router_group, w_router_expert, w_exp_gate, w_exp_up, w_exp_down):
    batch, seq, d = h.shape
    t = batch * seq
    dc = conv_w.shape[-1]
    lam_init = _lambda_init(l)
    x2 = h.reshape(t, d)

    reps = dc // HEAD_DIM
    assert dc // CONV_GROUPS == HEAD_DIM
    qg = (jnp.tile(q_norm_g[l], reps) * (HEAD_DIM ** -0.5 * math.log2(math.e))).reshape(1, dc)
    kg = jnp.tile(k_norm_g[l], reps).reshape(1, dc)
    grp = jnp.arange(MXU_TILE) // HEAD_DIM
    gmat = jnp.where(grp[:, None] == grp[None, :], 1.0 / HEAD_DIM, 0.0).astype(BF16)
    yc, qT, k, vT = _inproj(x2, attn_norm_g[l].reshape(1, d), w_in[l].astype(BF16), conv_w[l],
                          conv_out_g[l].reshape(1, dc), qg, kg, gmat, batch=batch, seq=seq)

    lam_params = jnp.stack([lambda_q1[l], lambda_k1[l], lambda_q2[l], lambda_k2[l]])
    at = _attention(lam_params, attn_subln_g[l].reshape(V_DIM, 1), qT, k, vT,
                    batch=batch, seq=seq, lam_init=lam_init)

    wr = jnp.concatenate([w_router_expert[l], w_router_group[l],
                          jnp.zeros((d, LANES - N_EXPERTS - N_GROUPS), F32)], axis=1)
    wr_hi = wr.astype(BF16)
    wr_lo = (wr - wr_hi.astype(F32)).astype(BF16)
    hres, hn2, route, routeT, cnt = _outproj_router(x2, yc, at, w_out[l].astype(BF16),
                                                    ffn_norm_g[l].reshape(1, d),
                                                    jnp.concatenate([wr_hi, wr_lo], axis=1))

    tmm = TM_MOE
    n_tiles_max = (2 * t) // tmm + N_EXPERTS
    p_rows = n_tiles_max * tmm
    counts = cnt[:, 0].astype(jnp.int32)
    tiles = (counts + tmm - 1) // tmm
    tile_end = jnp.cumsum(tiles)
    offs = (tile_end - tiles) * tmm
    pos = _positions(offs, routeT)
    n_tiles = tile_end[-1:].astype(jnp.int32)
    tile_ids = jnp.arange(n_tiles_max, dtype=jnp.int32)
    tile_expert = jnp.minimum(
        jnp.sum((tile_end[None, :] <= tile_ids[:, None]).astype(jnp.int32), axis=1), N_EXPERTS - 1)
    e_ids = jnp.arange(N_EXPERTS, dtype=jnp.int32)
    nonempty = tiles > 0
    later = jnp.where(nonempty[None, :] & (e_ids[None, :] > e_ids[:, None]), e_ids[None, :], N_EXPERTS)
    next_of = jnp.min(later, axis=1)
    next_of = jnp.where(next_of == N_EXPERTS, -1, next_of).astype(jnp.int32)
    slot_of = ((jnp.cumsum(nonempty.astype(jnp.int32)) - 1) % 2).astype(jnp.int32)
    onehot_te = (tile_expert[:, None] == e_ids[None, :]).astype(jnp.int32)
    next_expert = jnp.sum(onehot_te * next_of[None, :], axis=1)
    stage_slot = jnp.sum(onehot_te * slot_of[None, :], axis=1)

    posw = pos.reshape(2 * t // SC_WIN, SC_WIN)
    pos1w = posw[:t // SC_WIN]
    pos2w = posw[t // SC_WIN:]
    xs = _sc_dispatch(hn2, pos1w, pos2w, p_rows)

    f = w_exp_gate.shape[-1]
    ys = _moe(tile_expert, n_tiles, next_expert, stage_slot, xs,
              w_exp_gate[l].reshape(N_EXPERTS, d, f),
              w_exp_up[l].reshape(N_EXPERTS, d, f),
              w_exp_down[l].reshape(N_EXPERTS, f, d))
    yg = _sc_gather(ys, posw)
    out = _combine(hres, yg, route)
    return out.reshape(batch, seq, d)


def kernel(x, attn_norm_g, w_in, conv_w, conv_out_g, q_norm_g, k_norm_g, lambda_q1, lambda_k1,
           lambda_q2, lambda_k2, attn_subln_g, w_out, ffn_norm_g, w_router_group, w_router_expert,
           w_exp_gate, w_exp_up, w_exp_down):
    h = x
    for l in range(attn_norm_g.shape[0]):
        h = _layer(h, l, attn_norm_g, w_in, conv_w, conv_out_g, q_norm_g, k_norm_g,
                   lambda_q1, lambda_k1, lambda_q2, lambda_k2, attn_subln_g, w_out,
                   ffn_norm_g, w_router_group, w_router_expert, w_exp_gate, w_exp_up, w_exp_down)
    return h
```

```python
import functools
import math

import jax
import jax.numpy as jnp
from jax import lax
from jax.experimental import pallas as pl
from jax.experimental.pallas import tpu as pltpu
from jax.experimental.pallas import tpu_sc as plsc

F32 = jnp.float32
BF16 = jnp.bfloat16

HEAD_DIM = 64
V_DIM = 2 * HEAD_DIM
CONV_GROUPS = 8
N_GROUPS = 4
EXPERTS_PER_GROUP = 8
N_EXPERTS = N_GROUPS * EXPERTS_PER_GROUP
EPS = 1e-6
LANES = 128
MXU_TILE = 256
ONES_ROWS = 16
VMEM_LIMIT = 48 * 1024 * 1024

TM_PROJ = 256
TM_COMBINE = 1024
OUTPROJ_SUBTILES = 4
INPROJ_SUBTILES = 2
TQ = 512
TK = 512
TM_MOE = 512
MOE_TILES_PER_STEP = 2
SC_CORES = 2
SC_SUBCORES = 16
SC_WIN = 64


def _dot(a, b):
    return jnp.dot(a, b, preferred_element_type=F32)


def _pack_rows(x):
    w = x.shape[1] // 2
    bits = lax.bitcast_convert_type(x.astype(BF16).astype(F32), jnp.uint32)
    return lax.bitcast_convert_type((bits[:, :w] >> 16) | bits[:, w:], jnp.int32)


def _unpack_rows(packed):
    bits = lax.bitcast_convert_type(packed, jnp.uint32)
    left = lax.bitcast_convert_type(bits << 16, F32)
    right = lax.bitcast_convert_type(bits & jnp.uint32(0xFFFF0000), F32)
    return left, right


def _group_mean(sq, gm):
    w = gm.shape[0]
    sq = sq.astype(BF16)
    return jnp.concatenate([_dot(sq[:, c:c + w], gm) for c in range(0, sq.shape[1], w)], axis=1)


def _inproj_kernel(x_ref, g_ref, w_ref, cw_ref, cg_ref, qg_ref, kg_ref, gm_ref,
                   yc_ref, qT_ref, k_ref, vT_ref, carry_ref, *, tm, dc):
    j = pl.program_id(1)

    @pl.when(j == 0)
    def _():
        carry_ref[...] = jnp.zeros_like(carry_ref)

    for u in range(x_ref.shape[0] // tm):
        _inproj_subtile(u, x_ref, g_ref, w_ref, cw_ref, cg_ref, qg_ref, kg_ref, gm_ref,
                        yc_ref, qT_ref, k_ref, vT_ref, carry_ref, tm=tm, dc=dc)


def _inproj_subtile(u, x_ref, g_ref, w_ref, cw_ref, cg_ref, qg_ref, kg_ref, gm_ref,
                    yc_ref, qT_ref, k_ref, vT_ref, carry_ref, *, tm, dc):
    rows = slice(u * tm, (u + 1) * tm)
    x = x_ref[rows, :]
    ms = jnp.mean(x * x, axis=-1, keepdims=True)
    hn = (x * lax.rsqrt(ms + EPS) * g_ref[...]).astype(BF16)

    def proj(s):
        return _dot(hn, w_ref[:, s * dc:(s + 1) * dc])

    gm = gm_ref[...]

    c = proj(2) * proj(0)
    prev = carry_ref[...]
    r = lax.broadcasted_iota(jnp.int32, c.shape, 0)
    c1 = jnp.where(r == 0, prev[7:8, :], pltpu.roll(c, 1, 0))
    c2 = jnp.where(r == 0, prev[6:7, :], jnp.where(r == 1, prev[7:8, :], pltpu.roll(c, 2, 0)))
    carry_ref[...] = c[tm - 8:tm, :]
    cw = cw_ref[...]
    y = proj(1) * (cw[0:1, :] * c2 + cw[1:2, :] * c1 + cw[2:3, :] * c)
    yc_ref[rows, :] = (y * lax.rsqrt(_group_mean(y * y, gm) + EPS) * cg_ref[...]).astype(BF16)

    q = proj(3)
    qT_ref[:, rows] = (q * lax.rsqrt(_group_mean(q * q, gm) + EPS) * qg_ref[...]).T.astype(BF16)
    k = proj(4)
    k_ref[rows, :] = (k * lax.rsqrt(_group_mean(k * k, gm) + EPS) * kg_ref[...]).astype(BF16)
    vt = proj(5).T.astype(BF16)
    vrows = V_DIM + ONES_ROWS
    for h in range(dc // V_DIM):
        vT_ref[u, h * vrows:h * vrows + V_DIM, :] = vt[h * V_DIM:(h + 1) * V_DIM, :]
        vT_ref[u, h * vrows + V_DIM:(h + 1) * vrows, :] = jnp.ones((ONES_ROWS, tm), BF16)


def _inproj(x2, g, w_in, conv_w, conv_g, qg, kg, gmat, *, batch, seq):
    t, d = x2.shape
    dc = conv_g.shape[1]
    dv = dc // V_DIM * (V_DIM + ONES_ROWS)
    tm = TK
    tb = tm * INPROJ_SUBTILES
    nj = seq // tb
    row = lambda b, j: (b * nj + j, 0)
    const = lambda b, j: (0, 0)
    out_sds = jax.ShapeDtypeStruct((t, dc), BF16)
    return pl.pallas_call(
        functools.partial(_inproj_kernel, tm=tm, dc=dc),
        grid=(batch, nj),
        in_specs=[
            pl.BlockSpec((tb, d), row),
            pl.BlockSpec((1, d), const),
            pl.BlockSpec(w_in.shape, const),
            pl.BlockSpec(conv_w.shape, const),
            pl.BlockSpec((1, dc), const),
            pl.BlockSpec((1, dc), const),
            pl.BlockSpec((1, dc), const),
            pl.BlockSpec(gmat.shape, const),
        ],
        out_specs=[
            pl.BlockSpec((tb, dc), row),
            pl.BlockSpec((dc, tb), lambda b, j: (0, b * nj + j)),
            pl.BlockSpec((tb, dc), row),
            pl.BlockSpec((INPROJ_SUBTILES, dv, tm), lambda b, j: (b * nj + j, 0, 0)),
        ],
        out_shape=[out_sds, jax.ShapeDtypeStruct((dc, t), BF16), out_sds,
                   jax.ShapeDtypeStruct((t // tm, dv, tm), BF16)],
        scratch_shapes=[pltpu.VMEM((8, dc), F32)],
        compiler_params=pltpu.CompilerParams(
            dimension_semantics=("arbitrary", "arbitrary"), vmem_limit_bytes=VMEM_LIMIT),
        name="inproj_conv_qknorm",
    )(x2, g, w_in, conv_w, conv_g, qg, kg, gmat)


def _attn_kernel(lp_ref, sg_ref, qT_ref, k_ref, vT_ref, o_ref, m_ref, acc_ref,
                 sa_ref, pb_ref, ab_ref, *, tq, tk, nq, lam_init):
    map_a = slice(0, tq)
    map_b = slice(tq, 2 * tq)

    lp = lp_ref[...]
    lam = (jnp.exp(jnp.sum(lp[0:1, :] * lp[1:2, :], axis=-1, keepdims=True))
           - jnp.exp(jnp.sum(lp[2:3, :] * lp[3:4, :], axis=-1, keepdims=True)) + lam_init)

    def stacked_queries(qi):
        qT = qT_ref[:, qi * tq:(qi + 1) * tq]
        row = lax.broadcasted_iota(jnp.int32, qT.shape, 0)
        zero = jnp.zeros_like(qT)
        return jnp.concatenate([jnp.where(row < HEAD_DIM, qT, zero),
                                jnp.where(row >= HEAD_DIM, qT, zero)], axis=1)

    def pieces(diag):
        return ((slice(0, tq // 2), tk // 2), (slice(tq // 2, tq), tk)) if diag else ((slice(0, tq), tk),)

    def shifted(cols, off):
        return slice(cols.start + off, cols.stop + off)

    def scores(j, qq, cols, n_keys):
        return _dot(k_ref[j * tk:j * tk + n_keys, :], qq[:, cols])

    def softmax(m, cols, s, q0):
        if q0 is not None:
            qrel = q0 + lax.broadcasted_iota(jnp.int32, s.shape, 1)
            krel = lax.broadcasted_iota(jnp.int32, s.shape, 0)
            s = jnp.where(krel <= qrel, s, -jnp.inf)
        sb = s.astype(BF16)
        m_old = m[:, cols]
        m_new = jnp.maximum(m_old, jnp.max(sb, axis=0, keepdims=True).astype(F32))
        alpha = jnp.exp2(m_old - m_new)
        p = jnp.exp2(sb - m_new.astype(BF16))
        m[:, cols] = m_new
        return alpha, p

    def accumulate(acc, cols, alpha, vb, p):
        acc[:, cols] = alpha * acc[:, cols] + _dot(vb, p)

    def prefetch_map_a(sa, j, qq, diag):
        for cols, n_keys in pieces(diag):
            sa[0:n_keys, cols] = scores(j, qq, cols, n_keys)

    stacked = {0: stacked_queries(0)}
    m_ref[0] = jnp.full(m_ref.shape[1:], -jnp.inf, F32)
    acc_ref[0] = jnp.zeros(acc_ref.shape[1:], F32)
    prefetch_map_a(sa_ref.at[0], 0, stacked[0], True)

    for qi in range(nq):
        par = qi % 2
        m, acc, sa, qq = m_ref.at[par], acc_ref.at[par], sa_ref.at[par], stacked[qi]
        for j in range(qi + 1):
            last = j == qi
            if j > 0:
                accumulate(acc, map_b, ab_ref[...], vT_ref[j - 1], pb_ref[...])
            s_b = [scores(j, qq, shifted(cols, tq), n_keys) for cols, n_keys in pieces(last)]
            for cols, n_keys in pieces(last):
                alpha_a, p_a = softmax(m, cols, sa[0:n_keys, cols], cols.start if last else None)
                accumulate(acc, cols, alpha_a, vT_ref[j, :, 0:n_keys], p_a)
            if not last:
                prefetch_map_a(sa, j + 1, qq, j + 1 == qi)
            elif qi + 1 < nq:
                stacked[qi + 1] = stacked_queries(qi + 1)
                m_ref[1 - par] = jnp.full(m_ref.shape[1:], -jnp.inf, F32)
                acc_ref[1 - par] = jnp.zeros(acc_ref.shape[1:], F32)
                prefetch_map_a(sa_ref.at[1 - par], 0, stacked[qi + 1], False)
            for (cols, n_keys), s in zip(pieces(last), s_b):
                alpha_b, p_b = softmax(m, shifted(cols, tq), s, cols.start if last else None)
                if last:
                    accumulate(acc, shifted(cols, tq), alpha_b, vT_ref[j, :, 0:n_keys], p_b)
                else:
                    ab_ref[...] = alpha_b
                    pb_ref[...] = p_b

        o = acc[0:V_DIM, :] / acc[V_DIM:V_DIM + 1, :]
        d = o[:, map_a] - lam * o[:, map_b]
        ms = jnp.mean(d * d, axis=0, keepdims=True)
        o_ref[qi * tq:(qi + 1) * tq, :] = (
            d * lax.rsqrt(ms + EPS) * sg_ref[...] * (1.0 - lam_init)).T.astype(BF16)
        del stacked[qi]


def _attention(lam_params, subln_g, qT, k, vT, *, batch, seq, lam_init):
    dq, t = qT.shape
    n_heads = dq // V_DIM
    tq, tk = TQ, TK
    assert tq == tk and vT.shape[2] == tk
    nq = seq // tq
    nk = seq // tk
    vrows = V_DIM + ONES_ROWS
    const = lambda b, h: (0, 0)
    return pl.pallas_call(
        functools.partial(_attn_kernel, tq=tq, tk=tk, nq=nq, lam_init=lam_init),
        grid=(batch, n_heads),
        in_specs=[
            pl.BlockSpec(lam_params.shape, const),
            pl.BlockSpec((V_DIM, 1), const),
            pl.BlockSpec((V_DIM, seq), lambda b, h: (h, b)),
            pl.BlockSpec((seq, V_DIM), lambda b, h: (b, h)),
            pl.BlockSpec((nk, vrows, tk), lambda b, h: (b, h, 0)),
        ],
        out_specs=pl.BlockSpec((seq, V_DIM), lambda b, h: (b, h)),
        out_shape=jax.ShapeDtypeStruct((t, dq), BF16),
        scratch_shapes=[pltpu.VMEM((2, 1, 2 * tq), F32),
                        pltpu.VMEM((2, vrows, 2 * tq), F32),
                        pltpu.VMEM((2, tk, tq), F32), pltpu.VMEM((tk, tq), BF16), pltpu.VMEM((1, tq), F32)],
        compiler_params=pltpu.CompilerParams(
            dimension_semantics=("arbitrary", "arbitrary"), vmem_limit_bytes=VMEM_LIMIT),
        name="diff_attention",
    )(lam_params, subln_g, qT, k, vT)


def _outproj_router_kernel(x_ref, yc_ref, at_ref, wo_ref, g_ref, wr_ref,
                           h_ref, hn_ref, route_ref, routeT_ref, cnt_ref, *, tm, dc):
    i = pl.program_id(0)

    @pl.when(i == 0)
    def _():
        cnt_ref[...] = jnp.zeros_like(cnt_ref)

    for r0 in range(0, x_ref.shape[0], tm):
        _outproj_router_subtile(slice(r0, r0 + tm), x_ref, yc_ref, at_ref, wo_ref, g_ref, wr_ref,
                                h_ref, hn_ref, route_ref, routeT_ref, cnt_ref, tm=tm, dc=dc)


def _outproj_router_subtile(rows, x_ref, yc_ref, at_ref, wo_ref, g_ref, wr_ref,
                            h_ref, hn_ref, route_ref, routeT_ref, cnt_ref, *, tm, dc):
    h = x_ref[rows, :] + _dot(yc_ref[rows, :], wo_ref[0:dc, :]) + _dot(at_ref[rows, :], wo_ref[dc:2 * dc, :])
    h_ref[rows, :] = h
    ms = jnp.mean(h * h, axis=-1, keepdims=True)
    hn = h * lax.rsqrt(ms + EPS) * g_ref[...]
    hi = hn.astype(BF16)
    hn_ref[rows, :] = _pack_rows(hn)
    lo = (hn - hi.astype(F32)).astype(BF16)
    prod = _dot(hi, wr_ref[...])
    logits = prod[:, 0:LANES] + prod[:, LANES:2 * LANES] + _dot(lo, wr_ref[:, 0:LANES])

    lt = logits.T
    neg = -jnp.inf
    grow = lax.broadcasted_iota(jnp.int32, (8, tm), 0).astype(F32)
    gl = jnp.where(grow < N_GROUPS, lt[N_EXPERTS:N_EXPERTS + 8, :], neg)
    gmax = jnp.max(gl, axis=0, keepdims=True)
    g_gate = 1.0 / jnp.sum(jnp.exp(gl - gmax), axis=0, keepdims=True)
    g_idx = jnp.min(jnp.where(gl == gmax, grow, 1e9), axis=0, keepdims=True)
    erow = lax.broadcasted_iota(jnp.int32, (N_EXPERTS, tm), 0).astype(F32)
    e_lo = g_idx * EXPERTS_PER_GROUP
    el = jnp.where((erow >= e_lo) & (erow < e_lo + EXPERTS_PER_GROUP), lt[0:N_EXPERTS, :], neg)
    v1 = jnp.max(el, axis=0, keepdims=True)
    i1 = jnp.min(jnp.where(el == v1, erow, 1e9), axis=0, keepdims=True)
    el2 = jnp.where(erow == i1, neg, el)
    v2 = jnp.max(el2, axis=0, keepdims=True)
    i2 = jnp.min(jnp.where(el2 == v2, erow, 1e9), axis=0, keepdims=True)
    tt = jnp.exp(v2 - v1)
    w1 = g_gate / (1.0 + tt)
    w2 = g_gate * tt / (1.0 + tt)

    sel1 = erow == i1
    sel2 = erow == i2
    oh = jnp.where(sel1 | sel2, 1.0, 0.0)
    ss = lax.broadcasted_iota(jnp.int32, (tm, tm), 0)
    tt_i = lax.broadcasted_iota(jnp.int32, (tm, tm), 1)
    earlier = jnp.where(ss < tt_i, 1.0, 0.0).astype(BF16)
    ranks = _dot(oh.astype(BF16), earlier) + cnt_ref[...]
    r1 = jnp.sum(jnp.where(sel1, ranks, 0.0), axis=0, keepdims=True)
    r2 = jnp.sum(jnp.where(sel2, ranks, 0.0), axis=0, keepdims=True)
    cnt_ref[...] = cnt_ref[...] + jnp.sum(oh, axis=1, keepdims=True)

    routeT = jnp.concatenate([i1, i2, r1, r2, w1, w2, jnp.zeros((2, tm), F32)], axis=0)
    routeT_ref[:, rows] = routeT
    route_ref[rows, :] = jnp.concatenate([routeT, jnp.zeros((LANES - 8, tm), F32)], axis=0).T


def _outproj_router(x2, yc, at, w_out, g, wr_cat):
    t, d = x2.shape
    dc = yc.shape[1]
    tm = TM_PROJ
    tb = tm * OUTPROJ_SUBTILES
    row = lambda i: (i, 0)
    const = lambda i: (0, 0)
    return pl.pallas_call(
        functools.partial(_outproj_router_kernel, tm=tm, dc=dc),
        grid=(t // tb,),
        in_specs=[
            pl.BlockSpec((tb, d), row),
            pl.BlockSpec((tb, dc), row),
            pl.BlockSpec((tb, dc), row),
            pl.BlockSpec(w_out.shape, const),
            pl.BlockSpec((1, d), const),
            pl.BlockSpec(wr_cat.shape, const),
        ],
        out_specs=[
            pl.BlockSpec((tb, d), row),
            pl.BlockSpec((tb, d // 2), row),
            pl.BlockSpec((tb, LANES), row),
            pl.BlockSpec((8, tb), lambda i: (0, i)),
            pl.BlockSpec((N_EXPERTS, 1), const),
        ],
        out_shape=[
            jax.ShapeDtypeStruct((t, d), F32),
            jax.ShapeDtypeStruct((t, d // 2), jnp.int32),
            jax.ShapeDtypeStruct((t, LANES), F32),
            jax.ShapeDtypeStruct((8, t), F32),
            jax.ShapeDtypeStruct((N_EXPERTS, 1), F32),
        ],
        compiler_params=pltpu.CompilerParams(
            dimension_semantics=("arbitrary",), vmem_limit_bytes=VMEM_LIMIT),
        name="outproj_router",
    )(x2, yc, at, w_out, g, wr_cat)


def _positions_kernel(offs_ref, rt_ref, pos_ref):
    rt = rt_ref[...]
    ea, eb = rt[0:1, :], rt[1:2, :]
    sa = jnp.zeros_like(ea)
    sb = jnp.zeros_like(eb)
    for e in range(N_EXPERTS):
        start = offs_ref[e].astype(F32)
        sa = jnp.where(ea == e, start, sa)
        sb = jnp.where(eb == e, start, sb)
    pos_ref[0:1, :] = (sa + rt[2:3, :]).astype(jnp.int32)
    pos_ref[1:2, :] = (sb + rt[3:4, :]).astype(jnp.int32)


def _positions(offs, routeT):
    t = routeT.shape[1]
    return pl.pallas_call(
        _positions_kernel,
        grid_spec=pltpu.PrefetchScalarGridSpec(
            num_scalar_prefetch=1, grid=(1,),
            in_specs=[pl.BlockSpec(routeT.shape, lambda i, offs: (0, 0))],
            out_specs=pl.BlockSpec((2, t), lambda i, offs: (0, 0)),
        ),
        out_shape=jax.ShapeDtypeStruct((2, t), jnp.int32),
        name="positions",
    )(offs, routeT)


def _moe_kernel(te_ref, nt_ref, nx_ref, sl_ref, x_ref, wg_hbm, wu_hbm, wd_hbm, y_ref,
                wg_st, wu_st, wd_st, wgb_ref, wub_ref, wdb_ref, sem, *, tm, tiles_per_step):
    last = nt_ref[0] - 1

    def weight_copies(e, s):
        return (pltpu.make_async_copy(wg_hbm.at[e], wg_st.at[s], sem.at[s, 0]),
                pltpu.make_async_copy(wu_hbm.at[e], wu_st.at[s], sem.at[s, 1]),
                pltpu.make_async_copy(wd_hbm.at[e], wd_st.at[s], sem.at[s, 2]))

    for u in range(tiles_per_step):
        i = pl.program_id(0) * tiles_per_step + u
        rows = slice(u * tm, (u + 1) * tm)
        ic = jnp.minimum(i, last)
        expert = te_ref[ic]
        slot = sl_ref[ic]
        first_of_expert = (i == 0) | ((i <= last) & (expert != te_ref[jnp.maximum(ic - 1, 0)]))

        if u == 0:
            @pl.when(i == 0)
            def _():
                for c in weight_copies(expert, slot):
                    c.start()

        @pl.when(first_of_expert)
        def _():
            for c in weight_copies(expert, slot):
                c.wait()
            nxt = nx_ref[ic]

            @pl.when(nxt >= 0)
            def _():
                for c in weight_copies(nxt, 1 - slot):
                    c.start()

            wgb_ref[...] = wg_st[slot].astype(BF16)
            wub_ref[...] = wu_st[slot].astype(BF16)
            wdb_ref[...] = wd_st[slot].astype(BF16)

        @pl.when(i <= last)
        def _():
            x_l, x_r = _unpack_rows(x_ref[rows, :])
            x_l = x_l.astype(BF16)
            x_r = x_r.astype(BF16)
            half = x_l.shape[1]
            hg = _dot(x_l, wgb_ref[0:half, :]) + _dot(x_r, wgb_ref[half:2 * half, :])
            hu = _dot(x_l, wub_ref[0:half, :]) + _dot(x_r, wub_ref[half:2 * half, :])
            act = hg * (1.0 / (1.0 + jnp.exp(-hg))) * hu
            y_ref[rows, :] = _pack_rows(_dot(act.astype(BF16), wdb_ref[...]))


def _moe(tile_expert, n_tiles, next_expert, stage_slot, xs, wg, wu, wd):
    p, dp = xs.shape
    d, f = wg.shape[1], wg.shape[2]
    assert dp * 2 == d
    tm, tps = TM_MOE, MOE_TILES_PER_STEP
    tb = tm * tps
    row = lambda s, te, nt, nx, sl: (jnp.minimum(s, (nt[0] - 1) // tps), 0)
    hbm = pl.BlockSpec(memory_space=pl.ANY)
    grid_spec = pltpu.PrefetchScalarGridSpec(
        num_scalar_prefetch=4,
        grid=(p // tb,),
        in_specs=[pl.BlockSpec((tb, dp), row), hbm, hbm, hbm],
        out_specs=pl.BlockSpec((tb, dp), row),
        scratch_shapes=[pltpu.VMEM((2, d, f), F32), pltpu.VMEM((2, d, f), F32), pltpu.VMEM((2, f, d), F32),
                        pltpu.VMEM((d, f), BF16), pltpu.VMEM((d, f), BF16), pltpu.VMEM((f, d), BF16),
                        pltpu.SemaphoreType.DMA((2, 3))],
    )
    return pl.pallas_call(
        functools.partial(_moe_kernel, tm=tm, tiles_per_step=tps),
        grid_spec=grid_spec,
        out_shape=jax.ShapeDtypeStruct((p, dp), jnp.int32),
        compiler_params=pltpu.CompilerParams(
            dimension_semantics=("arbitrary",), vmem_limit_bytes=VMEM_LIMIT),
        name="moe_experts",
    )(tile_expert, n_tiles, next_expert, stage_slot, xs, wg, wu, wd)


def _sc_mesh():
    return plsc.VectorSubcoreMesh(core_axis_name="c", subcore_axis_name="s",
                                  num_cores=SC_CORES, num_subcores=SC_SUBCORES)


def _sc_dispatch(rows, pos_a, pos_b, n_out):
    t, d = rows.shape
    win = pos_a.shape[1]

    @functools.partial(pl.kernel, out_type=jax.ShapeDtypeStruct((n_out, d), rows.dtype),
                       mesh=_sc_mesh(), scratch_types=[], name="sc_dispatch")
    def run(rows_hbm, pa_hbm, pb_hbm, out_hbm):
        def body(rows_vmem, pa_vmem, pb_vmem):
            pltpu.sync_copy(rows_vmem, out_hbm.at[pa_vmem.at[0]])
            pltpu.sync_copy(rows_vmem, out_hbm.at[pb_vmem.at[0]])

        pltpu.emit_pipeline(
            body, grid=(t // win,),
            in_specs=[pl.BlockSpec((win, d), lambda i: (i, 0)),
                      pl.BlockSpec((1, win), lambda i: (i, 0)),
                      pl.BlockSpec((1, win), lambda i: (i, 0))],
            out_specs=[],
            core_axis_name=("c", "s"),
            dimension_semantics=(pltpu.PARALLEL,),
        )(rows_hbm, pa_hbm, pb_hbm)

    return run(rows, pos_a, pos_b)


def _sc_gather(table, idx):
    d = table.shape[1]
    n_win, win = idx.shape

    @functools.partial(pl.kernel, out_type=jax.ShapeDtypeStruct((n_win * win, d), table.dtype),
                       mesh=_sc_mesh(), scratch_types=[], name="sc_gather")
    def run(table_hbm, idx_hbm, out_hbm):
        def body(idx_vmem, out_vmem):
            pltpu.sync_copy(table_hbm.at[idx_vmem.at[0]], out_vmem)

        pltpu.emit_pipeline(
            body, grid=(n_win,),
            in_specs=[pl.BlockSpec((1, win), lambda i: (i, 0))],
            out_specs=[pl.BlockSpec((win, d), lambda i: (i, 0))],
            core_axis_name=("c", "s"),
            dimension_semantics=(pltpu.PARALLEL,),
        )(idx_hbm, out_hbm)

    return run(table, idx)


def _combine_kernel(h_ref, ya_ref, yb_ref, r_ref, o_ref):
    r = r_ref[...]
    wa, wb = r[:, 4:5], r[:, 5:6]
    a_l, a_r = _unpack_rows(ya_ref[...])
    b_l, b_r = _unpack_rows(yb_ref[...])
    half = a_l.shape[1]
    o_ref[:, 0:half] = h_ref[:, 0:half] + wa * a_l + wb * b_l
    o_ref[:, half:2 * half] = h_ref[:, half:2 * half] + wa * a_r + wb * b_r


def _combine(hres, yg, route):
    t, d = hres.shape
    tm = TM_COMBINE
    nb = t // tm
    return pl.pallas_call(
        _combine_kernel,
        grid=(nb,),
        in_specs=[pl.BlockSpec((tm, d), lambda i: (i, 0)),
                  pl.BlockSpec((tm, d // 2), lambda i: (i, 0)),
                  pl.BlockSpec((tm, d // 2), lambda i: (i + nb, 0)),
                  pl.BlockSpec((tm, LANES), lambda i: (i, 0))],
        out_specs=pl.BlockSpec((tm, d), lambda i: (i, 0)),
        out_shape=jax.ShapeDtypeStruct((t, d), F32),
        compiler_params=pltpu.CompilerParams(
            dimension_semantics=("arbitrary",), vmem_limit_bytes=VMEM_LIMIT),
        name="combine",
    )(hres, yg, yg, route)


def _lambda_init(layer_idx):
    return 0.8 - 0.6 * math.exp(-0.3 * layer_idx)


def _layer(h, l, attn_norm_g, w_in, conv_w, conv_out_g, q_norm_g, k_norm_g,
           lambda_q1, lambda_k1, lambda_q2, lambda_k2, attn_subln_g, w_out,
           ffn_norm_g, w_router_group, w_router_expert, w_exp_gate, w_exp_up, w_exp_down):
    batch, seq, d = h.shape
    t = batch * seq
    dc = conv_w.shape[-1]
    lam_init = _lambda_init(l)
    x2 = h.reshape(t, d)

    reps = dc // HEAD_DIM
    assert dc // CONV_GROUPS == HEAD_DIM
    qg = (jnp.tile(q_norm_g[l], reps) * (HEAD_DIM ** -0.5 * math.log2(math.e))).reshape(1, dc)
    kg = jnp.tile(k_norm_g[l], reps).reshape(1, dc)
    grp = jnp.arange(MXU_TILE) // HEAD_DIM
    gmat = jnp.where(grp[:, None] == grp[None, :], 1.0 / HEAD_DIM, 0.0).astype(BF16)
    yc, qT, k, vT = _inproj(x2, attn_norm_g[l].reshape(1, d), w_in[l].astype(BF16), conv_w[l],
                          conv_out_g[l].reshape(1, dc), qg, kg, gmat, batch=batch, seq=seq)

    lam_params = jnp.stack([lambda_q1[l], lambda_k1[l], lambda_q2[l], lambda_k2[l]])
    at = _attention(lam_params, attn_subln_g[l].reshape(V_DIM, 1), qT, k, vT,
                    batch=batch, seq=seq, lam_init=lam_init)

    wr = jnp.concatenate([w_router_expert[l], w_router_group[l],
                          jnp.zeros((d, LANES - N_EXPERTS - N_GROUPS), F32)], axis=1)
    wr_hi = wr.astype(BF16)
    wr_lo = (wr - wr_hi.astype(F32)).astype(BF16)
    hres, hn2, route, routeT, cnt = _outproj_router(x2, yc, at, w_out[l].astype(BF16),
                                                    ffn_norm_g[l].reshape(1, d),
                                                    jnp.concatenate([wr_hi, wr_lo], axis=1))

    tmm = TM_MOE
    n_tiles_max = (2 * t) // tmm + N_EXPERTS
    p_rows = n_tiles_max * tmm
    counts = cnt[:, 0].astype(jnp.int32)
    tiles = (counts + tmm - 1) // tmm
    tile_end = jnp.cumsum(tiles)
    offs = (tile_end - tiles) * tmm
    pos = _positions(offs, routeT)
    n_tiles = tile_end[-1:].astype(jnp.int32)
    tile_ids = jnp.arange(n_tiles_max, dtype=jnp.int32)
    tile_expert = jnp.minimum(
        jnp.sum((tile_end[None, :] <= tile_ids[:, None]).astype(jnp.int32), axis=1), N_EXPERTS - 1)
    e_ids = jnp.arange(N_EXPERTS, dtype=jnp.int32)
    nonempty = tiles > 0
    later = jnp.where(nonempty[None, :] & (e_ids[None, :] > e_ids[:, None]), e_ids[None, :], N_EXPERTS)
    next_of = jnp.min(later, axis=1)
    next_of = jnp.where(next_of == N_EXPERTS, -1, next_of).astype(jnp.int32)
    slot_of = ((jnp.cumsum(nonempty.astype(jnp.int32)) - 1) % 2).astype(jnp.int32)
    onehot_te = (tile_expert[:, None] == e_ids[None, :]).astype(jnp.int32)
    next_expert = jnp.sum(onehot_te * next_of[None, :], axis=1)
    stage_slot = jnp.sum(onehot_te * slot_of[None, :], axis=1)

    posw = pos.reshape(2 * t // SC_WIN, SC_WIN)
    pos1w = posw[:t // SC_WIN]
    pos2w = posw[t // SC_WIN:]
    xs = _sc_dispatch(hn2, pos1w, pos2w, p_rows)

    f = w_exp_gate.shape[-1]
    ys = _moe(tile_expert, n_tiles, next_expert, stage_slot, xs,
              w_exp_gate[l].reshape(N_EXPERTS, d, f),
              w_exp_up[l].reshape(N_EXPERTS, d, f),
              w_exp_down[l].reshape(N_EXPERTS, f, d))
    yg = _sc_gather(ys, posw)
    out = _combine(hres, yg, route)
    return out.reshape(batch, seq, d)


def kernel(x, attn_norm_g, w_in, conv_w, conv_out_g, q_norm_g, k_norm_g, lambda_q1, lambda_k1,
           lambda_q2, lambda_k2, attn_subln_g, w_out, ffn_norm_g, w_router_group, w_router_expert,
           w_exp_gate, w_exp_up, w_exp_down):
    h = x
    for l in range(attn_norm_g.shape[0]):
        h = _layer(h, l, attn_norm_g, w_in, conv_w, conv_out_g, q_norm_g, k_norm_g,
                   lambda_q1, lambda_k1, lambda_q2, lambda_k2, attn_subln_g, w_out,
                   ffn_norm_g, w_router_group, w_router_expert, w_exp_gate, w_exp_up, w_exp_down)
    return h
```

```python
import functools
import math

import jax
import jax.numpy as jnp
from jax import lax
from jax.experimental import pallas as pl
from jax.experimental.pallas import tpu as pltpu
from jax.experimental.pallas import tpu_sc as plsc

F32 = jnp.float32
BF16 = jnp.bfloat16

HEAD_DIM = 64
V_DIM = 2 * HEAD_DIM
CONV_GROUPS = 8
N_GROUPS = 4
EXPERTS_PER_GROUP = 8
N_EXPERTS = N_GROUPS * EXPERTS_PER_GROUP
EPS = 1e-6
LANES = 128
MXU_TILE = 256
ONES_ROWS = 16
VMEM_LIMIT = 48 * 1024 * 1024

TM_PROJ = 256
TM_COMBINE = 1024
OUTPROJ_SUBTILES = 4
INPROJ_SUBTILES = 2
TQ = 512
TK = 512
TM_MOE = 512
SC_CORES = 2
SC_SUBCORES = 16
SC_WIN = 64


def _dot(a, b):
    return jnp.dot(a, b, preferred_element_type=F32)


def _pack_rows(x):
    w = x.shape[1] // 2
    bits = lax.bitcast_convert_type(x.astype(BF16).astype(F32), jnp.uint32)
    return lax.bitcast_convert_type((bits[:, :w] >> 16) | bits[:, w:], jnp.int32)


def _unpack_rows(packed):
    bits = lax.bitcast_convert_type(packed, jnp.uint32)
    left = lax.bitcast_convert_type(bits << 16, F32)
    right = lax.bitcast_convert_type(bits & jnp.uint32(0xFFFF0000), F32)
    return left, right


def _group_mean(sq, gm):
    w = gm.shape[0]
    sq = sq.astype(BF16)
    return jnp.concatenate([_dot(sq[:, c:c + w], gm) for c in range(0, sq.shape[1], w)], axis=1)


def _inproj_kernel(x_ref, g_ref, w_ref, cw_ref, cg_ref, qg_ref, kg_ref, gm_ref,
                   yc_ref, qT_ref, k_ref, vT_ref, carry_ref, *, tm, dc):
    j = pl.program_id(1)

    @pl.when(j == 0)
    def _():
        carry_ref[...] = jnp.zeros_like(carry_ref)

    for u in range(x_ref.shape[0] // tm):
        _inproj_subtile(u, x_ref, g_ref, w_ref, cw_ref, cg_ref, qg_ref, kg_ref, gm_ref,
                        yc_ref, qT_ref, k_ref, vT_ref, carry_ref, tm=tm, dc=dc)


def _inproj_subtile(u, x_ref, g_ref, w_ref, cw_ref, cg_ref, qg_ref, kg_ref, gm_ref,
                    yc_ref, qT_ref, k_ref, vT_ref, carry_ref, *, tm, dc):
    rows = slice(u * tm, (u + 1) * tm)
    x = x_ref[rows, :]
    ms = jnp.mean(x * x, axis=-1, keepdims=True)
    hn = (x * lax.rsqrt(ms + EPS) * g_ref[...]).astype(BF16)

    def proj(s):
        return _dot(hn, w_ref[:, s * dc:(s + 1) * dc])

    gm = gm_ref[...]

    c = proj(2) * proj(0)
    prev = carry_ref[...]
    r = lax.broadcasted_iota(jnp.int32, c.shape, 0)
    c1 = jnp.where(r == 0, prev[7:8, :], pltpu.roll(c, 1, 0))
    c2 = jnp.where(r == 0, prev[6:7, :], jnp.where(r == 1, prev[7:8, :], pltpu.roll(c, 2, 0)))
    carry_ref[...] = c[tm - 8:tm, :]
    cw = cw_ref[...]
    y = proj(1) * (cw[0:1, :] * c2 + cw[1:2, :] * c1 + cw[2:3, :] * c)
    yc_ref[rows, :] = (y * lax.rsqrt(_group_mean(y * y, gm) + EPS) * cg_ref[...]).astype(BF16)

    q = proj(3)
    qT_ref[:, rows] = (q * lax.rsqrt(_group_mean(q * q, gm) + EPS) * qg_ref[...]).T.astype(BF16)
    k = proj(4)
    k_ref[rows, :] = (k * lax.rsqrt(_group_mean(k * k, gm) + EPS) * kg_ref[...]).astype(BF16)
    vt = proj(5).T.astype(BF16)
    vrows = V_DIM + ONES_ROWS
    for h in range(dc // V_DIM):
        vT_ref[u, h * vrows:h * vrows + V_DIM, :] = vt[h * V_DIM:(h + 1) * V_DIM, :]
        vT_ref[u, h * vrows + V_DIM:(h + 1) * vrows, :] = jnp.ones((ONES_ROWS, tm), BF16)


def _inproj(x2, g, w_in, conv_w, conv_g, qg, kg, gmat, *, batch, seq):
    t, d = x2.shape
    dc = conv_g.shape[1]
    dv = dc // V_DIM * (V_DIM + ONES_ROWS)
    tm = TK
    tb = tm * INPROJ_SUBTILES
    nj = seq // tb
    row = lambda b, j: (b * nj + j, 0)
    const = lambda b, j: (0, 0)
    out_sds = jax.ShapeDtypeStruct((t, dc), BF16)
    return pl.pallas_call(
        functools.partial(_inproj_kernel, tm=tm, dc=dc),
        grid=(batch, nj),
        in_specs=[
            pl.BlockSpec((tb, d), row),
            pl.BlockSpec((1, d), const),
            pl.BlockSpec(w_in.shape, const),
            pl.BlockSpec(conv_w.shape, const),
            pl.BlockSpec((1, dc), const),
            pl.BlockSpec((1, dc), const),
            pl.BlockSpec((1, dc), const),
            pl.BlockSpec(gmat.shape, const),
        ],
        out_specs=[
            pl.BlockSpec((tb, dc), row),
            pl.BlockSpec((dc, tb), lambda b, j: (0, b * nj + j)),
            pl.BlockSpec((tb, dc), row),
            pl.BlockSpec((INPROJ_SUBTILES, dv, tm), lambda b, j: (b * nj + j, 0, 0)),
        ],
        out_shape=[out_sds, jax.ShapeDtypeStruct((dc, t), BF16), out_sds,
                   jax.ShapeDtypeStruct((t // tm, dv, tm), BF16)],
        scratch_shapes=[pltpu.VMEM((8, dc), F32)],
        compiler_params=pltpu.CompilerParams(
            dimension_semantics=("arbitrary", "arbitrary"), vmem_limit_bytes=VMEM_LIMIT),
        name="inproj_conv_qknorm",
    )(x2, g, w_in, conv_w, conv_g, qg, kg, gmat)


def _attn_kernel(lp_ref, sg_ref, qT_ref, k_ref, vT_ref, o_ref, m_ref, acc_ref,
                 sa_ref, pb_ref, ab_ref, *, tq, tk, nq, lam_init):
    map_a = slice(0, tq)
    map_b = slice(tq, 2 * tq)

    lp = lp_ref[...]
    lam = (jnp.exp(jnp.sum(lp[0:1, :] * lp[1:2, :], axis=-1, keepdims=True))
           - jnp.exp(jnp.sum(lp[2:3, :] * lp[3:4, :], axis=-1, keepdims=True)) + lam_init)

    def stacked_queries(qi):
        qT = qT_ref[:, qi * tq:(qi + 1) * tq]
        row = lax.broadcasted_iota(jnp.int32, qT.shape, 0)
        zero = jnp.zeros_like(qT)
        return jnp.concatenate([jnp.where(row < HEAD_DIM, qT, zero),
                                jnp.where(row >= HEAD_DIM, qT, zero)], axis=1)

    def pieces(diag):
        return ((slice(0, tq // 2), tk // 2), (slice(tq // 2, tq), tk)) if diag else ((slice(0, tq), tk),)

    def shifted(cols, off):
        return slice(cols.start + off, cols.stop + off)

    def scores(j, qq, cols, n_keys):
        return _dot(k_ref[j * tk:j * tk + n_keys, :], qq[:, cols])

    def softmax(m, cols, s, q0):
        if q0 is not None:
            qrel = q0 + lax.broadcasted_iota(jnp.int32, s.shape, 1)
            krel = lax.broadcasted_iota(jnp.int32, s.shape, 0)
            s = jnp.where(krel <= qrel, s, -jnp.inf)
        sb = s.astype(BF16)
        m_old = m[:, cols]
        m_new = jnp.maximum(m_old, jnp.max(sb, axis=0, keepdims=True).astype(F32))
        alpha = jnp.exp2(m_old - m_new)
        p = jnp.exp2(sb - m_new.astype(BF16))
        m[:, cols] = m_new
        return alpha, p

    def accumulate(acc, cols, alpha, vb, p):
        acc[:, cols] = alpha * acc[:, cols] + _dot(vb, p)

    def prefetch_map_a(sa, j, qq, diag):
        for cols, n_keys in pieces(diag):
            sa[0:n_keys, cols] = scores(j, qq, cols, n_keys)

    stacked = {0: stacked_queries(0)}
    m_ref[0] = jnp.full(m_ref.shape[1:], -jnp.inf, F32)
    acc_ref[0] = jnp.zeros(acc_ref.shape[1:], F32)
    prefetch_map_a(sa_ref.at[0], 0, stacked[0], True)

    for qi in range(nq):
        par = qi % 2
        m, acc, sa, qq = m_ref.at[par], acc_ref.at[par], sa_ref.at[par], stacked[qi]
        for j in range(qi + 1):
            last = j == qi
            if j > 0:
                accumulate(acc, map_b, ab_ref[...], vT_ref[j - 1], pb_ref[...])
            s_b = [scores(j, qq, shifted(cols, tq), n_keys) for cols, n_keys in pieces(last)]
            for cols, n_keys in pieces(last):
                alpha_a, p_a = softmax(m, cols, sa[0:n_keys, cols], cols.start if last else None)
                accumulate(acc, cols, alpha_a, vT_ref[j, :, 0:n_keys], p_a)
            if not last:
                prefetch_map_a(sa, j + 1, qq, j + 1 == qi)
            elif qi + 1 < nq:
                stacked[qi + 1] = stacked_queries(qi + 1)
                m_ref[1 - par] = jnp.full(m_ref.shape[1:], -jnp.inf, F32)
                acc_ref[1 - par] = jnp.zeros(acc_ref.shape[1:], F32)
                prefetch_map_a(sa_ref.at[1 - par], 0, stacked[qi + 1], False)
            for (cols, n_keys), s in zip(pieces(last), s_b):
                alpha_b, p_b = softmax(m, shifted(cols, tq), s, cols.start if last else None)
                if last:
                    accumulate(acc, shifted(cols, tq), alpha_b, vT_ref[j, :, 0:n_keys], p_b)
                else:
                    ab_ref[...] = alpha_b
                    pb_ref[...] = p_b

        o = acc[0:V_DIM, :] / acc[V_DIM:V_DIM + 1, :]
        d = o[:, map_a] - lam * o[:, map_b]
        ms = jnp.mean(d * d, axis=0, keepdims=True)
        o_ref[qi * tq:(qi + 1) * tq, :] = (
            d * lax.rsqrt(ms + EPS) * sg_ref[...] * (1.0 - lam_init)).T.astype(BF16)
        del stacked[qi]


def _attention(lam_params, subln_g, qT, k, vT, *, batch, seq, lam_init):
    dq, t = qT.shape
    n_heads = dq // V_DIM
    tq, tk = TQ, TK
    assert tq == tk and vT.shape[2] == tk
    nq = seq // tq
    nk = seq // tk
    vrows = V_DIM + ONES_ROWS
    const = lambda b, h: (0, 0)
    return pl.pallas_call(
        functools.partial(_attn_kernel, tq=tq, tk=tk, nq=nq, lam_init=lam_init),
        grid=(batch, n_heads),
        in_specs=[
            pl.BlockSpec(lam_params.shape, const),
            pl.BlockSpec((V_DIM, 1), const),
            pl.BlockSpec((V_DIM, seq), lambda b, h: (h, b)),
            pl.BlockSpec((seq, V_DIM), lambda b, h: (b, h)),
            pl.BlockSpec((nk, vrows, tk), lambda b, h: (b, h, 0)),
        ],
        out_specs=pl.BlockSpec((seq, V_DIM), lambda b, h: (b, h)),
        out_shape=jax.ShapeDtypeStruct((t, dq), BF16),
        scratch_shapes=[pltpu.VMEM((2, 1, 2 * tq), F32),
                        pltpu.VMEM((2, vrows, 2 * tq), F32),
                        pltpu.VMEM((2, tk, tq), F32), pltpu.VMEM((tk, tq), BF16), pltpu.VMEM((1, tq), F32)],
        compiler_params=pltpu.CompilerParams(
            dimension_semantics=("arbitrary", "arbitrary"), vmem_limit_bytes=VMEM_LIMIT),
        name="diff_attention",
    )(lam_params, subln_g, qT, k, vT)


def _outproj_router_kernel(x_ref, yc_ref, at_ref, wo_ref, g_ref, wr_ref,
                           h_ref, hn_ref, route_ref, routeT_ref, cnt_ref, *, tm, dc):
    i = pl.program_id(0)

    @pl.when(i == 0)
    def _():
        cnt_ref[...] = jnp.zeros_like(cnt_ref)

    for r0 in range(0, x_ref.shape[0], tm):
        _outproj_router_subtile(slice(r0, r0 + tm), x_ref, yc_ref, at_ref, wo_ref, g_ref, wr_ref,
                                h_ref, hn_ref, route_ref, routeT_ref, cnt_ref, tm=tm, dc=dc)


def _outproj_router_subtile(rows, x_ref, yc_ref, at_ref, wo_ref, g_ref, wr_ref,
                            h_ref, hn_ref, route_ref, routeT_ref, cnt_ref, *, tm, dc):
    h = x_ref[rows, :] + _dot(yc_ref[rows, :], wo_ref[0:dc, :]) + _dot(at_ref[rows, :], wo_ref[dc:2 * dc, :])
    h_ref[rows, :] = h
    ms = jnp.mean(h * h, axis=-1, keepdims=True)
    hn = h * lax.rsqrt(ms + EPS) * g_ref[...]
    hi = hn.astype(BF16)
    hn_ref[rows, :] = _pack_rows(hn)
    lo = (hn - hi.astype(F32)).astype(BF16)
    prod = _dot(hi, wr_ref[...])
    logits = prod[:, 0:LANES] + prod[:, LANES:2 * LANES] + _dot(lo, wr_ref[:, 0:LANES])

    lt = logits.T
    neg = -jnp.inf
    grow = lax.broadcasted_iota(jnp.int32, (8, tm), 0).astype(F32)
    gl = jnp.where(grow < N_GROUPS, lt[N_EXPERTS:N_EXPERTS + 8, :], neg)
    gmax = jnp.max(gl, axis=0, keepdims=True)
    g_gate = 1.0 / jnp.sum(jnp.exp(gl - gmax), axis=0, keepdims=True)
    g_idx = jnp.min(jnp.where(gl == gmax, grow, 1e9), axis=0, keepdims=True)
    erow = lax.broadcasted_iota(jnp.int32, (N_EXPERTS, tm), 0).astype(F32)
    e_lo = g_idx * EXPERTS_PER_GROUP
    el = jnp.where((erow >= e_lo) & (erow < e_lo + EXPERTS_PER_GROUP), lt[0:N_EXPERTS, :], neg)
    v1 = jnp.max(el, axis=0, keepdims=True)
    i1 = jnp.min(jnp.where(el == v1, erow, 1e9), axis=0, keepdims=True)
    el2 = jnp.where(erow == i1, neg, el)
    v2 = jnp.max(el2, axis=0, keepdims=True)
    i2 = jnp.min(jnp.where(el2 == v2, erow, 1e9), axis=0, keepdims=True)
    tt = jnp.exp(v2 - v1)
    w1 = g_gate / (1.0 + tt)
    w2 = g_gate * tt / (1.0 + tt)

    sel1 = erow == i1
    sel2 = erow == i2
    oh = jnp.where(sel1 | sel2, 1.0, 0.0)
    ss = lax.broadcasted_iota(jnp.int32, (tm, tm), 0)
    tt_i = lax.broadcasted_iota(jnp.int32, (tm, tm), 1)
    earlier = jnp.where(ss < tt_i, 1.0, 0.0).astype(BF16)
    ranks = _dot(oh.astype(BF16), earlier) + cnt_ref[...]
    r1 = jnp.sum(jnp.where(sel1, ranks, 0.0), axis=0, keepdims=True)
    r2 = jnp.sum(jnp.where(sel2, ranks, 0.0), axis=0, keepdims=True)
    cnt_ref[...] = cnt_ref[...] + jnp.sum(oh, axis=1, keepdims=True)

    routeT = jnp.concatenate([i1, i2, r1, r2, w1, w2, jnp.zeros((2, tm), F32)], axis=0)
    routeT_ref[:, rows] = routeT
    route_ref[rows, :] = jnp.concatenate([routeT, jnp.zeros((LANES - 8, tm), F32)], axis=0).T


def _outproj_router(x2, yc, at, w_out, g, wr_cat):
    t, d = x2.shape
    dc = yc.shape[1]
    tm = TM_PROJ
    tb = tm * OUTPROJ_SUBTILES
    row = lambda i: (i, 0)
    const = lambda i: (0, 0)
    return pl.pallas_call(
        functools.partial(_outproj_router_kernel, tm=tm, dc=dc),
        grid=(t // tb,),
        in_specs=[
            pl.BlockSpec((tb, d), row),
            pl.BlockSpec((tb, dc), row),
            pl.BlockSpec((tb, dc), row),
            pl.BlockSpec(w_out.shape, const),
            pl.BlockSpec((1, d), const),
            pl.BlockSpec(wr_cat.shape, const),
        ],
        out_specs=[
            pl.BlockSpec((tb, d), row),
            pl.BlockSpec((tb, d // 2), row),
            pl.BlockSpec((tb, LANES), row),
            pl.BlockSpec((8, tb), lambda i: (0, i)),
            pl.BlockSpec((N_EXPERTS, 1), const),
        ],
        out_shape=[
            jax.ShapeDtypeStruct((t, d), F32),
            jax.ShapeDtypeStruct((t, d // 2), jnp.int32),
            jax.ShapeDtypeStruct((t, LANES), F32),
            jax.ShapeDtypeStruct((8, t), F32),
            jax.ShapeDtypeStruct((N_EXPERTS, 1), F32),
        ],
        compiler_params=pltpu.CompilerParams(
            dimension_semantics=("arbitrary",), vmem_limit_bytes=VMEM_LIMIT),
        name="outproj_router",
    )(x2, yc, at, w_out, g, wr_cat)


def _positions_kernel(xoffs_ref, yoffs_ref, rt_ref, pos_ref):
    rt = rt_ref[...]
    ea, eb = rt[0:1, :], rt[1:2, :]
    zero = jnp.zeros_like(ea)
    xa, xb, ya, yb = zero, zero, zero, zero
    for e in range(N_EXPERTS):
        xs, ys = xoffs_ref[e].astype(F32), yoffs_ref[e].astype(F32)
        xa = jnp.where(ea == e, xs, xa)
        xb = jnp.where(eb == e, xs, xb)
        ya = jnp.where(ea == e, ys, ya)
        yb = jnp.where(eb == e, ys, yb)
    pos_ref[0:1, :] = (xa + rt[2:3, :]).astype(jnp.int32)
    pos_ref[1:2, :] = (xb + rt[3:4, :]).astype(jnp.int32)
    pos_ref[2:3, :] = (ya + rt[2:3, :]).astype(jnp.int32)
    pos_ref[3:4, :] = (yb + rt[3:4, :]).astype(jnp.int32)


def _positions(xoffs, yoffs, routeT):
    t = routeT.shape[1]
    return pl.pallas_call(
        _positions_kernel,
        grid_spec=pltpu.PrefetchScalarGridSpec(
            num_scalar_prefetch=2, grid=(1,),
            in_specs=[pl.BlockSpec(routeT.shape, lambda i, xo, yo: (0, 0))],
            out_specs=pl.BlockSpec((4, t), lambda i, xo, yo: (0, 0)),
        ),
        out_shape=jax.ShapeDtypeStruct((4, t), jnp.int32),
        name="positions",
    )(xoffs, yoffs, routeT)


def _moe_kernel(te_ref, nt_ref, nx_ref, sl_ref, hb_ref, nh_ref, xa_ref, xb_ref, wg_hbm, wu_hbm, wd_hbm,
                y_ref, wg_st, wu_st, wd_st, wgb_ref, wub_ref, wdb_ref, sem):
    i = pl.program_id(0)
    last = nt_ref[0] - 1
    ic = jnp.minimum(i, last)
    expert = te_ref[ic]
    slot = sl_ref[ic]
    first_of_expert = (i == 0) | ((i <= last) & (expert != te_ref[jnp.maximum(ic - 1, 0)]))

    def weight_copies(e, s):
        return (pltpu.make_async_copy(wg_hbm.at[e], wg_st.at[s], sem.at[s, 0]),
                pltpu.make_async_copy(wu_hbm.at[e], wu_st.at[s], sem.at[s, 1]),
                pltpu.make_async_copy(wd_hbm.at[e], wd_st.at[s], sem.at[s, 2]))

    @pl.when(i == 0)
    def _():
        for c in weight_copies(expert, slot):
            c.start()

    @pl.when(first_of_expert)
    def _():
        for c in weight_copies(expert, slot):
            c.wait()
        nxt = nx_ref[ic]

        @pl.when(nxt >= 0)
        def _():
            for c in weight_copies(nxt, 1 - slot):
                c.start()

        wgb_ref[...] = wg_st[slot].astype(BF16)
        wub_ref[...] = wu_st[slot].astype(BF16)
        wdb_ref[...] = wd_st[slot].astype(BF16)

    def expert_mlp(x_packed):
        x_l, x_r = _unpack_rows(x_packed)
        x_l = x_l.astype(BF16)
        x_r = x_r.astype(BF16)
        half = x_l.shape[1]
        hg = _dot(x_l, wgb_ref[0:half, :]) + _dot(x_r, wgb_ref[half:2 * half, :])
        hu = _dot(x_l, wub_ref[0:half, :]) + _dot(x_r, wub_ref[half:2 * half, :])
        act = hg * (1.0 / (1.0 + jnp.exp(-hg))) * hu
        return _pack_rows(_dot(act.astype(BF16), wdb_ref[...]))

    n_half = nh_ref[ic]
    th = xa_ref.shape[0]

    @pl.when((i <= last) & (n_half == 2))
    def _():
        y_ref[...] = expert_mlp(jnp.concatenate([xa_ref[...], xb_ref[...]], axis=0))

    @pl.when((i <= last) & (n_half == 1))
    def _():
        y_ref[0:th, :] = expert_mlp(xa_ref[...])


def _moe(tile_expert, n_tiles, next_expert, stage_slot, half_block, n_halves, xs, wg, wu, wd):
    dp = xs.shape[1]
    d, f = wg.shape[1], wg.shape[2]
    assert dp * 2 == d
    th = TM_MOE // 2
    n_tiles_max = tile_expert.shape[0]

    def tile(s, nt):
        return jnp.minimum(s, nt[0] - 1)

    first = lambda s, te, nt, nx, sl, hb, nh: (hb[tile(s, nt)], 0)
    second = lambda s, te, nt, nx, sl, hb, nh: (hb[tile(s, nt)] + nh[tile(s, nt)] - 1, 0)
    hbm = pl.BlockSpec(memory_space=pl.ANY)
    grid_spec = pltpu.PrefetchScalarGridSpec(
        num_scalar_prefetch=6,
        grid=(n_tiles_max,),
        in_specs=[pl.BlockSpec((th, dp), first), pl.BlockSpec((th, dp), second), hbm, hbm, hbm],
        out_specs=pl.BlockSpec((TM_MOE, dp), lambda s, te, nt, nx, sl, hb, nh: (tile(s, nt), 0)),
        scratch_shapes=[pltpu.VMEM((2, d, f), F32), pltpu.VMEM((2, d, f), F32), pltpu.VMEM((2, f, d), F32),
                        pltpu.VMEM((d, f), BF16), pltpu.VMEM((d, f), BF16), pltpu.VMEM((f, d), BF16),
                        pltpu.SemaphoreType.DMA((2, 3))],
    )
    return pl.pallas_call(
        _moe_kernel,
        grid_spec=grid_spec,
        out_shape=jax.ShapeDtypeStruct((n_tiles_max * TM_MOE, dp), jnp.int32),
        compiler_params=pltpu.CompilerParams(
            dimension_semantics=("arbitrary",), vmem_limit_bytes=VMEM_LIMIT),
        name="moe_experts",
    )(tile_expert, n_tiles, next_expert, stage_slot, half_block, n_halves, xs, xs, wg, wu, wd)


def _sc_mesh():
    return plsc.VectorSubcoreMesh(core_axis_name="c", subcore_axis_name="s",
                                  num_cores=SC_CORES, num_subcores=SC_SUBCORES)


def _sc_dispatch(rows, pos_a, pos_b, n_out):
    t, d = rows.shape
    win = pos_a.shape[1]

    @functools.partial(pl.kernel, out_type=jax.ShapeDtypeStruct((n_out, d), rows.dtype),
                       mesh=_sc_mesh(), scratch_types=[], name="sc_dispatch")
    def run(rows_hbm, pa_hbm, pb_hbm, out_hbm):
        def body(rows_vmem, pa_vmem, pb_vmem):
            pltpu.sync_copy(rows_vmem, out_hbm.at[pa_vmem.at[0]])
            pltpu.sync_copy(rows_vmem, out_hbm.at[pb_vmem.at[0]])

        pltpu.emit_pipeline(
            body, grid=(t // win,),
            in_specs=[pl.BlockSpec((win, d), lambda i: (i, 0)),
                      pl.BlockSpec((1, win), lambda i: (i, 0)),
                      pl.BlockSpec((1, win), lambda i: (i, 0))],
            out_specs=[],
            core_axis_name=("c", "s"),
            dimension_semantics=(pltpu.PARALLEL,),
        )(rows_hbm, pa_hbm, pb_hbm)

    return run(rows, pos_a, pos_b)


def _sc_gather(table, idx):
    d = table.shape[1]
    n_win, win = idx.shape

    @functools.partial(pl.kernel, out_type=jax.ShapeDtypeStruct((n_win * win, d), table.dtype),
                       mesh=_sc_mesh(), scratch_types=[], name="sc_gather")
    def run(table_hbm, idx_hbm, out_hbm):
        def body(idx_vmem, out_vmem):
            pltpu.sync_copy(table_hbm.at[idx_vmem.at[0]], out_vmem)

        pltpu.emit_pipeline(
            body, grid=(n_win,),
            in_specs=[pl.BlockSpec((1, win), lambda i: (i, 0))],
            out_specs=[pl.BlockSpec((win, d), lambda i: (i, 0))],
            core_axis_name=("c", "s"),
            dimension_semantics=(pltpu.PARALLEL,),
        )(idx_hbm, out_hbm)

    return run(table, idx)


def _combine_kernel(h_ref, ya_ref, yb_ref, r_ref, o_ref):
    r = r_ref[...]
    wa, wb = r[:, 4:5], r[:, 5:6]
    a_l, a_r = _unpack_rows(ya_ref[...])
    b_l, b_r = _unpack_rows(yb_ref[...])
    half = a_l.shape[1]
    o_ref[:, 0:half] = h_ref[:, 0:half] + wa * a_l + wb * b_l
    o_ref[:, half:2 * half] = h_ref[:, half:2 * half] + wa * a_r + wb * b_r


def _combine(hres, yg, route):
    t, d = hres.shape
    tm = TM_COMBINE
    nb = t // tm
    return pl.pallas_call(
        _combine_kernel,
        grid=(nb,),
        in_specs=[pl.BlockSpec((tm, d), lambda i: (i, 0)),
                  pl.BlockSpec((tm, d // 2), lambda i: (i, 0)),
                  pl.BlockSpec((tm, d // 2), lambda i: (i + nb, 0)),
                  pl.BlockSpec((tm, LANES), lambda i: (i, 0))],
        out_specs=pl.BlockSpec((tm, d), lambda i: (i, 0)),
        out_shape=jax.ShapeDtypeStruct((t, d), F32),
        compiler_params=pltpu.CompilerParams(
            dimension_semantics=("arbitrary",), vmem_limit_bytes=VMEM_LIMIT),
        name="combine",
    )(hres, yg, yg, route)


def _lambda_init(layer_idx):
    return 0.8 - 0.6 * math.exp(-0.3 * layer_idx)


def _layer(h, l, attn_norm_g, w_in, conv_w, conv_out_g, q_norm_g, k_norm_g,
           lambda_q1, lambda_k1, lambda_q2, lambda_k2, attn_subln_g, w_out,
           ffn_norm_g, w_router_group, w_router_expert, w_exp_gate, w_exp_up, w_exp_down):
    batch, seq, d = h.shape
    t = batch * seq
    dc = conv_w.shape[-1]
    lam_init = _lambda_init(l)
    x2 = h.reshape(t, d)

    reps = dc // HEAD_DIM
    assert dc // CONV_GROUPS == HEAD_DIM
    qg = (jnp.tile(q_norm_g[l], reps) * (HEAD_DIM ** -0.5 * math.log2(math.e))).reshape(1, dc)
    kg = jnp.tile(k_norm_g[l], reps).reshape(1, dc)
    grp = jnp.arange(MXU_TILE) // HEAD_DIM
    gmat = jnp.where(grp[:, None] == grp[None, :], 1.0 / HEAD_DIM, 0.0).astype(BF16)
    yc, qT, k, vT = _inproj(x2, attn_norm_g[l].reshape(1, d), w_in[l].astype(BF16), conv_w[l],
                          conv_out_g[l].reshape(1, dc), qg, kg, gmat, batch=batch, seq=seq)

    lam_params = jnp.stack([lambda_q1[l], lambda_k1[l], lambda_q2[l], lambda_k2[l]])
    at = _attention(lam_params, attn_subln_g[l].reshape(V_DIM, 1), qT, k, vT,
                    batch=batch, seq=seq, lam_init=lam_init)

    wr = jnp.concatenate([w_router_expert[l], w_router_group[l],
                          jnp.zeros((d, LANES - N_EXPERTS - N_GROUPS), F32)], axis=1)
    wr_hi = wr.astype(BF16)
    wr_lo = (wr - wr_hi.astype(F32)).astype(BF16)
    hres, hn2, route, routeT, cnt = _outproj_router(x2, yc, at, w_out[l].astype(BF16),
                                                    ffn_norm_g[l].reshape(1, d),
                                                    jnp.concatenate([wr_hi, wr_lo], axis=1))

    th = TM_MOE // 2
    n_half_max = (2 * t) // th + N_EXPERTS
    n_tiles_max = (n_half_max + N_EXPERTS) // 2
    counts = cnt[:, 0].astype(jnp.int32)
    halves = (counts + th - 1) // th
    half_start = jnp.cumsum(halves) - halves
    tiles = (halves + 1) // 2
    tile_end = jnp.cumsum(tiles)
    tile_start = tile_end - tiles
    pos = _positions(half_start * th, tile_start * TM_MOE, routeT)
    n_tiles = tile_end[-1:].astype(jnp.int32)
    tile_ids = jnp.arange(n_tiles_max, dtype=jnp.int32)
    tile_expert = jnp.minimum(
        jnp.sum((tile_end[None, :] <= tile_ids[:, None]).astype(jnp.int32), axis=1), N_EXPERTS - 1)
    e_ids = jnp.arange(N_EXPERTS, dtype=jnp.int32)
    nonempty = tiles > 0
    later = jnp.where(nonempty[None, :] & (e_ids[None, :] > e_ids[:, None]), e_ids[None, :], N_EXPERTS)
    next_of = jnp.min(later, axis=1)
    next_of = jnp.where(next_of == N_EXPERTS, -1, next_of).astype(jnp.int32)
    slot_of = ((jnp.cumsum(nonempty.astype(jnp.int32)) - 1) % 2).astype(jnp.int32)
    onehot_te = (tile_expert[:, None] == e_ids[None, :]).astype(jnp.int32)
    next_expert = jnp.sum(onehot_te * next_of[None, :], axis=1)
    stage_slot = jnp.sum(onehot_te * slot_of[None, :], axis=1)
    tile_in_expert = tile_ids - jnp.sum(onehot_te * tile_start[None, :], axis=1)
    half_block = jnp.sum(onehot_te * half_start[None, :], axis=1) + 2 * tile_in_expert
    n_halves = jnp.clip(jnp.sum(onehot_te * halves[None, :], axis=1) - 2 * tile_in_expert, 1, 2)

    posw = pos.reshape(4 * t // SC_WIN, SC_WIN)
    nw = t // SC_WIN
    xs = _sc_dispatch(hn2, posw[0:nw], posw[nw:2 * nw], n_half_max * th)

    f = w_exp_gate.shape[-1]
    ys = _moe(tile_expert, n_tiles, next_expert, stage_slot, half_block, n_halves, xs,
              w_exp_gate[l].reshape(N_EXPERTS, d, f),
              w_exp_up[l].reshape(N_EXPERTS, d, f),
              w_exp_down[l].reshape(N_EXPERTS, f, d))
    yg = _sc_gather(ys, posw[2 * nw:4 * nw])
    out = _combine(hres, yg, route)
    return out.reshape(batch, seq, d)


def kernel(x, attn_norm_g, w_in, conv_w, conv_out_g, q_norm_g, k_norm_g, lambda_q1, lambda_k1,
           lambda_q2, lambda_k2, attn_subln_g, w_out, ffn_norm_g, w_router_group, w_router_expert,
           w_exp_gate, w_exp_up, w_exp_down):
    h = x
    for l in range(attn_norm_g.shape[0]):
        h = _layer(h, l, attn_norm_g, w_in, conv_w, conv_out_g, q_norm_g, k_norm_g,
                   lambda_q1, lambda_k1, lambda_q2, lambda_k2, attn_subln_g, w_out,
                   ffn_norm_g, w_router_group, w_router_expert, w_exp_gate, w_exp_up, w_exp_down)
    return h
```

```python
import functools
import math

import jax
import jax.numpy as jnp
from jax import lax
from jax.experimental import pallas as pl
from jax.experimental.pallas import tpu as pltpu
from jax.experimental.pallas import tpu_sc as plsc

F32 = jnp.float32
BF16 = jnp.bfloat16

HEAD_DIM = 64
V_DIM = 2 * HEAD_DIM
CONV_GROUPS = 8
N_GROUPS = 4
EXPERTS_PER_GROUP = 8
N_EXPERTS = N_GROUPS * EXPERTS_PER_GROUP
EPS = 1e-6
LANES = 128
MXU_TILE = 256
ONES_ROWS = 16
VMEM_LIMIT = 48 * 1024 * 1024

TM_PROJ = 256
TM_COMBINE = 1024
OUTPROJ_SUBTILES = 4
INPROJ_SUBTILES = 2
TQ = 512
TK = 512
TM_MOE = 512
SC_CORES = 2
SC_SUBCORES = 16
SC_WIN = 64


def _dot(a, b):
    return jnp.dot(a, b, preferred_element_type=F32)


def _pack_rows(x):
    w = x.shape[1] // 2
    bits = lax.bitcast_convert_type(x.astype(BF16).astype(F32), jnp.uint32)
    return lax.bitcast_convert_type((bits[:, :w] >> 16) | bits[:, w:], jnp.int32)


def _unpack_rows(packed):
    bits = lax.bitcast_convert_type(packed, jnp.uint32)
    left = lax.bitcast_convert_type(bits << 16, F32)
    right = lax.bitcast_convert_type(bits & jnp.uint32(0xFFFF0000), F32)
    return left, right


def _group_mean(sq, gm):
    w = gm.shape[0]
    sq = sq.astype(BF16)
    return jnp.concatenate([_dot(sq[:, c:c + w], gm) for c in range(0, sq.shape[1], w)], axis=1)


def _inproj_kernel(x_ref, g_ref, w_ref, cw_ref, cg_ref, qg_ref, kg_ref, gm_ref,
                   yc_ref, qT_ref, k_ref, vT_ref, carry_ref, *, tm, dc):
    j = pl.program_id(1)

    @pl.when(j == 0)
    def _():
        carry_ref[...] = jnp.zeros_like(carry_ref)

    for u in range(x_ref.shape[0] // tm):
        _inproj_subtile(u, x_ref, g_ref, w_ref, cw_ref, cg_ref, qg_ref, kg_ref, gm_ref,
                        yc_ref, qT_ref, k_ref, vT_ref, carry_ref, tm=tm, dc=dc)


def _inproj_subtile(u, x_ref, g_ref, w_ref, cw_ref, cg_ref, qg_ref, kg_ref, gm_ref,
                    yc_ref, qT_ref, k_ref, vT_ref, carry_ref, *, tm, dc):
    rows = slice(u * tm, (u + 1) * tm)
    x = x_ref[rows, :]
    ms = jnp.mean(x * x, axis=-1, keepdims=True)
    hn = (x * lax.rsqrt(ms + EPS) * g_ref[...]).astype(BF16)

    def proj(s):
        return _dot(hn, w_ref[:, s * dc:(s + 1) * dc])

    gm = gm_ref[...]

    c = proj(2) * proj(0)
    prev = carry_ref[...]
    r = lax.broadcasted_iota(jnp.int32, c.shape, 0)
    c1 = jnp.where(r == 0, prev[7:8, :], pltpu.roll(c, 1, 0))
    c2 = jnp.where(r == 0, prev[6:7, :], jnp.where(r == 1, prev[7:8, :], pltpu.roll(c, 2, 0)))
    carry_ref[...] = c[tm - 8:tm, :]
    cw = cw_ref[...]
    y = proj(1) * (cw[0:1, :] * c2 + cw[1:2, :] * c1 + cw[2:3, :] * c)
    yc_ref[rows, :] = (y * lax.rsqrt(_group_mean(y * y, gm) + EPS) * cg_ref[...]).astype(BF16)

    q = proj(3)
    qT_ref[:, rows] = (q * lax.rsqrt(_group_mean(q * q, gm) + EPS) * qg_ref[...]).T.astype(BF16)
    k = proj(4)
    k_ref[rows, :] = (k * lax.rsqrt(_group_mean(k * k, gm) + EPS) * kg_ref[...]).astype(BF16)
    vt = proj(5).T.astype(BF16)
    vrows = V_DIM + ONES_ROWS
    for h in range(dc // V_DIM):
        vT_ref[u, h * vrows:h * vrows + V_DIM, :] = vt[h * V_DIM:(h + 1) * V_DIM, :]
        vT_ref[u, h * vrows + V_DIM:(h + 1) * vrows, :] = jnp.ones((ONES_ROWS, tm), BF16)


def _inproj(x2, g, w_in, conv_w, conv_g, qg, kg, gmat, *, batch, seq):
    t, d = x2.shape
    dc = conv_g.shape[1]
    dv = dc // V_DIM * (V_DIM + ONES_ROWS)
    tm = TK
    tb = tm * INPROJ_SUBTILES
    nj = seq // tb
    row = lambda b, j: (b * nj + j, 0)
    const = lambda b, j: (0, 0)
    out_sds = jax.ShapeDtypeStruct((t, dc), BF16)
    return pl.pallas_call(
        functools.partial(_inproj_kernel, tm=tm, dc=dc),
        grid=(batch, nj),
        in_specs=[
            pl.BlockSpec((tb, d), row),
            pl.BlockSpec((1, d), const),
            pl.BlockSpec(w_in.shape, const),
            pl.BlockSpec(conv_w.shape, const),
            pl.BlockSpec((1, dc), const),
            pl.BlockSpec((1, dc), const),
            pl.BlockSpec((1, dc), const),
            pl.BlockSpec(gmat.shape, const),
        ],
        out_specs=[
            pl.BlockSpec((tb, dc), row),
            pl.BlockSpec((dc, tb), lambda b, j: (0, b * nj + j)),
            pl.BlockSpec((tb, dc), row),
            pl.BlockSpec((INPROJ_SUBTILES, dv, tm), lambda b, j: (b * nj + j, 0, 0)),
        ],
        out_shape=[out_sds, jax.ShapeDtypeStruct((dc, t), BF16), out_sds,
                   jax.ShapeDtypeStruct((t // tm, dv, tm), BF16)],
        scratch_shapes=[pltpu.VMEM((8, dc), F32)],
        compiler_params=pltpu.CompilerParams(
            dimension_semantics=("arbitrary", "arbitrary"), vmem_limit_bytes=VMEM_LIMIT),
        name="inproj_conv_qknorm",
    )(x2, g, w_in, conv_w, conv_g, qg, kg, gmat)


def _attn_kernel(lp_ref, sg_ref, qT_ref, k_ref, vT_ref, o_ref, m_ref, acc_ref,
                 sa_ref, pb_ref, ab_ref, *, tq, tk, nq, lam_init):
    map_a = slice(0, tq)
    map_b = slice(tq, 2 * tq)

    lp = lp_ref[...]
    lam = (jnp.exp(jnp.sum(lp[0:1, :] * lp[1:2, :], axis=-1, keepdims=True))
           - jnp.exp(jnp.sum(lp[2:3, :] * lp[3:4, :], axis=-1, keepdims=True)) + lam_init)

    def stacked_queries(qi):
        qT = qT_ref[:, qi * tq:(qi + 1) * tq]
        row = lax.broadcasted_iota(jnp.int32, qT.shape, 0)
        zero = jnp.zeros_like(qT)
        return jnp.concatenate([jnp.where(row < HEAD_DIM, qT, zero),
                                jnp.where(row >= HEAD_DIM, qT, zero)], axis=1)

    def pieces(diag):
        return ((slice(0, tq // 2), tk // 2), (slice(tq // 2, tq), tk)) if diag else ((slice(0, tq), tk),)

    def shifted(cols, off):
        return slice(cols.start + off, cols.stop + off)

    def scores(j, qq, cols, n_keys):
        return _dot(k_ref[j * tk:j * tk + n_keys, :], qq[:, cols])

    def softmax(m, cols, s, q0):
        if q0 is not None:
            qrel = q0 + lax.broadcasted_iota(jnp.int32, s.shape, 1)
            krel = lax.broadcasted_iota(jnp.int32, s.shape, 0)
            s = jnp.where(krel <= qrel, s, -jnp.inf)
        sb = s.astype(BF16)
        m_old = m[:, cols]
        m_new = jnp.maximum(m_old, jnp.max(sb, axis=0, keepdims=True).astype(F32))
        alpha = jnp.exp2(m_old - m_new)
        p = jnp.exp2(sb - m_new.astype(BF16))
        m[:, cols] = m_new
        return alpha, p

    def accumulate(acc, cols, alpha, vb, p):
        acc[:, cols] = alpha * acc[:, cols] + _dot(vb, p)

    def prefetch_map_a(sa, j, qq, diag):
        for cols, n_keys in pieces(diag):
            sa[0:n_keys, cols] = scores(j, qq, cols, n_keys)

    stacked = {0: stacked_queries(0)}
    m_ref[0] = jnp.full(m_ref.shape[1:], -jnp.inf, F32)
    acc_ref[0] = jnp.zeros(acc_ref.shape[1:], F32)
    prefetch_map_a(sa_ref.at[0], 0, stacked[0], True)

    for qi in range(nq):
        par = qi % 2
        m, acc, sa, qq = m_ref.at[par], acc_ref.at[par], sa_ref.at[par], stacked[qi]
        for j in range(qi + 1):
            last = j == qi
            if j > 0:
                accumulate(acc, map_b, ab_ref[...], vT_ref[j - 1], pb_ref[...])
            s_b = [scores(j, qq, shifted(cols, tq), n_keys) for cols, n_keys in pieces(last)]
            for cols, n_keys in pieces(last):
                alpha_a, p_a = softmax(m, cols, sa[0:n_keys, cols], cols.start if last else None)
                accumulate(acc, cols, alpha_a, vT_ref[j, :, 0:n_keys], p_a)
            if not last:
                prefetch_map_a(sa, j + 1, qq, j + 1 == qi)
            elif qi + 1 < nq:
                stacked[qi + 1] = stacked_queries(qi + 1)
                m_ref[1 - par] = jnp.full(m_ref.shape[1:], -jnp.inf, F32)
                acc_ref[1 - par] = jnp.zeros(acc_ref.shape[1:], F32)
                prefetch_map_a(sa_ref.at[1 - par], 0, stacked[qi + 1], False)
            for (cols, n_keys), s in zip(pieces(last), s_b):
                alpha_b, p_b = softmax(m, shifted(cols, tq), s, cols.start if last else None)
                if last:
                    accumulate(acc, shifted(cols, tq), alpha_b, vT_ref[j, :, 0:n_keys], p_b)
                else:
                    ab_ref[...] = alpha_b
                    pb_ref[...] = p_b

        o = acc[0:V_DIM, :] / acc[V_DIM:V_DIM + 1, :]
        d = o[:, map_a] - lam * o[:, map_b]
        ms = jnp.mean(d * d, axis=0, keepdims=True)
        o_ref[qi * tq:(qi + 1) * tq, :] = (
            d * lax.rsqrt(ms + EPS) * sg_ref[...] * (1.0 - lam_init)).T.astype(BF16)
        del stacked[qi]


def _attention(lam_params, subln_g, qT, k, vT, *, batch, seq, lam_init):
    dq, t = qT.shape
    n_heads = dq // V_DIM
    tq, tk = TQ, TK
    assert tq == tk and vT.shape[2] == tk
    nq = seq // tq
    nk = seq // tk
    vrows = V_DIM + ONES_ROWS
    const = lambda b, h: (0, 0)
    return pl.pallas_call(
        functools.partial(_attn_kernel, tq=tq, tk=tk, nq=nq, lam_init=lam_init),
        grid=(batch, n_heads),
        in_specs=[
            pl.BlockSpec(lam_params.shape, const),
            pl.BlockSpec((V_DIM, 1), const),
            pl.BlockSpec((V_DIM, seq), lambda b, h: (h, b)),
            pl.BlockSpec((seq, V_DIM), lambda b, h: (b, h)),
            pl.BlockSpec((nk, vrows, tk), lambda b, h: (b, h, 0)),
        ],
        out_specs=pl.BlockSpec((seq, V_DIM), lambda b, h: (b, h)),
        out_shape=jax.ShapeDtypeStruct((t, dq), BF16),
        scratch_shapes=[pltpu.VMEM((2, 1, 2 * tq), F32),
                        pltpu.VMEM((2, vrows, 2 * tq), F32),
                        pltpu.VMEM((2, tk, tq), F32), pltpu.VMEM((tk, tq), BF16), pltpu.VMEM((1, tq), F32)],
        compiler_params=pltpu.CompilerParams(
            dimension_semantics=("arbitrary", "arbitrary"), vmem_limit_bytes=VMEM_LIMIT),
        name="diff_attention",
    )(lam_params, subln_g, qT, k, vT)


def _outproj_router_kernel(x_ref, yc_ref, at_ref, wo_ref, g_ref, wr_ref,
                           h_ref, hn_ref, route_ref, routeT_ref, cnt_ref, *, tm, dc):
    i = pl.program_id(0)

    @pl.when(i == 0)
    def _():
        cnt_ref[...] = jnp.zeros_like(cnt_ref)

    for r0 in range(0, x_ref.shape[0], tm):
        _outproj_router_subtile(slice(r0, r0 + tm), x_ref, yc_ref, at_ref, wo_ref, g_ref, wr_ref,
                                h_ref, hn_ref, route_ref, routeT_ref, cnt_ref, tm=tm, dc=dc)


def _outproj_router_subtile(rows, x_ref, yc_ref, at_ref, wo_ref, g_ref, wr_ref,
                            h_ref, hn_ref, route_ref, routeT_ref, cnt_ref, *, tm, dc):
    h = x_ref[rows, :] + _dot(yc_ref[rows, :], wo_ref[0:dc, :]) + _dot(at_ref[rows, :], wo_ref[dc:2 * dc, :])
    h_ref[rows, :] = h
    ms = jnp.mean(h * h, axis=-1, keepdims=True)
    hn = h * lax.rsqrt(ms + EPS) * g_ref[...]
    hi = hn.astype(BF16)
    hn_ref[rows, :] = _pack_rows(hn)
    lo = (hn - hi.astype(F32)).astype(BF16)
    prod = _dot(hi, wr_ref[...])
    logits = prod[:, 0:LANES] + prod[:, LANES:2 * LANES] + _dot(lo, wr_ref[:, 0:LANES])

    lt = logits.T
    neg = -jnp.inf
    grow = lax.broadcasted_iota(jnp.int32, (8, tm), 0).astype(F32)
    gl = jnp.where(grow < N_GROUPS, lt[N_EXPERTS:N_EXPERTS + 8, :], neg)
    gmax = jnp.max(gl, axis=0, keepdims=True)
    g_gate = 1.0 / jnp.sum(jnp.exp(gl - gmax), axis=0, keepdims=True)
    g_idx = jnp.min(jnp.where(gl == gmax, grow, 1e9), axis=0, keepdims=True)
    erow = lax.broadcasted_iota(jnp.int32, (N_EXPERTS, tm), 0).astype(F32)
    e_lo = g_idx * EXPERTS_PER_GROUP
    el = jnp.where((erow >= e_lo) & (erow < e_lo + EXPERTS_PER_GROUP), lt[0:N_EXPERTS, :], neg)
    v1 = jnp.max(el, axis=0, keepdims=True)
    i1 = jnp.min(jnp.where(el == v1, erow, 1e9), axis=0, keepdims=True)
    el2 = jnp.where(erow == i1, neg, el)
    v2 = jnp.max(el2, axis=0, keepdims=True)
    i2 = jnp.min(jnp.where(el2 == v2, erow, 1e9), axis=0, keepdims=True)
    tt = jnp.exp(v2 - v1)
    w1 = g_gate / (1.0 + tt)
    w2 = g_gate * tt / (1.0 + tt)

    sel1 = erow == i1
    sel2 = erow == i2
    oh = jnp.where(sel1 | sel2, 1.0, 0.0)
    ss = lax.broadcasted_iota(jnp.int32, (tm, tm), 0)
    tt_i = lax.broadcasted_iota(jnp.int32, (tm, tm), 1)
    earlier = jnp.where(ss < tt_i, 1.0, 0.0).astype(BF16)
    ranks = _dot(oh.astype(BF16), earlier) + cnt_ref[...]
    r1 = jnp.sum(jnp.where(sel1, ranks, 0.0), axis=0, keepdims=True)
    r2 = jnp.sum(jnp.where(sel2, ranks, 0.0), axis=0, keepdims=True)
    cnt_ref[...] = cnt_ref[...] + jnp.sum(oh, axis=1, keepdims=True)

    routeT = jnp.concatenate([i1, i2, r1, r2, w1, w2, jnp.zeros((2, tm), F32)], axis=0)
    routeT_ref[:, rows] = routeT
    route_ref[rows, :] = jnp.concatenate([routeT, jnp.zeros((LANES - 8, tm), F32)], axis=0).T


def _outproj_router(x2, yc, at, w_out, g, wr_cat):
    t, d = x2.shape
    dc = yc.shape[1]
    tm = TM_PROJ
    tb = tm * OUTPROJ_SUBTILES
    row = lambda i: (i, 0)
    const = lambda i: (0, 0)
    return pl.pallas_call(
        functools.partial(_outproj_router_kernel, tm=tm, dc=dc),
        grid=(t // tb,),
        in_specs=[
            pl.BlockSpec((tb, d), row),
            pl.BlockSpec((tb, dc), row),
            pl.BlockSpec((tb, dc), row),
            pl.BlockSpec(w_out.shape, const),
            pl.BlockSpec((1, d), const),
            pl.BlockSpec(wr_cat.shape, const),
        ],
        out_specs=[
            pl.BlockSpec((tb, d), row),
            pl.BlockSpec((tb, d // 2), row),
            pl.BlockSpec((tb, LANES), row),
            pl.BlockSpec((8, tb), lambda i: (0, i)),
            pl.BlockSpec((N_EXPERTS, 1), const),
        ],
        out_shape=[
            jax.ShapeDtypeStruct((t, d), F32),
            jax.ShapeDtypeStruct((t, d // 2), jnp.int32),
            jax.ShapeDtypeStruct((t, LANES), F32),
            jax.ShapeDtypeStruct((8, t), F32),
            jax.ShapeDtypeStruct((N_EXPERTS, 1), F32),
        ],
        compiler_params=pltpu.CompilerParams(
            dimension_semantics=("arbitrary",), vmem_limit_bytes=VMEM_LIMIT),
        name="outproj_router",
    )(x2, yc, at, w_out, g, wr_cat)


def _positions_kernel(xoffs_ref, yoffs_ref, rt_ref, pos_ref):
    rt = rt_ref[...]
    ea, eb = rt[0:1, :], rt[1:2, :]
    zero = jnp.zeros_like(ea)
    xa, xb, ya, yb = zero, zero, zero, zero
    for e in range(N_EXPERTS):
        xs, ys = xoffs_ref[e].astype(F32), yoffs_ref[e].astype(F32)
        xa = jnp.where(ea == e, xs, xa)
        xb = jnp.where(eb == e, xs, xb)
        ya = jnp.where(ea == e, ys, ya)
        yb = jnp.where(eb == e, ys, yb)
    pos_ref[0:1, :] = (xa + rt[2:3, :]).astype(jnp.int32)
    pos_ref[1:2, :] = (xb + rt[3:4, :]).astype(jnp.int32)
    pos_ref[2:3, :] = (ya + rt[2:3, :]).astype(jnp.int32)
    pos_ref[3:4, :] = (yb + rt[3:4, :]).astype(jnp.int32)


def _positions(xoffs, yoffs, routeT):
    t = routeT.shape[1]
    return pl.pallas_call(
        _positions_kernel,
        grid_spec=pltpu.PrefetchScalarGridSpec(
            num_scalar_prefetch=2, grid=(1,),
            in_specs=[pl.BlockSpec(routeT.shape, lambda i, xo, yo: (0, 0))],
            out_specs=pl.BlockSpec((4, t), lambda i, xo, yo: (0, 0)),
        ),
        out_shape=jax.ShapeDtypeStruct((4, t), jnp.int32),
        name="positions",
    )(xoffs, yoffs, routeT)


def _moe_kernel(te_ref, nt_ref, nx_ref, sl_ref, hb_ref, nh_ref, xa_ref, xb_ref, wg_hbm, wu_hbm, wd_hbm,
                y_ref, wg_st, wu_st, wd_st, wgb_ref, wub_ref, wdb_ref, sem):
    i = pl.program_id(0)
    last = nt_ref[0] - 1
    ic = jnp.minimum(i, last)
    expert = te_ref[ic]
    slot = sl_ref[ic]
    first_of_expert = (i == 0) | ((i <= last) & (expert != te_ref[jnp.maximum(ic - 1, 0)]))

    def weight_copies(e, s):
        return (pltpu.make_async_copy(wg_hbm.at[e], wg_st.at[s], sem.at[s, 0]),
                pltpu.make_async_copy(wu_hbm.at[e], wu_st.at[s], sem.at[s, 1]),
                pltpu.make_async_copy(wd_hbm.at[e], wd_st.at[s], sem.at[s, 2]))

    @pl.when(i == 0)
    def _():
        for c in weight_copies(expert, slot):
            c.start()

    @pl.when(first_of_expert)
    def _():
        for c in weight_copies(expert, slot):
            c.wait()
        nxt = nx_ref[ic]

        @pl.when(nxt >= 0)
        def _():
            for c in weight_copies(nxt, 1 - slot):
                c.start(priority=1)

        wgb_ref[...] = wg_st[slot].astype(BF16)
        wub_ref[...] = wu_st[slot].astype(BF16)
        wdb_ref[...] = wd_st[slot].astype(BF16)

    def expert_mlp(x_packed):
        x_l, x_r = _unpack_rows(x_packed)
        x_l = x_l.astype(BF16)
        x_r = x_r.astype(BF16)
        half = x_l.shape[1]
        hg = _dot(x_l, wgb_ref[0:half, :]) + _dot(x_r, wgb_ref[half:2 * half, :])
        hu = _dot(x_l, wub_ref[0:half, :]) + _dot(x_r, wub_ref[half:2 * half, :])
        act = hg * (1.0 / (1.0 + jnp.exp(-hg))) * hu
        return _pack_rows(_dot(act.astype(BF16), wdb_ref[...]))

    n_half = nh_ref[ic]
    th = xa_ref.shape[0]

    @pl.when((i <= last) & (n_half == 2))
    def _():
        y_ref[...] = expert_mlp(jnp.concatenate([xa_ref[...], xb_ref[...]], axis=0))

    @pl.when((i <= last) & (n_half == 1))
    def _():
        y_ref[0:th, :] = expert_mlp(xa_ref[...])


def _moe(tile_expert, n_tiles, next_expert, stage_slot, half_block, n_halves, xs, wg, wu, wd):
    dp = xs.shape[1]
    d, f = wg.shape[1], wg.shape[2]
    assert dp * 2 == d
    th = TM_MOE // 2
    n_tiles_max = tile_expert.shape[0]

    def tile(s, nt):
        return jnp.minimum(s, nt[0] - 1)

    first = lambda s, te, nt, nx, sl, hb, nh: (hb[tile(s, nt)], 0)
    second = lambda s, te, nt, nx, sl, hb, nh: (hb[tile(s, nt)] + nh[tile(s, nt)] - 1, 0)
    hbm = pl.BlockSpec(memory_space=pl.ANY)
    grid_spec = pltpu.PrefetchScalarGridSpec(
        num_scalar_prefetch=6,
        grid=(n_tiles_max,),
        in_specs=[pl.BlockSpec((th, dp), first), pl.BlockSpec((th, dp), second), hbm, hbm, hbm],
        out_specs=pl.BlockSpec((TM_MOE, dp), lambda s, te, nt, nx, sl, hb, nh: (tile(s, nt), 0)),
        scratch_shapes=[pltpu.VMEM((2, d, f), F32), pltpu.VMEM((2, d, f), F32), pltpu.VMEM((2, f, d), F32),
                        pltpu.VMEM((d, f), BF16), pltpu.VMEM((d, f), BF16), pltpu.VMEM((f, d), BF16),
                        pltpu.SemaphoreType.DMA((2, 3))],
    )
    return pl.pallas_call(
        _moe_kernel,
        grid_spec=grid_spec,
        out_shape=jax.ShapeDtypeStruct((n_tiles_max * TM_MOE, dp), jnp.int32),
        compiler_params=pltpu.CompilerParams(
            dimension_semantics=("arbitrary",), vmem_limit_bytes=VMEM_LIMIT),
        name="moe_experts",
    )(tile_expert, n_tiles, next_expert, stage_slot, half_block, n_halves, xs, xs, wg, wu, wd)


def _sc_mesh():
    return plsc.VectorSubcoreMesh(core_axis_name="c", subcore_axis_name="s",
                                  num_cores=SC_CORES, num_subcores=SC_SUBCORES)


def _sc_dispatch(rows, pos_a, pos_b, n_out):
    t, d = rows.shape
    win = pos_a.shape[1]

    @functools.partial(pl.kernel, out_type=jax.ShapeDtypeStruct((n_out, d), rows.dtype),
                       mesh=_sc_mesh(), scratch_types=[], name="sc_dispatch")
    def run(rows_hbm, pa_hbm, pb_hbm, out_hbm):
        def body(rows_vmem, pa_vmem, pb_vmem):
            pltpu.sync_copy(rows_vmem, out_hbm.at[pa_vmem.at[0]])
            pltpu.sync_copy(rows_vmem, out_hbm.at[pb_vmem.at[0]])

        pltpu.emit_pipeline(
            body, grid=(t // win,),
            in_specs=[pl.BlockSpec((win, d), lambda i: (i, 0)),
                      pl.BlockSpec((1, win), lambda i: (i, 0)),
                      pl.BlockSpec((1, win), lambda i: (i, 0))],
            out_specs=[],
            core_axis_name=("c", "s"),
            dimension_semantics=(pltpu.PARALLEL,),
        )(rows_hbm, pa_hbm, pb_hbm)

    return run(rows, pos_a, pos_b)


def _sc_gather(table, idx):
    d = table.shape[1]
    n_win, win = idx.shape

    @functools.partial(pl.kernel, out_type=jax.ShapeDtypeStruct((n_win * win, d), table.dtype),
                       mesh=_sc_mesh(), scratch_types=[], name="sc_gather")
    def run(table_hbm, idx_hbm, out_hbm):
        def body(idx_vmem, out_vmem):
            pltpu.sync_copy(table_hbm.at[idx_vmem.at[0]], out_vmem)

        pltpu.emit_pipeline(
            body, grid=(n_win,),
            in_specs=[pl.BlockSpec((1, win), lambda i: (i, 0))],
            out_specs=[pl.BlockSpec((win, d), lambda i: (i, 0))],
            core_axis_name=("c", "s"),
            dimension_semantics=(pltpu.PARALLEL,),
        )(idx_hbm, out_hbm)

    return run(table, idx)


def _combine_kernel(h_ref, ya_ref, yb_ref, r_ref, o_ref):
    r = r_ref[...]
    wa, wb = r[:, 4:5], r[:, 5:6]
    a_l, a_r = _unpack_rows(ya_ref[...])
    b_l, b_r = _unpack_rows(yb_ref[...])
    half = a_l.shape[1]
    o_ref[:, 0:half] = h_ref[:, 0:half] + wa * a_l + wb * b_l
    o_ref[:, half:2 * half] = h_ref[:, half:2 * half] + wa * a_r + wb * b_r


def _combine(hres, yg, route):
    t, d = hres.shape
    tm = TM_COMBINE
    nb = t // tm
    return pl.pallas_call(
        _combine_kernel,
        grid=(nb,),
        in_specs=[pl.BlockSpec((tm, d), lambda i: (i, 0)),
                  pl.BlockSpec((tm, d // 2), lambda i: (i, 0)),
                  pl.BlockSpec((tm, d // 2), lambda i: (i + nb, 0)),
                  pl.BlockSpec((tm, LANES), lambda i: (i, 0))],
        out_specs=pl.BlockSpec((tm, d), lambda i: (i, 0)),
        out_shape=jax.ShapeDtypeStruct((t, d), F32),
        compiler_params=pltpu.CompilerParams(
            dimension_semantics=("arbitrary",), vmem_limit_bytes=VMEM_LIMIT),
        name="combine",
    )(hres, yg, yg, route)


def _lambda_init(layer_idx):
    return 0.8 - 0.6 * math.exp(-0.3 * layer_idx)


def _layer(h, l, attn_norm_g, w_in, conv_w, conv_out_g, q_norm_g, k_norm_g,
           lambda_q1, lambda_k1, lambda_q2, lambda_k2, attn_subln_g, w_out,
           ffn_norm_g, w_router_group, w_router_expert, w_exp_gate, w_exp_up, w_exp_down):
    batch, seq, d = h.shape
    t = batch * seq
    dc = conv_w.shape[-1]
    lam_init = _lambda_init(l)
    x2 = h.reshape(t, d)

    reps = dc // HEAD_DIM
    assert dc // CONV_GROUPS == HEAD_DIM
    qg = (jnp.tile(q_norm_g[l], reps) * (HEAD_DIM ** -0.5 * math.log2(math.e))).reshape(1, dc)
    kg = jnp.tile(k_norm_g[l], reps).reshape(1, dc)
    grp = jnp.arange(MXU_TILE) // HEAD_DIM
    gmat = jnp.where(grp[:, None] == grp[None, :], 1.0 / HEAD_DIM, 0.0).astype(BF16)
    yc, qT, k, vT = _inproj(x2, attn_norm_g[l].reshape(1, d), w_in[l].astype(BF16), conv_w[l],
                          conv_out_g[l].reshape(1, dc), qg, kg, gmat, batch=batch, seq=seq)

    lam_params = jnp.stack([lambda_q1[l], lambda_k1[l], lambda_q2[l], lambda_k2[l]])
    at = _attention(lam_params, attn_subln_g[l].reshape(V_DIM, 1), qT, k, vT,
                    batch=batch, seq=seq, lam_init=lam_init)

    wr = jnp.concatenate([w_router_expert[l], w_router_group[l],
                          jnp.zeros((d, LANES - N_EXPERTS - N_GROUPS), F32)], axis=1)
    wr_hi = wr.astype(BF16)
    wr_lo = (wr - wr_hi.astype(F32)).astype(BF16)
    hres, hn2, route, routeT, cnt = _outproj_router(x2, yc, at, w_out[l].astype(BF16),
                                                    ffn_norm_g[l].reshape(1, d),
                                                    jnp.concatenate([wr_hi, wr_lo], axis=1))

    th = TM_MOE // 2
    n_half_max = (2 * t) // th + N_EXPERTS
    n_tiles_max = (n_half_max + N_EXPERTS) // 2
    counts = cnt[:, 0].astype(jnp.int32)
    halves = (counts + th - 1) // th
    half_start = jnp.cumsum(halves) - halves
    tiles = (halves + 1) // 2
    tile_end = jnp.cumsum(tiles)
    tile_start = tile_end - tiles
    pos = _positions(half_start * th, tile_start * TM_MOE, routeT)
    n_tiles = tile_end[-1:].astype(jnp.int32)
    tile_ids = jnp.arange(n_tiles_max, dtype=jnp.int32)
    tile_expert = jnp.minimum(
        jnp.sum((tile_end[None, :] <= tile_ids[:, None]).astype(jnp.int32), axis=1), N_EXPERTS - 1)
    e_ids = jnp.arange(N_EXPERTS, dtype=jnp.int32)
    nonempty = tiles > 0
    later = jnp.where(nonempty[None, :] & (e_ids[None, :] > e_ids[:, None]), e_ids[None, :], N_EXPERTS)
    next_of = jnp.min(later, axis=1)
    next_of = jnp.where(next_of == N_EXPERTS, -1, next_of).astype(jnp.int32)
    slot_of = ((jnp.cumsum(nonempty.astype(jnp.int32)) - 1) % 2).astype(jnp.int32)
    onehot_te = (tile_expert[:, None] == e_ids[None, :]).astype(jnp.int32)
    next_expert = jnp.sum(onehot_te * next_of[None, :], axis=1)
    stage_slot = jnp.sum(onehot_te * slot_of[None, :], axis=1)
    tile_in_expert = tile_ids - jnp.sum(onehot_te * tile_start[None, :], axis=1)
    half_block = jnp.sum(onehot_te * half_start[None, :], axis=1) + 2 * tile_in_expert
    n_halves = jnp.clip(jnp.sum(onehot_te * halves[None, :], axis=1) - 2 * tile_in_expert, 1, 2)

    posw = pos.reshape(4 * t // SC_WIN, SC_WIN)
    nw = t // SC_WIN
    xs = _sc_dispatch(hn2, posw[0:nw], posw[nw:2 * nw], n_half_max * th)

    f = w_exp_gate.shape[-1]
    ys = _moe(tile_expert, n_tiles, next_expert, stage_slot, half_block, n_halves, xs,
              w_exp_gate[l].reshape(N_EXPERTS, d, f),
              w_exp_up[l].reshape(N_EXPERTS, d, f),
              w_exp_down[l].reshape(N_EXPERTS, f, d))
    yg = _sc_gather(ys, posw[2 * nw:4 * nw])
    out = _combine(hres, yg, route)
    return out.reshape(batch, seq, d)


def kernel(x, attn_norm_g, w_in, conv_w, conv_out_g, q_norm_g, k_norm_g, lambda_q1, lambda_k1,
           lambda_q2, lambda_k2, attn_subln_g, w_out, ffn_norm_g, w_router_group, w_router_expert,
           w_exp_gate, w_exp_up, w_exp_down):
    h = x
    for l in range(attn_norm_g.shape[0]):
        h = _layer(h, l, attn_norm_g, w_in, conv_w, conv_out_g, q_norm_g, k_norm_g,
                   lambda_q1, lambda_k1, lambda_q2, lambda_k2, attn_subln_g, w_out,
                   ffn_norm_g, w_router_group, w_router_expert, w_exp_gate, w_exp_up, w_exp_down)
    return h
```

```python
import functools
import math

import jax
import jax.numpy as jnp
from jax import lax
from jax.experimental import pallas as pl
from jax.experimental.pallas import tpu as pltpu
from jax.experimental.pallas import tpu_sc as plsc

F32 = jnp.float32
BF16 = jnp.bfloat16

HEAD_DIM = 64
V_DIM = 2 * HEAD_DIM
CONV_GROUPS = 8
N_GROUPS = 4
EXPERTS_PER_GROUP = 8
N_EXPERTS = N_GROUPS * EXPERTS_PER_GROUP
EPS = 1e-6
LANES = 128
MXU_TILE = 256
ONES_ROWS = 16
VMEM_LIMIT = 48 * 1024 * 1024

TM_PROJ = 256
TM_COMBINE = 1024
OUTPROJ_SUBTILES = 4
INPROJ_SUBTILES = 2
TQ = 512
TK = 512
TM_MOE = 512
WEIGHT_SLOTS = 3
SC_CORES = 2
SC_SUBCORES = 16
SC_WIN = 64


def _dot(a, b):
    return jnp.dot(a, b, preferred_element_type=F32)


def _pack_rows(x):
    w = x.shape[1] // 2
    bits = lax.bitcast_convert_type(x.astype(BF16).astype(F32), jnp.uint32)
    return lax.bitcast_convert_type((bits[:, :w] >> 16) | bits[:, w:], jnp.int32)


def _unpack_rows(packed):
    bits = lax.bitcast_convert_type(packed, jnp.uint32)
    left = lax.bitcast_convert_type(bits << 16, F32)
    right = lax.bitcast_convert_type(bits & jnp.uint32(0xFFFF0000), F32)
    return left, right


def _group_mean(sq, gm):
    w = gm.shape[0]
    sq = sq.astype(BF16)
    return jnp.concatenate([_dot(sq[:, c:c + w], gm) for c in range(0, sq.shape[1], w)], axis=1)


def _inproj_kernel(x_ref, g_ref, w_ref, cw_ref, cg_ref, qg_ref, kg_ref, gm_ref,
                   yc_ref, qT_ref, k_ref, vT_ref, carry_ref, *, tm, dc):
    j = pl.program_id(1)

    @pl.when(j == 0)
    def _():
        carry_ref[...] = jnp.zeros_like(carry_ref)

    for u in range(x_ref.shape[0] // tm):
        _inproj_subtile(u, x_ref, g_ref, w_ref, cw_ref, cg_ref, qg_ref, kg_ref, gm_ref,
                        yc_ref, qT_ref, k_ref, vT_ref, carry_ref, tm=tm, dc=dc)


def _inproj_subtile(u, x_ref, g_ref, w_ref, cw_ref, cg_ref, qg_ref, kg_ref, gm_ref,
                    yc_ref, qT_ref, k_ref, vT_ref, carry_ref, *, tm, dc):
    rows = slice(u * tm, (u + 1) * tm)
    x = x_ref[rows, :]
    ms = jnp.mean(x * x, axis=-1, keepdims=True)
    hn = (x * lax.rsqrt(ms + EPS) * g_ref[...]).astype(BF16)

    def proj(s):
        return _dot(hn, w_ref[:, s * dc:(s + 1) * dc])

    gm = gm_ref[...]

    c = proj(2) * proj(0)
    prev = carry_ref[...]
    r = lax.broadcasted_iota(jnp.int32, c.shape, 0)
    c1 = jnp.where(r == 0, prev[7:8, :], pltpu.roll(c, 1, 0))
    c2 = jnp.where(r == 0, prev[6:7, :], jnp.where(r == 1, prev[7:8, :], pltpu.roll(c, 2, 0)))
    carry_ref[...] = c[tm - 8:tm, :]
    cw = cw_ref[...]
    y = proj(1) * (cw[0:1, :] * c2 + cw[1:2, :] * c1 + cw[2:3, :] * c)
    yc_ref[rows, :] = (y * lax.rsqrt(_group_mean(y * y, gm) + EPS) * cg_ref[...]).astype(BF16)

    q = proj(3)
    qT_ref[:, rows] = (q * lax.rsqrt(_group_mean(q * q, gm) + EPS) * qg_ref[...]).T.astype(BF16)
    k = proj(4)
    k_ref[rows, :] = (k * lax.rsqrt(_group_mean(k * k, gm) + EPS) * kg_ref[...]).astype(BF16)
    vt = proj(5).T.astype(BF16)
    vrows = V_DIM + ONES_ROWS
    for h in range(dc // V_DIM):
        vT_ref[u, h * vrows:h * vrows + V_DIM, :] = vt[h * V_DIM:(h + 1) * V_DIM, :]
        vT_ref[u, h * vrows + V_DIM:(h + 1) * vrows, :] = jnp.ones((ONES_ROWS, tm), BF16)


def _inproj(x2, g, w_in, conv_w, conv_g, qg, kg, gmat, *, batch, seq):
    t, d = x2.shape
    dc = conv_g.shape[1]
    dv = dc // V_DIM * (V_DIM + ONES_ROWS)
    tm = TK
    tb = tm * INPROJ_SUBTILES
    nj = seq // tb
    row = lambda b, j: (b * nj + j, 0)
    const = lambda b, j: (0, 0)
    out_sds = jax.ShapeDtypeStruct((t, dc), BF16)
    return pl.pallas_call(
        functools.partial(_inproj_kernel, tm=tm, dc=dc),
        grid=(batch, nj),
        in_specs=[
            pl.BlockSpec((tb, d), row),
            pl.BlockSpec((1, d), const),
            pl.BlockSpec(w_in.shape, const),
            pl.BlockSpec(conv_w.shape, const),
            pl.BlockSpec((1, dc), const),
            pl.BlockSpec((1, dc), const),
            pl.BlockSpec((1, dc), const),
            pl.BlockSpec(gmat.shape, const),
        ],
        out_specs=[
            pl.BlockSpec((tb, dc), row),
            pl.BlockSpec((dc, tb), lambda b, j: (0, b * nj + j)),
            pl.BlockSpec((tb, dc), row),
            pl.BlockSpec((INPROJ_SUBTILES, dv, tm), lambda b, j: (b * nj + j, 0, 0)),
        ],
        out_shape=[out_sds, jax.ShapeDtypeStruct((dc, t), BF16), out_sds,
                   jax.ShapeDtypeStruct((t // tm, dv, tm), BF16)],
        scratch_shapes=[pltpu.VMEM((8, dc), F32)],
        compiler_params=pltpu.CompilerParams(
            dimension_semantics=("arbitrary", "arbitrary"), vmem_limit_bytes=VMEM_LIMIT),
        name="inproj_conv_qknorm",
    )(x2, g, w_in, conv_w, conv_g, qg, kg, gmat)


def _attn_kernel(lp_ref, sg_ref, qT_ref, k_ref, vT_ref, o_ref, m_ref, acc_ref,
                 sa_ref, pb_ref, ab_ref, *, tq, tk, nq, lam_init):
    map_a = slice(0, tq)
    map_b = slice(tq, 2 * tq)

    lp = lp_ref[...]
    lam = (jnp.exp(jnp.sum(lp[0:1, :] * lp[1:2, :], axis=-1, keepdims=True))
           - jnp.exp(jnp.sum(lp[2:3, :] * lp[3:4, :], axis=-1, keepdims=True)) + lam_init)

    def stacked_queries(qi):
        qT = qT_ref[:, qi * tq:(qi + 1) * tq]
        row = lax.broadcasted_iota(jnp.int32, qT.shape, 0)
        zero = jnp.zeros_like(qT)
        return jnp.concatenate([jnp.where(row < HEAD_DIM, qT, zero),
                                jnp.where(row >= HEAD_DIM, qT, zero)], axis=1)

    def pieces(diag):
        return ((slice(0, tq // 2), tk // 2), (slice(tq // 2, tq), tk)) if diag else ((slice(0, tq), tk),)

    def shifted(cols, off):
        return slice(cols.start + off, cols.stop + off)

    def scores(j, qq, cols, n_keys):
        return _dot(k_ref[j * tk:j * tk + n_keys, :], qq[:, cols])

    def softmax(m, cols, s, q0):
        if q0 is not None:
            qrel = q0 + lax.broadcasted_iota(jnp.int32, s.shape, 1)
            krel = lax.broadcasted_iota(jnp.int32, s.shape, 0)
            s = jnp.where(krel <= qrel, s, -jnp.inf)
        sb = s.astype(BF16)
        m_old = m[:, cols]
        m_new = jnp.maximum(m_old, jnp.max(sb, axis=0, keepdims=True).astype(F32))
        alpha = jnp.exp2(m_old - m_new)
        p = jnp.exp2(sb - m_new.astype(BF16))
        m[:, cols] = m_new
        return alpha, p

    def accumulate(acc, cols, alpha, vb, p):
        acc[:, cols] = alpha * acc[:, cols] + _dot(vb, p)

    def prefetch_map_a(sa, j, qq, diag):
        for cols, n_keys in pieces(diag):
            sa[0:n_keys, cols] = scores(j, qq, cols, n_keys)

    stacked = {0: stacked_queries(0)}
    m_ref[0] = jnp.full(m_ref.shape[1:], -jnp.inf, F32)
    acc_ref[0] = jnp.zeros(acc_ref.shape[1:], F32)
    prefetch_map_a(sa_ref.at[0], 0, stacked[0], True)

    for qi in range(nq):
        par = qi % 2
        m, acc, sa, qq = m_ref.at[par], acc_ref.at[par], sa_ref.at[par], stacked[qi]
        for j in range(qi + 1):
            last = j == qi
            if j > 0:
                accumulate(acc, map_b, ab_ref[...], vT_ref[j - 1], pb_ref[...])
            s_b = [scores(j, qq, shifted(cols, tq), n_keys) for cols, n_keys in pieces(last)]
            for cols, n_keys in pieces(last):
                alpha_a, p_a = softmax(m, cols, sa[0:n_keys, cols], cols.start if last else None)
                accumulate(acc, cols, alpha_a, vT_ref[j, :, 0:n_keys], p_a)
            if not last:
                prefetch_map_a(sa, j + 1, qq, j + 1 == qi)
            elif qi + 1 < nq:
                stacked[qi + 1] = stacked_queries(qi + 1)
                m_ref[1 - par] = jnp.full(m_ref.shape[1:], -jnp.inf, F32)
                acc_ref[1 - par] = jnp.zeros(acc_ref.shape[1:], F32)
                prefetch_map_a(sa_ref.at[1 - par], 0, stacked[qi + 1], False)
            for (cols, n_keys), s in zip(pieces(last), s_b):
                alpha_b, p_b = softmax(m, shifted(cols, tq), s, cols.start if last else None)
                if last:
                    accumulate(acc, shifted(cols, tq), alpha_b, vT_ref[j, :, 0:n_keys], p_b)
                else:
                    ab_ref[...] = alpha_b
                    pb_ref[...] = p_b

        o = acc[0:V_DIM, :] / acc[V_DIM:V_DIM + 1, :]
        d = o[:, map_a] - lam * o[:, map_b]
        ms = jnp.mean(d * d, axis=0, keepdims=True)
        o_ref[qi * tq:(qi + 1) * tq, :] = (
            d * lax.rsqrt(ms + EPS) * sg_ref[...] * (1.0 - lam_init)).T.astype(BF16)
        del stacked[qi]


def _attention(lam_params, subln_g, qT, k, vT, *, batch, seq, lam_init):
    dq, t = qT.shape
    n_heads = dq // V_DIM
    tq, tk = TQ, TK
    assert tq == tk and vT.shape[2] == tk
    nq = seq // tq
    nk = seq // tk
    vrows = V_DIM + ONES_ROWS
    const = lambda b, h: (0, 0)
    return pl.pallas_call(
        functools.partial(_attn_kernel, tq=tq, tk=tk, nq=nq, lam_init=lam_init),
        grid=(batch, n_heads),
        in_specs=[
            pl.BlockSpec(lam_params.shape, const),
            pl.BlockSpec((V_DIM, 1), const),
            pl.BlockSpec((V_DIM, seq), lambda b, h: (h, b)),
            pl.BlockSpec((seq, V_DIM), lambda b, h: (b, h)),
            pl.BlockSpec((nk, vrows, tk), lambda b, h: (b, h, 0)),
        ],
        out_specs=pl.BlockSpec((seq, V_DIM), lambda b, h: (b, h)),
        out_shape=jax.ShapeDtypeStruct((t, dq), BF16),
        scratch_shapes=[pltpu.VMEM((2, 1, 2 * tq), F32),
                        pltpu.VMEM((2, vrows, 2 * tq), F32),
                        pltpu.VMEM((2, tk, tq), F32), pltpu.VMEM((tk, tq), BF16), pltpu.VMEM((1, tq), F32)],
        compiler_params=pltpu.CompilerParams(
            dimension_semantics=("arbitrary", "arbitrary"), vmem_limit_bytes=VMEM_LIMIT),
        name="diff_attention",
    )(lam_params, subln_g, qT, k, vT)


def _outproj_router_kernel(x_ref, yc_ref, at_ref, wo_ref, g_ref, wr_ref,
                           h_ref, hn_ref, route_ref, routeT_ref, cnt_ref, *, tm, dc):
    i = pl.program_id(0)

    @pl.when(i == 0)
    def _():
        cnt_ref[...] = jnp.zeros_like(cnt_ref)

    for r0 in range(0, x_ref.shape[0], tm):
        _outproj_router_subtile(slice(r0, r0 + tm), x_ref, yc_ref, at_ref, wo_ref, g_ref, wr_ref,
                                h_ref, hn_ref, route_ref, routeT_ref, cnt_ref, tm=tm, dc=dc)


def _outproj_router_subtile(rows, x_ref, yc_ref, at_ref, wo_ref, g_ref, wr_ref,
                            h_ref, hn_ref, route_ref, routeT_ref, cnt_ref, *, tm, dc):
    h = x_ref[rows, :] + _dot(yc_ref[rows, :], wo_ref[0:dc, :]) + _dot(at_ref[rows, :], wo_ref[dc:2 * dc, :])
    h_ref[rows, :] = h
    ms = jnp.mean(h * h, axis=-1, keepdims=True)
    hn = h * lax.rsqrt(ms + EPS) * g_ref[...]
    hi = hn.astype(BF16)
    hn_ref[rows, :] = _pack_rows(hn)
    lo = (hn - hi.astype(F32)).astype(BF16)
    prod = _dot(hi, wr_ref[...])
    logits = prod[:, 0:LANES] + prod[:, LANES:2 * LANES] + _dot(lo, wr_ref[:, 0:LANES])

    lt = logits.T
    neg = -jnp.inf
    grow = lax.broadcasted_iota(jnp.int32, (8, tm), 0).astype(F32)
    gl = jnp.where(grow < N_GROUPS, lt[N_EXPERTS:N_EXPERTS + 8, :], neg)
    gmax = jnp.max(gl, axis=0, keepdims=True)
    g_gate = 1.0 / jnp.sum(jnp.exp(gl - gmax), axis=0, keepdims=True)
    g_idx = jnp.min(jnp.where(gl == gmax, grow, 1e9), axis=0, keepdims=True)
    erow = lax.broadcasted_iota(jnp.int32, (N_EXPERTS, tm), 0).astype(F32)
    e_lo = g_idx * EXPERTS_PER_GROUP
    el = jnp.where((erow >= e_lo) & (erow < e_lo + EXPERTS_PER_GROUP), lt[0:N_EXPERTS, :], neg)
    v1 = jnp.max(el, axis=0, keepdims=True)
    i1 = jnp.min(jnp.where(el == v1, erow, 1e9), axis=0, keepdims=True)
    el2 = jnp.where(erow == i1, neg, el)
    v2 = jnp.max(el2, axis=0, keepdims=True)
    i2 = jnp.min(jnp.where(el2 == v2, erow, 1e9), axis=0, keepdims=True)
    tt = jnp.exp(v2 - v1)
    w1 = g_gate / (1.0 + tt)
    w2 = g_gate * tt / (1.0 + tt)

    sel1 = erow == i1
    sel2 = erow == i2
    oh = jnp.where(sel1 | sel2, 1.0, 0.0)
    ss = lax.broadcasted_iota(jnp.int32, (tm, tm), 0)
    tt_i = lax.broadcasted_iota(jnp.int32, (tm, tm), 1)
    earlier = jnp.where(ss < tt_i, 1.0, 0.0).astype(BF16)
    ranks = _dot(oh.astype(BF16), earlier) + cnt_ref[...]
    r1 = jnp.sum(jnp.where(sel1, ranks, 0.0), axis=0, keepdims=True)
    r2 = jnp.sum(jnp.where(sel2, ranks, 0.0), axis=0, keepdims=True)
    cnt_ref[...] = cnt_ref[...] + jnp.sum(oh, axis=1, keepdims=True)

    routeT = jnp.concatenate([i1, i2, r1, r2, w1, w2, jnp.zeros((2, tm), F32)], axis=0)
    routeT_ref[:, rows] = routeT
    route_ref[rows, :] = jnp.concatenate([routeT, jnp.zeros((LANES - 8, tm), F32)], axis=0).T


def _outproj_router(x2, yc, at, w_out, g, wr_cat):
    t, d = x2.shape
    dc = yc.shape[1]
    tm = TM_PROJ
    tb = tm * OUTPROJ_SUBTILES
    row = lambda i: (i, 0)
    const = lambda i: (0, 0)
    return pl.pallas_call(
        functools.partial(_outproj_router_kernel, tm=tm, dc=dc),
        grid=(t // tb,),
        in_specs=[
            pl.BlockSpec((tb, d), row),
            pl.BlockSpec((tb, dc), row),
            pl.BlockSpec((tb, dc), row),
            pl.BlockSpec(w_out.shape, const),
            pl.BlockSpec((1, d), const),
            pl.BlockSpec(wr_cat.shape, const),
        ],
        out_specs=[
            pl.BlockSpec((tb, d), row),
            pl.BlockSpec((tb, d // 2), row),
            pl.BlockSpec((tb, LANES), row),
            pl.BlockSpec((8, tb), lambda i: (0, i)),
            pl.BlockSpec((N_EXPERTS, 1), const),
        ],
        out_shape=[
            jax.ShapeDtypeStruct((t, d), F32),
            jax.ShapeDtypeStruct((t, d // 2), jnp.int32),
            jax.ShapeDtypeStruct((t, LANES), F32),
            jax.ShapeDtypeStruct((8, t), F32),
            jax.ShapeDtypeStruct((N_EXPERTS, 1), F32),
        ],
        compiler_params=pltpu.CompilerParams(
            dimension_semantics=("arbitrary",), vmem_limit_bytes=VMEM_LIMIT),
        name="outproj_router",
    )(x2, yc, at, w_out, g, wr_cat)


def _positions_kernel(xoffs_ref, yoffs_ref, rt_ref, pos_ref):
    rt = rt_ref[...]
    ea, eb = rt[0:1, :], rt[1:2, :]
    zero = jnp.zeros_like(ea)
    xa, xb, ya, yb = zero, zero, zero, zero
    for e in range(N_EXPERTS):
        xs, ys = xoffs_ref[e].astype(F32), yoffs_ref[e].astype(F32)
        xa = jnp.where(ea == e, xs, xa)
        xb = jnp.where(eb == e, xs, xb)
        ya = jnp.where(ea == e, ys, ya)
        yb = jnp.where(eb == e, ys, yb)
    pos_ref[0:1, :] = (xa + rt[2:3, :]).astype(jnp.int32)
    pos_ref[1:2, :] = (xb + rt[3:4, :]).astype(jnp.int32)
    pos_ref[2:3, :] = (ya + rt[2:3, :]).astype(jnp.int32)
    pos_ref[3:4, :] = (yb + rt[3:4, :]).astype(jnp.int32)


def _positions(xoffs, yoffs, routeT):
    t = routeT.shape[1]
    return pl.pallas_call(
        _positions_kernel,
        grid_spec=pltpu.PrefetchScalarGridSpec(
            num_scalar_prefetch=2, grid=(1,),
            in_specs=[pl.BlockSpec(routeT.shape, lambda i, xo, yo: (0, 0))],
            out_specs=pl.BlockSpec((4, t), lambda i, xo, yo: (0, 0)),
        ),
        out_shape=jax.ShapeDtypeStruct((4, t), jnp.int32),
        name="positions",
    )(xoffs, yoffs, routeT)


def _moe_kernel(te_ref, nt_ref, nx_ref, n2_ref, sl_ref, hb_ref, nh_ref, xa_ref, xb_ref, wg_hbm, wu_hbm, wd_hbm,
                y_ref, wg_st, wu_st, wd_st, wgb_ref, wub_ref, wdb_ref, sem):
    i = pl.program_id(0)
    last = nt_ref[0] - 1
    ic = jnp.minimum(i, last)
    expert = te_ref[ic]
    slot = sl_ref[ic]
    first_of_expert = (i == 0) | ((i <= last) & (expert != te_ref[jnp.maximum(ic - 1, 0)]))

    def weight_copies(e, s):
        return (pltpu.make_async_copy(wg_hbm.at[e], wg_st.at[s], sem.at[s, 0]),
                pltpu.make_async_copy(wu_hbm.at[e], wu_st.at[s], sem.at[s, 1]),
                pltpu.make_async_copy(wd_hbm.at[e], wd_st.at[s], sem.at[s, 2]))

    def next_slot(s):
        return jnp.where(s == WEIGHT_SLOTS - 1, 0, s + 1)

    @pl.when(i == 0)
    def _():
        for c in weight_copies(expert, slot):
            c.start()
        nxt = nx_ref[ic]

        @pl.when(nxt >= 0)
        def _():
            for c in weight_copies(nxt, next_slot(slot)):
                c.start()

    @pl.when(first_of_expert)
    def _():
        for c in weight_copies(expert, slot):
            c.wait()
        ahead = n2_ref[ic]

        @pl.when(ahead >= 0)
        def _():
            for c in weight_copies(ahead, next_slot(next_slot(slot))):
                c.start()

        wgb_ref[...] = wg_st[slot].astype(BF16)
        wub_ref[...] = wu_st[slot].astype(BF16)
        wdb_ref[...] = wd_st[slot].astype(BF16)

    def expert_mlp(x_packed):
        x_l, x_r = _unpack_rows(x_packed)
        x_l = x_l.astype(BF16)
        x_r = x_r.astype(BF16)
        half = x_l.shape[1]
        hg = _dot(x_l, wgb_ref[0:half, :]) + _dot(x_r, wgb_ref[half:2 * half, :])
        hu = _dot(x_l, wub_ref[0:half, :]) + _dot(x_r, wub_ref[half:2 * half, :])
        act = hg * (1.0 / (1.0 + jnp.exp(-hg))) * hu
        return _pack_rows(_dot(act.astype(BF16), wdb_ref[...]))

    n_half = nh_ref[ic]
    th = xa_ref.shape[0]

    @pl.when((i <= last) & (n_half == 2))
    def _():
        y_ref[...] = expert_mlp(jnp.concatenate([xa_ref[...], xb_ref[...]], axis=0))

    @pl.when((i <= last) & (n_half == 1))
    def _():
        y_ref[0:th, :] = expert_mlp(xa_ref[...])


def _moe(tile_expert, n_tiles, next_expert, next2_expert, stage_slot, half_block, n_halves, xs, wg, wu, wd):
    dp = xs.shape[1]
    d, f = wg.shape[1], wg.shape[2]
    assert dp * 2 == d
    th = TM_MOE // 2
    n_tiles_max = tile_expert.shape[0]

    def tile(s, nt):
        return jnp.minimum(s, nt[0] - 1)

    first = lambda s, te, nt, nx, n2, sl, hb, nh: (hb[tile(s, nt)], 0)
    second = lambda s, te, nt, nx, n2, sl, hb, nh: (hb[tile(s, nt)] + nh[tile(s, nt)] - 1, 0)
    hbm = pl.BlockSpec(memory_space=pl.ANY)
    grid_spec = pltpu.PrefetchScalarGridSpec(
        num_scalar_prefetch=7,
        grid=(n_tiles_max,),
        in_specs=[pl.BlockSpec((th, dp), first), pl.BlockSpec((th, dp), second), hbm, hbm, hbm],
        out_specs=pl.BlockSpec((TM_MOE, dp), lambda s, te, nt, nx, n2, sl, hb, nh: (tile(s, nt), 0)),
        scratch_shapes=[pltpu.VMEM((WEIGHT_SLOTS, d, f), F32), pltpu.VMEM((WEIGHT_SLOTS, d, f), F32),
                        pltpu.VMEM((WEIGHT_SLOTS, f, d), F32),
                        pltpu.VMEM((d, f), BF16), pltpu.VMEM((d, f), BF16), pltpu.VMEM((f, d), BF16),
                        pltpu.SemaphoreType.DMA((WEIGHT_SLOTS, 3))],
    )
    return pl.pallas_call(
        _moe_kernel,
        grid_spec=grid_spec,
        out_shape=jax.ShapeDtypeStruct((n_tiles_max * TM_MOE, dp), jnp.int32),
        compiler_params=pltpu.CompilerParams(
            dimension_semantics=("arbitrary",), vmem_limit_bytes=VMEM_LIMIT),
        name="moe_experts",
    )(tile_expert, n_tiles, next_expert, next2_expert, stage_slot, half_block, n_halves, xs, xs, wg, wu, wd)


def _sc_mesh():
    return plsc.VectorSubcoreMesh(core_axis_name="c", subcore_axis_name="s",
                                  num_cores=SC_CORES, num_subcores=SC_SUBCORES)


def _sc_dispatch(rows, pos_a, pos_b, n_out):
    t, d = rows.shape
    win = pos_a.shape[1]

    @functools.partial(pl.kernel, out_type=jax.ShapeDtypeStruct((n_out, d), rows.dtype),
                       mesh=_sc_mesh(), scratch_types=[], name="sc_dispatch")
    def run(rows_hbm, pa_hbm, pb_hbm, out_hbm):
        def body(rows_vmem, pa_vmem, pb_vmem):
            pltpu.sync_copy(rows_vmem, out_hbm.at[pa_vmem.at[0]])
            pltpu.sync_copy(rows_vmem, out_hbm.at[pb_vmem.at[0]])

        pltpu.emit_pipeline(
            body, grid=(t // win,),
            in_specs=[pl.BlockSpec((win, d), lambda i: (i, 0)),
                      pl.BlockSpec((1, win), lambda i: (i, 0)),
                      pl.BlockSpec((1, win), lambda i: (i, 0))],
            out_specs=[],
            core_axis_name=("c", "s"),
            dimension_semantics=(pltpu.PARALLEL,),
        )(rows_hbm, pa_hbm, pb_hbm)

    return run(rows, pos_a, pos_b)


def _sc_gather(table, idx):
    d = table.shape[1]
    n_win, win = idx.shape

    @functools.partial(pl.kernel, out_type=jax.ShapeDtypeStruct((n_win * win, d), table.dtype),
                       mesh=_sc_mesh(), scratch_types=[], name="sc_gather")
    def run(table_hbm, idx_hbm, out_hbm):
        def body(idx_vmem, out_vmem):
            pltpu.sync_copy(table_hbm.at[idx_vmem.at[0]], out_vmem)

        pltpu.emit_pipeline(
            body, grid=(n_win,),
            in_specs=[pl.BlockSpec((1, win), lambda i: (i, 0))],
            out_specs=[pl.BlockSpec((win, d), lambda i: (i, 0))],
            core_axis_name=("c", "s"),
            dimension_semantics=(pltpu.PARALLEL,),
        )(idx_hbm, out_hbm)

    return run(table, idx)


def _combine_kernel(h_ref, ya_ref, yb_ref, r_ref, o_ref):
    r = r_ref[...]
    wa, wb = r[:, 4:5], r[:, 5:6]
    a_l, a_r = _unpack_rows(ya_ref[...])
    b_l, b_r = _unpack_rows(yb_ref[...])
    half = a_l.shape[1]
    o_ref[:, 0:half] = h_ref[:, 0:half] + wa * a_l + wb * b_l
    o_ref[:, half:2 * half] = h_ref[:, half:2 * half] + wa * a_r + wb * b_r


def _combine(hres, yg, route):
    t, d = hres.shape
    tm = TM_COMBINE
    nb = t // tm
    return pl.pallas_call(
        _combine_kernel,
        grid=(nb,),
        in_specs=[pl.BlockSpec((tm, d), lambda i: (i, 0)),
                  pl.BlockSpec((tm, d // 2), lambda i: (i, 0)),
                  pl.BlockSpec((tm, d // 2), lambda i: (i + nb, 0)),
                  pl.BlockSpec((tm, LANES), lambda i: (i, 0))],
        out_specs=pl.BlockSpec((tm, d), lambda i: (i, 0)),
        out_shape=jax.ShapeDtypeStruct((t, d), F32),
        compiler_params=pltpu.CompilerParams(
            dimension_semantics=("arbitrary",), vmem_limit_bytes=VMEM_LIMIT),
        name="combine",
    )(hres, yg, yg, route)


def _lambda_init(layer_idx):
    return 0.8 - 0.6 * math.exp(-0.3 * layer_idx)


def _layer(h, l, attn_norm_g, w_in, conv_w, conv_out_g, q_norm_g, k_norm_g,
           lambda_q1, lambda_k1, lambda_q2, lambda_k2, attn_subln_g, w_out,
           ffn_norm_g, w_router_group, w_router_expert, w_exp_gate, w_exp_up, w_exp_down):
    batch, seq, d = h.shape
    t = batch * seq
    dc = conv_w.shape[-1]
    lam_init = _lambda_init(l)
    x2 = h.reshape(t, d)

    reps = dc // HEAD_DIM
    assert dc // CONV_GROUPS == HEAD_DIM
    qg = (jnp.tile(q_norm_g[l], reps) * (HEAD_DIM ** -0.5 * math.log2(math.e))).reshape(1, dc)
    kg = jnp.tile(k_norm_g[l], reps).reshape(1, dc)
    grp = jnp.arange(MXU_TILE) // HEAD_DIM
    gmat = jnp.where(grp[:, None] == grp[None, :], 1.0 / HEAD_DIM, 0.0).astype(BF16)
    yc, qT, k, vT = _inproj(x2, attn_norm_g[l].reshape(1, d), w_in[l].astype(BF16), conv_w[l],
                          conv_out_g[l].reshape(1, dc), qg, kg, gmat, batch=batch, seq=seq)

    lam_params = jnp.stack([lambda_q1[l], lambda_k1[l], lambda_q2[l], lambda_k2[l]])
    at = _attention(lam_params, attn_subln_g[l].reshape(V_DIM, 1), qT, k, vT,
                    batch=batch, seq=seq, lam_init=lam_init)

    wr = jnp.concatenate([w_router_expert[l], w_router_group[l],
                          jnp.zeros((d, LANES - N_EXPERTS - N_GROUPS), F32)], axis=1)
    wr_hi = wr.astype(BF16)
    wr_lo = (wr - wr_hi.astype(F32)).astype(BF16)
    hres, hn2, route, routeT, cnt = _outproj_router(x2, yc, at, w_out[l].astype(BF16),
                                                    ffn_norm_g[l].reshape(1, d),
                                                    jnp.concatenate([wr_hi, wr_lo], axis=1))

    th = TM_MOE // 2
    n_half_max = (2 * t) // th + N_EXPERTS
    n_tiles_max = (n_half_max + N_EXPERTS) // 2
    counts = cnt[:, 0].astype(jnp.int32)
    halves = (counts + th - 1) // th
    half_start = jnp.cumsum(halves) - halves
    tiles = (halves + 1) // 2
    tile_end = jnp.cumsum(tiles)
    tile_start = tile_end - tiles
    pos = _positions(half_start * th, tile_start * TM_MOE, routeT)
    n_tiles = tile_end[-1:].astype(jnp.int32)
    tile_ids = jnp.arange(n_tiles_max, dtype=jnp.int32)
    tile_expert = jnp.minimum(
        jnp.sum((tile_end[None, :] <= tile_ids[:, None]).astype(jnp.int32), axis=1), N_EXPERTS - 1)
    e_ids = jnp.arange(N_EXPERTS, dtype=jnp.int32)
    nonempty = tiles > 0
    later = jnp.where(nonempty[None, :] & (e_ids[None, :] > e_ids[:, None]), e_ids[None, :], N_EXPERTS)
    next_of = jnp.min(later, axis=1)
    next_of = jnp.where(next_of == N_EXPERTS, -1, next_of).astype(jnp.int32)
    next2_of = jnp.where(next_of >= 0, next_of[jnp.maximum(next_of, 0)], -1)
    slot_of = ((jnp.cumsum(nonempty.astype(jnp.int32)) - 1) % WEIGHT_SLOTS).astype(jnp.int32)
    onehot_te = (tile_expert[:, None] == e_ids[None, :]).astype(jnp.int32)
    next_expert = jnp.sum(onehot_te * next_of[None, :], axis=1)
    next2_expert = jnp.sum(onehot_te * next2_of[None, :], axis=1)
    stage_slot = jnp.sum(onehot_te * slot_of[None, :], axis=1)
    tile_in_expert = tile_ids - jnp.sum(onehot_te * tile_start[None, :], axis=1)
    half_block = jnp.sum(onehot_te * half_start[None, :], axis=1) + 2 * tile_in_expert
    n_halves = jnp.clip(jnp.sum(onehot_te * halves[None, :], axis=1) - 2 * tile_in_expert, 1, 2)

    posw = pos.reshape(4 * t // SC_WIN, SC_WIN)
    nw = t // SC_WIN
    xs = _sc_dispatch(hn2, posw[0:nw], posw[nw:2 * nw], n_half_max * th)

    f = w_exp_gate.shape[-1]
    ys = _moe(tile_expert, n_tiles, next_expert, next2_expert, stage_slot, half_block, n_halves, xs,
              w_exp_gate[l].reshape(N_EXPERTS, d, f),
              w_exp_up[l].reshape(N_EXPERTS, d, f),
              w_exp_down[l].reshape(N_EXPERTS, f, d))
    yg = _sc_gather(ys, posw[2 * nw:4 * nw])
    out = _combine(hres, yg, route)
    return out.reshape(batch, seq, d)


def kernel(x, attn_norm_g, w_in, conv_w, conv_out_g, q_norm_g, k_norm_g, lambda_q1, lambda_k1,
           lambda_q2, lambda_k2, attn_subln_g, w_out, ffn_norm_g, w_router_group, w_router_expert,
           w_exp_gate, w_exp_up, w_exp_down):
    h = x
    for l in range(attn_norm_g.shape[0]):
        h = _layer(h, l, attn_norm_g, w_in, conv_w, conv_out_g, q_norm_g, k_norm_g,
                   lambda_q1, lambda_k1, lambda_q2, lambda_k2, attn_subln_g, w_out,
                   ffn_norm_g, w_router_group, w_router_expert, w_exp_gate, w_exp_up, w_exp_down)
    return h
```

```python
import functools
import math

import jax
import jax.numpy as jnp
from jax import lax
from jax.experimental import pallas as pl
from jax.experimental.pallas import tpu as pltpu
from jax.experimental.pallas import tpu_sc as plsc

F32 = jnp.float32
BF16 = jnp.bfloat16

HEAD_DIM = 64
V_DIM = 2 * HEAD_DIM
CONV_GROUPS = 8
N_GROUPS = 4
EXPERTS_PER_GROUP = 8
N_EXPERTS = N_GROUPS * EXPERTS_PER_GROUP
EPS = 1e-6
LANES = 128
MXU_TILE = 256
ONES_ROWS = 16
VMEM_LIMIT = 48 * 1024 * 1024

TM_PROJ = 256
TM_COMBINE = 1024
OUTPROJ_SUBTILES = 4
INPROJ_SUBTILES = 2
TQ = 512
TK = 512
TM_MOE = 512
WEIGHT_SLOTS = 3
SC_CORES = 2
SC_SUBCORES = 16
SC_WIN = 64


def _dot(a, b):
    return jnp.dot(a, b, preferred_element_type=F32)


def _pack_rows(x):
    w = x.shape[1] // 2
    bits = lax.bitcast_convert_type(x.astype(BF16).astype(F32), jnp.uint32)
    return lax.bitcast_convert_type((bits[:, :w] >> 16) | bits[:, w:], jnp.int32)


def _unpack_rows(packed):
    bits = lax.bitcast_convert_type(packed, jnp.uint32)
    left = lax.bitcast_convert_type(bits << 16, F32)
    right = lax.bitcast_convert_type(bits & jnp.uint32(0xFFFF0000), F32)
    return left, right


def _group_mean(sq, gm):
    w = gm.shape[0]
    sq = sq.astype(BF16)
    return jnp.concatenate([_dot(sq[:, c:c + w], gm) for c in range(0, sq.shape[1], w)], axis=1)


def _inproj_kernel(x_ref, g_ref, w_ref, cw_ref, cg_ref, qg_ref, kg_ref, gm_ref,
                   yc_ref, qT_ref, k_ref, vT_ref, carry_ref, *, tm, dc):
    j = pl.program_id(1)

    @pl.when(j == 0)
    def _():
        carry_ref[...] = jnp.zeros_like(carry_ref)

    for u in range(x_ref.shape[0] // tm):
        _inproj_subtile(u, x_ref, g_ref, w_ref, cw_ref, cg_ref, qg_ref, kg_ref, gm_ref,
                        yc_ref, qT_ref, k_ref, vT_ref, carry_ref, tm=tm, dc=dc)


def _inproj_subtile(u, x_ref, g_ref, w_ref, cw_ref, cg_ref, qg_ref, kg_ref, gm_ref,
                    yc_ref, qT_ref, k_ref, vT_ref, carry_ref, *, tm, dc):
    rows = slice(u * tm, (u + 1) * tm)
    x = x_ref[rows, :]
    ms = jnp.mean(x * x, axis=-1, keepdims=True)
    hn = (x * lax.rsqrt(ms + EPS) * g_ref[...]).astype(BF16)

    def proj(s):
        return _dot(hn, w_ref[:, s * dc:(s + 1) * dc])

    gm = gm_ref[...]

    c = proj(2) * proj(0)
    prev = carry_ref[...]
    r = lax.broadcasted_iota(jnp.int32, c.shape, 0)
    c1 = jnp.where(r == 0, prev[7:8, :], pltpu.roll(c, 1, 0))
    c2 = jnp.where(r == 0, prev[6:7, :], jnp.where(r == 1, prev[7:8, :], pltpu.roll(c, 2, 0)))
    carry_ref[...] = c[tm - 8:tm, :]
    cw = cw_ref[...]
    y = proj(1) * (cw[0:1, :] * c2 + cw[1:2, :] * c1 + cw[2:3, :] * c)
    yc_ref[rows, :] = (y * lax.rsqrt(_group_mean(y * y, gm) + EPS) * cg_ref[...]).astype(BF16)

    q = proj(3)
    qT_ref[:, rows] = (q * lax.rsqrt(_group_mean(q * q, gm) + EPS) * qg_ref[...]).T.astype(BF16)
    k = proj(4)
    k_ref[rows, :] = (k * lax.rsqrt(_group_mean(k * k, gm) + EPS) * kg_ref[...]).astype(BF16)
    vt = proj(5).T.astype(BF16)
    vrows = V_DIM + ONES_ROWS
    for h in range(dc // V_DIM):
        vT_ref[u, h * vrows:h * vrows + V_DIM, :] = vt[h * V_DIM:(h + 1) * V_DIM, :]
        vT_ref[u, h * vrows + V_DIM:(h + 1) * vrows, :] = jnp.ones((ONES_ROWS, tm), BF16)


def _inproj(x2, g, w_in, conv_w, conv_g, qg, kg, gmat, *, batch, seq):
    t, d = x2.shape
    dc = conv_g.shape[1]
    dv = dc // V_DIM * (V_DIM + ONES_ROWS)
    tm = TK
    tb = tm * INPROJ_SUBTILES
    nj = seq // tb
    row = lambda b, j: (b * nj + j, 0)
    const = lambda b, j: (0, 0)
    out_sds = jax.ShapeDtypeStruct((t, dc), BF16)
    return pl.pallas_call(
        functools.partial(_inproj_kernel, tm=tm, dc=dc),
        grid=(batch, nj),
        in_specs=[
            pl.BlockSpec((tb, d), row),
            pl.BlockSpec((1, d), const),
            pl.BlockSpec(w_in.shape, const),
            pl.BlockSpec(conv_w.shape, const),
            pl.BlockSpec((1, dc), const),
            pl.BlockSpec((1, dc), const),
            pl.BlockSpec((1, dc), const),
            pl.BlockSpec(gmat.shape, const),
        ],
        out_specs=[
            pl.BlockSpec((tb, dc), row),
            pl.BlockSpec((dc, tb), lambda b, j: (0, b * nj + j)),
            pl.BlockSpec((tb, dc), row),
            pl.BlockSpec((INPROJ_SUBTILES, dv, tm), lambda b, j: (b * nj + j, 0, 0)),
        ],
        out_shape=[out_sds, jax.ShapeDtypeStruct((dc, t), BF16), out_sds,
                   jax.ShapeDtypeStruct((t // tm, dv, tm), BF16)],
        scratch_shapes=[pltpu.VMEM((8, dc), F32)],
        compiler_params=pltpu.CompilerParams(
            dimension_semantics=("arbitrary", "arbitrary"), vmem_limit_bytes=VMEM_LIMIT),
        name="inproj_conv_qknorm",
    )(x2, g, w_in, conv_w, conv_g, qg, kg, gmat)


def _attn_kernel(lp_ref, sg_ref, qT_ref, k_ref, vT_ref, o_ref, m_ref, acc_ref,
                 sa_ref, pb_ref, ab_ref, *, tq, tk, nq, lam_init):
    map_a = slice(0, tq)
    map_b = slice(tq, 2 * tq)

    lp = lp_ref[...]
    lam = (jnp.exp(jnp.sum(lp[0:1, :] * lp[1:2, :], axis=-1, keepdims=True))
           - jnp.exp(jnp.sum(lp[2:3, :] * lp[3:4, :], axis=-1, keepdims=True)) + lam_init)

    def stacked_queries(qi):
        qT = qT_ref[:, qi * tq:(qi + 1) * tq]
        row = lax.broadcasted_iota(jnp.int32, qT.shape, 0)
        zero = jnp.zeros_like(qT)
        return jnp.concatenate([jnp.where(row < HEAD_DIM, qT, zero),
                                jnp.where(row >= HEAD_DIM, qT, zero)], axis=1)

    def pieces(diag):
        return ((slice(0, tq // 2), tk // 2), (slice(tq // 2, tq), tk)) if diag else ((slice(0, tq), tk),)

    def shifted(cols, off):
        return slice(cols.start + off, cols.stop + off)

    def scores(j, qq, cols, n_keys):
        return _dot(k_ref[j * tk:j * tk + n_keys, :], qq[:, cols])

    def softmax(m, cols, s, q0):
        if q0 is not None:
            qrel = q0 + lax.broadcasted_iota(jnp.int32, s.shape, 1)
            krel = lax.broadcasted_iota(jnp.int32, s.shape, 0)
            s = jnp.where(krel <= qrel, s, -jnp.inf)
        sb = s.astype(BF16)
        m_old = m[:, cols]
        m_new = jnp.maximum(m_old, jnp.max(sb, axis=0, keepdims=True).astype(F32))
        alpha = jnp.exp2(m_old - m_new)
        p = jnp.exp2(sb - m_new.astype(BF16))
        m[:, cols] = m_new
        return alpha, p

    def accumulate(acc, cols, alpha, vb, p):
        acc[:, cols] = alpha * acc[:, cols] + _dot(vb, p)

    def prefetch_map_a(sa, j, qq, diag):
        for cols, n_keys in pieces(diag):
            sa[0:n_keys, cols] = scores(j, qq, cols, n_keys)

    stacked = {0: stacked_queries(0)}
    m_ref[0] = jnp.full(m_ref.shape[1:], -jnp.inf, F32)
    acc_ref[0] = jnp.zeros(acc_ref.shape[1:], F32)
    prefetch_map_a(sa_ref.at[0], 0, stacked[0], True)

    for qi in range(nq):
        par = qi % 2
        m, acc, sa, qq = m_ref.at[par], acc_ref.at[par], sa_ref.at[par], stacked[qi]
        for j in range(qi + 1):
            last = j == qi
            if j > 0:
                accumulate(acc, map_b, ab_ref[...], vT_ref[j - 1], pb_ref[...])
            s_b = [scores(j, qq, shifted(cols, tq), n_keys) for cols, n_keys in pieces(last)]
            for cols, n_keys in pieces(last):
                alpha_a, p_a = softmax(m, cols, sa[0:n_keys, cols], cols.start if last else None)
                accumulate(acc, cols, alpha_a, vT_ref[j, :, 0:n_keys], p_a)
            if not last:
                prefetch_map_a(sa, j + 1, qq, j + 1 == qi)
            elif qi + 1 < nq:
                stacked[qi + 1] = stacked_queries(qi + 1)
                m_ref[1 - par] = jnp.full(m_ref.shape[1:], -jnp.inf, F32)
                acc_ref[1 - par] = jnp.zeros(acc_ref.shape[1:], F32)
                prefetch_map_a(sa_ref.at[1 - par], 0, stacked[qi + 1], False)
            for (cols, n_keys), s in zip(pieces(last), s_b):
                alpha_b, p_b = softmax(m, shifted(cols, tq), s, cols.start if last else None)
                if last:
                    accumulate(acc, shifted(cols, tq), alpha_b, vT_ref[j, :, 0:n_keys], p_b)
                else:
                    ab_ref[...] = alpha_b
                    pb_ref[...] = p_b

        o = acc[0:V_DIM, :] / acc[V_DIM:V_DIM + 1, :]
        d = o[:, map_a] - lam * o[:, map_b]
        ms = jnp.mean(d * d, axis=0, keepdims=True)
        o_ref[qi * tq:(qi + 1) * tq, :] = (
            d * lax.rsqrt(ms + EPS) * sg_ref[...] * (1.0 - lam_init)).T.astype(BF16)
        del stacked[qi]


def _attention(lam_params, subln_g, qT, k, vT, *, batch, seq, lam_init):
    dq, t = qT.shape
    n_heads = dq // V_DIM
    tq, tk = TQ, TK
    assert tq == tk and vT.shape[2] == tk
    nq = seq // tq
    nk = seq // tk
    vrows = V_DIM + ONES_ROWS
    const = lambda b, h: (0, 0)
    return pl.pallas_call(
        functools.partial(_attn_kernel, tq=tq, tk=tk, nq=nq, lam_init=lam_init),
        grid=(batch, n_heads),
        in_specs=[
            pl.BlockSpec(lam_params.shape, const),
            pl.BlockSpec((V_DIM, 1), const),
            pl.BlockSpec((V_DIM, seq), lambda b, h: (h, b)),
            pl.BlockSpec((seq, V_DIM), lambda b, h: (b, h)),
            pl.BlockSpec((nk, vrows, tk), lambda b, h: (b, h, 0)),
        ],
        out_specs=pl.BlockSpec((seq, V_DIM), lambda b, h: (b, h)),
        out_shape=jax.ShapeDtypeStruct((t, dq), BF16),
        scratch_shapes=[pltpu.VMEM((2, 1, 2 * tq), F32),
                        pltpu.VMEM((2, vrows, 2 * tq), F32),
                        pltpu.VMEM((2, tk, tq), F32), pltpu.VMEM((tk, tq), BF16), pltpu.VMEM((1, tq), F32)],
        compiler_params=pltpu.CompilerParams(
            dimension_semantics=("arbitrary", "arbitrary"), vmem_limit_bytes=VMEM_LIMIT),
        name="diff_attention",
    )(lam_params, subln_g, qT, k, vT)


def _outproj_router_kernel(x_ref, yc_ref, at_ref, wo_ref, g_ref, wr_ref,
                           h_ref, hn_ref, route_ref, routeT_ref, cnt_ref, *, tm, dc):
    i = pl.program_id(0)

    @pl.when(i == 0)
    def _():
        cnt_ref[...] = jnp.zeros_like(cnt_ref)

    for r0 in range(0, x_ref.shape[0], tm):
        _outproj_router_subtile(slice(r0, r0 + tm), x_ref, yc_ref, at_ref, wo_ref, g_ref, wr_ref,
                                h_ref, hn_ref, route_ref, routeT_ref, cnt_ref, tm=tm, dc=dc)


def _outproj_router_subtile(rows, x_ref, yc_ref, at_ref, wo_ref, g_ref, wr_ref,
                            h_ref, hn_ref, route_ref, routeT_ref, cnt_ref, *, tm, dc):
    h = x_ref[rows, :] + _dot(yc_ref[rows, :], wo_ref[0:dc, :]) + _dot(at_ref[rows, :], wo_ref[dc:2 * dc, :])
    h_ref[rows, :] = h
    ms = jnp.mean(h * h, axis=-1, keepdims=True)
    hn = h * lax.rsqrt(ms + EPS) * g_ref[...]
    hi = hn.astype(BF16)
    hn_ref[rows, :] = _pack_rows(hn)
    lo = (hn - hi.astype(F32)).astype(BF16)
    prod = _dot(hi, wr_ref[...])
    logits = prod[:, 0:LANES] + prod[:, LANES:2 * LANES] + _dot(lo, wr_ref[:, 0:LANES])

    lt = logits.T
    neg = -jnp.inf
    grow = lax.broadcasted_iota(jnp.int32, (8, tm), 0).astype(F32)
    gl = jnp.where(grow < N_GROUPS, lt[N_EXPERTS:N_EXPERTS + 8, :], neg)
    gmax = jnp.max(gl, axis=0, keepdims=True)
    g_gate = 1.0 / jnp.sum(jnp.exp(gl - gmax), axis=0, keepdims=True)
    g_idx = jnp.min(jnp.where(gl == gmax, grow, 1e9), axis=0, keepdims=True)
    erow = lax.broadcasted_iota(jnp.int32, (N_EXPERTS, tm), 0).astype(F32)
    e_lo = g_idx * EXPERTS_PER_GROUP
    el = jnp.where((erow >= e_lo) & (erow < e_lo + EXPERTS_PER_GROUP), lt[0:N_EXPERTS, :], neg)
    v1 = jnp.max(el, axis=0, keepdims=True)
    i1 = jnp.min(jnp.where(el == v1, erow, 1e9), axis=0, keepdims=True)
    el2 = jnp.where(erow == i1, neg, el)
    v2 = jnp.max(el2, axis=0, keepdims=True)
    i2 = jnp.min(jnp.where(el2 == v2, erow, 1e9), axis=0, keepdims=True)
    tt = jnp.exp(v2 - v1)
    w1 = g_gate / (1.0 + tt)
    w2 = g_gate * tt / (1.0 + tt)

    sel1 = erow == i1
    sel2 = erow == i2
    oh = jnp.where(sel1 | sel2, 1.0, 0.0)
    ss = lax.broadcasted_iota(jnp.int32, (tm, tm), 0)
    tt_i = lax.broadcasted_iota(jnp.int32, (tm, tm), 1)
    earlier = jnp.where(ss < tt_i, 1.0, 0.0).astype(BF16)
    ranks = _dot(oh.astype(BF16), earlier) + cnt_ref[...]
    r1 = jnp.sum(jnp.where(sel1, ranks, 0.0), axis=0, keepdims=True)
    r2 = jnp.sum(jnp.where(sel2, ranks, 0.0), axis=0, keepdims=True)
    cnt_ref[...] = cnt_ref[...] + jnp.sum(oh, axis=1, keepdims=True)

    routeT = jnp.concatenate([i1, i2, r1, r2, w1, w2, jnp.zeros((2, tm), F32)], axis=0)
    routeT_ref[:, rows] = routeT
    route_ref[rows, :] = jnp.concatenate([routeT, jnp.zeros((LANES - 8, tm), F32)], axis=0).T


def _outproj_router(x2, yc, at, w_out, g, wr_cat):
    t, d = x2.shape
    dc = yc.shape[1]
    tm = TM_PROJ
    tb = tm * OUTPROJ_SUBTILES
    row = lambda i: (i, 0)
    const = lambda i: (0, 0)
    return pl.pallas_call(
        functools.partial(_outproj_router_kernel, tm=tm, dc=dc),
        grid=(t // tb,),
        in_specs=[
            pl.BlockSpec((tb, d), row),
            pl.BlockSpec((tb, dc), row),
            pl.BlockSpec((tb, dc), row),
            pl.BlockSpec(w_out.shape, const),
            pl.BlockSpec((1, d), const),
            pl.BlockSpec(wr_cat.shape, const),
        ],
        out_specs=[
            pl.BlockSpec((tb, d), row),
            pl.BlockSpec((tb, d // 2), row),
            pl.BlockSpec((tb, LANES), row),
            pl.BlockSpec((8, tb), lambda i: (0, i)),
            pl.BlockSpec((N_EXPERTS, 1), const),
        ],
        out_shape=[
            jax.ShapeDtypeStruct((t, d), F32),
            jax.ShapeDtypeStruct((t, d // 2), jnp.int32),
            jax.ShapeDtypeStruct((t, LANES), F32),
            jax.ShapeDtypeStruct((8, t), F32),
            jax.ShapeDtypeStruct((N_EXPERTS, 1), F32),
        ],
        compiler_params=pltpu.CompilerParams(
            dimension_semantics=("arbitrary",), vmem_limit_bytes=VMEM_LIMIT),
        name="outproj_router",
    )(x2, yc, at, w_out, g, wr_cat)


def _positions_kernel(xoffs_ref, yoffs_ref, rt_ref, pos_ref):
    rt = rt_ref[...]
    ea, eb = rt[0:1, :], rt[1:2, :]
    zero = jnp.zeros_like(ea)
    xa, xb, ya, yb = zero, zero, zero, zero
    for e in range(N_EXPERTS):
        xs, ys = xoffs_ref[e].astype(F32), yoffs_ref[e].astype(F32)
        xa = jnp.where(ea == e, xs, xa)
        xb = jnp.where(eb == e, xs, xb)
        ya = jnp.where(ea == e, ys, ya)
        yb = jnp.where(eb == e, ys, yb)
    pos_ref[0:1, :] = (xa + rt[2:3, :]).astype(jnp.int32)
    pos_ref[1:2, :] = (xb + rt[3:4, :]).astype(jnp.int32)
    pos_ref[2:3, :] = (ya + rt[2:3, :]).astype(jnp.int32)
    pos_ref[3:4, :] = (yb + rt[3:4, :]).astype(jnp.int32)


def _positions(xoffs, yoffs, routeT):
    t = routeT.shape[1]
    return pl.pallas_call(
        _positions_kernel,
        grid_spec=pltpu.PrefetchScalarGridSpec(
            num_scalar_prefetch=2, grid=(1,),
            in_specs=[pl.BlockSpec(routeT.shape, lambda i, xo, yo: (0, 0))],
            out_specs=pl.BlockSpec((4, t), lambda i, xo, yo: (0, 0)),
        ),
        out_shape=jax.ShapeDtypeStruct((4, t), jnp.int32),
        name="positions",
    )(xoffs, yoffs, routeT)


def _moe_kernel(te_ref, nt_ref, nx_ref, n2_ref, sl_ref, hb_ref, nh_ref, x_ref, wg_hbm, wu_hbm, wd_hbm,
                y_ref, wg_st, wu_st, wd_st, wgb_ref, wub_ref, wdb_ref, sem):
    i = pl.program_id(0)
    last = nt_ref[0] - 1
    ic = jnp.minimum(i, last)
    expert = te_ref[ic]
    slot = sl_ref[ic]
    first_of_expert = (i == 0) | ((i <= last) & (expert != te_ref[jnp.maximum(ic - 1, 0)]))

    def weight_copies(e, s):
        return (pltpu.make_async_copy(wg_hbm.at[e], wg_st.at[s], sem.at[s, 0]),
                pltpu.make_async_copy(wu_hbm.at[e], wu_st.at[s], sem.at[s, 1]),
                pltpu.make_async_copy(wd_hbm.at[e], wd_st.at[s], sem.at[s, 2]))

    def next_slot(s):
        return jnp.where(s == WEIGHT_SLOTS - 1, 0, s + 1)

    @pl.when(i == 0)
    def _():
        for c in weight_copies(expert, slot):
            c.start()
        nxt = nx_ref[ic]

        @pl.when(nxt >= 0)
        def _():
            for c in weight_copies(nxt, next_slot(slot)):
                c.start()

    @pl.when(first_of_expert)
    def _():
        for c in weight_copies(expert, slot):
            c.wait()
        ahead = n2_ref[ic]

        @pl.when(ahead >= 0)
        def _():
            for c in weight_copies(ahead, next_slot(next_slot(slot))):
                c.start()

        wgb_ref[...] = wg_st[slot].astype(BF16)
        wub_ref[...] = wu_st[slot].astype(BF16)
        wdb_ref[...] = wd_st[slot].astype(BF16)

    def expert_mlp(x_packed):
        x_l, x_r = _unpack_rows(x_packed)
        x_l = x_l.astype(BF16)
        x_r = x_r.astype(BF16)
        half = x_l.shape[1]
        hg = _dot(x_l, wgb_ref[0:half, :]) + _dot(x_r, wgb_ref[half:2 * half, :])
        hu = _dot(x_l, wub_ref[0:half, :]) + _dot(x_r, wub_ref[half:2 * half, :])
        act = hg * (1.0 / (1.0 + jnp.exp(-hg))) * hu
        return _pack_rows(_dot(act.astype(BF16), wdb_ref[...]))

    n_half = nh_ref[ic]
    th = x_ref.shape[0] // 2

    @pl.when((i <= last) & (n_half == 2))
    def _():
        y_ref[...] = expert_mlp(x_ref[...])

    @pl.when((i <= last) & (n_half == 1))
    def _():
        y_ref[0:th, :] = expert_mlp(x_ref[0:th, :])


def _moe(tile_expert, n_tiles, next_expert, next2_expert, stage_slot, half_block, n_halves, xs, wg, wu, wd):
    dp = xs.shape[1]
    d, f = wg.shape[1], wg.shape[2]
    assert dp * 2 == d
    th = TM_MOE // 2
    n_tiles_max = tile_expert.shape[0]

    def tile(s, nt):
        return jnp.minimum(s, nt[0] - 1)

    window = pl.BlockSpec((pl.Element(TM_MOE), pl.Element(dp)),
                          lambda s, te, nt, nx, n2, sl, hb, nh: (hb[tile(s, nt)] * th, 0))
    hbm = pl.BlockSpec(memory_space=pl.ANY)
    grid_spec = pltpu.PrefetchScalarGridSpec(
        num_scalar_prefetch=7,
        grid=(n_tiles_max,),
        in_specs=[window, hbm, hbm, hbm],
        out_specs=pl.BlockSpec((TM_MOE, dp), lambda s, te, nt, nx, n2, sl, hb, nh: (tile(s, nt), 0)),
        scratch_shapes=[pltpu.VMEM((WEIGHT_SLOTS, d, f), F32), pltpu.VMEM((WEIGHT_SLOTS, d, f), F32),
                        pltpu.VMEM((WEIGHT_SLOTS, f, d), F32),
                        pltpu.VMEM((d, f), BF16), pltpu.VMEM((d, f), BF16), pltpu.VMEM((f, d), BF16),
                        pltpu.SemaphoreType.DMA((WEIGHT_SLOTS, 3))],
    )
    return pl.pallas_call(
        _moe_kernel,
        grid_spec=grid_spec,
        out_shape=jax.ShapeDtypeStruct((n_tiles_max * TM_MOE, dp), jnp.int32),
        compiler_params=pltpu.CompilerParams(
            dimension_semantics=("arbitrary",), vmem_limit_bytes=VMEM_LIMIT),
        name="moe_experts",
    )(tile_expert, n_tiles, next_expert, next2_expert, stage_slot, half_block, n_halves, xs, wg, wu, wd)


def _sc_mesh():
    return plsc.VectorSubcoreMesh(core_axis_name="c", subcore_axis_name="s",
                                  num_cores=SC_CORES, num_subcores=SC_SUBCORES)


def _sc_dispatch(rows, pos_a, pos_b, n_out):
    t, d = rows.shape
    win = pos_a.shape[1]

    @functools.partial(pl.kernel, out_type=jax.ShapeDtypeStruct((n_out, d), rows.dtype),
                       mesh=_sc_mesh(), scratch_types=[], name="sc_dispatch")
    def run(rows_hbm, pa_hbm, pb_hbm, out_hbm):
        def body(rows_vmem, pa_vmem, pb_vmem):
            pltpu.sync_copy(rows_vmem, out_hbm.at[pa_vmem.at[0]])
            pltpu.sync_copy(rows_vmem, out_hbm.at[pb_vmem.at[0]])

        pltpu.emit_pipeline(
            body, grid=(t // win,),
            in_specs=[pl.BlockSpec((win, d), lambda i: (i, 0)),
                      pl.BlockSpec((1, win), lambda i: (i, 0)),
                      pl.BlockSpec((1, win), lambda i: (i, 0))],
            out_specs=[],
            core_axis_name=("c", "s"),
            dimension_semantics=(pltpu.PARALLEL,),
        )(rows_hbm, pa_hbm, pb_hbm)

    return run(rows, pos_a, pos_b)


def _sc_gather(table, idx):
    d = table.shape[1]
    n_win, win = idx.shape

    @functools.partial(pl.kernel, out_type=jax.ShapeDtypeStruct((n_win * win, d), table.dtype),
                       mesh=_sc_mesh(), scratch_types=[], name="sc_gather")
    def run(table_hbm, idx_hbm, out_hbm):
        def body(idx_vmem, out_vmem):
            pltpu.sync_copy(table_hbm.at[idx_vmem.at[0]], out_vmem)

        pltpu.emit_pipeline(
            body, grid=(n_win,),
            in_specs=[pl.BlockSpec((1, win), lambda i: (i, 0))],
            out_specs=[pl.BlockSpec((win, d), lambda i: (i, 0))],
            core_axis_name=("c", "s"),
            dimension_semantics=(pltpu.PARALLEL,),
        )(idx_hbm, out_hbm)

    return run(table, idx)


def _combine_kernel(h_ref, ya_ref, yb_ref, r_ref, o_ref):
    r = r_ref[...]
    wa, wb = r[:, 4:5], r[:, 5:6]
    a_l, a_r = _unpack_rows(ya_ref[...])
    b_l, b_r = _unpack_rows(yb_ref[...])
    half = a_l.shape[1]
    o_ref[:, 0:half] = h_ref[:, 0:half] + wa * a_l + wb * b_l
    o_ref[:, half:2 * half] = h_ref[:, half:2 * half] + wa * a_r + wb * b_r


def _combine(hres, yg, route):
    t, d = hres.shape
    tm = TM_COMBINE
    nb = t // tm
    return pl.pallas_call(
        _combine_kernel,
        grid=(nb,),
        in_specs=[pl.BlockSpec((tm, d), lambda i: (i, 0)),
                  pl.BlockSpec((tm, d // 2), lambda i: (i, 0)),
                  pl.BlockSpec((tm, d // 2), lambda i: (i + nb, 0)),
                  pl.BlockSpec((tm, LANES), lambda i: (i, 0))],
        out_specs=pl.BlockSpec((tm, d), lambda i: (i, 0)),
        out_shape=jax.ShapeDtypeStruct((t, d), F32),
        compiler_params=pltpu.CompilerParams(
            dimension_semantics=("arbitrary",), vmem_limit_bytes=VMEM_LIMIT),
        name="combine",
    )(hres, yg, yg, route)


def _lambda_init(layer_idx):
    return 0.8 - 0.6 * math.exp(-0.3 * layer_idx)


def _layer(h, l, attn_norm_g, w_in, conv_w, conv_out_g, q_norm_g, k_norm_g,
           lambda_q1, lambda_k1, lambda_q2, lambda_k2, attn_subln_g, w_out,
           ffn_norm_g, w_router_group, w_router_expert, w_exp_gate, w_exp_up, w_exp_down):
    batch, seq, d = h.shape
    t = batch * seq
    dc = conv_w.shape[-1]
    lam_init = _lambda_init(l)
    x2 = h.reshape(t, d)

    reps = dc // HEAD_DIM
    assert dc // CONV_GROUPS == HEAD_DIM
    qg = (jnp.tile(q_norm_g[l], reps) * (HEAD_DIM ** -0.5 * math.log2(math.e))).reshape(1, dc)
    kg = jnp.tile(k_norm_g[l], reps).reshape(1, dc)
    grp = jnp.arange(MXU_TILE) // HEAD_DIM
    gmat = jnp.where(grp[:, None] == grp[None, :], 1.0 / HEAD_DIM, 0.0).astype(BF16)
    yc, qT, k, vT = _inproj(x2, attn_norm_g[l].reshape(1, d), w_in[l].astype(BF16), conv_w[l],
                          conv_out_g[l].reshape(1, dc), qg, kg, gmat, batch=batch, seq=seq)

    lam_params = jnp.stack([lambda_q1[l], lambda_k1[l], lambda_q2[l], lambda_k2[l]])
    at = _attention(lam_params, attn_subln_g[l].reshape(V_DIM, 1), qT, k, vT,
                    batch=batch, seq=seq, lam_init=lam_init)

    wr = jnp.concatenate([w_router_expert[l], w_router_group[l],
                          jnp.zeros((d, LANES - N_EXPERTS - N_GROUPS), F32)], axis=1)
    wr_hi = wr.astype(BF16)
    wr_lo = (wr - wr_hi.astype(F32)).astype(BF16)
    hres, hn2, route, routeT, cnt = _outproj_router(x2, yc, at, w_out[l].astype(BF16),
                                                    ffn_norm_g[l].reshape(1, d),
                                                    jnp.concatenate([wr_hi, wr_lo], axis=1))

    th = TM_MOE // 2
    n_half_max = (2 * t) // th + N_EXPERTS
    n_tiles_max = (n_half_max + N_EXPERTS) // 2
    counts = cnt[:, 0].astype(jnp.int32)
    halves = (counts + th - 1) // th
    half_start = jnp.cumsum(halves) - halves
    tiles = (halves + 1) // 2
    tile_end = jnp.cumsum(tiles)
    tile_start = tile_end - tiles
    pos = _positions(half_start * th, tile_start * TM_MOE, routeT)
    n_tiles = tile_end[-1:].astype(jnp.int32)
    tile_ids = jnp.arange(n_tiles_max, dtype=jnp.int32)
    tile_expert = jnp.minimum(
        jnp.sum((tile_end[None, :] <= tile_ids[:, None]).astype(jnp.int32), axis=1), N_EXPERTS - 1)
    e_ids = jnp.arange(N_EXPERTS, dtype=jnp.int32)
    nonempty = tiles > 0
    later = jnp.where(nonempty[None, :] & (e_ids[None, :] > e_ids[:, None]), e_ids[None, :], N_EXPERTS)
    next_of = jnp.min(later, axis=1)
    next_of = jnp.where(next_of == N_EXPERTS, -1, next_of).astype(jnp.int32)
    next2_of = jnp.where(next_of >= 0, next_of[jnp.maximum(next_of, 0)], -1)
    slot_of = ((jnp.cumsum(nonempty.astype(jnp.int32)) - 1) % WEIGHT_SLOTS).astype(jnp.int32)
    onehot_te = (tile_expert[:, None] == e_ids[None, :]).astype(jnp.int32)
    next_expert = jnp.sum(onehot_te * next_of[None, :], axis=1)
    next2_expert = jnp.sum(onehot_te * next2_of[None, :], axis=1)
    stage_slot = jnp.sum(onehot_te * slot_of[None, :], axis=1)
    tile_in_expert = tile_ids - jnp.sum(onehot_te * tile_start[None, :], axis=1)
    half_block = jnp.sum(onehot_te * half_start[None, :], axis=1) + 2 * tile_in_expert
    n_halves = jnp.clip(jnp.sum(onehot_te * halves[None, :], axis=1) - 2 * tile_in_expert, 1, 2)

    posw = pos.reshape(4 * t // SC_WIN, SC_WIN)
    nw = t // SC_WIN
    xs = _sc_dispatch(hn2, posw[0:nw], posw[nw:2 * nw], (n_half_max + 1) * th)

    f = w_exp_gate.shape[-1]
    ys = _moe(tile_expert, n_tiles, next_expert, next2_expert, stage_slot, half_block, n_halves, xs,
              w_exp_gate[l].reshape(N_EXPERTS, d, f),
              w_exp_up[l].reshape(N_EXPERTS, d, f),
              w_exp_down[l].reshape(N_EXPERTS, f, d))
    yg = _sc_gather(ys, posw[2 * nw:4 * nw])
    out = _combine(hres, yg, route)
    return out.reshape(batch, seq, d)


def kernel(x, attn_norm_g, w_in, conv_w, conv_out_g, q_norm_g, k_norm_g, lambda_q1, lambda_k1,
           lambda_q2, lambda_k2, attn_subln_g, w_out, ffn_norm_g, w_router_group, w_router_expert,
           w_exp_gate, w_exp_up, w_exp_down):
    h = x
    for l in range(attn_norm_g.shape[0]):
        h = _layer(h, l, attn_norm_g, w_in, conv_w, conv_out_g, q_norm_g, k_norm_g,
                   lambda_q1, lambda_k1, lambda_q2, lambda_k2, attn_subln_g, w_out,
                   ffn_norm_g, w_router_group, w_router_expert, w_exp_gate, w_exp_up, w_exp_down)
    return h
```

```python
import functools
import math

import jax
import jax.numpy as jnp
from jax import lax
from jax.experimental import pallas as pl
from jax.experimental.pallas import tpu as pltpu
from jax.experimental.pallas import tpu_sc as plsc

F32 = jnp.float32
BF16 = jnp.bfloat16

HEAD_DIM = 64
V_DIM = 2 * HEAD_DIM
CONV_GROUPS = 8
N_GROUPS = 4
EXPERTS_PER_GROUP = 8
N_EXPERTS = N_GROUPS * EXPERTS_PER_GROUP
EPS = 1e-6
LANES = 128
MXU_TILE = 256
ONES_ROWS = 16
VMEM_LIMIT = 48 * 1024 * 1024

TM_PROJ = 256
TM_COMBINE = 1024
OUTPROJ_SUBTILES = 4
INPROJ_SUBTILES = 2
TQ = 512
TK = 512
TM_MOE = 512
MOE_TILES_PER_STEP = 2
SC_CORES = 2
SC_SUBCORES = 16
SC_WIN = 64


def _dot(a, b):
    return jnp.dot(a, b, preferred_element_type=F32)


def _pack_rows(x):
    w = x.shape[1] // 2
    bits = lax.bitcast_convert_type(x.astype(BF16).astype(F32), jnp.uint32)
    return lax.bitcast_convert_type((bits[:, :w] >> 16) | bits[:, w:], jnp.int32)


def _unpack_rows(packed):
    bits = lax.bitcast_convert_type(packed, jnp.uint32)
    left = lax.bitcast_convert_type(bits << 16, F32)
    right = lax.bitcast_convert_type(bits & jnp.uint32(0xFFFF0000), F32)
    return left, right


def _group_mean(sq, gm):
    w = gm.shape[0]
    sq = sq.astype(BF16)
    return jnp.concatenate([_dot(sq[:, c:c + w], gm) for c in range(0, sq.shape[1], w)], axis=1)


def _inproj_kernel(x_ref, g_ref, w_ref, cw_ref, cg_ref, qg_ref, kg_ref, gm_ref,
                   yc_ref, qT_ref, k_ref, vT_ref, carry_ref, *, tm, dc):
    j = pl.program_id(1)

    @pl.when(j == 0)
    def _():
        carry_ref[...] = jnp.zeros_like(carry_ref)

    for u in range(x_ref.shape[0] // tm):
        _inproj_subtile(u, x_ref, g_ref, w_ref, cw_ref, cg_ref, qg_ref, kg_ref, gm_ref,
                        yc_ref, qT_ref, k_ref, vT_ref, carry_ref, tm=tm, dc=dc)


def _inproj_subtile(u, x_ref, g_ref, w_ref, cw_ref, cg_ref, qg_ref, kg_ref, gm_ref,
                    yc_ref, qT_ref, k_ref, vT_ref, carry_ref, *, tm, dc):
    rows = slice(u * tm, (u + 1) * tm)
    x = x_ref[rows, :]
    ms = jnp.mean(x * x, axis=-1, keepdims=True)
    hn = (x * lax.rsqrt(ms + EPS) * g_ref[...]).astype(BF16)

    def proj(s):
        return _dot(hn, w_ref[:, s * dc:(s + 1) * dc])

    gm = gm_ref[...]

    c = proj(2) * proj(0)
    prev = carry_ref[...]
    r = lax.broadcasted_iota(jnp.int32, c.shape, 0)
    c1 = jnp.where(r == 0, prev[7:8, :], pltpu.roll(c, 1, 0))
    c2 = jnp.where(r == 0, prev[6:7, :], jnp.where(r == 1, prev[7:8, :], pltpu.roll(c, 2, 0)))
    carry_ref[...] = c[tm - 8:tm, :]
    cw = cw_ref[...]
    y = proj(1) * (cw[0:1, :] * c2 + cw[1:2, :] * c1 + cw[2:3, :] * c)
    yc_ref[rows, :] = (y * lax.rsqrt(_group_mean(y * y, gm) + EPS) * cg_ref[...]).astype(BF16)

    q = proj(3)
    qT_ref[:, rows] = (q * lax.rsqrt(_group_mean(q * q, gm) + EPS) * qg_ref[...]).T.astype(BF16)
    k = proj(4)
    k_ref[rows, :] = (k * lax.rsqrt(_group_mean(k * k, gm) + EPS) * kg_ref[...]).astype(BF16)
    vt = proj(5).T.astype(BF16)
    vrows = V_DIM + ONES_ROWS
    for h in range(dc // V_DIM):
        vT_ref[u, h * vrows:h * vrows + V_DIM, :] = vt[h * V_DIM:(h + 1) * V_DIM, :]
        vT_ref[u, h * vrows + V_DIM:(h + 1) * vrows, :] = jnp.ones((ONES_ROWS, tm), BF16)


def _inproj(x2, g, w_in, conv_w, conv_g, qg, kg, gmat, *, batch, seq):
    t, d = x2.shape
    dc = conv_g.shape[1]
    dv = dc // V_DIM * (V_DIM + ONES_ROWS)
    tm = TK
    tb = tm * INPROJ_SUBTILES
    nj = seq // tb
    row = lambda b, j: (b * nj + j, 0)
    const = lambda b, j: (0, 0)
    out_sds = jax.ShapeDtypeStruct((t, dc), BF16)
    return pl.pallas_call(
        functools.partial(_inproj_kernel, tm=tm, dc=dc),
        grid=(batch, nj),
        in_specs=[
            pl.BlockSpec((tb, d), row),
            pl.BlockSpec((1, d), const),
            pl.BlockSpec(w_in.shape, const),
            pl.BlockSpec(conv_w.shape, const),
            pl.BlockSpec((1, dc), const),
            pl.BlockSpec((1, dc), const),
            pl.BlockSpec((1, dc), const),
            pl.BlockSpec(gmat.shape, const),
        ],
        out_specs=[
            pl.BlockSpec((tb, dc), row),
            pl.BlockSpec((dc, tb), lambda b, j: (0, b * nj + j)),
            pl.BlockSpec((tb, dc), row),
            pl.BlockSpec((INPROJ_SUBTILES, dv, tm), lambda b, j: (b * nj + j, 0, 0)),
        ],
        out_shape=[out_sds, jax.ShapeDtypeStruct((dc, t), BF16), out_sds,
                   jax.ShapeDtypeStruct((t // tm, dv, tm), BF16)],
        scratch_shapes=[pltpu.VMEM((8, dc), F32)],
        compiler_params=pltpu.CompilerParams(
            dimension_semantics=("arbitrary", "arbitrary"), vmem_limit_bytes=VMEM_LIMIT),
        name="inproj_conv_qknorm",
    )(x2, g, w_in, conv_w, conv_g, qg, kg, gmat)


def _attn_kernel(lp_ref, sg_ref, qT_ref, k_ref, vT_ref, o_ref, m_ref, acc_ref,
                 sa_ref, pb_ref, ab_ref, *, tq, tk, nq, lam_init):
    map_a = slice(0, tq)
    map_b = slice(tq, 2 * tq)

    lp = lp_ref[...]
    lam = (jnp.exp(jnp.sum(lp[0:1, :] * lp[1:2, :], axis=-1, keepdims=True))
           - jnp.exp(jnp.sum(lp[2:3, :] * lp[3:4, :], axis=-1, keepdims=True)) + lam_init)

    def stacked_queries(qi):
        qT = qT_ref[:, qi * tq:(qi + 1) * tq]
        row = lax.broadcasted_iota(jnp.int32, qT.shape, 0)
        zero = jnp.zeros_like(qT)
        return jnp.concatenate([jnp.where(row < HEAD_DIM, qT, zero),
                                jnp.where(row >= HEAD_DIM, qT, zero)], axis=1)

    def pieces(diag):
        return ((slice(0, tq // 2), tk // 2), (slice(tq // 2, tq), tk)) if diag else ((slice(0, tq), tk),)

    def shifted(cols, off):
        return slice(cols.start + off, cols.stop + off)

    def scores(j, qq, cols, n_keys):
        return _dot(k_ref[j * tk:j * tk + n_keys, :], qq[:, cols])

    def softmax(m, cols, s, q0):
        if q0 is not None:
            qrel = q0 + lax.broadcasted_iota(jnp.int32, s.shape, 1)
            krel = lax.broadcasted_iota(jnp.int32, s.shape, 0)
            s = jnp.where(krel <= qrel, s, -jnp.inf)
        sb = s.astype(BF16)
        m_old = m[:, cols]
        m_new = jnp.maximum(m_old, jnp.max(sb, axis=0, keepdims=True).astype(F32))
        alpha = jnp.exp2(m_old - m_new)
        p = jnp.exp2(sb - m_new.astype(BF16))
        m[:, cols] = m_new
        return alpha, p

    def accumulate(acc, cols, alpha, vb, p):
        acc[:, cols] = alpha * acc[:, cols] + _dot(vb, p)

    def prefetch_map_a(sa, j, qq, diag):
        for cols, n_keys in pieces(diag):
            sa[0:n_keys, cols] = scores(j, qq, cols, n_keys)

    stacked = {0: stacked_queries(0)}
    m_ref[0] = jnp.full(m_ref.shape[1:], -jnp.inf, F32)
    acc_ref[0] = jnp.zeros(acc_ref.shape[1:], F32)
    prefetch_map_a(sa_ref.at[0], 0, stacked[0], True)

    for qi in range(nq):
        par = qi % 2
        m, acc, sa, qq = m_ref.at[par], acc_ref.at[par], sa_ref.at[par], stacked[qi]
        for j in range(qi + 1):
            last = j == qi
            if j > 0:
                accumulate(acc, map_b, ab_ref[...], vT_ref[j - 1], pb_ref[...])
            s_b = [scores(j, qq, shifted(cols, tq), n_keys) for cols, n_keys in pieces(last)]
            for cols, n_keys in pieces(last):
                alpha_a, p_a = softmax(m, cols, sa[0:n_keys, cols], cols.start if last else None)
                accumulate(acc, cols, alpha_a, vT_ref[j, :, 0:n_keys], p_a)
            if not last:
                prefetch_map_a(sa, j + 1, qq, j + 1 == qi)
            elif qi + 1 < nq:
                stacked[qi + 1] = stacked_queries(qi + 1)
                m_ref[1 - par] = jnp.full(m_ref.shape[1:], -jnp.inf, F32)
                acc_ref[1 - par] = jnp.zeros(acc_ref.shape[1:], F32)
                prefetch_map_a(sa_ref.at[1 - par], 0, stacked[qi + 1], False)
            for (cols, n_keys), s in zip(pieces(last), s_b):
                alpha_b, p_b = softmax(m, shifted(cols, tq), s, cols.start if last else None)
                if last:
                    accumulate(acc, shifted(cols, tq), alpha_b, vT_ref[j, :, 0:n_keys], p_b)
                else:
                    ab_ref[...] = alpha_b
                    pb_ref[...] = p_b

        o = acc[0:V_DIM, :] / acc[V_DIM:V_DIM + 1, :]
        d = o[:, map_a] - lam * o[:, map_b]
        ms = jnp.mean(d * d, axis=0, keepdims=True)
        o_ref[qi * tq:(qi + 1) * tq, :] = (
            d * lax.rsqrt(ms + EPS) * sg_ref[...] * (1.0 - lam_init)).T.astype(BF16)
        del stacked[qi]


def _attention(lam_params, subln_g, qT, k, vT, *, batch, seq, lam_init):
    dq, t = qT.shape
    n_heads = dq // V_DIM
    tq, tk = TQ, TK
    assert tq == tk and vT.shape[2] == tk
    nq = seq // tq
    nk = seq // tk
    vrows = V_DIM + ONES_ROWS
    const = lambda b, h: (0, 0)
    return pl.pallas_call(
        functools.partial(_attn_kernel, tq=tq, tk=tk, nq=nq, lam_init=lam_init),
        grid=(batch, n_heads),
        in_specs=[
            pl.BlockSpec(lam_params.shape, const),
            pl.BlockSpec((V_DIM, 1), const),
            pl.BlockSpec((V_DIM, seq), lambda b, h: (h, b)),
            pl.BlockSpec((seq, V_DIM), lambda b, h: (b, h)),
            pl.BlockSpec((nk, vrows, tk), lambda b, h: (b, h, 0)),
        ],
        out_specs=pl.BlockSpec((seq, V_DIM), lambda b, h: (b, h)),
        out_shape=jax.ShapeDtypeStruct((t, dq), BF16),
        scratch_shapes=[pltpu.VMEM((2, 1, 2 * tq), F32),
                        pltpu.VMEM((2, vrows, 2 * tq), F32),
                        pltpu.VMEM((2, tk, tq), F32), pltpu.VMEM((tk, tq), BF16), pltpu.VMEM((1, tq), F32)],
        compiler_params=pltpu.CompilerParams(
            dimension_semantics=("arbitrary", "arbitrary"), vmem_limit_bytes=VMEM_LIMIT),
        name="diff_attention",
    )(lam_params, subln_g, qT, k, vT)


def _outproj_router_kernel(x_ref, yc_ref, at_ref, wo_ref, g_ref, wr_ref,
                           h_ref, hn_ref, route_ref, routeT_ref, cnt_ref, *, tm, dc):
    i = pl.program_id(0)

    @pl.when(i == 0)
    def _():
        cnt_ref[...] = jnp.zeros_like(cnt_ref)

    for r0 in range(0, x_ref.shape[0], tm):
        _outproj_router_subtile(slice(r0, r0 + tm), x_ref, yc_ref, at_ref, wo_ref, g_ref, wr_ref,
                                h_ref, hn_ref, route_ref, routeT_ref, cnt_ref, tm=tm, dc=dc)


def _outproj_router_subtile(rows, x_ref, yc_ref, at_ref, wo_ref, g_ref, wr_ref,
                            h_ref, hn_ref, route_ref, routeT_ref, cnt_ref, *, tm, dc):
    h = x_ref[rows, :] + _dot(yc_ref[rows, :], wo_ref[0:dc, :]) + _dot(at_ref[rows, :], wo_ref[dc:2 * dc, :])
    h_ref[rows, :] = h
    ms = jnp.mean(h * h, axis=-1, keepdims=True)
    hn = h * lax.rsqrt(ms + EPS) * g_ref[...]
    hi = hn.astype(BF16)
    hn_ref[rows, :] = _pack_rows(hn)
    lo = (hn - hi.astype(F32)).astype(BF16)
    prod = _dot(hi, wr_ref[...])
    logits = prod[:, 0:LANES] + prod[:, LANES:2 * LANES] + _dot(lo, wr_ref[:, 0:LANES])

    lt = logits.T
    neg = -jnp.inf
    grow = lax.broadcasted_iota(jnp.int32, (8, tm), 0).astype(F32)
    gl = jnp.where(grow < N_GROUPS, lt[N_EXPERTS:N_EXPERTS + 8, :], neg)
    gmax = jnp.max(gl, axis=0, keepdims=True)
    g_gate = 1.0 / jnp.sum(jnp.exp(gl - gmax), axis=0, keepdims=True)
    g_idx = jnp.min(jnp.where(gl == gmax, grow, 1e9), axis=0, keepdims=True)
    erow = lax.broadcasted_iota(jnp.int32, (N_EXPERTS, tm), 0).astype(F32)
    e_lo = g_idx * EXPERTS_PER_GROUP
    el = jnp.where((erow >= e_lo) & (erow < e_lo + EXPERTS_PER_GROUP), lt[0:N_EXPERTS, :], neg)
    v1 = jnp.max(el, axis=0, keepdims=True)
    i1 = jnp.min(jnp.where(el == v1, erow, 1e9), axis=0, keepdims=True)
    el2 = jnp.where(erow == i1, neg, el)
    v2 = jnp.max(el2, axis=0, keepdims=True)
    i2 = jnp.min(jnp.where(el2 == v2, erow, 1e9), axis=0, keepdims=True)
    tt = jnp.exp(v2 - v1)
    w1 = g_gate / (1.0 + tt)
    w2 = g_gate * tt / (1.0 + tt)

    sel1 = erow == i1
    sel2 = erow == i2
    oh = jnp.where(sel1 | sel2, 1.0, 0.0)
    ss = lax.broadcasted_iota(jnp.int32, (tm, tm), 0)
    tt_i = lax.broadcasted_iota(jnp.int32, (tm, tm), 1)
    earlier = jnp.where(ss < tt_i, 1.0, 0.0).astype(BF16)
    ranks = _dot(oh.astype(BF16), earlier) + cnt_ref[...]
    r1 = jnp.sum(jnp.where(sel1, ranks, 0.0), axis=0, keepdims=True)
    r2 = jnp.sum(jnp.where(sel2, ranks, 0.0), axis=0, keepdims=True)
    cnt_ref[...] = cnt_ref[...] + jnp.sum(oh, axis=1, keepdims=True)

    routeT = jnp.concatenate([i1, i2, r1, r2, w1, w2, jnp.zeros((2, tm), F32)], axis=0)
    routeT_ref[:, rows] = routeT
    route_ref[rows, :] = jnp.concatenate([routeT, jnp.zeros((LANES - 8, tm), F32)], axis=0).T


def _outproj_router(x2, yc, at, w_out, g, wr_cat):
    t, d = x2.shape
    dc = yc.shape[1]
    tm = TM_PROJ
    tb = tm * OUTPROJ_SUBTILES
    row = lambda i: (i, 0)
    const = lambda i: (0, 0)
    return pl.pallas_call(
        functools.partial(_outproj_router_kernel, tm=tm, dc=dc),
        grid=(t // tb,),
        in_specs=[
            pl.BlockSpec((tb, d), row),
            pl.BlockSpec((tb, dc), row),
            pl.BlockSpec((tb, dc), row),
            pl.BlockSpec(w_out.shape, const),
            pl.BlockSpec((1, d), const),
            pl.BlockSpec(wr_cat.shape, const),
        ],
        out_specs=[
            pl.BlockSpec((tb, d), row),
            pl.BlockSpec((tb, d // 2), row),
            pl.BlockSpec((tb, LANES), row),
            pl.BlockSpec((8, tb), lambda i: (0, i)),
            pl.BlockSpec((N_EXPERTS, 1), const),
        ],
        out_shape=[
            jax.ShapeDtypeStruct((t, d), F32),
            jax.ShapeDtypeStruct((t, d // 2), jnp.int32),
            jax.ShapeDtypeStruct((t, LANES), F32),
            jax.ShapeDtypeStruct((8, t), F32),
            jax.ShapeDtypeStruct((N_EXPERTS, 1), F32),
        ],
        compiler_params=pltpu.CompilerParams(
            dimension_semantics=("arbitrary",), vmem_limit_bytes=VMEM_LIMIT),
        name="outproj_router",
    )(x2, yc, at, w_out, g, wr_cat)


def _plan_kernel(cnt_ref, rt_ref, pos_ref, te_ref, nx_ref, sl_ref, nt_ref, *, tm):
    shift = tm.bit_length() - 1
    tiles, starts = [], []
    total = jnp.int32(0)
    for e in range(N_EXPERTS):
        n = lax.shift_right_logical(cnt_ref[e] + (tm - 1), shift)
        tiles.append(n)
        starts.append(total)
        total = total + n
    nt_ref[0] = total

    rt = rt_ref[...]
    ea, eb = rt[0:1, :], rt[1:2, :]
    sa = jnp.zeros_like(ea)
    sb = jnp.zeros_like(eb)
    for e in range(N_EXPERTS):
        start = (starts[e] * tm).astype(F32)
        sa = jnp.where(ea == e, start, sa)
        sb = jnp.where(eb == e, start, sb)
    pos_ref[0:1, :] = (sa + rt[2:3, :]).astype(jnp.int32)
    pos_ref[1:2, :] = (sb + rt[3:4, :]).astype(jnp.int32)

    def clear(i, c):
        te_ref[i] = 0
        nx_ref[i] = -1
        sl_ref[i] = 0
        return c

    lax.fori_loop(0, te_ref.shape[0], clear, 0)

    nxt = jnp.int32(-1)
    next_of = [None] * N_EXPERTS
    for e in reversed(range(N_EXPERTS)):
        next_of[e] = nxt
        nxt = jnp.where(tiles[e] > 0, e, nxt)
    ordinal = jnp.int32(0)
    for e in range(N_EXPERTS):
        slot = ordinal & 1

        def fill(j, c, e=e, slot=slot):
            te_ref[starts[e] + j] = e
            nx_ref[starts[e] + j] = next_of[e]
            sl_ref[starts[e] + j] = slot
            return c

        lax.fori_loop(0, tiles[e], fill, 0)
        ordinal = ordinal + (tiles[e] > 0).astype(jnp.int32)


def _plan(counts, routeT, n_tiles_max):
    t = routeT.shape[1]
    smem = pl.BlockSpec(memory_space=pltpu.SMEM)
    table = jax.ShapeDtypeStruct((n_tiles_max,), jnp.int32)
    return pl.pallas_call(
        functools.partial(_plan_kernel, tm=TM_MOE),
        grid_spec=pltpu.PrefetchScalarGridSpec(
            num_scalar_prefetch=1, grid=(1,),
            in_specs=[pl.BlockSpec(routeT.shape, lambda i, cnt: (0, 0))],
            out_specs=[pl.BlockSpec((2, t), lambda i, cnt: (0, 0)), smem, smem, smem, smem],
        ),
        out_shape=[jax.ShapeDtypeStruct((2, t), jnp.int32), table, table, table,
                   jax.ShapeDtypeStruct((1,), jnp.int32)],
        name="routing_plan",
    )(counts, routeT)


def _moe_kernel(te_ref, nt_ref, nx_ref, sl_ref, x_ref, wg_hbm, wu_hbm, wd_hbm, y_ref,
                wg_st, wu_st, wd_st, wgb_ref, wub_ref, wdb_ref, sem, *, tm, tiles_per_step):
    last = nt_ref[0] - 1

    def weight_copies(e, s):
        return (pltpu.make_async_copy(wg_hbm.at[e], wg_st.at[s], sem.at[s, 0]),
                pltpu.make_async_copy(wu_hbm.at[e], wu_st.at[s], sem.at[s, 1]),
                pltpu.make_async_copy(wd_hbm.at[e], wd_st.at[s], sem.at[s, 2]))

    for u in range(tiles_per_step):
        i = pl.program_id(0) * tiles_per_step + u
        rows = slice(u * tm, (u + 1) * tm)
        ic = jnp.minimum(i, last)
        expert = te_ref[ic]
        slot = sl_ref[ic]
        first_of_expert = (i == 0) | ((i <= last) & (expert != te_ref[jnp.maximum(ic - 1, 0)]))

        if u == 0:
            @pl.when(i == 0)
            def _():
                for c in weight_copies(expert, slot):
                    c.start()

        @pl.when(first_of_expert)
        def _():
            for c in weight_copies(expert, slot):
                c.wait()
            nxt = nx_ref[ic]

            @pl.when(nxt >= 0)
            def _():
                for c in weight_copies(nxt, 1 - slot):
                    c.start()

            wgb_ref[...] = wg_st[slot].astype(BF16)
            wub_ref[...] = wu_st[slot].astype(BF16)
            wdb_ref[...] = wd_st[slot].astype(BF16)

        @pl.when(i <= last)
        def _():
            x_l, x_r = _unpack_rows(x_ref[rows, :])
            x_l = x_l.astype(BF16)
            x_r = x_r.astype(BF16)
            half = x_l.shape[1]
            hg = _dot(x_l, wgb_ref[0:half, :]) + _dot(x_r, wgb_ref[half:2 * half, :])
            hu = _dot(x_l, wub_ref[0:half, :]) + _dot(x_r, wub_ref[half:2 * half, :])
            act = hg * (1.0 / (1.0 + jnp.exp(-hg))) * hu
            y_ref[rows, :] = _pack_rows(_dot(act.astype(BF16), wdb_ref[...]))


def _moe(tile_expert, n_tiles, next_expert, stage_slot, xs, wg, wu, wd):
    p, dp = xs.shape
    d, f = wg.shape[1], wg.shape[2]
    assert dp * 2 == d
    tm, tps = TM_MOE, MOE_TILES_PER_STEP
    tb = tm * tps
    row = lambda s, te, nt, nx, sl: (jnp.minimum(s, (nt[0] - 1) // tps), 0)
    hbm = pl.BlockSpec(memory_space=pl.ANY)
    grid_spec = pltpu.PrefetchScalarGridSpec(
        num_scalar_prefetch=4,
        grid=(p // tb,),
        in_specs=[pl.BlockSpec((tb, dp), row), hbm, hbm, hbm],
        out_specs=pl.BlockSpec((tb, dp), row),
        scratch_shapes=[pltpu.VMEM((2, d, f), F32), pltpu.VMEM((2, d, f), F32), pltpu.VMEM((2, f, d), F32),
                        pltpu.VMEM((d, f), BF16), pltpu.VMEM((d, f), BF16), pltpu.VMEM((f, d), BF16),
                        pltpu.SemaphoreType.DMA((2, 3))],
    )
    return pl.pallas_call(
        functools.partial(_moe_kernel, tm=tm, tiles_per_step=tps),
        grid_spec=grid_spec,
        out_shape=jax.ShapeDtypeStruct((p, dp), jnp.int32),
        compiler_params=pltpu.CompilerParams(
            dimension_semantics=("arbitrary",), vmem_limit_bytes=VMEM_LIMIT),
        name="moe_experts",
    )(tile_expert, n_tiles, next_expert, stage_slot, xs, wg, wu, wd)


def _sc_mesh():
    return plsc.VectorSubcoreMesh(core_axis_name="c", subcore_axis_name="s",
                                  num_cores=SC_CORES, num_subcores=SC_SUBCORES)


def _sc_dispatch(rows, pos_a, pos_b, n_out):
    t, d = rows.shape
    win = pos_a.shape[1]

    @functools.partial(pl.kernel, out_type=jax.ShapeDtypeStruct((n_out, d), rows.dtype),
                       mesh=_sc_mesh(), scratch_types=[], name="sc_dispatch")
    def run(rows_hbm, pa_hbm, pb_hbm, out_hbm):
        def body(rows_vmem, pa_vmem, pb_vmem):
            pltpu.sync_copy(rows_vmem, out_hbm.at[pa_vmem.at[0]])
            pltpu.sync_copy(rows_vmem, out_hbm.at[pb_vmem.at[0]])

        pltpu.emit_pipeline(
            body, grid=(t // win,),
            in_specs=[pl.BlockSpec((win, d), lambda i: (i, 0)),
                      pl.BlockSpec((1, win), lambda i: (i, 0)),
                      pl.BlockSpec((1, win), lambda i: (i, 0))],
            out_specs=[],
            core_axis_name=("c", "s"),
            dimension_semantics=(pltpu.PARALLEL,),
        )(rows_hbm, pa_hbm, pb_hbm)

    return run(rows, pos_a, pos_b)


def _sc_gather(table, idx):
    d = table.shape[1]
    n_win, win = idx.shape

    @functools.partial(pl.kernel, out_type=jax.ShapeDtypeStruct((n_win * win, d), table.dtype),
                       mesh=_sc_mesh(), scratch_types=[], name="sc_gather")
    def run(table_hbm, idx_hbm, out_hbm):
        def body(idx_vmem, out_vmem):
            pltpu.sync_copy(table_hbm.at[idx_vmem.at[0]], out_vmem)

        pltpu.emit_pipeline(
            body, grid=(n_win,),
            in_specs=[pl.BlockSpec((1, win), lambda i: (i, 0))],
            out_specs=[pl.BlockSpec((win, d), lambda i: (i, 0))],
            core_axis_name=("c", "s"),
            dimension_semantics=(pltpu.PARALLEL,),
        )(idx_hbm, out_hbm)

    return run(table, idx)


def _combine_kernel(h_ref, ya_ref, yb_ref, r_ref, o_ref):
    r = r_ref[...]
    wa, wb = r[:, 4:5], r[:, 5:6]
    a_l, a_r = _unpack_rows(ya_ref[...])
    b_l, b_r = _unpack_rows(yb_ref[...])
    half = a_l.shape[1]
    o_ref[:, 0:half] = h_ref[:, 0:half] + wa * a_l + wb * b_l
    o_ref[:, half:2 * half] = h_ref[:, half:2 * half] + wa * a_r + wb * b_r


def _combine(hres, yg, route):
    t, d = hres.shape
    tm = TM_COMBINE
    nb = t // tm
    return pl.pallas_call(
        _combine_kernel,
        grid=(nb,),
        in_specs=[pl.BlockSpec((tm, d), lambda i: (i, 0)),
                  pl.BlockSpec((tm, d // 2), lambda i: (i, 0)),
                  pl.BlockSpec((tm, d // 2), lambda i: (i + nb, 0)),
                  pl.BlockSpec((tm, LANES), lambda i: (i, 0))],
        out_specs=pl.BlockSpec((tm, d), lambda i: (i, 0)),
        out_shape=jax.ShapeDtypeStruct((t, d), F32),
        compiler_params=pltpu.CompilerParams(
            dimension_semantics=("arbitrary",), vmem_limit_bytes=VMEM_LIMIT),
        name="combine",
    )(hres, yg, yg, route)


def _lambda_init(layer_idx):
    return 0.8 - 0.6 * math.exp(-0.3 * layer_idx)


def _layer(h, l, attn_norm_g, w_in, conv_w, conv_out_g, q_norm_g, k_norm_g,
           lambda_q1, lambda_k1, lambda_q2, lambda_k2, attn_subln_g, w_out,
           ffn_norm_g, w_router_group, w_router_expert, w_exp_gate, w_exp_up, w_exp_down):
    batch, seq, d = h.shape
    t = batch * seq
    dc = conv_w.shape[-1]
    lam_init = _lambda_init(l)
    x2 = h.reshape(t, d)

    reps = dc // HEAD_DIM
    assert dc // CONV_GROUPS == HEAD_DIM
    qg = (jnp.tile(q_norm_g[l], reps) * (HEAD_DIM ** -0.5 * math.log2(math.e))).reshape(1, dc)
    kg = jnp.tile(k_norm_g[l], reps).reshape(1, dc)
    grp = jnp.arange(MXU_TILE) // HEAD_DIM
    gmat = jnp.where(grp[:, None] == grp[None, :], 1.0 / HEAD_DIM, 0.0).astype(BF16)
    yc, qT, k, vT = _inproj(x2, attn_norm_g[l].reshape(1, d), w_in[l].astype(BF16), conv_w[l],
                          conv_out_g[l].reshape(1, dc), qg, kg, gmat, batch=batch, seq=seq)

    lam_params = jnp.stack([lambda_q1[l], lambda_k1[l], lambda_q2[l], lambda_k2[l]])
    at = _attention(lam_params, attn_subln_g[l].reshape(V_DIM, 1), qT, k, vT,
                    batch=batch, seq=seq, lam_init=lam_init)

    wr = jnp.concatenate([w_router_expert[l], w_router_group[l],
                          jnp.zeros((d, LANES - N_EXPERTS - N_GROUPS), F32)], axis=1)
    wr_hi = wr.astype(BF16)
    wr_lo = (wr - wr_hi.astype(F32)).astype(BF16)
    hres, hn2, route, routeT, cnt = _outproj_router(x2, yc, at, w_out[l].astype(BF16),
                                                    ffn_norm_g[l].reshape(1, d),
                                                    jnp.concatenate([wr_hi, wr_lo], axis=1))

    tmm = TM_MOE
    n_tiles_max = (2 * t) // tmm + N_EXPERTS
    p_rows = n_tiles_max * tmm
    pos, tile_expert, next_expert, stage_slot, n_tiles = _plan(
        cnt[:, 0].astype(jnp.int32), routeT, n_tiles_max)

    posw = pos.reshape(2 * t // SC_WIN, SC_WIN)
    pos1w = posw[:t // SC_WIN]
    pos2w = posw[t // SC_WIN:]
    xs = _sc_dispatch(hn2, pos1w, pos2w, p_rows)

    f = w_exp_gate.shape[-1]
    ys = _moe(tile_expert, n_tiles, next_expert, stage_slot, xs,
              w_exp_gate[l].reshape(N_EXPERTS, d, f),
              w_exp_up[l].reshape(N_EXPERTS, d, f),
              w_exp_down[l].reshape(N_EXPERTS, f, d))
    yg = _sc_gather(ys, posw)
    out = _combine(hres, yg, route)
    return out.reshape(batch, seq, d)


def kernel(x, attn_norm_g, w_in, conv_w, conv_out_g, q_norm_g, k_norm_g, lambda_q1, lambda_k1,
           lambda_q2, lambda_k2, attn_subln_g, w_out, ffn_norm_g, w_router_group, w_router_expert,
           w_exp_gate, w_exp_up, w_exp_down):
    h = x
    for l in range(attn_norm_g.shape[0]):
        h = _layer(h, l, attn_norm_g, w_in, conv_w, conv_out_g, q_norm_g, k_norm_g,
                   lambda_q1, lambda_k1, lambda_q2, lambda_k2, attn_subln_g, w_out,
                   ffn_norm_g, w_router_group, w_router_expert, w_exp_gate, w_exp_up, w_exp_down)
    return h
```

```python
import functools
import math

import jax
import jax.numpy as jnp
from jax import lax
from jax.experimental import pallas as pl
from jax.experimental.pallas import tpu as pltpu
from jax.experimental.pallas import tpu_sc as plsc

F32 = jnp.float32
BF16 = jnp.bfloat16

HEAD_DIM = 64
V_DIM = 2 * HEAD_DIM
CONV_GROUPS = 8
N_GROUPS = 4
EXPERTS_PER_GROUP = 8
N_EXPERTS = N_GROUPS * EXPERTS_PER_GROUP
EPS = 1e-6
LANES = 128
MXU_TILE = 256
ONES_ROWS = 16
VMEM_LIMIT = 48 * 1024 * 1024

TM_PROJ = 256
TM_COMBINE = 1024
OUTPROJ_SUBTILES = 4
INPROJ_SUBTILES = 2
TQ = 512
TK = 512
TM_MOE = 512
MOE_TILES_PER_STEP = 2
SC_CORES = 2
SC_SUBCORES = 16
SC_WIN = 64


def _dot(a, b):
    return jnp.dot(a, b, preferred_element_type=F32)


def _pack_rows(x):
    w = x.shape[1] // 2
    bits = lax.bitcast_convert_type(x.astype(BF16).astype(F32), jnp.uint32)
    return lax.bitcast_convert_type((bits[:, :w] >> 16) | bits[:, w:], jnp.int32)


def _unpack_rows(packed):
    bits = lax.bitcast_convert_type(packed, jnp.uint32)
    left = lax.bitcast_convert_type(bits << 16, F32)
    right = lax.bitcast_convert_type(bits & jnp.uint32(0xFFFF0000), F32)
    return left, right


def _group_mean(sq, gm):
    w = gm.shape[0]
    sq = sq.astype(BF16)
    return jnp.concatenate([_dot(sq[:, c:c + w], gm) for c in range(0, sq.shape[1], w)], axis=1)


def _inproj_kernel(x_ref, g_ref, w_ref, cw_ref, cg_ref, qg_ref, kg_ref, gm_ref,
                   yc_ref, qT_ref, k_ref, vT_ref, carry_ref, *, tm, dc):
    j = pl.program_id(1)

    @pl.when(j == 0)
    def _():
        carry_ref[...] = jnp.zeros_like(carry_ref)

    for u in range(x_ref.shape[0] // tm):
        _inproj_subtile(u, x_ref, g_ref, w_ref, cw_ref, cg_ref, qg_ref, kg_ref, gm_ref,
                        yc_ref, qT_ref, k_ref, vT_ref, carry_ref, tm=tm, dc=dc)


def _inproj_subtile(u, x_ref, g_ref, w_ref, cw_ref, cg_ref, qg_ref, kg_ref, gm_ref,
                    yc_ref, qT_ref, k_ref, vT_ref, carry_ref, *, tm, dc):
    rows = slice(u * tm, (u + 1) * tm)
    x = x_ref[rows, :]
    ms = jnp.mean(x * x, axis=-1, keepdims=True)
    hn = (x * lax.rsqrt(ms + EPS) * g_ref[...]).astype(BF16)

    def proj(s):
        return _dot(hn, w_ref[:, s * dc:(s + 1) * dc])

    gm = gm_ref[...]

    c = proj(2) * proj(0)
    prev = carry_ref[...]
    r = lax.broadcasted_iota(jnp.int32, c.shape, 0)
    c1 = jnp.where(r == 0, prev[7:8, :], pltpu.roll(c, 1, 0))
    c2 = jnp.where(r == 0, prev[6:7, :], jnp.where(r == 1, prev[7:8, :], pltpu.roll(c, 2, 0)))
    carry_ref[...] = c[tm - 8:tm, :]
    cw = cw_ref[...]
    y = proj(1) * (cw[0:1, :] * c2 + cw[1:2, :] * c1 + cw[2:3, :] * c)
    yc_ref[rows, :] = (y * lax.rsqrt(_group_mean(y * y, gm) + EPS) * cg_ref[...]).astype(BF16)

    q = proj(3)
    qT_ref[:, rows] = (q * lax.rsqrt(_group_mean(q * q, gm) + EPS) * qg_ref[...]).T.astype(BF16)
    k = proj(4)
    k_ref[rows, :] = (k * lax.rsqrt(_group_mean(k * k, gm) + EPS) * kg_ref[...]).astype(BF16)
    vt = proj(5).T.astype(BF16)
    vrows = V_DIM + ONES_ROWS
    for h in range(dc // V_DIM):
        vT_ref[u, h * vrows:h * vrows + V_DIM, :] = vt[h * V_DIM:(h + 1) * V_DIM, :]
        vT_ref[u, h * vrows + V_DIM:(h + 1) * vrows, :] = jnp.ones((ONES_ROWS, tm), BF16)


def _inproj(x2, g, w_in, conv_w, conv_g, qg, kg, gmat, *, batch, seq):
    t, d = x2.shape
    dc = conv_g.shape[1]
    dv = dc // V_DIM * (V_DIM + ONES_ROWS)
    tm = TK
    tb = tm * INPROJ_SUBTILES
    nj = seq // tb
    row = lambda b, j: (b * nj + j, 0)
    const = lambda b, j: (0, 0)
    out_sds = jax.ShapeDtypeStruct((t, dc), BF16)
    return pl.pallas_call(
        functools.partial(_inproj_kernel, tm=tm, dc=dc),
        grid=(batch, nj),
        in_specs=[
            pl.BlockSpec((tb, d), row),
            pl.BlockSpec((1, d), const),
            pl.BlockSpec(w_in.shape, const),
            pl.BlockSpec(conv_w.shape, const),
            pl.BlockSpec((1, dc), const),
            pl.BlockSpec((1, dc), const),
            pl.BlockSpec((1, dc), const),
            pl.BlockSpec(gmat.shape, const),
        ],
        out_specs=[
            pl.BlockSpec((tb, dc), row),
            pl.BlockSpec((dc, tb), lambda b, j: (0, b * nj + j)),
            pl.BlockSpec((tb, dc), row),
            pl.BlockSpec((INPROJ_SUBTILES, dv, tm), lambda b, j: (b * nj + j, 0, 0)),
        ],
        out_shape=[out_sds, jax.ShapeDtypeStruct((dc, t), BF16), out_sds,
                   jax.ShapeDtypeStruct((t // tm, dv, tm), BF16)],
        scratch_shapes=[pltpu.VMEM((8, dc), F32)],
        compiler_params=pltpu.CompilerParams(
            dimension_semantics=("arbitrary", "arbitrary"), vmem_limit_bytes=VMEM_LIMIT),
        name="inproj_conv_qknorm",
    )(x2, g, w_in, conv_w, conv_g, qg, kg, gmat)


def _attn_kernel(lp_ref, sg_ref, qT_ref, k_ref, vT_ref, o_ref, m_ref, acc_ref,
                 sa_ref, pb_ref, ab_ref, *, tq, tk, nq, lam_init):
    map_a = slice(0, tq)
    map_b = slice(tq, 2 * tq)

    lp = lp_ref[...]
    lam = (jnp.exp(jnp.sum(lp[0:1, :] * lp[1:2, :], axis=-1, keepdims=True))
           - jnp.exp(jnp.sum(lp[2:3, :] * lp[3:4, :], axis=-1, keepdims=True)) + lam_init)

    def stacked_queries(qi):
        qT = qT_ref[:, qi * tq:(qi + 1) * tq]
        row = lax.broadcasted_iota(jnp.int32, qT.shape, 0)
        zero = jnp.zeros_like(qT)
        return jnp.concatenate([jnp.where(row < HEAD_DIM, qT, zero),
                                jnp.where(row >= HEAD_DIM, qT, zero)], axis=1)

    def pieces(diag):
        return ((slice(0, tq // 2), tk // 2), (slice(tq // 2, tq), tk)) if diag else ((slice(0, tq), tk),)

    def shifted(cols, off):
        return slice(cols.start + off, cols.stop + off)

    def scores(j, qq, cols, n_keys):
        return _dot(k_ref[j * tk:j * tk + n_keys, :], qq[:, cols])

    def softmax(m, cols, s, q0):
        if q0 is not None:
            qrel = q0 + lax.broadcasted_iota(jnp.int32, s.shape, 1)
            krel = lax.broadcasted_iota(jnp.int32, s.shape, 0)
            s = jnp.where(krel <= qrel, s, -jnp.inf)
        sb = s.astype(BF16)
        m_old = m[:, cols]
        m_new = jnp.maximum(m_old, jnp.max(sb, axis=0, keepdims=True).astype(F32))
        alpha = jnp.exp2(m_old - m_new)
        p = jnp.exp2(sb - m_new.astype(BF16))
        m[:, cols] = m_new
        return alpha, p

    def accumulate(acc, cols, alpha, vb, p):
        acc[:, cols] = alpha * acc[:, cols] + _dot(vb, p)

    def prefetch_map_a(sa, j, qq, diag):
        for cols, n_keys in pieces(diag):
            sa[0:n_keys, cols] = scores(j, qq, cols, n_keys)

    stacked = {0: stacked_queries(0)}
    m_ref[0] = jnp.full(m_ref.shape[1:], -jnp.inf, F32)
    acc_ref[0] = jnp.zeros(acc_ref.shape[1:], F32)
    prefetch_map_a(sa_ref.at[0], 0, stacked[0], True)

    for qi in range(nq):
        par = qi % 2
        m, acc, sa, qq = m_ref.at[par], acc_ref.at[par], sa_ref.at[par], stacked[qi]
        for j in range(qi + 1):
            last = j == qi
            if j > 0:
                accumulate(acc, map_b, ab_ref[...], vT_ref[j - 1], pb_ref[...])
            s_b = [scores(j, qq, shifted(cols, tq), n_keys) for cols, n_keys in pieces(last)]
            for cols, n_keys in pieces(last):
                alpha_a, p_a = softmax(m, cols, sa[0:n_keys, cols], cols.start if last else None)
                accumulate(acc, cols, alpha_a, vT_ref[j, :, 0:n_keys], p_a)
            if not last:
                prefetch_map_a(sa, j + 1, qq, j + 1 == qi)
            elif qi + 1 < nq:
                stacked[qi + 1] = stacked_queries(qi + 1)
                m_ref[1 - par] = jnp.full(m_ref.shape[1:], -jnp.inf, F32)
                acc_ref[1 - par] = jnp.zeros(acc_ref.shape[1:], F32)
                prefetch_map_a(sa_ref.at[1 - par], 0, stacked[qi + 1], False)
            for (cols, n_keys), s in zip(pieces(last), s_b):
                alpha_b, p_b = softmax(m, shifted(cols, tq), s, cols.start if last else None)
                if last:
                    accumulate(acc, shifted(cols, tq), alpha_b, vT_ref[j, :, 0:n_keys], p_b)
                else:
                    ab_ref[...] = alpha_b
                    pb_ref[...] = p_b

        o = acc[0:V_DIM, :] / acc[V_DIM:V_DIM + 1, :]
        d = o[:, map_a] - lam * o[:, map_b]
        ms = jnp.mean(d * d, axis=0, keepdims=True)
        o_ref[qi * tq:(qi + 1) * tq, :] = (
            d * lax.rsqrt(ms + EPS) * sg_ref[...] * (1.0 - lam_init)).T.astype(BF16)
        del stacked[qi]


def _attention(lam_params, subln_g, qT, k, vT, *, batch, seq, lam_init):
    dq, t = qT.shape
    n_heads = dq // V_DIM
    tq, tk = TQ, TK
    assert tq == tk and vT.shape[2] == tk
    nq = seq // tq
    nk = seq // tk
    vrows = V_DIM + ONES_ROWS
    const = lambda b, h: (0, 0)
    return pl.pallas_call(
        functools.partial(_attn_kernel, tq=tq, tk=tk, nq=nq, lam_init=lam_init),
        grid=(batch, n_heads),
        in_specs=[
            pl.BlockSpec(lam_params.shape, const),
            pl.BlockSpec((V_DIM, 1), const),
            pl.BlockSpec((V_DIM, seq), lambda b, h: (h, b)),
            pl.BlockSpec((seq, V_DIM), lambda b, h: (b, h)),
            pl.BlockSpec((nk, vrows, tk), lambda b, h: (b, h, 0)),
        ],
        out_specs=pl.BlockSpec((seq, V_DIM), lambda b, h: (b, h)),
        out_shape=jax.ShapeDtypeStruct((t, dq), BF16),
        scratch_shapes=[pltpu.VMEM((2, 1, 2 * tq), F32),
                        pltpu.VMEM((2, vrows, 2 * tq), F32),
                        pltpu.VMEM((2, tk, tq), F32), pltpu.VMEM((tk, tq), BF16), pltpu.VMEM((1, tq), F32)],
        compiler_params=pltpu.CompilerParams(
            dimension_semantics=("arbitrary", "arbitrary"), vmem_limit_bytes=VMEM_LIMIT),
        name="diff_attention",
    )(lam_params, subln_g, qT, k, vT)


def _outproj_router_kernel(x_ref, yc_ref, at_ref, wo_ref, g_ref, wr_ref,
                           h_ref, hn_ref, route_ref, routeT_ref, cnt_ref, *, tm, dc):
    i = pl.program_id(0)

    @pl.when(i == 0)
    def _():
        cnt_ref[...] = jnp.zeros_like(cnt_ref)

    for r0 in range(0, x_ref.shape[0], tm):
        _outproj_router_subtile(slice(r0, r0 + tm), x_ref, yc_ref, at_ref, wo_ref, g_ref, wr_ref,
                                h_ref, hn_ref, route_ref, routeT_ref, cnt_ref, tm=tm, dc=dc)


def _outproj_router_subtile(rows, x_ref, yc_ref, at_ref, wo_ref, g_ref, wr_ref,
                            h_ref, hn_ref, route_ref, routeT_ref, cnt_ref, *, tm, dc):
    h = x_ref[rows, :] + _dot(yc_ref[rows, :], wo_ref[0:dc, :]) + _dot(at_ref[rows, :], wo_ref[dc:2 * dc, :])
    h_ref[rows, :] = h.astype(BF16)
    ms = jnp.mean(h * h, axis=-1, keepdims=True)
    hn = h * lax.rsqrt(ms + EPS) * g_ref[...]
    hi = hn.astype(BF16)
    hn_ref[rows, :] = _pack_rows(hn)
    lo = (hn - hi.astype(F32)).astype(BF16)
    prod = _dot(hi, wr_ref[...])
    logits = prod[:, 0:LANES] + prod[:, LANES:2 * LANES] + _dot(lo, wr_ref[:, 0:LANES])

    lt = logits.T
    neg = -jnp.inf
    grow = lax.broadcasted_iota(jnp.int32, (8, tm), 0).astype(F32)
    gl = jnp.where(grow < N_GROUPS, lt[N_EXPERTS:N_EXPERTS + 8, :], neg)
    gmax = jnp.max(gl, axis=0, keepdims=True)
    g_gate = 1.0 / jnp.sum(jnp.exp(gl - gmax), axis=0, keepdims=True)
    g_idx = jnp.min(jnp.where(gl == gmax, grow, 1e9), axis=0, keepdims=True)
    erow = lax.broadcasted_iota(jnp.int32, (N_EXPERTS, tm), 0).astype(F32)
    e_lo = g_idx * EXPERTS_PER_GROUP
    el = jnp.where((erow >= e_lo) & (erow < e_lo + EXPERTS_PER_GROUP), lt[0:N_EXPERTS, :], neg)
    v1 = jnp.max(el, axis=0, keepdims=True)
    i1 = jnp.min(jnp.where(el == v1, erow, 1e9), axis=0, keepdims=True)
    el2 = jnp.where(erow == i1, neg, el)
    v2 = jnp.max(el2, axis=0, keepdims=True)
    i2 = jnp.min(jnp.where(el2 == v2, erow, 1e9), axis=0, keepdims=True)
    tt = jnp.exp(v2 - v1)
    w1 = g_gate / (1.0 + tt)
    w2 = g_gate * tt / (1.0 + tt)

    sel1 = erow == i1
    sel2 = erow == i2
    oh = jnp.where(sel1 | sel2, 1.0, 0.0)
    ss = lax.broadcasted_iota(jnp.int32, (tm, tm), 0)
    tt_i = lax.broadcasted_iota(jnp.int32, (tm, tm), 1)
    earlier = jnp.where(ss < tt_i, 1.0, 0.0).astype(BF16)
    ranks = _dot(oh.astype(BF16), earlier) + cnt_ref[...]
    r1 = jnp.sum(jnp.where(sel1, ranks, 0.0), axis=0, keepdims=True)
    r2 = jnp.sum(jnp.where(sel2, ranks, 0.0), axis=0, keepdims=True)
    cnt_ref[...] = cnt_ref[...] + jnp.sum(oh, axis=1, keepdims=True)

    routeT = jnp.concatenate([i1, i2, r1, r2, w1, w2, jnp.zeros((2, tm), F32)], axis=0)
    routeT_ref[:, rows] = routeT
    route_ref[rows, :] = jnp.concatenate([routeT, jnp.zeros((LANES - 8, tm), F32)], axis=0).T


def _outproj_router(x2, yc, at, w_out, g, wr_cat):
    t, d = x2.shape
    dc = yc.shape[1]
    tm = TM_PROJ
    tb = tm * OUTPROJ_SUBTILES
    row = lambda i: (i, 0)
    const = lambda i: (0, 0)
    return pl.pallas_call(
        functools.partial(_outproj_router_kernel, tm=tm, dc=dc),
        grid=(t // tb,),
        in_specs=[
            pl.BlockSpec((tb, d), row),
            pl.BlockSpec((tb, dc), row),
            pl.BlockSpec((tb, dc), row),
            pl.BlockSpec(w_out.shape, const),
            pl.BlockSpec((1, d), const),
            pl.BlockSpec(wr_cat.shape, const),
        ],
        out_specs=[
            pl.BlockSpec((tb, d), row),
            pl.BlockSpec((tb, d // 2), row),
            pl.BlockSpec((tb, LANES), row),
            pl.BlockSpec((8, tb), lambda i: (0, i)),
            pl.BlockSpec((N_EXPERTS, 1), const),
        ],
        out_shape=[
            jax.ShapeDtypeStruct((t, d), BF16),
            jax.ShapeDtypeStruct((t, d // 2), jnp.int32),
            jax.ShapeDtypeStruct((t, LANES), F32),
            jax.ShapeDtypeStruct((8, t), F32),
            jax.ShapeDtypeStruct((N_EXPERTS, 1), F32),
        ],
        compiler_params=pltpu.CompilerParams(
            dimension_semantics=("arbitrary",), vmem_limit_bytes=VMEM_LIMIT),
        name="outproj_router",
    )(x2, yc, at, w_out, g, wr_cat)


def _plan_kernel(cnt_ref, rt_ref, pos_ref, te_ref, nx_ref, sl_ref, nt_ref, *, tm):
    shift = tm.bit_length() - 1
    tiles, starts = [], []
    total = jnp.int32(0)
    for e in range(N_EXPERTS):
        n = lax.shift_right_logical(cnt_ref[e] + (tm - 1), shift)
        tiles.append(n)
        starts.append(total)
        total = total + n
    nt_ref[0] = total

    rt = rt_ref[...]
    ea, eb = rt[0:1, :], rt[1:2, :]
    sa = jnp.zeros_like(ea)
    sb = jnp.zeros_like(eb)
    for e in range(N_EXPERTS):
        start = (starts[e] * tm).astype(F32)
        sa = jnp.where(ea == e, start, sa)
        sb = jnp.where(eb == e, start, sb)
    pos_ref[0:1, :] = (sa + rt[2:3, :]).astype(jnp.int32)
    pos_ref[1:2, :] = (sb + rt[3:4, :]).astype(jnp.int32)

    def clear(i, c):
        te_ref[i] = 0
        nx_ref[i] = -1
        sl_ref[i] = 0
        return c

    lax.fori_loop(0, te_ref.shape[0], clear, 0)

    nxt = jnp.int32(-1)
    next_of = [None] * N_EXPERTS
    for e in reversed(range(N_EXPERTS)):
        next_of[e] = nxt
        nxt = jnp.where(tiles[e] > 0, e, nxt)
    ordinal = jnp.int32(0)
    for e in range(N_EXPERTS):
        slot = ordinal & 1

        def fill(j, c, e=e, slot=slot):
            te_ref[starts[e] + j] = e
            nx_ref[starts[e] + j] = next_of[e]
            sl_ref[starts[e] + j] = slot
            return c

        lax.fori_loop(0, tiles[e], fill, 0)
        ordinal = ordinal + (tiles[e] > 0).astype(jnp.int32)


def _plan(counts, routeT, n_tiles_max):
    t = routeT.shape[1]
    smem = pl.BlockSpec(memory_space=pltpu.SMEM)
    table = jax.ShapeDtypeStruct((n_tiles_max,), jnp.int32)
    return pl.pallas_call(
        functools.partial(_plan_kernel, tm=TM_MOE),
        grid_spec=pltpu.PrefetchScalarGridSpec(
            num_scalar_prefetch=1, grid=(1,),
            in_specs=[pl.BlockSpec(routeT.shape, lambda i, cnt: (0, 0))],
            out_specs=[pl.BlockSpec((2, t), lambda i, cnt: (0, 0)), smem, smem, smem, smem],
        ),
        out_shape=[jax.ShapeDtypeStruct((2, t), jnp.int32), table, table, table,
                   jax.ShapeDtypeStruct((1,), jnp.int32)],
        name="routing_plan",
    )(counts, routeT)


def _moe_kernel(te_ref, nt_ref, nx_ref, sl_ref, x_ref, wg_hbm, wu_hbm, wd_hbm, y_ref,
                wg_st, wu_st, wd_st, wgb_ref, wub_ref, wdb_ref, sem, *, tm, tiles_per_step):
    last = nt_ref[0] - 1

    def weight_copies(e, s):
        return (pltpu.make_async_copy(wg_hbm.at[e], wg_st.at[s], sem.at[s, 0]),
                pltpu.make_async_copy(wu_hbm.at[e], wu_st.at[s], sem.at[s, 1]),
                pltpu.make_async_copy(wd_hbm.at[e], wd_st.at[s], sem.at[s, 2]))

    for u in range(tiles_per_step):
        i = pl.program_id(0) * tiles_per_step + u
        rows = slice(u * tm, (u + 1) * tm)
        ic = jnp.minimum(i, last)
        expert = te_ref[ic]
        slot = sl_ref[ic]
        first_of_expert = (i == 0) | ((i <= last) & (expert != te_ref[jnp.maximum(ic - 1, 0)]))

        if u == 0:
            @pl.when(i == 0)
            def _():
                for c in weight_copies(expert, slot):
                    c.start()

        @pl.when(first_of_expert)
        def _():
            for c in weight_copies(expert, slot):
                c.wait()
            nxt = nx_ref[ic]

            @pl.when(nxt >= 0)
            def _():
                for c in weight_copies(nxt, 1 - slot):
                    c.start()

            wgb_ref[...] = wg_st[slot].astype(BF16)
            wub_ref[...] = wu_st[slot].astype(BF16)
            wdb_ref[...] = wd_st[slot].astype(BF16)

        @pl.when(i <= last)
        def _():
            x_l, x_r = _unpack_rows(x_ref[rows, :])
            x_l = x_l.astype(BF16)
            x_r = x_r.astype(BF16)
            half = x_l.shape[1]
            hg = _dot(x_l, wgb_ref[0:half, :]) + _dot(x_r, wgb_ref[half:2 * half, :])
            hu = _dot(x_l, wub_ref[0:half, :]) + _dot(x_r, wub_ref[half:2 * half, :])
            act = hg * (1.0 / (1.0 + jnp.exp(-hg))) * hu
            y_ref[rows, :] = _pack_rows(_dot(act.astype(BF16), wdb_ref[...]))


def _moe(tile_expert, n_tiles, next_expert, stage_slot, xs, wg, wu, wd):
    p, dp = xs.shape
    d, f = wg.shape[1], wg.shape[2]
    assert dp * 2 == d
    tm, tps = TM_MOE, MOE_TILES_PER_STEP
    tb = tm * tps
    row = lambda s, te, nt, nx, sl: (jnp.minimum(s, (nt[0] - 1) // tps), 0)
    hbm = pl.BlockSpec(memory_space=pl.ANY)
    grid_spec = pltpu.PrefetchScalarGridSpec(
        num_scalar_prefetch=4,
        grid=(p // tb,),
        in_specs=[pl.BlockSpec((tb, dp), row), hbm, hbm, hbm],
        out_specs=pl.BlockSpec((tb, dp), row),
        scratch_shapes=[pltpu.VMEM((2, d, f), F32), pltpu.VMEM((2, d, f), F32), pltpu.VMEM((2, f, d), F32),
                        pltpu.VMEM((d, f), BF16), pltpu.VMEM((d, f), BF16), pltpu.VMEM((f, d), BF16),
                        pltpu.SemaphoreType.DMA((2, 3))],
    )
    return pl.pallas_call(
        functools.partial(_moe_kernel, tm=tm, tiles_per_step=tps),
        grid_spec=grid_spec,
        out_shape=jax.ShapeDtypeStruct((p, dp), jnp.int32),
        compiler_params=pltpu.CompilerParams(
            dimension_semantics=("arbitrary",), vmem_limit_bytes=VMEM_LIMIT),
        name="moe_experts",
    )(tile_expert, n_tiles, next_expert, stage_slot, xs, wg, wu, wd)


def _sc_mesh():
    return plsc.VectorSubcoreMesh(core_axis_name="c", subcore_axis_name="s",
                                  num_cores=SC_CORES, num_subcores=SC_SUBCORES)


def _sc_dispatch(rows, pos_a, pos_b, n_out):
    t, d = rows.shape
    win = pos_a.shape[1]

    @functools.partial(pl.kernel, out_type=jax.ShapeDtypeStruct((n_out, d), rows.dtype),
                       mesh=_sc_mesh(), scratch_types=[], name="sc_dispatch")
    def run(rows_hbm, pa_hbm, pb_hbm, out_hbm):
        def body(rows_vmem, pa_vmem, pb_vmem):
            pltpu.sync_copy(rows_vmem, out_hbm.at[pa_vmem.at[0]])
            pltpu.sync_copy(rows_vmem, out_hbm.at[pb_vmem.at[0]])

        pltpu.emit_pipeline(
            body, grid=(t // win,),
            in_specs=[pl.BlockSpec((win, d), lambda i: (i, 0)),
                      pl.BlockSpec((1, win), lambda i: (i, 0)),
                      pl.BlockSpec((1, win), lambda i: (i, 0))],
            out_specs=[],
            core_axis_name=("c", "s"),
            dimension_semantics=(pltpu.PARALLEL,),
        )(rows_hbm, pa_hbm, pb_hbm)

    return run(rows, pos_a, pos_b)


def _sc_gather(table, idx):
    d = table.shape[1]
    n_win, win = idx.shape

    @functools.partial(pl.kernel, out_type=jax.ShapeDtypeStruct((n_win * win, d), table.dtype),
                       mesh=_sc_mesh(), scratch_types=[], name="sc_gather")
    def run(table_hbm, idx_hbm, out_hbm):
        def body(idx_vmem, out_vmem):
            pltpu.sync_copy(table_hbm.at[idx_vmem.at[0]], out_vmem)

        pltpu.emit_pipeline(
            body, grid=(n_win,),
            in_specs=[pl.BlockSpec((1, win), lambda i: (i, 0))],
            out_specs=[pl.BlockSpec((win, d), lambda i: (i, 0))],
            core_axis_name=("c", "s"),
            dimension_semantics=(pltpu.PARALLEL,),
        )(idx_hbm, out_hbm)

    return run(table, idx)


def _combine_kernel(h_ref, ya_ref, yb_ref, r_ref, o_ref):
    r = r_ref[...]
    wa, wb = r[:, 4:5], r[:, 5:6]
    a_l, a_r = _unpack_rows(ya_ref[...])
    b_l, b_r = _unpack_rows(yb_ref[...])
    half = a_l.shape[1]
    o_ref[:, 0:half] = h_ref[:, 0:half].astype(F32) + wa * a_l + wb * b_l
    o_ref[:, half:2 * half] = h_ref[:, half:2 * half].astype(F32) + wa * a_r + wb * b_r


def _combine(hres, yg, route):
    t, d = hres.shape
    tm = TM_COMBINE
    nb = t // tm
    return pl.pallas_call(
        _combine_kernel,
        grid=(nb,),
        in_specs=[pl.BlockSpec((tm, d), lambda i: (i, 0)),
                  pl.BlockSpec((tm, d // 2), lambda i: (i, 0)),
                  pl.BlockSpec((tm, d // 2), lambda i: (i + nb, 0)),
                  pl.BlockSpec((tm, LANES), lambda i: (i, 0))],
        out_specs=pl.BlockSpec((tm, d), lambda i: (i, 0)),
        out_shape=jax.ShapeDtypeStruct((t, d), F32),
        compiler_params=pltpu.CompilerParams(
            dimension_semantics=("arbitrary",), vmem_limit_bytes=VMEM_LIMIT),
        name="combine",
    )(hres, yg, yg, route)


def _lambda_init(layer_idx):
    return 0.8 - 0.6 * math.exp(-0.3 * layer_idx)


def _layer(h, l, attn_norm_g, w_in, conv_w, conv_out_g, q_norm_g, k_norm_g,
           lambda_q1, lambda_k1, lambda_q2, lambda_k2, attn_subln_g, w_out,
           ffn_norm_g, w_router_group, w_router_expert, w_exp_gate, w_exp_up, w_exp_down):
    batch, seq, d = h.shape
    t = batch * seq
    dc = conv_w.shape[-1]
    lam_init = _lambda_init(l)
    x2 = h.reshape(t, d)

    reps = dc // HEAD_DIM
    assert dc // CONV_GROUPS == HEAD_DIM
    qg = (jnp.tile(q_norm_g[l], reps) * (HEAD_DIM ** -0.5 * math.log2(math.e))).reshape(1, dc)
    kg = jnp.tile(k_norm_g[l], reps).reshape(1, dc)
    grp = jnp.arange(MXU_TILE) // HEAD_DIM
    gmat = jnp.where(grp[:, None] == grp[None, :], 1.0 / HEAD_DIM, 0.0).astype(BF16)
    yc, qT, k, vT = _inproj(x2, attn_norm_g[l].reshape(1, d), w_in[l].astype(BF16), conv_w[l],
                          conv_out_g[l].reshape(1, dc), qg, kg, gmat, batch=batch, seq=seq)

    lam_params = jnp.stack([lambda_q1[l], lambda_k1[l], lambda_q2[l], lambda_k2[l]])
    at = _attention(lam_params, attn_subln_g[l].reshape(V_DIM, 1), qT, k, vT,
                    batch=batch, seq=seq, lam_init=lam_init)

    wr = jnp.concatenate([w_router_expert[l], w_router_group[l],
                          jnp.zeros((d, LANES - N_EXPERTS - N_GROUPS), F32)], axis=1)
    wr_hi = wr.astype(BF16)
    wr_lo = (wr - wr_hi.astype(F32)).astype(BF16)
    hres, hn2, route, routeT, cnt = _outproj_router(x2, yc, at, w_out[l].astype(BF16),
                                                    ffn_norm_g[l].reshape(1, d),
                                                    jnp.concatenate([wr_hi, wr_lo], axis=1))

    tmm = TM_MOE
    n_tiles_max = (2 * t) // tmm + N_EXPERTS
    p_rows = n_tiles_max * tmm
    pos, tile_expert, next_expert, stage_slot, n_tiles = _plan(
        cnt[:, 0].astype(jnp.int32), routeT, n_tiles_max)

    posw = pos.reshape(2 * t // SC_WIN, SC_WIN)
    pos1w = posw[:t // SC_WIN]
    pos2w = posw[t // SC_WIN:]
    xs = _sc_dispatch(hn2, pos1w, pos2w, p_rows)

    f = w_exp_gate.shape[-1]
    ys = _moe(tile_expert, n_tiles, next_expert, stage_slot, xs,
              w_exp_gate[l].reshape(N_EXPERTS, d, f),
              w_exp_up[l].reshape(N_EXPERTS, d, f),
              w_exp_down[l].reshape(N_EXPERTS, f, d))
    yg = _sc_gather(ys, posw)
    out = _combine(hres, yg, route)
    return out.reshape(batch, seq, d)


def kernel(x, attn_norm_g, w_in, conv_w, conv_out_g, q_norm_g, k_norm_g, lambda_q1, lambda_k1,
           lambda_q2, lambda_k2, attn_subln_g, w_out, ffn_norm_g, w_router_group, w_router_expert,
           w_exp_gate, w_exp_up, w_exp_down):
    h = x
    for l in range(attn_norm_g.shape[0]):
        h = _layer(h, l, attn_norm_g, w_in, conv_w, conv_out_g, q_norm_g, k_norm_g,
                   lambda_q1, lambda_k1, lambda_q2, lambda_k2, attn_subln_g, w_out,
                   ffn_norm_g, w_router_group, w_router_expert, w_exp_gate, w_exp_up, w_exp_down)
    return h
```

```python
import functools
import math

import jax
import jax.numpy as jnp
from jax import lax
from jax.experimental import pallas as pl
from jax.experimental.pallas import tpu as pltpu
from jax.experimental.pallas import tpu_sc as plsc

F32 = jnp.float32
BF16 = jnp.bfloat16

HEAD_DIM = 64
V_DIM = 2 * HEAD_DIM
CONV_GROUPS = 8
N_GROUPS = 4
EXPERTS_PER_GROUP = 8
N_EXPERTS = N_GROUPS * EXPERTS_PER_GROUP
EPS = 1e-6
LANES = 128
MXU_TILE = 256
ONES_ROWS = 16
VMEM_LIMIT = 48 * 1024 * 1024

TM_PROJ = 256
TM_COMBINE = 1024
OUTPROJ_SUBTILES = 4
INPROJ_SUBTILES = 2
TQ = 512
TK = 512
TM_MOE = 512
MOE_TILES_PER_STEP = 2
SC_CORES = 2
SC_SUBCORES = 16
SC_WIN = 64


def _dot(a, b):
    return jnp.dot(a, b, preferred_element_type=F32)


def _pack_rows(x):
    w = x.shape[1] // 2
    bits = lax.bitcast_convert_type(x.astype(BF16).astype(F32), jnp.uint32)
    return lax.bitcast_convert_type((bits[:, :w] >> 16) | bits[:, w:], jnp.int32)


def _unpack_rows(packed):
    bits = lax.bitcast_convert_type(packed, jnp.uint32)
    left = lax.bitcast_convert_type(bits << 16, F32)
    right = lax.bitcast_convert_type(bits & jnp.uint32(0xFFFF0000), F32)
    return left, right


def _group_mean(sq, gm):
    w = gm.shape[0]
    sq = sq.astype(BF16)
    return jnp.concatenate([_dot(sq[:, c:c + w], gm) for c in range(0, sq.shape[1], w)], axis=1)


def _inproj_kernel(x_ref, g_ref, w_ref, cw_ref, cg_ref, qg_ref, kg_ref, gm_ref,
                   yc_ref, qT_ref, k_ref, vT_ref, carry_ref, *, tm, dc):
    j = pl.program_id(1)

    @pl.when(j == 0)
    def _():
        carry_ref[...] = jnp.zeros_like(carry_ref)

    for u in range(x_ref.shape[0] // tm):
        _inproj_subtile(u, x_ref, g_ref, w_ref, cw_ref, cg_ref, qg_ref, kg_ref, gm_ref,
                        yc_ref, qT_ref, k_ref, vT_ref, carry_ref, tm=tm, dc=dc)


def _inproj_subtile(u, x_ref, g_ref, w_ref, cw_ref, cg_ref, qg_ref, kg_ref, gm_ref,
                    yc_ref, qT_ref, k_ref, vT_ref, carry_ref, *, tm, dc):
    rows = slice(u * tm, (u + 1) * tm)
    x = x_ref[rows, :]
    ms = jnp.mean(x * x, axis=-1, keepdims=True)
    hn = (x * lax.rsqrt(ms + EPS) * g_ref[...]).astype(BF16)

    def proj(s):
        return _dot(hn, w_ref[:, s * dc:(s + 1) * dc])

    gm = gm_ref[...]

    c = proj(2) * proj(0)
    prev = carry_ref[...]
    r = lax.broadcasted_iota(jnp.int32, c.shape, 0)
    c1 = jnp.where(r == 0, prev[7:8, :], pltpu.roll(c, 1, 0))
    c2 = jnp.where(r == 0, prev[6:7, :], jnp.where(r == 1, prev[7:8, :], pltpu.roll(c, 2, 0)))
    carry_ref[...] = c[tm - 8:tm, :]
    cw = cw_ref[...]
    y = proj(1) * (cw[0:1, :] * c2 + cw[1:2, :] * c1 + cw[2:3, :] * c)
    yc_ref[rows, :] = (y * lax.rsqrt(_group_mean(y * y, gm) + EPS) * cg_ref[...]).astype(BF16)

    q = proj(3)
    qT_ref[:, rows] = (q * lax.rsqrt(_group_mean(q * q, gm) + EPS) * qg_ref[...]).T.astype(BF16)
    k = proj(4)
    k_ref[rows, :] = (k * lax.rsqrt(_group_mean(k * k, gm) + EPS) * kg_ref[...]).astype(BF16)
    vt = proj(5).T.astype(BF16)
    vrows = V_DIM + ONES_ROWS
    for h in range(dc // V_DIM):
        vT_ref[u, h * vrows:h * vrows + V_DIM, :] = vt[h * V_DIM:(h + 1) * V_DIM, :]
        vT_ref[u, h * vrows + V_DIM:(h + 1) * vrows, :] = jnp.ones((ONES_ROWS, tm), BF16)


def _inproj(x2, g, w_in, conv_w, conv_g, qg, kg, gmat, *, batch, seq):
    t, d = x2.shape
    dc = conv_g.shape[1]
    dv = dc // V_DIM * (V_DIM + ONES_ROWS)
    tm = TK
    tb = tm * INPROJ_SUBTILES
    nj = seq // tb
    row = lambda b, j: (b * nj + j, 0)
    const = lambda b, j: (0, 0)
    out_sds = jax.ShapeDtypeStruct((t, dc), BF16)
    return pl.pallas_call(
        functools.partial(_inproj_kernel, tm=tm, dc=dc),
        grid=(batch, nj),
        in_specs=[
            pl.BlockSpec((tb, d), row),
            pl.BlockSpec((1, d), const),
            pl.BlockSpec(w_in.shape, const),
            pl.BlockSpec(conv_w.shape, const),
            pl.BlockSpec((1, dc), const),
            pl.BlockSpec((1, dc), const),
            pl.BlockSpec((1, dc), const),
            pl.BlockSpec(gmat.shape, const),
        ],
        out_specs=[
            pl.BlockSpec((tb, dc), row),
            pl.BlockSpec((dc, tb), lambda b, j: (0, b * nj + j)),
            pl.BlockSpec((tb, dc), row),
            pl.BlockSpec((INPROJ_SUBTILES, dv, tm), lambda b, j: (b * nj + j, 0, 0)),
        ],
        out_shape=[out_sds, jax.ShapeDtypeStruct((dc, t), BF16), out_sds,
                   jax.ShapeDtypeStruct((t // tm, dv, tm), BF16)],
        scratch_shapes=[pltpu.VMEM((8, dc), F32)],
        compiler_params=pltpu.CompilerParams(
            dimension_semantics=("arbitrary", "arbitrary"), vmem_limit_bytes=VMEM_LIMIT),
        name="inproj_conv_qknorm",
    )(x2, g, w_in, conv_w, conv_g, qg, kg, gmat)


def _attn_kernel(lp_ref, sg_ref, qT_ref, k_ref, vT_ref, o_ref, m_ref, acc_ref,
                 sa_ref, pb_ref, ab_ref, *, tq, tk, nq, lam_init):
    map_a = slice(0, tq)
    map_b = slice(tq, 2 * tq)

    lp = lp_ref[...]
    lam = (jnp.exp(jnp.sum(lp[0:1, :] * lp[1:2, :], axis=-1, keepdims=True))
           - jnp.exp(jnp.sum(lp[2:3, :] * lp[3:4, :], axis=-1, keepdims=True)) + lam_init)

    def stacked_queries(qi):
        qT = qT_ref[:, qi * tq:(qi + 1) * tq]
        row = lax.broadcasted_iota(jnp.int32, qT.shape, 0)
        zero = jnp.zeros_like(qT)
        return jnp.concatenate([jnp.where(row < HEAD_DIM, qT, zero),
                                jnp.where(row >= HEAD_DIM, qT, zero)], axis=1)

    def pieces(diag):
        return ((slice(0, tq // 2), tk // 2), (slice(tq // 2, tq), tk)) if diag else ((slice(0, tq), tk),)

    def shifted(cols, off):
        return slice(cols.start + off, cols.stop + off)

    def scores(j, qq, cols, n_keys):
        return _dot(k_ref[j * tk:j * tk + n_keys, :], qq[:, cols])

    def softmax(m, cols, s, q0):
        if q0 is not None:
            qrel = q0 + lax.broadcasted_iota(jnp.int32, s.shape, 1)
            krel = lax.broadcasted_iota(jnp.int32, s.shape, 0)
            s = jnp.where(krel <= qrel, s, -jnp.inf)
        sb = s.astype(BF16)
        m_old = m[:, cols]
        m_new = jnp.maximum(m_old, jnp.max(sb, axis=0, keepdims=True).astype(F32))
        alpha = jnp.exp2(m_old - m_new)
        p = jnp.exp2(sb - m_new.astype(BF16))
        m[:, cols] = m_new
        return alpha, p

    def accumulate(acc, cols, alpha, vb, p):
        acc[:, cols] = alpha * acc[:, cols] + _dot(vb, p)

    def prefetch_map_a(sa, j, qq, diag):
        for cols, n_keys in pieces(diag):
            sa[0:n_keys, cols] = scores(j, qq, cols, n_keys)

    stacked = {0: stacked_queries(0)}
    m_ref[0] = jnp.full(m_ref.shape[1:], -jnp.inf, F32)
    acc_ref[0] = jnp.zeros(acc_ref.shape[1:], F32)
    prefetch_map_a(sa_ref.at[0], 0, stacked[0], True)

    for qi in range(nq):
        par = qi % 2
        m, acc, sa, qq = m_ref.at[par], acc_ref.at[par], sa_ref.at[par], stacked[qi]
        for j in range(qi + 1):
            last = j == qi
            if j > 0:
                accumulate(acc, map_b, ab_ref[...], vT_ref[j - 1], pb_ref[...])
            s_b = [scores(j, qq, shifted(cols, tq), n_keys) for cols, n_keys in pieces(last)]
            for cols, n_keys in pieces(last):
                alpha_a, p_a = softmax(m, cols, sa[0:n_keys, cols], cols.start if last else None)
                accumulate(acc, cols, alpha_a, vT_ref[j, :, 0:n_keys], p_a)
            if not last:
                prefetch_map_a(sa, j + 1, qq, j + 1 == qi)
            elif qi + 1 < nq:
                stacked[qi + 1] = stacked_queries(qi + 1)
                m_ref[1 - par] = jnp.full(m_ref.shape[1:], -jnp.inf, F32)
                acc_ref[1 - par] = jnp.zeros(acc_ref.shape[1:], F32)
                prefetch_map_a(sa_ref.at[1 - par], 0, stacked[qi + 1], False)
            for (cols, n_keys), s in zip(pieces(last), s_b):
                alpha_b, p_b = softmax(m, shifted(cols, tq), s, cols.start if last else None)
                if last:
                    accumulate(acc, shifted(cols, tq), alpha_b, vT_ref[j, :, 0:n_keys], p_b)
                else:
                    ab_ref[...] = alpha_b
                    pb_ref[...] = p_b

        o = acc[0:V_DIM, :] / acc[V_DIM:V_DIM + 1, :]
        d = o[:, map_a] - lam * o[:, map_b]
        ms = jnp.mean(d * d, axis=0, keepdims=True)
        o_ref[qi * tq:(qi + 1) * tq, :] = (
            d * lax.rsqrt(ms + EPS) * sg_ref[...] * (1.0 - lam_init)).T.astype(BF16)
        del stacked[qi]


def _attention(lam_params, subln_g, qT, k, vT, *, batch, seq, lam_init):
    dq, t = qT.shape
    n_heads = dq // V_DIM
    tq, tk = TQ, TK
    assert tq == tk and vT.shape[2] == tk
    nq = seq // tq
    nk = seq // tk
    vrows = V_DIM + ONES_ROWS
    const = lambda b, h: (0, 0)
    return pl.pallas_call(
        functools.partial(_attn_kernel, tq=tq, tk=tk, nq=nq, lam_init=lam_init),
        grid=(batch, n_heads),
        in_specs=[
            pl.BlockSpec(lam_params.shape, const),
            pl.BlockSpec((V_DIM, 1), const),
            pl.BlockSpec((V_DIM, seq), lambda b, h: (h, b)),
            pl.BlockSpec((seq, V_DIM), lambda b, h: (b, h)),
            pl.BlockSpec((nk, vrows, tk), lambda b, h: (b, h, 0)),
        ],
        out_specs=pl.BlockSpec((seq, V_DIM), lambda b, h: (b, h)),
        out_shape=jax.ShapeDtypeStruct((t, dq), BF16),
        scratch_shapes=[pltpu.VMEM((2, 1, 2 * tq), F32),
                        pltpu.VMEM((2, vrows, 2 * tq), F32),
                        pltpu.VMEM((2, tk, tq), F32), pltpu.VMEM((tk, tq), BF16), pltpu.VMEM((1, tq), F32)],
        compiler_params=pltpu.CompilerParams(
            dimension_semantics=("arbitrary", "arbitrary"), vmem_limit_bytes=VMEM_LIMIT),
        name="diff_attention",
    )(lam_params, subln_g, qT, k, vT)


def _outproj_router_kernel(x_ref, yc_ref, at_ref, wo_ref, g_ref, wr_ref,
                           h_ref, hn_ref, routeT_ref, cnt_ref, *, tm, dc):
    i = pl.program_id(0)

    @pl.when(i == 0)
    def _():
        cnt_ref[...] = jnp.zeros_like(cnt_ref)

    for r0 in range(0, x_ref.shape[0], tm):
        _outproj_router_subtile(slice(r0, r0 + tm), x_ref, yc_ref, at_ref, wo_ref, g_ref, wr_ref,
                                h_ref, hn_ref, routeT_ref, cnt_ref, tm=tm, dc=dc)


def _outproj_router_subtile(rows, x_ref, yc_ref, at_ref, wo_ref, g_ref, wr_ref,
                            h_ref, hn_ref, routeT_ref, cnt_ref, *, tm, dc):
    h = x_ref[rows, :] + _dot(yc_ref[rows, :], wo_ref[0:dc, :]) + _dot(at_ref[rows, :], wo_ref[dc:2 * dc, :])
    h_ref[rows, :] = h.astype(BF16)
    ms = jnp.mean(h * h, axis=-1, keepdims=True)
    hn = h * lax.rsqrt(ms + EPS) * g_ref[...]
    hi = hn.astype(BF16)
    hn_ref[rows, :] = _pack_rows(hn)
    lo = (hn - hi.astype(F32)).astype(BF16)
    prod = _dot(hi, wr_ref[...])
    logits = prod[:, 0:LANES] + prod[:, LANES:2 * LANES] + _dot(lo, wr_ref[:, 0:LANES])

    lt = logits.T
    neg = -jnp.inf
    grow = lax.broadcasted_iota(jnp.int32, (8, tm), 0).astype(F32)
    gl = jnp.where(grow < N_GROUPS, lt[N_EXPERTS:N_EXPERTS + 8, :], neg)
    gmax = jnp.max(gl, axis=0, keepdims=True)
    g_gate = 1.0 / jnp.sum(jnp.exp(gl - gmax), axis=0, keepdims=True)
    g_idx = jnp.min(jnp.where(gl == gmax, grow, 1e9), axis=0, keepdims=True)
    erow = lax.broadcasted_iota(jnp.int32, (N_EXPERTS, tm), 0).astype(F32)
    e_lo = g_idx * EXPERTS_PER_GROUP
    el = jnp.where((erow >= e_lo) & (erow < e_lo + EXPERTS_PER_GROUP), lt[0:N_EXPERTS, :], neg)
    v1 = jnp.max(el, axis=0, keepdims=True)
    i1 = jnp.min(jnp.where(el == v1, erow, 1e9), axis=0, keepdims=True)
    el2 = jnp.where(erow == i1, neg, el)
    v2 = jnp.max(el2, axis=0, keepdims=True)
    i2 = jnp.min(jnp.where(el2 == v2, erow, 1e9), axis=0, keepdims=True)
    tt = jnp.exp(v2 - v1)
    w1 = g_gate / (1.0 + tt)
    w2 = g_gate * tt / (1.0 + tt)

    sel1 = erow == i1
    sel2 = erow == i2
    oh = jnp.where(sel1 | sel2, 1.0, 0.0)
    ss = lax.broadcasted_iota(jnp.int32, (tm, tm), 0)
    tt_i = lax.broadcasted_iota(jnp.int32, (tm, tm), 1)
    earlier = jnp.where(ss < tt_i, 1.0, 0.0).astype(BF16)
    ranks = _dot(oh.astype(BF16), earlier) + cnt_ref[...]
    r1 = jnp.sum(jnp.where(sel1, ranks, 0.0), axis=0, keepdims=True)
    r2 = jnp.sum(jnp.where(sel2, ranks, 0.0), axis=0, keepdims=True)
    cnt_ref[...] = cnt_ref[...] + jnp.sum(oh, axis=1, keepdims=True)

    routeT = jnp.concatenate([i1, i2, r1, r2, w1, w2, jnp.zeros((2, tm), F32)], axis=0)
    routeT_ref[:, rows] = routeT


def _outproj_router(x2, yc, at, w_out, g, wr_cat):
    t, d = x2.shape
    dc = yc.shape[1]
    tm = TM_PROJ
    tb = tm * OUTPROJ_SUBTILES
    row = lambda i: (i, 0)
    const = lambda i: (0, 0)
    return pl.pallas_call(
        functools.partial(_outproj_router_kernel, tm=tm, dc=dc),
        grid=(t // tb,),
        in_specs=[
            pl.BlockSpec((tb, d), row),
            pl.BlockSpec((tb, dc), row),
            pl.BlockSpec((tb, dc), row),
            pl.BlockSpec(w_out.shape, const),
            pl.BlockSpec((1, d), const),
            pl.BlockSpec(wr_cat.shape, const),
        ],
        out_specs=[
            pl.BlockSpec((tb, d), row),
            pl.BlockSpec((tb, d // 2), row),
            pl.BlockSpec((8, tb), lambda i: (0, i)),
            pl.BlockSpec((N_EXPERTS, 1), const),
        ],
        out_shape=[
            jax.ShapeDtypeStruct((t, d), BF16),
            jax.ShapeDtypeStruct((t, d // 2), jnp.int32),
            jax.ShapeDtypeStruct((8, t), F32),
            jax.ShapeDtypeStruct((N_EXPERTS, 1), F32),
        ],
        compiler_params=pltpu.CompilerParams(
            dimension_semantics=("arbitrary",), vmem_limit_bytes=VMEM_LIMIT),
        name="outproj_router",
    )(x2, yc, at, w_out, g, wr_cat)


def _plan_kernel(cnt_ref, rt_ref, pos_ref, te_ref, nx_ref, sl_ref, nt_ref, *, tm):
    shift = tm.bit_length() - 1
    tiles, starts = [], []
    total = jnp.int32(0)
    for e in range(N_EXPERTS):
        n = lax.shift_right_logical(cnt_ref[e] + (tm - 1), shift)
        tiles.append(n)
        starts.append(total)
        total = total + n
    nt_ref[0] = total

    rt = rt_ref[...]
    ea, eb = rt[0:1, :], rt[1:2, :]
    sa = jnp.zeros_like(ea)
    sb = jnp.zeros_like(eb)
    for e in range(N_EXPERTS):
        start = (starts[e] * tm).astype(F32)
        sa = jnp.where(ea == e, start, sa)
        sb = jnp.where(eb == e, start, sb)
    pos_ref[0:1, :] = (sa + rt[2:3, :]).astype(jnp.int32)
    pos_ref[1:2, :] = (sb + rt[3:4, :]).astype(jnp.int32)

    def clear(i, c):
        te_ref[i] = 0
        nx_ref[i] = -1
        sl_ref[i] = 0
        return c

    lax.fori_loop(0, te_ref.shape[0], clear, 0)

    nxt = jnp.int32(-1)
    next_of = [None] * N_EXPERTS
    for e in reversed(range(N_EXPERTS)):
        next_of[e] = nxt
        nxt = jnp.where(tiles[e] > 0, e, nxt)
    ordinal = jnp.int32(0)
    for e in range(N_EXPERTS):
        slot = ordinal & 1

        def fill(j, c, e=e, slot=slot):
            te_ref[starts[e] + j] = e
            nx_ref[starts[e] + j] = next_of[e]
            sl_ref[starts[e] + j] = slot
            return c

        lax.fori_loop(0, tiles[e], fill, 0)
        ordinal = ordinal + (tiles[e] > 0).astype(jnp.int32)


def _plan(counts, routeT, n_tiles_max):
    t = routeT.shape[1]
    smem = pl.BlockSpec(memory_space=pltpu.SMEM)
    table = jax.ShapeDtypeStruct((n_tiles_max,), jnp.int32)
    return pl.pallas_call(
        functools.partial(_plan_kernel, tm=TM_MOE),
        grid_spec=pltpu.PrefetchScalarGridSpec(
            num_scalar_prefetch=1, grid=(1,),
            in_specs=[pl.BlockSpec(routeT.shape, lambda i, cnt: (0, 0))],
            out_specs=[pl.BlockSpec((2, t), lambda i, cnt: (0, 0)), smem, smem, smem, smem],
        ),
        out_shape=[jax.ShapeDtypeStruct((2, t), jnp.int32), table, table, table,
                   jax.ShapeDtypeStruct((1,), jnp.int32)],
        name="routing_plan",
    )(counts, routeT)


def _moe_kernel(te_ref, nt_ref, nx_ref, sl_ref, x_ref, wg_hbm, wu_hbm, wd_hbm, y_ref,
                wg_st, wu_st, wd_st, wgb_ref, wub_ref, wdb_ref, sem, *, tm, tiles_per_step):
    last = nt_ref[0] - 1

    def weight_copies(e, s):
        return (pltpu.make_async_copy(wg_hbm.at[e], wg_st.at[s], sem.at[s, 0]),
                pltpu.make_async_copy(wu_hbm.at[e], wu_st.at[s], sem.at[s, 1]),
                pltpu.make_async_copy(wd_hbm.at[e], wd_st.at[s], sem.at[s, 2]))

    for u in range(tiles_per_step):
        i = pl.program_id(0) * tiles_per_step + u
        rows = slice(u * tm, (u + 1) * tm)
        ic = jnp.minimum(i, last)
        expert = te_ref[ic]
        slot = sl_ref[ic]
        first_of_expert = (i == 0) | ((i <= last) & (expert != te_ref[jnp.maximum(ic - 1, 0)]))

        if u == 0:
            @pl.when(i == 0)
            def _():
                for c in weight_copies(expert, slot):
                    c.start()

        @pl.when(first_of_expert)
        def _():
            for c in weight_copies(expert, slot):
                c.wait()
            nxt = nx_ref[ic]

            @pl.when(nxt >= 0)
            def _():
                for c in weight_copies(nxt, 1 - slot):
                    c.start()

            wgb_ref[...] = wg_st[slot].astype(BF16)
            wub_ref[...] = wu_st[slot].astype(BF16)
            wdb_ref[...] = wd_st[slot].astype(BF16)

        @pl.when(i <= last)
        def _():
            x_l, x_r = _unpack_rows(x_ref[rows, :])
            x_l = x_l.astype(BF16)
            x_r = x_r.astype(BF16)
            half = x_l.shape[1]
            hg = _dot(x_l, wgb_ref[0:half, :]) + _dot(x_r, wgb_ref[half:2 * half, :])
            hu = _dot(x_l, wub_ref[0:half, :]) + _dot(x_r, wub_ref[half:2 * half, :])
            act = hg * (1.0 / (1.0 + jnp.exp(-hg))) * hu
            y_ref[rows, :] = _pack_rows(_dot(act.astype(BF16), wdb_ref[...]))


def _moe(tile_expert, n_tiles, next_expert, stage_slot, xs, wg, wu, wd):
    p, dp = xs.shape
    d, f = wg.shape[1], wg.shape[2]
    assert dp * 2 == d
    tm, tps = TM_MOE, MOE_TILES_PER_STEP
    tb = tm * tps
    row = lambda s, te, nt, nx, sl: (jnp.minimum(s, (nt[0] - 1) // tps), 0)
    hbm = pl.BlockSpec(memory_space=pl.ANY)
    grid_spec = pltpu.PrefetchScalarGridSpec(
        num_scalar_prefetch=4,
        grid=(p // tb,),
        in_specs=[pl.BlockSpec((tb, dp), row), hbm, hbm, hbm],
        out_specs=pl.BlockSpec((tb, dp), row),
        scratch_shapes=[pltpu.VMEM((2, d, f), F32), pltpu.VMEM((2, d, f), F32), pltpu.VMEM((2, f, d), F32),
                        pltpu.VMEM((d, f), BF16), pltpu.VMEM((d, f), BF16), pltpu.VMEM((f, d), BF16),
                        pltpu.SemaphoreType.DMA((2, 3))],
    )
    return pl.pallas_call(
        functools.partial(_moe_kernel, tm=tm, tiles_per_step=tps),
        grid_spec=grid_spec,
        out_shape=jax.ShapeDtypeStruct((p, dp), jnp.int32),
        compiler_params=pltpu.CompilerParams(
            dimension_semantics=("arbitrary",), vmem_limit_bytes=VMEM_LIMIT),
        name="moe_experts",
    )(tile_expert, n_tiles, next_expert, stage_slot, xs, wg, wu, wd)


def _sc_mesh():
    return plsc.VectorSubcoreMesh(core_axis_name="c", subcore_axis_name="s",
                                  num_cores=SC_CORES, num_subcores=SC_SUBCORES)


def _sc_dispatch(rows, pos_a, pos_b, n_out):
    t, d = rows.shape
    win = pos_a.shape[1]

    @functools.partial(pl.kernel, out_type=jax.ShapeDtypeStruct((n_out, d), rows.dtype),
                       mesh=_sc_mesh(), scratch_types=[], name="sc_dispatch")
    def run(rows_hbm, pa_hbm, pb_hbm, out_hbm):
        def body(rows_vmem, pa_vmem, pb_vmem):
            pltpu.sync_copy(rows_vmem, out_hbm.at[pa_vmem.at[0]])
            pltpu.sync_copy(rows_vmem, out_hbm.at[pb_vmem.at[0]])

        pltpu.emit_pipeline(
            body, grid=(t // win,),
            in_specs=[pl.BlockSpec((win, d), lambda i: (i, 0)),
                      pl.BlockSpec((1, win), lambda i: (i, 0)),
                      pl.BlockSpec((1, win), lambda i: (i, 0))],
            out_specs=[],
            core_axis_name=("c", "s"),
            dimension_semantics=(pltpu.PARALLEL,),
        )(rows_hbm, pa_hbm, pb_hbm)

    return run(rows, pos_a, pos_b)


def _sc_gather(table, idx):
    d = table.shape[1]
    n_win, win = idx.shape

    @functools.partial(pl.kernel, out_type=jax.ShapeDtypeStruct((n_win * win, d), table.dtype),
                       mesh=_sc_mesh(), scratch_types=[], name="sc_gather")
    def run(table_hbm, idx_hbm, out_hbm):
        def body(idx_vmem, out_vmem):
            pltpu.sync_copy(table_hbm.at[idx_vmem.at[0]], out_vmem)

        pltpu.emit_pipeline(
            body, grid=(n_win,),
            in_specs=[pl.BlockSpec((1, win), lambda i: (i, 0))],
            out_specs=[pl.BlockSpec((win, d), lambda i: (i, 0))],
            core_axis_name=("c", "s"),
            dimension_semantics=(pltpu.PARALLEL,),
        )(idx_hbm, out_hbm)

    return run(table, idx)


def _combine_kernel(h_ref, ya_ref, yb_ref, rt_ref, o_ref):
    rt = rt_ref[...]
    r = jnp.concatenate([rt, jnp.zeros((LANES - rt.shape[0], rt.shape[1]), F32)], axis=0).T
    wa, wb = r[:, 4:5], r[:, 5:6]
    a_l, a_r = _unpack_rows(ya_ref[...])
    b_l, b_r = _unpack_rows(yb_ref[...])
    half = a_l.shape[1]
    o_ref[:, 0:half] = h_ref[:, 0:half].astype(F32) + wa * a_l + wb * b_l
    o_ref[:, half:2 * half] = h_ref[:, half:2 * half].astype(F32) + wa * a_r + wb * b_r


def _combine(hres, yg, routeT):
    t, d = hres.shape
    tm = TM_COMBINE
    nb = t // tm
    return pl.pallas_call(
        _combine_kernel,
        grid=(nb,),
        in_specs=[pl.BlockSpec((tm, d), lambda i: (i, 0)),
                  pl.BlockSpec((tm, d // 2), lambda i: (i, 0)),
                  pl.BlockSpec((tm, d // 2), lambda i: (i + nb, 0)),
                  pl.BlockSpec((routeT.shape[0], tm), lambda i: (0, i))],
        out_specs=pl.BlockSpec((tm, d), lambda i: (i, 0)),
        out_shape=jax.ShapeDtypeStruct((t, d), F32),
        compiler_params=pltpu.CompilerParams(
            dimension_semantics=("arbitrary",), vmem_limit_bytes=VMEM_LIMIT),
        name="combine",
    )(hres, yg, yg, routeT)


def _lambda_init(layer_idx):
    return 0.8 - 0.6 * math.exp(-0.3 * layer_idx)


def _layer(h, l, attn_norm_g, w_in, conv_w, conv_out_g, q_norm_g, k_norm_g,
           lambda_q1, lambda_k1, lambda_q2, lambda_k2, attn_subln_g, w_out,
           ffn_norm_g, w_router_group, w_router_expert, w_exp_gate, w_exp_up, w_exp_down):
    batch, seq, d = h.shape
    t = batch * seq
    dc = conv_w.shape[-1]
    lam_init = _lambda_init(l)
    x2 = h.reshape(t, d)

    reps = dc // HEAD_DIM
    assert dc // CONV_GROUPS == HEAD_DIM
    qg = (jnp.tile(q_norm_g[l], reps) * (HEAD_DIM ** -0.5 * math.log2(math.e))).reshape(1, dc)
    kg = jnp.tile(k_norm_g[l], reps).reshape(1, dc)
    grp = jnp.arange(MXU_TILE) // HEAD_DIM
    gmat = jnp.where(grp[:, None] == grp[None, :], 1.0 / HEAD_DIM, 0.0).astype(BF16)
    yc, qT, k, vT = _inproj(x2, attn_norm_g[l].reshape(1, d), w_in[l].astype(BF16), conv_w[l],
                          conv_out_g[l].reshape(1, dc), qg, kg, gmat, batch=batch, seq=seq)

    lam_params = jnp.stack([lambda_q1[l], lambda_k1[l], lambda_q2[l], lambda_k2[l]])
    at = _attention(lam_params, attn_subln_g[l].reshape(V_DIM, 1), qT, k, vT,
                    batch=batch, seq=seq, lam_init=lam_init)

    wr = jnp.concatenate([w_router_expert[l], w_router_group[l],
                          jnp.zeros((d, LANES - N_EXPERTS - N_GROUPS), F32)], axis=1)
    wr_hi = wr.astype(BF16)
    wr_lo = (wr - wr_hi.astype(F32)).astype(BF16)
    hres, hn2, routeT, cnt = _outproj_router(x2, yc, at, w_out[l].astype(BF16),
                                                    ffn_norm_g[l].reshape(1, d),
                                                    jnp.concatenate([wr_hi, wr_lo], axis=1))

    tmm = TM_MOE
    n_tiles_max = (2 * t) // tmm + N_EXPERTS
    p_rows = n_tiles_max * tmm
    pos, tile_expert, next_expert, stage_slot, n_tiles = _plan(
        cnt[:, 0].astype(jnp.int32), routeT, n_tiles_max)

    posw = pos.reshape(2 * t // SC_WIN, SC_WIN)
    pos1w = posw[:t // SC_WIN]
    pos2w = posw[t // SC_WIN:]
    xs = _sc_dispatch(hn2, pos1w, pos2w, p_rows)

    f = w_exp_gate.shape[-1]
    ys = _moe(tile_expert, n_tiles, next_expert, stage_slot, xs,
              w_exp_gate[l].reshape(N_EXPERTS, d, f),
              w_exp_up[l].reshape(N_EXPERTS, d, f),
              w_exp_down[l].reshape(N_EXPERTS, f, d))
    yg = _sc_gather(ys, posw)
    out = _combine(hres, yg, routeT)
    return out.reshape(batch, seq, d)


def kernel(x, attn_norm_g, w_in, conv_w, conv_out_g, q_norm_g, k_norm_g, lambda_q1, lambda_k1,
           lambda_q2, lambda_k2, attn_subln_g, w_out, ffn_norm_g, w_router_group, w_router_expert,
           w_exp_gate, w_exp_up, w_exp_down):
    h = x
    for l in range(attn_norm_g.shape[0]):
        h = _layer(h, l, attn_norm_g, w_in, conv_w, conv_out_g, q_norm_g, k_norm_g,
                   lambda_q1, lambda_k1, lambda_q2, lambda_k2, attn_subln_g, w_out,
                   ffn_norm_g, w_router_group, w_router_expert, w_exp_gate, w_exp_up, w_exp_down)
    return h
```

```python
import functools
import math

import jax
import jax.numpy as jnp
from jax import lax
from jax.experimental import pallas as pl
from jax.experimental.pallas import tpu as pltpu
from jax.experimental.pallas import tpu_sc as plsc

F32 = jnp.float32
BF16 = jnp.bfloat16

HEAD_DIM = 64
V_DIM = 2 * HEAD_DIM
CONV_GROUPS = 8
N_GROUPS = 4
EXPERTS_PER_GROUP = 8
N_EXPERTS = N_GROUPS * EXPERTS_PER_GROUP
EPS = 1e-6
LANES = 128
MXU_TILE = 256
ONES_ROWS = 16
VMEM_LIMIT = 48 * 1024 * 1024

TM_PROJ = 256
TM_COMBINE = 1024
OUTPROJ_SUBTILES = 4
INPROJ_SUBTILES = 2
TQ = 512
TK = 512
ATTN_HEADS_PER_STEP = 2
TM_MOE = 512
MOE_TILES_PER_STEP = 2
SC_CORES = 2
SC_SUBCORES = 16
SC_WIN = 64


def _dot(a, b):
    return jnp.dot(a, b, preferred_element_type=F32)


def _pack_rows(x):
    w = x.shape[1] // 2
    bits = lax.bitcast_convert_type(x.astype(BF16).astype(F32), jnp.uint32)
    return lax.bitcast_convert_type((bits[:, :w] >> 16) | bits[:, w:], jnp.int32)


def _unpack_rows(packed):
    bits = lax.bitcast_convert_type(packed, jnp.uint32)
    left = lax.bitcast_convert_type(bits << 16, F32)
    right = lax.bitcast_convert_type(bits & jnp.uint32(0xFFFF0000), F32)
    return left, right


def _group_mean(sq, gm):
    w = gm.shape[0]
    sq = sq.astype(BF16)
    return jnp.concatenate([_dot(sq[:, c:c + w], gm) for c in range(0, sq.shape[1], w)], axis=1)


def _inproj_kernel(x_ref, g_ref, w_ref, cw_ref, cg_ref, qg_ref, kg_ref, gm_ref,
                   yc_ref, qT_ref, k_ref, vT_ref, carry_ref, *, tm, dc):
    j = pl.program_id(1)

    @pl.when(j == 0)
    def _():
        carry_ref[...] = jnp.zeros_like(carry_ref)

    for u in range(x_ref.shape[0] // tm):
        _inproj_subtile(u, x_ref, g_ref, w_ref, cw_ref, cg_ref, qg_ref, kg_ref, gm_ref,
                        yc_ref, qT_ref, k_ref, vT_ref, carry_ref, tm=tm, dc=dc)


def _inproj_subtile(u, x_ref, g_ref, w_ref, cw_ref, cg_ref, qg_ref, kg_ref, gm_ref,
                    yc_ref, qT_ref, k_ref, vT_ref, carry_ref, *, tm, dc):
    rows = slice(u * tm, (u + 1) * tm)
    x = x_ref[rows, :]
    ms = jnp.mean(x * x, axis=-1, keepdims=True)
    hn = (x * lax.rsqrt(ms + EPS) * g_ref[...]).astype(BF16)

    def proj(s):
        return _dot(hn, w_ref[:, s * dc:(s + 1) * dc])

    gm = gm_ref[...]

    c = proj(2) * proj(0)
    prev = carry_ref[...]
    r = lax.broadcasted_iota(jnp.int32, c.shape, 0)
    c1 = jnp.where(r == 0, prev[7:8, :], pltpu.roll(c, 1, 0))
    c2 = jnp.where(r == 0, prev[6:7, :], jnp.where(r == 1, prev[7:8, :], pltpu.roll(c, 2, 0)))
    carry_ref[...] = c[tm - 8:tm, :]
    cw = cw_ref[...]
    y = proj(1) * (cw[0:1, :] * c2 + cw[1:2, :] * c1 + cw[2:3, :] * c)
    yc_ref[rows, :] = (y * lax.rsqrt(_group_mean(y * y, gm) + EPS) * cg_ref[...]).astype(BF16)

    q = proj(3)
    qT_ref[:, rows] = (q * lax.rsqrt(_group_mean(q * q, gm) + EPS) * qg_ref[...]).T.astype(BF16)
    k = proj(4)
    k_ref[rows, :] = (k * lax.rsqrt(_group_mean(k * k, gm) + EPS) * kg_ref[...]).astype(BF16)
    vt = proj(5).T.astype(BF16)
    vrows = V_DIM + ONES_ROWS
    for h in range(dc // V_DIM):
        vT_ref[u, h * vrows:h * vrows + V_DIM, :] = vt[h * V_DIM:(h + 1) * V_DIM, :]
        vT_ref[u, h * vrows + V_DIM:(h + 1) * vrows, :] = jnp.ones((ONES_ROWS, tm), BF16)


def _inproj(x2, g, w_in, conv_w, conv_g, qg, kg, gmat, *, batch, seq):
    t, d = x2.shape
    dc = conv_g.shape[1]
    dv = dc // V_DIM * (V_DIM + ONES_ROWS)
    tm = TK
    tb = tm * INPROJ_SUBTILES
    nj = seq // tb
    row = lambda b, j: (b * nj + j, 0)
    const = lambda b, j: (0, 0)
    out_sds = jax.ShapeDtypeStruct((t, dc), BF16)
    return pl.pallas_call(
        functools.partial(_inproj_kernel, tm=tm, dc=dc),
        grid=(batch, nj),
        in_specs=[
            pl.BlockSpec((tb, d), row),
            pl.BlockSpec((1, d), const),
            pl.BlockSpec(w_in.shape, const),
            pl.BlockSpec(conv_w.shape, const),
            pl.BlockSpec((1, dc), const),
            pl.BlockSpec((1, dc), const),
            pl.BlockSpec((1, dc), const),
            pl.BlockSpec(gmat.shape, const),
        ],
        out_specs=[
            pl.BlockSpec((tb, dc), row),
            pl.BlockSpec((dc, tb), lambda b, j: (0, b * nj + j)),
            pl.BlockSpec((tb, dc), row),
            pl.BlockSpec((INPROJ_SUBTILES, dv, tm), lambda b, j: (b * nj + j, 0, 0)),
        ],
        out_shape=[out_sds, jax.ShapeDtypeStruct((dc, t), BF16), out_sds,
                   jax.ShapeDtypeStruct((t // tm, dv, tm), BF16)],
        scratch_shapes=[pltpu.VMEM((8, dc), F32)],
        compiler_params=pltpu.CompilerParams(
            dimension_semantics=("arbitrary", "arbitrary"), vmem_limit_bytes=VMEM_LIMIT),
        name="inproj_conv_qknorm",
    )(x2, g, w_in, conv_w, conv_g, qg, kg, gmat)


def _attn_kernel(lp_ref, sg_ref, qT_ref, k_ref, vT_ref, o_ref, m_ref, acc_ref,
                 sa_ref, pb_ref, ab_ref, *, tq, tk, nq, heads, lam_init):
    map_a = slice(0, tq)
    map_b = slice(tq, 2 * tq)

    lp = lp_ref[...]
    lam = (jnp.exp(jnp.sum(lp[0:1, :] * lp[1:2, :], axis=-1, keepdims=True))
           - jnp.exp(jnp.sum(lp[2:3, :] * lp[3:4, :], axis=-1, keepdims=True)) + lam_init)

    vrows = V_DIM + ONES_ROWS

    def stacked_queries(hd, qi):
        qT = qT_ref[hd * V_DIM:(hd + 1) * V_DIM, qi * tq:(qi + 1) * tq]
        row = lax.broadcasted_iota(jnp.int32, qT.shape, 0)
        zero = jnp.zeros_like(qT)
        return jnp.concatenate([jnp.where(row < HEAD_DIM, qT, zero),
                                jnp.where(row >= HEAD_DIM, qT, zero)], axis=1)

    def pieces(diag):
        return ((slice(0, tq // 2), tk // 2), (slice(tq // 2, tq), tk)) if diag else ((slice(0, tq), tk),)

    def shifted(cols, off):
        return slice(cols.start + off, cols.stop + off)

    def scores(hd, j, qq, cols, n_keys):
        return _dot(k_ref[j * tk:j * tk + n_keys, hd * V_DIM:(hd + 1) * V_DIM], qq[:, cols])

    def values(hd, j, n_keys):
        return vT_ref[j, hd * vrows:(hd + 1) * vrows, 0:n_keys]

    def softmax(m, cols, s, q0):
        if q0 is not None:
            qrel = q0 + lax.broadcasted_iota(jnp.int32, s.shape, 1)
            krel = lax.broadcasted_iota(jnp.int32, s.shape, 0)
            s = jnp.where(krel <= qrel, s, -jnp.inf)
        sb = s.astype(BF16)
        m_old = m[:, cols]
        m_new = jnp.maximum(m_old, jnp.max(sb, axis=0, keepdims=True).astype(F32))
        alpha = jnp.exp2(m_old - m_new)
        p = jnp.exp2(sb - m_new.astype(BF16))
        m[:, cols] = m_new
        return alpha, p

    def accumulate(acc, cols, alpha, vb, p):
        acc[:, cols] = alpha * acc[:, cols] + _dot(vb, p)

    def prefetch_map_a(sa, hd, j, qq, diag):
        for cols, n_keys in pieces(diag):
            sa[0:n_keys, cols] = scores(hd, j, qq, cols, n_keys)

    tiles = [(hd, qi) for hd in range(heads) for qi in range(nq)]
    stacked = {0: stacked_queries(*tiles[0])}
    m_ref[0] = jnp.full(m_ref.shape[1:], -jnp.inf, F32)
    acc_ref[0] = jnp.zeros(acc_ref.shape[1:], F32)
    prefetch_map_a(sa_ref.at[0], 0, 0, stacked[0], True)

    for n, (hd, qi) in enumerate(tiles):
        par = n % 2
        m, acc, sa, qq = m_ref.at[par], acc_ref.at[par], sa_ref.at[par], stacked[n]
        for j in range(qi + 1):
            last = j == qi
            if j > 0:
                accumulate(acc, map_b, ab_ref[...], values(hd, j - 1, tk), pb_ref[...])
            s_b = [scores(hd, j, qq, shifted(cols, tq), n_keys) for cols, n_keys in pieces(last)]
            for cols, n_keys in pieces(last):
                alpha_a, p_a = softmax(m, cols, sa[0:n_keys, cols], cols.start if last else None)
                accumulate(acc, cols, alpha_a, values(hd, j, n_keys), p_a)
            if not last:
                prefetch_map_a(sa, hd, j + 1, qq, j + 1 == qi)
            elif n + 1 < len(tiles):
                hd_next, qi_next = tiles[n + 1]
                stacked[n + 1] = stacked_queries(hd_next, qi_next)
                m_ref[1 - par] = jnp.full(m_ref.shape[1:], -jnp.inf, F32)
                acc_ref[1 - par] = jnp.zeros(acc_ref.shape[1:], F32)
                prefetch_map_a(sa_ref.at[1 - par], hd_next, 0, stacked[n + 1], qi_next == 0)
            for (cols, n_keys), s in zip(pieces(last), s_b):
                alpha_b, p_b = softmax(m, shifted(cols, tq), s, cols.start if last else None)
                if last:
                    accumulate(acc, shifted(cols, tq), alpha_b, values(hd, j, n_keys), p_b)
                else:
                    ab_ref[...] = alpha_b
                    pb_ref[...] = p_b

        o = acc[0:V_DIM, :] / acc[V_DIM:V_DIM + 1, :]
        d = o[:, map_a] - lam * o[:, map_b]
        ms = jnp.mean(d * d, axis=0, keepdims=True)
        o_ref[qi * tq:(qi + 1) * tq, hd * V_DIM:(hd + 1) * V_DIM] = (
            d * lax.rsqrt(ms + EPS) * sg_ref[...] * (1.0 - lam_init)).T.astype(BF16)
        del stacked[n]


def _attention(lam_params, subln_g, qT, k, vT, *, batch, seq, lam_init):
    dq, t = qT.shape
    n_heads = dq // V_DIM
    tq, tk = TQ, TK
    assert tq == tk and vT.shape[2] == tk
    nq = seq // tq
    nk = seq // tk
    vrows = V_DIM + ONES_ROWS
    hps = ATTN_HEADS_PER_STEP
    const = lambda b, h: (0, 0)
    return pl.pallas_call(
        functools.partial(_attn_kernel, tq=tq, tk=tk, nq=nq, heads=hps, lam_init=lam_init),
        grid=(batch, n_heads // hps),
        in_specs=[
            pl.BlockSpec(lam_params.shape, const),
            pl.BlockSpec((V_DIM, 1), const),
            pl.BlockSpec((hps * V_DIM, seq), lambda b, h: (h, b)),
            pl.BlockSpec((seq, hps * V_DIM), lambda b, h: (b, h)),
            pl.BlockSpec((nk, hps * vrows, tk), lambda b, h: (b, h, 0)),
        ],
        out_specs=pl.BlockSpec((seq, hps * V_DIM), lambda b, h: (b, h)),
        out_shape=jax.ShapeDtypeStruct((t, dq), BF16),
        scratch_shapes=[pltpu.VMEM((2, 1, 2 * tq), F32),
                        pltpu.VMEM((2, vrows, 2 * tq), F32),
                        pltpu.VMEM((2, tk, tq), F32), pltpu.VMEM((tk, tq), BF16), pltpu.VMEM((1, tq), F32)],
        compiler_params=pltpu.CompilerParams(
            dimension_semantics=("arbitrary", "arbitrary"), vmem_limit_bytes=VMEM_LIMIT),
        name="diff_attention",
    )(lam_params, subln_g, qT, k, vT)


def _outproj_router_kernel(x_ref, yc_ref, at_ref, wo_ref, g_ref, wr_ref,
                           h_ref, hn_ref, routeT_ref, cnt_ref, *, tm, dc):
    i = pl.program_id(0)

    @pl.when(i == 0)
    def _():
        cnt_ref[...] = jnp.zeros_like(cnt_ref)

    for r0 in range(0, x_ref.shape[0], tm):
        _outproj_router_subtile(slice(r0, r0 + tm), x_ref, yc_ref, at_ref, wo_ref, g_ref, wr_ref,
                                h_ref, hn_ref, routeT_ref, cnt_ref, tm=tm, dc=dc)


def _outproj_router_subtile(rows, x_ref, yc_ref, at_ref, wo_ref, g_ref, wr_ref,
                            h_ref, hn_ref, routeT_ref, cnt_ref, *, tm, dc):
    h = x_ref[rows, :] + _dot(yc_ref[rows, :], wo_ref[0:dc, :]) + _dot(at_ref[rows, :], wo_ref[dc:2 * dc, :])
    h_ref[rows, :] = h.astype(BF16)
    ms = jnp.mean(h * h, axis=-1, keepdims=True)
    hn = h * lax.rsqrt(ms + EPS) * g_ref[...]
    hi = hn.astype(BF16)
    hn_ref[rows, :] = _pack_rows(hn)
    lo = (hn - hi.astype(F32)).astype(BF16)
    prod = _dot(hi, wr_ref[...])
    logits = prod[:, 0:LANES] + prod[:, LANES:2 * LANES] + _dot(lo, wr_ref[:, 0:LANES])

    lt = logits.T
    neg = -jnp.inf
    grow = lax.broadcasted_iota(jnp.int32, (8, tm), 0).astype(F32)
    gl = jnp.where(grow < N_GROUPS, lt[N_EXPERTS:N_EXPERTS + 8, :], neg)
    gmax = jnp.max(gl, axis=0, keepdims=True)
    g_gate = 1.0 / jnp.sum(jnp.exp(gl - gmax), axis=0, keepdims=True)
    g_idx = jnp.min(jnp.where(gl == gmax, grow, 1e9), axis=0, keepdims=True)
    erow = lax.broadcasted_iota(jnp.int32, (N_EXPERTS, tm), 0).astype(F32)
    e_lo = g_idx * EXPERTS_PER_GROUP
    el = jnp.where((erow >= e_lo) & (erow < e_lo + EXPERTS_PER_GROUP), lt[0:N_EXPERTS, :], neg)
    v1 = jnp.max(el, axis=0, keepdims=True)
    i1 = jnp.min(jnp.where(el == v1, erow, 1e9), axis=0, keepdims=True)
    el2 = jnp.where(erow == i1, neg, el)
    v2 = jnp.max(el2, axis=0, keepdims=True)
    i2 = jnp.min(jnp.where(el2 == v2, erow, 1e9), axis=0, keepdims=True)
    tt = jnp.exp(v2 - v1)
    w1 = g_gate / (1.0 + tt)
    w2 = g_gate * tt / (1.0 + tt)

    sel1 = erow == i1
    sel2 = erow == i2
    oh = jnp.where(sel1 | sel2, 1.0, 0.0)
    ss = lax.broadcasted_iota(jnp.int32, (tm, tm), 0)
    tt_i = lax.broadcasted_iota(jnp.int32, (tm, tm), 1)
    earlier = jnp.where(ss < tt_i, 1.0, 0.0).astype(BF16)
    ranks = _dot(oh.astype(BF16), earlier) + cnt_ref[...]
    r1 = jnp.sum(jnp.where(sel1, ranks, 0.0), axis=0, keepdims=True)
    r2 = jnp.sum(jnp.where(sel2, ranks, 0.0), axis=0, keepdims=True)
    cnt_ref[...] = cnt_ref[...] + jnp.sum(oh, axis=1, keepdims=True)

    routeT = jnp.concatenate([i1, i2, r1, r2, w1, w2, jnp.zeros((2, tm), F32)], axis=0)
    routeT_ref[:, rows] = routeT


def _outproj_router(x2, yc, at, w_out, g, wr_cat):
    t, d = x2.shape
    dc = yc.shape[1]
    tm = TM_PROJ
    tb = tm * OUTPROJ_SUBTILES
    row = lambda i: (i, 0)
    const = lambda i: (0, 0)
    return pl.pallas_call(
        functools.partial(_outproj_router_kernel, tm=tm, dc=dc),
        grid=(t // tb,),
        in_specs=[
            pl.BlockSpec((tb, d), row),
            pl.BlockSpec((tb, dc), row),
            pl.BlockSpec((tb, dc), row),
            pl.BlockSpec(w_out.shape, const),
            pl.BlockSpec((1, d), const),
            pl.BlockSpec(wr_cat.shape, const),
        ],
        out_specs=[
            pl.BlockSpec((tb, d), row),
            pl.BlockSpec((tb, d // 2), row),
            pl.BlockSpec((8, tb), lambda i: (0, i)),
            pl.BlockSpec((N_EXPERTS, 1), const),
        ],
        out_shape=[
            jax.ShapeDtypeStruct((t, d), BF16),
            jax.ShapeDtypeStruct((t, d // 2), jnp.int32),
            jax.ShapeDtypeStruct((8, t), F32),
            jax.ShapeDtypeStruct((N_EXPERTS, 1), F32),
        ],
        compiler_params=pltpu.CompilerParams(
            dimension_semantics=("arbitrary",), vmem_limit_bytes=VMEM_LIMIT),
        name="outproj_router",
    )(x2, yc, at, w_out, g, wr_cat)


def _plan_kernel(cnt_ref, rt_ref, pos_ref, te_ref, nx_ref, sl_ref, nt_ref, *, tm):
    shift = tm.bit_length() - 1
    tiles, starts = [], []
    total = jnp.int32(0)
    for e in range(N_EXPERTS):
        n = lax.shift_right_logical(cnt_ref[e] + (tm - 1), shift)
        tiles.append(n)
        starts.append(total)
        total = total + n
    nt_ref[0] = total

    rt = rt_ref[...]
    ea, eb = rt[0:1, :], rt[1:2, :]
    sa = jnp.zeros_like(ea)
    sb = jnp.zeros_like(eb)
    for e in range(N_EXPERTS):
        start = (starts[e] * tm).astype(F32)
        sa = jnp.where(ea == e, start, sa)
        sb = jnp.where(eb == e, start, sb)
    pos_ref[0:1, :] = (sa + rt[2:3, :]).astype(jnp.int32)
    pos_ref[1:2, :] = (sb + rt[3:4, :]).astype(jnp.int32)

    def clear(i, c):
        te_ref[i] = 0
        nx_ref[i] = -1
        sl_ref[i] = 0
        return c

    lax.fori_loop(0, te_ref.shape[0], clear, 0)

    nxt = jnp.int32(-1)
    next_of = [None] * N_EXPERTS
    for e in reversed(range(N_EXPERTS)):
        next_of[e] = nxt
        nxt = jnp.where(tiles[e] > 0, e, nxt)
    ordinal = jnp.int32(0)
    for e in range(N_EXPERTS):
        slot = ordinal & 1

        def fill(j, c, e=e, slot=slot):
            te_ref[starts[e] + j] = e
            nx_ref[starts[e] + j] = next_of[e]
            sl_ref[starts[e] + j] = slot
            return c

        lax.fori_loop(0, tiles[e], fill, 0)
        ordinal = ordinal + (tiles[e] > 0).astype(jnp.int32)


def _plan(counts, routeT, n_tiles_max):
    t = routeT.shape[1]
    smem = pl.BlockSpec(memory_space=pltpu.SMEM)
    table = jax.ShapeDtypeStruct((n_tiles_max,), jnp.int32)
    return pl.pallas_call(
        functools.partial(_plan_kernel, tm=TM_MOE),
        grid_spec=pltpu.PrefetchScalarGridSpec(
            num_scalar_prefetch=1, grid=(1,),
            in_specs=[pl.BlockSpec(routeT.shape, lambda i, cnt: (0, 0))],
            out_specs=[pl.BlockSpec((2, t), lambda i, cnt: (0, 0)), smem, smem, smem, smem],
        ),
        out_shape=[jax.ShapeDtypeStruct((2, t), jnp.int32), table, table, table,
                   jax.ShapeDtypeStruct((1,), jnp.int32)],
        name="routing_plan",
    )(counts, routeT)


def _moe_kernel(te_ref, nt_ref, nx_ref, sl_ref, x_ref, wg_hbm, wu_hbm, wd_hbm, y_ref,
                wg_st, wu_st, wd_st, wgb_ref, wub_ref, wdb_ref, sem, *, tm, tiles_per_step):
    last = nt_ref[0] - 1

    def weight_copies(e, s):
        return (pltpu.make_async_copy(wg_hbm.at[e], wg_st.at[s], sem.at[s, 0]),
                pltpu.make_async_copy(wu_hbm.at[e], wu_st.at[s], sem.at[s, 1]),
                pltpu.make_async_copy(wd_hbm.at[e], wd_st.at[s], sem.at[s, 2]))

    for u in range(tiles_per_step):
        i = pl.program_id(0) * tiles_per_step + u
        rows = slice(u * tm, (u + 1) * tm)
        ic = jnp.minimum(i, last)
        expert = te_ref[ic]
        slot = sl_ref[ic]
        first_of_expert = (i == 0) | ((i <= last) & (expert != te_ref[jnp.maximum(ic - 1, 0)]))

        if u == 0:
            @pl.when(i == 0)
            def _():
                for c in weight_copies(expert, slot):
                    c.start()

        @pl.when(first_of_expert)
        def _():
            for c in weight_copies(expert, slot):
                c.wait()
            nxt = nx_ref[ic]

            @pl.when(nxt >= 0)
            def _():
                for c in weight_copies(nxt, 1 - slot):
                    c.start()

            wgb_ref[...] = wg_st[slot].astype(BF16)
            wub_ref[...] = wu_st[slot].astype(BF16)
            wdb_ref[...] = wd_st[slot].astype(BF16)

        @pl.when(i <= last)
        def _():
            x_l, x_r = _unpack_rows(x_ref[rows, :])
            x_l = x_l.astype(BF16)
            x_r = x_r.astype(BF16)
            half = x_l.shape[1]
            hg = _dot(x_l, wgb_ref[0:half, :]) + _dot(x_r, wgb_ref[half:2 * half, :])
            hu = _dot(x_l, wub_ref[0:half, :]) + _dot(x_r, wub_ref[half:2 * half, :])
            act = hg * (1.0 / (1.0 + jnp.exp(-hg))) * hu
            y_ref[rows, :] = _pack_rows(_dot(act.astype(BF16), wdb_ref[...]))


def _moe(tile_expert, n_tiles, next_expert, stage_slot, xs, wg, wu, wd):
    p, dp = xs.shape
    d, f = wg.shape[1], wg.shape[2]
    assert dp * 2 == d
    tm, tps = TM_MOE, MOE_TILES_PER_STEP
    tb = tm * tps
    row = lambda s, te, nt, nx, sl: (jnp.minimum(s, (nt[0] - 1) // tps), 0)
    hbm = pl.BlockSpec(memory_space=pl.ANY)
    grid_spec = pltpu.PrefetchScalarGridSpec(
        num_scalar_prefetch=4,
        grid=(p // tb,),
        in_specs=[pl.BlockSpec((tb, dp), row), hbm, hbm, hbm],
        out_specs=pl.BlockSpec((tb, dp), row),
        scratch_shapes=[pltpu.VMEM((2, d, f), F32), pltpu.VMEM((2, d, f), F32), pltpu.VMEM((2, f, d), F32),
                        pltpu.VMEM((d, f), BF16), pltpu.VMEM((d, f), BF16), pltpu.VMEM((f, d), BF16),
                        pltpu.SemaphoreType.DMA((2, 3))],
    )
    return pl.pallas_call(
        functools.partial(_moe_kernel, tm=tm, tiles_per_step=tps),
        grid_spec=grid_spec,
        out_shape=jax.ShapeDtypeStruct((p, dp), jnp.int32),
        compiler_params=pltpu.CompilerParams(
            dimension_semantics=("arbitrary",), vmem_limit_bytes=VMEM_LIMIT),
        name="moe_experts",
    )(tile_expert, n_tiles, next_expert, stage_slot, xs, wg, wu, wd)


def _sc_mesh():
    return plsc.VectorSubcoreMesh(core_axis_name="c", subcore_axis_name="s",
                                  num_cores=SC_CORES, num_subcores=SC_SUBCORES)


def _sc_dispatch(rows, pos_a, pos_b, n_out):
    t, d = rows.shape
    win = pos_a.shape[1]

    @functools.partial(pl.kernel, out_type=jax.ShapeDtypeStruct((n_out, d), rows.dtype),
                       mesh=_sc_mesh(), scratch_types=[], name="sc_dispatch")
    def run(rows_hbm, pa_hbm, pb_hbm, out_hbm):
        def body(rows_vmem, pa_vmem, pb_vmem):
            pltpu.sync_copy(rows_vmem, out_hbm.at[pa_vmem.at[0]])
            pltpu.sync_copy(rows_vmem, out_hbm.at[pb_vmem.at[0]])

        pltpu.emit_pipeline(
            body, grid=(t // win,),
            in_specs=[pl.BlockSpec((win, d), lambda i: (i, 0)),
                      pl.BlockSpec((1, win), lambda i: (i, 0)),
                      pl.BlockSpec((1, win), lambda i: (i, 0))],
            out_specs=[],
            core_axis_name=("c", "s"),
            dimension_semantics=(pltpu.PARALLEL,),
        )(rows_hbm, pa_hbm, pb_hbm)

    return run(rows, pos_a, pos_b)


def _sc_gather(table, idx):
    d = table.shape[1]
    n_win, win = idx.shape

    @functools.partial(pl.kernel, out_type=jax.ShapeDtypeStruct((n_win * win, d), table.dtype),
                       mesh=_sc_mesh(), scratch_types=[], name="sc_gather")
    def run(table_hbm, idx_hbm, out_hbm):
        def body(idx_vmem, out_vmem):
            pltpu.sync_copy(table_hbm.at[idx_vmem.at[0]], out_vmem)

        pltpu.emit_pipeline(
            body, grid=(n_win,),
            in_specs=[pl.BlockSpec((1, win), lambda i: (i, 0))],
            out_specs=[pl.BlockSpec((win, d), lambda i: (i, 0))],
            core_axis_name=("c", "s"),
            dimension_semantics=(pltpu.PARALLEL,),
        )(idx_hbm, out_hbm)

    return run(table, idx)


def _combine_kernel(h_ref, ya_ref, yb_ref, rt_ref, o_ref):
    rt = rt_ref[...]
    r = jnp.concatenate([rt, jnp.zeros((LANES - rt.shape[0], rt.shape[1]), F32)], axis=0).T
    wa, wb = r[:, 4:5], r[:, 5:6]
    a_l, a_r = _unpack_rows(ya_ref[...])
    b_l, b_r = _unpack_rows(yb_ref[...])
    half = a_l.shape[1]
    o_ref[:, 0:half] = h_ref[:, 0:half].astype(F32) + wa * a_l + wb * b_l
    o_ref[:, half:2 * half] = h_ref[:, half:2 * half].astype(F32) + wa * a_r + wb * b_r


def _combine(hres, yg, routeT):
    t, d = hres.shape
    tm = TM_COMBINE
    nb = t // tm
    return pl.pallas_call(
        _combine_kernel,
        grid=(nb,),
        in_specs=[pl.BlockSpec((tm, d), lambda i: (i, 0)),
                  pl.BlockSpec((tm, d // 2), lambda i: (i, 0)),
                  pl.BlockSpec((tm, d // 2), lambda i: (i + nb, 0)),
                  pl.BlockSpec((routeT.shape[0], tm), lambda i: (0, i))],
        out_specs=pl.BlockSpec((tm, d), lambda i: (i, 0)),
        out_shape=jax.ShapeDtypeStruct((t, d), F32),
        compiler_params=pltpu.CompilerParams(
            dimension_semantics=("arbitrary",), vmem_limit_bytes=VMEM_LIMIT),
        name="combine",
    )(hres, yg, yg, routeT)


def _lambda_init(layer_idx):
    return 0.8 - 0.6 * math.exp(-0.3 * layer_idx)


def _layer(h, l, attn_norm_g, w_in, conv_w, conv_out_g, q_norm_g, k_norm_g,
           lambda_q1, lambda_k1, lambda_q2, lambda_k2, attn_subln_g, w_out,
           ffn_norm_g, w_router_group, w_router_expert, w_exp_gate, w_exp_up, w_exp_down):
    batch, seq, d = h.shape
    t = batch * seq
    dc = conv_w.shape[-1]
    lam_init = _lambda_init(l)
    x2 = h.reshape(t, d)

    reps = dc // HEAD_DIM
    assert dc // CONV_GROUPS == HEAD_DIM
    qg = (jnp.tile(q_norm_g[l], reps) * (HEAD_DIM ** -0.5 * math.log2(math.e))).reshape(1, dc)
    kg = jnp.tile(k_norm_g[l], reps).reshape(1, dc)
    grp = jnp.arange(MXU_TILE) // HEAD_DIM
    gmat = jnp.where(grp[:, None] == grp[None, :], 1.0 / HEAD_DIM, 0.0).astype(BF16)
    yc, qT, k, vT = _inproj(x2, attn_norm_g[l].reshape(1, d), w_in[l].astype(BF16), conv_w[l],
                          conv_out_g[l].reshape(1, dc), qg, kg, gmat, batch=batch, seq=seq)

    lam_params = jnp.stack([lambda_q1[l], lambda_k1[l], lambda_q2[l], lambda_k2[l]])
    at = _attention(lam_params, attn_subln_g[l].reshape(V_DIM, 1), qT, k, vT,
                    batch=batch, seq=seq, lam_init=lam_init)

    wr = jnp.concatenate([w_router_expert[l], w_router_group[l],
                          jnp.zeros((d, LANES - N_EXPERTS - N_GROUPS), F32)], axis=1)
    wr_hi = wr.astype(BF16)
    wr_lo = (wr - wr_hi.astype(F32)).astype(BF16)
    hres, hn2, routeT, cnt = _outproj_router(x2, yc, at, w_out[l].astype(BF16),
                                                    ffn_norm_g[l].reshape(1, d),
                                                    jnp.concatenate([wr_hi, wr_lo], axis=1))

    tmm = TM_MOE
    n_tiles_max = (2 * t) // tmm + N_EXPERTS
    p_rows = n_tiles_max * tmm
    pos, tile_expert, next_expert, stage_slot, n_tiles = _plan(
        cnt[:, 0].astype(jnp.int32), routeT, n_tiles_max)

    posw = pos.reshape(2 * t // SC_WIN, SC_WIN)
    pos1w = posw[:t // SC_WIN]
    pos2w = posw[t // SC_WIN:]
    xs = _sc_dispatch(hn2, pos1w, pos2w, p_rows)

    f = w_exp_gate.shape[-1]
    ys = _moe(tile_expert, n_tiles, next_expert, stage_slot, xs,
              w_exp_gate[l].reshape(N_EXPERTS, d, f),
              w_exp_up[l].reshape(N_EXPERTS, d, f),
              w_exp_down[l].reshape(N_EXPERTS, f, d))
    yg = _sc_gather(ys, posw)
    out = _combine(hres, yg, routeT)
    return out.reshape(batch, seq, d)


def kernel(x, attn_norm_g, w_in, conv_w, conv_out_g, q_norm_g, k_norm_g, lambda_q1, lambda_k1,
           lambda_q2, lambda_k2, attn_subln_g, w_out, ffn_norm_g, w_router_group, w_router_expert,
           w_exp_gate, w_exp_up, w_exp_down):
    h = x
    for l in range(attn_norm_g.shape[0]):
        h = _layer(h, l, attn_norm_g, w_in, conv_w, conv_out_g, q_norm_g, k_norm_g,
                   lambda_q1, lambda_k1, lambda_q2, lambda_k2, attn_subln_g, w_out,
                   ffn_norm_g, w_router_group, w_router_expert, w_exp_gate, w_exp_up, w_exp_down)
    return h
```

```python
import functools
import math

import jax
import jax.numpy as jnp
from jax import lax
from jax.experimental import pallas as pl
from jax.experimental.pallas import tpu as pltpu
from jax.experimental.pallas import tpu_sc as plsc

F32 = jnp.float32
BF16 = jnp.bfloat16

HEAD_DIM = 64
V_DIM = 2 * HEAD_DIM
CONV_GROUPS = 8
N_GROUPS = 4
EXPERTS_PER_GROUP = 8
N_EXPERTS = N_GROUPS * EXPERTS_PER_GROUP
EPS = 1e-6
LANES = 128
MXU_TILE = 256
ONES_ROWS = 16
VMEM_LIMIT = 48 * 1024 * 1024

TM_PROJ = 256
TM_COMBINE = 1024
OUTPROJ_SUBTILES = 4
INPROJ_SUBTILES = 2
TQ = 512
TK = 512
TM_MOE = 512
MOE_TILES_PER_STEP = 2
SC_CORES = 2
SC_SUBCORES = 16
SC_WIN = 64


def _dot(a, b):
    return jnp.dot(a, b, preferred_element_type=F32)


def _pack_rows(x):
    w = x.shape[1] // 2
    bits = lax.bitcast_convert_type(x.astype(BF16).astype(F32), jnp.uint32)
    return lax.bitcast_convert_type((bits[:, :w] >> 16) | bits[:, w:], jnp.int32)


def _unpack_rows(packed):
    bits = lax.bitcast_convert_type(packed, jnp.uint32)
    left = lax.bitcast_convert_type(bits << 16, F32)
    right = lax.bitcast_convert_type(bits & jnp.uint32(0xFFFF0000), F32)
    return left, right


def _group_mean(sq, gm):
    w = gm.shape[0]
    sq = sq.astype(BF16)
    return jnp.concatenate([_dot(sq[:, c:c + w], gm) for c in range(0, sq.shape[1], w)], axis=1)


def _inproj_kernel(x_ref, g_ref, w_ref, cw_ref, cg_ref, qg_ref, kg_ref, gm_ref,
                   yc_ref, qT_ref, k_ref, vT_ref, carry_ref, *, tm, dc):
    j = pl.program_id(1)

    @pl.when(j == 0)
    def _():
        carry_ref[...] = jnp.zeros_like(carry_ref)

    for u in range(x_ref.shape[0] // tm):
        _inproj_subtile(u, x_ref, g_ref, w_ref, cw_ref, cg_ref, qg_ref, kg_ref, gm_ref,
                        yc_ref, qT_ref, k_ref, vT_ref, carry_ref, tm=tm, dc=dc)


def _inproj_subtile(u, x_ref, g_ref, w_ref, cw_ref, cg_ref, qg_ref, kg_ref, gm_ref,
                    yc_ref, qT_ref, k_ref, vT_ref, carry_ref, *, tm, dc):
    rows = slice(u * tm, (u + 1) * tm)
    x = x_ref[rows, :]
    ms = jnp.mean(x * x, axis=-1, keepdims=True)
    hn = (x * lax.rsqrt(ms + EPS) * g_ref[...]).astype(BF16)

    def proj(s):
        return _dot(hn, w_ref[:, s * dc:(s + 1) * dc])

    gm = gm_ref[...]

    c = proj(2) * proj(0)
    prev = carry_ref[...]
    r = lax.broadcasted_iota(jnp.int32, c.shape, 0)
    c1 = jnp.where(r == 0, prev[7:8, :], pltpu.roll(c, 1, 0))
    c2 = jnp.where(r == 0, prev[6:7, :], jnp.where(r == 1, prev[7:8, :], pltpu.roll(c, 2, 0)))
    carry_ref[...] = c[tm - 8:tm, :]
    cw = cw_ref[...]
    y = proj(1) * (cw[0:1, :] * c2 + cw[1:2, :] * c1 + cw[2:3, :] * c)
    yc_ref[rows, :] = (y * lax.rsqrt(_group_mean(y * y, gm) + EPS) * cg_ref[...]).astype(BF16)

    q = proj(3)
    qT_ref[:, rows] = (q * lax.rsqrt(_group_mean(q * q, gm) + EPS) * qg_ref[...]).T.astype(BF16)
    k = proj(4)
    k_ref[rows, :] = (k * lax.rsqrt(_group_mean(k * k, gm) + EPS) * kg_ref[...]).astype(BF16)
    vt = proj(5).T.astype(BF16)
    vrows = V_DIM + ONES_ROWS
    for h in range(dc // V_DIM):
        vT_ref[u, h * vrows:h * vrows + V_DIM, :] = vt[h * V_DIM:(h + 1) * V_DIM, :]
        vT_ref[u, h * vrows + V_DIM:(h + 1) * vrows, :] = jnp.ones((ONES_ROWS, tm), BF16)


def _inproj(x2, g, w_in, conv_w, conv_g, qg, kg, gmat, *, batch, seq):
    t, d = x2.shape
    dc = conv_g.shape[1]
    dv = dc // V_DIM * (V_DIM + ONES_ROWS)
    tm = TK
    tb = tm * INPROJ_SUBTILES
    nj = seq // tb
    row = lambda b, j: (b * nj + j, 0)
    const = lambda b, j: (0, 0)
    out_sds = jax.ShapeDtypeStruct((t, dc), BF16)
    return pl.pallas_call(
        functools.partial(_inproj_kernel, tm=tm, dc=dc),
        grid=(batch, nj),
        in_specs=[
            pl.BlockSpec((tb, d), row),
            pl.BlockSpec((1, d), const),
            pl.BlockSpec(w_in.shape, const),
            pl.BlockSpec(conv_w.shape, const),
            pl.BlockSpec((1, dc), const),
            pl.BlockSpec((1, dc), const),
            pl.BlockSpec((1, dc), const),
            pl.BlockSpec(gmat.shape, const),
        ],
        out_specs=[
            pl.BlockSpec((tb, dc), row),
            pl.BlockSpec((dc, tb), lambda b, j: (0, b * nj + j)),
            pl.BlockSpec((tb, dc), row),
            pl.BlockSpec((INPROJ_SUBTILES, dv, tm), lambda b, j: (b * nj + j, 0, 0)),
        ],
        out_shape=[out_sds, jax.ShapeDtypeStruct((dc, t), BF16), out_sds,
                   jax.ShapeDtypeStruct((t // tm, dv, tm), BF16)],
        scratch_shapes=[pltpu.VMEM((8, dc), F32)],
        compiler_params=pltpu.CompilerParams(
            dimension_semantics=("arbitrary", "arbitrary"), vmem_limit_bytes=VMEM_LIMIT),
        name="inproj_conv_qknorm",
    )(x2, g, w_in, conv_w, conv_g, qg, kg, gmat)


def _attn_kernel(lp_ref, sg_ref, qT_ref, k_ref, vT_ref, o_ref, m_ref, acc_ref,
                 sa_ref, pb_ref, ab_ref, *, tq, tk, nq, lam_init):
    map_a = slice(0, tq)
    map_b = slice(tq, 2 * tq)

    lp = lp_ref[...]
    lam = (jnp.exp(jnp.sum(lp[0:1, :] * lp[1:2, :], axis=-1, keepdims=True))
           - jnp.exp(jnp.sum(lp[2:3, :] * lp[3:4, :], axis=-1, keepdims=True)) + lam_init)

    def stacked_queries(qi):
        qT = qT_ref[:, qi * tq:(qi + 1) * tq]
        row = lax.broadcasted_iota(jnp.int32, qT.shape, 0)
        zero = jnp.zeros_like(qT)
        return jnp.concatenate([jnp.where(row < HEAD_DIM, qT, zero),
                                jnp.where(row >= HEAD_DIM, qT, zero)], axis=1)

    def pieces(diag):
        return ((slice(0, tq // 2), tk // 2), (slice(tq // 2, tq), tk)) if diag else ((slice(0, tq), tk),)

    def shifted(cols, off):
        return slice(cols.start + off, cols.stop + off)

    def scores(j, qq, cols, n_keys):
        return _dot(k_ref[j * tk:j * tk + n_keys, :], qq[:, cols])

    def softmax(m, cols, s, q0):
        if q0 is not None:
            qrel = q0 + lax.broadcasted_iota(jnp.int32, s.shape, 1)
            krel = lax.broadcasted_iota(jnp.int32, s.shape, 0)
            s = jnp.where(krel <= qrel, s, -jnp.inf)
        sb = s.astype(BF16)
        m_old = m[:, cols]
        m_new = jnp.maximum(m_old, jnp.max(sb, axis=0, keepdims=True).astype(F32))
        alpha = jnp.exp2(m_old - m_new)
        p = jnp.exp2(sb - m_new.astype(BF16))
        m[:, cols] = m_new
        return alpha, p

    def accumulate(acc, cols, alpha, vb, p):
        acc[:, cols] = alpha * acc[:, cols] + _dot(vb, p)

    def prefetch_map_a(sa, j, qq, diag):
        for cols, n_keys in pieces(diag):
            sa[0:n_keys, cols] = scores(j, qq, cols, n_keys)

    stacked = {0: stacked_queries(0)}
    m_ref[0] = jnp.full(m_ref.shape[1:], -jnp.inf, F32)
    acc_ref[0] = jnp.zeros(acc_ref.shape[1:], F32)
    prefetch_map_a(sa_ref.at[0], 0, stacked[0], True)

    for qi in range(nq):
        par = qi % 2
        m, acc, sa, qq = m_ref.at[par], acc_ref.at[par], sa_ref.at[par], stacked[qi]
        for j in range(qi + 1):
            last = j == qi
            if j > 0:
                accumulate(acc, map_b, ab_ref[...], vT_ref[j - 1], pb_ref[...])
            s_b = [scores(j, qq, shifted(cols, tq), n_keys) for cols, n_keys in pieces(last)]
            for cols, n_keys in pieces(last):
                alpha_a, p_a = softmax(m, cols, sa[0:n_keys, cols], cols.start if last else None)
                accumulate(acc, cols, alpha_a, vT_ref[j, :, 0:n_keys], p_a)
            if not last:
                prefetch_map_a(sa, j + 1, qq, j + 1 == qi)
            elif qi + 1 < nq:
                stacked[qi + 1] = stacked_queries(qi + 1)
                m_ref[1 - par] = jnp.full(m_ref.shape[1:], -jnp.inf, F32)
                acc_ref[1 - par] = jnp.zeros(acc_ref.shape[1:], F32)
                prefetch_map_a(sa_ref.at[1 - par], 0, stacked[qi + 1], False)
            for (cols, n_keys), s in zip(pieces(last), s_b):
                alpha_b, p_b = softmax(m, shifted(cols, tq), s, cols.start if last else None)
                if last:
                    accumulate(acc, shifted(cols, tq), alpha_b, vT_ref[j, :, 0:n_keys], p_b)
                else:
                    ab_ref[...] = alpha_b
                    pb_ref[...] = p_b

        o = acc[0:V_DIM, :] / acc[V_DIM:V_DIM + 1, :]
        d = o[:, map_a] - lam * o[:, map_b]
        ms = jnp.mean(d * d, axis=0, keepdims=True)
        o_ref[qi * tq:(qi + 1) * tq, :] = (
            d * lax.rsqrt(ms + EPS) * sg_ref[...] * (1.0 - lam_init)).T.astype(BF16)
        del stacked[qi]


def _attention(lam_params, subln_g, qT, k, vT, *, batch, seq, lam_init):
    dq, t = qT.shape
    n_heads = dq // V_DIM
    tq, tk = TQ, TK
    assert tq == tk and vT.shape[2] == tk
    nq = seq // tq
    nk = seq // tk
    vrows = V_DIM + ONES_ROWS
    const = lambda b, h: (0, 0)
    return pl.pallas_call(
        functools.partial(_attn_kernel, tq=tq, tk=tk, nq=nq, lam_init=lam_init),
        grid=(batch, n_heads),
        in_specs=[
            pl.BlockSpec(lam_params.shape, const),
            pl.BlockSpec((V_DIM, 1), const),
            pl.BlockSpec((V_DIM, seq), lambda b, h: (h, b)),
            pl.BlockSpec((seq, V_DIM), lambda b, h: (b, h)),
            pl.BlockSpec((nk, vrows, tk), lambda b, h: (b, h, 0)),
        ],
        out_specs=pl.BlockSpec((seq, V_DIM), lambda b, h: (b, h)),
        out_shape=jax.ShapeDtypeStruct((t, dq), BF16),
        scratch_shapes=[pltpu.VMEM((2, 1, 2 * tq), F32),
                        pltpu.VMEM((2, vrows, 2 * tq), F32),
                        pltpu.VMEM((2, tk, tq), F32), pltpu.VMEM((tk, tq), BF16), pltpu.VMEM((1, tq), F32)],
        compiler_params=pltpu.CompilerParams(
            dimension_semantics=("arbitrary", "arbitrary"), vmem_limit_bytes=VMEM_LIMIT),
        name="diff_attention",
    )(lam_params, subln_g, qT, k, vT)


def _outproj_router_kernel(x_ref, yc_ref, at_ref, wo_ref, g_ref, wr_ref,
                           h_ref, hn_ref, routeT_ref, cnt_ref, *, tm, dc):
    i = pl.program_id(0)

    @pl.when(i == 0)
    def _():
        cnt_ref[...] = jnp.zeros_like(cnt_ref)

    for r0 in range(0, x_ref.shape[0], tm):
        _outproj_router_subtile(slice(r0, r0 + tm), x_ref, yc_ref, at_ref, wo_ref, g_ref, wr_ref,
                                h_ref, hn_ref, routeT_ref, cnt_ref, tm=tm, dc=dc)


def _outproj_router_subtile(rows, x_ref, yc_ref, at_ref, wo_ref, g_ref, wr_ref,
                            h_ref, hn_ref, routeT_ref, cnt_ref, *, tm, dc):
    h = x_ref[rows, :] + _dot(yc_ref[rows, :], wo_ref[0:dc, :]) + _dot(at_ref[rows, :], wo_ref[dc:2 * dc, :])
    h_ref[rows, :] = h.astype(BF16)
    ms = jnp.mean(h * h, axis=-1, keepdims=True)
    hn = h * lax.rsqrt(ms + EPS) * g_ref[...]
    hi = hn.astype(BF16)
    hn_ref[rows, :] = _pack_rows(hn)
    lo = (hn - hi.astype(F32)).astype(BF16)
    prod = _dot(hi, wr_ref[...])
    logits = prod[:, 0:LANES] + prod[:, LANES:2 * LANES] + _dot(lo, wr_ref[:, 0:LANES])

    lt = logits.T
    neg = -jnp.inf
    grow = lax.broadcasted_iota(jnp.int32, (8, tm), 0).astype(F32)
    gl = jnp.where(grow < N_GROUPS, lt[N_EXPERTS:N_EXPERTS + 8, :], neg)
    gmax = jnp.max(gl, axis=0, keepdims=True)
    g_gate = 1.0 / jnp.sum(jnp.exp(gl - gmax), axis=0, keepdims=True)
    g_idx = jnp.min(jnp.where(gl == gmax, grow, 1e9), axis=0, keepdims=True)
    erow = lax.broadcasted_iota(jnp.int32, (N_EXPERTS, tm), 0).astype(F32)
    e_lo = g_idx * EXPERTS_PER_GROUP
    el = jnp.where((erow >= e_lo) & (erow < e_lo + EXPERTS_PER_GROUP), lt[0:N_EXPERTS, :], neg)
    v1 = jnp.max(el, axis=0, keepdims=True)
    i1 = jnp.min(jnp.where(el == v1, erow, 1e9), axis=0, keepdims=True)
    el2 = jnp.where(erow == i1, neg, el)
    v2 = jnp.max(el2, axis=0, keepdims=True)
    i2 = jnp.min(jnp.where(el2 == v2, erow, 1e9), axis=0, keepdims=True)
    tt = jnp.exp(v2 - v1)
    w1 = g_gate / (1.0 + tt)
    w2 = g_gate * tt / (1.0 + tt)

    sel1 = erow == i1
    sel2 = erow == i2
    oh = jnp.where(sel1 | sel2, 1.0, 0.0)
    ss = lax.broadcasted_iota(jnp.int32, (tm, tm), 0)
    tt_i = lax.broadcasted_iota(jnp.int32, (tm, tm), 1)
    earlier = jnp.where(ss < tt_i, 1.0, 0.0).astype(BF16)
    ranks = _dot(oh.astype(BF16), earlier) + cnt_ref[...]
    r1 = jnp.sum(jnp.where(sel1, ranks, 0.0), axis=0, keepdims=True)
    r2 = jnp.sum(jnp.where(sel2, ranks, 0.0), axis=0, keepdims=True)
    cnt_ref[...] = cnt_ref[...] + jnp.sum(oh, axis=1, keepdims=True)

    routeT = jnp.concatenate([i1, i2, r1, r2, w1, w2, jnp.zeros((2, tm), F32)], axis=0)
    routeT_ref[:, rows] = routeT


def _outproj_router(x2, yc, at, w_out, g, wr_cat):
    t, d = x2.shape
    dc = yc.shape[1]
    tm = TM_PROJ
    tb = tm * OUTPROJ_SUBTILES
    row = lambda i: (i, 0)
    const = lambda i: (0, 0)
    return pl.pallas_call(
        functools.partial(_outproj_router_kernel, tm=tm, dc=dc),
        grid=(t // tb,),
        in_specs=[
            pl.BlockSpec((tb, d), row),
            pl.BlockSpec((tb, dc), row),
            pl.BlockSpec((tb, dc), row),
            pl.BlockSpec(w_out.shape, const),
            pl.BlockSpec((1, d), const),
            pl.BlockSpec(wr_cat.shape, const),
        ],
        out_specs=[
            pl.BlockSpec((tb, d), row),
            pl.BlockSpec((tb, d // 2), row),
            pl.BlockSpec((8, tb), lambda i: (0, i)),
            pl.BlockSpec((N_EXPERTS, 1), const),
        ],
        out_shape=[
            jax.ShapeDtypeStruct((t, d), BF16),
            jax.ShapeDtypeStruct((t, d // 2), jnp.int32),
            jax.ShapeDtypeStruct((8, t), F32),
            jax.ShapeDtypeStruct((N_EXPERTS, 1), F32),
        ],
        compiler_params=pltpu.CompilerParams(
            dimension_semantics=("arbitrary",), vmem_limit_bytes=VMEM_LIMIT),
        name="outproj_router",
    )(x2, yc, at, w_out, g, wr_cat)


def _plan_kernel(cnt_ref, rt_ref, pos_ref, te_ref, nx_ref, sl_ref, nt_ref, *, tm):
    shift = tm.bit_length() - 1
    tiles, starts = [], []
    total = jnp.int32(0)
    for e in range(N_EXPERTS):
        n = lax.shift_right_logical(cnt_ref[e] + (tm - 1), shift)
        tiles.append(n)
        starts.append(total)
        total = total + n
    nt_ref[0] = total

    rt = rt_ref[...]
    ea, eb = rt[0:1, :], rt[1:2, :]
    sa = jnp.zeros_like(ea)
    sb = jnp.zeros_like(eb)
    for e in range(N_EXPERTS):
        start = (starts[e] * tm).astype(F32)
        sa = jnp.where(ea == e, start, sa)
        sb = jnp.where(eb == e, start, sb)
    pos_ref[0:1, :] = (sa + rt[2:3, :]).astype(jnp.int32)
    pos_ref[1:2, :] = (sb + rt[3:4, :]).astype(jnp.int32)

    def clear(i, c):
        te_ref[i] = 0
        nx_ref[i] = -1
        sl_ref[i] = 0
        return c

    lax.fori_loop(0, te_ref.shape[0], clear, 0)

    nxt = jnp.int32(-1)
    next_of = [None] * N_EXPERTS
    for e in reversed(range(N_EXPERTS)):
        next_of[e] = nxt
        nxt = jnp.where(tiles[e] > 0, e, nxt)
    ordinal = jnp.int32(0)
    for e in range(N_EXPERTS):
        slot = ordinal & 1

        def fill(j, c, e=e, slot=slot):
            te_ref[starts[e] + j] = e
            nx_ref[starts[e] + j] = next_of[e]
            sl_ref[starts[e] + j] = slot
            return c

        lax.fori_loop(0, tiles[e], fill, 0)
        ordinal = ordinal + (tiles[e] > 0).astype(jnp.int32)


def _plan(counts, routeT, n_tiles_max):
    t = routeT.shape[1]
    smem = pl.BlockSpec(memory_space=pltpu.SMEM)
    table = jax.ShapeDtypeStruct((n_tiles_max,), jnp.int32)
    return pl.pallas_call(
        functools.partial(_plan_kernel, tm=TM_MOE),
        grid_spec=pltpu.PrefetchScalarGridSpec(
            num_scalar_prefetch=1, grid=(1,),
            in_specs=[pl.BlockSpec(routeT.shape, lambda i, cnt: (0, 0))],
            out_specs=[pl.BlockSpec((2, t), lambda i, cnt: (0, 0)), smem, smem, smem, smem],
        ),
        out_shape=[jax.ShapeDtypeStruct((2, t), jnp.int32), table, table, table,
                   jax.ShapeDtypeStruct((1,), jnp.int32)],
        name="routing_plan",
    )(counts, routeT)


def _moe_kernel(te_ref, nt_ref, nx_ref, sl_ref, x_ref, wg_hbm, wu_hbm, wd_hbm, y_ref,
                wg_st, wu_st, wd_st, wgb_ref, wub_ref, wdb_ref, sem, *, tm, tiles_per_step):
    last = nt_ref[0] - 1

    def weight_copies(e, s):
        return (pltpu.make_async_copy(wg_hbm.at[e], wg_st.at[s], sem.at[s, 0]),
                pltpu.make_async_copy(wu_hbm.at[e], wu_st.at[s], sem.at[s, 1]),
                pltpu.make_async_copy(wd_hbm.at[e], wd_st.at[s], sem.at[s, 2]))

    for u in range(tiles_per_step):
        i = pl.program_id(0) * tiles_per_step + u
        rows = slice(u * tm, (u + 1) * tm)
        ic = jnp.minimum(i, last)
        expert = te_ref[ic]
        slot = sl_ref[ic]
        first_of_expert = (i == 0) | ((i <= last) & (expert != te_ref[jnp.maximum(ic - 1, 0)]))

        if u == 0:
            @pl.when(i == 0)
            def _():
                for c in weight_copies(expert, slot):
                    c.start()

        def expert_mlp(rows, new_weights):
            x_l, x_r = _unpack_rows(x_ref[rows, :])
            x_l = x_l.astype(BF16)
            x_r = x_r.astype(BF16)
            half = x_l.shape[1]
            if new_weights:
                wgb_ref[...] = wg_st[slot].astype(BF16)
            hg = _dot(x_l, wgb_ref[0:half, :]) + _dot(x_r, wgb_ref[half:2 * half, :])
            if new_weights:
                wub_ref[...] = wu_st[slot].astype(BF16)
            hu = _dot(x_l, wub_ref[0:half, :]) + _dot(x_r, wub_ref[half:2 * half, :])
            if new_weights:
                wdb_ref[...] = wd_st[slot].astype(BF16)
            act = hg * (1.0 / (1.0 + jnp.exp(-hg))) * hu
            y_ref[rows, :] = _pack_rows(_dot(act.astype(BF16), wdb_ref[...]))

        @pl.when(first_of_expert)
        def _():
            for c in weight_copies(expert, slot):
                c.wait()
            nxt = nx_ref[ic]

            @pl.when(nxt >= 0)
            def _():
                for c in weight_copies(nxt, 1 - slot):
                    c.start()

            expert_mlp(rows, True)

        @pl.when((i <= last) & jnp.logical_not(first_of_expert))
        def _():
            expert_mlp(rows, False)


def _moe(tile_expert, n_tiles, next_expert, stage_slot, xs, wg, wu, wd):
    p, dp = xs.shape
    d, f = wg.shape[1], wg.shape[2]
    assert dp * 2 == d
    tm, tps = TM_MOE, MOE_TILES_PER_STEP
    tb = tm * tps
    row = lambda s, te, nt, nx, sl: (jnp.minimum(s, (nt[0] - 1) // tps), 0)
    hbm = pl.BlockSpec(memory_space=pl.ANY)
    grid_spec = pltpu.PrefetchScalarGridSpec(
        num_scalar_prefetch=4,
        grid=(p // tb,),
        in_specs=[pl.BlockSpec((tb, dp), row), hbm, hbm, hbm],
        out_specs=pl.BlockSpec((tb, dp), row),
        scratch_shapes=[pltpu.VMEM((2, d, f), F32), pltpu.VMEM((2, d, f), F32), pltpu.VMEM((2, f, d), F32),
                        pltpu.VMEM((d, f), BF16), pltpu.VMEM((d, f), BF16), pltpu.VMEM((f, d), BF16),
                        pltpu.SemaphoreType.DMA((2, 3))],
    )
    return pl.pallas_call(
        functools.partial(_moe_kernel, tm=tm, tiles_per_step=tps),
        grid_spec=grid_spec,
        out_shape=jax.ShapeDtypeStruct((p, dp), jnp.int32),
        compiler_params=pltpu.CompilerParams(
            dimension_semantics=("arbitrary",), vmem_limit_bytes=VMEM_LIMIT),
        name="moe_experts",
    )(tile_expert, n_tiles, next_expert, stage_slot, xs, wg, wu, wd)


def _sc_mesh():
    return plsc.VectorSubcoreMesh(core_axis_name="c", subcore_axis_name="s",
                                  num_cores=SC_CORES, num_subcores=SC_SUBCORES)


def _sc_dispatch(rows, pos_a, pos_b, n_out):
    t, d = rows.shape
    win = pos_a.shape[1]

    @functools.partial(pl.kernel, out_type=jax.ShapeDtypeStruct((n_out, d), rows.dtype),
                       mesh=_sc_mesh(), scratch_types=[], name="sc_dispatch")
    def run(rows_hbm, pa_hbm, pb_hbm, out_hbm):
        def body(rows_vmem, pa_vmem, pb_vmem):
            pltpu.sync_copy(rows_vmem, out_hbm.at[pa_vmem.at[0]])
            pltpu.sync_copy(rows_vmem, out_hbm.at[pb_vmem.at[0]])

        pltpu.emit_pipeline(
            body, grid=(t // win,),
            in_specs=[pl.BlockSpec((win, d), lambda i: (i, 0)),
                      pl.BlockSpec((1, win), lambda i: (i, 0)),
                      pl.BlockSpec((1, win), lambda i: (i, 0))],
            out_specs=[],
            core_axis_name=("c", "s"),
            dimension_semantics=(pltpu.PARALLEL,),
        )(rows_hbm, pa_hbm, pb_hbm)

    return run(rows, pos_a, pos_b)


def _sc_gather(table, idx):
    d = table.shape[1]
    n_win, win = idx.shape

    @functools.partial(pl.kernel, out_type=jax.ShapeDtypeStruct((n_win * win, d), table.dtype),
                       mesh=_sc_mesh(), scratch_types=[], name="sc_gather")
    def run(table_hbm, idx_hbm, out_hbm):
        def body(idx_vmem, out_vmem):
            pltpu.sync_copy(table_hbm.at[idx_vmem.at[0]], out_vmem)

        pltpu.emit_pipeline(
            body, grid=(n_win,),
            in_specs=[pl.BlockSpec((1, win), lambda i: (i, 0))],
            out_specs=[pl.BlockSpec((win, d), lambda i: (i, 0))],
            core_axis_name=("c", "s"),
            dimension_semantics=(pltpu.PARALLEL,),
        )(idx_hbm, out_hbm)

    return run(table, idx)


def _combine_kernel(h_ref, ya_ref, yb_ref, rt_ref, o_ref):
    rt = rt_ref[...]
    r = jnp.concatenate([rt, jnp.zeros((LANES - rt.shape[0], rt.shape[1]), F32)], axis=0).T
    wa, wb = r[:, 4:5], r[:, 5:6]
    a_l, a_r = _unpack_rows(ya_ref[...])
    b_l, b_r = _unpack_rows(yb_ref[...])
    half = a_l.shape[1]
    o_ref[:, 0:half] = h_ref[:, 0:half].astype(F32) + wa * a_l + wb * b_l
    o_ref[:, half:2 * half] = h_ref[:, half:2 * half].astype(F32) + wa * a_r + wb * b_r


def _combine(hres, yg, routeT):
    t, d = hres.shape
    tm = TM_COMBINE
    nb = t // tm
    return pl.pallas_call(
        _combine_kernel,
        grid=(nb,),
        in_specs=[pl.BlockSpec((tm, d), lambda i: (i, 0)),
                  pl.BlockSpec((tm, d // 2), lambda i: (i, 0)),
                  pl.BlockSpec((tm, d // 2), lambda i: (i + nb, 0)),
                  pl.BlockSpec((routeT.shape[0], tm), lambda i: (0, i))],
        out_specs=pl.BlockSpec((tm, d), lambda i: (i, 0)),
        out_shape=jax.ShapeDtypeStruct((t, d), F32),
        compiler_params=pltpu.CompilerParams(
            dimension_semantics=("arbitrary",), vmem_limit_bytes=VMEM_LIMIT),
        name="combine",
    )(hres, yg, yg, routeT)


def _lambda_init(layer_idx):
    return 0.8 - 0.6 * math.exp(-0.3 * layer_idx)


def _layer(h, l, attn_norm_g, w_in, conv_w, conv_out_g, q_norm_g, k_norm_g,
           lambda_q1, lambda_k1, lambda_q2, lambda_k2, attn_subln_g, w_out,
           ffn_norm_g, w_router_group, w_router_expert, w_exp_gate, w_exp_up, w_exp_down):
    batch, seq, d = h.shape
    t = batch * seq
    dc = conv_w.shape[-1]
    lam_init = _lambda_init(l)
    x2 = h.reshape(t, d)

    reps = dc // HEAD_DIM
    assert dc // CONV_GROUPS == HEAD_DIM
    qg = (jnp.tile(q_norm_g[l], reps) * (HEAD_DIM ** -0.5 * math.log2(math.e))).reshape(1, dc)
    kg = jnp.tile(k_norm_g[l], reps).reshape(1, dc)
    grp = jnp.arange(MXU_TILE) // HEAD_DIM
    gmat = jnp.where(grp[:, None] == grp[None, :], 1.0 / HEAD_DIM, 0.0).astype(BF16)
    yc, qT, k, vT = _inproj(x2, attn_norm_g[l].reshape(1, d), w_in[l].astype(BF16), conv_w[l],
                          conv_out_g[l].reshape(1, dc), qg, kg, gmat, batch=batch, seq=seq)

    lam_params = jnp.stack([lambda_q1[l], lambda_k1[l], lambda_q2[l], lambda_k2[l]])
    at = _attention(lam_params, attn_subln_g[l].reshape(V_DIM, 1), qT, k, vT,
                    batch=batch, seq=seq, lam_init=lam_init)

    wr = jnp.concatenate([w_router_expert[l], w_router_group[l],
                          jnp.zeros((d, LANES - N_EXPERTS - N_GROUPS), F32)], axis=1)
    wr_hi = wr.astype(BF16)
    wr_lo = (wr - wr_hi.astype(F32)).astype(BF16)
    hres, hn2, routeT, cnt = _outproj_router(x2, yc, at, w_out[l].astype(BF16),
                                                    ffn_norm_g[l].reshape(1, d),
                                                    jnp.concatenate([wr_hi, wr_lo], axis=1))

    tmm = TM_MOE
    n_tiles_max = (2 * t) // tmm + N_EXPERTS
    p_rows = n_tiles_max * tmm
    pos, tile_expert, next_expert, stage_slot, n_tiles = _plan(
        cnt[:, 0].astype(jnp.int32), routeT, n_tiles_max)

    posw = pos.reshape(2 * t // SC_WIN, SC_WIN)
    pos1w = posw[:t // SC_WIN]
    pos2w = posw[t // SC_WIN:]
    xs = _sc_dispatch(hn2, pos1w, pos2w, p_rows)

    f = w_exp_gate.shape[-1]
    ys = _moe(tile_expert, n_tiles, next_expert, stage_slot, xs,
              w_exp_gate[l].reshape(N_EXPERTS, d, f),
              w_exp_up[l].reshape(N_EXPERTS, d, f),
              w_exp_down[l].reshape(N_EXPERTS, f, d))
    yg = _sc_gather(ys, posw)
    out = _combine(hres, yg, routeT)
    return out.reshape(batch, seq, d)


def kernel(x, attn_norm_g, w_in, conv_w, conv_out_g, q_norm_g, k_norm_g, lambda_q1, lambda_k1,
           lambda_q2, lambda_k2, attn_subln_g, w_out, ffn_norm_g, w_router_group, w_router_expert,
           w_exp_gate, w_exp_up, w_exp_down):
    h = x
    for l in range(attn_norm_g.shape[0]):
        h = _layer(h, l, attn_norm_g, w_in, conv_w, conv_out_g, q_norm_g, k_norm_g,
                   lambda_q1, lambda_k1, lambda_q2, lambda_k2, attn_subln_g, w_out,
                   ffn_norm_g, w_router_group, w_router_expert, w_exp_gate, w_exp_up, w_exp_down)
    return h
```

```python
import functools
import math

import jax
import jax.numpy as jnp
from jax import lax
from jax.experimental import pallas as pl
from jax.experimental.pallas import tpu as pltpu
from jax.experimental.pallas import tpu_sc as plsc

F32 = jnp.float32
BF16 = jnp.bfloat16

HEAD_DIM = 64
V_DIM = 2 * HEAD_DIM
CONV_GROUPS = 8
N_GROUPS = 4
EXPERTS_PER_GROUP = 8
N_EXPERTS = N_GROUPS * EXPERTS_PER_GROUP
EPS = 1e-6
LANES = 128
MXU_TILE = 256
ONES_ROWS = 16
VMEM_LIMIT = 48 * 1024 * 1024

TM_PROJ = 256
TM_COMBINE = 1024
OUTPROJ_SUBTILES = 4
INPROJ_SUBTILES = 2
TQ = 512
TK = 512
TM_MOE = 512
MOE_TILES_PER_STEP = 2
SC_CORES = 2
SC_SUBCORES = 16
SC_WIN = 64


def _dot(a, b):
    return jnp.dot(a, b, preferred_element_type=F32)


def _pack_rows(x):
    w = x.shape[1] // 2
    bits = lax.bitcast_convert_type(x.astype(BF16).astype(F32), jnp.uint32)
    return lax.bitcast_convert_type((bits[:, :w] >> 16) | bits[:, w:], jnp.int32)


def _unpack_rows(packed):
    bits = lax.bitcast_convert_type(packed, jnp.uint32)
    left = lax.bitcast_convert_type(bits << 16, F32)
    right = lax.bitcast_convert_type(bits & jnp.uint32(0xFFFF0000), F32)
    return left, right


def _group_mean(sq, gm):
    w = gm.shape[0]
    sq = sq.astype(BF16)
    return jnp.concatenate([_dot(sq[:, c:c + w], gm) for c in range(0, sq.shape[1], w)], axis=1)


def _inproj_kernel(x_ref, g_ref, w_hbm, cw_ref, cg_ref, qg_ref, kg_ref, gm_ref,
                   yc_ref, qT_ref, k_ref, vT_ref, carry_ref, wst_ref, w_ref, sem, *, tm, dc):
    j = pl.program_id(1)

    def weight_copy(s):
        return pltpu.make_async_copy(w_hbm.at[:, s * dc:(s + 1) * dc], wst_ref.at[s], sem.at[s])

    @pl.when((pl.program_id(0) == 0) & (j == 0))
    def _():
        for s in range(wst_ref.shape[0]):
            weight_copy(s).start()
        for s in range(wst_ref.shape[0]):
            weight_copy(s).wait()
            w_ref[:, s * dc:(s + 1) * dc] = wst_ref[s].astype(BF16)

    @pl.when(j == 0)
    def _():
        carry_ref[...] = jnp.zeros_like(carry_ref)

    for u in range(x_ref.shape[0] // tm):
        _inproj_subtile(u, x_ref, g_ref, w_ref, cw_ref, cg_ref, qg_ref, kg_ref, gm_ref,
                        yc_ref, qT_ref, k_ref, vT_ref, carry_ref, tm=tm, dc=dc)


def _inproj_subtile(u, x_ref, g_ref, w_ref, cw_ref, cg_ref, qg_ref, kg_ref, gm_ref,
                    yc_ref, qT_ref, k_ref, vT_ref, carry_ref, *, tm, dc):
    rows = slice(u * tm, (u + 1) * tm)
    x = x_ref[rows, :]
    ms = jnp.mean(x * x, axis=-1, keepdims=True)
    hn = (x * lax.rsqrt(ms + EPS) * g_ref[...]).astype(BF16)

    def proj(s):
        return _dot(hn, w_ref[:, s * dc:(s + 1) * dc])

    gm = gm_ref[...]

    c = proj(2) * proj(0)
    prev = carry_ref[...]
    r = lax.broadcasted_iota(jnp.int32, c.shape, 0)
    c1 = jnp.where(r == 0, prev[7:8, :], pltpu.roll(c, 1, 0))
    c2 = jnp.where(r == 0, prev[6:7, :], jnp.where(r == 1, prev[7:8, :], pltpu.roll(c, 2, 0)))
    carry_ref[...] = c[tm - 8:tm, :]
    cw = cw_ref[...]
    y = proj(1) * (cw[0:1, :] * c2 + cw[1:2, :] * c1 + cw[2:3, :] * c)
    yc_ref[rows, :] = (y * lax.rsqrt(_group_mean(y * y, gm) + EPS) * cg_ref[...]).astype(BF16)

    q = proj(3)
    qT_ref[:, rows] = (q * lax.rsqrt(_group_mean(q * q, gm) + EPS) * qg_ref[...]).T.astype(BF16)
    k = proj(4)
    k_ref[rows, :] = (k * lax.rsqrt(_group_mean(k * k, gm) + EPS) * kg_ref[...]).astype(BF16)
    vt = proj(5).T.astype(BF16)
    vrows = V_DIM + ONES_ROWS
    for h in range(dc // V_DIM):
        vT_ref[u, h * vrows:h * vrows + V_DIM, :] = vt[h * V_DIM:(h + 1) * V_DIM, :]
        vT_ref[u, h * vrows + V_DIM:(h + 1) * vrows, :] = jnp.ones((ONES_ROWS, tm), BF16)


def _inproj(x2, g, w_in, conv_w, conv_g, qg, kg, gmat, *, batch, seq):
    t, d = x2.shape
    dc = conv_g.shape[1]
    n_proj = w_in.shape[1] // dc
    assert w_in.shape == (d, n_proj * dc) and w_in.dtype == F32
    dv = dc // V_DIM * (V_DIM + ONES_ROWS)
    tm = TK
    tb = tm * INPROJ_SUBTILES
    nj = seq // tb
    row = lambda b, j: (b * nj + j, 0)
    const = lambda b, j: (0, 0)
    out_sds = jax.ShapeDtypeStruct((t, dc), BF16)
    return pl.pallas_call(
        functools.partial(_inproj_kernel, tm=tm, dc=dc),
        grid=(batch, nj),
        in_specs=[
            pl.BlockSpec((tb, d), row),
            pl.BlockSpec((1, d), const),
            pl.BlockSpec(memory_space=pl.ANY),
            pl.BlockSpec(conv_w.shape, const),
            pl.BlockSpec((1, dc), const),
            pl.BlockSpec((1, dc), const),
            pl.BlockSpec((1, dc), const),
            pl.BlockSpec(gmat.shape, const),
        ],
        out_specs=[
            pl.BlockSpec((tb, dc), row),
            pl.BlockSpec((dc, tb), lambda b, j: (0, b * nj + j)),
            pl.BlockSpec((tb, dc), row),
            pl.BlockSpec((INPROJ_SUBTILES, dv, tm), lambda b, j: (b * nj + j, 0, 0)),
        ],
        out_shape=[out_sds, jax.ShapeDtypeStruct((dc, t), BF16), out_sds,
                   jax.ShapeDtypeStruct((t // tm, dv, tm), BF16)],
        scratch_shapes=[pltpu.VMEM((8, dc), F32),
                        pltpu.VMEM((n_proj, d, dc), F32), pltpu.VMEM((d, n_proj * dc), BF16),
                        pltpu.SemaphoreType.DMA((n_proj,))],
        compiler_params=pltpu.CompilerParams(
            dimension_semantics=("arbitrary", "arbitrary"), vmem_limit_bytes=VMEM_LIMIT),
        name="inproj_conv_qknorm",
    )(x2, g, w_in, conv_w, conv_g, qg, kg, gmat)


def _attn_kernel(lp_ref, sg_ref, qT_ref, k_ref, vT_ref, o_ref, m_ref, acc_ref,
                 sa_ref, pb_ref, ab_ref, *, tq, tk, nq, lam_init):
    map_a = slice(0, tq)
    map_b = slice(tq, 2 * tq)

    lp = lp_ref[...]
    lam = (jnp.exp(jnp.sum(lp[0:1, :] * lp[1:2, :], axis=-1, keepdims=True))
           - jnp.exp(jnp.sum(lp[2:3, :] * lp[3:4, :], axis=-1, keepdims=True)) + lam_init)

    def stacked_queries(qi):
        qT = qT_ref[:, qi * tq:(qi + 1) * tq]
        row = lax.broadcasted_iota(jnp.int32, qT.shape, 0)
        zero = jnp.zeros_like(qT)
        return jnp.concatenate([jnp.where(row < HEAD_DIM, qT, zero),
                                jnp.where(row >= HEAD_DIM, qT, zero)], axis=1)

    def pieces(diag):
        return ((slice(0, tq // 2), tk // 2), (slice(tq // 2, tq), tk)) if diag else ((slice(0, tq), tk),)

    def shifted(cols, off):
        return slice(cols.start + off, cols.stop + off)

    def scores(j, qq, cols, n_keys):
        return _dot(k_ref[j * tk:j * tk + n_keys, :], qq[:, cols])

    def softmax(m, cols, s, q0):
        if q0 is not None:
            qrel = q0 + lax.broadcasted_iota(jnp.int32, s.shape, 1)
            krel = lax.broadcasted_iota(jnp.int32, s.shape, 0)
            s = jnp.where(krel <= qrel, s, -jnp.inf)
        sb = s.astype(BF16)
        m_old = m[:, cols]
        m_new = jnp.maximum(m_old, jnp.max(sb, axis=0, keepdims=True).astype(F32))
        alpha = jnp.exp2(m_old - m_new)
        p = jnp.exp2(sb - m_new.astype(BF16))
        m[:, cols] = m_new
        return alpha, p

    def accumulate(acc, cols, alpha, vb, p):
        acc[:, cols] = alpha * acc[:, cols] + _dot(vb, p)

    def prefetch_map_a(sa, j, qq, diag):
        for cols, n_keys in pieces(diag):
            sa[0:n_keys, cols] = scores(j, qq, cols, n_keys)

    stacked = {0: stacked_queries(0)}
    m_ref[0] = jnp.full(m_ref.shape[1:], -jnp.inf, F32)
    acc_ref[0] = jnp.zeros(acc_ref.shape[1:], F32)
    prefetch_map_a(sa_ref.at[0], 0, stacked[0], True)

    for qi in range(nq):
        par = qi % 2
        m, acc, sa, qq = m_ref.at[par], acc_ref.at[par], sa_ref.at[par], stacked[qi]
        for j in range(qi + 1):
            last = j == qi
            if j > 0:
                accumulate(acc, map_b, ab_ref[...], vT_ref[j - 1], pb_ref[...])
            s_b = [scores(j, qq, shifted(cols, tq), n_keys) for cols, n_keys in pieces(last)]
            for cols, n_keys in pieces(last):
                alpha_a, p_a = softmax(m, cols, sa[0:n_keys, cols], cols.start if last else None)
                accumulate(acc, cols, alpha_a, vT_ref[j, :, 0:n_keys], p_a)
            if not last:
                prefetch_map_a(sa, j + 1, qq, j + 1 == qi)
            elif qi + 1 < nq:
                stacked[qi + 1] = stacked_queries(qi + 1)
                m_ref[1 - par] = jnp.full(m_ref.shape[1:], -jnp.inf, F32)
                acc_ref[1 - par] = jnp.zeros(acc_ref.shape[1:], F32)
                prefetch_map_a(sa_ref.at[1 - par], 0, stacked[qi + 1], False)
            for (cols, n_keys), s in zip(pieces(last), s_b):
                alpha_b, p_b = softmax(m, shifted(cols, tq), s, cols.start if last else None)
                if last:
                    accumulate(acc, shifted(cols, tq), alpha_b, vT_ref[j, :, 0:n_keys], p_b)
                else:
                    ab_ref[...] = alpha_b
                    pb_ref[...] = p_b

        o = acc[0:V_DIM, :] / acc[V_DIM:V_DIM + 1, :]
        d = o[:, map_a] - lam * o[:, map_b]
        ms = jnp.mean(d * d, axis=0, keepdims=True)
        o_ref[qi * tq:(qi + 1) * tq, :] = (
            d * lax.rsqrt(ms + EPS) * sg_ref[...] * (1.0 - lam_init)).T.astype(BF16)
        del stacked[qi]


def _attention(lam_params, subln_g, qT, k, vT, *, batch, seq, lam_init):
    dq, t = qT.shape
    n_heads = dq // V_DIM
    tq, tk = TQ, TK
    assert tq == tk and vT.shape[2] == tk
    nq = seq // tq
    nk = seq // tk
    vrows = V_DIM + ONES_ROWS
    const = lambda b, h: (0, 0)
    return pl.pallas_call(
        functools.partial(_attn_kernel, tq=tq, tk=tk, nq=nq, lam_init=lam_init),
        grid=(batch, n_heads),
        in_specs=[
            pl.BlockSpec(lam_params.shape, const),
            pl.BlockSpec((V_DIM, 1), const),
            pl.BlockSpec((V_DIM, seq), lambda b, h: (h, b)),
            pl.BlockSpec((seq, V_DIM), lambda b, h: (b, h)),
            pl.BlockSpec((nk, vrows, tk), lambda b, h: (b, h, 0)),
        ],
        out_specs=pl.BlockSpec((seq, V_DIM), lambda b, h: (b, h)),
        out_shape=jax.ShapeDtypeStruct((t, dq), BF16),
        scratch_shapes=[pltpu.VMEM((2, 1, 2 * tq), F32),
                        pltpu.VMEM((2, vrows, 2 * tq), F32),
                        pltpu.VMEM((2, tk, tq), F32), pltpu.VMEM((tk, tq), BF16), pltpu.VMEM((1, tq), F32)],
        compiler_params=pltpu.CompilerParams(
            dimension_semantics=("arbitrary", "arbitrary"), vmem_limit_bytes=VMEM_LIMIT),
        name="diff_attention",
    )(lam_params, subln_g, qT, k, vT)


def _outproj_router_kernel(x_ref, yc_ref, at_ref, wo_ref, g_ref, wr_ref,
                           h_ref, hn_ref, routeT_ref, cnt_ref, *, tm, dc):
    i = pl.program_id(0)

    @pl.when(i == 0)
    def _():
        cnt_ref[...] = jnp.zeros_like(cnt_ref)

    for r0 in range(0, x_ref.shape[0], tm):
        _outproj_router_subtile(slice(r0, r0 + tm), x_ref, yc_ref, at_ref, wo_ref, g_ref, wr_ref,
                                h_ref, hn_ref, routeT_ref, cnt_ref, tm=tm, dc=dc)


def _outproj_router_subtile(rows, x_ref, yc_ref, at_ref, wo_ref, g_ref, wr_ref,
                            h_ref, hn_ref, routeT_ref, cnt_ref, *, tm, dc):
    h = x_ref[rows, :] + _dot(yc_ref[rows, :], wo_ref[0:dc, :]) + _dot(at_ref[rows, :], wo_ref[dc:2 * dc, :])
    h_ref[rows, :] = h.astype(BF16)
    ms = jnp.mean(h * h, axis=-1, keepdims=True)
    hn = h * lax.rsqrt(ms + EPS) * g_ref[...]
    hi = hn.astype(BF16)
    hn_ref[rows, :] = _pack_rows(hn)
    lo = (hn - hi.astype(F32)).astype(BF16)
    prod = _dot(hi, wr_ref[...])
    logits = prod[:, 0:LANES] + prod[:, LANES:2 * LANES] + _dot(lo, wr_ref[:, 0:LANES])

    lt = logits.T
    neg = -jnp.inf
    grow = lax.broadcasted_iota(jnp.int32, (8, tm), 0).astype(F32)
    gl = jnp.where(grow < N_GROUPS, lt[N_EXPERTS:N_EXPERTS + 8, :], neg)
    gmax = jnp.max(gl, axis=0, keepdims=True)
    g_gate = 1.0 / jnp.sum(jnp.exp(gl - gmax), axis=0, keepdims=True)
    g_idx = jnp.min(jnp.where(gl == gmax, grow, 1e9), axis=0, keepdims=True)
    erow = lax.broadcasted_iota(jnp.int32, (N_EXPERTS, tm), 0).astype(F32)
    e_lo = g_idx * EXPERTS_PER_GROUP
    el = jnp.where((erow >= e_lo) & (erow < e_lo + EXPERTS_PER_GROUP), lt[0:N_EXPERTS, :], neg)
    v1 = jnp.max(el, axis=0, keepdims=True)
    i1 = jnp.min(jnp.where(el == v1, erow, 1e9), axis=0, keepdims=True)
    el2 = jnp.where(erow == i1, neg, el)
    v2 = jnp.max(el2, axis=0, keepdims=True)
    i2 = jnp.min(jnp.where(el2 == v2, erow, 1e9), axis=0, keepdims=True)
    tt = jnp.exp(v2 - v1)
    w1 = g_gate / (1.0 + tt)
    w2 = g_gate * tt / (1.0 + tt)

    sel1 = erow == i1
    sel2 = erow == i2
    oh = jnp.where(sel1 | sel2, 1.0, 0.0)
    ss = lax.broadcasted_iota(jnp.int32, (tm, tm), 0)
    tt_i = lax.broadcasted_iota(jnp.int32, (tm, tm), 1)
    earlier = jnp.where(ss < tt_i, 1.0, 0.0).astype(BF16)
    ranks = _dot(oh.astype(BF16), earlier) + cnt_ref[...]
    r1 = jnp.sum(jnp.where(sel1, ranks, 0.0), axis=0, keepdims=True)
    r2 = jnp.sum(jnp.where(sel2, ranks, 0.0), axis=0, keepdims=True)
    cnt_ref[...] = cnt_ref[...] + jnp.sum(oh, axis=1, keepdims=True)

    routeT = jnp.concatenate([i1, i2, r1, r2, w1, w2, jnp.zeros((2, tm), F32)], axis=0)
    routeT_ref[:, rows] = routeT


def _outproj_router(x2, yc, at, w_out, g, wr_cat):
    t, d = x2.shape
    dc = yc.shape[1]
    tm = TM_PROJ
    tb = tm * OUTPROJ_SUBTILES
    row = lambda i: (i, 0)
    const = lambda i: (0, 0)
    return pl.pallas_call(
        functools.partial(_outproj_router_kernel, tm=tm, dc=dc),
        grid=(t // tb,),
        in_specs=[
            pl.BlockSpec((tb, d), row),
            pl.BlockSpec((tb, dc), row),
            pl.BlockSpec((tb, dc), row),
            pl.BlockSpec(w_out.shape, const),
            pl.BlockSpec((1, d), const),
            pl.BlockSpec(wr_cat.shape, const),
        ],
        out_specs=[
            pl.BlockSpec((tb, d), row),
            pl.BlockSpec((tb, d // 2), row),
            pl.BlockSpec((8, tb), lambda i: (0, i)),
            pl.BlockSpec((N_EXPERTS, 1), const),
        ],
        out_shape=[
            jax.ShapeDtypeStruct((t, d), BF16),
            jax.ShapeDtypeStruct((t, d // 2), jnp.int32),
            jax.ShapeDtypeStruct((8, t), F32),
            jax.ShapeDtypeStruct((N_EXPERTS, 1), F32),
        ],
        compiler_params=pltpu.CompilerParams(
            dimension_semantics=("arbitrary",), vmem_limit_bytes=VMEM_LIMIT),
        name="outproj_router",
    )(x2, yc, at, w_out, g, wr_cat)


def _plan_kernel(cnt_ref, rt_ref, pos_ref, te_ref, nx_ref, sl_ref, nt_ref, *, tm):
    shift = tm.bit_length() - 1
    tiles, starts = [], []
    total = jnp.int32(0)
    for e in range(N_EXPERTS):
        n = lax.shift_right_logical(cnt_ref[e] + (tm - 1), shift)
        tiles.append(n)
        starts.append(total)
        total = total + n
    nt_ref[0] = total

    rt = rt_ref[...]
    ea, eb = rt[0:1, :], rt[1:2, :]
    sa = jnp.zeros_like(ea)
    sb = jnp.zeros_like(eb)
    for e in range(N_EXPERTS):
        start = (starts[e] * tm).astype(F32)
        sa = jnp.where(ea == e, start, sa)
        sb = jnp.where(eb == e, start, sb)
    pos_ref[0:1, :] = (sa + rt[2:3, :]).astype(jnp.int32)
    pos_ref[1:2, :] = (sb + rt[3:4, :]).astype(jnp.int32)

    def clear(i, c):
        te_ref[i] = 0
        nx_ref[i] = -1
        sl_ref[i] = 0
        return c

    lax.fori_loop(0, te_ref.shape[0], clear, 0)

    nxt = jnp.int32(-1)
    next_of = [None] * N_EXPERTS
    for e in reversed(range(N_EXPERTS)):
        next_of[e] = nxt
        nxt = jnp.where(tiles[e] > 0, e, nxt)
    ordinal = jnp.int32(0)
    for e in range(N_EXPERTS):
        slot = ordinal & 1

        def fill(j, c, e=e, slot=slot):
            te_ref[starts[e] + j] = e
            nx_ref[starts[e] + j] = next_of[e]
            sl_ref[starts[e] + j] = slot
            return c

        lax.fori_loop(0, tiles[e], fill, 0)
        ordinal = ordinal + (tiles[e] > 0).astype(jnp.int32)


def _plan(counts, routeT, n_tiles_max):
    t = routeT.shape[1]
    smem = pl.BlockSpec(memory_space=pltpu.SMEM)
    table = jax.ShapeDtypeStruct((n_tiles_max,), jnp.int32)
    return pl.pallas_call(
        functools.partial(_plan_kernel, tm=TM_MOE),
        grid_spec=pltpu.PrefetchScalarGridSpec(
            num_scalar_prefetch=1, grid=(1,),
            in_specs=[pl.BlockSpec(routeT.shape, lambda i, cnt: (0, 0))],
            out_specs=[pl.BlockSpec((2, t), lambda i, cnt: (0, 0)), smem, smem, smem, smem],
        ),
        out_shape=[jax.ShapeDtypeStruct((2, t), jnp.int32), table, table, table,
                   jax.ShapeDtypeStruct((1,), jnp.int32)],
        name="routing_plan",
    )(counts, routeT)


def _moe_kernel(te_ref, nt_ref, nx_ref, sl_ref, x_ref, wg_hbm, wu_hbm, wd_hbm, y_ref,
                wg_st, wu_st, wd_st, wgb_ref, wub_ref, wdb_ref, sem, *, tm, tiles_per_step):
    last = nt_ref[0] - 1

    def weight_copies(e, s):
        return (pltpu.make_async_copy(wg_hbm.at[e], wg_st.at[s], sem.at[s, 0]),
                pltpu.make_async_copy(wu_hbm.at[e], wu_st.at[s], sem.at[s, 1]),
                pltpu.make_async_copy(wd_hbm.at[e], wd_st.at[s], sem.at[s, 2]))

    for u in range(tiles_per_step):
        i = pl.program_id(0) * tiles_per_step + u
        rows = slice(u * tm, (u + 1) * tm)
        ic = jnp.minimum(i, last)
        expert = te_ref[ic]
        slot = sl_ref[ic]
        first_of_expert = (i == 0) | ((i <= last) & (expert != te_ref[jnp.maximum(ic - 1, 0)]))

        if u == 0:
            @pl.when(i == 0)
            def _():
                for c in weight_copies(expert, slot):
                    c.start()

        def expert_mlp(rows, new_weights):
            x_l, x_r = _unpack_rows(x_ref[rows, :])
            x_l = x_l.astype(BF16)
            x_r = x_r.astype(BF16)
            half = x_l.shape[1]
            if new_weights:
                wgb_ref[...] = wg_st[slot].astype(BF16)
            hg = _dot(x_l, wgb_ref[0:half, :]) + _dot(x_r, wgb_ref[half:2 * half, :])
            if new_weights:
                wub_ref[...] = wu_st[slot].astype(BF16)
            hu = _dot(x_l, wub_ref[0:half, :]) + _dot(x_r, wub_ref[half:2 * half, :])
            if new_weights:
                wdb_ref[...] = wd_st[slot].astype(BF16)
            act = hg * (1.0 / (1.0 + jnp.exp(-hg))) * hu
            y_ref[rows, :] = _pack_rows(_dot(act.astype(BF16), wdb_ref[...]))

        @pl.when(first_of_expert)
        def _():
            for c in weight_copies(expert, slot):
                c.wait()
            nxt = nx_ref[ic]

            @pl.when(nxt >= 0)
            def _():
                for c in weight_copies(nxt, 1 - slot):
                    c.start()

            expert_mlp(rows, True)

        @pl.when((i <= last) & jnp.logical_not(first_of_expert))
        def _():
            expert_mlp(rows, False)


def _moe(tile_expert, n_tiles, next_expert, stage_slot, xs, wg, wu, wd):
    p, dp = xs.shape
    d, f = wg.shape[1], wg.shape[2]
    assert dp * 2 == d
    tm, tps = TM_MOE, MOE_TILES_PER_STEP
    tb = tm * tps
    row = lambda s, te, nt, nx, sl: (jnp.minimum(s, (nt[0] - 1) // tps), 0)
    hbm = pl.BlockSpec(memory_space=pl.ANY)
    grid_spec = pltpu.PrefetchScalarGridSpec(
        num_scalar_prefetch=4,
        grid=(p // tb,),
        in_specs=[pl.BlockSpec((tb, dp), row), hbm, hbm, hbm],
        out_specs=pl.BlockSpec((tb, dp), row),
        scratch_shapes=[pltpu.VMEM((2, d, f), F32), pltpu.VMEM((2, d, f), F32), pltpu.VMEM((2, f, d), F32),
                        pltpu.VMEM((d, f), BF16), pltpu.VMEM((d, f), BF16), pltpu.VMEM((f, d), BF16),
                        pltpu.SemaphoreType.DMA((2, 3))],
    )
    return pl.pallas_call(
        functools.partial(_moe_kernel, tm=tm, tiles_per_step=tps),
        grid_spec=grid_spec,
        out_shape=jax.ShapeDtypeStruct((p, dp), jnp.int32),
        compiler_params=pltpu.CompilerParams(
            dimension_semantics=("arbitrary",), vmem_limit_bytes=VMEM_LIMIT),
        name="moe_experts",
    )(tile_expert, n_tiles, next_expert, stage_slot, xs, wg, wu, wd)


def _sc_mesh():
    return plsc.VectorSubcoreMesh(core_axis_name="c", subcore_axis_name="s",
                                  num_cores=SC_CORES, num_subcores=SC_SUBCORES)


def _sc_dispatch(rows, pos_a, pos_b, n_out):
    t, d = rows.shape
    win = pos_a.shape[1]

    @functools.partial(pl.kernel, out_type=jax.ShapeDtypeStruct((n_out, d), rows.dtype),
                       mesh=_sc_mesh(), scratch_types=[], name="sc_dispatch")
    def run(rows_hbm, pa_hbm, pb_hbm, out_hbm):
        def body(rows_vmem, pa_vmem, pb_vmem):
            pltpu.sync_copy(rows_vmem, out_hbm.at[pa_vmem.at[0]])
            pltpu.sync_copy(rows_vmem, out_hbm.at[pb_vmem.at[0]])

        pltpu.emit_pipeline(
            body, grid=(t // win,),
            in_specs=[pl.BlockSpec((win, d), lambda i: (i, 0)),
                      pl.BlockSpec((1, win), lambda i: (i, 0)),
                      pl.BlockSpec((1, win), lambda i: (i, 0))],
            out_specs=[],
            core_axis_name=("c", "s"),
            dimension_semantics=(pltpu.PARALLEL,),
        )(rows_hbm, pa_hbm, pb_hbm)

    return run(rows, pos_a, pos_b)


def _sc_gather(table, idx):
    d = table.shape[1]
    n_win, win = idx.shape

    @functools.partial(pl.kernel, out_type=jax.ShapeDtypeStruct((n_win * win, d), table.dtype),
                       mesh=_sc_mesh(), scratch_types=[], name="sc_gather")
    def run(table_hbm, idx_hbm, out_hbm):
        def body(idx_vmem, out_vmem):
            pltpu.sync_copy(table_hbm.at[idx_vmem.at[0]], out_vmem)

        pltpu.emit_pipeline(
            body, grid=(n_win,),
            in_specs=[pl.BlockSpec((1, win), lambda i: (i, 0))],
            out_specs=[pl.BlockSpec((win, d), lambda i: (i, 0))],
            core_axis_name=("c", "s"),
            dimension_semantics=(pltpu.PARALLEL,),
        )(idx_hbm, out_hbm)

    return run(table, idx)


def _combine_kernel(h_ref, ya_ref, yb_ref, rt_ref, o_ref):
    rt = rt_ref[...]
    r = jnp.concatenate([rt, jnp.zeros((LANES - rt.shape[0], rt.shape[1]), F32)], axis=0).T
    wa, wb = r[:, 4:5], r[:, 5:6]
    a_l, a_r = _unpack_rows(ya_ref[...])
    b_l, b_r = _unpack_rows(yb_ref[...])
    half = a_l.shape[1]
    o_ref[:, 0:half] = h_ref[:, 0:half].astype(F32) + wa * a_l + wb * b_l
    o_ref[:, half:2 * half] = h_ref[:, half:2 * half].astype(F32) + wa * a_r + wb * b_r


def _combine(hres, yg, routeT):
    t, d = hres.shape
    tm = TM_COMBINE
    nb = t // tm
    return pl.pallas_call(
        _combine_kernel,
        grid=(nb,),
        in_specs=[pl.BlockSpec((tm, d), lambda i: (i, 0)),
                  pl.BlockSpec((tm, d // 2), lambda i: (i, 0)),
                  pl.BlockSpec((tm, d // 2), lambda i: (i + nb, 0)),
                  pl.BlockSpec((routeT.shape[0], tm), lambda i: (0, i))],
        out_specs=pl.BlockSpec((tm, d), lambda i: (i, 0)),
        out_shape=jax.ShapeDtypeStruct((t, d), F32),
        compiler_params=pltpu.CompilerParams(
            dimension_semantics=("arbitrary",), vmem_limit_bytes=VMEM_LIMIT),
        name="combine",
    )(hres, yg, yg, routeT)


def _lambda_init(layer_idx):
    return 0.8 - 0.6 * math.exp(-0.3 * layer_idx)


def _layer(h, l, attn_norm_g, w_in, conv_w, conv_out_g, q_norm_g, k_norm_g,
           lambda_q1, lambda_k1, lambda_q2, lambda_k2, attn_subln_g, w_out,
           ffn_norm_g, w_router_group, w_router_expert, w_exp_gate, w_exp_up, w_exp_down):
    batch, seq, d = h.shape
    t = batch * seq
    dc = conv_w.shape[-1]
    lam_init = _lambda_init(l)
    x2 = h.reshape(t, d)

    reps = dc // HEAD_DIM
    assert dc // CONV_GROUPS == HEAD_DIM
    qg = (jnp.tile(q_norm_g[l], reps) * (HEAD_DIM ** -0.5 * math.log2(math.e))).reshape(1, dc)
    kg = jnp.tile(k_norm_g[l], reps).reshape(1, dc)
    grp = jnp.arange(MXU_TILE) // HEAD_DIM
    gmat = jnp.where(grp[:, None] == grp[None, :], 1.0 / HEAD_DIM, 0.0).astype(BF16)
    yc, qT, k, vT = _inproj(x2, attn_norm_g[l].reshape(1, d), w_in[l], conv_w[l],
                          conv_out_g[l].reshape(1, dc), qg, kg, gmat, batch=batch, seq=seq)

    lam_params = jnp.stack([lambda_q1[l], lambda_k1[l], lambda_q2[l], lambda_k2[l]])
    at = _attention(lam_params, attn_subln_g[l].reshape(V_DIM, 1), qT, k, vT,
                    batch=batch, seq=seq, lam_init=lam_init)

    wr = jnp.concatenate([w_router_expert[l], w_router_group[l],
                          jnp.zeros((d, LANES - N_EXPERTS - N_GROUPS), F32)], axis=1)
    wr_hi = wr.astype(BF16)
    wr_lo = (wr - wr_hi.astype(F32)).astype(BF16)
    hres, hn2, routeT, cnt = _outproj_router(x2, yc, at, w_out[l].astype(BF16),
                                                    ffn_norm_g[l].reshape(1, d),
                                                    jnp.concatenate([wr_hi, wr_lo], axis=1))

    tmm = TM_MOE
    n_tiles_max = (2 * t) // tmm + N_EXPERTS
    p_rows = n_tiles_max * tmm
    pos, tile_expert, next_expert, stage_slot, n_tiles = _plan(
        cnt[:, 0].astype(jnp.int32), routeT, n_tiles_max)

    posw = pos.reshape(2 * t // SC_WIN, SC_WIN)
    pos1w = posw[:t // SC_WIN]
    pos2w = posw[t // SC_WIN:]
    xs = _sc_dispatch(hn2, pos1w, pos2w, p_rows)

    f = w_exp_gate.shape[-1]
    ys = _moe(tile_expert, n_tiles, next_expert, stage_slot, xs,
              w_exp_gate[l].reshape(N_EXPERTS, d, f),
              w_exp_up[l].reshape(N_EXPERTS, d, f),
              w_exp_down[l].reshape(N_EXPERTS, f, d))
    yg = _sc_gather(ys, posw)
    out = _combine(hres, yg, routeT)
    return out.reshape(batch, seq, d)


def kernel(x, attn_norm_g, w_in, conv_w, conv_out_g, q_norm_g, k_norm_g, lambda_q1, lambda_k1,
           lambda_q2, lambda_k2, attn_subln_g, w_out, ffn_norm_g, w_router_group, w_router_expert,
           w_exp_gate, w_exp_up, w_exp_down):
    h = x
    for l in range(attn_norm_g.shape[0]):
        h = _layer(h, l, attn_norm_g, w_in, conv_w, conv_out_g, q_norm_g, k_norm_g,
                   lambda_q1, lambda_k1, lambda_q2, lambda_k2, attn_subln_g, w_out,
                   ffn_norm_g, w_router_group, w_router_expert, w_exp_gate, w_exp_up, w_exp_down)
    return h
```

```python
import functools
import math

import jax
import jax.numpy as jnp
from jax import lax
from jax.experimental import pallas as pl
from jax.experimental.pallas import tpu as pltpu
from jax.experimental.pallas import tpu_sc as plsc

F32 = jnp.float32
BF16 = jnp.bfloat16

HEAD_DIM = 64
V_DIM = 2 * HEAD_DIM
CONV_GROUPS = 8
N_GROUPS = 4
EXPERTS_PER_GROUP = 8
N_EXPERTS = N_GROUPS * EXPERTS_PER_GROUP
EPS = 1e-6
LANES = 128
MXU_TILE = 256
ONES_ROWS = 16
VMEM_LIMIT = 48 * 1024 * 1024

TM_PROJ = 256
TM_COMBINE = 1024
OUTPROJ_SUBTILES = 4
INPROJ_SUBTILES = 2
TQ = 512
TK = 512
TM_MOE = 512
MOE_TILES_PER_STEP = 2
SC_CORES = 2
SC_SUBCORES = 16
SC_WIN = 64


def _dot(a, b):
    return jnp.dot(a, b, preferred_element_type=F32)


def _pack_rows(x):
    w = x.shape[1] // 2
    bits = lax.bitcast_convert_type(x.astype(BF16).astype(F32), jnp.uint32)
    return lax.bitcast_convert_type((bits[:, :w] >> 16) | bits[:, w:], jnp.int32)


def _unpack_rows(packed):
    bits = lax.bitcast_convert_type(packed, jnp.uint32)
    left = lax.bitcast_convert_type(bits << 16, F32)
    right = lax.bitcast_convert_type(bits & jnp.uint32(0xFFFF0000), F32)
    return left, right


def _group_mean(sq, gm):
    w = gm.shape[0]
    sq = sq.astype(BF16)
    return jnp.concatenate([_dot(sq[:, c:c + w], gm) for c in range(0, sq.shape[1], w)], axis=1)


def _inproj_kernel(x_ref, g_ref, w_hbm, cw_ref, cg_ref, qg_ref, kg_ref, gm_ref,
                   yc_ref, qT_ref, k_ref, vT_ref, carry_ref, wst_ref, w_ref, sem, *, tm, dc):
    j = pl.program_id(1)

    def weight_copy(s):
        return pltpu.make_async_copy(w_hbm.at[:, s * dc:(s + 1) * dc], wst_ref.at[s], sem.at[s])

    @pl.when((pl.program_id(0) == 0) & (j == 0))
    def _():
        for s in range(wst_ref.shape[0]):
            weight_copy(s).start()
        for s in range(wst_ref.shape[0]):
            weight_copy(s).wait()
            w_ref[:, s * dc:(s + 1) * dc] = wst_ref[s].astype(BF16)

    @pl.when(j == 0)
    def _():
        carry_ref[...] = jnp.zeros_like(carry_ref)

    for u in range(x_ref.shape[0] // tm):
        _inproj_subtile(u, x_ref, g_ref, w_ref, cw_ref, cg_ref, qg_ref, kg_ref, gm_ref,
                        yc_ref, qT_ref, k_ref, vT_ref, carry_ref, tm=tm, dc=dc)


def _inproj_subtile(u, x_ref, g_ref, w_ref, cw_ref, cg_ref, qg_ref, kg_ref, gm_ref,
                    yc_ref, qT_ref, k_ref, vT_ref, carry_ref, *, tm, dc):
    rows = slice(u * tm, (u + 1) * tm)
    x = x_ref[rows, :]
    ms = jnp.mean(x * x, axis=-1, keepdims=True)
    hn = (x * lax.rsqrt(ms + EPS) * g_ref[...]).astype(BF16)

    def proj(s):
        return _dot(hn, w_ref[:, s * dc:(s + 1) * dc])

    gm = gm_ref[...]

    c = proj(2) * proj(0)
    prev = carry_ref[...]
    r = lax.broadcasted_iota(jnp.int32, c.shape, 0)
    c1 = jnp.where(r == 0, prev[7:8, :], pltpu.roll(c, 1, 0))
    c2 = jnp.where(r == 0, prev[6:7, :], jnp.where(r == 1, prev[7:8, :], pltpu.roll(c, 2, 0)))
    carry_ref[...] = c[tm - 8:tm, :]
    cw = cw_ref[...]
    y = proj(1) * (cw[0:1, :] * c2 + cw[1:2, :] * c1 + cw[2:3, :] * c)
    yc_ref[rows, :] = (y * lax.rsqrt(_group_mean(y * y, gm) + EPS) * cg_ref[...]).astype(BF16)

    q = proj(3)
    qT_ref[:, rows] = (q * lax.rsqrt(_group_mean(q * q, gm) + EPS) * qg_ref[...]).T.astype(BF16)
    k = proj(4)
    k_ref[rows, :] = (k * lax.rsqrt(_group_mean(k * k, gm) + EPS) * kg_ref[...]).astype(BF16)
    vt = proj(5).T.astype(BF16)
    vrows = V_DIM + ONES_ROWS
    for h in range(dc // V_DIM):
        vT_ref[u, h * vrows:h * vrows + V_DIM, :] = vt[h * V_DIM:(h + 1) * V_DIM, :]
        vT_ref[u, h * vrows + V_DIM:(h + 1) * vrows, :] = jnp.ones((ONES_ROWS, tm), BF16)


def _inproj(x2, g, w_in, conv_w, conv_g, qg, kg, gmat, *, batch, seq):
    t, d = x2.shape
    dc = conv_g.shape[1]
    n_proj = w_in.shape[1] // dc
    assert w_in.shape == (d, n_proj * dc) and w_in.dtype == F32
    dv = dc // V_DIM * (V_DIM + ONES_ROWS)
    tm = TK
    tb = tm * INPROJ_SUBTILES
    nj = seq // tb
    row = lambda b, j: (b * nj + j, 0)
    const = lambda b, j: (0, 0)
    out_sds = jax.ShapeDtypeStruct((t, dc), BF16)
    return pl.pallas_call(
        functools.partial(_inproj_kernel, tm=tm, dc=dc),
        grid=(batch, nj),
        in_specs=[
            pl.BlockSpec((tb, d), row),
            pl.BlockSpec((1, d), const),
            pl.BlockSpec(memory_space=pl.ANY),
            pl.BlockSpec(conv_w.shape, const),
            pl.BlockSpec((1, dc), const),
            pl.BlockSpec((1, dc), const),
            pl.BlockSpec((1, dc), const),
            pl.BlockSpec(gmat.shape, const),
        ],
        out_specs=[
            pl.BlockSpec((tb, dc), row),
            pl.BlockSpec((dc, tb), lambda b, j: (0, b * nj + j)),
            pl.BlockSpec((tb, dc), row),
            pl.BlockSpec((INPROJ_SUBTILES, dv, tm), lambda b, j: (b * nj + j, 0, 0)),
        ],
        out_shape=[out_sds, jax.ShapeDtypeStruct((dc, t), BF16), out_sds,
                   jax.ShapeDtypeStruct((t // tm, dv, tm), BF16)],
        scratch_shapes=[pltpu.VMEM((8, dc), F32),
                        pltpu.VMEM((n_proj, d, dc), F32), pltpu.VMEM((d, n_proj * dc), BF16),
                        pltpu.SemaphoreType.DMA((n_proj,))],
        compiler_params=pltpu.CompilerParams(
            dimension_semantics=("arbitrary", "arbitrary"), vmem_limit_bytes=VMEM_LIMIT),
        name="inproj_conv_qknorm",
    )(x2, g, w_in, conv_w, conv_g, qg, kg, gmat)


def _attn_kernel(lp_ref, sg_ref, qT_ref, k_ref, vT_ref, o_ref, m_ref, acc_ref,
                 sa_ref, pb_ref, ab_ref, *, tq, tk, nq, lam_init):
    map_a = slice(0, tq)
    map_b = slice(tq, 2 * tq)

    lp = lp_ref[...]
    lam = (jnp.exp(jnp.sum(lp[0:1, :] * lp[1:2, :], axis=-1, keepdims=True))
           - jnp.exp(jnp.sum(lp[2:3, :] * lp[3:4, :], axis=-1, keepdims=True)) + lam_init)

    def stacked_queries(qi):
        qT = qT_ref[:, qi * tq:(qi + 1) * tq]
        row = lax.broadcasted_iota(jnp.int32, qT.shape, 0)
        zero = jnp.zeros_like(qT)
        return jnp.concatenate([jnp.where(row < HEAD_DIM, qT, zero),
                                jnp.where(row >= HEAD_DIM, qT, zero)], axis=1)

    def pieces(diag):
        return ((slice(0, tq // 2), tk // 2), (slice(tq // 2, tq), tk)) if diag else ((slice(0, tq), tk),)

    def shifted(cols, off):
        return slice(cols.start + off, cols.stop + off)

    def scores(j, qq, cols, n_keys):
        return _dot(k_ref[j * tk:j * tk + n_keys, :], qq[:, cols])

    def softmax(m, cols, s, q0):
        if q0 is not None:
            qrel = q0 + lax.broadcasted_iota(jnp.int32, s.shape, 1)
            krel = lax.broadcasted_iota(jnp.int32, s.shape, 0)
            s = jnp.where(krel <= qrel, s, -jnp.inf)
        sb = s.astype(BF16)
        m_old = m[:, cols]
        m_new = jnp.maximum(m_old, jnp.max(sb, axis=0, keepdims=True).astype(F32))
        alpha = jnp.exp2(m_old - m_new)
        p = jnp.exp2(sb - m_new.astype(BF16))
        m[:, cols] = m_new
        return alpha, p

    def accumulate(acc, cols, alpha, vb, p):
        acc[:, cols] = alpha * acc[:, cols] + _dot(vb, p)

    def prefetch_map_a(sa, j, qq, diag):
        for cols, n_keys in pieces(diag):
            sa[0:n_keys, cols] = scores(j, qq, cols, n_keys)

    stacked = {0: stacked_queries(0)}
    m_ref[0] = jnp.full(m_ref.shape[1:], -jnp.inf, F32)
    acc_ref[0] = jnp.zeros(acc_ref.shape[1:], F32)
    prefetch_map_a(sa_ref.at[0], 0, stacked[0], True)

    for qi in range(nq):
        par = qi % 2
        m, acc, sa, qq = m_ref.at[par], acc_ref.at[par], sa_ref.at[par], stacked[qi]
        for j in range(qi + 1):
            last = j == qi
            if j > 0:
                accumulate(acc, map_b, ab_ref[...], vT_ref[j - 1], pb_ref[...])
            s_b = [scores(j, qq, shifted(cols, tq), n_keys) for cols, n_keys in pieces(last)]
            for cols, n_keys in pieces(last):
                alpha_a, p_a = softmax(m, cols, sa[0:n_keys, cols], cols.start if last else None)
                accumulate(acc, cols, alpha_a, vT_ref[j, :, 0:n_keys], p_a)
            if not last:
                prefetch_map_a(sa, j + 1, qq, j + 1 == qi)
            elif qi + 1 < nq:
                stacked[qi + 1] = stacked_queries(qi + 1)
                m_ref[1 - par] = jnp.full(m_ref.shape[1:], -jnp.inf, F32)
                acc_ref[1 - par] = jnp.zeros(acc_ref.shape[1:], F32)
                prefetch_map_a(sa_ref.at[1 - par], 0, stacked[qi + 1], False)
            for (cols, n_keys), s in zip(pieces(last), s_b):
                alpha_b, p_b = softmax(m, shifted(cols, tq), s, cols.start if last else None)
                if last:
                    accumulate(acc, shifted(cols, tq), alpha_b, vT_ref[j, :, 0:n_keys], p_b)
                else:
                    ab_ref[...] = alpha_b
                    pb_ref[...] = p_b

        o = acc[0:V_DIM, :] / acc[V_DIM:V_DIM + 1, :]
        d = o[:, map_a] - lam * o[:, map_b]
        ms = jnp.mean(d * d, axis=0, keepdims=True)
        o_ref[qi * tq:(qi + 1) * tq, :] = (
            d * lax.rsqrt(ms + EPS) * sg_ref[...] * (1.0 - lam_init)).T.astype(BF16)
        del stacked[qi]


def _attention(lam_params, subln_g, qT, k, vT, *, batch, seq, lam_init):
    dq, t = qT.shape
    n_heads = dq // V_DIM
    tq, tk = TQ, TK
    assert tq == tk and vT.shape[2] == tk
    nq = seq // tq
    nk = seq // tk
    vrows = V_DIM + ONES_ROWS
    const = lambda b, h: (0, 0)
    return pl.pallas_call(
        functools.partial(_attn_kernel, tq=tq, tk=tk, nq=nq, lam_init=lam_init),
        grid=(batch, n_heads),
        in_specs=[
            pl.BlockSpec(lam_params.shape, const),
            pl.BlockSpec((V_DIM, 1), const),
            pl.BlockSpec((V_DIM, seq), lambda b, h: (h, b)),
            pl.BlockSpec((seq, V_DIM), lambda b, h: (b, h)),
            pl.BlockSpec((nk, vrows, tk), lambda b, h: (b, h, 0)),
        ],
        out_specs=pl.BlockSpec((seq, V_DIM), lambda b, h: (b, h)),
        out_shape=jax.ShapeDtypeStruct((t, dq), BF16),
        scratch_shapes=[pltpu.VMEM((2, 1, 2 * tq), F32),
                        pltpu.VMEM((2, vrows, 2 * tq), F32),
                        pltpu.VMEM((2, tk, tq), F32), pltpu.VMEM((tk, tq), BF16), pltpu.VMEM((1, tq), F32)],
        compiler_params=pltpu.CompilerParams(
            dimension_semantics=("arbitrary", "arbitrary"), vmem_limit_bytes=VMEM_LIMIT),
        name="diff_attention",
    )(lam_params, subln_g, qT, k, vT)


def _outproj_router_kernel(x_ref, yc_ref, at_ref, wo_ref, g_ref, wr_ref,
                           h_ref, hn_ref, routeT_ref, cnt_ref, *, tm, dc):
    i = pl.program_id(0)

    @pl.when(i == 0)
    def _():
        cnt_ref[...] = jnp.zeros_like(cnt_ref)

    for r0 in range(0, x_ref.shape[0], tm):
        _outproj_router_subtile(slice(r0, r0 + tm), x_ref, yc_ref, at_ref, wo_ref, g_ref, wr_ref,
                                h_ref, hn_ref, routeT_ref, cnt_ref, tm=tm, dc=dc)


def _outproj_router_subtile(rows, x_ref, yc_ref, at_ref, wo_ref, g_ref, wr_ref,
                            h_ref, hn_ref, routeT_ref, cnt_ref, *, tm, dc):
    h = x_ref[rows, :] + _dot(yc_ref[rows, :], wo_ref[0:dc, :]) + _dot(at_ref[rows, :], wo_ref[dc:2 * dc, :])
    h_ref[rows, :] = h.astype(BF16)
    ms = jnp.mean(h * h, axis=-1, keepdims=True)
    hn = h * lax.rsqrt(ms + EPS) * g_ref[...]
    hi = hn.astype(BF16)
    hn_ref[rows, :] = _pack_rows(hn)
    lo = (hn - hi.astype(F32)).astype(BF16)
    prod = _dot(hi, wr_ref[...])
    logits = prod[:, 0:LANES] + prod[:, LANES:2 * LANES] + _dot(lo, wr_ref[:, 0:LANES])

    lt = logits.T
    neg = -jnp.inf
    grow = lax.broadcasted_iota(jnp.int32, (8, tm), 0).astype(F32)
    gl = jnp.where(grow < N_GROUPS, lt[N_EXPERTS:N_EXPERTS + 8, :], neg)
    gmax = jnp.max(gl, axis=0, keepdims=True)
    g_gate = 1.0 / jnp.sum(jnp.exp(gl - gmax), axis=0, keepdims=True)
    g_idx = jnp.min(jnp.where(gl == gmax, grow, 1e9), axis=0, keepdims=True)
    erow = lax.broadcasted_iota(jnp.int32, (N_EXPERTS, tm), 0).astype(F32)
    e_lo = g_idx * EXPERTS_PER_GROUP
    el = jnp.where((erow >= e_lo) & (erow < e_lo + EXPERTS_PER_GROUP), lt[0:N_EXPERTS, :], neg)
    v1 = jnp.max(el, axis=0, keepdims=True)
    i1 = jnp.min(jnp.where(el == v1, erow, 1e9), axis=0, keepdims=True)
    el2 = jnp.where(erow == i1, neg, el)
    v2 = jnp.max(el2, axis=0, keepdims=True)
    i2 = jnp.min(jnp.where(el2 == v2, erow, 1e9), axis=0, keepdims=True)
    tt = jnp.exp(v2 - v1)
    w1 = g_gate / (1.0 + tt)
    w2 = g_gate * tt / (1.0 + tt)

    sel1 = erow == i1
    sel2 = erow == i2
    oh = jnp.where(sel1 | sel2, 1.0, 0.0)
    ss = lax.broadcasted_iota(jnp.int32, (tm, tm), 0)
    tt_i = lax.broadcasted_iota(jnp.int32, (tm, tm), 1)
    earlier = jnp.where(ss < tt_i, 1.0, 0.0).astype(BF16)
    ranks = _dot(oh.astype(BF16), earlier) + cnt_ref[...]
    r1 = jnp.sum(jnp.where(sel1, ranks, 0.0), axis=0, keepdims=True)
    r2 = jnp.sum(jnp.where(sel2, ranks, 0.0), axis=0, keepdims=True)
    cnt_ref[...] = cnt_ref[...] + jnp.sum(oh, axis=1, keepdims=True)

    routeT = jnp.concatenate([i1, i2, r1, r2, w1, w2, jnp.zeros((2, tm), F32)], axis=0)
    routeT_ref[:, rows] = routeT


def _outproj_router(x2, yc, at, w_out, g, wr_cat):
    t, d = x2.shape
    dc = yc.shape[1]
    tm = TM_PROJ
    tb = tm * OUTPROJ_SUBTILES
    row = lambda i: (i, 0)
    const = lambda i: (0, 0)
    return pl.pallas_call(
        functools.partial(_outproj_router_kernel, tm=tm, dc=dc),
        grid=(t // tb,),
        in_specs=[
            pl.BlockSpec((tb, d), row),
            pl.BlockSpec((tb, dc), row),
            pl.BlockSpec((tb, dc), row),
            pl.BlockSpec(w_out.shape, const),
            pl.BlockSpec((1, d), const),
            pl.BlockSpec(wr_cat.shape, const),
        ],
        out_specs=[
            pl.BlockSpec((tb, d), row),
            pl.BlockSpec((tb, d // 2), row),
            pl.BlockSpec((8, tb), lambda i: (0, i)),
            pl.BlockSpec((N_EXPERTS, 1), const),
        ],
        out_shape=[
            jax.ShapeDtypeStruct((t, d), BF16),
            jax.ShapeDtypeStruct((t, d // 2), jnp.int32),
            jax.ShapeDtypeStruct((8, t), F32),
            jax.ShapeDtypeStruct((N_EXPERTS, 1), F32),
        ],
        compiler_params=pltpu.CompilerParams(
            dimension_semantics=("arbitrary",), vmem_limit_bytes=VMEM_LIMIT),
        name="outproj_router",
    )(x2, yc, at, w_out, g, wr_cat)


def _plan_kernel(cnt_ref, rt_ref, pos_ref, te_ref, nx_ref, sl_ref, nt_ref, *, tm):
    shift = tm.bit_length() - 1
    tiles, starts = [], []
    total = jnp.int32(0)
    for e in range(N_EXPERTS):
        n = lax.shift_right_logical(cnt_ref[e] + (tm - 1), shift)
        tiles.append(n)
        starts.append(total)
        total = total + n
    nt_ref[0] = total

    rt = rt_ref[...]
    ea, eb = rt[0:1, :], rt[1:2, :]
    sa = jnp.zeros_like(ea)
    sb = jnp.zeros_like(eb)
    for e in range(N_EXPERTS):
        start = (starts[e] * tm).astype(F32)
        sa = jnp.where(ea == e, start, sa)
        sb = jnp.where(eb == e, start, sb)
    pos_ref[0:1, :] = (sa + rt[2:3, :]).astype(jnp.int32)
    pos_ref[1:2, :] = (sb + rt[3:4, :]).astype(jnp.int32)

    def clear(i, c):
        te_ref[i] = 0
        nx_ref[i] = -1
        sl_ref[i] = 0
        return c

    lax.fori_loop(0, te_ref.shape[0], clear, 0)

    nxt = jnp.int32(-1)
    next_of = [None] * N_EXPERTS
    for e in reversed(range(N_EXPERTS)):
        next_of[e] = nxt
        nxt = jnp.where(tiles[e] > 0, e, nxt)
    ordinal = jnp.int32(0)
    for e in range(N_EXPERTS):
        slot = ordinal & 1

        def fill(j, c, e=e, slot=slot):
            te_ref[starts[e] + j] = e
            nx_ref[starts[e] + j] = next_of[e]
            sl_ref[starts[e] + j] = slot
            return c

        lax.fori_loop(0, tiles[e], fill, 0)
        ordinal = ordinal + (tiles[e] > 0).astype(jnp.int32)


def _plan(counts, routeT, n_tiles_max):
    t = routeT.shape[1]
    smem = pl.BlockSpec(memory_space=pltpu.SMEM)
    table = jax.ShapeDtypeStruct((n_tiles_max,), jnp.int32)
    return pl.pallas_call(
        functools.partial(_plan_kernel, tm=TM_MOE),
        grid_spec=pltpu.PrefetchScalarGridSpec(
            num_scalar_prefetch=1, grid=(1,),
            in_specs=[pl.BlockSpec(routeT.shape, lambda i, cnt: (0, 0))],
            out_specs=[pl.BlockSpec((2, t), lambda i, cnt: (0, 0)), smem, smem, smem, smem],
        ),
        out_shape=[jax.ShapeDtypeStruct((2, t), jnp.int32), table, table, table,
                   jax.ShapeDtypeStruct((1,), jnp.int32)],
        name="routing_plan",
    )(counts, routeT)


def _moe_kernel(te_ref, nt_ref, nx_ref, sl_ref, x_ref, wg_hbm, wu_hbm, wd_hbm, y_ref,
                wg_st, wu_st, wd_st, wgb_ref, wub_ref, wdb_ref, sem, *, tm, tiles_per_step):
    last = nt_ref[0] - 1

    def weight_copies(e, s):
        return (pltpu.make_async_copy(wg_hbm.at[e], wg_st.at[s], sem.at[s, 0]),
                pltpu.make_async_copy(wu_hbm.at[e], wu_st.at[s], sem.at[s, 1]),
                pltpu.make_async_copy(wd_hbm.at[e], wd_st.at[s], sem.at[s, 2]))

    for u in range(tiles_per_step):
        i = pl.program_id(0) * tiles_per_step + u
        rows = slice(u * tm, (u + 1) * tm)
        ic = jnp.minimum(i, last)
        expert = te_ref[ic]
        slot = sl_ref[ic]
        first_of_expert = (i == 0) | ((i <= last) & (expert != te_ref[jnp.maximum(ic - 1, 0)]))

        if u == 0:
            @pl.when(i == 0)
            def _():
                for c in weight_copies(expert, slot):
                    c.start()

        def expert_mlp(rows, new_weights):
            x_l, x_r = _unpack_rows(x_ref[rows, :])
            x_l = x_l.astype(BF16)
            x_r = x_r.astype(BF16)
            half = x_l.shape[1]
            if new_weights:
                wgb_ref[...] = wg_st[slot].astype(BF16)
            hg = _dot(x_l, wgb_ref[0:half, :]) + _dot(x_r, wgb_ref[half:2 * half, :])
            if new_weights:
                wub_ref[...] = wu_st[slot].astype(BF16)
            hu = _dot(x_l, wub_ref[0:half, :]) + _dot(x_r, wub_ref[half:2 * half, :])
            if new_weights:
                wdb_ref[...] = wd_st[slot].astype(BF16)
            act = hg * (1.0 / (1.0 + jnp.exp(-hg))) * hu
            y_ref[rows, :] = _pack_rows(_dot(act.astype(BF16), wdb_ref[...]))

        @pl.when(first_of_expert)
        def _():
            for c in weight_copies(expert, slot):
                c.wait()
            nxt = nx_ref[ic]

            @pl.when(nxt >= 0)
            def _():
                for c in weight_copies(nxt, 1 - slot):
                    c.start()

            expert_mlp(rows, True)

        @pl.when((i <= last) & jnp.logical_not(first_of_expert))
        def _():
            expert_mlp(rows, False)


def _moe(tile_expert, n_tiles, next_expert, stage_slot, xs, wg, wu, wd):
    p, dp = xs.shape
    d, f = wg.shape[1], wg.shape[2]
    assert dp * 2 == d
    tm, tps = TM_MOE, MOE_TILES_PER_STEP
    tb = tm * tps
    row = lambda s, te, nt, nx, sl: (jnp.minimum(s, (nt[0] - 1) // tps), 0)
    hbm = pl.BlockSpec(memory_space=pl.ANY)
    grid_spec = pltpu.PrefetchScalarGridSpec(
        num_scalar_prefetch=4,
        grid=(p // tb,),
        in_specs=[pl.BlockSpec((tb, dp), row), hbm, hbm, hbm],
        out_specs=pl.BlockSpec((tb, dp), row),
        scratch_shapes=[pltpu.VMEM((2, d, f), F32), pltpu.VMEM((2, d, f), F32), pltpu.VMEM((2, f, d), F32),
                        pltpu.VMEM((d, f), BF16), pltpu.VMEM((d, f), BF16), pltpu.VMEM((f, d), BF16),
                        pltpu.SemaphoreType.DMA((2, 3))],
    )
    return pl.pallas_call(
        functools.partial(_moe_kernel, tm=tm, tiles_per_step=tps),
        grid_spec=grid_spec,
        out_shape=jax.ShapeDtypeStruct((p, dp), jnp.int32),
        compiler_params=pltpu.CompilerParams(
            dimension_semantics=("arbitrary",), vmem_limit_bytes=VMEM_LIMIT),
        name="moe_experts",
    )(tile_expert, n_tiles, next_expert, stage_slot, xs, wg, wu, wd)


def _sc_mesh():
    return plsc.VectorSubcoreMesh(core_axis_name="c", subcore_axis_name="s",
                                  num_cores=SC_CORES, num_subcores=SC_SUBCORES)


def _sc_dispatch(rows, pos, n_out):
    t, d = rows.shape
    win = pos.shape[1]
    assert pos.shape[0] * win == 2 * t

    @functools.partial(pl.kernel, out_type=jax.ShapeDtypeStruct((n_out, d), rows.dtype),
                       mesh=_sc_mesh(), scratch_types=[], name="sc_dispatch")
    def run(rows_hbm, pos_hbm, out_hbm):
        def body(rows_vmem, pa_vmem, pb_vmem):
            pltpu.sync_copy(rows_vmem, out_hbm.at[pa_vmem.at[0]])
            pltpu.sync_copy(rows_vmem, out_hbm.at[pb_vmem.at[0]])

        pltpu.emit_pipeline(
            body, grid=(t // win,),
            in_specs=[pl.BlockSpec((win, d), lambda i: (i, 0)),
                      pl.BlockSpec((1, win), lambda i: (i, 0)),
                      pl.BlockSpec((1, win), lambda i: (i + t // win, 0))],
            out_specs=[],
            core_axis_name=("c", "s"),
            dimension_semantics=(pltpu.PARALLEL,),
        )(rows_hbm, pos_hbm, pos_hbm)

    return run(rows, pos)


def _sc_gather(table, idx):
    d = table.shape[1]
    n_win, win = idx.shape

    @functools.partial(pl.kernel, out_type=jax.ShapeDtypeStruct((n_win * win, d), table.dtype),
                       mesh=_sc_mesh(), scratch_types=[], name="sc_gather")
    def run(table_hbm, idx_hbm, out_hbm):
        def body(idx_vmem, out_vmem):
            pltpu.sync_copy(table_hbm.at[idx_vmem.at[0]], out_vmem)

        pltpu.emit_pipeline(
            body, grid=(n_win,),
            in_specs=[pl.BlockSpec((1, win), lambda i: (i, 0))],
            out_specs=[pl.BlockSpec((win, d), lambda i: (i, 0))],
            core_axis_name=("c", "s"),
            dimension_semantics=(pltpu.PARALLEL,),
        )(idx_hbm, out_hbm)

    return run(table, idx)


def _combine_kernel(h_ref, ya_ref, yb_ref, rt_ref, o_ref):
    rt = rt_ref[...]
    r = jnp.concatenate([rt, jnp.zeros((LANES - rt.shape[0], rt.shape[1]), F32)], axis=0).T
    wa, wb = r[:, 4:5], r[:, 5:6]
    a_l, a_r = _unpack_rows(ya_ref[...])
    b_l, b_r = _unpack_rows(yb_ref[...])
    half = a_l.shape[1]
    o_ref[:, 0:half] = h_ref[:, 0:half].astype(F32) + wa * a_l + wb * b_l
    o_ref[:, half:2 * half] = h_ref[:, half:2 * half].astype(F32) + wa * a_r + wb * b_r


def _combine(hres, yg, routeT):
    t, d = hres.shape
    tm = TM_COMBINE
    nb = t // tm
    return pl.pallas_call(
        _combine_kernel,
        grid=(nb,),
        in_specs=[pl.BlockSpec((tm, d), lambda i: (i, 0)),
                  pl.BlockSpec((tm, d // 2), lambda i: (i, 0)),
                  pl.BlockSpec((tm, d // 2), lambda i: (i + nb, 0)),
                  pl.BlockSpec((routeT.shape[0], tm), lambda i: (0, i))],
        out_specs=pl.BlockSpec((tm, d), lambda i: (i, 0)),
        out_shape=jax.ShapeDtypeStruct((t, d), F32),
        compiler_params=pltpu.CompilerParams(
            dimension_semantics=("arbitrary",), vmem_limit_bytes=VMEM_LIMIT),
        name="combine",
    )(hres, yg, yg, routeT)


def _lambda_init(layer_idx):
    return 0.8 - 0.6 * math.exp(-0.3 * layer_idx)


def _layer(h, l, attn_norm_g, w_in, conv_w, conv_out_g, q_norm_g, k_norm_g,
           lambda_q1, lambda_k1, lambda_q2, lambda_k2, attn_subln_g, w_out,
           ffn_norm_g, w_router_group, w_router_expert, w_exp_gate, w_exp_up, w_exp_down):
    batch, seq, d = h.shape
    t = batch * seq
    dc = conv_w.shape[-1]
    lam_init = _lambda_init(l)
    x2 = h.reshape(t, d)

    reps = dc // HEAD_DIM
    assert dc // CONV_GROUPS == HEAD_DIM
    qg = (jnp.tile(q_norm_g[l], reps) * (HEAD_DIM ** -0.5 * math.log2(math.e))).reshape(1, dc)
    kg = jnp.tile(k_norm_g[l], reps).reshape(1, dc)
    grp = jnp.arange(MXU_TILE) // HEAD_DIM
    gmat = jnp.where(grp[:, None] == grp[None, :], 1.0 / HEAD_DIM, 0.0).astype(BF16)
    yc, qT, k, vT = _inproj(x2, attn_norm_g[l].reshape(1, d), w_in[l], conv_w[l],
                          conv_out_g[l].reshape(1, dc), qg, kg, gmat, batch=batch, seq=seq)

    lam_params = jnp.stack([lambda_q1[l], lambda_k1[l], lambda_q2[l], lambda_k2[l]])
    at = _attention(lam_params, attn_subln_g[l].reshape(V_DIM, 1), qT, k, vT,
                    batch=batch, seq=seq, lam_init=lam_init)

    wr = jnp.concatenate([w_router_expert[l], w_router_group[l],
                          jnp.zeros((d, LANES - N_EXPERTS - N_GROUPS), F32)], axis=1)
    wr_hi = wr.astype(BF16)
    wr_lo = (wr - wr_hi.astype(F32)).astype(BF16)
    hres, hn2, routeT, cnt = _outproj_router(x2, yc, at, w_out[l].astype(BF16),
                                                    ffn_norm_g[l].reshape(1, d),
                                                    jnp.concatenate([wr_hi, wr_lo], axis=1))

    tmm = TM_MOE
    n_tiles_max = (2 * t) // tmm + N_EXPERTS
    p_rows = n_tiles_max * tmm
    pos, tile_expert, next_expert, stage_slot, n_tiles = _plan(
        cnt[:, 0].astype(jnp.int32), routeT, n_tiles_max)

    posw = pos.reshape(2 * t // SC_WIN, SC_WIN)
    xs = _sc_dispatch(hn2, posw, p_rows)

    f = w_exp_gate.shape[-1]
    ys = _moe(tile_expert, n_tiles, next_expert, stage_slot, xs,
              w_exp_gate[l].reshape(N_EXPERTS, d, f),
              w_exp_up[l].reshape(N_EXPERTS, d, f),
              w_exp_down[l].reshape(N_EXPERTS, f, d))
    yg = _sc_gather(ys, posw)
    out = _combine(hres, yg, routeT)
    return out.reshape(batch, seq, d)


def kernel(x, attn_norm_g, w_in, conv_w, conv_out_g, q_norm_g, k_norm_g, lambda_q1, lambda_k1,
           lambda_q2, lambda_k2, attn_subln_g, w_out, ffn_norm_g, w_router_group, w_router_expert,
           w_exp_gate, w_exp_up, w_exp_down):
    h = x
    for l in range(attn_norm_g.shape[0]):
        h = _layer(h, l, attn_norm_g, w_in, conv_w, conv_out_g, q_norm_g, k_norm_g,
                   lambda_q1, lambda_k1, lambda_q2, lambda_k2, attn_subln_g, w_out,
                   ffn_norm_g, w_router_group, w_router_expert, w_exp_gate, w_exp_up, w_exp_down)
    return h
```

```python
import functools
import math

import jax
import jax.numpy as jnp
from jax import lax
from jax.experimental import pallas as pl
from jax.experimental.pallas import tpu as pltpu
from jax.experimental.pallas import tpu_sc as plsc

F32 = jnp.float32
BF16 = jnp.bfloat16

HEAD_DIM = 64
V_DIM = 2 * HEAD_DIM
CONV_GROUPS = 8
N_GROUPS = 4
EXPERTS_PER_GROUP = 8
N_EXPERTS = N_GROUPS * EXPERTS_PER_GROUP
EPS = 1e-6
LANES = 128
MXU_TILE = 256
ONES_ROWS = 16
VMEM_LIMIT = 48 * 1024 * 1024

TM_PROJ = 256
TM_COMBINE = 1024
OUTPROJ_SUBTILES = 4
INPROJ_SUBTILES = 2
TQ = 512
TK = 512
TM_MOE = 512
MOE_TILES_PER_STEP = 2
SC_CORES = 2
SC_SUBCORES = 16
SC_WIN = 64


def _dot(a, b):
    return jnp.dot(a, b, preferred_element_type=F32)


def _pack_rows(x):
    w = x.shape[1] // 2
    bits = lax.bitcast_convert_type(x.astype(BF16).astype(F32), jnp.uint32)
    return lax.bitcast_convert_type((bits[:, :w] >> 16) | bits[:, w:], jnp.int32)


def _unpack_rows(packed):
    bits = lax.bitcast_convert_type(packed, jnp.uint32)
    left = lax.bitcast_convert_type(bits << 16, F32)
    right = lax.bitcast_convert_type(bits & jnp.uint32(0xFFFF0000), F32)
    return left, right


def _group_mean(sq, gm):
    w = gm.shape[0]
    sq = sq.astype(BF16)
    return jnp.concatenate([_dot(sq[:, c:c + w], gm) for c in range(0, sq.shape[1], w)], axis=1)


def _inproj_kernel(x_ref, g_ref, w_hbm, cw_ref, cg_ref, qg_ref, kg_ref, gm_ref,
                   yc_ref, qT_ref, k_ref, vT_ref, carry_ref, wst_ref, w_ref, sem, *, tm, dc):
    j = pl.program_id(1)

    def weight_copy(s):
        return pltpu.make_async_copy(w_hbm.at[:, s * dc:(s + 1) * dc], wst_ref.at[s], sem.at[s])

    @pl.when((pl.program_id(0) == 0) & (j == 0))
    def _():
        for s in range(wst_ref.shape[0]):
            weight_copy(s).start()
        for s in range(wst_ref.shape[0]):
            weight_copy(s).wait()
            w_ref[:, s * dc:(s + 1) * dc] = wst_ref[s].astype(BF16)

    @pl.when(j == 0)
    def _():
        carry_ref[...] = jnp.zeros_like(carry_ref)

    for u in range(x_ref.shape[0] // tm):
        _inproj_subtile(u, x_ref, g_ref, w_ref, cw_ref, cg_ref, qg_ref, kg_ref, gm_ref,
                        yc_ref, qT_ref, k_ref, vT_ref, carry_ref, tm=tm, dc=dc)


def _inproj_subtile(u, x_ref, g_ref, w_ref, cw_ref, cg_ref, qg_ref, kg_ref, gm_ref,
                    yc_ref, qT_ref, k_ref, vT_ref, carry_ref, *, tm, dc):
    rows = slice(u * tm, (u + 1) * tm)
    x = x_ref[rows, :]
    ms = jnp.mean(x * x, axis=-1, keepdims=True)
    hn = (x * lax.rsqrt(ms + EPS) * g_ref[...]).astype(BF16)

    def proj(s):
        return _dot(hn, w_ref[:, s * dc:(s + 1) * dc])

    gm = gm_ref[...]

    c = proj(2) * proj(0)
    prev = carry_ref[...]
    r = lax.broadcasted_iota(jnp.int32, c.shape, 0)
    c1 = jnp.where(r == 0, prev[7:8, :], pltpu.roll(c, 1, 0))
    c2 = jnp.where(r == 0, prev[6:7, :], jnp.where(r == 1, prev[7:8, :], pltpu.roll(c, 2, 0)))
    carry_ref[...] = c[tm - 8:tm, :]
    cw = cw_ref[...]
    y = proj(1) * (cw[0:1, :] * c2 + cw[1:2, :] * c1 + cw[2:3, :] * c)
    yc_ref[rows, :] = (y * lax.rsqrt(_group_mean(y * y, gm) + EPS) * cg_ref[...]).astype(BF16)

    q = proj(3)
    qT_ref[:, rows] = (q * lax.rsqrt(_group_mean(q * q, gm) + EPS) * qg_ref[...]).T.astype(BF16)
    k = proj(4)
    k_ref[rows, :] = (k * lax.rsqrt(_group_mean(k * k, gm) + EPS) * kg_ref[...]).astype(BF16)
    vt = proj(5).T.astype(BF16)
    vrows = V_DIM + ONES_ROWS
    for h in range(dc // V_DIM):
        vT_ref[u, h * vrows:h * vrows + V_DIM, :] = vt[h * V_DIM:(h + 1) * V_DIM, :]
        vT_ref[u, h * vrows + V_DIM:(h + 1) * vrows, :] = jnp.ones((ONES_ROWS, tm), BF16)


def _inproj(x2, g, w_in, conv_w, conv_g, qg, kg, gmat, *, batch, seq):
    t, d = x2.shape
    dc = conv_g.shape[1]
    n_proj = w_in.shape[1] // dc
    assert w_in.shape == (d, n_proj * dc) and w_in.dtype == F32
    dv = dc // V_DIM * (V_DIM + ONES_ROWS)
    tm = TK
    tb = tm * INPROJ_SUBTILES
    nj = seq // tb
    row = lambda b, j: (b * nj + j, 0)
    const = lambda b, j: (0, 0)
    out_sds = jax.ShapeDtypeStruct((t, dc), BF16)
    return pl.pallas_call(
        functools.partial(_inproj_kernel, tm=tm, dc=dc),
        grid=(batch, nj),
        in_specs=[
            pl.BlockSpec((tb, d), row),
            pl.BlockSpec((1, d), const),
            pl.BlockSpec(memory_space=pl.ANY),
            pl.BlockSpec(conv_w.shape, const),
            pl.BlockSpec((1, dc), const),
            pl.BlockSpec((1, dc), const),
            pl.BlockSpec((1, dc), const),
            pl.BlockSpec(gmat.shape, const),
        ],
        out_specs=[
            pl.BlockSpec((tb, dc), row),
            pl.BlockSpec((dc, tb), lambda b, j: (0, b * nj + j)),
            pl.BlockSpec((tb, dc), row),
            pl.BlockSpec((INPROJ_SUBTILES, dv, tm), lambda b, j: (b * nj + j, 0, 0)),
        ],
        out_shape=[out_sds, jax.ShapeDtypeStruct((dc, t), BF16), out_sds,
                   jax.ShapeDtypeStruct((t // tm, dv, tm), BF16)],
        scratch_shapes=[pltpu.VMEM((8, dc), F32),
                        pltpu.VMEM((n_proj, d, dc), F32), pltpu.VMEM((d, n_proj * dc), BF16),
                        pltpu.SemaphoreType.DMA((n_proj,))],
        compiler_params=pltpu.CompilerParams(
            dimension_semantics=("arbitrary", "arbitrary"), vmem_limit_bytes=VMEM_LIMIT),
        name="inproj_conv_qknorm",
    )(x2, g, w_in, conv_w, conv_g, qg, kg, gmat)


def _attn_kernel(lp_ref, sg_ref, qT_ref, k_ref, vT_ref, o_ref, m_ref, acc_ref,
                 sa_ref, pb_ref, ab_ref, *, tq, tk, nq, lam_init):
    map_a = slice(0, tq)
    map_b = slice(tq, 2 * tq)

    lp = lp_ref[...]
    lam = (jnp.exp(jnp.sum(lp[0:1, :] * lp[1:2, :], axis=-1, keepdims=True))
           - jnp.exp(jnp.sum(lp[2:3, :] * lp[3:4, :], axis=-1, keepdims=True)) + lam_init)

    def stacked_queries(qi):
        qT = qT_ref[:, qi * tq:(qi + 1) * tq]
        row = lax.broadcasted_iota(jnp.int32, qT.shape, 0)
        zero = jnp.zeros_like(qT)
        return jnp.concatenate([jnp.where(row < HEAD_DIM, qT, zero),
                                jnp.where(row >= HEAD_DIM, qT, zero)], axis=1)

    def pieces(diag):
        return ((slice(0, tq // 2), tk // 2), (slice(tq // 2, tq), tk)) if diag else ((slice(0, tq), tk),)

    def shifted(cols, off):
        return slice(cols.start + off, cols.stop + off)

    def scores(j, qq, cols, n_keys):
        return _dot(k_ref[j * tk:j * tk + n_keys, :], qq[:, cols])

    def softmax(m, cols, s, q0):
        if q0 is not None:
            qrel = q0 + lax.broadcasted_iota(jnp.int32, s.shape, 1)
            krel = lax.broadcasted_iota(jnp.int32, s.shape, 0)
            s = jnp.where(krel <= qrel, s, -jnp.inf)
        sb = s.astype(BF16)
        m_old = m[:, cols]
        m_new = jnp.maximum(m_old, jnp.max(sb, axis=0, keepdims=True).astype(F32))
        alpha = jnp.exp2(m_old - m_new)
        p = jnp.exp2(sb - m_new.astype(BF16))
        m[:, cols] = m_new
        return alpha, p

    def accumulate(acc, cols, alpha, vb, p):
        acc[:, cols] = alpha * acc[:, cols] + _dot(vb, p)

    def prefetch_map_a(sa, j, qq, diag):
        for cols, n_keys in pieces(diag):
            sa[0:n_keys, cols] = scores(j, qq, cols, n_keys)

    stacked = {0: stacked_queries(0)}
    m_ref[0] = jnp.full(m_ref.shape[1:], -jnp.inf, F32)
    acc_ref[0] = jnp.zeros(acc_ref.shape[1:], F32)
    prefetch_map_a(sa_ref.at[0], 0, stacked[0], True)

    for qi in range(nq):
        par = qi % 2
        m, acc, sa, qq = m_ref.at[par], acc_ref.at[par], sa_ref.at[par], stacked[qi]
        for j in range(qi + 1):
            last = j == qi
            if j > 0:
                accumulate(acc, map_b, ab_ref[...], vT_ref[j - 1], pb_ref[...])
            s_b = [scores(j, qq, shifted(cols, tq), n_keys) for cols, n_keys in pieces(last)]
            for cols, n_keys in pieces(last):
                alpha_a, p_a = softmax(m, cols, sa[0:n_keys, cols], cols.start if last else None)
                accumulate(acc, cols, alpha_a, vT_ref[j, :, 0:n_keys], p_a)
            if not last:
                prefetch_map_a(sa, j + 1, qq, j + 1 == qi)
            elif qi + 1 < nq:
                stacked[qi + 1] = stacked_queries(qi + 1)
                m_ref[1 - par] = jnp.full(m_ref.shape[1:], -jnp.inf, F32)
                acc_ref[1 - par] = jnp.zeros(acc_ref.shape[1:], F32)
                prefetch_map_a(sa_ref.at[1 - par], 0, stacked[qi + 1], False)
            for (cols, n_keys), s in zip(pieces(last), s_b):
                alpha_b, p_b = softmax(m, shifted(cols, tq), s, cols.start if last else None)
                if last:
                    accumulate(acc, shifted(cols, tq), alpha_b, vT_ref[j, :, 0:n_keys], p_b)
                else:
                    ab_ref[...] = alpha_b
                    pb_ref[...] = p_b

        o = acc[0:V_DIM, :] / acc[V_DIM:V_DIM + 1, :]
        d = o[:, map_a] - lam * o[:, map_b]
        ms = jnp.mean(d * d, axis=0, keepdims=True)
        o_ref[qi * tq:(qi + 1) * tq, :] = (
            d * lax.rsqrt(ms + EPS) * sg_ref[...] * (1.0 - lam_init)).T.astype(BF16)
        del stacked[qi]


def _attention(lam_params, subln_g, qT, k, vT, *, batch, seq, lam_init):
    dq, t = qT.shape
    n_heads = dq // V_DIM
    tq, tk = TQ, TK
    assert tq == tk and vT.shape[2] == tk
    nq = seq // tq
    nk = seq // tk
    vrows = V_DIM + ONES_ROWS
    const = lambda b, h: (0, 0)
    return pl.pallas_call(
        functools.partial(_attn_kernel, tq=tq, tk=tk, nq=nq, lam_init=lam_init),
        grid=(batch, n_heads),
        in_specs=[
            pl.BlockSpec(lam_params.shape, const),
            pl.BlockSpec((V_DIM, 1), const),
            pl.BlockSpec((V_DIM, seq), lambda b, h: (h, b)),
            pl.BlockSpec((seq, V_DIM), lambda b, h: (b, h)),
            pl.BlockSpec((nk, vrows, tk), lambda b, h: (b, h, 0)),
        ],
        out_specs=pl.BlockSpec((seq, V_DIM), lambda b, h: (b, h)),
        out_shape=jax.ShapeDtypeStruct((t, dq), BF16),
        scratch_shapes=[pltpu.VMEM((2, 1, 2 * tq), F32),
                        pltpu.VMEM((2, vrows, 2 * tq), F32),
                        pltpu.VMEM((2, tk, tq), F32), pltpu.VMEM((tk, tq), BF16), pltpu.VMEM((1, tq), F32)],
        compiler_params=pltpu.CompilerParams(
            dimension_semantics=("arbitrary", "arbitrary"), vmem_limit_bytes=VMEM_LIMIT),
        name="diff_attention",
    )(lam_params, subln_g, qT, k, vT)


def _outproj_router_kernel(x_ref, yc_ref, at_ref, wo_hbm, g_ref, wr_ref,
                           h_ref, hn_ref, routeT_ref, cnt_ref, wst_ref, wo_ref, sem, *, tm, dc):
    i = pl.program_id(0)

    @pl.when(i == 0)
    def _():
        cnt_ref[...] = jnp.zeros_like(cnt_ref)
        weight_copy = pltpu.make_async_copy(wo_hbm, wst_ref, sem.at[0])
        weight_copy.start()
        weight_copy.wait()
        wo_ref[...] = wst_ref[...].astype(BF16)

    for r0 in range(0, x_ref.shape[0], tm):
        _outproj_router_subtile(slice(r0, r0 + tm), x_ref, yc_ref, at_ref, wo_ref, g_ref, wr_ref,
                                h_ref, hn_ref, routeT_ref, cnt_ref, tm=tm, dc=dc)


def _outproj_router_subtile(rows, x_ref, yc_ref, at_ref, wo_ref, g_ref, wr_ref,
                            h_ref, hn_ref, routeT_ref, cnt_ref, *, tm, dc):
    h = x_ref[rows, :] + _dot(yc_ref[rows, :], wo_ref[0:dc, :]) + _dot(at_ref[rows, :], wo_ref[dc:2 * dc, :])
    h_ref[rows, :] = h.astype(BF16)
    ms = jnp.mean(h * h, axis=-1, keepdims=True)
    hn = h * lax.rsqrt(ms + EPS) * g_ref[...]
    hi = hn.astype(BF16)
    hn_ref[rows, :] = _pack_rows(hn)
    lo = (hn - hi.astype(F32)).astype(BF16)
    prod = _dot(hi, wr_ref[...])
    logits = prod[:, 0:LANES] + prod[:, LANES:2 * LANES] + _dot(lo, wr_ref[:, 0:LANES])

    lt = logits.T
    neg = -jnp.inf
    grow = lax.broadcasted_iota(jnp.int32, (8, tm), 0).astype(F32)
    gl = jnp.where(grow < N_GROUPS, lt[N_EXPERTS:N_EXPERTS + 8, :], neg)
    gmax = jnp.max(gl, axis=0, keepdims=True)
    g_gate = 1.0 / jnp.sum(jnp.exp(gl - gmax), axis=0, keepdims=True)
    g_idx = jnp.min(jnp.where(gl == gmax, grow, 1e9), axis=0, keepdims=True)
    erow = lax.broadcasted_iota(jnp.int32, (N_EXPERTS, tm), 0).astype(F32)
    e_lo = g_idx * EXPERTS_PER_GROUP
    el = jnp.where((erow >= e_lo) & (erow < e_lo + EXPERTS_PER_GROUP), lt[0:N_EXPERTS, :], neg)
    v1 = jnp.max(el, axis=0, keepdims=True)
    i1 = jnp.min(jnp.where(el == v1, erow, 1e9), axis=0, keepdims=True)
    el2 = jnp.where(erow == i1, neg, el)
    v2 = jnp.max(el2, axis=0, keepdims=True)
    i2 = jnp.min(jnp.where(el2 == v2, erow, 1e9), axis=0, keepdims=True)
    tt = jnp.exp(v2 - v1)
    w1 = g_gate / (1.0 + tt)
    w2 = g_gate * tt / (1.0 + tt)

    sel1 = erow == i1
    sel2 = erow == i2
    oh = jnp.where(sel1 | sel2, 1.0, 0.0)
    ss = lax.broadcasted_iota(jnp.int32, (tm, tm), 0)
    tt_i = lax.broadcasted_iota(jnp.int32, (tm, tm), 1)
    earlier = jnp.where(ss < tt_i, 1.0, 0.0).astype(BF16)
    ranks = _dot(oh.astype(BF16), earlier) + cnt_ref[...]
    r1 = jnp.sum(jnp.where(sel1, ranks, 0.0), axis=0, keepdims=True)
    r2 = jnp.sum(jnp.where(sel2, ranks, 0.0), axis=0, keepdims=True)
    cnt_ref[...] = cnt_ref[...] + jnp.sum(oh, axis=1, keepdims=True)

    routeT = jnp.concatenate([i1, i2, r1, r2, w1, w2, jnp.zeros((2, tm), F32)], axis=0)
    routeT_ref[:, rows] = routeT


def _outproj_router(x2, yc, at, w_out, g, wr_cat):
    t, d = x2.shape
    dc = yc.shape[1]
    tm = TM_PROJ
    tb = tm * OUTPROJ_SUBTILES
    row = lambda i: (i, 0)
    const = lambda i: (0, 0)
    return pl.pallas_call(
        functools.partial(_outproj_router_kernel, tm=tm, dc=dc),
        grid=(t // tb,),
        in_specs=[
            pl.BlockSpec((tb, d), row),
            pl.BlockSpec((tb, dc), row),
            pl.BlockSpec((tb, dc), row),
            pl.BlockSpec(memory_space=pl.ANY),
            pl.BlockSpec((1, d), const),
            pl.BlockSpec(wr_cat.shape, const),
        ],
        out_specs=[
            pl.BlockSpec((tb, d), row),
            pl.BlockSpec((tb, d // 2), row),
            pl.BlockSpec((8, tb), lambda i: (0, i)),
            pl.BlockSpec((N_EXPERTS, 1), const),
        ],
        out_shape=[
            jax.ShapeDtypeStruct((t, d), BF16),
            jax.ShapeDtypeStruct((t, d // 2), jnp.int32),
            jax.ShapeDtypeStruct((8, t), F32),
            jax.ShapeDtypeStruct((N_EXPERTS, 1), F32),
        ],
        scratch_shapes=[pltpu.VMEM(w_out.shape, F32), pltpu.VMEM(w_out.shape, BF16),
                        pltpu.SemaphoreType.DMA((1,))],
        compiler_params=pltpu.CompilerParams(
            dimension_semantics=("arbitrary",), vmem_limit_bytes=VMEM_LIMIT),
        name="outproj_router",
    )(x2, yc, at, w_out, g, wr_cat)


def _plan_kernel(cnt_ref, rt_ref, pos_ref, te_ref, nx_ref, sl_ref, nt_ref, *, tm):
    shift = tm.bit_length() - 1
    tiles, starts = [], []
    total = jnp.int32(0)
    for e in range(N_EXPERTS):
        n = lax.shift_right_logical(cnt_ref[e] + (tm - 1), shift)
        tiles.append(n)
        starts.append(total)
        total = total + n
    nt_ref[0] = total

    rt = rt_ref[...]
    ea, eb = rt[0:1, :], rt[1:2, :]
    sa = jnp.zeros_like(ea)
    sb = jnp.zeros_like(eb)
    for e in range(N_EXPERTS):
        start = (starts[e] * tm).astype(F32)
        sa = jnp.where(ea == e, start, sa)
        sb = jnp.where(eb == e, start, sb)
    pos_ref[0:1, :] = (sa + rt[2:3, :]).astype(jnp.int32)
    pos_ref[1:2, :] = (sb + rt[3:4, :]).astype(jnp.int32)

    def clear(i, c):
        te_ref[i] = 0
        nx_ref[i] = -1
        sl_ref[i] = 0
        return c

    lax.fori_loop(0, te_ref.shape[0], clear, 0)

    nxt = jnp.int32(-1)
    next_of = [None] * N_EXPERTS
    for e in reversed(range(N_EXPERTS)):
        next_of[e] = nxt
        nxt = jnp.where(tiles[e] > 0, e, nxt)
    ordinal = jnp.int32(0)
    for e in range(N_EXPERTS):
        slot = ordinal & 1

        def fill(j, c, e=e, slot=slot):
            te_ref[starts[e] + j] = e
            nx_ref[starts[e] + j] = next_of[e]
            sl_ref[starts[e] + j] = slot
            return c

        lax.fori_loop(0, tiles[e], fill, 0)
        ordinal = ordinal + (tiles[e] > 0).astype(jnp.int32)


def _plan(counts, routeT, n_tiles_max):
    t = routeT.shape[1]
    smem = pl.BlockSpec(memory_space=pltpu.SMEM)
    table = jax.ShapeDtypeStruct((n_tiles_max,), jnp.int32)
    return pl.pallas_call(
        functools.partial(_plan_kernel, tm=TM_MOE),
        grid_spec=pltpu.PrefetchScalarGridSpec(
            num_scalar_prefetch=1, grid=(1,),
            in_specs=[pl.BlockSpec(routeT.shape, lambda i, cnt: (0, 0))],
            out_specs=[pl.BlockSpec((2, t), lambda i, cnt: (0, 0)), smem, smem, smem, smem],
        ),
        out_shape=[jax.ShapeDtypeStruct((2, t), jnp.int32), table, table, table,
                   jax.ShapeDtypeStruct((1,), jnp.int32)],
        name="routing_plan",
    )(counts, routeT)


def _moe_kernel(te_ref, nt_ref, nx_ref, sl_ref, x_ref, wg_hbm, wu_hbm, wd_hbm, y_ref,
                wg_st, wu_st, wd_st, wgb_ref, wub_ref, wdb_ref, sem, *, tm, tiles_per_step):
    last = nt_ref[0] - 1

    def weight_copies(e, s):
        return (pltpu.make_async_copy(wg_hbm.at[e], wg_st.at[s], sem.at[s, 0]),
                pltpu.make_async_copy(wu_hbm.at[e], wu_st.at[s], sem.at[s, 1]),
                pltpu.make_async_copy(wd_hbm.at[e], wd_st.at[s], sem.at[s, 2]))

    for u in range(tiles_per_step):
        i = pl.program_id(0) * tiles_per_step + u
        rows = slice(u * tm, (u + 1) * tm)
        ic = jnp.minimum(i, last)
        expert = te_ref[ic]
        slot = sl_ref[ic]
        first_of_expert = (i == 0) | ((i <= last) & (expert != te_ref[jnp.maximum(ic - 1, 0)]))

        if u == 0:
            @pl.when(i == 0)
            def _():
                for c in weight_copies(expert, slot):
                    c.start()

        def expert_mlp(rows, new_weights):
            x_l, x_r = _unpack_rows(x_ref[rows, :])
            x_l = x_l.astype(BF16)
            x_r = x_r.astype(BF16)
            half = x_l.shape[1]
            if new_weights:
                wgb_ref[...] = wg_st[slot].astype(BF16)
            hg = _dot(x_l, wgb_ref[0:half, :]) + _dot(x_r, wgb_ref[half:2 * half, :])
            if new_weights:
                wub_ref[...] = wu_st[slot].astype(BF16)
            hu = _dot(x_l, wub_ref[0:half, :]) + _dot(x_r, wub_ref[half:2 * half, :])
            if new_weights:
                wdb_ref[...] = wd_st[slot].astype(BF16)
            act = hg * (1.0 / (1.0 + jnp.exp(-hg))) * hu
            y_ref[rows, :] = _pack_rows(_dot(act.astype(BF16), wdb_ref[...]))

        @pl.when(first_of_expert)
        def _():
            for c in weight_copies(expert, slot):
                c.wait()
            nxt = nx_ref[ic]

            @pl.when(nxt >= 0)
            def _():
                for c in weight_copies(nxt, 1 - slot):
                    c.start()

            expert_mlp(rows, True)

        @pl.when((i <= last) & jnp.logical_not(first_of_expert))
        def _():
            expert_mlp(rows, False)


def _moe(tile_expert, n_tiles, next_expert, stage_slot, xs, wg, wu, wd):
    p, dp = xs.shape
    d, f = wg.shape[1], wg.shape[2]
    assert dp * 2 == d
    tm, tps = TM_MOE, MOE_TILES_PER_STEP
    tb = tm * tps
    row = lambda s, te, nt, nx, sl: (jnp.minimum(s, (nt[0] - 1) // tps), 0)
    hbm = pl.BlockSpec(memory_space=pl.ANY)
    grid_spec = pltpu.PrefetchScalarGridSpec(
        num_scalar_prefetch=4,
        grid=(p // tb,),
        in_specs=[pl.BlockSpec((tb, dp), row), hbm, hbm, hbm],
        out_specs=pl.BlockSpec((tb, dp), row),
        scratch_shapes=[pltpu.VMEM((2, d, f), F32), pltpu.VMEM((2, d, f), F32), pltpu.VMEM((2, f, d), F32),
                        pltpu.VMEM((d, f), BF16), pltpu.VMEM((d, f), BF16), pltpu.VMEM((f, d), BF16),
                        pltpu.SemaphoreType.DMA((2, 3))],
    )
    return pl.pallas_call(
        functools.partial(_moe_kernel, tm=tm, tiles_per_step=tps),
        grid_spec=grid_spec,
        out_shape=jax.ShapeDtypeStruct((p, dp), jnp.int32),
        compiler_params=pltpu.CompilerParams(
            dimension_semantics=("arbitrary",), vmem_limit_bytes=VMEM_LIMIT),
        name="moe_experts",
    )(tile_expert, n_tiles, next_expert, stage_slot, xs, wg, wu, wd)


def _sc_mesh():
    return plsc.VectorSubcoreMesh(core_axis_name="c", subcore_axis_name="s",
                                  num_cores=SC_CORES, num_subcores=SC_SUBCORES)


def _sc_dispatch(rows, pos, n_out):
    t, d = rows.shape
    win = pos.shape[1]
    assert pos.shape[0] * win == 2 * t

    @functools.partial(pl.kernel, out_type=jax.ShapeDtypeStruct((n_out, d), rows.dtype),
                       mesh=_sc_mesh(), scratch_types=[], name="sc_dispatch")
    def run(rows_hbm, pos_hbm, out_hbm):
        def body(rows_vmem, pa_vmem, pb_vmem):
            pltpu.sync_copy(rows_vmem, out_hbm.at[pa_vmem.at[0]])
            pltpu.sync_copy(rows_vmem, out_hbm.at[pb_vmem.at[0]])

        pltpu.emit_pipeline(
            body, grid=(t // win,),
            in_specs=[pl.BlockSpec((win, d), lambda i: (i, 0)),
                      pl.BlockSpec((1, win), lambda i: (i, 0)),
                      pl.BlockSpec((1, win), lambda i: (i + t // win, 0))],
            out_specs=[],
            core_axis_name=("c", "s"),
            dimension_semantics=(pltpu.PARALLEL,),
        )(rows_hbm, pos_hbm, pos_hbm)

    return run(rows, pos)


def _sc_gather(table, idx):
    d = table.shape[1]
    n_win, win = idx.shape

    @functools.partial(pl.kernel, out_type=jax.ShapeDtypeStruct((n_win * win, d), table.dtype),
                       mesh=_sc_mesh(), scratch_types=[], name="sc_gather")
    def run(table_hbm, idx_hbm, out_hbm):
        def body(idx_vmem, out_vmem):
            pltpu.sync_copy(table_hbm.at[idx_vmem.at[0]], out_vmem)

        pltpu.emit_pipeline(
            body, grid=(n_win,),
            in_specs=[pl.BlockSpec((1, win), lambda i: (i, 0))],
            out_specs=[pl.BlockSpec((win, d), lambda i: (i, 0))],
            core_axis_name=("c", "s"),
            dimension_semantics=(pltpu.PARALLEL,),
        )(idx_hbm, out_hbm)

    return run(table, idx)


def _combine_kernel(h_ref, ya_ref, yb_ref, rt_ref, o_ref):
    rt = rt_ref[...]
    r = jnp.concatenate([rt, jnp.zeros((LANES - rt.shape[0], rt.shape[1]), F32)], axis=0).T
    wa, wb = r[:, 4:5], r[:, 5:6]
    a_l, a_r = _unpack_rows(ya_ref[...])
    b_l, b_r = _unpack_rows(yb_ref[...])
    half = a_l.shape[1]
    o_ref[:, 0:half] = h_ref[:, 0:half].astype(F32) + wa * a_l + wb * b_l
    o_ref[:, half:2 * half] = h_ref[:, half:2 * half].astype(F32) + wa * a_r + wb * b_r


def _combine(hres, yg, routeT):
    t, d = hres.shape
    tm = TM_COMBINE
    nb = t // tm
    return pl.pallas_call(
        _combine_kernel,
        grid=(nb,),
        in_specs=[pl.BlockSpec((tm, d), lambda i: (i, 0)),
                  pl.BlockSpec((tm, d // 2), lambda i: (i, 0)),
                  pl.BlockSpec((tm, d // 2), lambda i: (i + nb, 0)),
                  pl.BlockSpec((routeT.shape[0], tm), lambda i: (0, i))],
        out_specs=pl.BlockSpec((tm, d), lambda i: (i, 0)),
        out_shape=jax.ShapeDtypeStruct((t, d), F32),
        compiler_params=pltpu.CompilerParams(
            dimension_semantics=("arbitrary",), vmem_limit_bytes=VMEM_LIMIT),
        name="combine",
    )(hres, yg, yg, routeT)


def _lambda_init(layer_idx):
    return 0.8 - 0.6 * math.exp(-0.3 * layer_idx)


def _layer(h, l, attn_norm_g, w_in, conv_w, conv_out_g, q_norm_g, k_norm_g,
           lambda_q1, lambda_k1, lambda_q2, lambda_k2, attn_subln_g, w_out,
           ffn_norm_g, w_router_group, w_router_expert, w_exp_gate, w_exp_up, w_exp_down):
    batch, seq, d = h.shape
    t = batch * seq
    dc = conv_w.shape[-1]
    lam_init = _lambda_init(l)
    x2 = h.reshape(t, d)

    reps = dc // HEAD_DIM
    assert dc // CONV_GROUPS == HEAD_DIM
    qg = (jnp.tile(q_norm_g[l], reps) * (HEAD_DIM ** -0.5 * math.log2(math.e))).reshape(1, dc)
    kg = jnp.tile(k_norm_g[l], reps).reshape(1, dc)
    grp = jnp.arange(MXU_TILE) // HEAD_DIM
    gmat = jnp.where(grp[:, None] == grp[None, :], 1.0 / HEAD_DIM, 0.0).astype(BF16)
    yc, qT, k, vT = _inproj(x2, attn_norm_g[l].reshape(1, d), w_in[l], conv_w[l],
                          conv_out_g[l].reshape(1, dc), qg, kg, gmat, batch=batch, seq=seq)

    lam_params = jnp.stack([lambda_q1[l], lambda_k1[l], lambda_q2[l], lambda_k2[l]])
    at = _attention(lam_params, attn_subln_g[l].reshape(V_DIM, 1), qT, k, vT,
                    batch=batch, seq=seq, lam_init=lam_init)

    wr = jnp.concatenate([w_router_expert[l], w_router_group[l],
                          jnp.zeros((d, LANES - N_EXPERTS - N_GROUPS), F32)], axis=1)
    wr_hi = wr.astype(BF16)
    wr_lo = (wr - wr_hi.astype(F32)).astype(BF16)
    hres, hn2, routeT, cnt = _outproj_router(x2, yc, at, w_out[l],
                                                    ffn_norm_g[l].reshape(1, d),
                                                    jnp.concatenate([wr_hi, wr_lo], axis=1))

    tmm = TM_MOE
    n_tiles_max = (2 * t) // tmm + N_EXPERTS
    p_rows = n_tiles_max * tmm
    pos, tile_expert, next_expert, stage_slot, n_tiles = _plan(
        cnt[:, 0].astype(jnp.int32), routeT, n_tiles_max)

    posw = pos.reshape(2 * t // SC_WIN, SC_WIN)
    xs = _sc_dispatch(hn2, posw, p_rows)

    f = w_exp_gate.shape[-1]
    ys = _moe(tile_expert, n_tiles, next_expert, stage_slot, xs,
              w_exp_gate[l].reshape(N_EXPERTS, d, f),
              w_exp_up[l].reshape(N_EXPERTS, d, f),
              w_exp_down[l].reshape(N_EXPERTS, f, d))
    yg = _sc_gather(ys, posw)
    out = _combine(hres, yg, routeT)
    return out.reshape(batch, seq, d)


def kernel(x, attn_norm_g, w_in, conv_w, conv_out_g, q_norm_g, k_norm_g, lambda_q1, lambda_k1,
           lambda_q2, lambda_k2, attn_subln_g, w_out, ffn_norm_g, w_router_group, w_router_expert,
           w_exp_gate, w_exp_up, w_exp_down):
    h = x
    for l in range(attn_norm_g.shape[0]):
        h = _layer(h, l, attn_norm_g, w_in, conv_w, conv_out_g, q_norm_g, k_norm_g,
                   lambda_q1, lambda_k1, lambda_q2, lambda_k2, attn_subln_g, w_out,
                   ffn_norm_g, w_router_group, w_router_expert, w_exp_gate, w_exp_up, w_exp_down)
    return h
```

```python
import functools
import math

import jax
import jax.numpy as jnp
from jax import lax
from jax.experimental import pallas as pl
from jax.experimental.pallas import tpu as pltpu
from jax.experimental.pallas import tpu_sc as plsc

F32 = jnp.float32
BF16 = jnp.bfloat16

HEAD_DIM = 64
V_DIM = 2 * HEAD_DIM
CONV_GROUPS = 8
N_GROUPS = 4
EXPERTS_PER_GROUP = 8
N_EXPERTS = N_GROUPS * EXPERTS_PER_GROUP
EPS = 1e-6
LANES = 128
MXU_TILE = 256
ONES_ROWS = 16
VMEM_LIMIT = 48 * 1024 * 1024

TM_PROJ = 256
TM_COMBINE = 1024
COMBINE_RING = 3
OUTPROJ_SUBTILES = 4
INPROJ_SUBTILES = 2
TQ = 512
TK = 512
TM_MOE = 512
MOE_TILES_PER_STEP = 2
SC_CORES = 2
SC_SUBCORES = 16
SC_WIN = 64


def _dot(a, b):
    return jnp.dot(a, b, preferred_element_type=F32)


def _pack_rows(x):
    w = x.shape[1] // 2
    bits = lax.bitcast_convert_type(x.astype(BF16).astype(F32), jnp.uint32)
    return lax.bitcast_convert_type((bits[:, :w] >> 16) | bits[:, w:], jnp.int32)


def _unpack_rows(packed):
    bits = lax.bitcast_convert_type(packed, jnp.uint32)
    left = lax.bitcast_convert_type(bits << 16, F32)
    right = lax.bitcast_convert_type(bits & jnp.uint32(0xFFFF0000), F32)
    return left, right


def _group_mean(sq, gm):
    w = gm.shape[0]
    sq = sq.astype(BF16)
    return jnp.concatenate([_dot(sq[:, c:c + w], gm) for c in range(0, sq.shape[1], w)], axis=1)


def _inproj_kernel(x_ref, g_ref, w_hbm, cw_ref, cg_ref, qg_ref, kg_ref, gm_ref,
                   yc_ref, qT_ref, k_ref, vT_ref, carry_ref, wst_ref, w_ref, sem, *, tm, dc):
    j = pl.program_id(1)

    def weight_copy(s):
        return pltpu.make_async_copy(w_hbm.at[:, s * dc:(s + 1) * dc], wst_ref.at[s], sem.at[s])

    @pl.when((pl.program_id(0) == 0) & (j == 0))
    def _():
        for s in range(wst_ref.shape[0]):
            weight_copy(s).start()
        for s in range(wst_ref.shape[0]):
            weight_copy(s).wait()
            w_ref[:, s * dc:(s + 1) * dc] = wst_ref[s].astype(BF16)

    @pl.when(j == 0)
    def _():
        carry_ref[...] = jnp.zeros_like(carry_ref)

    for u in range(x_ref.shape[0] // tm):
        _inproj_subtile(u, x_ref, g_ref, w_ref, cw_ref, cg_ref, qg_ref, kg_ref, gm_ref,
                        yc_ref, qT_ref, k_ref, vT_ref, carry_ref, tm=tm, dc=dc)


def _inproj_subtile(u, x_ref, g_ref, w_ref, cw_ref, cg_ref, qg_ref, kg_ref, gm_ref,
                    yc_ref, qT_ref, k_ref, vT_ref, carry_ref, *, tm, dc):
    rows = slice(u * tm, (u + 1) * tm)
    x = x_ref[rows, :]
    ms = jnp.mean(x * x, axis=-1, keepdims=True)
    hn = (x * lax.rsqrt(ms + EPS) * g_ref[...]).astype(BF16)

    def proj(s):
        return _dot(hn, w_ref[:, s * dc:(s + 1) * dc])

    gm = gm_ref[...]

    c = proj(2) * proj(0)
    prev = carry_ref[...]
    r = lax.broadcasted_iota(jnp.int32, c.shape, 0)
    c1 = jnp.where(r == 0, prev[7:8, :], pltpu.roll(c, 1, 0))
    c2 = jnp.where(r == 0, prev[6:7, :], jnp.where(r == 1, prev[7:8, :], pltpu.roll(c, 2, 0)))
    carry_ref[...] = c[tm - 8:tm, :]
    cw = cw_ref[...]
    y = proj(1) * (cw[0:1, :] * c2 + cw[1:2, :] * c1 + cw[2:3, :] * c)
    yc_ref[rows, :] = (y * lax.rsqrt(_group_mean(y * y, gm) + EPS) * cg_ref[...]).astype(BF16)

    q = proj(3)
    qT_ref[:, rows] = (q * lax.rsqrt(_group_mean(q * q, gm) + EPS) * qg_ref[...]).T.astype(BF16)
    k = proj(4)
    k_ref[rows, :] = (k * lax.rsqrt(_group_mean(k * k, gm) + EPS) * kg_ref[...]).astype(BF16)
    vt = proj(5).T.astype(BF16)
    vrows = V_DIM + ONES_ROWS
    for h in range(dc // V_DIM):
        vT_ref[u, h * vrows:h * vrows + V_DIM, :] = vt[h * V_DIM:(h + 1) * V_DIM, :]
        vT_ref[u, h * vrows + V_DIM:(h + 1) * vrows, :] = jnp.ones((ONES_ROWS, tm), BF16)


def _inproj(x2, g, w_in, conv_w, conv_g, qg, kg, gmat, *, batch, seq):
    t, d = x2.shape
    dc = conv_g.shape[1]
    n_proj = w_in.shape[1] // dc
    assert w_in.shape == (d, n_proj * dc) and w_in.dtype == F32
    dv = dc // V_DIM * (V_DIM + ONES_ROWS)
    tm = TK
    tb = tm * INPROJ_SUBTILES
    nj = seq // tb
    row = lambda b, j: (b * nj + j, 0)
    const = lambda b, j: (0, 0)
    out_sds = jax.ShapeDtypeStruct((t, dc), BF16)
    return pl.pallas_call(
        functools.partial(_inproj_kernel, tm=tm, dc=dc),
        grid=(batch, nj),
        in_specs=[
            pl.BlockSpec((tb, d), row),
            pl.BlockSpec((1, d), const),
            pl.BlockSpec(memory_space=pl.ANY),
            pl.BlockSpec(conv_w.shape, const),
            pl.BlockSpec((1, dc), const),
            pl.BlockSpec((1, dc), const),
            pl.BlockSpec((1, dc), const),
            pl.BlockSpec(gmat.shape, const),
        ],
        out_specs=[
            pl.BlockSpec((tb, dc), row),
            pl.BlockSpec((dc, tb), lambda b, j: (0, b * nj + j)),
            pl.BlockSpec((tb, dc), row),
            pl.BlockSpec((INPROJ_SUBTILES, dv, tm), lambda b, j: (b * nj + j, 0, 0)),
        ],
        out_shape=[out_sds, jax.ShapeDtypeStruct((dc, t), BF16), out_sds,
                   jax.ShapeDtypeStruct((t // tm, dv, tm), BF16)],
        scratch_shapes=[pltpu.VMEM((8, dc), F32),
                        pltpu.VMEM((n_proj, d, dc), F32), pltpu.VMEM((d, n_proj * dc), BF16),
                        pltpu.SemaphoreType.DMA((n_proj,))],
        compiler_params=pltpu.CompilerParams(
            dimension_semantics=("arbitrary", "arbitrary"), vmem_limit_bytes=VMEM_LIMIT),
        name="inproj_conv_qknorm",
    )(x2, g, w_in, conv_w, conv_g, qg, kg, gmat)


def _attn_kernel(lp_ref, sg_ref, qT_ref, k_ref, vT_ref, o_ref, m_ref, acc_ref,
                 sa_ref, pb_ref, ab_ref, *, tq, tk, nq, lam_init):
    map_a = slice(0, tq)
    map_b = slice(tq, 2 * tq)

    lp = lp_ref[...]
    lam = (jnp.exp(jnp.sum(lp[0:1, :] * lp[1:2, :], axis=-1, keepdims=True))
           - jnp.exp(jnp.sum(lp[2:3, :] * lp[3:4, :], axis=-1, keepdims=True)) + lam_init)

    def stacked_queries(qi):
        qT = qT_ref[:, qi * tq:(qi + 1) * tq]
        row = lax.broadcasted_iota(jnp.int32, qT.shape, 0)
        zero = jnp.zeros_like(qT)
        return jnp.concatenate([jnp.where(row < HEAD_DIM, qT, zero),
                                jnp.where(row >= HEAD_DIM, qT, zero)], axis=1)

    def pieces(diag):
        return ((slice(0, tq // 2), tk // 2), (slice(tq // 2, tq), tk)) if diag else ((slice(0, tq), tk),)

    def shifted(cols, off):
        return slice(cols.start + off, cols.stop + off)

    def scores(j, qq, cols, n_keys):
        return _dot(k_ref[j * tk:j * tk + n_keys, :], qq[:, cols])

    def softmax(m, cols, s, q0):
        if q0 is not None:
            qrel = q0 + lax.broadcasted_iota(jnp.int32, s.shape, 1)
            krel = lax.broadcasted_iota(jnp.int32, s.shape, 0)
            s = jnp.where(krel <= qrel, s, -jnp.inf)
        sb = s.astype(BF16)
        m_old = m[:, cols]
        m_new = jnp.maximum(m_old, jnp.max(sb, axis=0, keepdims=True).astype(F32))
        alpha = jnp.exp2(m_old - m_new)
        p = jnp.exp2(sb - m_new.astype(BF16))
        m[:, cols] = m_new
        return alpha, p

    def accumulate(acc, cols, alpha, vb, p):
        acc[:, cols] = alpha * acc[:, cols] + _dot(vb, p)

    def prefetch_map_a(sa, j, qq, diag):
        for cols, n_keys in pieces(diag):
            sa[0:n_keys, cols] = scores(j, qq, cols, n_keys)

    stacked = {0: stacked_queries(0)}
    m_ref[0] = jnp.full(m_ref.shape[1:], -jnp.inf, F32)
    acc_ref[0] = jnp.zeros(acc_ref.shape[1:], F32)
    prefetch_map_a(sa_ref.at[0], 0, stacked[0], True)

    for qi in range(nq):
        par = qi % 2
        m, acc, sa, qq = m_ref.at[par], acc_ref.at[par], sa_ref.at[par], stacked[qi]
        for j in range(qi + 1):
            last = j == qi
            if j > 0:
                accumulate(acc, map_b, ab_ref[...], vT_ref[j - 1], pb_ref[...])
            s_b = [scores(j, qq, shifted(cols, tq), n_keys) for cols, n_keys in pieces(last)]
            for cols, n_keys in pieces(last):
                alpha_a, p_a = softmax(m, cols, sa[0:n_keys, cols], cols.start if last else None)
                accumulate(acc, cols, alpha_a, vT_ref[j, :, 0:n_keys], p_a)
            if not last:
                prefetch_map_a(sa, j + 1, qq, j + 1 == qi)
            elif qi + 1 < nq:
                stacked[qi + 1] = stacked_queries(qi + 1)
                m_ref[1 - par] = jnp.full(m_ref.shape[1:], -jnp.inf, F32)
                acc_ref[1 - par] = jnp.zeros(acc_ref.shape[1:], F32)
                prefetch_map_a(sa_ref.at[1 - par], 0, stacked[qi + 1], False)
            for (cols, n_keys), s in zip(pieces(last), s_b):
                alpha_b, p_b = softmax(m, shifted(cols, tq), s, cols.start if last else None)
                if last:
                    accumulate(acc, shifted(cols, tq), alpha_b, vT_ref[j, :, 0:n_keys], p_b)
                else:
                    ab_ref[...] = alpha_b
                    pb_ref[...] = p_b

        o = acc[0:V_DIM, :] / acc[V_DIM:V_DIM + 1, :]
        d = o[:, map_a] - lam * o[:, map_b]
        ms = jnp.mean(d * d, axis=0, keepdims=True)
        o_ref[qi * tq:(qi + 1) * tq, :] = (
            d * lax.rsqrt(ms + EPS) * sg_ref[...] * (1.0 - lam_init)).T.astype(BF16)
        del stacked[qi]


def _attention(lam_params, subln_g, qT, k, vT, *, batch, seq, lam_init):
    dq, t = qT.shape
    n_heads = dq // V_DIM
    tq, tk = TQ, TK
    assert tq == tk and vT.shape[2] == tk
    nq = seq // tq
    nk = seq // tk
    vrows = V_DIM + ONES_ROWS
    const = lambda b, h: (0, 0)
    return pl.pallas_call(
        functools.partial(_attn_kernel, tq=tq, tk=tk, nq=nq, lam_init=lam_init),
        grid=(batch, n_heads),
        in_specs=[
            pl.BlockSpec(lam_params.shape, const),
            pl.BlockSpec((V_DIM, 1), const),
            pl.BlockSpec((V_DIM, seq), lambda b, h: (h, b)),
            pl.BlockSpec((seq, V_DIM), lambda b, h: (b, h)),
            pl.BlockSpec((nk, vrows, tk), lambda b, h: (b, h, 0)),
        ],
        out_specs=pl.BlockSpec((seq, V_DIM), lambda b, h: (b, h)),
        out_shape=jax.ShapeDtypeStruct((t, dq), BF16),
        scratch_shapes=[pltpu.VMEM((2, 1, 2 * tq), F32),
                        pltpu.VMEM((2, vrows, 2 * tq), F32),
                        pltpu.VMEM((2, tk, tq), F32), pltpu.VMEM((tk, tq), BF16), pltpu.VMEM((1, tq), F32)],
        compiler_params=pltpu.CompilerParams(
            dimension_semantics=("arbitrary", "arbitrary"), vmem_limit_bytes=VMEM_LIMIT),
        name="diff_attention",
    )(lam_params, subln_g, qT, k, vT)


def _outproj_router_kernel(x_ref, yc_ref, at_ref, wo_hbm, g_ref, wr_ref,
                           h_ref, hn_ref, routeT_ref, cnt_ref, wst_ref, wo_ref, sem, *, tm, dc):
    i = pl.program_id(0)

    @pl.when(i == 0)
    def _():
        cnt_ref[...] = jnp.zeros_like(cnt_ref)
        weight_copy = pltpu.make_async_copy(wo_hbm, wst_ref, sem.at[0])
        weight_copy.start()
        weight_copy.wait()
        wo_ref[...] = wst_ref[...].astype(BF16)

    for r0 in range(0, x_ref.shape[0], tm):
        _outproj_router_subtile(slice(r0, r0 + tm), x_ref, yc_ref, at_ref, wo_ref, g_ref, wr_ref,
                                h_ref, hn_ref, routeT_ref, cnt_ref, tm=tm, dc=dc)


def _outproj_router_subtile(rows, x_ref, yc_ref, at_ref, wo_ref, g_ref, wr_ref,
                            h_ref, hn_ref, routeT_ref, cnt_ref, *, tm, dc):
    h = x_ref[rows, :] + _dot(yc_ref[rows, :], wo_ref[0:dc, :]) + _dot(at_ref[rows, :], wo_ref[dc:2 * dc, :])
    h_ref[rows, :] = h.astype(BF16)
    ms = jnp.mean(h * h, axis=-1, keepdims=True)
    hn = h * lax.rsqrt(ms + EPS) * g_ref[...]
    hi = hn.astype(BF16)
    hn_ref[rows, :] = _pack_rows(hn)
    lo = (hn - hi.astype(F32)).astype(BF16)
    prod = _dot(hi, wr_ref[...])
    logits = prod[:, 0:LANES] + prod[:, LANES:2 * LANES] + _dot(lo, wr_ref[:, 0:LANES])

    lt = logits.T
    neg = -jnp.inf
    grow = lax.broadcasted_iota(jnp.int32, (8, tm), 0).astype(F32)
    gl = jnp.where(grow < N_GROUPS, lt[N_EXPERTS:N_EXPERTS + 8, :], neg)
    gmax = jnp.max(gl, axis=0, keepdims=True)
    g_gate = 1.0 / jnp.sum(jnp.exp(gl - gmax), axis=0, keepdims=True)
    g_idx = jnp.min(jnp.where(gl == gmax, grow, 1e9), axis=0, keepdims=True)
    erow = lax.broadcasted_iota(jnp.int32, (N_EXPERTS, tm), 0).astype(F32)
    e_lo = g_idx * EXPERTS_PER_GROUP
    el = jnp.where((erow >= e_lo) & (erow < e_lo + EXPERTS_PER_GROUP), lt[0:N_EXPERTS, :], neg)
    v1 = jnp.max(el, axis=0, keepdims=True)
    i1 = jnp.min(jnp.where(el == v1, erow, 1e9), axis=0, keepdims=True)
    el2 = jnp.where(erow == i1, neg, el)
    v2 = jnp.max(el2, axis=0, keepdims=True)
    i2 = jnp.min(jnp.where(el2 == v2, erow, 1e9), axis=0, keepdims=True)
    tt = jnp.exp(v2 - v1)
    w1 = g_gate / (1.0 + tt)
    w2 = g_gate * tt / (1.0 + tt)

    sel1 = erow == i1
    sel2 = erow == i2
    oh = jnp.where(sel1 | sel2, 1.0, 0.0)
    ss = lax.broadcasted_iota(jnp.int32, (tm, tm), 0)
    tt_i = lax.broadcasted_iota(jnp.int32, (tm, tm), 1)
    earlier = jnp.where(ss < tt_i, 1.0, 0.0).astype(BF16)
    ranks = _dot(oh.astype(BF16), earlier) + cnt_ref[...]
    r1 = jnp.sum(jnp.where(sel1, ranks, 0.0), axis=0, keepdims=True)
    r2 = jnp.sum(jnp.where(sel2, ranks, 0.0), axis=0, keepdims=True)
    cnt_ref[...] = cnt_ref[...] + jnp.sum(oh, axis=1, keepdims=True)

    routeT = jnp.concatenate([i1, i2, r1, r2, w1, w2, jnp.zeros((2, tm), F32)], axis=0)
    routeT_ref[:, rows] = routeT


def _outproj_router(x2, yc, at, w_out, g, wr_cat):
    t, d = x2.shape
    dc = yc.shape[1]
    tm = TM_PROJ
    tb = tm * OUTPROJ_SUBTILES
    row = lambda i: (i, 0)
    const = lambda i: (0, 0)
    return pl.pallas_call(
        functools.partial(_outproj_router_kernel, tm=tm, dc=dc),
        grid=(t // tb,),
        in_specs=[
            pl.BlockSpec((tb, d), row),
            pl.BlockSpec((tb, dc), row),
            pl.BlockSpec((tb, dc), row),
            pl.BlockSpec(memory_space=pl.ANY),
            pl.BlockSpec((1, d), const),
            pl.BlockSpec(wr_cat.shape, const),
        ],
        out_specs=[
            pl.BlockSpec((tb, d), row),
            pl.BlockSpec((tb, d // 2), row),
            pl.BlockSpec((8, tb), lambda i: (0, i)),
            pl.BlockSpec((N_EXPERTS, 1), const),
        ],
        out_shape=[
            jax.ShapeDtypeStruct((t, d), BF16),
            jax.ShapeDtypeStruct((t, d // 2), jnp.int32),
            jax.ShapeDtypeStruct((8, t), F32),
            jax.ShapeDtypeStruct((N_EXPERTS, 1), F32),
        ],
        scratch_shapes=[pltpu.VMEM(w_out.shape, F32), pltpu.VMEM(w_out.shape, BF16),
                        pltpu.SemaphoreType.DMA((1,))],
        compiler_params=pltpu.CompilerParams(
            dimension_semantics=("arbitrary",), vmem_limit_bytes=VMEM_LIMIT),
        name="outproj_router",
    )(x2, yc, at, w_out, g, wr_cat)


def _plan_kernel(cnt_ref, rt_ref, pos_ref, te_ref, nx_ref, sl_ref, nt_ref, *, tm):
    shift = tm.bit_length() - 1
    tiles, starts = [], []
    total = jnp.int32(0)
    for e in range(N_EXPERTS):
        n = lax.shift_right_logical(cnt_ref[e] + (tm - 1), shift)
        tiles.append(n)
        starts.append(total)
        total = total + n
    nt_ref[0] = total

    rt = rt_ref[...]
    ea, eb = rt[0:1, :], rt[1:2, :]
    sa = jnp.zeros_like(ea)
    sb = jnp.zeros_like(eb)
    for e in range(N_EXPERTS):
        start = (starts[e] * tm).astype(F32)
        sa = jnp.where(ea == e, start, sa)
        sb = jnp.where(eb == e, start, sb)
    pos_ref[0:1, :] = (sa + rt[2:3, :]).astype(jnp.int32)
    pos_ref[1:2, :] = (sb + rt[3:4, :]).astype(jnp.int32)

    def clear(i, c):
        te_ref[i] = 0
        nx_ref[i] = -1
        sl_ref[i] = 0
        return c

    lax.fori_loop(0, te_ref.shape[0], clear, 0)

    nxt = jnp.int32(-1)
    next_of = [None] * N_EXPERTS
    for e in reversed(range(N_EXPERTS)):
        next_of[e] = nxt
        nxt = jnp.where(tiles[e] > 0, e, nxt)
    ordinal = jnp.int32(0)
    for e in range(N_EXPERTS):
        slot = ordinal & 1

        def fill(j, c, e=e, slot=slot):
            te_ref[starts[e] + j] = e
            nx_ref[starts[e] + j] = next_of[e]
            sl_ref[starts[e] + j] = slot
            return c

        lax.fori_loop(0, tiles[e], fill, 0)
        ordinal = ordinal + (tiles[e] > 0).astype(jnp.int32)


def _plan(counts, routeT, n_tiles_max):
    t = routeT.shape[1]
    smem = pl.BlockSpec(memory_space=pltpu.SMEM)
    table = jax.ShapeDtypeStruct((n_tiles_max,), jnp.int32)
    return pl.pallas_call(
        functools.partial(_plan_kernel, tm=TM_MOE),
        grid_spec=pltpu.PrefetchScalarGridSpec(
            num_scalar_prefetch=1, grid=(1,),
            in_specs=[pl.BlockSpec(routeT.shape, lambda i, cnt: (0, 0))],
            out_specs=[pl.BlockSpec((2, t), lambda i, cnt: (0, 0)), smem, smem, smem, smem],
        ),
        out_shape=[jax.ShapeDtypeStruct((2, t), jnp.int32), table, table, table,
                   jax.ShapeDtypeStruct((1,), jnp.int32)],
        name="routing_plan",
    )(counts, routeT)


def _moe_kernel(te_ref, nt_ref, nx_ref, sl_ref, x_ref, wg_hbm, wu_hbm, wd_hbm, y_ref,
                wg_st, wu_st, wd_st, wgb_ref, wub_ref, wdb_ref, sem, *, tm, tiles_per_step):
    last = nt_ref[0] - 1

    def weight_copies(e, s):
        return (pltpu.make_async_copy(wg_hbm.at[e], wg_st.at[s], sem.at[s, 0]),
                pltpu.make_async_copy(wu_hbm.at[e], wu_st.at[s], sem.at[s, 1]),
                pltpu.make_async_copy(wd_hbm.at[e], wd_st.at[s], sem.at[s, 2]))

    for u in range(tiles_per_step):
        i = pl.program_id(0) * tiles_per_step + u
        rows = slice(u * tm, (u + 1) * tm)
        ic = jnp.minimum(i, last)
        expert = te_ref[ic]
        slot = sl_ref[ic]
        first_of_expert = (i == 0) | ((i <= last) & (expert != te_ref[jnp.maximum(ic - 1, 0)]))

        if u == 0:
            @pl.when(i == 0)
            def _():
                for c in weight_copies(expert, slot):
                    c.start()

        def expert_mlp(rows, new_weights):
            x_l, x_r = _unpack_rows(x_ref[rows, :])
            x_l = x_l.astype(BF16)
            x_r = x_r.astype(BF16)
            half = x_l.shape[1]
            if new_weights:
                wgb_ref[...] = wg_st[slot].astype(BF16)
            hg = _dot(x_l, wgb_ref[0:half, :]) + _dot(x_r, wgb_ref[half:2 * half, :])
            if new_weights:
                wub_ref[...] = wu_st[slot].astype(BF16)
            hu = _dot(x_l, wub_ref[0:half, :]) + _dot(x_r, wub_ref[half:2 * half, :])
            if new_weights:
                wdb_ref[...] = wd_st[slot].astype(BF16)
            act = hg * (1.0 / (1.0 + jnp.exp(-hg))) * hu
            y_ref[rows, :] = _pack_rows(_dot(act.astype(BF16), wdb_ref[...]))

        @pl.when(first_of_expert)
        def _():
            for c in weight_copies(expert, slot):
                c.wait()
            nxt = nx_ref[ic]

            @pl.when(nxt >= 0)
            def _():
                for c in weight_copies(nxt, 1 - slot):
                    c.start()

            expert_mlp(rows, True)

        @pl.when((i <= last) & jnp.logical_not(first_of_expert))
        def _():
            expert_mlp(rows, False)


def _moe(tile_expert, n_tiles, next_expert, stage_slot, xs, wg, wu, wd):
    p, dp = xs.shape
    d, f = wg.shape[1], wg.shape[2]
    assert dp * 2 == d
    tm, tps = TM_MOE, MOE_TILES_PER_STEP
    tb = tm * tps
    row = lambda s, te, nt, nx, sl: (jnp.minimum(s, (nt[0] - 1) // tps), 0)
    hbm = pl.BlockSpec(memory_space=pl.ANY)
    grid_spec = pltpu.PrefetchScalarGridSpec(
        num_scalar_prefetch=4,
        grid=(p // tb,),
        in_specs=[pl.BlockSpec((tb, dp), row), hbm, hbm, hbm],
        out_specs=pl.BlockSpec((tb, dp), row),
        scratch_shapes=[pltpu.VMEM((2, d, f), F32), pltpu.VMEM((2, d, f), F32), pltpu.VMEM((2, f, d), F32),
                        pltpu.VMEM((d, f), BF16), pltpu.VMEM((d, f), BF16), pltpu.VMEM((f, d), BF16),
                        pltpu.SemaphoreType.DMA((2, 3))],
    )
    return pl.pallas_call(
        functools.partial(_moe_kernel, tm=tm, tiles_per_step=tps),
        grid_spec=grid_spec,
        out_shape=jax.ShapeDtypeStruct((p, dp), jnp.int32),
        compiler_params=pltpu.CompilerParams(
            dimension_semantics=("arbitrary",), vmem_limit_bytes=VMEM_LIMIT),
        name="moe_experts",
    )(tile_expert, n_tiles, next_expert, stage_slot, xs, wg, wu, wd)


def _sc_mesh():
    return plsc.VectorSubcoreMesh(core_axis_name="c", subcore_axis_name="s",
                                  num_cores=SC_CORES, num_subcores=SC_SUBCORES)


def _sc_dispatch(rows, pos, n_out):
    t, d = rows.shape
    win = pos.shape[1]
    assert pos.shape[0] * win == 2 * t

    @functools.partial(pl.kernel, out_type=jax.ShapeDtypeStruct((n_out, d), rows.dtype),
                       mesh=_sc_mesh(), scratch_types=[], name="sc_dispatch")
    def run(rows_hbm, pos_hbm, out_hbm):
        def body(rows_vmem, pa_vmem, pb_vmem):
            pltpu.sync_copy(rows_vmem, out_hbm.at[pa_vmem.at[0]])
            pltpu.sync_copy(rows_vmem, out_hbm.at[pb_vmem.at[0]])

        pltpu.emit_pipeline(
            body, grid=(t // win,),
            in_specs=[pl.BlockSpec((win, d), lambda i: (i, 0)),
                      pl.BlockSpec((1, win), lambda i: (i, 0)),
                      pl.BlockSpec((1, win), lambda i: (i + t // win, 0))],
            out_specs=[],
            core_axis_name=("c", "s"),
            dimension_semantics=(pltpu.PARALLEL,),
        )(rows_hbm, pos_hbm, pos_hbm)

    return run(rows, pos)


def _sc_gather(table, idx):
    d = table.shape[1]
    n_win, win = idx.shape

    @functools.partial(pl.kernel, out_type=jax.ShapeDtypeStruct((n_win * win, d), table.dtype),
                       mesh=_sc_mesh(), scratch_types=[], name="sc_gather")
    def run(table_hbm, idx_hbm, out_hbm):
        def body(idx_vmem, out_vmem):
            pltpu.sync_copy(table_hbm.at[idx_vmem.at[0]], out_vmem)

        pltpu.emit_pipeline(
            body, grid=(n_win,),
            in_specs=[pl.BlockSpec((1, win), lambda i: (i, 0))],
            out_specs=[pl.BlockSpec((win, d), lambda i: (i, 0))],
            core_axis_name=("c", "s"),
            dimension_semantics=(pltpu.PARALLEL,),
        )(idx_hbm, out_hbm)

    return run(table, idx)


def _combine_kernel(h_hbm, y_hbm, rt_ref, o_ref, h_buf, ya_buf, yb_buf, sem, *, tm, nb):
    s = pl.program_id(0)
    depth = h_buf.shape[0]

    def copies(step, slot):
        return (pltpu.make_async_copy(h_hbm.at[pl.ds(step * tm, tm)], h_buf.at[slot], sem.at[slot, 0]),
                pltpu.make_async_copy(y_hbm.at[pl.ds(step * tm, tm)], ya_buf.at[slot], sem.at[slot, 1]),
                pltpu.make_async_copy(y_hbm.at[pl.ds((step + nb) * tm, tm)], yb_buf.at[slot], sem.at[slot, 2]))

    @pl.when(s == 0)
    def _():
        for k in range(depth - 1):
            for c in copies(k, k):
                c.start()

    ahead = s + depth - 1

    @pl.when(ahead < nb)
    def _():
        for c in copies(ahead, ahead % depth):
            c.start()

    slot = s % depth
    for c in copies(s, slot):
        c.wait()

    rt = rt_ref[...]
    r = jnp.concatenate([rt, jnp.zeros((LANES - rt.shape[0], rt.shape[1]), F32)], axis=0).T
    wa, wb = r[:, 4:5], r[:, 5:6]
    a_l, a_r = _unpack_rows(ya_buf[slot])
    b_l, b_r = _unpack_rows(yb_buf[slot])
    half = a_l.shape[1]
    h = h_buf[slot]
    o_ref[:, 0:half] = h[:, 0:half].astype(F32) + wa * a_l + wb * b_l
    o_ref[:, half:2 * half] = h[:, half:2 * half].astype(F32) + wa * a_r + wb * b_r


def _combine(hres, yg, routeT):
    t, d = hres.shape
    tm = TM_COMBINE
    nb = t // tm
    ring = COMBINE_RING
    assert nb >= ring - 1 and yg.shape == (2 * t, d // 2)
    hbm = pl.BlockSpec(memory_space=pl.ANY)
    return pl.pallas_call(
        functools.partial(_combine_kernel, tm=tm, nb=nb),
        grid=(nb,),
        in_specs=[hbm, hbm, pl.BlockSpec((routeT.shape[0], tm), lambda i: (0, i))],
        out_specs=pl.BlockSpec((tm, d), lambda i: (i, 0)),
        out_shape=jax.ShapeDtypeStruct((t, d), F32),
        scratch_shapes=[pltpu.VMEM((ring, tm, d), hres.dtype),
                        pltpu.VMEM((ring, tm, d // 2), yg.dtype), pltpu.VMEM((ring, tm, d // 2), yg.dtype),
                        pltpu.SemaphoreType.DMA((ring, 3))],
        compiler_params=pltpu.CompilerParams(
            dimension_semantics=("arbitrary",), vmem_limit_bytes=VMEM_LIMIT),
        name="combine",
    )(hres, yg, routeT)


def _lambda_init(layer_idx):
    return 0.8 - 0.6 * math.exp(-0.3 * layer_idx)


def _layer(h, l, attn_norm_g, w_in, conv_w, conv_out_g, q_norm_g, k_norm_g,
           lambda_q1, lambda_k1, lambda_q2, lambda_k2, attn_subln_g, w_out,
           ffn_norm_g, w_router_group, w_router_expert, w_exp_gate, w_exp_up, w_exp_down):
    batch, seq, d = h.shape
    t = batch * seq
    dc = conv_w.shape[-1]
    lam_init = _lambda_init(l)
    x2 = h.reshape(t, d)

    reps = dc // HEAD_DIM
    assert dc // CONV_GROUPS == HEAD_DIM
    qg = (jnp.tile(q_norm_g[l], reps) * (HEAD_DIM ** -0.5 * math.log2(math.e))).reshape(1, dc)
    kg = jnp.tile(k_norm_g[l], reps).reshape(1, dc)
    grp = jnp.arange(MXU_TILE) // HEAD_DIM
    gmat = jnp.where(grp[:, None] == grp[None, :], 1.0 / HEAD_DIM, 0.0).astype(BF16)
    yc, qT, k, vT = _inproj(x2, attn_norm_g[l].reshape(1, d), w_in[l], conv_w[l],
                          conv_out_g[l].reshape(1, dc), qg, kg, gmat, batch=batch, seq=seq)

    lam_params = jnp.stack([lambda_q1[l], lambda_k1[l], lambda_q2[l], lambda_k2[l]])
    at = _attention(lam_params, attn_subln_g[l].reshape(V_DIM, 1), qT, k, vT,
                    batch=batch, seq=seq, lam_init=lam_init)

    wr = jnp.concatenate([w_router_expert[l], w_router_group[l],
                          jnp.zeros((d, LANES - N_EXPERTS - N_GROUPS), F32)], axis=1)
    wr_hi = wr.astype(BF16)
    wr_lo = (wr - wr_hi.astype(F32)).astype(BF16)
    hres, hn2, routeT, cnt = _outproj_router(x2, yc, at, w_out[l],
                                                    ffn_norm_g[l].reshape(1, d),
                                                    jnp.concatenate([wr_hi, wr_lo], axis=1))

    tmm = TM_MOE
    n_tiles_max = (2 * t) // tmm + N_EXPERTS
    p_rows = n_tiles_max * tmm
    pos, tile_expert, next_expert, stage_slot, n_tiles = _plan(
        cnt[:, 0].astype(jnp.int32), routeT, n_tiles_max)

    posw = pos.reshape(2 * t // SC_WIN, SC_WIN)
    xs = _sc_dispatch(hn2, posw, p_rows)

    f = w_exp_gate.shape[-1]
    ys = _moe(tile_expert, n_tiles, next_expert, stage_slot, xs,
              w_exp_gate[l].reshape(N_EXPERTS, d, f),
              w_exp_up[l].reshape(N_EXPERTS, d, f),
              w_exp_down[l].reshape(N_EXPERTS, f, d))
    yg = _sc_gather(ys, posw)
    out = _combine(hres, yg, routeT)
    return out.reshape(batch, seq, d)


def kernel(x, attn_norm_g, w_in, conv_w, conv_out_g, q_norm_g, k_norm_g, lambda_q1, lambda_k1,
           lambda_q2, lambda_k2, attn_subln_g, w_out, ffn_norm_g, w_router_group, w_router_expert,
           w_exp_gate, w_exp_up, w_exp_down):
    h = x
    for l in range(attn_norm_g.shape[0]):
        h = _layer(h, l, attn_norm_g, w_in, conv_w, conv_out_g, q_norm_g, k_norm_g,
                   lambda_q1, lambda_k1, lambda_q2, lambda_k2, attn_subln_g, w_out,
                   ffn_norm_g, w_router_group, w_router_expert, w_exp_gate, w_exp_up, w_exp_down)
    return h
```

```python
import functools
import math

import jax
import jax.numpy as jnp
from jax import lax
from jax.experimental import pallas as pl
from jax.experimental.pallas import tpu as pltpu
from jax.experimental.pallas import tpu_sc as plsc

F32 = jnp.float32
BF16 = jnp.bfloat16

HEAD_DIM = 64
V_DIM = 2 * HEAD_DIM
CONV_GROUPS = 8
N_GROUPS = 4
EXPERTS_PER_GROUP = 8
N_EXPERTS = N_GROUPS * EXPERTS_PER_GROUP
EPS = 1e-6
LANES = 128
MXU_TILE = 256
ONES_ROWS = 16
VMEM_LIMIT = 48 * 1024 * 1024

TM_PROJ = 256
TM_COMBINE = 1024
COMBINE_RING = 3
OUTPROJ_SUBTILES = 4
INPROJ_SUBTILES = 2
TQ = 512
TK = 512
TM_MOE = 512
MOE_TILES_PER_STEP = 2
SC_CORES = 2
SC_SUBCORES = 16
SC_WIN = 64


def _dot(a, b):
    return jnp.dot(a, b, preferred_element_type=F32)


def _pack_rows(x):
    w = x.shape[1] // 2
    bits = lax.bitcast_convert_type(x.astype(BF16).astype(F32), jnp.uint32)
    return lax.bitcast_convert_type((bits[:, :w] >> 16) | bits[:, w:], jnp.int32)


def _unpack_rows(packed):
    bits = lax.bitcast_convert_type(packed, jnp.uint32)
    left = lax.bitcast_convert_type(bits << 16, F32)
    right = lax.bitcast_convert_type(bits & jnp.uint32(0xFFFF0000), F32)
    return left, right


def _group_mean(sq, gm):
    w = gm.shape[0]
    sq = sq.astype(BF16)
    return jnp.concatenate([_dot(sq[:, c:c + w], gm) for c in range(0, sq.shape[1], w)], axis=1)


def _inproj_kernel(x_ref, g_ref, w_hbm, cw_ref, cg_ref, qg_ref, kg_ref, gm_ref,
                   yc_ref, qT_ref, k_ref, vT_ref, carry_ref, wst_ref, w_ref, sem, *, tm, dc):
    j = pl.program_id(1)

    def weight_copy(s):
        return pltpu.make_async_copy(w_hbm.at[:, s * dc:(s + 1) * dc], wst_ref.at[s], sem.at[s])

    @pl.when((pl.program_id(0) == 0) & (j == 0))
    def _():
        for s in range(wst_ref.shape[0]):
            weight_copy(s).start()
        for s in range(wst_ref.shape[0]):
            weight_copy(s).wait()
            w_ref[:, s * dc:(s + 1) * dc] = wst_ref[s].astype(BF16)

    @pl.when(j == 0)
    def _():
        carry_ref[...] = jnp.zeros_like(carry_ref)

    for u in range(x_ref.shape[0] // tm):
        _inproj_subtile(u, x_ref, g_ref, w_ref, cw_ref, cg_ref, qg_ref, kg_ref, gm_ref,
                        yc_ref, qT_ref, k_ref, vT_ref, carry_ref, tm=tm, dc=dc)


def _inproj_subtile(u, x_ref, g_ref, w_ref, cw_ref, cg_ref, qg_ref, kg_ref, gm_ref,
                    yc_ref, qT_ref, k_ref, vT_ref, carry_ref, *, tm, dc):
    rows = slice(u * tm, (u + 1) * tm)
    x = x_ref[rows, :]
    ms = jnp.mean(x * x, axis=-1, keepdims=True)
    hn = (x * lax.rsqrt(ms + EPS) * g_ref[...]).astype(BF16)

    def proj(s):
        return _dot(hn, w_ref[:, s * dc:(s + 1) * dc])

    gm = gm_ref[...]

    c = proj(2) * proj(0)
    prev = carry_ref[...]
    r = lax.broadcasted_iota(jnp.int32, c.shape, 0)
    c1 = jnp.where(r == 0, prev[7:8, :], pltpu.roll(c, 1, 0))
    c2 = jnp.where(r == 0, prev[6:7, :], jnp.where(r == 1, prev[7:8, :], pltpu.roll(c, 2, 0)))
    carry_ref[...] = c[tm - 8:tm, :]
    cw = cw_ref[...]
    y = proj(1) * (cw[0:1, :] * c2 + cw[1:2, :] * c1 + cw[2:3, :] * c)
    yc_ref[rows, :] = (y * lax.rsqrt(_group_mean(y * y, gm) + EPS) * cg_ref[...]).astype(BF16)

    q = proj(3)
    qT_ref[:, rows] = (q * lax.rsqrt(_group_mean(q * q, gm) + EPS) * qg_ref[...]).T.astype(BF16)
    k = proj(4)
    k_ref[rows, :] = (k * lax.rsqrt(_group_mean(k * k, gm) + EPS) * kg_ref[...]).astype(BF16)
    vt = proj(5).T.astype(BF16)
    vrows = V_DIM + ONES_ROWS
    for h in range(dc // V_DIM):
        vT_ref[u, h * vrows:h * vrows + V_DIM, :] = vt[h * V_DIM:(h + 1) * V_DIM, :]
        vT_ref[u, h * vrows + V_DIM:(h + 1) * vrows, :] = jnp.ones((ONES_ROWS, tm), BF16)


def _inproj(x2, g, w_in, conv_w, conv_g, qg, kg, gmat, *, batch, seq):
    t, d = x2.shape
    dc = conv_g.shape[1]
    n_proj = w_in.shape[1] // dc
    assert w_in.shape == (d, n_proj * dc) and w_in.dtype == F32
    dv = dc // V_DIM * (V_DIM + ONES_ROWS)
    tm = TK
    tb = tm * INPROJ_SUBTILES
    nj = seq // tb
    row = lambda b, j: (b * nj + j, 0)
    const = lambda b, j: (0, 0)
    out_sds = jax.ShapeDtypeStruct((t, dc), BF16)
    return pl.pallas_call(
        functools.partial(_inproj_kernel, tm=tm, dc=dc),
        grid=(batch, nj),
        in_specs=[
            pl.BlockSpec((tb, d), row),
            pl.BlockSpec((1, d), const),
            pl.BlockSpec(memory_space=pl.ANY),
            pl.BlockSpec(conv_w.shape, const),
            pl.BlockSpec((1, dc), const),
            pl.BlockSpec((1, dc), const),
            pl.BlockSpec((1, dc), const),
            pl.BlockSpec(gmat.shape, const),
        ],
        out_specs=[
            pl.BlockSpec((tb, dc), row),
            pl.BlockSpec((dc, tb), lambda b, j: (0, b * nj + j)),
            pl.BlockSpec((tb, dc), row),
            pl.BlockSpec((INPROJ_SUBTILES, dv, tm), lambda b, j: (b * nj + j, 0, 0)),
        ],
        out_shape=[out_sds, jax.ShapeDtypeStruct((dc, t), BF16), out_sds,
                   jax.ShapeDtypeStruct((t // tm, dv, tm), BF16)],
        scratch_shapes=[pltpu.VMEM((8, dc), F32),
                        pltpu.VMEM((n_proj, d, dc), F32), pltpu.VMEM((d, n_proj * dc), BF16),
                        pltpu.SemaphoreType.DMA((n_proj,))],
        compiler_params=pltpu.CompilerParams(
            dimension_semantics=("arbitrary", "arbitrary"), vmem_limit_bytes=VMEM_LIMIT),
        name="inproj_conv_qknorm",
    )(x2, g, w_in, conv_w, conv_g, qg, kg, gmat)


def _attn_kernel(lp_ref, sg_ref, qT_ref, k_ref, vT_ref, o_ref, m_ref, acc_ref,
                 sa_ref, pb_ref, ab_ref, *, tq, tk, nq, lam_init):
    map_a = slice(0, tq)
    map_b = slice(tq, 2 * tq)

    lp = lp_ref[...]
    lam = (jnp.exp(jnp.sum(lp[0:1, :] * lp[1:2, :], axis=-1, keepdims=True))
           - jnp.exp(jnp.sum(lp[2:3, :] * lp[3:4, :], axis=-1, keepdims=True)) + lam_init)

    def stacked_queries(qi):
        qT = qT_ref[:, qi * tq:(qi + 1) * tq]
        row = lax.broadcasted_iota(jnp.int32, qT.shape, 0)
        zero = jnp.zeros_like(qT)
        return jnp.concatenate([jnp.where(row < HEAD_DIM, qT, zero),
                                jnp.where(row >= HEAD_DIM, qT, zero)], axis=1)

    def pieces(diag):
        return ((slice(0, tq // 2), tk // 2), (slice(tq // 2, tq), tk)) if diag else ((slice(0, tq), tk),)

    def shifted(cols, off):
        return slice(cols.start + off, cols.stop + off)

    def scores(j, qq, cols, n_keys):
        return _dot(k_ref[j * tk:j * tk + n_keys, :], qq[:, cols])

    def softmax(m, cols, s, q0):
        if q0 is not None:
            qrel = q0 + lax.broadcasted_iota(jnp.int32, s.shape, 1)
            krel = lax.broadcasted_iota(jnp.int32, s.shape, 0)
            s = jnp.where(krel <= qrel, s, -jnp.inf)
        sb = s.astype(BF16)
        m_old = m[:, cols]
        m_new = jnp.maximum(m_old, jnp.max(sb, axis=0, keepdims=True).astype(F32))
        alpha = jnp.exp2(m_old - m_new)
        p = jnp.exp2(sb - m_new.astype(BF16))
        m[:, cols] = m_new
        return alpha, p

    def accumulate(acc, cols, alpha, vb, p):
        acc[:, cols] = alpha * acc[:, cols] + _dot(vb, p)

    def prefetch_map_a(sa, j, qq, diag):
        for cols, n_keys in pieces(diag):
            sa[0:n_keys, cols] = scores(j, qq, cols, n_keys)

    stacked = {0: stacked_queries(0)}
    m_ref[0] = jnp.full(m_ref.shape[1:], -jnp.inf, F32)
    acc_ref[0] = jnp.zeros(acc_ref.shape[1:], F32)
    prefetch_map_a(sa_ref.at[0], 0, stacked[0], True)

    for qi in range(nq):
        par = qi % 2
        m, acc, sa, qq = m_ref.at[par], acc_ref.at[par], sa_ref.at[par], stacked[qi]
        for j in range(qi + 1):
            last = j == qi
            if j > 0:
                accumulate(acc, map_b, ab_ref[...], vT_ref[j - 1], pb_ref[...])
            s_b = [scores(j, qq, shifted(cols, tq), n_keys) for cols, n_keys in pieces(last)]
            for cols, n_keys in pieces(last):
                alpha_a, p_a = softmax(m, cols, sa[0:n_keys, cols], cols.start if last else None)
                accumulate(acc, cols, alpha_a, vT_ref[j, :, 0:n_keys], p_a)
            if not last:
                prefetch_map_a(sa, j + 1, qq, j + 1 == qi)
            elif qi + 1 < nq:
                stacked[qi + 1] = stacked_queries(qi + 1)
                m_ref[1 - par] = jnp.full(m_ref.shape[1:], -jnp.inf, F32)
                acc_ref[1 - par] = jnp.zeros(acc_ref.shape[1:], F32)
                prefetch_map_a(sa_ref.at[1 - par], 0, stacked[qi + 1], False)
            for (cols, n_keys), s in zip(pieces(last), s_b):
                alpha_b, p_b = softmax(m, shifted(cols, tq), s, cols.start if last else None)
                if last:
                    accumulate(acc, shifted(cols, tq), alpha_b, vT_ref[j, :, 0:n_keys], p_b)
                else:
                    ab_ref[...] = alpha_b
                    pb_ref[...] = p_b

        o = acc[0:V_DIM, :] / acc[V_DIM:V_DIM + 1, :]
        d = o[:, map_a] - lam * o[:, map_b]
        ms = jnp.mean(d * d, axis=0, keepdims=True)
        o_ref[qi * tq:(qi + 1) * tq, :] = (
            d * lax.rsqrt(ms + EPS) * sg_ref[...] * (1.0 - lam_init)).T.astype(BF16)
        del stacked[qi]


def _attention(lam_params, subln_g, qT, k, vT, *, batch, seq, lam_init):
    dq, t = qT.shape
    n_heads = dq // V_DIM
    tq, tk = TQ, TK
    assert tq == tk and vT.shape[2] == tk
    nq = seq // tq
    nk = seq // tk
    vrows = V_DIM + ONES_ROWS
    const = lambda b, h: (0, 0)
    return pl.pallas_call(
        functools.partial(_attn_kernel, tq=tq, tk=tk, nq=nq, lam_init=lam_init),
        grid=(batch, n_heads),
        in_specs=[
            pl.BlockSpec(lam_params.shape, const),
            pl.BlockSpec((V_DIM, 1), const),
            pl.BlockSpec((V_DIM, seq), lambda b, h: (h, b)),
            pl.BlockSpec((seq, V_DIM), lambda b, h: (b, h)),
            pl.BlockSpec((nk, vrows, tk), lambda b, h: (b, h, 0)),
        ],
        out_specs=pl.BlockSpec((seq, V_DIM), lambda b, h: (b, h)),
        out_shape=jax.ShapeDtypeStruct((t, dq), BF16),
        scratch_shapes=[pltpu.VMEM((2, 1, 2 * tq), F32),
                        pltpu.VMEM((2, vrows, 2 * tq), F32),
                        pltpu.VMEM((2, tk, tq), F32), pltpu.VMEM((tk, tq), BF16), pltpu.VMEM((1, tq), F32)],
        compiler_params=pltpu.CompilerParams(
            dimension_semantics=("arbitrary", "arbitrary"), vmem_limit_bytes=VMEM_LIMIT),
        name="diff_attention",
    )(lam_params, subln_g, qT, k, vT)


def _outproj_router_kernel(x_ref, yc_ref, at_ref, wo_hbm, g_ref, wr_ref,
                           h_ref, hn_ref, routeT_ref, cnt_ref, wst_ref, wo_ref, sem, *, tm, dc):
    i = pl.program_id(0)

    @pl.when(i == 0)
    def _():
        cnt_ref[...] = jnp.zeros_like(cnt_ref)
        weight_copy = pltpu.make_async_copy(wo_hbm, wst_ref, sem.at[0])
        weight_copy.start()
        weight_copy.wait()
        wo_ref[...] = wst_ref[...].astype(BF16)

    for r0 in range(0, x_ref.shape[0], tm):
        _outproj_router_subtile(slice(r0, r0 + tm), x_ref, yc_ref, at_ref, wo_ref, g_ref, wr_ref,
                                h_ref, hn_ref, routeT_ref, cnt_ref, tm=tm, dc=dc)


def _outproj_router_subtile(rows, x_ref, yc_ref, at_ref, wo_ref, g_ref, wr_ref,
                            h_ref, hn_ref, routeT_ref, cnt_ref, *, tm, dc):
    h = x_ref[rows, :] + _dot(yc_ref[rows, :], wo_ref[0:dc, :]) + _dot(at_ref[rows, :], wo_ref[dc:2 * dc, :])
    h_ref[rows, :] = h.astype(BF16)
    ms = jnp.mean(h * h, axis=-1, keepdims=True)
    hn = h * lax.rsqrt(ms + EPS) * g_ref[...]
    hi = hn.astype(BF16)
    hn_ref[rows, :] = _pack_rows(hn)
    lo = (hn - hi.astype(F32)).astype(BF16)
    prod = _dot(hi, wr_ref[...])
    logits = prod[:, 0:LANES] + prod[:, LANES:2 * LANES] + _dot(lo, wr_ref[:, 0:LANES])

    lt = logits.T
    neg = -jnp.inf
    grow = lax.broadcasted_iota(jnp.int32, (8, tm), 0).astype(F32)
    gl = jnp.where(grow < N_GROUPS, lt[N_EXPERTS:N_EXPERTS + 8, :], neg)
    gmax = jnp.max(gl, axis=0, keepdims=True)
    g_gate = 1.0 / jnp.sum(jnp.exp(gl - gmax), axis=0, keepdims=True)
    g_idx = jnp.min(jnp.where(gl == gmax, grow, 1e9), axis=0, keepdims=True)
    erow = lax.broadcasted_iota(jnp.int32, (N_EXPERTS, tm), 0).astype(F32)
    e_lo = g_idx * EXPERTS_PER_GROUP
    el = jnp.where((erow >= e_lo) & (erow < e_lo + EXPERTS_PER_GROUP), lt[0:N_EXPERTS, :], neg)
    v1 = jnp.max(el, axis=0, keepdims=True)
    i1 = jnp.min(jnp.where(el == v1, erow, 1e9), axis=0, keepdims=True)
    el2 = jnp.where(erow == i1, neg, el)
    v2 = jnp.max(el2, axis=0, keepdims=True)
    i2 = jnp.min(jnp.where(el2 == v2, erow, 1e9), axis=0, keepdims=True)
    tt = jnp.exp(v2 - v1)
    w1 = g_gate / (1.0 + tt)
    w2 = g_gate * tt / (1.0 + tt)

    sel1 = erow == i1
    sel2 = erow == i2
    oh = jnp.where(sel1 | sel2, 1.0, 0.0)
    ss = lax.broadcasted_iota(jnp.int32, (tm, tm), 0)
    tt_i = lax.broadcasted_iota(jnp.int32, (tm, tm), 1)
    earlier = jnp.where(ss < tt_i, 1.0, 0.0).astype(BF16)
    ranks = _dot(oh.astype(BF16), earlier) + cnt_ref[...]
    r1 = jnp.sum(jnp.where(sel1, ranks, 0.0), axis=0, keepdims=True)
    r2 = jnp.sum(jnp.where(sel2, ranks, 0.0), axis=0, keepdims=True)
    cnt_ref[...] = cnt_ref[...] + jnp.sum(oh, axis=1, keepdims=True)

    routeT = jnp.concatenate([i1, i2, r1, r2, w1, w2, jnp.zeros((2, tm), F32)], axis=0)
    routeT_ref[:, rows] = routeT


def _outproj_router(x2, yc, at, w_out, g, wr_cat):
    t, d = x2.shape
    dc = yc.shape[1]
    tm = TM_PROJ
    tb = tm * OUTPROJ_SUBTILES
    row = lambda i: (i, 0)
    const = lambda i: (0, 0)
    return pl.pallas_call(
        functools.partial(_outproj_router_kernel, tm=tm, dc=dc),
        grid=(t // tb,),
        in_specs=[
            pl.BlockSpec((tb, d), row),
            pl.BlockSpec((tb, dc), row),
            pl.BlockSpec((tb, dc), row),
            pl.BlockSpec(memory_space=pl.ANY),
            pl.BlockSpec((1, d), const),
            pl.BlockSpec(wr_cat.shape, const),
        ],
        out_specs=[
            pl.BlockSpec((tb, d), row),
            pl.BlockSpec((tb, d // 2), row),
            pl.BlockSpec((8, tb), lambda i: (0, i)),
            pl.BlockSpec((N_EXPERTS, 1), const),
        ],
        out_shape=[
            jax.ShapeDtypeStruct((t, d), BF16),
            jax.ShapeDtypeStruct((t, d // 2), jnp.int32),
            jax.ShapeDtypeStruct((8, t), F32),
            jax.ShapeDtypeStruct((N_EXPERTS, 1), F32),
        ],
        scratch_shapes=[pltpu.VMEM(w_out.shape, F32), pltpu.VMEM(w_out.shape, BF16),
                        pltpu.SemaphoreType.DMA((1,))],
        compiler_params=pltpu.CompilerParams(
            dimension_semantics=("arbitrary",), vmem_limit_bytes=VMEM_LIMIT),
        name="outproj_router",
    )(x2, yc, at, w_out, g, wr_cat)


def _plan_kernel(cnt_ref, rt_ref, pos_ref, te_ref, nx_ref, sl_ref, nt_ref, *, tm):
    shift = tm.bit_length() - 1
    tiles, starts = [], []
    total = jnp.int32(0)
    for e in range(N_EXPERTS):
        n = lax.shift_right_logical(cnt_ref[e] + (tm - 1), shift)
        tiles.append(n)
        starts.append(total)
        total = total + n
    nt_ref[0] = total

    rt = rt_ref[...]
    ea, eb = rt[0:1, :], rt[1:2, :]
    sa = jnp.zeros_like(ea)
    sb = jnp.zeros_like(eb)
    for e in range(N_EXPERTS):
        start = (starts[e] * tm).astype(F32)
        sa = jnp.where(ea == e, start, sa)
        sb = jnp.where(eb == e, start, sb)
    pos_ref[0:1, :] = (sa + rt[2:3, :]).astype(jnp.int32)
    pos_ref[1:2, :] = (sb + rt[3:4, :]).astype(jnp.int32)

    def clear(i, c):
        te_ref[i] = 0
        nx_ref[i] = -1
        sl_ref[i] = 0
        return c

    lax.fori_loop(0, te_ref.shape[0], clear, 0)

    nxt = jnp.int32(-1)
    next_of = [None] * N_EXPERTS
    for e in reversed(range(N_EXPERTS)):
        next_of[e] = nxt
        nxt = jnp.where(tiles[e] > 0, e, nxt)
    ordinal = jnp.int32(0)
    for e in range(N_EXPERTS):
        slot = ordinal & 1

        def fill(j, c, e=e, slot=slot):
            te_ref[starts[e] + j] = e
            nx_ref[starts[e] + j] = next_of[e]
            sl_ref[starts[e] + j] = slot
            return c

        lax.fori_loop(0, tiles[e], fill, 0)
        ordinal = ordinal + (tiles[e] > 0).astype(jnp.int32)


def _plan(counts, routeT, n_tiles_max):
    t = routeT.shape[1]
    smem = pl.BlockSpec(memory_space=pltpu.SMEM)
    table = jax.ShapeDtypeStruct((n_tiles_max,), jnp.int32)
    return pl.pallas_call(
        functools.partial(_plan_kernel, tm=TM_MOE),
        grid_spec=pltpu.PrefetchScalarGridSpec(
            num_scalar_prefetch=1, grid=(1,),
            in_specs=[pl.BlockSpec(routeT.shape, lambda i, cnt: (0, 0))],
            out_specs=[pl.BlockSpec((2, t), lambda i, cnt: (0, 0)), smem, smem, smem, smem],
        ),
        out_shape=[jax.ShapeDtypeStruct((2, t), jnp.int32), table, table, table,
                   jax.ShapeDtypeStruct((1,), jnp.int32)],
        name="routing_plan",
    )(counts, routeT)


def _moe_kernel(te_ref, nt_ref, nx_ref, sl_ref, x_ref, wg_hbm, wu_hbm, wd_hbm, y_ref,
                wg_st, wu_st, wd_st, wgb_ref, wub_ref, wdb_ref, sem, *, tm, tiles_per_step):
    last = nt_ref[0] - 1

    def weight_copies(e, s):
        return (pltpu.make_async_copy(wg_hbm.at[e], wg_st.at[s], sem.at[s, 0]),
                pltpu.make_async_copy(wu_hbm.at[e], wu_st.at[s], sem.at[s, 1]),
                pltpu.make_async_copy(wd_hbm.at[e], wd_st.at[s], sem.at[s, 2]))

    for u in range(tiles_per_step):
        i = pl.program_id(0) * tiles_per_step + u
        rows = slice(u * tm, (u + 1) * tm)
        ic = jnp.minimum(i, last)
        expert = te_ref[ic]
        slot = sl_ref[ic]
        first_of_expert = (i == 0) | ((i <= last) & (expert != te_ref[jnp.maximum(ic - 1, 0)]))

        if u == 0:
            @pl.when(i == 0)
            def _():
                for c in weight_copies(expert, slot):
                    c.start()

        def expert_mlp(rows, new_weights):
            x_l, x_r = _unpack_rows(x_ref[rows, :])
            x_l = x_l.astype(BF16)
            x_r = x_r.astype(BF16)
            half = x_l.shape[1]
            if new_weights:
                wgb_ref[...] = wg_st[slot].astype(BF16)
            hg = _dot(x_l, wgb_ref[0:half, :]) + _dot(x_r, wgb_ref[half:2 * half, :])
            if new_weights:
                wub_ref[...] = wu_st[slot].astype(BF16)
            hu = _dot(x_l, wub_ref[0:half, :]) + _dot(x_r, wub_ref[half:2 * half, :])
            if new_weights:
                wdb_ref[...] = wd_st[slot].astype(BF16)
            act = hg * (1.0 / (1.0 + jnp.exp(-hg))) * hu
            y_ref[rows, :] = _pack_rows(_dot(act.astype(BF16), wdb_ref[...]))

        @pl.when(first_of_expert)
        def _():
            for c in weight_copies(expert, slot):
                c.wait()
            nxt = nx_ref[ic]

            @pl.when(nxt >= 0)
            def _():
                for c in weight_copies(nxt, 1 - slot):
                    c.start()

            expert_mlp(rows, True)

        @pl.when((i <= last) & jnp.logical_not(first_of_expert))
        def _():
            expert_mlp(rows, False)


def _moe(tile_expert, n_tiles, next_expert, stage_slot, xs, wg, wu, wd):
    p, dp = xs.shape
    d, f = wg.shape[1], wg.shape[2]
    assert dp * 2 == d
    tm, tps = TM_MOE, MOE_TILES_PER_STEP
    tb = tm * tps
    row = lambda s, te, nt, nx, sl: (jnp.minimum(s, (nt[0] - 1) // tps), 0)
    hbm = pl.BlockSpec(memory_space=pl.ANY)
    grid_spec = pltpu.PrefetchScalarGridSpec(
        num_scalar_prefetch=4,
        grid=(p // tb,),
        in_specs=[pl.BlockSpec((tb, dp), row), hbm, hbm, hbm],
        out_specs=pl.BlockSpec((tb, dp), row),
        scratch_shapes=[pltpu.VMEM((2, d, f), F32), pltpu.VMEM((2, d, f), F32), pltpu.VMEM((2, f, d), F32),
                        pltpu.VMEM((d, f), BF16), pltpu.VMEM((d, f), BF16), pltpu.VMEM((f, d), BF16),
                        pltpu.SemaphoreType.DMA((2, 3))],
    )
    return pl.pallas_call(
        functools.partial(_moe_kernel, tm=tm, tiles_per_step=tps),
        grid_spec=grid_spec,
        out_shape=jax.ShapeDtypeStruct((p, dp), jnp.int32),
        compiler_params=pltpu.CompilerParams(
            dimension_semantics=("arbitrary",), vmem_limit_bytes=VMEM_LIMIT),
        name="moe_experts",
    )(tile_expert, n_tiles, next_expert, stage_slot, xs, wg, wu, wd)


def _sc_mesh():
    return plsc.VectorSubcoreMesh(core_axis_name="c", subcore_axis_name="s",
                                  num_cores=SC_CORES, num_subcores=SC_SUBCORES)


def _sc_dispatch(rows, pos, n_out):
    t, d = rows.shape
    win = pos.shape[1]
    assert pos.shape[0] * win == 2 * t

    @functools.partial(pl.kernel, out_type=jax.ShapeDtypeStruct((n_out, d), rows.dtype),
                       mesh=_sc_mesh(), scratch_types=[pltpu.SemaphoreType.DMA((2,))], name="sc_dispatch")
    def run(rows_hbm, pos_hbm, out_hbm, sem):
        def body(rows_vmem, pa_vmem, pb_vmem):
            first = pltpu.async_copy(rows_vmem, out_hbm.at[pa_vmem.at[0]], sem.at[0])
            second = pltpu.async_copy(rows_vmem, out_hbm.at[pb_vmem.at[0]], sem.at[1])
            first.wait()
            second.wait()

        pltpu.emit_pipeline(
            body, grid=(t // win,),
            in_specs=[pl.BlockSpec((win, d), lambda i: (i, 0)),
                      pl.BlockSpec((1, win), lambda i: (i, 0)),
                      pl.BlockSpec((1, win), lambda i: (i + t // win, 0))],
            out_specs=[],
            core_axis_name=("c", "s"),
            dimension_semantics=(pltpu.PARALLEL,),
        )(rows_hbm, pos_hbm, pos_hbm)

    return run(rows, pos)


def _sc_gather(table, idx):
    d = table.shape[1]
    n_win, win = idx.shape

    @functools.partial(pl.kernel, out_type=jax.ShapeDtypeStruct((n_win * win, d), table.dtype),
                       mesh=_sc_mesh(), scratch_types=[], name="sc_gather")
    def run(table_hbm, idx_hbm, out_hbm):
        def body(idx_vmem, out_vmem):
            pltpu.sync_copy(table_hbm.at[idx_vmem.at[0]], out_vmem)

        pltpu.emit_pipeline(
            body, grid=(n_win,),
            in_specs=[pl.BlockSpec((1, win), lambda i: (i, 0))],
            out_specs=[pl.BlockSpec((win, d), lambda i: (i, 0))],
            core_axis_name=("c", "s"),
            dimension_semantics=(pltpu.PARALLEL,),
        )(idx_hbm, out_hbm)

    return run(table, idx)


def _combine_kernel(h_hbm, y_hbm, rt_ref, o_ref, h_buf, ya_buf, yb_buf, sem, *, tm, nb):
    s = pl.program_id(0)
    depth = h_buf.shape[0]

    def copies(step, slot):
        return (pltpu.make_async_copy(h_hbm.at[pl.ds(step * tm, tm)], h_buf.at[slot], sem.at[slot, 0]),
                pltpu.make_async_copy(y_hbm.at[pl.ds(step * tm, tm)], ya_buf.at[slot], sem.at[slot, 1]),
                pltpu.make_async_copy(y_hbm.at[pl.ds((step + nb) * tm, tm)], yb_buf.at[slot], sem.at[slot, 2]))

    @pl.when(s == 0)
    def _():
        for k in range(depth - 1):
            for c in copies(k, k):
                c.start()

    ahead = s + depth - 1

    @pl.when(ahead < nb)
    def _():
        for c in copies(ahead, ahead % depth):
            c.start()

    slot = s % depth
    for c in copies(s, slot):
        c.wait()

    rt = rt_ref[...]
    r = jnp.concatenate([rt, jnp.zeros((LANES - rt.shape[0], rt.shape[1]), F32)], axis=0).T
    wa, wb = r[:, 4:5], r[:, 5:6]
    a_l, a_r = _unpack_rows(ya_buf[slot])
    b_l, b_r = _unpack_rows(yb_buf[slot])
    half = a_l.shape[1]
    h = h_buf[slot]
    o_ref[:, 0:half] = h[:, 0:half].astype(F32) + wa * a_l + wb * b_l
    o_ref[:, half:2 * half] = h[:, half:2 * half].astype(F32) + wa * a_r + wb * b_r


def _combine(hres, yg, routeT):
    t, d = hres.shape
    tm = TM_COMBINE
    nb = t // tm
    ring = COMBINE_RING
    assert nb >= ring - 1 and yg.shape == (2 * t, d // 2)
    hbm = pl.BlockSpec(memory_space=pl.ANY)
    return pl.pallas_call(
        functools.partial(_combine_kernel, tm=tm, nb=nb),
        grid=(nb,),
        in_specs=[hbm, hbm, pl.BlockSpec((routeT.shape[0], tm), lambda i: (0, i))],
        out_specs=pl.BlockSpec((tm, d), lambda i: (i, 0)),
        out_shape=jax.ShapeDtypeStruct((t, d), F32),
        scratch_shapes=[pltpu.VMEM((ring, tm, d), hres.dtype),
                        pltpu.VMEM((ring, tm, d // 2), yg.dtype), pltpu.VMEM((ring, tm, d // 2), yg.dtype),
                        pltpu.SemaphoreType.DMA((ring, 3))],
        compiler_params=pltpu.CompilerParams(
            dimension_semantics=("arbitrary",), vmem_limit_bytes=VMEM_LIMIT),
        name="combine",
    )(hres, yg, routeT)


def _lambda_init(layer_idx):
    return 0.8 - 0.6 * math.exp(-0.3 * layer_idx)


def _layer(h, l, attn_norm_g, w_in, conv_w, conv_out_g, q_norm_g, k_norm_g,
           lambda_q1, lambda_k1, lambda_q2, lambda_k2, attn_subln_g, w_out,
           ffn_norm_g, w_router_group, w_router_expert, w_exp_gate, w_exp_up, w_exp_down):
    batch, seq, d = h.shape
    t = batch * seq
    dc = conv_w.shape[-1]
    lam_init = _lambda_init(l)
    x2 = h.reshape(t, d)

    reps = dc // HEAD_DIM
    assert dc // CONV_GROUPS == HEAD_DIM
    qg = (jnp.tile(q_norm_g[l], reps) * (HEAD_DIM ** -0.5 * math.log2(math.e))).reshape(1, dc)
    kg = jnp.tile(k_norm_g[l], reps).reshape(1, dc)
    grp = jnp.arange(MXU_TILE) // HEAD_DIM
    gmat = jnp.where(grp[:, None] == grp[None, :], 1.0 / HEAD_DIM, 0.0).astype(BF16)
    yc, qT, k, vT = _inproj(x2, attn_norm_g[l].reshape(1, d), w_in[l], conv_w[l],
                          conv_out_g[l].reshape(1, dc), qg, kg, gmat, batch=batch, seq=seq)

    lam_params = jnp.stack([lambda_q1[l], lambda_k1[l], lambda_q2[l], lambda_k2[l]])
    at = _attention(lam_params, attn_subln_g[l].reshape(V_DIM, 1), qT, k, vT,
                    batch=batch, seq=seq, lam_init=lam_init)

    wr = jnp.concatenate([w_router_expert[l], w_router_group[l],
                          jnp.zeros((d, LANES - N_EXPERTS - N_GROUPS), F32)], axis=1)
    wr_hi = wr.astype(BF16)
    wr_lo = (wr - wr_hi.astype(F32)).astype(BF16)
    hres, hn2, routeT, cnt = _outproj_router(x2, yc, at, w_out[l],
                                                    ffn_norm_g[l].reshape(1, d),
                                                    jnp.concatenate([wr_hi, wr_lo], axis=1))

    tmm = TM_MOE
    n_tiles_max = (2 * t) // tmm + N_EXPERTS
    p_rows = n_tiles_max * tmm
    pos, tile_expert, next_expert, stage_slot, n_tiles = _plan(
        cnt[:, 0].astype(jnp.int32), routeT, n_tiles_max)

    posw = pos.reshape(2 * t // SC_WIN, SC_WIN)
    xs = _sc_dispatch(hn2, posw, p_rows)

    f = w_exp_gate.shape[-1]
    ys = _moe(tile_expert, n_tiles, next_expert, stage_slot, xs,
              w_exp_gate[l].reshape(N_EXPERTS, d, f),
              w_exp_up[l].reshape(N_EXPERTS, d, f),
              w_exp_down[l].reshape(N_EXPERTS, f, d))
    yg = _sc_gather(ys, posw)
    out = _combine(hres, yg, routeT)
    return out.reshape(batch, seq, d)


def kernel(x, attn_norm_g, w_in, conv_w, conv_out_g, q_norm_g, k_norm_g, lambda_q1, lambda_k1,
           lambda_q2, lambda_k2, attn_subln_g, w_out, ffn_norm_g, w_router_group, w_router_expert,
           w_exp_gate, w_exp_up, w_exp_down):
    h = x
    for l in range(attn_norm_g.shape[0]):
        h = _layer(h, l, attn_norm_g, w_in, conv_w, conv_out_g, q_norm_g, k_norm_g,
                   lambda_q1, lambda_k1, lambda_q2, lambda_k2, attn_subln_g, w_out,
                   ffn_norm_g, w_router_group, w_router_expert, w_exp_gate, w_exp_up, w_exp_down)
    return h
```

```python
import functools
import math

import jax
import jax.numpy as jnp
from jax import lax
from jax.experimental import pallas as pl
from jax.experimental.pallas import tpu as pltpu
from jax.experimental.pallas import tpu_sc as plsc

F32 = jnp.float32
BF16 = jnp.bfloat16

HEAD_DIM = 64
V_DIM = 2 * HEAD_DIM
CONV_GROUPS = 8
N_GROUPS = 4
EXPERTS_PER_GROUP = 8
N_EXPERTS = N_GROUPS * EXPERTS_PER_GROUP
EPS = 1e-6
LANES = 128
MXU_TILE = 256
ONES_ROWS = 16
VMEM_LIMIT = 48 * 1024 * 1024

TM_PROJ = 256
TM_COMBINE = 1024
COMBINE_RING = 3
OUTPROJ_SUBTILES = 4
INPROJ_SUBTILES = 2
TQ = 512
TK = 512
TM_MOE = 512
MOE_TILES_PER_STEP = 2
SC_CORES = 2
SC_SUBCORES = 16
SC_WIN = 64


def _dot(a, b):
    return jnp.dot(a, b, preferred_element_type=F32)


def _pack_rows(x):
    w = x.shape[1] // 2
    bits = lax.bitcast_convert_type(x.astype(BF16).astype(F32), jnp.uint32)
    return lax.bitcast_convert_type((bits[:, :w] >> 16) | bits[:, w:], jnp.int32)


def _unpack_rows(packed):
    bits = lax.bitcast_convert_type(packed, jnp.uint32)
    left = lax.bitcast_convert_type(bits << 16, F32)
    right = lax.bitcast_convert_type(bits & jnp.uint32(0xFFFF0000), F32)
    return left, right


def _group_mean(sq, gm):
    w = gm.shape[0]
    sq = sq.astype(BF16)
    return jnp.concatenate([_dot(sq[:, c:c + w], gm) for c in range(0, sq.shape[1], w)], axis=1)


def _inproj_kernel(x_ref, g_ref, w_hbm, cw_ref, cg_ref, qg_ref, kg_ref, gm_ref,
                   yc_ref, qT_ref, k_ref, vT_ref, carry_ref, wst_ref, w_ref, sem, *, tm, dc):
    j = pl.program_id(1)

    def weight_copy(s):
        return pltpu.make_async_copy(w_hbm.at[:, s * dc:(s + 1) * dc], wst_ref.at[s], sem.at[s])

    @pl.when((pl.program_id(0) == 0) & (j == 0))
    def _():
        for s in range(wst_ref.shape[0]):
            weight_copy(s).start()
        for s in range(wst_ref.shape[0]):
            weight_copy(s).wait()
            w_ref[:, s * dc:(s + 1) * dc] = wst_ref[s].astype(BF16)

    @pl.when(j == 0)
    def _():
        carry_ref[...] = jnp.zeros_like(carry_ref)

    for u in range(x_ref.shape[0] // tm):
        _inproj_subtile(u, x_ref, g_ref, w_ref, cw_ref, cg_ref, qg_ref, kg_ref, gm_ref,
                        yc_ref, qT_ref, k_ref, vT_ref, carry_ref, tm=tm, dc=dc)


def _inproj_subtile(u, x_ref, g_ref, w_ref, cw_ref, cg_ref, qg_ref, kg_ref, gm_ref,
                    yc_ref, qT_ref, k_ref, vT_ref, carry_ref, *, tm, dc):
    rows = slice(u * tm, (u + 1) * tm)
    x = x_ref[rows, :]
    ms = jnp.mean(x * x, axis=-1, keepdims=True)
    hn = (x * lax.rsqrt(ms + EPS) * g_ref[...]).astype(BF16)

    def proj(s):
        return _dot(hn, w_ref[:, s * dc:(s + 1) * dc])

    gm = gm_ref[...]

    c = proj(2) * proj(0)
    prev = carry_ref[...]
    r = lax.broadcasted_iota(jnp.int32, c.shape, 0)
    c1 = jnp.where(r == 0, prev[7:8, :], pltpu.roll(c, 1, 0))
    c2 = jnp.where(r == 0, prev[6:7, :], jnp.where(r == 1, prev[7:8, :], pltpu.roll(c, 2, 0)))
    carry_ref[...] = c[tm - 8:tm, :]
    cw = cw_ref[...]
    y = proj(1) * (cw[0:1, :] * c2 + cw[1:2, :] * c1 + cw[2:3, :] * c)
    yc_ref[rows, :] = (y * lax.rsqrt(_group_mean(y * y, gm) + EPS) * cg_ref[...]).astype(BF16)

    q = proj(3)
    qT_ref[:, rows] = (q * lax.rsqrt(_group_mean(q * q, gm) + EPS) * qg_ref[...]).T.astype(BF16)
    k = proj(4)
    k_ref[rows, :] = (k * lax.rsqrt(_group_mean(k * k, gm) + EPS) * kg_ref[...]).astype(BF16)
    vt = proj(5).T.astype(BF16)
    vrows = V_DIM + ONES_ROWS
    for h in range(dc // V_DIM):
        vT_ref[u, h * vrows:h * vrows + V_DIM, :] = vt[h * V_DIM:(h + 1) * V_DIM, :]
        vT_ref[u, h * vrows + V_DIM:(h + 1) * vrows, :] = jnp.ones((ONES_ROWS, tm), BF16)


def _inproj(x2, g, w_in, conv_w, conv_g, qg, kg, gmat, *, batch, seq):
    t, d = x2.shape
    dc = conv_g.shape[1]
    n_proj = w_in.shape[1] // dc
    assert w_in.shape == (d, n_proj * dc) and w_in.dtype == F32
    dv = dc // V_DIM * (V_DIM + ONES_ROWS)
    tm = TK
    tb = tm * INPROJ_SUBTILES
    nj = seq // tb
    row = lambda b, j: (b * nj + j, 0)
    const = lambda b, j: (0, 0)
    out_sds = jax.ShapeDtypeStruct((t, dc), BF16)
    return pl.pallas_call(
        functools.partial(_inproj_kernel, tm=tm, dc=dc),
        grid=(batch, nj),
        in_specs=[
            pl.BlockSpec((tb, d), row),
            pl.BlockSpec((1, d), const),
            pl.BlockSpec(memory_space=pl.ANY),
            pl.BlockSpec(conv_w.shape, const),
            pl.BlockSpec((1, dc), const),
            pl.BlockSpec((1, dc), const),
            pl.BlockSpec((1, dc), const),
            pl.BlockSpec(gmat.shape, const),
        ],
        out_specs=[
            pl.BlockSpec((tb, dc), row),
            pl.BlockSpec((dc, tb), lambda b, j: (0, b * nj + j)),
            pl.BlockSpec((tb, dc), row),
            pl.BlockSpec((INPROJ_SUBTILES, dv, tm), lambda b, j: (b * nj + j, 0, 0)),
        ],
        out_shape=[out_sds, jax.ShapeDtypeStruct((dc, t), BF16), out_sds,
                   jax.ShapeDtypeStruct((t // tm, dv, tm), BF16)],
        scratch_shapes=[pltpu.VMEM((8, dc), F32),
                        pltpu.VMEM((n_proj, d, dc), F32), pltpu.VMEM((d, n_proj * dc), BF16),
                        pltpu.SemaphoreType.DMA((n_proj,))],
        compiler_params=pltpu.CompilerParams(
            dimension_semantics=("arbitrary", "arbitrary"), vmem_limit_bytes=VMEM_LIMIT),
        name="inproj_conv_qknorm",
    )(x2, g, w_in, conv_w, conv_g, qg, kg, gmat)


def _attn_kernel(lp_ref, sg_ref, qT_ref, k_ref, vT_ref, o_ref, m_ref, acc_ref,
                 sa_ref, pb_ref, ab_ref, *, tq, tk, nq, lam_init):
    map_a = slice(0, tq)
    map_b = slice(tq, 2 * tq)

    lp = lp_ref[...]
    lam = (jnp.exp(jnp.sum(lp[0:1, :] * lp[1:2, :], axis=-1, keepdims=True))
           - jnp.exp(jnp.sum(lp[2:3, :] * lp[3:4, :], axis=-1, keepdims=True)) + lam_init)

    def stacked_queries(qi):
        qT = qT_ref[:, qi * tq:(qi + 1) * tq]
        row = lax.broadcasted_iota(jnp.int32, qT.shape, 0)
        zero = jnp.zeros_like(qT)
        return jnp.concatenate([jnp.where(row < HEAD_DIM, qT, zero),
                                jnp.where(row >= HEAD_DIM, qT, zero)], axis=1)

    def pieces(diag):
        return ((slice(0, tq // 2), tk // 2), (slice(tq // 2, tq), tk)) if diag else ((slice(0, tq), tk),)

    def shifted(cols, off):
        return slice(cols.start + off, cols.stop + off)

    def scores(j, qq, cols, n_keys):
        return _dot(k_ref[j * tk:j * tk + n_keys, :], qq[:, cols])

    def softmax(m, cols, s, q0):
        if q0 is not None:
            qrel = q0 + lax.broadcasted_iota(jnp.int32, s.shape, 1)
            krel = lax.broadcasted_iota(jnp.int32, s.shape, 0)
            s = jnp.where(krel <= qrel, s, -jnp.inf)
        sb = s.astype(BF16)
        m_old = m[:, cols]
        m_new = jnp.maximum(m_old, jnp.max(sb, axis=0, keepdims=True).astype(F32))
        alpha = jnp.exp2(m_old - m_new)
        p = jnp.exp2(sb - m_new.astype(BF16))
        m[:, cols] = m_new
        return alpha, p

    def accumulate(acc, cols, alpha, vb, p):
        acc[:, cols] = alpha * acc[:, cols] + _dot(vb, p)

    def prefetch_map_a(sa, j, qq, diag):
        for cols, n_keys in pieces(diag):
            sa[0:n_keys, cols] = scores(j, qq, cols, n_keys)

    stacked = {0: stacked_queries(0)}
    m_ref[0] = jnp.full(m_ref.shape[1:], -jnp.inf, F32)
    acc_ref[0] = jnp.zeros(acc_ref.shape[1:], F32)
    prefetch_map_a(sa_ref.at[0], 0, stacked[0], True)

    for qi in range(nq):
        par = qi % 2
        m, acc, sa, qq = m_ref.at[par], acc_ref.at[par], sa_ref.at[par], stacked[qi]
        for j in range(qi + 1):
            last = j == qi
            if j > 0:
                accumulate(acc, map_b, ab_ref[...], vT_ref[j - 1], pb_ref[...])
            s_b = [scores(j, qq, shifted(cols, tq), n_keys) for cols, n_keys in pieces(last)]
            for cols, n_keys in pieces(last):
                alpha_a, p_a = softmax(m, cols, sa[0:n_keys, cols], cols.start if last else None)
                accumulate(acc, cols, alpha_a, vT_ref[j, :, 0:n_keys], p_a)
            if not last:
                prefetch_map_a(sa, j + 1, qq, j + 1 == qi)
            elif qi + 1 < nq:
                stacked[qi + 1] = stacked_queries(qi + 1)
                m_ref[1 - par] = jnp.full(m_ref.shape[1:], -jnp.inf, F32)
                acc_ref[1 - par] = jnp.zeros(acc_ref.shape[1:], F32)
                prefetch_map_a(sa_ref.at[1 - par], 0, stacked[qi + 1], False)
            for (cols, n_keys), s in zip(pieces(last), s_b):
                alpha_b, p_b = softmax(m, shifted(cols, tq), s, cols.start if last else None)
                if last:
                    accumulate(acc, shifted(cols, tq), alpha_b, vT_ref[j, :, 0:n_keys], p_b)
                else:
                    ab_ref[...] = alpha_b
                    pb_ref[...] = p_b

        o = acc[0:V_DIM, :] / acc[V_DIM:V_DIM + 1, :]
        d = o[:, map_a] - lam * o[:, map_b]
        ms = jnp.mean(d * d, axis=0, keepdims=True)
        o_ref[qi * tq:(qi + 1) * tq, :] = (
            d * lax.rsqrt(ms + EPS) * sg_ref[...] * (1.0 - lam_init)).T.astype(BF16)
        del stacked[qi]


def _attention(lam_params, subln_g, qT, k, vT, *, batch, seq, lam_init):
    dq, t = qT.shape
    n_heads = dq // V_DIM
    tq, tk = TQ, TK
    assert tq == tk and vT.shape[2] == tk
    nq = seq // tq
    nk = seq // tk
    vrows = V_DIM + ONES_ROWS
    const = lambda b, h: (0, 0)
    return pl.pallas_call(
        functools.partial(_attn_kernel, tq=tq, tk=tk, nq=nq, lam_init=lam_init),
        grid=(batch, n_heads),
        in_specs=[
            pl.BlockSpec(lam_params.shape, const),
            pl.BlockSpec((V_DIM, 1), const),
            pl.BlockSpec((V_DIM, seq), lambda b, h: (h, b)),
            pl.BlockSpec((seq, V_DIM), lambda b, h: (b, h)),
            pl.BlockSpec((nk, vrows, tk), lambda b, h: (b, h, 0)),
        ],
        out_specs=pl.BlockSpec((seq, V_DIM), lambda b, h: (b, h)),
        out_shape=jax.ShapeDtypeStruct((t, dq), BF16),
        scratch_shapes=[pltpu.VMEM((2, 1, 2 * tq), F32),
                        pltpu.VMEM((2, vrows, 2 * tq), F32),
                        pltpu.VMEM((2, tk, tq), F32), pltpu.VMEM((tk, tq), BF16), pltpu.VMEM((1, tq), F32)],
        compiler_params=pltpu.CompilerParams(
            dimension_semantics=("arbitrary", "arbitrary"), vmem_limit_bytes=VMEM_LIMIT),
        name="diff_attention",
    )(lam_params, subln_g, qT, k, vT)


def _outproj_router_kernel(x_ref, yc_ref, at_ref, wo_hbm, g_ref, wr_ref,
                           h_ref, hn_ref, routeT_ref, cnt_ref, wst_ref, wo_ref, sem, *, tm, dc):
    i = pl.program_id(0)

    @pl.when(i == 0)
    def _():
        cnt_ref[...] = jnp.zeros_like(cnt_ref)
        weight_copy = pltpu.make_async_copy(wo_hbm, wst_ref, sem.at[0])
        weight_copy.start()
        weight_copy.wait()
        wo_ref[...] = wst_ref[...].astype(BF16)

    for r0 in range(0, x_ref.shape[0], tm):
        _outproj_router_subtile(slice(r0, r0 + tm), x_ref, yc_ref, at_ref, wo_ref, g_ref, wr_ref,
                                h_ref, hn_ref, routeT_ref, cnt_ref, tm=tm, dc=dc)


def _outproj_router_subtile(rows, x_ref, yc_ref, at_ref, wo_ref, g_ref, wr_ref,
                            h_ref, hn_ref, routeT_ref, cnt_ref, *, tm, dc):
    h = x_ref[rows, :] + _dot(yc_ref[rows, :], wo_ref[0:dc, :]) + _dot(at_ref[rows, :], wo_ref[dc:2 * dc, :])
    h_ref[rows, :] = h.astype(BF16)
    ms = jnp.mean(h * h, axis=-1, keepdims=True)
    hn = h * lax.rsqrt(ms + EPS) * g_ref[...]
    hi = hn.astype(BF16)
    hn_ref[rows, :] = _pack_rows(hn)
    lo = (hn - hi.astype(F32)).astype(BF16)
    prod = _dot(hi, wr_ref[...])
    logits = prod[:, 0:LANES] + prod[:, LANES:2 * LANES] + _dot(lo, wr_ref[:, 0:LANES])

    lt = logits.T
    neg = -jnp.inf
    grow = lax.broadcasted_iota(jnp.int32, (8, tm), 0).astype(F32)
    gl = jnp.where(grow < N_GROUPS, lt[N_EXPERTS:N_EXPERTS + 8, :], neg)
    gmax = jnp.max(gl, axis=0, keepdims=True)
    g_gate = 1.0 / jnp.sum(jnp.exp(gl - gmax), axis=0, keepdims=True)
    g_idx = jnp.min(jnp.where(gl == gmax, grow, 1e9), axis=0, keepdims=True)
    erow = lax.broadcasted_iota(jnp.int32, (N_EXPERTS, tm), 0).astype(F32)
    e_lo = g_idx * EXPERTS_PER_GROUP
    el = jnp.where((erow >= e_lo) & (erow < e_lo + EXPERTS_PER_GROUP), lt[0:N_EXPERTS, :], neg)
    v1 = jnp.max(el, axis=0, keepdims=True)
    i1 = jnp.min(jnp.where(el == v1, erow, 1e9), axis=0, keepdims=True)
    el2 = jnp.where(erow == i1, neg, el)
    v2 = jnp.max(el2, axis=0, keepdims=True)
    i2 = jnp.min(jnp.where(el2 == v2, erow, 1e9), axis=0, keepdims=True)
    tt = jnp.exp(v2 - v1)
    w1 = g_gate / (1.0 + tt)
    w2 = g_gate * tt / (1.0 + tt)

    sel1 = erow == i1
    sel2 = erow == i2
    oh = jnp.where(sel1 | sel2, 1.0, 0.0)
    ss = lax.broadcasted_iota(jnp.int32, (tm, tm), 0)
    tt_i = lax.broadcasted_iota(jnp.int32, (tm, tm), 1)
    earlier = jnp.where(ss < tt_i, 1.0, 0.0).astype(BF16)
    ranks = _dot(oh.astype(BF16), earlier) + cnt_ref[...]
    r1 = jnp.sum(jnp.where(sel1, ranks, 0.0), axis=0, keepdims=True)
    r2 = jnp.sum(jnp.where(sel2, ranks, 0.0), axis=0, keepdims=True)
    cnt_ref[...] = cnt_ref[...] + jnp.sum(oh, axis=1, keepdims=True)

    routeT = jnp.concatenate([i1, i2, r1, r2, w1, w2, jnp.zeros((2, tm), F32)], axis=0)
    routeT_ref[:, rows] = routeT


def _outproj_router(x2, yc, at, w_out, g, wr_cat):
    t, d = x2.shape
    dc = yc.shape[1]
    tm = TM_PROJ
    tb = tm * OUTPROJ_SUBTILES
    row = lambda i: (i, 0)
    const = lambda i: (0, 0)
    return pl.pallas_call(
        functools.partial(_outproj_router_kernel, tm=tm, dc=dc),
        grid=(t // tb,),
        in_specs=[
            pl.BlockSpec((tb, d), row),
            pl.BlockSpec((tb, dc), row),
            pl.BlockSpec((tb, dc), row),
            pl.BlockSpec(memory_space=pl.ANY),
            pl.BlockSpec((1, d), const),
            pl.BlockSpec(wr_cat.shape, const),
        ],
        out_specs=[
            pl.BlockSpec((tb, d), row),
            pl.BlockSpec((tb, d // 2), row),
            pl.BlockSpec((8, tb), lambda i: (0, i)),
            pl.BlockSpec((N_EXPERTS, 1), const),
        ],
        out_shape=[
            jax.ShapeDtypeStruct((t, d), BF16),
            jax.ShapeDtypeStruct((t, d // 2), jnp.int32),
            jax.ShapeDtypeStruct((8, t), F32),
            jax.ShapeDtypeStruct((N_EXPERTS, 1), F32),
        ],
        scratch_shapes=[pltpu.VMEM(w_out.shape, F32), pltpu.VMEM(w_out.shape, BF16),
                        pltpu.SemaphoreType.DMA((1,))],
        compiler_params=pltpu.CompilerParams(
            dimension_semantics=("arbitrary",), vmem_limit_bytes=VMEM_LIMIT),
        name="outproj_router",
    )(x2, yc, at, w_out, g, wr_cat)


def _plan_kernel(cnt_ref, rt_ref, pos_ref, te_ref, nx_ref, sl_ref, nt_ref, *, tm):
    shift = tm.bit_length() - 1
    tiles, starts = [], []
    total = jnp.int32(0)
    for e in range(N_EXPERTS):
        n = lax.shift_right_logical(cnt_ref[e] + (tm - 1), shift)
        tiles.append(n)
        starts.append(total)
        total = total + n
    nt_ref[0] = total

    rt = rt_ref[...]
    ea, eb = rt[0:1, :], rt[1:2, :]
    sa = jnp.zeros_like(ea)
    sb = jnp.zeros_like(eb)
    for e in range(N_EXPERTS):
        start = (starts[e] * tm).astype(F32)
        sa = jnp.where(ea == e, start, sa)
        sb = jnp.where(eb == e, start, sb)
    pos_ref[0:1, :] = (sa + rt[2:3, :]).astype(jnp.int32)
    pos_ref[1:2, :] = (sb + rt[3:4, :]).astype(jnp.int32)

    def clear(i, c):
        te_ref[i] = 0
        nx_ref[i] = -1
        sl_ref[i] = 0
        return c

    lax.fori_loop(0, te_ref.shape[0], clear, 0)

    nxt = jnp.int32(-1)
    next_of = [None] * N_EXPERTS
    for e in reversed(range(N_EXPERTS)):
        next_of[e] = nxt
        nxt = jnp.where(tiles[e] > 0, e, nxt)
    ordinal = jnp.int32(0)
    for e in range(N_EXPERTS):
        slot = ordinal & 1

        def fill(j, c, e=e, slot=slot):
            te_ref[starts[e] + j] = e
            nx_ref[starts[e] + j] = next_of[e]
            sl_ref[starts[e] + j] = slot
            return c

        lax.fori_loop(0, tiles[e], fill, 0)
        ordinal = ordinal + (tiles[e] > 0).astype(jnp.int32)


def _plan(counts, routeT, n_tiles_max):
    t = routeT.shape[1]
    smem = pl.BlockSpec(memory_space=pltpu.SMEM)
    table = jax.ShapeDtypeStruct((n_tiles_max,), jnp.int32)
    return pl.pallas_call(
        functools.partial(_plan_kernel, tm=TM_MOE),
        grid_spec=pltpu.PrefetchScalarGridSpec(
            num_scalar_prefetch=1, grid=(1,),
            in_specs=[pl.BlockSpec(routeT.shape, lambda i, cnt: (0, 0))],
            out_specs=[pl.BlockSpec((2, t), lambda i, cnt: (0, 0)), smem, smem, smem, smem],
        ),
        out_shape=[jax.ShapeDtypeStruct((2, t), jnp.int32), table, table, table,
                   jax.ShapeDtypeStruct((1,), jnp.int32)],
        name="routing_plan",
    )(counts, routeT)


def _moe_kernel(te_ref, nt_ref, nx_ref, sl_ref, x_ref, wg_hbm, wu_hbm, wd_hbm, y_ref,
                wg_st, wu_st, wd_st, wgb_ref, wub_ref, wdb_ref, sem, *, tm, tiles_per_step):
    last = nt_ref[0] - 1

    def weight_copies(e, s):
        return (pltpu.make_async_copy(wg_hbm.at[e], wg_st.at[s], sem.at[s, 0]),
                pltpu.make_async_copy(wu_hbm.at[e], wu_st.at[s], sem.at[s, 1]),
                pltpu.make_async_copy(wd_hbm.at[e], wd_st.at[s], sem.at[s, 2]))

    for u in range(tiles_per_step):
        i = pl.program_id(0) * tiles_per_step + u
        rows = slice(u * tm, (u + 1) * tm)
        ic = jnp.minimum(i, last)
        expert = te_ref[ic]
        slot = sl_ref[ic]
        first_of_expert = (i == 0) | ((i <= last) & (expert != te_ref[jnp.maximum(ic - 1, 0)]))

        if u == 0:
            @pl.when(i == 0)
            def _():
                for c in weight_copies(expert, slot):
                    c.start()

        def expert_mlp(rows, new_weights):
            x_l, x_r = _unpack_rows(x_ref[rows, :])
            x_l = x_l.astype(BF16)
            x_r = x_r.astype(BF16)
            half = x_l.shape[1]
            if new_weights:
                wgb_ref[...] = wg_st[slot].astype(BF16)
            hg = _dot(x_l, wgb_ref[0:half, :]) + _dot(x_r, wgb_ref[half:2 * half, :])
            if new_weights:
                wub_ref[...] = wu_st[slot].astype(BF16)
            hu = _dot(x_l, wub_ref[0:half, :]) + _dot(x_r, wub_ref[half:2 * half, :])
            if new_weights:
                wdb_ref[...] = wd_st[slot].astype(BF16)
            act = hg * (1.0 / (1.0 + jnp.exp(-hg))) * hu
            y_ref[rows, :] = _pack_rows(_dot(act.astype(BF16), wdb_ref[...]))

        @pl.when(first_of_expert)
        def _():
            for c in weight_copies(expert, slot):
                c.wait()
            nxt = nx_ref[ic]

            @pl.when(nxt >= 0)
            def _():
                for c in weight_copies(nxt, 1 - slot):
                    c.start()

            expert_mlp(rows, True)

        @pl.when((i <= last) & jnp.logical_not(first_of_expert))
        def _():
            expert_mlp(rows, False)


def _moe(tile_expert, n_tiles, next_expert, stage_slot, xs, wg, wu, wd):
    p, dp = xs.shape
    d, f = wg.shape[1], wg.shape[2]
    assert dp * 2 == d
    tm, tps = TM_MOE, MOE_TILES_PER_STEP
    tb = tm * tps
    row = lambda s, te, nt, nx, sl: (jnp.minimum(s, (nt[0] - 1) // tps), 0)
    hbm = pl.BlockSpec(memory_space=pl.ANY)
    grid_spec = pltpu.PrefetchScalarGridSpec(
        num_scalar_prefetch=4,
        grid=(p // tb,),
        in_specs=[pl.BlockSpec((tb, dp), row), hbm, hbm, hbm],
        out_specs=pl.BlockSpec((tb, dp), row),
        scratch_shapes=[pltpu.VMEM((2, d, f), F32), pltpu.VMEM((2, d, f), F32), pltpu.VMEM((2, f, d), F32),
                        pltpu.VMEM((d, f), BF16), pltpu.VMEM((d, f), BF16), pltpu.VMEM((f, d), BF16),
                        pltpu.SemaphoreType.DMA((2, 3))],
    )
    return pl.pallas_call(
        functools.partial(_moe_kernel, tm=tm, tiles_per_step=tps),
        grid_spec=grid_spec,
        out_shape=jax.ShapeDtypeStruct((p, dp), jnp.int32),
        compiler_params=pltpu.CompilerParams(
            dimension_semantics=("arbitrary",), vmem_limit_bytes=VMEM_LIMIT),
        name="moe_experts",
    )(tile_expert, n_tiles, next_expert, stage_slot, xs, wg, wu, wd)


def _sc_mesh():
    return plsc.VectorSubcoreMesh(core_axis_name="c", subcore_axis_name="s",
                                  num_cores=SC_CORES, num_subcores=SC_SUBCORES)


def _sc_dispatch(rows, pos, n_out):
    t, d = rows.shape
    win = pos.shape[1]
    assert pos.shape[0] * win == 2 * t

    @functools.partial(pl.kernel, out_type=jax.ShapeDtypeStruct((n_out, d), rows.dtype),
                       mesh=_sc_mesh(), scratch_types=[], name="sc_dispatch")
    def run(rows_hbm, pos_hbm, out_hbm):
        def body(rows_vmem, pa_vmem, pb_vmem):
            pltpu.sync_copy(rows_vmem, out_hbm.at[pa_vmem.at[0]])
            pltpu.sync_copy(rows_vmem, out_hbm.at[pb_vmem.at[0]])

        pltpu.emit_pipeline(
            body, grid=(t // win,),
            in_specs=[pl.BlockSpec((win, d), lambda i: (i, 0)),
                      pl.BlockSpec((1, win), lambda i: (i, 0)),
                      pl.BlockSpec((1, win), lambda i: (i + t // win, 0))],
            out_specs=[],
            core_axis_name=("c", "s"),
            dimension_semantics=(pltpu.PARALLEL,),
        )(rows_hbm, pos_hbm, pos_hbm)

    return run(rows, pos)


def _sc_gather(table, idx):
    d = table.shape[1]
    n_win, win = idx.shape

    @functools.partial(pl.kernel, out_type=jax.ShapeDtypeStruct((n_win * win, d), table.dtype),
                       mesh=_sc_mesh(), scratch_types=[], name="sc_gather")
    def run(table_hbm, idx_hbm, out_hbm):
        def body(idx_vmem, out_vmem):
            pltpu.sync_copy(table_hbm.at[idx_vmem.at[0]], out_vmem)

        pltpu.emit_pipeline(
            body, grid=(n_win,),
            in_specs=[pl.BlockSpec((1, win), lambda i: (i, 0))],
            out_specs=[pl.BlockSpec((win, d), lambda i: (i, 0))],
            core_axis_name=("c", "s"),
            dimension_semantics=(pltpu.PARALLEL,),
        )(idx_hbm, out_hbm)

    return run(table, idx)


def _combine_kernel(h_hbm, y_hbm, rt_ref, o_ref, h_buf, ya_buf, yb_buf, sem, *, tm, nb):
    s = pl.program_id(0)
    depth = h_buf.shape[0]

    def copies(step, slot):
        return (pltpu.make_async_copy(h_hbm.at[pl.ds(step * tm, tm)], h_buf.at[slot], sem.at[slot, 0]),
                pltpu.make_async_copy(y_hbm.at[pl.ds(step * tm, tm)], ya_buf.at[slot], sem.at[slot, 1]),
                pltpu.make_async_copy(y_hbm.at[pl.ds((step + nb) * tm, tm)], yb_buf.at[slot], sem.at[slot, 2]))

    @pl.when(s == 0)
    def _():
        for k in range(depth - 1):
            for n, c in enumerate(copies(k, k)):
                c.start(priority=n % 2)

    ahead = s + depth - 1

    @pl.when(ahead < nb)
    def _():
        for n, c in enumerate(copies(ahead, ahead % depth)):
            c.start(priority=n % 2)

    slot = s % depth
    for c in copies(s, slot):
        c.wait()

    rt = rt_ref[...]
    r = jnp.concatenate([rt, jnp.zeros((LANES - rt.shape[0], rt.shape[1]), F32)], axis=0).T
    wa, wb = r[:, 4:5], r[:, 5:6]
    a_l, a_r = _unpack_rows(ya_buf[slot])
    b_l, b_r = _unpack_rows(yb_buf[slot])
    half = a_l.shape[1]
    h = h_buf[slot]
    o_ref[:, 0:half] = h[:, 0:half].astype(F32) + wa * a_l + wb * b_l
    o_ref[:, half:2 * half] = h[:, half:2 * half].astype(F32) + wa * a_r + wb * b_r


def _combine(hres, yg, routeT):
    t, d = hres.shape
    tm = TM_COMBINE
    nb = t // tm
    ring = COMBINE_RING
    assert nb >= ring - 1 and yg.shape == (2 * t, d // 2)
    hbm = pl.BlockSpec(memory_space=pl.ANY)
    return pl.pallas_call(
        functools.partial(_combine_kernel, tm=tm, nb=nb),
        grid=(nb,),
        in_specs=[hbm, hbm, pl.BlockSpec((routeT.shape[0], tm), lambda i: (0, i))],
        out_specs=pl.BlockSpec((tm, d), lambda i: (i, 0)),
        out_shape=jax.ShapeDtypeStruct((t, d), F32),
        scratch_shapes=[pltpu.VMEM((ring, tm, d), hres.dtype),
                        pltpu.VMEM((ring, tm, d // 2), yg.dtype), pltpu.VMEM((ring, tm, d // 2), yg.dtype),
                        pltpu.SemaphoreType.DMA((ring, 3))],
        compiler_params=pltpu.CompilerParams(
            dimension_semantics=("arbitrary",), vmem_limit_bytes=VMEM_LIMIT),
        name="combine",
    )(hres, yg, routeT)


def _lambda_init(layer_idx):
    return 0.8 - 0.6 * math.exp(-0.3 * layer_idx)


def _layer(h, l, attn_norm_g, w_in, conv_w, conv_out_g, q_norm_g, k_norm_g,
           lambda_q1, lambda_k1, lambda_q2, lambda_k2, attn_subln_g, w_out,
           ffn_norm_g, w_router_group, w_router_expert, w_exp_gate, w_exp_up, w_exp_down):
    batch, seq, d = h.shape
    t = batch * seq
    dc = conv_w.shape[-1]
    lam_init = _lambda_init(l)
    x2 = h.reshape(t, d)

    reps = dc // HEAD_DIM
    assert dc // CONV_GROUPS == HEAD_DIM
    qg = (jnp.tile(q_norm_g[l], reps) * (HEAD_DIM ** -0.5 * math.log2(math.e))).reshape(1, dc)
    kg = jnp.tile(k_norm_g[l], reps).reshape(1, dc)
    grp = jnp.arange(MXU_TILE) // HEAD_DIM
    gmat = jnp.where(grp[:, None] == grp[None, :], 1.0 / HEAD_DIM, 0.0).astype(BF16)
    yc, qT, k, vT = _inproj(x2, attn_norm_g[l].reshape(1, d), w_in[l], conv_w[l],
                          conv_out_g[l].reshape(1, dc), qg, kg, gmat, batch=batch, seq=seq)

    lam_params = jnp.stack([lambda_q1[l], lambda_k1[l], lambda_q2[l], lambda_k2[l]])
    at = _attention(lam_params, attn_subln_g[l].reshape(V_DIM, 1), qT, k, vT,
                    batch=batch, seq=seq, lam_init=lam_init)

    wr = jnp.concatenate([w_router_expert[l], w_router_group[l],
                          jnp.zeros((d, LANES - N_EXPERTS - N_GROUPS), F32)], axis=1)
    wr_hi = wr.astype(BF16)
    wr_lo = (wr - wr_hi.astype(F32)).astype(BF16)
    hres, hn2, routeT, cnt = _outproj_router(x2, yc, at, w_out[l],
                                                    ffn_norm_g[l].reshape(1, d),
                                                    jnp.concatenate([wr_hi, wr_lo], axis=1))

    tmm = TM_MOE
    n_tiles_max = (2 * t) // tmm + N_EXPERTS
    p_rows = n_tiles_max * tmm
    pos, tile_expert, next_expert, stage_slot, n_tiles = _plan(
        cnt[:, 0].astype(jnp.int32), routeT, n_tiles_max)

    posw = pos.reshape(2 * t // SC_WIN, SC_WIN)
    xs = _sc_dispatch(hn2, posw, p_rows)

    f = w_exp_gate.shape[-1]
    ys = _moe(tile_expert, n_tiles, next_expert, stage_slot, xs,
              w_exp_gate[l].reshape(N_EXPERTS, d, f),
              w_exp_up[l].reshape(N_EXPERTS, d, f),
              w_exp_down[l].reshape(N_EXPERTS, f, d))
    yg = _sc_gather(ys, posw)
    out = _combine(hres, yg, routeT)
    return out.reshape(batch, seq, d)


def kernel(x, attn_norm_g, w_in, conv_w, conv_out_g, q_norm_g, k_norm_g, lambda_q1, lambda_k1,
           lambda_q2, lambda_k2, attn_subln_g, w_out, ffn_norm_g, w_router_group, w_router_expert,
           w_exp_gate, w_exp_up, w_exp_down):
    h = x
    for l in range(attn_norm_g.shape[0]):
        h = _layer(h, l, attn_norm_g, w_in, conv_w, conv_out_g, q_norm_g, k_norm_g,
                   lambda_q1, lambda_k1, lambda_q2, lambda_k2, attn_subln_g, w_out,
                   ffn_norm_g, w_router_group, w_router_expert, w_exp_gate, w_exp_up, w_exp_down)
    return h
```
